```python
import jax, jax.numpy as jnp
from jax import lax
import numpy as np

D_MODEL = 1024
BATCH = 16
SEQ = 256
DEPTH = 2
DEC_BATCH = 8
DEC_SEQ = 1024
PAST_LEN = 512

GRID_W = 64
N_HEADS_A = 8
HEAD_DIM_A = 64
D_A = N_HEADS_A * HEAD_DIM_A
WIN_ROWS = 8
WIN_COLS = 16
Q_COL_BLOCK = 16
K_COL_BLOCK = 2 * WIN_COLS
CTX_Q_BLOCK = 128
D_CONV = 512
CONV_WIDTH = 31
N_HEADS_M = 4
HEAD_DIM_M = 128
D_M = N_HEADS_M * HEAD_DIM_M
N_GATE_M = 4 * N_HEADS_M
CHUNK = 64
N_EXPERTS = 32
TOP_K = 4
D_EXPERT = 1024
SWIGLU_ALPHA = 1.702
SWIGLU_LIMIT = 7.0
MOE_BLOCK = 128
N_IN = 3 * D_A + 2 * D_CONV + 4 * D_M + N_GATE_M + 3 * D_MODEL
GATE_OFF = 3 * D_A + 2 * D_CONV + 4 * D_M
RMS_EPS = 1e-6
LN_EPS = 1e-5

kernel_name = 'hybrid_natten_conformer_mlstm_moe_step'


def _rmsnorm(x, g):
    xf = x.astype(jnp.float32)
    y = xf * lax.rsqrt(jnp.mean(xf * xf, axis=-1, keepdims=True) + RMS_EPS)
    return y.astype(x.dtype) * g


def _layernorm(x, g, b):
    xf = x.astype(jnp.float32)
    mu = jnp.mean(xf, axis=-1, keepdims=True)
    var = jnp.mean(jnp.square(xf - mu), axis=-1, keepdims=True)
    return ((xf - mu) * lax.rsqrt(var + LN_EPS)).astype(x.dtype) * g + b


def _split_in(z):
    sizes = [D_A, D_A, D_A, D_CONV, D_CONV, D_M, D_M, D_M, D_M, N_GATE_M, D_MODEL, D_MODEL, D_MODEL]
    return jnp.split(z, [int(s) for s in np.cumsum(sizes)[:-1]], axis=-1)


def _context_attention(q, k, v):
    B, L, H, Dh = q.shape
    nb = L // CTX_Q_BLOCK
    scale = Dh ** -0.5
    qb = q.reshape(B, nb, CTX_Q_BLOCK, H, Dh).transpose(1, 0, 2, 3, 4)

    def block(qi):
        s = jnp.einsum('bqhd,bkhd->bhqk', qi, k).astype(jnp.float32) * scale
        p = jax.nn.softmax(s, axis=-1).astype(q.dtype)
        return jnp.einsum('bhqk,bkhd->bqhd', p, v)

    o = lax.map(block, qb)
    return o.transpose(1, 0, 2, 3, 4).reshape(B, L, H * Dh)


def _neighbourhood_attention(q, k, v, k_ctx, v_ctx, rpb):
    B, L, H, Dh = q.shape
    rows = L // GRID_W
    kr = min(WIN_ROWS, rows)
    ncb = GRID_W // Q_COL_BLOCK
    scale = Dh ** -0.5
    qcol = np.arange(GRID_W).reshape(ncb, Q_COL_BLOCK)
    kcs = np.clip(np.arange(ncb) * Q_COL_BLOCK - WIN_COLS // 2, 0, GRID_W - K_COL_BLOCK)
    kcol = (kcs[:, None] + np.arange(K_COL_BLOCK)[None, :]).astype(np.int32)
    cs = np.clip(qcol - WIN_COLS // 2, 0, GRID_W - WIN_COLS)
    col_ok = (kcol[:, None, :] >= cs[:, :, None]) & (kcol[:, None, :] < cs[:, :, None] + WIN_COLS)
    dc_idx = np.clip(kcol[:, None, :] - qcol[:, :, None] + WIN_COLS - 1, 0, 2 * WIN_COLS - 2).astype(np.int32)
    mask = np.broadcast_to(col_ok[:, :, None, :], (ncb, Q_COL_BLOCK, kr, K_COL_BLOCK)).reshape(ncb, Q_COL_BLOCK, kr * K_COL_BLOCK)
    n_loc = kr * K_COL_BLOCK
    k_grid = k.reshape(B, rows, GRID_W, H, Dh)
    v_grid = v.reshape(B, rows, GRID_W, H, Dh)
    q_rows = q.reshape(B, rows, ncb, Q_COL_BLOCK, H, Dh).transpose(1, 0, 2, 3, 4, 5)

    def one_row(args):
        r, q_r = args
        rs = jnp.clip(r - kr // 2, 0, rows - kr)

        def band(t):
            t_r = lax.dynamic_slice_in_dim(t, rs, kr, axis=1)
            t_b = t_r[:, :, kcol]
            return t_b.transpose(0, 2, 1, 3, 4, 5).reshape(B, ncb, n_loc, H, Dh)

        k_b, v_b = band(k_grid), band(v_grid)
        dr_idx = rs + jnp.arange(kr) - r + WIN_ROWS - 1
        bias = rpb[:, dr_idx[:, None, None, None], dc_idx[None]]
        bias = bias.transpose(0, 2, 3, 1, 4).reshape(H, ncb, Q_COL_BLOCK, n_loc).astype(jnp.float32)
        s_loc = jnp.einsum('bnqhd,bnkhd->bhnqk', q_r, k_b).astype(jnp.float32) * scale + bias[None]
        s_loc = jnp.where(mask, s_loc, -jnp.inf)
        s_ctx = jnp.einsum('bnqhd,bchd->bhnqc', q_r, k_ctx).astype(jnp.float32) * scale
        p = jax.nn.softmax(jnp.concatenate([s_loc, s_ctx], axis=-1), axis=-1).astype(q.dtype)
        o = (jnp.einsum('bhnqk,bnkhd->bnqhd', p[..., :n_loc], v_b)
             + jnp.einsum('bhnqc,bchd->bnqhd', p[..., n_loc:], v_ctx))
        return o

    o = lax.map(one_row, (jnp.arange(rows), q_rows))
    return o.transpose(1, 0, 2, 3, 4, 5).reshape(B, L, H * Dh)


def _conformer_conv(u, g, w_dw, b_dw, ln_g, ln_b):
    a = u * jax.nn.sigmoid(g)
    a = lax.conv_general_dilated(a, w_dw[:, None, :], window_strides=(1,),
                                 padding=[(CONV_WIDTH // 2, CONV_WIDTH // 2)],
                                 dimension_numbers=('NWC', 'WIO', 'NWC'),
                                 feature_group_count=D_CONV) + b_dw
    return jax.nn.silu(_layernorm(a, ln_g, ln_b))


def _mlstm_scan(q, k, v, ig, lf, C0, n0, m0):
    B, L, H, _ = q.shape
    nc = L // CHUNK

    def seq_chunks(t):
        return t.reshape(B, nc, CHUNK, H, t.shape[-1]).transpose(1, 0, 3, 2, 4)

    def gate_chunks(t):
        return t.reshape(B, nc, CHUNK, H).transpose(1, 0, 3, 2)

    causal = jnp.tril(jnp.ones((CHUNK, CHUNK), dtype=bool))

    def step(carry, xs):
        C, n, m = carry
        qc, kc, vc, ic, fc = xs
        b = jnp.cumsum(fc, axis=-1)
        log_d = jnp.where(causal, b[..., :, None] - b[..., None, :] + ic[..., None, :], -jnp.inf)
        inter = b + m[..., None]
        m_t = jnp.maximum(inter, jnp.max(log_d, axis=-1))
        w_inter = jnp.exp(inter - m_t)
        s = jnp.einsum('bhtd,bhsd->bhts', qc, kc) * jnp.exp(log_d - m_t[..., None])
        num = w_inter[..., None] * jnp.einsum('bhvk,bhtk->bhtv', C, qc) + jnp.einsum('bhts,bhsv->bhtv', s, vc)
        den = w_inter * jnp.einsum('bhk,bhtk->bht', n, qc) + jnp.sum(s, axis=-1)
        h = num / jnp.maximum(jnp.abs(den), jnp.exp(-m_t))[..., None]
        m_new = m_t[..., -1]
        w_prev = jnp.exp(b[..., -1] + m - m_new)
        w_src = jnp.exp(b[..., -1:] - b + ic - m_new[..., None])
        C_new = w_prev[..., None, None] * C + jnp.einsum('bhs,bhsv,bhsk->bhvk', w_src, vc, kc)
        n_new = w_prev[..., None] * n + jnp.einsum('bhs,bhsk->bhk', w_src, kc)
        return (C_new, n_new, m_new), h

    (C, n, m), h = lax.scan(step, (C0, n0, m0),
                            (seq_chunks(q), seq_chunks(k), seq_chunks(v), gate_chunks(ig), gate_chunks(lf)))
    h = h.transpose(1, 0, 3, 2, 4).reshape(B, L, H, v.shape[-1])
    return h, (C, n, m)


def _mlstm_branch(qm, km, vm, om, gm, norm_g, C0, n0, m0):
    B, L, _ = qm.shape
    f32 = jnp.float32

    def heads(t):
        return t.reshape(B, L, N_HEADS_M, HEAD_DIM_M).astype(f32)

    q, k, v = heads(qm), heads(km) * HEAD_DIM_M ** -0.5, heads(vm)
    g = gm.reshape(B, L, 4, N_HEADS_M).astype(f32)
    C0, n0, m0 = C0.astype(f32), n0.astype(f32), m0.astype(f32)
    h_f, (C_f, n_f, m_f) = _mlstm_scan(q, k, v, g[:, :, 0], jax.nn.log_sigmoid(g[:, :, 1]),
                                       C0[:, 0], n0[:, 0], m0[:, 0])

    def rev(t):
        return jnp.flip(t, axis=1)

    h_b, (C_b, n_b, m_b) = _mlstm_scan(rev(q), rev(k), rev(v), rev(g[:, :, 2]),
                                       rev(jax.nn.log_sigmoid(g[:, :, 3])),
                                       C0[:, 1], n0[:, 1], m0[:, 1])
    h = h_f + rev(h_b)
    mu = jnp.mean(h, axis=-1, keepdims=True)
    var = jnp.mean(jnp.square(h - mu), axis=-1, keepdims=True)
    hn = ((h - mu) * lax.rsqrt(var + LN_EPS)).reshape(B, L, D_M) * norm_g.astype(f32)
    y = (jax.nn.sigmoid(om.astype(f32)) * hn).astype(qm.dtype)
    return y, (jnp.stack([C_f, C_b], axis=1), jnp.stack([n_f, n_b], axis=1), jnp.stack([m_f, m_b], axis=1))


def _merge(y_a, y_c, y_m, ga, gc, gm, lp):
    merged = (jax.nn.sigmoid(ga) * (y_a @ lp['w_pa'])
              + jax.nn.sigmoid(gc) * (y_c @ lp['w_pc'])
              + jax.nn.sigmoid(gm) * (y_m @ lp['w_pm']))
    return merged @ lp['w_out']


def _mixer_context(h, lp):
    B, L, _ = h.shape
    qa, ka, va, cu, cg, qm, km, vm, om, gm, ga, gc, gmx = _split_in(h @ lp['w_in'] + lp['b_in'])
    qa = qa.reshape(B, L, N_HEADS_A, HEAD_DIM_A)
    ka = ka.reshape(B, L, N_HEADS_A, HEAD_DIM_A)
    va = va.reshape(B, L, N_HEADS_A, HEAD_DIM_A)
    y_a = _context_attention(qa, ka, va)
    y_c = _conformer_conv(cu, cg, lp['w_dw'], lp['b_dw'], lp['cln_g'], lp['cln_b'])
    C0 = jnp.zeros((B, 2, N_HEADS_M, HEAD_DIM_M, HEAD_DIM_M), jnp.float32)
    n0 = jnp.zeros((B, 2, N_HEADS_M, HEAD_DIM_M), jnp.float32)
    m0 = jnp.full((B, 2, N_HEADS_M), -jnp.inf, jnp.float32)
    y_m, state = _mlstm_branch(qm, km, vm, om, gm, lp['mnorm_g'], C0, n0, m0)
    return _merge(y_a, y_c, y_m, ga, gc, gmx, lp), ka, va, state


def _mixer_latent(h, lp, k_ctx, v_ctx, C0, n0, m0):
    B, L, _ = h.shape
    qa, ka, va, cu, cg, qm, km, vm, om, gm, ga, gc, gmx = _split_in(h @ lp['w_in'] + lp['b_in'])
    qa = qa.reshape(B, L, N_HEADS_A, HEAD_DIM_A)
    ka = ka.reshape(B, L, N_HEADS_A, HEAD_DIM_A)
    va = va.reshape(B, L, N_HEADS_A, HEAD_DIM_A)
    y_a = _neighbourhood_attention(qa, ka, va, k_ctx.astype(qa.dtype), v_ctx.astype(qa.dtype), lp['rpb'])
    y_c = _conformer_conv(cu, cg, lp['w_dw'], lp['b_dw'], lp['cln_g'], lp['cln_b'])
    y_m, _ = _mlstm_branch(qm, km, vm, om, gm, lp['mnorm_g'], C0, n0, m0)
    return _merge(y_a, y_c, y_m, ga, gc, gmx, lp)


def _moe(x, lp):
    B, L, D = x.shape
    t = x.reshape(B * L, D)
    T = B * L
    logits = (t @ lp['w_router'] + lp['b_router']).astype(jnp.float32)
    top_val, top_idx = lax.top_k(logits, TOP_K)
    gate = jax.nn.softmax(top_val, axis=-1).astype(x.dtype)
    M = T * TOP_K
    flat_e = top_idx.reshape(M)
    flat_tok = jnp.repeat(jnp.arange(T, dtype=jnp.int32), TOP_K)
    flat_g = gate.reshape(M)
    order = jnp.argsort(flat_e)
    se = flat_e[order]
    counts = jnp.bincount(flat_e, length=N_EXPERTS)
    padded = (counts + MOE_BLOCK - 1) // MOE_BLOCK * MOE_BLOCK
    starts = jnp.cumsum(counts) - counts
    pend = jnp.cumsum(padded)
    pstarts = pend - padded
    pos = jnp.arange(M, dtype=jnp.int32) - starts[se] + pstarts[se]
    n_blocks = -(-M // MOE_BLOCK) + N_EXPERTS
    P = n_blocks * MOE_BLOCK
    slot_tok = jnp.full((P,), T, jnp.int32).at[pos].set(flat_tok[order])
    slot_g = jnp.zeros((P,), x.dtype).at[pos].set(flat_g[order])
    block_e = jnp.minimum(jnp.searchsorted(pend, jnp.arange(n_blocks, dtype=jnp.int32) * MOE_BLOCK, side='right'),
                          N_EXPERTS - 1)
    xs = jnp.concatenate([t, jnp.zeros((1, D), t.dtype)], axis=0)[slot_tok].reshape(n_blocks, MOE_BLOCK, D)
    w_gu, b_gu, w_down, b_down = lp['w_gu'], lp['b_gu'], lp['w_down'], lp['b_down']

    def expert_block(args):
        xb, e = args
        hgu = xb @ w_gu[e] + b_gu[e]
        h_glu, h_lin = jnp.split(hgu, 2, axis=-1)
        h_glu = jnp.minimum(h_glu, SWIGLU_LIMIT)
        h_lin = jnp.clip(h_lin, -SWIGLU_LIMIT, SWIGLU_LIMIT)
        act = (h_lin + 1) * (h_glu * jax.nn.sigmoid(SWIGLU_ALPHA * h_glu))
        return act @ w_down[e] + b_down[e]

    ys = lax.map(expert_block, (xs, block_e)).reshape(P, D) * slot_g[:, None]
    out = jnp.zeros((T + 1, D), x.dtype).at[slot_tok].add(ys)[:T]
    return out.reshape(B, L, D)


def setup_inputs(seed: int = 0) -> dict:
    key = jax.random.key(seed)
    ks = iter(jax.random.split(key, 40))

    def nrm(shape, scale):
        return jax.random.normal(next(ks), shape, jnp.float32) * scale

    D = D_MODEL
    x_prompt = nrm((BATCH, SEQ, D), 1.0)
    x_sample = nrm((DEC_BATCH, DEC_SEQ, D), 1.0)
    cache_k = nrm((DEC_BATCH, DEPTH, PAST_LEN, N_HEADS_A, HEAD_DIM_A), 1.0)
    cache_v = nrm((DEC_BATCH, DEPTH, PAST_LEN, N_HEADS_A, HEAD_DIM_A), 1.0)
    state_C = nrm((DEC_BATCH, DEPTH, 2, N_HEADS_M, HEAD_DIM_M, HEAD_DIM_M), 0.3)
    state_n = nrm((DEC_BATCH, DEPTH, 2, N_HEADS_M, HEAD_DIM_M), 0.3)
    state_m = nrm((DEC_BATCH, DEPTH, 2, N_HEADS_M), 0.5) + 1.0
    c = nrm((DEC_BATCH, D), 1.0)
    c_ctx = nrm((D,), 1.0)
    norm1_g = 1.0 + nrm((DEPTH, D), 0.02)
    w_mod = nrm((DEPTH, D, 6 * D), 0.5 * D ** -0.5)
    b_mod = nrm((DEPTH, 6 * D), 0.02)
    w_in = nrm((DEPTH, D, N_IN), D ** -0.5)
    forget_cols = np.concatenate([np.arange(GATE_OFF + N_HEADS_M, GATE_OFF + 2 * N_HEADS_M),
                                  np.arange(GATE_OFF + 3 * N_HEADS_M, GATE_OFF + 4 * N_HEADS_M)])
    forget_bias = jnp.tile(jnp.linspace(3.0, 6.0, N_HEADS_M), 2)
    b_in = nrm((DEPTH, N_IN), 0.02).at[:, forget_cols].add(forget_bias)
    rpb = nrm((DEPTH, N_HEADS_A, 2 * WIN_ROWS - 1, 2 * WIN_COLS - 1), 0.05)
    w_dw = nrm((DEPTH, CONV_WIDTH, D_CONV), CONV_WIDTH ** -0.5)
    b_dw = nrm((DEPTH, D_CONV), 0.02)
    cln_g = 1.0 + nrm((DEPTH, D_CONV), 0.02)
    cln_b = nrm((DEPTH, D_CONV), 0.02)
    mnorm_g = 1.0 + nrm((DEPTH, D_M), 0.02)
    w_pa = nrm((DEPTH, D_A, D), D_A ** -0.5)
    w_pc = nrm((DEPTH, D_CONV, D), D_CONV ** -0.5)
    w_pm = nrm((DEPTH, D_M, D), D_M ** -0.5)
    w_out = nrm((DEPTH, D, D), D ** -0.5)
    norm2_g = 1.0 + nrm((DEPTH, D), 0.02)
    w_router = nrm((DEPTH, D, N_EXPERTS), D ** -0.5)
    b_router = nrm((DEPTH, N_EXPERTS), 0.01)
    w_gu = nrm((DEPTH, N_EXPERTS, D, 2 * D_EXPERT), D ** -0.5)
    b_gu = nrm((DEPTH, N_EXPERTS, 2 * D_EXPERT), 0.02)
    w_down = nrm((DEPTH, N_EXPERTS, D_EXPERT, D), D_EXPERT ** -0.5)
    b_down = nrm((DEPTH, N_EXPERTS, D), 0.02)
    final_g = 1.0 + nrm((D,), 0.02)
    return {'x_prompt': x_prompt, 'x_sample': x_sample, 'cache_k': cache_k, 'cache_v': cache_v,
            'state_C': state_C, 'state_n': state_n, 'state_m': state_m, 'c': c, 'c_ctx': c_ctx,
            'norm1_g': norm1_g, 'w_mod': w_mod, 'b_mod': b_mod, 'w_in': w_in, 'b_in': b_in,
            'rpb': rpb, 'w_dw': w_dw, 'b_dw': b_dw, 'cln_g': cln_g, 'cln_b': cln_b,
            'mnorm_g': mnorm_g, 'w_pa': w_pa, 'w_pc': w_pc, 'w_pm': w_pm, 'w_out': w_out,
            'norm2_g': norm2_g, 'w_router': w_router, 'b_router': b_router, 'w_gu': w_gu,
            'b_gu': b_gu, 'w_down': w_down, 'b_down': b_down, 'final_g': final_g}


def reference(x_prompt, x_sample, cache_k, cache_v, state_C, state_n, state_m, c, c_ctx,
              norm1_g, w_mod, b_mod, w_in, b_in, rpb, w_dw, b_dw, cln_g, cln_b, mnorm_g,
              w_pa, w_pc, w_pm, w_out, norm2_g, w_router, b_router, w_gu, b_gu, w_down, b_down,
              final_g):
    ctx_cond = jax.nn.silu(c_ctx)
    lat_cond = jax.nn.silu(c)
    xp, xs = x_prompt, x_sample
    ks, vs, Cs, ns, ms = [], [], [], [], []
    for l in range(DEPTH):
        lp = {'w_in': w_in[l], 'b_in': b_in[l], 'rpb': rpb[l], 'w_dw': w_dw[l], 'b_dw': b_dw[l],
              'cln_g': cln_g[l], 'cln_b': cln_b[l], 'mnorm_g': mnorm_g[l], 'w_pa': w_pa[l],
              'w_pc': w_pc[l], 'w_pm': w_pm[l], 'w_out': w_out[l], 'w_router': w_router[l],
              'b_router': b_router[l], 'w_gu': w_gu[l], 'b_gu': b_gu[l], 'w_down': w_down[l],
              'b_down': b_down[l]}
        sh1, sc1, g1, sh2, sc2, g2 = jnp.split(ctx_cond @ w_mod[l] + b_mod[l], 6, axis=-1)
        h = _rmsnorm(xp, norm1_g[l]) * (1 + sc1) + sh1
        mix, k_l, v_l, (C_l, n_l, m_l) = _mixer_context(h, lp)
        xp = xp + g1 * mix
        h = _rmsnorm(xp, norm2_g[l]) * (1 + sc2) + sh2
        xp = xp + g2 * _moe(h, lp)
        ks.append(k_l)
        vs.append(v_l)
        Cs.append(C_l)
        ns.append(n_l)
        ms.append(m_l)
        sh1, sc1, g1, sh2, sc2, g2 = jnp.split((lat_cond @ w_mod[l] + b_mod[l])[:, None, :], 6, axis=-1)
        h = _rmsnorm(xs, norm1_g[l]) * (1 + sc1) + sh1
        mix = _mixer_latent(h, lp, cache_k[:, l], cache_v[:, l], state_C[:, l], state_n[:, l], state_m[:, l])
        xs = xs + g1 * mix
        h = _rmsnorm(xs, norm2_g[l]) * (1 + sc2) + sh2
        xs = xs + g2 * _moe(h, lp)
    y_prompt = _rmsnorm(xp, final_g)
    y_sample = _rmsnorm(xs, final_g)
    new_cache_k = jnp.stack(ks, axis=1)
    new_cache_v = jnp.stack(vs, axis=1)
    new_state_C = jnp.stack(Cs, axis=1)
    new_state_n = jnp.stack(ns, axis=1)
    new_state_m = jnp.stack(ms, axis=1)
    return (y_prompt, y_sample, new_cache_k, new_cache_v, new_state_C, new_state_n, new_state_m)
```

```python
import functools

import numpy as np
import jax
import jax.numpy as jnp
from jax import lax
from jax.experimental import pallas as pl
from jax.experimental.pallas import tpu as pltpu

F32 = jnp.float32
BF16 = jnp.bfloat16
HIGHEST = lax.Precision.HIGHEST

D_MODEL = 1024
BATCH = 16
SEQ = 256
DEPTH = 2
DEC_BATCH = 8
DEC_SEQ = 1024
PAST_LEN = 512
GRID_W = 64
N_HEADS_A = 8
HEAD_DIM_A = 64
D_A = N_HEADS_A * HEAD_DIM_A
WIN_ROWS = 8
WIN_COLS = 16
D_CONV = 512
CONV_WIDTH = 31
N_HEADS_M = 4
HEAD_DIM_M = 128
D_M = N_HEADS_M * HEAD_DIM_M
N_GATE_M = 4 * N_HEADS_M
CHUNK = 64
N_EXPERTS = 32
TOP_K = 4
D_EXPERT = 1024
SWIGLU_ALPHA = 1.702
SWIGLU_LIMIT = 7.0
RMS_EPS = 1e-6
LN_EPS = 1e-5
GATE_OFF = 3 * D_A + 2 * D_CONV + 4 * D_M
N_IN = GATE_OFF + N_GATE_M + 3 * D_MODEL

T_CTX = BATCH * SEQ
T_LAT = DEC_BATCH * DEC_SEQ
T_ALL = T_CTX + T_LAT
N_SEG = 1 + DEC_BATCH
SEG_PAD = 16
GRID_ROWS = DEC_SEQ // GRID_W
NEG = -1e30

N_Z = 8192
ZB_GA, ZB_GC, ZB_GM = 0, 1, 2
ZB_QA, ZB_KA, ZB_VA = 6, 7, 8
ZB_CU, ZB_CG = 9, 10
ZB_QM, ZB_KM, ZB_VM, ZB_OM = 11, 12, 13, 14
ZB_GATES = 60
Z_KA_OFF = ZB_KA * 512
Z_VA_OFF = ZB_VA * 512

TM_TOK = 512
MOE_BLK = 256
M_ASSIGN = T_ALL * TOP_K
N_BLOCKS = M_ASSIGN // MOE_BLK + N_EXPERTS
P_SLOTS = N_BLOCKS * MOE_BLK
NB_PAD = 256
TM_DMA = 256
VMEM_LIMIT = 56 * 1024 * 1024


def _cparams(sem=None):
    return pltpu.CompilerParams(dimension_semantics=sem, vmem_limit_bytes=VMEM_LIMIT)


def _seg_of_tile(i, tile):
    n_ctx = T_CTX // tile
    per_lat = DEC_SEQ // tile
    return jnp.where(i < n_ctx, 0, 1 + (i - n_ctx) // per_lat)


def _dot_nt(a, b):
    return lax.dot_general(a, b, (((1,), (1,)), ((), ())), preferred_element_type=F32)


def _sigmoid(x):
    return 1.0 / (1.0 + jnp.exp(-x))


def _mod_kernel(c_ref, w_ref, b_ref, o_ref):
    c = c_ref[...]
    s = c * _sigmoid(c)
    o_ref[...] = jnp.dot(s, w_ref[...], precision=HIGHEST, preferred_element_type=F32) + b_ref[...]


def _modulation(cond, w_mod, b_mod):
    tn = 1536
    return pl.pallas_call(
        _mod_kernel,
        grid=(DEPTH, 6 * D_MODEL // tn),
        in_specs=[pl.BlockSpec((SEG_PAD, D_MODEL), lambda l, j: (0, 0)),
                  pl.BlockSpec((None, D_MODEL, tn), lambda l, j: (l, 0, j)),
                  pl.BlockSpec((None, 1, tn), lambda l, j: (l, 0, j))],
        out_specs=pl.BlockSpec((None, SEG_PAD, tn), lambda l, j: (l, 0, j)),
        out_shape=jax.ShapeDtypeStruct((DEPTH, SEG_PAD, 6 * D_MODEL), F32),
        compiler_params=_cparams(("arbitrary", "arbitrary")),
        name="modulation",
    )(cond, w_mod, b_mod.reshape(DEPTH, 1, 6 * D_MODEL))


def _normmod(x, g, mod, shift_idx, scale_idx):
    y = x * lax.rsqrt(jnp.mean(x * x, axis=-1, keepdims=True) + RMS_EPS) * g
    return y * (1.0 + mod[scale_idx:scale_idx + 1, :]) + mod[shift_idx:shift_idx + 1, :]


def _in_proj_kernel(x_ref, g_ref, mod_ref, w_ref, b_ref, o_ref, h_scr):
    @pl.when(pl.program_id(1) == 0)
    def _():
        h_scr[...] = _normmod(x_ref[...], g_ref[...], mod_ref[...], 0, 1).astype(BF16)

    o_ref[...] = jnp.dot(h_scr[...], w_ref[...], preferred_element_type=F32) + b_ref[...]


def _in_proj(x, norm_g, mod, w, b):
    tn = 1024
    return pl.pallas_call(
        _in_proj_kernel,
        grid=(T_ALL // TM_TOK, N_Z // tn),
        in_specs=[pl.BlockSpec((TM_TOK, D_MODEL), lambda i, j: (i, 0)),
                  pl.BlockSpec((1, D_MODEL), lambda i, j: (0, 0)),
                  pl.BlockSpec((None, 6, D_MODEL), lambda i, j: (_seg_of_tile(i, TM_TOK), 0, 0)),
                  pl.BlockSpec((D_MODEL, tn), lambda i, j: (0, j)),
                  pl.BlockSpec((1, tn), lambda i, j: (0, j))],
        out_specs=pl.BlockSpec((TM_TOK, tn), lambda i, j: (i, j)),
        out_shape=jax.ShapeDtypeStruct((T_ALL, N_Z), F32),
        scratch_shapes=[pltpu.VMEM((TM_TOK, D_MODEL), BF16)],
        compiler_params=_cparams(("arbitrary", "arbitrary")),
        name="in_proj",
    )(x, norm_g, mod, w, b)


def _ctx_attn_kernel(q_ref, k_ref, v_ref, o_ref):
    scale = HEAD_DIM_A ** -0.5
    for h in range(N_HEADS_A):
        sl = slice(h * HEAD_DIM_A, (h + 1) * HEAD_DIM_A)
        q = q_ref[:, sl].astype(BF16)
        k = k_ref[:, sl].astype(BF16)
        v = v_ref[:, sl].astype(BF16)
        s = _dot_nt(q, k) * scale
        p = jnp.exp(s - jnp.max(s, axis=-1, keepdims=True))
        l = jnp.sum(p, axis=-1, keepdims=True)
        o_ref[:, sl] = jnp.dot(p.astype(BF16), v, preferred_element_type=F32) / l


def _ctx_attention(z):
    def spec(cb):
        return pl.BlockSpec((SEQ, D_A), lambda b: (b, cb))

    return pl.pallas_call(
        _ctx_attn_kernel,
        grid=(BATCH,),
        in_specs=[spec(ZB_QA), spec(ZB_KA), spec(ZB_VA)],
        out_specs=pl.BlockSpec((SEQ, D_A), lambda b: (b, 0)),
        out_shape=jax.ShapeDtypeStruct((T_CTX, D_A), F32),
        compiler_params=_cparams(("arbitrary",)),
        name="ctx_attention",
    )(z, z, z)


def _natt_kernel(q_ref, k_ref, v_ref, kc_ref, vc_ref, bias_ref, o_ref):
    scale = HEAD_DIM_A ** -0.5
    r = pl.program_id(1)
    rs = jnp.clip(r - WIN_ROWS // 2, 0, GRID_ROWS - WIN_ROWS)
    start = pl.multiple_of(rs * GRID_W, GRID_W)
    band = WIN_ROWS * GRID_W
    for h in range(N_HEADS_A):
        sl = slice(h * HEAD_DIM_A, (h + 1) * HEAD_DIM_A)
        q = q_ref[:, sl].astype(BF16)
        kb = k_ref[pl.ds(start, band), sl].astype(BF16)
        vb = v_ref[pl.ds(start, band), sl].astype(BF16)
        kc = kc_ref[:, sl].astype(BF16)
        vc = vc_ref[:, sl].astype(BF16)
        s_loc = _dot_nt(q, kb) * scale + bias_ref[h]
        s_ctx = _dot_nt(q, kc) * scale
        m = jnp.maximum(jnp.max(s_loc, axis=-1, keepdims=True), jnp.max(s_ctx, axis=-1, keepdims=True))
        p_loc = jnp.exp(s_loc - m)
        p_ctx = jnp.exp(s_ctx - m)
        l = jnp.sum(p_loc, axis=-1, keepdims=True) + jnp.sum(p_ctx, axis=-1, keepdims=True)
        o = (jnp.dot(p_loc.astype(BF16), vb, preferred_element_type=F32)
             + jnp.dot(p_ctx.astype(BF16), vc, preferred_element_type=F32))
        o_ref[:, sl] = o / l


def _natt_bias(rpb_l):
    r = np.arange(GRID_ROWS)
    rs = np.clip(r - WIN_ROWS // 2, 0, GRID_ROWS - WIN_ROWS)
    dr = rs[:, None] + np.arange(WIN_ROWS)[None, :] - r[:, None] + WIN_ROWS - 1
    qc = np.arange(GRID_W)
    kc = np.arange(GRID_W)
    cs = np.clip(qc - WIN_COLS // 2, 0, GRID_W - WIN_COLS)
    ok = (kc[None, :] >= cs[:, None]) & (kc[None, :] < cs[:, None] + WIN_COLS)
    dc = np.clip(kc[None, :] - qc[:, None] + WIN_COLS - 1, 0, 2 * WIN_COLS - 2)
    b = rpb_l[:, dr[:, :, None, None], dc[None, None, :, :]]
    b = jnp.where(jnp.asarray(ok)[None, None, None], b, NEG)
    return b.transpose(1, 0, 3, 2, 4).reshape(GRID_ROWS, N_HEADS_A, GRID_W, WIN_ROWS * GRID_W)


def _natt(z, cache_k, cache_v, bias, layer):
    lat0 = T_CTX // DEC_SEQ
    row0 = T_CTX // GRID_W
    return pl.pallas_call(
        _natt_kernel,
        grid=(DEC_BATCH, GRID_ROWS),
        in_specs=[pl.BlockSpec((GRID_W, D_A), lambda b, r: (row0 + b * GRID_ROWS + r, ZB_QA)),
                  pl.BlockSpec((DEC_SEQ, D_A), lambda b, r: (lat0 + b, ZB_KA)),
                  pl.BlockSpec((DEC_SEQ, D_A), lambda b, r: (lat0 + b, ZB_VA)),
                  pl.BlockSpec((None, None, PAST_LEN, D_A), lambda b, r: (b, layer, 0, 0)),
                  pl.BlockSpec((None, None, PAST_LEN, D_A), lambda b, r: (b, layer, 0, 0)),
                  pl.BlockSpec((None, N_HEADS_A, GRID_W, WIN_ROWS * GRID_W), lambda b, r: (r, 0, 0, 0))],
        out_specs=pl.BlockSpec((GRID_W, D_A), lambda b, r: (b * GRID_ROWS + r, 0)),
        out_shape=jax.ShapeDtypeStruct((T_LAT, D_A), F32),
        compiler_params=_cparams(("arbitrary", "arbitrary")),
        name="nbr_attention",
    )(z, z, z, cache_k, cache_v, bias)


CONV_HALO = 16
CONV_ROWS = 64


def _conv_kernel(u_ref, g_ref, w_ref, b_ref, lg_ref, lb_ref, o_ref, pad_scr, *, seq):
    zeros = jnp.zeros((CONV_HALO, D_CONV), F32)
    pad_scr[0:CONV_HALO, :] = zeros
    pad_scr[CONV_HALO + seq:2 * CONV_HALO + seq, :] = zeros
    pad_scr[CONV_HALO:CONV_HALO + seq, :] = u_ref[...] * _sigmoid(g_ref[...])
    first = CONV_HALO - CONV_WIDTH // 2
    for c in range(seq // CONV_ROWS):
        base = c * CONV_ROWS
        acc = jnp.broadcast_to(b_ref[...], (CONV_ROWS, D_CONV))
        for j in range(CONV_WIDTH):
            acc = acc + pad_scr[base + first + j:base + first + j + CONV_ROWS, :] * w_ref[j:j + 1, :]
        mu = jnp.mean(acc, axis=-1, keepdims=True)
        xc = acc - mu
        var = jnp.mean(xc * xc, axis=-1, keepdims=True)
        y = xc * lax.rsqrt(var + LN_EPS) * lg_ref[...] + lb_ref[...]
        o_ref[base:base + CONV_ROWS, :] = y * _sigmoid(y)


def _conv(z, w_dw, b_dw, ln_g, ln_b, seq, n_seq, row_block0):
    def vec():
        return pl.BlockSpec((1, D_CONV), lambda b: (0, 0))

    return pl.pallas_call(
        functools.partial(_conv_kernel, seq=seq),
        grid=(n_seq,),
        in_specs=[pl.BlockSpec((seq, D_CONV), lambda b: (row_block0 + b, ZB_CU)),
                  pl.BlockSpec((seq, D_CONV), lambda b: (row_block0 + b, ZB_CG)),
                  pl.BlockSpec((CONV_WIDTH, D_CONV), lambda b: (0, 0)),
                  vec(), vec(), vec()],
        out_specs=pl.BlockSpec((seq, D_CONV), lambda b: (b, 0)),
        out_shape=jax.ShapeDtypeStruct((n_seq * seq, D_CONV), F32),
        scratch_shapes=[pltpu.VMEM((seq + 2 * CONV_HALO, D_CONV), F32)],
        compiler_params=_cparams(("arbitrary",)),
        name="conformer_conv",
    )(z, z, w_dw, b_dw, ln_g, ln_b)


N_STREAM = 2 * N_HEADS_M


def _mlstm_kernel(q_ref, k_ref, v_ref, om_ref, gt_ref, c0_ref, n0_ref, m0_ref, ng_ref,
                  y_ref, c_out, n_out, m_out, hf_scr, hb_scr, c_scr, n_scr, m_scr, *, seq):
    nc = seq // CHUNK
    c_scr[...] = c0_ref[...]
    n_scr[...] = n0_ref[...]
    m_scr[...] = m0_ref[...]
    ti = lax.broadcasted_iota(jnp.int32, (CHUNK, CHUNK), 0)
    si = lax.broadcasted_iota(jnp.int32, (CHUNK, CHUNK), 1)
    masks = (si <= ti, si >= ti)
    tris = tuple(mk.astype(F32) for mk in masks)
    kscale = HEAD_DIM_M ** -0.5

    def chunk_step(c, carry):
        for d in range(2):
            cidx = c if d == 0 else nc - 1 - c
            off = pl.multiple_of(cidx * CHUNK, CHUNK)
            last = CHUNK - 1 if d == 0 else 0
            g = gt_ref[pl.ds(off, CHUNK), :]
            lf = jnp.minimum(g, 0.0) - jnp.log(1.0 + jnp.exp(-jnp.abs(g)))
            cum = jnp.dot(tris[d], lf, precision=HIGHEST, preferred_element_type=F32)
            g_t = g.T
            cum_t = cum.T
            for h in range(N_HEADS_M):
                s_id = N_HEADS_M * d + h
                icol = 2 * N_HEADS_M * d + h
                fcol = icol + N_HEADS_M
                hs = slice(h * HEAD_DIM_M, (h + 1) * HEAD_DIM_M)
                b_col = cum[:, fcol:fcol + 1]
                i_col = g[:, icol:icol + 1]
                b_row = cum_t[fcol:fcol + 1, :]
                i_row = g_t[icol:icol + 1, :]
                b_last = b_col[last:last + 1, :]
                m_prev = m_scr[s_id:s_id + 1, 0:1]
                log_d = jnp.where(masks[d], b_col - b_row + i_row, NEG)
                inter = b_col + m_prev
                m_t = jnp.maximum(inter, jnp.max(log_d, axis=-1, keepdims=True))
                w_inter = jnp.exp(inter - m_t)
                qc = q_ref[pl.ds(off, CHUNK), hs]
                kc = k_ref[pl.ds(off, CHUNK), hs] * kscale
                vc = v_ref[pl.ds(off, CHUNK), hs]
                qb = qc.astype(BF16)
                kb = kc.astype(BF16)
                c_prev = c_scr[s_id]
                n_prev = n_scr[s_id:s_id + 1, :]
                s_mat = _dot_nt(qb, kb) * jnp.exp(log_d - m_t)
                num = (w_inter * _dot_nt(qb, c_prev.astype(BF16))
                       + jnp.dot(s_mat.astype(BF16), vc.astype(BF16), preferred_element_type=F32))
                den = (w_inter * jnp.sum(qc * n_prev, axis=-1, keepdims=True)
                       + jnp.sum(s_mat, axis=-1, keepdims=True))
                hh = num / jnp.maximum(jnp.abs(den), jnp.exp(-m_t))
                if d == 0:
                    hf_scr[pl.ds(off, CHUNK), hs] = hh
                else:
                    hb_scr[pl.ds(off, CHUNK), hs] = hh
                m_new = m_t[last:last + 1, :]
                w_prev = jnp.exp(b_last + m_prev - m_new)
                w_src = jnp.exp(b_last - b_col + i_col - m_new)
                upd = jnp.dot((w_src * vc).T.astype(BF16), kb, preferred_element_type=F32)
                c_scr[s_id] = w_prev * c_prev + upd
                n_scr[s_id:s_id + 1, :] = w_prev * n_prev + jnp.sum(w_src * kc, axis=0, keepdims=True)
                m_scr[s_id:s_id + 1, :] = jnp.broadcast_to(m_new, (1, HEAD_DIM_M))
        return carry

    lax.fori_loop(0, nc, chunk_step, 0)

    for h in range(N_HEADS_M):
        hs = slice(h * HEAD_DIM_M, (h + 1) * HEAD_DIM_M)
        hsum = hf_scr[:, hs] + hb_scr[:, hs]
        mu = jnp.mean(hsum, axis=-1, keepdims=True)
        xc = hsum - mu
        var = jnp.mean(xc * xc, axis=-1, keepdims=True)
        hn = xc * lax.rsqrt(var + LN_EPS) * ng_ref[:, hs]
        y_ref[:, hs] = _sigmoid(om_ref[:, hs]) * hn
    c_out[...] = c_scr[...]
    n_out[...] = n_scr[...]
    m_out[...] = m_scr[...]


def _mlstm(z, c0, n0, m0, norm_g, seq, n_seq, row_block0, state_map):
    lead = len(state_map(0))

    def zspec(cb):
        return pl.BlockSpec((seq, D_M), lambda b: (row_block0 + b, cb))

    def sspec(tail):
        return pl.BlockSpec((None,) * lead + tail, lambda b: state_map(b) + (0,) * len(tail))

    return pl.pallas_call(
        functools.partial(_mlstm_kernel, seq=seq),
        grid=(n_seq,),
        in_specs=[zspec(ZB_QM), zspec(ZB_KM), zspec(ZB_VM), zspec(ZB_OM),
                  pl.BlockSpec((seq, 128), lambda b: (row_block0 + b, ZB_GATES)),
                  sspec((N_STREAM, HEAD_DIM_M, HEAD_DIM_M)),
                  sspec((N_STREAM, HEAD_DIM_M)),
                  sspec((N_STREAM, HEAD_DIM_M)),
                  pl.BlockSpec((1, D_M), lambda b: (0, 0))],
        out_specs=[pl.BlockSpec((seq, D_M), lambda b: (b, 0)),
                   pl.BlockSpec((None, N_STREAM, HEAD_DIM_M, HEAD_DIM_M), lambda b: (b, 0, 0, 0)),
                   pl.BlockSpec((None, N_STREAM, HEAD_DIM_M), lambda b: (b, 0, 0)),
                   pl.BlockSpec((None, N_STREAM, HEAD_DIM_M), lambda b: (b, 0, 0))],
        out_shape=[jax.ShapeDtypeStruct((n_seq * seq, D_M), F32),
                   jax.ShapeDtypeStruct((n_seq, N_STREAM, HEAD_DIM_M, HEAD_DIM_M), F32),
                   jax.ShapeDtypeStruct((n_seq, N_STREAM, HEAD_DIM_M), F32),
                   jax.ShapeDtypeStruct((n_seq, N_STREAM, HEAD_DIM_M), F32)],
        scratch_shapes=[pltpu.VMEM((seq, D_M), F32), pltpu.VMEM((seq, D_M), F32),
                        pltpu.VMEM((N_STREAM, HEAD_DIM_M, HEAD_DIM_M), F32),
                        pltpu.VMEM((N_STREAM, HEAD_DIM_M), F32),
                        pltpu.VMEM((N_STREAM, HEAD_DIM_M), F32)],
        compiler_params=_cparams(("arbitrary",)),
        name="mlstm",
    )(z, z, z, z, z, c0, n0, m0, norm_g)


def _merge_kernel(x_ref, mod_ref, ya_ref, yc_ref, ym_ref, ga_ref, gc_ref, gm_ref,
                  wa_ref, wc_ref, wm_ref, wo_ref, o_ref):
    def branch(y_ref, g_ref, w_ref):
        return _sigmoid(g_ref[...]) * jnp.dot(y_ref[...].astype(BF16), w_ref[...], preferred_element_type=F32)

    merged = branch(ya_ref, ga_ref, wa_ref) + branch(yc_ref, gc_ref, wc_ref) + branch(ym_ref, gm_ref, wm_ref)
    mix = jnp.dot(merged.astype(BF16), wo_ref[...], preferred_element_type=F32)
    o_ref[...] = x_ref[...] + mod_ref[2:3, :] * mix


def _merge(x, mod, ya, yc, ym, z, w_pa, w_pc, w_pm, w_out):
    def rows(width, cb=0):
        return pl.BlockSpec((TM_TOK, width), lambda i: (i, cb))

    def full(shape):
        return pl.BlockSpec(shape, lambda i: (0, 0))

    return pl.pallas_call(
        _merge_kernel,
        grid=(T_ALL // TM_TOK,),
        in_specs=[rows(D_MODEL),
                  pl.BlockSpec((None, 6, D_MODEL), lambda i: (_seg_of_tile(i, TM_TOK), 0, 0)),
                  rows(D_A), rows(D_CONV), rows(D_M),
                  rows(D_MODEL, ZB_GA), rows(D_MODEL, ZB_GC), rows(D_MODEL, ZB_GM),
                  full((D_A, D_MODEL)), full((D_CONV, D_MODEL)), full((D_M, D_MODEL)),
                  full((D_MODEL, D_MODEL))],
        out_specs=rows(D_MODEL),
        out_shape=jax.ShapeDtypeStruct((T_ALL, D_MODEL), F32),
        compiler_params=_cparams(("arbitrary",)),
        name="merge",
    )(x, mod, ya, yc, ym, z, z, z, w_pa, w_pc, w_pm, w_out)


def _router_kernel(x_ref, g_ref, mod_ref, wr_ref, br_ref, h_ref, idx_ref, gate_ref, rank_ref, cnt_ref, cnt_scr):
    i = pl.program_id(0)

    @pl.when(i == 0)
    def _():
        cnt_scr[...] = jnp.zeros_like(cnt_scr)

    h = _normmod(x_ref[...], g_ref[...], mod_ref[...], 3, 4)
    h_ref[...] = h
    logits = lax.dot_general(wr_ref[...], h, (((1,), (1,)), ((), ())), precision=HIGHEST,
                             preferred_element_type=F32) + br_ref[...]
    e_iota = lax.broadcasted_iota(jnp.int32, (N_EXPERTS, TM_TOK), 0).astype(F32)
    sels, vals = [], []
    l = logits
    for k in range(TOP_K):
        m = jnp.max(l, axis=0, keepdims=True)
        idx = jnp.min(jnp.where(l == m, e_iota, float(N_EXPERTS)), axis=0, keepdims=True)
        sel = e_iota == idx
        idx_ref[k:k + 1, :] = idx.astype(jnp.int32)
        vals.append(m)
        sels.append(sel)
        l = jnp.where(sel, -jnp.inf, l)
    exps = [jnp.exp(v - vals[0]) for v in vals]
    tot = exps[0] + exps[1] + exps[2] + exps[3]
    onehot = jnp.zeros((N_EXPERTS, TM_TOK), F32)
    for k in range(TOP_K):
        gate_ref[k:k + 1, :] = exps[k] / tot
        onehot = onehot + sels[k].astype(F32)
    t_src = lax.broadcasted_iota(jnp.int32, (TM_TOK, TM_TOK), 0)
    t_dst = lax.broadcasted_iota(jnp.int32, (TM_TOK, TM_TOK), 1)
    before = (t_src < t_dst).astype(BF16)
    carry = cnt_scr[:, 0:1]
    prefix = jnp.dot(onehot.astype(BF16), before, preferred_element_type=F32) + carry
    for k in range(TOP_K):
        rk = jnp.sum(jnp.where(sels[k], prefix, 0.0), axis=0, keepdims=True)
        rank_ref[k:k + 1, :] = rk.astype(jnp.int32)
    pad = jnp.zeros((8 - TOP_K, TM_TOK), jnp.int32)
    idx_ref[TOP_K:8, :] = pad
    rank_ref[TOP_K:8, :] = pad
    gate_ref[TOP_K:8, :] = pad.astype(F32)
    total = carry + jnp.sum(onehot, axis=1, keepdims=True)
    cnt_scr[...] = jnp.broadcast_to(total, cnt_scr.shape)
    cnt_ref[...] = cnt_scr[...]


def _router(x, norm_g, mod, w_rt, b_r):
    def tok(dtype):
        return jax.ShapeDtypeStruct((8, T_ALL), dtype)

    tspec = pl.BlockSpec((8, TM_TOK), lambda i: (0, i))
    return pl.pallas_call(
        _router_kernel,
        grid=(T_ALL // TM_TOK,),
        in_specs=[pl.BlockSpec((TM_TOK, D_MODEL), lambda i: (i, 0)),
                  pl.BlockSpec((1, D_MODEL), lambda i: (0, 0)),
                  pl.BlockSpec((None, 6, D_MODEL), lambda i: (_seg_of_tile(i, TM_TOK), 0, 0)),
                  pl.BlockSpec((N_EXPERTS, D_MODEL), lambda i: (0, 0)),
                  pl.BlockSpec((N_EXPERTS, 1), lambda i: (0, 0))],
        out_specs=[pl.BlockSpec((TM_TOK, D_MODEL), lambda i: (i, 0)), tspec, tspec, tspec,
                   pl.BlockSpec((N_EXPERTS, 128), lambda i: (0, 0))],
        out_shape=[jax.ShapeDtypeStruct((T_ALL, D_MODEL), F32), tok(jnp.int32), tok(F32), tok(jnp.int32),
                   jax.ShapeDtypeStruct((N_EXPERTS, 128), F32)],
        scratch_shapes=[pltpu.VMEM((N_EXPERTS, 128), F32)],
        compiler_params=_cparams(("arbitrary",)),
        name="moe_router",
    )(x, norm_g, mod, w_rt, b_r)


def _slots_kernel(idx_ref, rank_ref, cnt_ref, pos_ref, be_ref):
    cnt = cnt_ref[:, 0:1]
    padded = jnp.floor((cnt + (MOE_BLK - 1)) / MOE_BLK) * MOE_BLK
    ei = lax.broadcasted_iota(jnp.int32, (N_EXPERTS, N_EXPERTS), 0)
    ej = lax.broadcasted_iota(jnp.int32, (N_EXPERTS, N_EXPERTS), 1)
    lower = (ej < ei).astype(F32)
    pstart = jnp.dot(lower, jnp.broadcast_to(padded, (N_EXPERTS, 128)), precision=HIGHEST,
                     preferred_element_type=F32)[:, 0:1]
    pend = pstart + padded
    e_iota = lax.broadcasted_iota(jnp.int32, (N_EXPERTS, T_ALL), 0)
    for k in range(TOP_K):
        start_k = jnp.sum(jnp.where(e_iota == idx_ref[k:k + 1, :], pstart, 0.0), axis=0, keepdims=True)
        pos_ref[k:k + 1, :] = start_k.astype(jnp.int32) + rank_ref[k:k + 1, :]
    pos_ref[TOP_K:8, :] = jnp.zeros((8 - TOP_K, T_ALL), jnp.int32)
    row0 = (lax.broadcasted_iota(jnp.int32, (N_EXPERTS, NB_PAD), 1) * MOE_BLK).astype(F32)
    n_done = jnp.sum((pend <= row0).astype(jnp.int32), axis=0, keepdims=True)
    n_used = (pend[N_EXPERTS - 1:N_EXPERTS, :] / MOE_BLK).astype(jnp.int32)
    be_ref[0:1, :] = jnp.minimum(n_done, N_EXPERTS - 1)
    be_ref[1:2, :] = jnp.broadcast_to(n_used, (1, NB_PAD))
    be_ref[2:8, :] = jnp.zeros((6, NB_PAD), jnp.int32)


def _slots(idx, rank, cnt):
    return pl.pallas_call(
        _slots_kernel,
        out_shape=[jax.ShapeDtypeStruct((8, T_ALL), jnp.int32),
                   jax.ShapeDtypeStruct((8, NB_PAD), jnp.int32)],
        compiler_params=_cparams(),
        name="moe_slots",
    )(idx, rank, cnt)


def _dispatch_kernel(pos_ref, h_ref, xs_in, xs_ref, sems):
    del xs_in
    base = pl.program_id(0) * TM_DMA

    def row_copy(t, k):
        p = pos_ref[k * T_ALL + base + t]
        return pltpu.make_async_copy(h_ref.at[pl.ds(t, 1), :], xs_ref.at[pl.ds(p, 1), :], sems.at[k])

    def issue(t, carry):
        for k in range(TOP_K):
            row_copy(t, k).start()
        return carry

    lax.fori_loop(0, TM_DMA, issue, 0)

    def drain(t, carry):
        for k in range(TOP_K):
            row_copy(t, k).wait()
        return carry

    lax.fori_loop(0, TM_DMA, drain, 0)


def _dispatch(pos_flat, h, xs_init):
    return pl.pallas_call(
        _dispatch_kernel,
        grid_spec=pltpu.PrefetchScalarGridSpec(
            num_scalar_prefetch=1,
            grid=(T_ALL // TM_DMA,),
            in_specs=[pl.BlockSpec((TM_DMA, D_MODEL), lambda i, pos: (i, 0)),
                      pl.BlockSpec(memory_space=pl.ANY)],
            out_specs=pl.BlockSpec(memory_space=pl.ANY),
            scratch_shapes=[pltpu.SemaphoreType.DMA((TOP_K,))]),
        out_shape=jax.ShapeDtypeStruct((P_SLOTS, D_MODEL), F32),
        input_output_aliases={2: 0},
        compiler_params=_cparams(("arbitrary",)),
        name="moe_dispatch",
    )(pos_flat, h, xs_init)


def _expert_kernel(be_ref, xs_ref, wgu_ref, bgu_ref, wd_ref, bd_ref, ys_ref, wgu_scr, wd_scr):
    i = pl.program_id(0)
    n_used = be_ref[NB_PAD]
    changed = jnp.logical_or(i == 0, be_ref[i] != be_ref[jnp.maximum(i - 1, 0)])

    @pl.when(jnp.logical_and(changed, i < n_used))
    def _():
        wgu_scr[...] = wgu_ref[...].astype(BF16)
        wd_scr[...] = wd_ref[...].astype(BF16)

    @pl.when(i < n_used)
    def _():
        hgu = jnp.dot(xs_ref[...].astype(BF16), wgu_scr[...], preferred_element_type=F32) + bgu_ref[...]
        h_glu = jnp.minimum(hgu[:, :D_EXPERT], SWIGLU_LIMIT)
        h_lin = jnp.clip(hgu[:, D_EXPERT:], -SWIGLU_LIMIT, SWIGLU_LIMIT)
        act = (h_lin + 1.0) * (h_glu * _sigmoid(SWIGLU_ALPHA * h_glu))
        ys_ref[...] = jnp.dot(act.astype(BF16), wd_scr[...], preferred_element_type=F32) + bd_ref[...]

    @pl.when(i >= n_used)
    def _():
        ys_ref[...] = jnp.zeros_like(ys_ref)


def _experts(be_flat, xs, w_gu, b_gu, w_down, b_down):
    return pl.pallas_call(
        _expert_kernel,
        grid_spec=pltpu.PrefetchScalarGridSpec(
            num_scalar_prefetch=1,
            grid=(N_BLOCKS,),
            in_specs=[pl.BlockSpec((MOE_BLK, D_MODEL), lambda i, be: (i, 0)),
                      pl.BlockSpec((None, D_MODEL, 2 * D_EXPERT), lambda i, be: (be[i], 0, 0)),
                      pl.BlockSpec((None, 1, 2 * D_EXPERT), lambda i, be: (be[i], 0, 0)),
                      pl.BlockSpec((None, D_EXPERT, D_MODEL), lambda i, be: (be[i], 0, 0)),
                      pl.BlockSpec((None, 1, D_MODEL), lambda i, be: (be[i], 0, 0))],
            out_specs=pl.BlockSpec((MOE_BLK, D_MODEL), lambda i, be: (i, 0)),
            scratch_shapes=[pltpu.VMEM((D_MODEL, 2 * D_EXPERT), BF16),
                            pltpu.VMEM((D_EXPERT, D_MODEL), BF16)]),
        out_shape=jax.ShapeDtypeStruct((P_SLOTS, D_MODEL), F32),
        compiler_params=_cparams(("arbitrary",)),
        name="moe_experts",
    )(be_flat, xs, w_gu, b_gu, w_down, b_down)


def _combine_kernel(pos_ref, x_ref, mod_ref, gate_ref, ys_ref, o_ref, ybuf, sems):
    base = pl.program_id(0) * TM_DMA

    def row_copy(t, k):
        p = pos_ref[k * T_ALL + base + t]
        return pltpu.make_async_copy(ys_ref.at[pl.ds(p, 1), :], ybuf.at[k, pl.ds(t, 1), :], sems.at[k])

    def issue(t, carry):
        for k in range(TOP_K):
            row_copy(t, k).start()
        return carry

    lax.fori_loop(0, TM_DMA, issue, 0)

    def drain(t, carry):
        for k in range(TOP_K):
            row_copy(t, k).wait()
        return carry

    lax.fori_loop(0, TM_DMA, drain, 0)
    acc = gate_ref[:, 0:1] * ybuf[0]
    for k in range(1, TOP_K):
        acc = acc + gate_ref[:, k:k + 1] * ybuf[k]
    o_ref[...] = x_ref[...] + mod_ref[5:6, :] * acc


def _combine(pos_flat, x, mod, gate_t, ys):
    return pl.pallas_call(
        _combine_kernel,
        grid_spec=pltpu.PrefetchScalarGridSpec(
            num_scalar_prefetch=1,
            grid=(T_ALL // TM_DMA,),
            in_specs=[pl.BlockSpec((TM_DMA, D_MODEL), lambda i, pos: (i, 0)),
                      pl.BlockSpec((None, 6, D_MODEL), lambda i, pos: (_seg_of_tile(i, TM_DMA), 0, 0)),
                      pl.BlockSpec((TM_DMA, 8), lambda i, pos: (i, 0)),
                      pl.BlockSpec(memory_space=pl.ANY)],
            out_specs=pl.BlockSpec((TM_DMA, D_MODEL), lambda i, pos: (i, 0)),
            scratch_shapes=[pltpu.VMEM((TOP_K, TM_DMA, D_MODEL), F32),
                            pltpu.SemaphoreType.DMA((TOP_K,))]),
        out_shape=jax.ShapeDtypeStruct((T_ALL, D_MODEL), F32),
        compiler_params=_cparams(("arbitrary",)),
        name="moe_combine",
    )(pos_flat, x, mod, gate_t, ys)


def _moe(x, norm_g, mod, w_rt, b_r, w_gu, b_gu, w_down, b_down):
    h, idx, gate, rank, cnt = _router(x, norm_g, mod, w_rt, b_r)
    pos, be = _slots(idx, rank, cnt)
    pos_flat = pos[:TOP_K].reshape(TOP_K * T_ALL)
    be_flat = jnp.concatenate([be[0], be[1, :1]])
    xs = _dispatch(pos_flat, h, jnp.zeros((P_SLOTS, D_MODEL), F32))
    ys = _experts(be_flat, xs, w_gu, b_gu.reshape(N_EXPERTS, 1, 2 * D_EXPERT), w_down,
                  b_down.reshape(N_EXPERTS, 1, D_MODEL))
    return _combine(pos_flat, x, mod, gate.T, ys)


def _final_norm_kernel(x_ref, g_ref, o_ref):
    x = x_ref[...]
    o_ref[...] = x * lax.rsqrt(jnp.mean(x * x, axis=-1, keepdims=True) + RMS_EPS) * g_ref[...]


def _final_norm(x, g):
    return pl.pallas_call(
        _final_norm_kernel,
        grid=(T_ALL // TM_TOK,),
        in_specs=[pl.BlockSpec((TM_TOK, D_MODEL), lambda i: (i, 0)),
                  pl.BlockSpec((1, D_MODEL), lambda i: (0, 0))],
        out_specs=pl.BlockSpec((TM_TOK, D_MODEL), lambda i: (i, 0)),
        out_shape=jax.ShapeDtypeStruct((T_ALL, D_MODEL), F32),
        compiler_params=_cparams(("arbitrary",)),
        name="final_norm",
    )(x, g)


def _permute_in_cols(w):
    lead = w.shape[:-1]
    pad = jnp.zeros(lead + (N_Z - N_IN,), w.dtype)
    return jnp.concatenate([w[..., GATE_OFF + N_GATE_M:], w[..., :GATE_OFF],
                            w[..., GATE_OFF:GATE_OFF + N_GATE_M], pad], axis=-1)


def kernel(x_prompt, x_sample, cache_k, cache_v, state_C, state_n, state_m, c, c_ctx, norm1_g, w_mod, b_mod, w_in, b_in, rpb, w_dw, b_dw, cln_g, cln_b, mnorm_g, w_pa, w_pc, w_pm, w_out, norm2_g, w_router, b_router, w_gu, b_gu, w_down, b_down, final_g):
    cond = jnp.concatenate([c_ctx[None, :], c, jnp.zeros((SEG_PAD - N_SEG, D_MODEL), F32)], axis=0)
    mod_all = _modulation(cond, w_mod, b_mod).reshape(DEPTH, SEG_PAD, 6, D_MODEL)

    x = jnp.concatenate([x_prompt.reshape(T_CTX, D_MODEL), x_sample.reshape(T_LAT, D_MODEL)], axis=0)
    ck = cache_k.reshape(DEC_BATCH, DEPTH, PAST_LEN, D_A)
    cv = cache_v.reshape(DEC_BATCH, DEPTH, PAST_LEN, D_A)
    lat_c0 = state_C.reshape(DEC_BATCH, DEPTH, N_STREAM, HEAD_DIM_M, HEAD_DIM_M)
    lat_n0 = state_n.reshape(DEC_BATCH, DEPTH, N_STREAM, HEAD_DIM_M)
    lat_m0 = jnp.broadcast_to(state_m.reshape(DEC_BATCH, DEPTH, N_STREAM, 1),
                              (DEC_BATCH, DEPTH, N_STREAM, HEAD_DIM_M))
    ctx_c0 = jnp.zeros((1, N_STREAM, HEAD_DIM_M, HEAD_DIM_M), F32)
    ctx_n0 = jnp.zeros((1, N_STREAM, HEAD_DIM_M), F32)
    ctx_m0 = jnp.full((1, N_STREAM, HEAD_DIM_M), -jnp.inf, F32)

    ks, vs, cs, ns, ms = [], [], [], [], []
    for l in range(DEPTH):
        mod = mod_all[l]
        z = _in_proj(x, norm1_g[l][None, :], mod, _permute_in_cols(w_in[l]).astype(BF16),
                     _permute_in_cols(b_in[l])[None, :])
        ya = jnp.concatenate([_ctx_attention(z), _natt(z, ck, cv, _natt_bias(rpb[l]), l)], axis=0)
        conv_w = (w_dw[l], b_dw[l][None, :], cln_g[l][None, :], cln_b[l][None, :])
        yc = jnp.concatenate([_conv(z, *conv_w, SEQ, BATCH, 0),
                              _conv(z, *conv_w, DEC_SEQ, DEC_BATCH, T_CTX // DEC_SEQ)], axis=0)
        ng = mnorm_g[l][None, :]
        ym_ctx, c_l, n_l, m_l = _mlstm(z, ctx_c0, ctx_n0, ctx_m0, ng, SEQ, BATCH, 0, lambda b: (0,))
        ym_lat, _, _, _ = _mlstm(z, lat_c0, lat_n0, lat_m0, ng, DEC_SEQ, DEC_BATCH, T_CTX // DEC_SEQ,
                                 lambda b: (b, l))
        ym = jnp.concatenate([ym_ctx, ym_lat], axis=0)
        x = _merge(x, mod, ya, yc, ym, z, w_pa[l].astype(BF16), w_pc[l].astype(BF16),
                   w_pm[l].astype(BF16), w_out[l].astype(BF16))
        x = _moe(x, norm2_g[l][None, :], mod, w_router[l].T, b_router[l][:, None],
                 w_gu[l], b_gu[l], w_down[l], b_down[l])
        ks.append(z[:T_CTX, Z_KA_OFF:Z_KA_OFF + D_A].reshape(BATCH, SEQ, N_HEADS_A, HEAD_DIM_A))
        vs.append(z[:T_CTX, Z_VA_OFF:Z_VA_OFF + D_A].reshape(BATCH, SEQ, N_HEADS_A, HEAD_DIM_A))
        cs.append(c_l.reshape(BATCH, 2, N_HEADS_M, HEAD_DIM_M, HEAD_DIM_M))
        ns.append(n_l.reshape(BATCH, 2, N_HEADS_M, HEAD_DIM_M))
        ms.append(m_l[:, :, 0].reshape(BATCH, 2, N_HEADS_M))

    y = _final_norm(x, final_g[None, :])
    return (y[:T_CTX].reshape(BATCH, SEQ, D_MODEL), y[T_CTX:].reshape(DEC_BATCH, DEC_SEQ, D_MODEL),
            jnp.stack(ks, axis=1), jnp.stack(vs, axis=1), jnp.stack(cs, axis=1),
            jnp.stack(ns, axis=1), jnp.stack(ms, axis=1))
```

```python
import functools

import numpy as np
import jax
import jax.numpy as jnp
from jax import lax
from jax.experimental import pallas as pl
from jax.experimental.pallas import tpu as pltpu

F32 = jnp.float32
BF16 = jnp.bfloat16
HIGHEST = lax.Precision.HIGHEST

D_MODEL = 1024
BATCH = 16
SEQ = 256
DEPTH = 2
DEC_BATCH = 8
DEC_SEQ = 1024
PAST_LEN = 512
GRID_W = 64
N_HEADS_A = 8
HEAD_DIM_A = 64
D_A = N_HEADS_A * HEAD_DIM_A
WIN_ROWS = 8
WIN_COLS = 16
D_CONV = 512
CONV_WIDTH = 31
N_HEADS_M = 4
HEAD_DIM_M = 128
D_M = N_HEADS_M * HEAD_DIM_M
N_GATE_M = 4 * N_HEADS_M
CHUNK = 64
N_EXPERTS = 32
TOP_K = 4
D_EXPERT = 1024
SWIGLU_ALPHA = 1.702
SWIGLU_LIMIT = 7.0
RMS_EPS = 1e-6
LN_EPS = 1e-5
GATE_OFF = 3 * D_A + 2 * D_CONV + 4 * D_M
N_IN = GATE_OFF + N_GATE_M + 3 * D_MODEL

T_CTX = BATCH * SEQ
T_LAT = DEC_BATCH * DEC_SEQ
T_ALL = T_CTX + T_LAT
N_SEG = 1 + DEC_BATCH
SEG_PAD = 16
GRID_ROWS = DEC_SEQ // GRID_W
NEG = -1e30

N_Z = 8192
ZB_GA, ZB_GC, ZB_GM = 0, 1, 2
ZB_QA, ZB_KA, ZB_VA = 6, 7, 8
ZB_CU, ZB_CG = 9, 10
ZB_QM, ZB_KM, ZB_VM, ZB_OM = 11, 12, 13, 14
ZB_GATES = 60
Z_KA_OFF = ZB_KA * 512
Z_VA_OFF = ZB_VA * 512

TM_TOK = 512
MOE_BLK = 256
M_ASSIGN = T_ALL * TOP_K
N_BLOCKS = M_ASSIGN // MOE_BLK + N_EXPERTS
P_SLOTS = N_BLOCKS * MOE_BLK
NB_PAD = 256
TM_DMA = 256
VMEM_LIMIT = 56 * 1024 * 1024


def _cparams(sem=None):
    return pltpu.CompilerParams(dimension_semantics=sem, vmem_limit_bytes=VMEM_LIMIT)


def _seg_of_tile(i, tile):
    n_ctx = T_CTX // tile
    per_lat = DEC_SEQ // tile
    return jnp.where(i < n_ctx, 0, 1 + (i - n_ctx) // per_lat)


def _dot_nt(a, b):
    return lax.dot_general(a, b, (((1,), (1,)), ((), ())), preferred_element_type=F32)


def _sigmoid(x):
    return 1.0 / (1.0 + jnp.exp(-x))


def _mod_kernel(c_ref, w_ref, b_ref, o_ref):
    c = c_ref[...]
    s = c * _sigmoid(c)
    o_ref[...] = jnp.dot(s, w_ref[...], precision=HIGHEST, preferred_element_type=F32) + b_ref[...]


def _modulation(cond, w_mod, b_mod):
    tn = 1536
    return pl.pallas_call(
        _mod_kernel,
        grid=(DEPTH, 6 * D_MODEL // tn),
        in_specs=[pl.BlockSpec((SEG_PAD, D_MODEL), lambda l, j: (0, 0)),
                  pl.BlockSpec((None, D_MODEL, tn), lambda l, j: (l, 0, j)),
                  pl.BlockSpec((None, 1, tn), lambda l, j: (l, 0, j))],
        out_specs=pl.BlockSpec((None, SEG_PAD, tn), lambda l, j: (l, 0, j)),
        out_shape=jax.ShapeDtypeStruct((DEPTH, SEG_PAD, 6 * D_MODEL), F32),
        compiler_params=_cparams(("arbitrary", "arbitrary")),
        name="modulation",
    )(cond, w_mod, b_mod.reshape(DEPTH, 1, 6 * D_MODEL))


def _normmod(x, g, mod, shift_idx, scale_idx):
    y = x * lax.rsqrt(jnp.mean(x * x, axis=-1, keepdims=True) + RMS_EPS) * g
    return y * (1.0 + mod[scale_idx:scale_idx + 1, :]) + mod[shift_idx:shift_idx + 1, :]


def _in_proj_kernel(x_ref, g_ref, mod_ref, w_ref, b_ref, o_ref, h_scr):
    @pl.when(pl.program_id(1) == 0)
    def _():
        h_scr[...] = _normmod(x_ref[...], g_ref[...], mod_ref[...], 0, 1).astype(BF16)

    o_ref[...] = jnp.dot(h_scr[...], w_ref[...], preferred_element_type=F32) + b_ref[...]


def _in_proj(x, norm_g, mod, w, b):
    tn = 1024
    return pl.pallas_call(
        _in_proj_kernel,
        grid=(T_ALL // TM_TOK, N_Z // tn),
        in_specs=[pl.BlockSpec((TM_TOK, D_MODEL), lambda i, j: (i, 0)),
                  pl.BlockSpec((1, D_MODEL), lambda i, j: (0, 0)),
                  pl.BlockSpec((None, 6, D_MODEL), lambda i, j: (_seg_of_tile(i, TM_TOK), 0, 0)),
                  pl.BlockSpec((D_MODEL, tn), lambda i, j: (0, j)),
                  pl.BlockSpec((1, tn), lambda i, j: (0, j))],
        out_specs=pl.BlockSpec((TM_TOK, tn), lambda i, j: (i, j)),
        out_shape=jax.ShapeDtypeStruct((T_ALL, N_Z), F32),
        scratch_shapes=[pltpu.VMEM((TM_TOK, D_MODEL), BF16)],
        compiler_params=_cparams(("arbitrary", "arbitrary")),
        name="in_proj",
    )(x, norm_g, mod, w, b)


def _ctx_attn_kernel(q_ref, k_ref, v_ref, o_ref):
    scale = HEAD_DIM_A ** -0.5
    for h in range(N_HEADS_A):
        sl = slice(h * HEAD_DIM_A, (h + 1) * HEAD_DIM_A)
        q = q_ref[:, sl].astype(BF16)
        k = k_ref[:, sl].astype(BF16)
        v = v_ref[:, sl].astype(BF16)
        s = _dot_nt(q, k) * scale
        p = jnp.exp(s - jnp.max(s, axis=-1, keepdims=True))
        l = jnp.sum(p, axis=-1, keepdims=True)
        o_ref[:, sl] = jnp.dot(p.astype(BF16), v, preferred_element_type=F32) / l


def _ctx_attention(z):
    def spec(cb):
        return pl.BlockSpec((SEQ, D_A), lambda b: (b, cb))

    return pl.pallas_call(
        _ctx_attn_kernel,
        grid=(BATCH,),
        in_specs=[spec(ZB_QA), spec(ZB_KA), spec(ZB_VA)],
        out_specs=pl.BlockSpec((SEQ, D_A), lambda b: (b, 0)),
        out_shape=jax.ShapeDtypeStruct((T_CTX, D_A), F32),
        compiler_params=_cparams(("arbitrary",)),
        name="ctx_attention",
    )(z, z, z)


def _natt_kernel(q_ref, k_ref, v_ref, kc_ref, vc_ref, bias_ref, o_ref):
    scale = HEAD_DIM_A ** -0.5
    r = pl.program_id(1)
    rs = jnp.clip(r - WIN_ROWS // 2, 0, GRID_ROWS - WIN_ROWS)
    start = pl.multiple_of(rs * GRID_W, GRID_W)
    band = WIN_ROWS * GRID_W
    for h in range(N_HEADS_A):
        sl = slice(h * HEAD_DIM_A, (h + 1) * HEAD_DIM_A)
        q = q_ref[:, sl].astype(BF16)
        kb = k_ref[pl.ds(start, band), sl].astype(BF16)
        vb = v_ref[pl.ds(start, band), sl].astype(BF16)
        kc = kc_ref[:, sl].astype(BF16)
        vc = vc_ref[:, sl].astype(BF16)
        s_loc = _dot_nt(q, kb) * scale + bias_ref[h]
        s_ctx = _dot_nt(q, kc) * scale
        m = jnp.maximum(jnp.max(s_loc, axis=-1, keepdims=True), jnp.max(s_ctx, axis=-1, keepdims=True))
        p_loc = jnp.exp(s_loc - m)
        p_ctx = jnp.exp(s_ctx - m)
        l = jnp.sum(p_loc, axis=-1, keepdims=True) + jnp.sum(p_ctx, axis=-1, keepdims=True)
        o = (jnp.dot(p_loc.astype(BF16), vb, preferred_element_type=F32)
             + jnp.dot(p_ctx.astype(BF16), vc, preferred_element_type=F32))
        o_ref[:, sl] = o / l


def _band_row_offset(r):
    rs = jnp.clip(r - WIN_ROWS // 2, 0, GRID_ROWS - WIN_ROWS)
    return rs - r + WIN_ROWS - 1


def _natt_bias(rpb_l):
    qc = np.arange(GRID_W)
    kc = np.arange(GRID_W)
    cs = np.clip(qc - WIN_COLS // 2, 0, GRID_W - WIN_COLS)
    ok = (kc[None, :] >= cs[:, None]) & (kc[None, :] < cs[:, None] + WIN_COLS)
    dc = np.clip(kc[None, :] - qc[:, None] + WIN_COLS - 1, 0, 2 * WIN_COLS - 2)
    pick = (dc[None] == np.arange(2 * WIN_COLS - 1)[:, None, None]).astype(np.float32)
    toe = jnp.einsum('hdc,cqk->hdqk', rpb_l, jnp.asarray(pick), precision=HIGHEST)
    toe = jnp.where(jnp.asarray(ok)[None, None], toe, NEG)
    bands = [toe[:, d0:d0 + WIN_ROWS].transpose(0, 2, 1, 3).reshape(N_HEADS_A, GRID_W, WIN_ROWS * GRID_W)
             for d0 in range(WIN_ROWS)]
    return jnp.stack(bands, axis=0)


def _natt(z, cache_k, cache_v, bias, layer):
    lat0 = T_CTX // DEC_SEQ
    row0 = T_CTX // GRID_W
    return pl.pallas_call(
        _natt_kernel,
        grid=(DEC_BATCH, GRID_ROWS),
        in_specs=[pl.BlockSpec((GRID_W, D_A), lambda b, r: (row0 + b * GRID_ROWS + r, ZB_QA)),
                  pl.BlockSpec((DEC_SEQ, D_A), lambda b, r: (lat0 + b, ZB_KA)),
                  pl.BlockSpec((DEC_SEQ, D_A), lambda b, r: (lat0 + b, ZB_VA)),
                  pl.BlockSpec((None, None, PAST_LEN, D_A), lambda b, r: (b, layer, 0, 0)),
                  pl.BlockSpec((None, None, PAST_LEN, D_A), lambda b, r: (b, layer, 0, 0)),
                  pl.BlockSpec((None, N_HEADS_A, GRID_W, WIN_ROWS * GRID_W),
                               lambda b, r: (_band_row_offset(r), 0, 0, 0))],
        out_specs=pl.BlockSpec((GRID_W, D_A), lambda b, r: (b * GRID_ROWS + r, 0)),
        out_shape=jax.ShapeDtypeStruct((T_LAT, D_A), F32),
        compiler_params=_cparams(("arbitrary", "arbitrary")),
        name="nbr_attention",
    )(z, z, z, cache_k, cache_v, bias)


CONV_HALO = 16
CONV_ROWS = 64


def _conv_kernel(u_ref, g_ref, w_ref, b_ref, lg_ref, lb_ref, o_ref, pad_scr, *, seq):
    zeros = jnp.zeros((CONV_HALO, D_CONV), F32)
    pad_scr[0:CONV_HALO, :] = zeros
    pad_scr[CONV_HALO + seq:2 * CONV_HALO + seq, :] = zeros
    pad_scr[CONV_HALO:CONV_HALO + seq, :] = u_ref[...] * _sigmoid(g_ref[...])
    first = CONV_HALO - CONV_WIDTH // 2
    for c in range(seq // CONV_ROWS):
        base = c * CONV_ROWS
        acc = jnp.broadcast_to(b_ref[...], (CONV_ROWS, D_CONV))
        for j in range(CONV_WIDTH):
            acc = acc + pad_scr[base + first + j:base + first + j + CONV_ROWS, :] * w_ref[j:j + 1, :]
        mu = jnp.mean(acc, axis=-1, keepdims=True)
        xc = acc - mu
        var = jnp.mean(xc * xc, axis=-1, keepdims=True)
        y = xc * lax.rsqrt(var + LN_EPS) * lg_ref[...] + lb_ref[...]
        o_ref[base:base + CONV_ROWS, :] = y * _sigmoid(y)


def _conv(z, w_dw, b_dw, ln_g, ln_b, seq, n_seq, row_block0):
    def vec():
        return pl.BlockSpec((1, D_CONV), lambda b: (0, 0))

    return pl.pallas_call(
        functools.partial(_conv_kernel, seq=seq),
        grid=(n_seq,),
        in_specs=[pl.BlockSpec((seq, D_CONV), lambda b: (row_block0 + b, ZB_CU)),
                  pl.BlockSpec((seq, D_CONV), lambda b: (row_block0 + b, ZB_CG)),
                  pl.BlockSpec((CONV_WIDTH, D_CONV), lambda b: (0, 0)),
                  vec(), vec(), vec()],
        out_specs=pl.BlockSpec((seq, D_CONV), lambda b: (b, 0)),
        out_shape=jax.ShapeDtypeStruct((n_seq * seq, D_CONV), F32),
        scratch_shapes=[pltpu.VMEM((seq + 2 * CONV_HALO, D_CONV), F32)],
        compiler_params=_cparams(("arbitrary",)),
        name="conformer_conv",
    )(z, z, w_dw, b_dw, ln_g, ln_b)


N_STREAM = 2 * N_HEADS_M


def _mlstm_kernel(q_ref, k_ref, v_ref, om_ref, gt_ref, c0_ref, n0_ref, m0_ref, ng_ref,
                  y_ref, c_out, n_out, m_out, hf_scr, hb_scr, c_scr, n_scr, m_scr, *, seq):
    nc = seq // CHUNK
    c_scr[...] = c0_ref[...]
    n_scr[...] = n0_ref[...]
    m_scr[...] = m0_ref[...]
    ti = lax.broadcasted_iota(jnp.int32, (CHUNK, CHUNK), 0)
    si = lax.broadcasted_iota(jnp.int32, (CHUNK, CHUNK), 1)
    masks = (si <= ti, si >= ti)
    tris = tuple(mk.astype(F32) for mk in masks)
    kscale = HEAD_DIM_M ** -0.5

    def chunk_step(c, carry):
        for d in range(2):
            cidx = c if d == 0 else nc - 1 - c
            off = pl.multiple_of(cidx * CHUNK, CHUNK)
            last = CHUNK - 1 if d == 0 else 0
            g = gt_ref[pl.ds(off, CHUNK), :]
            lf = jnp.minimum(g, 0.0) - jnp.log(1.0 + jnp.exp(-jnp.abs(g)))
            cum = jnp.dot(tris[d], lf, precision=HIGHEST, preferred_element_type=F32)
            g_t = g.T
            cum_t = cum.T
            for h in range(N_HEADS_M):
                s_id = N_HEADS_M * d + h
                icol = 2 * N_HEADS_M * d + h
                fcol = icol + N_HEADS_M
                hs = slice(h * HEAD_DIM_M, (h + 1) * HEAD_DIM_M)
                b_col = cum[:, fcol:fcol + 1]
                i_col = g[:, icol:icol + 1]
                b_row = cum_t[fcol:fcol + 1, :]
                i_row = g_t[icol:icol + 1, :]
                b_last = b_col[last:last + 1, :]
                m_prev = m_scr[s_id:s_id + 1, 0:1]
                log_d = jnp.where(masks[d], b_col - b_row + i_row, NEG)
                inter = b_col + m_prev
                m_t = jnp.maximum(inter, jnp.max(log_d, axis=-1, keepdims=True))
                w_inter = jnp.exp(inter - m_t)
                qc = q_ref[pl.ds(off, CHUNK), hs]
                kc = k_ref[pl.ds(off, CHUNK), hs] * kscale
                vc = v_ref[pl.ds(off, CHUNK), hs]
                qb = qc.astype(BF16)
                kb = kc.astype(BF16)
                c_prev = c_scr[s_id]
                n_prev = n_scr[s_id:s_id + 1, :]
                s_mat = _dot_nt(qb, kb) * jnp.exp(log_d - m_t)
                num = (w_inter * _dot_nt(qb, c_prev.astype(BF16))
                       + jnp.dot(s_mat.astype(BF16), vc.astype(BF16), preferred_element_type=F32))
                den = (w_inter * jnp.sum(qc * n_prev, axis=-1, keepdims=True)
                       + jnp.sum(s_mat, axis=-1, keepdims=True))
                hh = num / jnp.maximum(jnp.abs(den), jnp.exp(-m_t))
                if d == 0:
                    hf_scr[pl.ds(off, CHUNK), hs] = hh
                else:
                    hb_scr[pl.ds(off, CHUNK), hs] = hh
                m_new = m_t[last:last + 1, :]
                w_prev = jnp.exp(b_last + m_prev - m_new)
                w_src = jnp.exp(b_last - b_col + i_col - m_new)
                upd = jnp.dot((w_src * vc).T.astype(BF16), kb, preferred_element_type=F32)
                c_scr[s_id] = w_prev * c_prev + upd
                n_scr[s_id:s_id + 1, :] = w_prev * n_prev + jnp.sum(w_src * kc, axis=0, keepdims=True)
                m_scr[s_id:s_id + 1, :] = jnp.broadcast_to(m_new, (1, HEAD_DIM_M))
        return carry

    lax.fori_loop(0, nc, chunk_step, 0)

    for h in range(N_HEADS_M):
        hs = slice(h * HEAD_DIM_M, (h + 1) * HEAD_DIM_M)
        hsum = hf_scr[:, hs] + hb_scr[:, hs]
        mu = jnp.mean(hsum, axis=-1, keepdims=True)
        xc = hsum - mu
        var = jnp.mean(xc * xc, axis=-1, keepdims=True)
        hn = xc * lax.rsqrt(var + LN_EPS) * ng_ref[:, hs]
        y_ref[:, hs] = _sigmoid(om_ref[:, hs]) * hn
    c_out[...] = c_scr[...]
    n_out[...] = n_scr[...]
    m_out[...] = m_scr[...]


def _mlstm(z, c0, n0, m0, norm_g, seq, n_seq, row_block0, state_map):
    lead = len(state_map(0))

    def zspec(cb):
        return pl.BlockSpec((seq, D_M), lambda b: (row_block0 + b, cb))

    def sspec(tail):
        return pl.BlockSpec((None,) * lead + tail, lambda b: state_map(b) + (0,) * len(tail))

    return pl.pallas_call(
        functools.partial(_mlstm_kernel, seq=seq),
        grid=(n_seq,),
        in_specs=[zspec(ZB_QM), zspec(ZB_KM), zspec(ZB_VM), zspec(ZB_OM),
                  pl.BlockSpec((seq, 128), lambda b: (row_block0 + b, ZB_GATES)),
                  sspec((N_STREAM, HEAD_DIM_M, HEAD_DIM_M)),
                  sspec((N_STREAM, HEAD_DIM_M)),
                  sspec((N_STREAM, HEAD_DIM_M)),
                  pl.BlockSpec((1, D_M), lambda b: (0, 0))],
        out_specs=[pl.BlockSpec((seq, D_M), lambda b: (b, 0)),
                   pl.BlockSpec((None, N_STREAM, HEAD_DIM_M, HEAD_DIM_M), lambda b: (b, 0, 0, 0)),
                   pl.BlockSpec((None, N_STREAM, HEAD_DIM_M), lambda b: (b, 0, 0)),
                   pl.BlockSpec((None, N_STREAM, HEAD_DIM_M), lambda b: (b, 0, 0))],
        out_shape=[jax.ShapeDtypeStruct((n_seq * seq, D_M), F32),
                   jax.ShapeDtypeStruct((n_seq, N_STREAM, HEAD_DIM_M, HEAD_DIM_M), F32),
                   jax.ShapeDtypeStruct((n_seq, N_STREAM, HEAD_DIM_M), F32),
                   jax.ShapeDtypeStruct((n_seq, N_STREAM, HEAD_DIM_M), F32)],
        scratch_shapes=[pltpu.VMEM((seq, D_M), F32), pltpu.VMEM((seq, D_M), F32),
                        pltpu.VMEM((N_STREAM, HEAD_DIM_M, HEAD_DIM_M), F32),
                        pltpu.VMEM((N_STREAM, HEAD_DIM_M), F32),
                        pltpu.VMEM((N_STREAM, HEAD_DIM_M), F32)],
        compiler_params=_cparams(("arbitrary",)),
        name="mlstm",
    )(z, z, z, z, z, c0, n0, m0, norm_g)


def _merge_kernel(x_ref, mod_ref, ya_ref, yc_ref, ym_ref, ga_ref, gc_ref, gm_ref,
                  wa_ref, wc_ref, wm_ref, wo_ref, o_ref):
    def branch(y_ref, g_ref, w_ref):
        return _sigmoid(g_ref[...]) * jnp.dot(y_ref[...].astype(BF16), w_ref[...], preferred_element_type=F32)

    merged = branch(ya_ref, ga_ref, wa_ref) + branch(yc_ref, gc_ref, wc_ref) + branch(ym_ref, gm_ref, wm_ref)
    mix = jnp.dot(merged.astype(BF16), wo_ref[...], preferred_element_type=F32)
    o_ref[...] = x_ref[...] + mod_ref[2:3, :] * mix


def _merge(x, mod, ya, yc, ym, z, w_pa, w_pc, w_pm, w_out):
    def rows(width, cb=0):
        return pl.BlockSpec((TM_TOK, width), lambda i: (i, cb))

    def full(shape):
        return pl.BlockSpec(shape, lambda i: (0, 0))

    return pl.pallas_call(
        _merge_kernel,
        grid=(T_ALL // TM_TOK,),
        in_specs=[rows(D_MODEL),
                  pl.BlockSpec((None, 6, D_MODEL), lambda i: (_seg_of_tile(i, TM_TOK), 0, 0)),
                  rows(D_A), rows(D_CONV), rows(D_M),
                  rows(D_MODEL, ZB_GA), rows(D_MODEL, ZB_GC), rows(D_MODEL, ZB_GM),
                  full((D_A, D_MODEL)), full((D_CONV, D_MODEL)), full((D_M, D_MODEL)),
                  full((D_MODEL, D_MODEL))],
        out_specs=rows(D_MODEL),
        out_shape=jax.ShapeDtypeStruct((T_ALL, D_MODEL), F32),
        compiler_params=_cparams(("arbitrary",)),
        name="merge",
    )(x, mod, ya, yc, ym, z, z, z, w_pa, w_pc, w_pm, w_out)


def _router_kernel(x_ref, g_ref, mod_ref, wr_ref, br_ref, h_ref, idx_ref, gate_ref, rank_ref, cnt_ref, cnt_scr):
    i = pl.program_id(0)

    @pl.when(i == 0)
    def _():
        cnt_scr[...] = jnp.zeros_like(cnt_scr)

    h = _normmod(x_ref[...], g_ref[...], mod_ref[...], 3, 4)
    h_ref[...] = h
    logits = lax.dot_general(wr_ref[...], h, (((1,), (1,)), ((), ())), precision=HIGHEST,
                             preferred_element_type=F32) + br_ref[...]
    e_iota = lax.broadcasted_iota(jnp.int32, (N_EXPERTS, TM_TOK), 0).astype(F32)
    sels, vals = [], []
    l = logits
    for k in range(TOP_K):
        m = jnp.max(l, axis=0, keepdims=True)
        idx = jnp.min(jnp.where(l == m, e_iota, float(N_EXPERTS)), axis=0, keepdims=True)
        sel = e_iota == idx
        idx_ref[k:k + 1, :] = idx.astype(jnp.int32)
        vals.append(m)
        sels.append(sel)
        l = jnp.where(sel, -jnp.inf, l)
    exps = [jnp.exp(v - vals[0]) for v in vals]
    tot = exps[0] + exps[1] + exps[2] + exps[3]
    onehot = jnp.zeros((N_EXPERTS, TM_TOK), F32)
    for k in range(TOP_K):
        gate_ref[k:k + 1, :] = exps[k] / tot
        onehot = onehot + sels[k].astype(F32)
    t_src = lax.broadcasted_iota(jnp.int32, (TM_TOK, TM_TOK), 0)
    t_dst = lax.broadcasted_iota(jnp.int32, (TM_TOK, TM_TOK), 1)
    before = (t_src < t_dst).astype(BF16)
    carry = cnt_scr[:, 0:1]
    prefix = jnp.dot(onehot.astype(BF16), before, preferred_element_type=F32) + carry
    for k in range(TOP_K):
        rk = jnp.sum(jnp.where(sels[k], prefix, 0.0), axis=0, keepdims=True)
        rank_ref[k:k + 1, :] = rk.astype(jnp.int32)
    pad = jnp.zeros((8 - TOP_K, TM_TOK), jnp.int32)
    idx_ref[TOP_K:8, :] = pad
    rank_ref[TOP_K:8, :] = pad
    gate_ref[TOP_K:8, :] = pad.astype(F32)
    total = carry + jnp.sum(onehot, axis=1, keepdims=True)
    cnt_scr[...] = jnp.broadcast_to(total, cnt_scr.shape)
    cnt_ref[...] = cnt_scr[...]


def _router(x, norm_g, mod, w_rt, b_r):
    def tok(dtype):
        return jax.ShapeDtypeStruct((8, T_ALL), dtype)

    tspec = pl.BlockSpec((8, TM_TOK), lambda i: (0, i))
    return pl.pallas_call(
        _router_kernel,
        grid=(T_ALL // TM_TOK,),
        in_specs=[pl.BlockSpec((TM_TOK, D_MODEL), lambda i: (i, 0)),
                  pl.BlockSpec((1, D_MODEL), lambda i: (0, 0)),
                  pl.BlockSpec((None, 6, D_MODEL), lambda i: (_seg_of_tile(i, TM_TOK), 0, 0)),
                  pl.BlockSpec((N_EXPERTS, D_MODEL), lambda i: (0, 0)),
                  pl.BlockSpec((N_EXPERTS, 1), lambda i: (0, 0))],
        out_specs=[pl.BlockSpec((TM_TOK, D_MODEL), lambda i: (i, 0)), tspec, tspec, tspec,
                   pl.BlockSpec((N_EXPERTS, 128), lambda i: (0, 0))],
        out_shape=[jax.ShapeDtypeStruct((T_ALL, D_MODEL), F32), tok(jnp.int32), tok(F32), tok(jnp.int32),
                   jax.ShapeDtypeStruct((N_EXPERTS, 128), F32)],
        scratch_shapes=[pltpu.VMEM((N_EXPERTS, 128), F32)],
        compiler_params=_cparams(("arbitrary",)),
        name="moe_router",
    )(x, norm_g, mod, w_rt, b_r)


def _slots_kernel(idx_ref, rank_ref, cnt_ref, pos_ref, be_ref):
    cnt = cnt_ref[:, 0:1]
    padded = jnp.floor((cnt + (MOE_BLK - 1)) / MOE_BLK) * MOE_BLK
    ei = lax.broadcasted_iota(jnp.int32, (N_EXPERTS, N_EXPERTS), 0)
    ej = lax.broadcasted_iota(jnp.int32, (N_EXPERTS, N_EXPERTS), 1)
    lower = (ej < ei).astype(F32)
    pstart = jnp.dot(lower, jnp.broadcast_to(padded, (N_EXPERTS, 128)), precision=HIGHEST,
                     preferred_element_type=F32)[:, 0:1]
    pend = pstart + padded
    e_iota = lax.broadcasted_iota(jnp.int32, (N_EXPERTS, T_ALL), 0)
    for k in range(TOP_K):
        start_k = jnp.sum(jnp.where(e_iota == idx_ref[k:k + 1, :], pstart, 0.0), axis=0, keepdims=True)
        pos_ref[k:k + 1, :] = start_k.astype(jnp.int32) + rank_ref[k:k + 1, :]
    pos_ref[TOP_K:8, :] = jnp.zeros((8 - TOP_K, T_ALL), jnp.int32)
    row0 = (lax.broadcasted_iota(jnp.int32, (N_EXPERTS, NB_PAD), 1) * MOE_BLK).astype(F32)
    n_done = jnp.sum((pend <= row0).astype(jnp.int32), axis=0, keepdims=True)
    n_used = (pend[N_EXPERTS - 1:N_EXPERTS, :] / MOE_BLK).astype(jnp.int32)
    be_ref[0:1, :] = jnp.minimum(n_done, N_EXPERTS - 1)
    be_ref[1:2, :] = jnp.broadcast_to(n_used, (1, NB_PAD))
    be_ref[2:8, :] = jnp.zeros((6, NB_PAD), jnp.int32)


def _slots(idx, rank, cnt):
    return pl.pallas_call(
        _slots_kernel,
        out_shape=[jax.ShapeDtypeStruct((8, T_ALL), jnp.int32),
                   jax.ShapeDtypeStruct((8, NB_PAD), jnp.int32)],
        compiler_params=_cparams(),
        name="moe_slots",
    )(idx, rank, cnt)


def _dispatch_kernel(pos_ref, h_ref, xs_in, xs_ref, sems):
    del xs_in
    base = pl.program_id(0) * TM_DMA

    def row_copy(t, k):
        p = pos_ref[k * T_ALL + base + t]
        return pltpu.make_async_copy(h_ref.at[pl.ds(t, 1), :], xs_ref.at[pl.ds(p, 1), :], sems.at[k])

    def issue(t, carry):
        for k in range(TOP_K):
            row_copy(t, k).start()
        return carry

    lax.fori_loop(0, TM_DMA, issue, 0)

    def drain(t, carry):
        for k in range(TOP_K):
            row_copy(t, k).wait()
        return carry

    lax.fori_loop(0, TM_DMA, drain, 0)


def _dispatch(pos_flat, h, xs_init):
    return pl.pallas_call(
        _dispatch_kernel,
        grid_spec=pltpu.PrefetchScalarGridSpec(
            num_scalar_prefetch=1,
            grid=(T_ALL // TM_DMA,),
            in_specs=[pl.BlockSpec((TM_DMA, D_MODEL), lambda i, pos: (i, 0)),
                      pl.BlockSpec(memory_space=pl.ANY)],
            out_specs=pl.BlockSpec(memory_space=pl.ANY),
            scratch_shapes=[pltpu.SemaphoreType.DMA((TOP_K,))]),
        out_shape=jax.ShapeDtypeStruct((P_SLOTS, D_MODEL), F32),
        input_output_aliases={2: 0},
        compiler_params=_cparams(("arbitrary",)),
        name="moe_dispatch",
    )(pos_flat, h, xs_init)


def _expert_kernel(be_ref, xs_ref, wgu_ref, bgu_ref, wd_ref, bd_ref, ys_ref, wgu_scr, wd_scr):
    i = pl.program_id(0)
    n_used = be_ref[NB_PAD]
    changed = jnp.logical_or(i == 0, be_ref[i] != be_ref[jnp.maximum(i - 1, 0)])

    @pl.when(jnp.logical_and(changed, i < n_used))
    def _():
        wgu_scr[...] = wgu_ref[...].astype(BF16)
        wd_scr[...] = wd_ref[...].astype(BF16)

    @pl.when(i < n_used)
    def _():
        hgu = jnp.dot(xs_ref[...].astype(BF16), wgu_scr[...], preferred_element_type=F32) + bgu_ref[...]
        h_glu = jnp.minimum(hgu[:, :D_EXPERT], SWIGLU_LIMIT)
        h_lin = jnp.clip(hgu[:, D_EXPERT:], -SWIGLU_LIMIT, SWIGLU_LIMIT)
        act = (h_lin + 1.0) * (h_glu * _sigmoid(SWIGLU_ALPHA * h_glu))
        ys_ref[...] = jnp.dot(act.astype(BF16), wd_scr[...], preferred_element_type=F32) + bd_ref[...]

    @pl.when(i >= n_used)
    def _():
        ys_ref[...] = jnp.zeros_like(ys_ref)


def _experts(be_flat, xs, w_gu, b_gu, w_down, b_down, layer):
    return pl.pallas_call(
        _expert_kernel,
        grid_spec=pltpu.PrefetchScalarGridSpec(
            num_scalar_prefetch=1,
            grid=(N_BLOCKS,),
            in_specs=[pl.BlockSpec((MOE_BLK, D_MODEL), lambda i, be: (i, 0)),
                      pl.BlockSpec((None, None, D_MODEL, 2 * D_EXPERT), lambda i, be: (layer, be[i], 0, 0)),
                      pl.BlockSpec((None, None, 1, 2 * D_EXPERT), lambda i, be: (layer, be[i], 0, 0)),
                      pl.BlockSpec((None, None, D_EXPERT, D_MODEL), lambda i, be: (layer, be[i], 0, 0)),
                      pl.BlockSpec((None, None, 1, D_MODEL), lambda i, be: (layer, be[i], 0, 0))],
            out_specs=pl.BlockSpec((MOE_BLK, D_MODEL), lambda i, be: (i, 0)),
            scratch_shapes=[pltpu.VMEM((D_MODEL, 2 * D_EXPERT), BF16),
                            pltpu.VMEM((D_EXPERT, D_MODEL), BF16)]),
        out_shape=jax.ShapeDtypeStruct((P_SLOTS, D_MODEL), F32),
        compiler_params=_cparams(("arbitrary",)),
        name="moe_experts",
    )(be_flat, xs, w_gu, b_gu, w_down, b_down)


def _combine_kernel(pos_ref, x_ref, mod_ref, gate_ref, ys_ref, o_ref, ybuf, sems):
    base = pl.program_id(0) * TM_DMA

    def row_copy(t, k):
        p = pos_ref[k * T_ALL + base + t]
        return pltpu.make_async_copy(ys_ref.at[pl.ds(p, 1), :], ybuf.at[k, pl.ds(t, 1), :], sems.at[k])

    def issue(t, carry):
        for k in range(TOP_K):
            row_copy(t, k).start()
        return carry

    lax.fori_loop(0, TM_DMA, issue, 0)

    def drain(t, carry):
        for k in range(TOP_K):
            row_copy(t, k).wait()
        return carry

    lax.fori_loop(0, TM_DMA, drain, 0)
    acc = gate_ref[:, 0:1] * ybuf[0]
    for k in range(1, TOP_K):
        acc = acc + gate_ref[:, k:k + 1] * ybuf[k]
    o_ref[...] = x_ref[...] + mod_ref[5:6, :] * acc


def _combine(pos_flat, x, mod, gate_t, ys):
    return pl.pallas_call(
        _combine_kernel,
        grid_spec=pltpu.PrefetchScalarGridSpec(
            num_scalar_prefetch=1,
            grid=(T_ALL // TM_DMA,),
            in_specs=[pl.BlockSpec((TM_DMA, D_MODEL), lambda i, pos: (i, 0)),
                      pl.BlockSpec((None, 6, D_MODEL), lambda i, pos: (_seg_of_tile(i, TM_DMA), 0, 0)),
                      pl.BlockSpec((TM_DMA, 8), lambda i, pos: (i, 0)),
                      pl.BlockSpec(memory_space=pl.ANY)],
            out_specs=pl.BlockSpec((TM_DMA, D_MODEL), lambda i, pos: (i, 0)),
            scratch_shapes=[pltpu.VMEM((TOP_K, TM_DMA, D_MODEL), F32),
                            pltpu.SemaphoreType.DMA((TOP_K,))]),
        out_shape=jax.ShapeDtypeStruct((T_ALL, D_MODEL), F32),
        compiler_params=_cparams(("arbitrary",)),
        name="moe_combine",
    )(pos_flat, x, mod, gate_t, ys)


def _moe(x, norm_g, mod, w_rt, b_r, w_gu, b_gu, w_down, b_down, layer):
    h, idx, gate, rank, cnt = _router(x, norm_g, mod, w_rt, b_r)
    pos, be = _slots(idx, rank, cnt)
    pos_flat = pos[:TOP_K].reshape(TOP_K * T_ALL)
    be_flat = jnp.concatenate([be[0], be[1, :1]])
    xs = _dispatch(pos_flat, h, jnp.zeros((P_SLOTS, D_MODEL), F32))
    ys = _experts(be_flat, xs, w_gu, b_gu.reshape(DEPTH, N_EXPERTS, 1, 2 * D_EXPERT), w_down,
                  b_down.reshape(DEPTH, N_EXPERTS, 1, D_MODEL), layer)
    return _combine(pos_flat, x, mod, gate.T, ys)


def _final_norm_kernel(x_ref, g_ref, o_ref):
    x = x_ref[...]
    o_ref[...] = x * lax.rsqrt(jnp.mean(x * x, axis=-1, keepdims=True) + RMS_EPS) * g_ref[...]


def _final_norm(x, g):
    return pl.pallas_call(
        _final_norm_kernel,
        grid=(T_ALL // TM_TOK,),
        in_specs=[pl.BlockSpec((TM_TOK, D_MODEL), lambda i: (i, 0)),
                  pl.BlockSpec((1, D_MODEL), lambda i: (0, 0))],
        out_specs=pl.BlockSpec((TM_TOK, D_MODEL), lambda i: (i, 0)),
        out_shape=jax.ShapeDtypeStruct((T_ALL, D_MODEL), F32),
        compiler_params=_cparams(("arbitrary",)),
        name="final_norm",
    )(x, g)


def _permute_in_cols(w):
    lead = w.shape[:-1]
    pad = jnp.zeros(lead + (N_Z - N_IN,), w.dtype)
    return jnp.concatenate([w[..., GATE_OFF + N_GATE_M:], w[..., :GATE_OFF],
                            w[..., GATE_OFF:GATE_OFF + N_GATE_M], pad], axis=-1)


def kernel(x_prompt, x_sample, cache_k, cache_v, state_C, state_n, state_m, c, c_ctx, norm1_g, w_mod, b_mod, w_in, b_in, rpb, w_dw, b_dw, cln_g, cln_b, mnorm_g, w_pa, w_pc, w_pm, w_out, norm2_g, w_router, b_router, w_gu, b_gu, w_down, b_down, final_g):
    cond = jnp.concatenate([c_ctx[None, :], c, jnp.zeros((SEG_PAD - N_SEG, D_MODEL), F32)], axis=0)
    mod_all = _modulation(cond, w_mod, b_mod).reshape(DEPTH, SEG_PAD, 6, D_MODEL)

    x = jnp.concatenate([x_prompt.reshape(T_CTX, D_MODEL), x_sample.reshape(T_LAT, D_MODEL)], axis=0)
    ck = cache_k.reshape(DEC_BATCH, DEPTH, PAST_LEN, D_A)
    cv = cache_v.reshape(DEC_BATCH, DEPTH, PAST_LEN, D_A)
    lat_c0 = state_C.reshape(DEC_BATCH, DEPTH, N_STREAM, HEAD_DIM_M, HEAD_DIM_M)
    lat_n0 = state_n.reshape(DEC_BATCH, DEPTH, N_STREAM, HEAD_DIM_M)
    lat_m0 = jnp.broadcast_to(state_m.reshape(DEC_BATCH, DEPTH, N_STREAM, 1),
                              (DEC_BATCH, DEPTH, N_STREAM, HEAD_DIM_M))
    ctx_c0 = jnp.zeros((1, N_STREAM, HEAD_DIM_M, HEAD_DIM_M), F32)
    ctx_n0 = jnp.zeros((1, N_STREAM, HEAD_DIM_M), F32)
    ctx_m0 = jnp.full((1, N_STREAM, HEAD_DIM_M), -jnp.inf, F32)

    ks, vs, cs, ns, ms = [], [], [], [], []
    for l in range(DEPTH):
        mod = mod_all[l]
        z = _in_proj(x, norm1_g[l][None, :], mod, _permute_in_cols(w_in[l]).astype(BF16),
                     _permute_in_cols(b_in[l])[None, :])
        ya = jnp.concatenate([_ctx_attention(z), _natt(z, ck, cv, _natt_bias(rpb[l]), l)], axis=0)
        conv_w = (w_dw[l], b_dw[l][None, :], cln_g[l][None, :], cln_b[l][None, :])
        yc = jnp.concatenate([_conv(z, *conv_w, SEQ, BATCH, 0),
                              _conv(z, *conv_w, DEC_SEQ, DEC_BATCH, T_CTX // DEC_SEQ)], axis=0)
        ng = mnorm_g[l][None, :]
        ym_ctx, c_l, n_l, m_l = _mlstm(z, ctx_c0, ctx_n0, ctx_m0, ng, SEQ, BATCH, 0, lambda b: (0,))
        ym_lat, _, _, _ = _mlstm(z, lat_c0, lat_n0, lat_m0, ng, DEC_SEQ, DEC_BATCH, T_CTX // DEC_SEQ,
                                 lambda b: (b, l))
        ym = jnp.concatenate([ym_ctx, ym_lat], axis=0)
        x = _merge(x, mod, ya, yc, ym, z, w_pa[l].astype(BF16), w_pc[l].astype(BF16),
                   w_pm[l].astype(BF16), w_out[l].astype(BF16))
        x = _moe(x, norm2_g[l][None, :], mod, w_router[l].T, b_router[l][:, None],
                 w_gu, b_gu, w_down, b_down, l)
        ks.append(z[:T_CTX, Z_KA_OFF:Z_KA_OFF + D_A].reshape(BATCH, SEQ, N_HEADS_A, HEAD_DIM_A))
        vs.append(z[:T_CTX, Z_VA_OFF:Z_VA_OFF + D_A].reshape(BATCH, SEQ, N_HEADS_A, HEAD_DIM_A))
        cs.append(c_l.reshape(BATCH, 2, N_HEADS_M, HEAD_DIM_M, HEAD_DIM_M))
        ns.append(n_l.reshape(BATCH, 2, N_HEADS_M, HEAD_DIM_M))
        ms.append(m_l[:, :, 0].reshape(BATCH, 2, N_HEADS_M))

    y = _final_norm(x, final_g[None, :])
    return (y[:T_CTX].reshape(BATCH, SEQ, D_MODEL), y[T_CTX:].reshape(DEC_BATCH, DEC_SEQ, D_MODEL),
            jnp.stack(ks, axis=1), jnp.stack(vs, axis=1), jnp.stack(cs, axis=1),
            jnp.stack(ns, axis=1), jnp.stack(ms, axis=1))
```

```python
import functools

import numpy as np
import jax
import jax.numpy as jnp
from jax import lax
from jax.experimental import pallas as pl
from jax.experimental.pallas import tpu as pltpu

F32 = jnp.float32
BF16 = jnp.bfloat16
HIGHEST = lax.Precision.HIGHEST

D_MODEL = 1024
BATCH = 16
SEQ = 256
DEPTH = 2
DEC_BATCH = 8
DEC_SEQ = 1024
PAST_LEN = 512
GRID_W = 64
N_HEADS_A = 8
HEAD_DIM_A = 64
D_A = N_HEADS_A * HEAD_DIM_A
WIN_ROWS = 8
WIN_COLS = 16
D_CONV = 512
CONV_WIDTH = 31
N_HEADS_M = 4
HEAD_DIM_M = 128
D_M = N_HEADS_M * HEAD_DIM_M
N_GATE_M = 4 * N_HEADS_M
CHUNK = 64
N_EXPERTS = 32
TOP_K = 4
D_EXPERT = 1024
SWIGLU_ALPHA = 1.702
SWIGLU_LIMIT = 7.0
RMS_EPS = 1e-6
LN_EPS = 1e-5
GATE_OFF = 3 * D_A + 2 * D_CONV + 4 * D_M
N_IN = GATE_OFF + N_GATE_M + 3 * D_MODEL

T_CTX = BATCH * SEQ
T_LAT = DEC_BATCH * DEC_SEQ
T_ALL = T_CTX + T_LAT
N_SEG = 1 + DEC_BATCH
SEG_PAD = 16
GRID_ROWS = DEC_SEQ // GRID_W
NEG = -1e30

N_ZB = 6 * 512
N_ZF = 5120
QB_QA, QB_KA, QB_VA, QB_QM, QB_KM, QB_VM = 0, 1, 2, 3, 4, 5
ZF_GA, ZF_GC, ZF_GM = 0, 1, 2
ZF_CU, ZF_CG, ZF_OM = 6, 7, 8
ZF_GATES = 36

TM_TOK = 512
TM_PROJ = 1024
MOE_BLK = 256
M_ASSIGN = T_ALL * TOP_K
N_BLOCKS = M_ASSIGN // MOE_BLK + N_EXPERTS
P_SLOTS = N_BLOCKS * MOE_BLK
NB_PAD = 256
TM_DMA = 256
VMEM_LIMIT = 56 * 1024 * 1024


def _cparams(sem=None):
    return pltpu.CompilerParams(dimension_semantics=sem, vmem_limit_bytes=VMEM_LIMIT)


def _seg_of_tile(i, tile):
    n_ctx = T_CTX // tile
    per_lat = DEC_SEQ // tile
    return jnp.where(i < n_ctx, 0, 1 + (i - n_ctx) // per_lat)


def _dot_nt(a, b):
    return lax.dot_general(a, b, (((1,), (1,)), ((), ())), preferred_element_type=F32)


def _sigmoid(x):
    return 1.0 / (1.0 + jnp.exp(-x))


def _mod_kernel(c_ref, w_ref, b_ref, o_ref):
    c = c_ref[...]
    s = c * _sigmoid(c)
    o_ref[...] = jnp.dot(s, w_ref[...], precision=HIGHEST, preferred_element_type=F32) + b_ref[...]


def _modulation(cond, w_mod, b_mod):
    tn = 1536
    return pl.pallas_call(
        _mod_kernel,
        grid=(DEPTH, 6 * D_MODEL // tn),
        in_specs=[pl.BlockSpec((SEG_PAD, D_MODEL), lambda l, j: (0, 0)),
                  pl.BlockSpec((None, D_MODEL, tn), lambda l, j: (l, 0, j)),
                  pl.BlockSpec((None, 1, tn), lambda l, j: (l, 0, j))],
        out_specs=pl.BlockSpec((None, SEG_PAD, tn), lambda l, j: (l, 0, j)),
        out_shape=jax.ShapeDtypeStruct((DEPTH, SEG_PAD, 6 * D_MODEL), F32),
        compiler_params=_cparams(("arbitrary", "arbitrary")),
        name="modulation",
    )(cond, w_mod, b_mod.reshape(DEPTH, 1, 6 * D_MODEL))


def _normmod(x, g, mod, shift_idx, scale_idx):
    y = x * lax.rsqrt(jnp.mean(x * x, axis=-1, keepdims=True) + RMS_EPS) * g
    return y * (1.0 + mod[scale_idx:scale_idx + 1, :]) + mod[shift_idx:shift_idx + 1, :]


def _in_proj_kernel(x_ref, g_ref, mod_ref, w_ref, b_ref, o_ref, h_scr):
    @pl.when(pl.program_id(1) == 0)
    def _():
        h_scr[...] = _normmod(x_ref[...], g_ref[...], mod_ref[...], 0, 1).astype(BF16)

    acc = jnp.dot(h_scr[...], w_ref[...], preferred_element_type=F32) + b_ref[...]
    o_ref[...] = acc.astype(o_ref.dtype)


def _in_proj(x, norm_g, mod, w, b, out_dtype, n_rows):
    tn = 1024
    n = w.shape[1]
    return pl.pallas_call(
        _in_proj_kernel,
        grid=(n_rows // TM_PROJ, n // tn),
        in_specs=[pl.BlockSpec((TM_PROJ, D_MODEL), lambda i, j: (i, 0)),
                  pl.BlockSpec((1, D_MODEL), lambda i, j: (0, 0)),
                  pl.BlockSpec((None, 6, D_MODEL), lambda i, j: (_seg_of_tile(i, TM_PROJ), 0, 0)),
                  pl.BlockSpec((D_MODEL, tn), lambda i, j: (0, j)),
                  pl.BlockSpec((1, tn), lambda i, j: (0, j))],
        out_specs=pl.BlockSpec((TM_PROJ, tn), lambda i, j: (i, j)),
        out_shape=jax.ShapeDtypeStruct((n_rows, n), out_dtype),
        scratch_shapes=[pltpu.VMEM((TM_PROJ, D_MODEL), BF16)],
        compiler_params=_cparams(("arbitrary", "arbitrary")),
        name="in_proj",
    )(x, norm_g, mod, w, b)


HEAD_PAIR = 2 * HEAD_DIM_A
ATT_SCALE = HEAD_DIM_A ** -0.5


def _pair_queries(q2):
    lo = lax.broadcasted_iota(jnp.int32, (1, HEAD_PAIR), 1) < HEAD_DIM_A
    q2 = q2 * ATT_SCALE
    zero = jnp.zeros_like(q2)
    return lo, (jnp.where(lo, q2, zero), jnp.where(lo, zero, q2))


def _ctx_attn_kernel(q_ref, k_ref, v_ref, o_ref):
    for hp in range(N_HEADS_A // 2):
        sl = slice(hp * HEAD_PAIR, (hp + 1) * HEAD_PAIR)
        lo, qs = _pair_queries(q_ref[:, sl])
        k2 = k_ref[:, sl]
        v2 = v_ref[:, sl]
        outs = []
        for qh in qs:
            s = _dot_nt(qh, k2)
            p = jnp.exp(s - jnp.max(s, axis=-1, keepdims=True))
            l = jnp.sum(p, axis=-1, keepdims=True)
            outs.append(jnp.dot(p.astype(BF16), v2, preferred_element_type=F32) / l)
        o_ref[:, sl] = jnp.where(lo, outs[0], outs[1])


def _ctx_attention(zb):
    def spec(cb):
        return pl.BlockSpec((SEQ, D_A), lambda b: (b, cb))

    return pl.pallas_call(
        _ctx_attn_kernel,
        grid=(BATCH,),
        in_specs=[spec(QB_QA), spec(QB_KA), spec(QB_VA)],
        out_specs=pl.BlockSpec((SEQ, D_A), lambda b: (b, 0)),
        out_shape=jax.ShapeDtypeStruct((T_CTX, D_A), F32),
        compiler_params=_cparams(("arbitrary",)),
        name="ctx_attention",
    )(zb, zb, zb)


def _natt_kernel(q_ref, k_ref, v_ref, kc_ref, vc_ref, bias_ref, o_ref):
    r = pl.program_id(1)
    rs = jnp.clip(r - WIN_ROWS // 2, 0, GRID_ROWS - WIN_ROWS)
    start = pl.multiple_of(rs * GRID_W, GRID_W)
    band = WIN_ROWS * GRID_W
    for hp in range(N_HEADS_A // 2):
        sl = slice(hp * HEAD_PAIR, (hp + 1) * HEAD_PAIR)
        lo, qs = _pair_queries(q_ref[:, sl])
        kb = k_ref[pl.ds(start, band), sl]
        vb = v_ref[pl.ds(start, band), sl]
        kc = kc_ref[:, sl]
        vc = vc_ref[:, sl]
        outs = []
        for half, qh in enumerate(qs):
            s_loc = _dot_nt(qh, kb) + bias_ref[2 * hp + half]
            s_ctx = _dot_nt(qh, kc)
            m = jnp.maximum(jnp.max(s_loc, axis=-1, keepdims=True), jnp.max(s_ctx, axis=-1, keepdims=True))
            p_loc = jnp.exp(s_loc - m)
            p_ctx = jnp.exp(s_ctx - m)
            l = jnp.sum(p_loc, axis=-1, keepdims=True) + jnp.sum(p_ctx, axis=-1, keepdims=True)
            o = (jnp.dot(p_loc.astype(BF16), vb, preferred_element_type=F32)
                 + jnp.dot(p_ctx.astype(BF16), vc, preferred_element_type=F32))
            outs.append(o / l)
        o_ref[:, sl] = jnp.where(lo, outs[0], outs[1])


def _band_row_offset(r):
    rs = jnp.clip(r - WIN_ROWS // 2, 0, GRID_ROWS - WIN_ROWS)
    return rs - r + WIN_ROWS - 1


def _natt_bias(rpb_l):
    qc = np.arange(GRID_W)
    kc = np.arange(GRID_W)
    cs = np.clip(qc - WIN_COLS // 2, 0, GRID_W - WIN_COLS)
    ok = (kc[None, :] >= cs[:, None]) & (kc[None, :] < cs[:, None] + WIN_COLS)
    dc = np.clip(kc[None, :] - qc[:, None] + WIN_COLS - 1, 0, 2 * WIN_COLS - 2)
    pick = (dc[None] == np.arange(2 * WIN_COLS - 1)[:, None, None]).astype(np.float32)
    toe = jnp.einsum('hdc,cqk->hdqk', rpb_l, jnp.asarray(pick), precision=HIGHEST)
    toe = jnp.where(jnp.asarray(ok)[None, None], toe, NEG)
    bands = [toe[:, d0:d0 + WIN_ROWS].transpose(0, 2, 1, 3).reshape(N_HEADS_A, GRID_W, WIN_ROWS * GRID_W)
             for d0 in range(WIN_ROWS)]
    return jnp.stack(bands, axis=0)


def _natt(zb, cache_k, cache_v, bias, layer):
    lat0 = T_CTX // DEC_SEQ
    row0 = T_CTX // GRID_W
    return pl.pallas_call(
        _natt_kernel,
        grid=(DEC_BATCH, GRID_ROWS),
        in_specs=[pl.BlockSpec((GRID_W, D_A), lambda b, r: (row0 + b * GRID_ROWS + r, QB_QA)),
                  pl.BlockSpec((DEC_SEQ, D_A), lambda b, r: (lat0 + b, QB_KA)),
                  pl.BlockSpec((DEC_SEQ, D_A), lambda b, r: (lat0 + b, QB_VA)),
                  pl.BlockSpec((None, None, PAST_LEN, D_A), lambda b, r: (b, layer, 0, 0)),
                  pl.BlockSpec((None, None, PAST_LEN, D_A), lambda b, r: (b, layer, 0, 0)),
                  pl.BlockSpec((None, N_HEADS_A, GRID_W, WIN_ROWS * GRID_W),
                               lambda b, r: (_band_row_offset(r), 0, 0, 0))],
        out_specs=pl.BlockSpec((GRID_W, D_A), lambda b, r: (b * GRID_ROWS + r, 0)),
        out_shape=jax.ShapeDtypeStruct((T_LAT, D_A), F32),
        compiler_params=_cparams(("arbitrary", "arbitrary")),
        name="nbr_attention",
    )(zb, zb, zb, cache_k, cache_v, bias)


CONV_HALO = 16
CONV_ROWS = 64


def _conv_kernel(u_ref, g_ref, w_ref, b_ref, lg_ref, lb_ref, o_ref, pad_scr, *, seq):
    zeros = jnp.zeros((CONV_HALO, D_CONV), F32)
    pad_scr[0:CONV_HALO, :] = zeros
    pad_scr[CONV_HALO + seq:2 * CONV_HALO + seq, :] = zeros
    pad_scr[CONV_HALO:CONV_HALO + seq, :] = u_ref[...] * _sigmoid(g_ref[...])
    first = CONV_HALO - CONV_WIDTH // 2
    for c in range(seq // CONV_ROWS):
        base = c * CONV_ROWS
        acc = jnp.broadcast_to(b_ref[...], (CONV_ROWS, D_CONV))
        for j in range(CONV_WIDTH):
            acc = acc + pad_scr[base + first + j:base + first + j + CONV_ROWS, :] * w_ref[j:j + 1, :]
        mu = jnp.mean(acc, axis=-1, keepdims=True)
        xc = acc - mu
        var = jnp.mean(xc * xc, axis=-1, keepdims=True)
        y = xc * lax.rsqrt(var + LN_EPS) * lg_ref[...] + lb_ref[...]
        o_ref[base:base + CONV_ROWS, :] = y * _sigmoid(y)


def _conv(z, w_dw, b_dw, ln_g, ln_b, seq, n_seq, row_block0):
    def vec():
        return pl.BlockSpec((1, D_CONV), lambda b: (0, 0))

    return pl.pallas_call(
        functools.partial(_conv_kernel, seq=seq),
        grid=(n_seq,),
        in_specs=[pl.BlockSpec((seq, D_CONV), lambda b: (row_block0 + b, ZF_CU)),
                  pl.BlockSpec((seq, D_CONV), lambda b: (row_block0 + b, ZF_CG)),
                  pl.BlockSpec((CONV_WIDTH, D_CONV), lambda b: (0, 0)),
                  vec(), vec(), vec()],
        out_specs=pl.BlockSpec((seq, D_CONV), lambda b: (b, 0)),
        out_shape=jax.ShapeDtypeStruct((n_seq * seq, D_CONV), F32),
        scratch_shapes=[pltpu.VMEM((seq + 2 * CONV_HALO, D_CONV), F32)],
        compiler_params=_cparams(("arbitrary",)),
        name="conformer_conv",
    )(z, z, w_dw, b_dw, ln_g, ln_b)


N_STREAM = 2 * N_HEADS_M
MCHUNK = 128


def _mlstm_kernel(q_ref, k_ref, v_ref, om_ref, gt_ref, c0_ref, n0_ref, m0_ref, ng_ref,
                  y_ref, c_out, n_out, m_out, hf_scr, hb_scr, c_scr, n_scr, m_scr, *, seq):
    nc = seq // MCHUNK
    c_scr[...] = c0_ref[...]
    n_scr[...] = n0_ref[...]
    m_scr[...] = m0_ref[...]
    ti = lax.broadcasted_iota(jnp.int32, (MCHUNK, MCHUNK), 0)
    si = lax.broadcasted_iota(jnp.int32, (MCHUNK, MCHUNK), 1)
    masks = (si <= ti, si >= ti)
    tris = tuple(mk.astype(F32) for mk in masks)
    kscale = HEAD_DIM_M ** -0.5

    def chunk_step(c, carry):
        for d in range(2):
            cidx = c if d == 0 else nc - 1 - c
            off = pl.multiple_of(cidx * MCHUNK, MCHUNK)
            last = MCHUNK - 1 if d == 0 else 0
            g = gt_ref[pl.ds(off, MCHUNK), :]
            lf = jnp.minimum(g, 0.0) - jnp.log(1.0 + jnp.exp(-jnp.abs(g)))
            cum = jnp.dot(tris[d], lf, precision=HIGHEST, preferred_element_type=F32)
            g_t = g.T
            cum_t = cum.T
            for h in range(N_HEADS_M):
                s_id = N_HEADS_M * d + h
                icol = 2 * N_HEADS_M * d + h
                fcol = icol + N_HEADS_M
                hs = slice(h * HEAD_DIM_M, (h + 1) * HEAD_DIM_M)
                b_col = cum[:, fcol:fcol + 1]
                i_col = g[:, icol:icol + 1]
                b_row = cum_t[fcol:fcol + 1, :]
                i_row = g_t[icol:icol + 1, :]
                b_last = b_col[last:last + 1, :]
                m_prev = m_scr[s_id:s_id + 1, 0:1]
                log_d = jnp.where(masks[d], b_col - b_row + i_row, NEG)
                inter = b_col + m_prev
                m_t = jnp.maximum(inter, jnp.max(log_d, axis=-1, keepdims=True))
                w_inter = jnp.exp(inter - m_t)
                qb = q_ref[pl.ds(off, MCHUNK), hs]
                kb = k_ref[pl.ds(off, MCHUNK), hs]
                vb = v_ref[pl.ds(off, MCHUNK), hs]
                c_prev = c_scr[s_id]
                n_prev = n_scr[s_id:s_id + 1, :]
                s_mat = _dot_nt(qb, kb) * (kscale * jnp.exp(log_d - m_t))
                num = (w_inter * _dot_nt(qb, c_prev.astype(BF16))
                       + jnp.dot(s_mat.astype(BF16), vb, preferred_element_type=F32))
                den = (w_inter * jnp.sum(qb.astype(F32) * n_prev, axis=-1, keepdims=True)
                       + jnp.sum(s_mat, axis=-1, keepdims=True))
                hh = num / jnp.maximum(jnp.abs(den), jnp.exp(-m_t))
                if d == 0:
                    hf_scr[pl.ds(off, MCHUNK), hs] = hh
                else:
                    hb_scr[pl.ds(off, MCHUNK), hs] = hh
                m_new = m_t[last:last + 1, :]
                w_prev = jnp.exp(b_last + m_prev - m_new)
                w_src = kscale * jnp.exp(b_last - b_col + i_col - m_new)
                upd = jnp.dot((w_src * vb.astype(F32)).T.astype(BF16), kb, preferred_element_type=F32)
                c_scr[s_id] = w_prev * c_prev + upd
                n_scr[s_id:s_id + 1, :] = (w_prev * n_prev
                                           + jnp.sum(w_src * kb.astype(F32), axis=0, keepdims=True))
                m_scr[s_id:s_id + 1, :] = jnp.broadcast_to(m_new, (1, HEAD_DIM_M))
        return carry

    lax.fori_loop(0, nc, chunk_step, 0)

    for h in range(N_HEADS_M):
        hs = slice(h * HEAD_DIM_M, (h + 1) * HEAD_DIM_M)
        hsum = hf_scr[:, hs] + hb_scr[:, hs]
        mu = jnp.mean(hsum, axis=-1, keepdims=True)
        xc = hsum - mu
        var = jnp.mean(xc * xc, axis=-1, keepdims=True)
        hn = xc * lax.rsqrt(var + LN_EPS) * ng_ref[:, hs]
        y_ref[:, hs] = _sigmoid(om_ref[:, hs]) * hn
    c_out[...] = c_scr[...]
    n_out[...] = n_scr[...]
    m_out[...] = m_scr[...]


def _mlstm(zb, zf, c0, n0, m0, norm_g, seq, n_seq, row_block0, state_map):
    lead = len(state_map(0))

    def zspec(cb):
        return pl.BlockSpec((seq, D_M), lambda b: (row_block0 + b, cb))

    def sspec(tail):
        return pl.BlockSpec((None,) * lead + tail, lambda b: state_map(b) + (0,) * len(tail))

    return pl.pallas_call(
        functools.partial(_mlstm_kernel, seq=seq),
        grid=(n_seq,),
        in_specs=[zspec(QB_QM), zspec(QB_KM), zspec(QB_VM), zspec(ZF_OM),
                  pl.BlockSpec((seq, 128), lambda b: (row_block0 + b, ZF_GATES)),
                  sspec((N_STREAM, HEAD_DIM_M, HEAD_DIM_M)),
                  sspec((N_STREAM, HEAD_DIM_M)),
                  sspec((N_STREAM, HEAD_DIM_M)),
                  pl.BlockSpec((1, D_M), lambda b: (0, 0))],
        out_specs=[pl.BlockSpec((seq, D_M), lambda b: (b, 0)),
                   pl.BlockSpec((None, N_STREAM, HEAD_DIM_M, HEAD_DIM_M), lambda b: (b, 0, 0, 0)),
                   pl.BlockSpec((None, N_STREAM, HEAD_DIM_M), lambda b: (b, 0, 0)),
                   pl.BlockSpec((None, N_STREAM, HEAD_DIM_M), lambda b: (b, 0, 0))],
        out_shape=[jax.ShapeDtypeStruct((n_seq * seq, D_M), F32),
                   jax.ShapeDtypeStruct((n_seq, N_STREAM, HEAD_DIM_M, HEAD_DIM_M), F32),
                   jax.ShapeDtypeStruct((n_seq, N_STREAM, HEAD_DIM_M), F32),
                   jax.ShapeDtypeStruct((n_seq, N_STREAM, HEAD_DIM_M), F32)],
        scratch_shapes=[pltpu.VMEM((seq, D_M), F32), pltpu.VMEM((seq, D_M), F32),
                        pltpu.VMEM((N_STREAM, HEAD_DIM_M, HEAD_DIM_M), F32),
                        pltpu.VMEM((N_STREAM, HEAD_DIM_M), F32),
                        pltpu.VMEM((N_STREAM, HEAD_DIM_M), F32)],
        compiler_params=_cparams(("arbitrary",)),
        name="mlstm",
    )(zb, zb, zb, zf, zf, c0, n0, m0, norm_g)


N_CTX_TILES = T_CTX // TM_TOK


def _merge_kernel(x_ref, mod_ref, ya_c, ya_l, yc_c, yc_l, ym_c, ym_l, ga_ref, gc_ref, gm_ref,
                  wa_ref, wc_ref, wm_ref, wo_ref, o_ref):
    is_ctx = pl.program_id(0) < N_CTX_TILES

    def branch(y_ctx, y_lat, g_ref, w_ref):
        y = jnp.where(is_ctx, y_ctx[...], y_lat[...]).astype(BF16)
        return _sigmoid(g_ref[...]) * jnp.dot(y, w_ref[...], preferred_element_type=F32)

    merged = (branch(ya_c, ya_l, ga_ref, wa_ref) + branch(yc_c, yc_l, gc_ref, wc_ref)
              + branch(ym_c, ym_l, gm_ref, wm_ref))
    mix = jnp.dot(merged.astype(BF16), wo_ref[...], preferred_element_type=F32)
    o_ref[...] = x_ref[...] + mod_ref[2:3, :] * mix


def _merge(x, mod, ya, yc, ym, zf, w_pa, w_pc, w_pm, w_out):
    def rows(width, cb=0):
        return pl.BlockSpec((TM_TOK, width), lambda i: (i, cb))

    def ctx_rows(width):
        return pl.BlockSpec((TM_TOK, width), lambda i: (jnp.minimum(i, N_CTX_TILES - 1), 0))

    def lat_rows(width):
        return pl.BlockSpec((TM_TOK, width), lambda i: (jnp.maximum(i - N_CTX_TILES, 0), 0))

    def full(shape):
        return pl.BlockSpec(shape, lambda i: (0, 0))

    return pl.pallas_call(
        _merge_kernel,
        grid=(T_ALL // TM_TOK,),
        in_specs=[rows(D_MODEL),
                  pl.BlockSpec((None, 6, D_MODEL), lambda i: (_seg_of_tile(i, TM_TOK), 0, 0)),
                  ctx_rows(D_A), lat_rows(D_A), ctx_rows(D_CONV), lat_rows(D_CONV),
                  ctx_rows(D_M), lat_rows(D_M),
                  rows(D_MODEL, ZF_GA), rows(D_MODEL, ZF_GC), rows(D_MODEL, ZF_GM),
                  full((D_A, D_MODEL)), full((D_CONV, D_MODEL)), full((D_M, D_MODEL)),
                  full((D_MODEL, D_MODEL))],
        out_specs=rows(D_MODEL),
        out_shape=jax.ShapeDtypeStruct((T_ALL, D_MODEL), F32),
        compiler_params=_cparams(("arbitrary",)),
        name="merge",
    )(x, mod, ya[0], ya[1], yc[0], yc[1], ym[0], ym[1], zf, zf, zf, w_pa, w_pc, w_pm, w_out)


def _router_kernel(x_ref, g_ref, mod_ref, wr_ref, br_ref, h_ref, idx_ref, gate_ref, rank_ref, cnt_ref, cnt_scr):
    i = pl.program_id(0)

    @pl.when(i == 0)
    def _():
        cnt_scr[...] = jnp.zeros_like(cnt_scr)

    h = _normmod(x_ref[...], g_ref[...], mod_ref[...], 3, 4)
    h_ref[...] = h
    logits = _dot_nt(wr_ref[...].astype(BF16), h.astype(BF16)) + br_ref[...]
    e_iota = lax.broadcasted_iota(jnp.int32, (N_EXPERTS, TM_TOK), 0).astype(F32)
    sels, vals = [], []
    l = logits
    for k in range(TOP_K):
        m = jnp.max(l, axis=0, keepdims=True)
        idx = jnp.min(jnp.where(l == m, e_iota, float(N_EXPERTS)), axis=0, keepdims=True)
        sel = e_iota == idx
        idx_ref[k:k + 1, :] = idx.astype(jnp.int32)
        vals.append(m)
        sels.append(sel)
        l = jnp.where(sel, -jnp.inf, l)
    exps = [jnp.exp(v - vals[0]) for v in vals]
    tot = exps[0] + exps[1] + exps[2] + exps[3]
    onehot = jnp.zeros((N_EXPERTS, TM_TOK), F32)
    for k in range(TOP_K):
        gate_ref[k:k + 1, :] = exps[k] / tot
        onehot = onehot + sels[k].astype(F32)
    t_src = lax.broadcasted_iota(jnp.int32, (TM_TOK, TM_TOK), 0)
    t_dst = lax.broadcasted_iota(jnp.int32, (TM_TOK, TM_TOK), 1)
    before = (t_src < t_dst).astype(BF16)
    carry = cnt_scr[:, 0:1]
    prefix = jnp.dot(onehot.astype(BF16), before, preferred_element_type=F32) + carry
    for k in range(TOP_K):
        rk = jnp.sum(jnp.where(sels[k], prefix, 0.0), axis=0, keepdims=True)
        rank_ref[k:k + 1, :] = rk.astype(jnp.int32)
    pad = jnp.zeros((8 - TOP_K, TM_TOK), jnp.int32)
    idx_ref[TOP_K:8, :] = pad
    rank_ref[TOP_K:8, :] = pad
    gate_ref[TOP_K:8, :] = pad.astype(F32)
    total = carry + jnp.sum(onehot, axis=1, keepdims=True)
    cnt_scr[...] = jnp.broadcast_to(total, cnt_scr.shape)
    cnt_ref[...] = cnt_scr[...]


def _router(x, norm_g, mod, w_rt, b_r):
    def tok(dtype):
        return jax.ShapeDtypeStruct((8, T_ALL), dtype)

    tspec = pl.BlockSpec((8, TM_TOK), lambda i: (0, i))
    return pl.pallas_call(
        _router_kernel,
        grid=(T_ALL // TM_TOK,),
        in_specs=[pl.BlockSpec((TM_TOK, D_MODEL), lambda i: (i, 0)),
                  pl.BlockSpec((1, D_MODEL), lambda i: (0, 0)),
                  pl.BlockSpec((None, 6, D_MODEL), lambda i: (_seg_of_tile(i, TM_TOK), 0, 0)),
                  pl.BlockSpec((N_EXPERTS, D_MODEL), lambda i: (0, 0)),
                  pl.BlockSpec((N_EXPERTS, 1), lambda i: (0, 0))],
        out_specs=[pl.BlockSpec((TM_TOK, D_MODEL), lambda i: (i, 0)), tspec, tspec, tspec,
                   pl.BlockSpec((N_EXPERTS, 128), lambda i: (0, 0))],
        out_shape=[jax.ShapeDtypeStruct((T_ALL, D_MODEL), F32), tok(jnp.int32), tok(F32), tok(jnp.int32),
                   jax.ShapeDtypeStruct((N_EXPERTS, 128), F32)],
        scratch_shapes=[pltpu.VMEM((N_EXPERTS, 128), F32)],
        compiler_params=_cparams(("arbitrary",)),
        name="moe_router",
    )(x, norm_g, mod, w_rt, b_r)


def _slots_kernel(idx_ref, rank_ref, cnt_ref, pos_ref, be_ref):
    cnt = cnt_ref[:, 0:1]
    padded = jnp.floor((cnt + (MOE_BLK - 1)) / MOE_BLK) * MOE_BLK
    ei = lax.broadcasted_iota(jnp.int32, (N_EXPERTS, N_EXPERTS), 0)
    ej = lax.broadcasted_iota(jnp.int32, (N_EXPERTS, N_EXPERTS), 1)
    lower = (ej < ei).astype(F32)
    pstart = jnp.dot(lower, jnp.broadcast_to(padded, (N_EXPERTS, 128)), precision=HIGHEST,
                     preferred_element_type=F32)[:, 0:1]
    pend = pstart + padded
    e_iota = lax.broadcasted_iota(jnp.int32, (N_EXPERTS, T_ALL), 0)
    for k in range(TOP_K):
        start_k = jnp.sum(jnp.where(e_iota == idx_ref[k:k + 1, :], pstart, 0.0), axis=0, keepdims=True)
        pos_ref[k:k + 1, :] = start_k.astype(jnp.int32) + rank_ref[k:k + 1, :]
    pos_ref[TOP_K:8, :] = jnp.zeros((8 - TOP_K, T_ALL), jnp.int32)
    row0 = (lax.broadcasted_iota(jnp.int32, (N_EXPERTS, NB_PAD), 1) * MOE_BLK).astype(F32)
    n_done = jnp.sum((pend <= row0).astype(jnp.int32), axis=0, keepdims=True)
    n_used = (pend[N_EXPERTS - 1:N_EXPERTS, :] / MOE_BLK).astype(jnp.int32)
    be_ref[0:1, :] = jnp.minimum(n_done, N_EXPERTS - 1)
    be_ref[1:2, :] = jnp.broadcast_to(n_used, (1, NB_PAD))
    be_ref[2:8, :] = jnp.zeros((6, NB_PAD), jnp.int32)


def _slots(idx, rank, cnt):
    return pl.pallas_call(
        _slots_kernel,
        out_shape=[jax.ShapeDtypeStruct((8, T_ALL), jnp.int32),
                   jax.ShapeDtypeStruct((8, NB_PAD), jnp.int32)],
        compiler_params=_cparams(),
        name="moe_slots",
    )(idx, rank, cnt)


def _dispatch_kernel(pos_ref, h_ref, xs_in, xs_ref, sems):
    del xs_in
    base = pl.program_id(0) * TM_DMA

    def row_copy(t, k):
        p = pos_ref[k * T_ALL + base + t]
        return pltpu.make_async_copy(h_ref.at[pl.ds(t, 1), :], xs_ref.at[pl.ds(p, 1), :], sems.at[k])

    def issue(t, carry):
        for k in range(TOP_K):
            row_copy(t, k).start()
        return carry

    lax.fori_loop(0, TM_DMA, issue, 0)
    for k in range(TOP_K):
        pltpu.make_async_copy(h_ref, xs_ref.at[pl.ds(0, TM_DMA), :], sems.at[k]).wait()


def _dispatch(pos_flat, h, xs_init):
    return pl.pallas_call(
        _dispatch_kernel,
        grid_spec=pltpu.PrefetchScalarGridSpec(
            num_scalar_prefetch=1,
            grid=(T_ALL // TM_DMA,),
            in_specs=[pl.BlockSpec((TM_DMA, D_MODEL), lambda i, pos: (i, 0)),
                      pl.BlockSpec(memory_space=pl.ANY)],
            out_specs=pl.BlockSpec(memory_space=pl.ANY),
            scratch_shapes=[pltpu.SemaphoreType.DMA((TOP_K,))]),
        out_shape=jax.ShapeDtypeStruct((P_SLOTS, D_MODEL), F32),
        input_output_aliases={2: 0},
        compiler_params=_cparams(("arbitrary",)),
        name="moe_dispatch",
    )(pos_flat, h, xs_init)


def _expert_kernel(be_ref, xs_ref, wgu_ref, bgu_ref, wd_ref, bd_ref, ys_ref, wgu_scr, wd_scr):
    i = pl.program_id(0)
    n_used = be_ref[NB_PAD]
    changed = jnp.logical_or(i == 0, be_ref[i] != be_ref[jnp.maximum(i - 1, 0)])

    @pl.when(jnp.logical_and(changed, i < n_used))
    def _():
        wgu_scr[...] = wgu_ref[...].astype(BF16)
        wd_scr[...] = wd_ref[...].astype(BF16)

    @pl.when(i < n_used)
    def _():
        hgu = jnp.dot(xs_ref[...].astype(BF16), wgu_scr[...], preferred_element_type=F32) + bgu_ref[...]
        h_glu = jnp.minimum(hgu[:, :D_EXPERT], SWIGLU_LIMIT)
        h_lin = jnp.clip(hgu[:, D_EXPERT:], -SWIGLU_LIMIT, SWIGLU_LIMIT)
        act = (h_lin + 1.0) * (h_glu * _sigmoid(SWIGLU_ALPHA * h_glu))
        ys_ref[...] = jnp.dot(act.astype(BF16), wd_scr[...], preferred_element_type=F32) + bd_ref[...]

    @pl.when(i >= n_used)
    def _():
        ys_ref[...] = jnp.zeros_like(ys_ref)


def _experts(be_flat, xs, w_gu, b_gu, w_down, b_down, layer):
    return pl.pallas_call(
        _expert_kernel,
        grid_spec=pltpu.PrefetchScalarGridSpec(
            num_scalar_prefetch=1,
            grid=(N_BLOCKS,),
            in_specs=[pl.BlockSpec((MOE_BLK, D_MODEL), lambda i, be: (i, 0)),
                      pl.BlockSpec((None, None, D_MODEL, 2 * D_EXPERT), lambda i, be: (layer, be[i], 0, 0)),
                      pl.BlockSpec((None, None, 1, 2 * D_EXPERT), lambda i, be: (layer, be[i], 0, 0)),
                      pl.BlockSpec((None, None, D_EXPERT, D_MODEL), lambda i, be: (layer, be[i], 0, 0)),
                      pl.BlockSpec((None, None, 1, D_MODEL), lambda i, be: (layer, be[i], 0, 0))],
            out_specs=pl.BlockSpec((MOE_BLK, D_MODEL), lambda i, be: (i, 0)),
            scratch_shapes=[pltpu.VMEM((D_MODEL, 2 * D_EXPERT), BF16),
                            pltpu.VMEM((D_EXPERT, D_MODEL), BF16)]),
        out_shape=jax.ShapeDtypeStruct((P_SLOTS, D_MODEL), F32),
        compiler_params=_cparams(("arbitrary",)),
        name="moe_experts",
    )(be_flat, xs, w_gu, b_gu, w_down, b_down)


def _combine_kernel(pos_ref, x_ref, mod_ref, gate_ref, ys_ref, o_ref, ybuf, sems):
    base = pl.program_id(0) * TM_DMA

    def row_copy(t, k):
        p = pos_ref[k * T_ALL + base + t]
        return pltpu.make_async_copy(ys_ref.at[pl.ds(p, 1), :], ybuf.at[k, pl.ds(t, 1), :], sems.at[k])

    def issue(t, carry):
        for k in range(TOP_K):
            row_copy(t, k).start()
        return carry

    lax.fori_loop(0, TM_DMA, issue, 0)
    for k in range(TOP_K):
        pltpu.make_async_copy(ys_ref.at[pl.ds(0, TM_DMA), :], ybuf.at[k], sems.at[k]).wait()
    acc = gate_ref[:, 0:1] * ybuf[0]
    for k in range(1, TOP_K):
        acc = acc + gate_ref[:, k:k + 1] * ybuf[k]
    o_ref[...] = x_ref[...] + mod_ref[5:6, :] * acc


def _combine(pos_flat, x, mod, gate_t, ys):
    return pl.pallas_call(
        _combine_kernel,
        grid_spec=pltpu.PrefetchScalarGridSpec(
            num_scalar_prefetch=1,
            grid=(T_ALL // TM_DMA,),
            in_specs=[pl.BlockSpec((TM_DMA, D_MODEL), lambda i, pos: (i, 0)),
                      pl.BlockSpec((None, 6, D_MODEL), lambda i, pos: (_seg_of_tile(i, TM_DMA), 0, 0)),
                      pl.BlockSpec((TM_DMA, 8), lambda i, pos: (i, 0)),
                      pl.BlockSpec(memory_space=pl.ANY)],
            out_specs=pl.BlockSpec((TM_DMA, D_MODEL), lambda i, pos: (i, 0)),
            scratch_shapes=[pltpu.VMEM((TOP_K, TM_DMA, D_MODEL), F32),
                            pltpu.SemaphoreType.DMA((TOP_K,))]),
        out_shape=jax.ShapeDtypeStruct((T_ALL, D_MODEL), F32),
        compiler_params=_cparams(("arbitrary",)),
        name="moe_combine",
    )(pos_flat, x, mod, gate_t, ys)


def _moe(x, norm_g, mod, w_rt, b_r, w_gu, b_gu, w_down, b_down, layer):
    h, idx, gate, rank, cnt = _router(x, norm_g, mod, w_rt, b_r)
    pos, be = _slots(idx, rank, cnt)
    pos_flat = pos[:TOP_K].reshape(TOP_K * T_ALL)
    be_flat = jnp.concatenate([be[0], be[1, :1]])
    xs = _dispatch(pos_flat, h, jnp.zeros((P_SLOTS, D_MODEL), F32))
    ys = _experts(be_flat, xs, w_gu, b_gu.reshape(DEPTH, N_EXPERTS, 1, 2 * D_EXPERT), w_down,
                  b_down.reshape(DEPTH, N_EXPERTS, 1, D_MODEL), layer)
    return _combine(pos_flat, x, mod, gate.T, ys)


def _final_norm_kernel(x_ref, g_ref, o_ref):
    x = x_ref[...]
    o_ref[...] = x * lax.rsqrt(jnp.mean(x * x, axis=-1, keepdims=True) + RMS_EPS) * g_ref[...]


def _final_norm(x, g):
    return pl.pallas_call(
        _final_norm_kernel,
        grid=(T_ALL // TM_TOK,),
        in_specs=[pl.BlockSpec((TM_TOK, D_MODEL), lambda i: (i, 0)),
                  pl.BlockSpec((1, D_MODEL), lambda i: (0, 0))],
        out_specs=pl.BlockSpec((TM_TOK, D_MODEL), lambda i: (i, 0)),
        out_shape=jax.ShapeDtypeStruct((T_ALL, D_MODEL), F32),
        compiler_params=_cparams(("arbitrary",)),
        name="final_norm",
    )(x, g)


def _split_in_cols(w):
    conv0, mq0, om0 = 3 * D_A, 3 * D_A + 2 * D_CONV, 3 * D_A + 2 * D_CONV + 3 * D_M
    gates_end = GATE_OFF + N_GATE_M
    pad = jnp.zeros(w.shape[:-1] + (N_ZF - (3 * D_MODEL + 2 * D_CONV + D_M + N_GATE_M),), w.dtype)
    zb = jnp.concatenate([w[..., :conv0], w[..., mq0:om0]], axis=-1)
    zf = jnp.concatenate([w[..., gates_end:], w[..., conv0:mq0], w[..., om0:GATE_OFF],
                          w[..., GATE_OFF:gates_end], pad], axis=-1)
    kv = w[..., D_A:3 * D_A]
    return zb, zf, kv


def kernel(x_prompt, x_sample, cache_k, cache_v, state_C, state_n, state_m, c, c_ctx, norm1_g, w_mod, b_mod, w_in, b_in, rpb, w_dw, b_dw, cln_g, cln_b, mnorm_g, w_pa, w_pc, w_pm, w_out, norm2_g, w_router, b_router, w_gu, b_gu, w_down, b_down, final_g):
    cond = jnp.concatenate([c_ctx[None, :], c, jnp.zeros((SEG_PAD - N_SEG, D_MODEL), F32)], axis=0)
    mod_all = _modulation(cond, w_mod, b_mod).reshape(DEPTH, SEG_PAD, 6, D_MODEL)

    x = jnp.concatenate([x_prompt.reshape(T_CTX, D_MODEL), x_sample.reshape(T_LAT, D_MODEL)], axis=0)
    ck = cache_k.reshape(DEC_BATCH, DEPTH, PAST_LEN, D_A).astype(BF16)
    cv = cache_v.reshape(DEC_BATCH, DEPTH, PAST_LEN, D_A).astype(BF16)
    lat_c0 = state_C.reshape(DEC_BATCH, DEPTH, N_STREAM, HEAD_DIM_M, HEAD_DIM_M)
    lat_n0 = state_n.reshape(DEC_BATCH, DEPTH, N_STREAM, HEAD_DIM_M)
    lat_m0 = jnp.broadcast_to(state_m.reshape(DEC_BATCH, DEPTH, N_STREAM, 1),
                              (DEC_BATCH, DEPTH, N_STREAM, HEAD_DIM_M))
    ctx_c0 = jnp.zeros((1, N_STREAM, HEAD_DIM_M, HEAD_DIM_M), F32)
    ctx_n0 = jnp.zeros((1, N_STREAM, HEAD_DIM_M), F32)
    ctx_m0 = jnp.full((1, N_STREAM, HEAD_DIM_M), -jnp.inf, F32)

    ks, vs, cs, ns, ms = [], [], [], [], []
    for l in range(DEPTH):
        mod = mod_all[l]
        g1 = norm1_g[l][None, :]
        w_zb, w_zf, w_kv = _split_in_cols(w_in[l].astype(BF16))
        b_zb, b_zf, b_kv = _split_in_cols(b_in[l][None, :])
        zb = _in_proj(x, g1, mod, w_zb, b_zb, BF16, T_ALL)
        zf = _in_proj(x, g1, mod, w_zf, b_zf, F32, T_ALL)
        kv = _in_proj(x, g1, mod, w_kv, b_kv, F32, T_CTX)
        ya = (_ctx_attention(zb), _natt(zb, ck, cv, _natt_bias(rpb[l]), l))
        conv_w = (w_dw[l], b_dw[l][None, :], cln_g[l][None, :], cln_b[l][None, :])
        yc = (_conv(zf, *conv_w, SEQ, BATCH, 0), _conv(zf, *conv_w, DEC_SEQ, DEC_BATCH, T_CTX // DEC_SEQ))
        ng = mnorm_g[l][None, :]
        ym_ctx, c_l, n_l, m_l = _mlstm(zb, zf, ctx_c0, ctx_n0, ctx_m0, ng, SEQ, BATCH, 0, lambda b: (0,))
        ym_lat, _, _, _ = _mlstm(zb, zf, lat_c0, lat_n0, lat_m0, ng, DEC_SEQ, DEC_BATCH, T_CTX // DEC_SEQ,
                                 lambda b: (b, l))
        x = _merge(x, mod, ya, yc, (ym_ctx, ym_lat), zf, w_pa[l].astype(BF16), w_pc[l].astype(BF16),
                   w_pm[l].astype(BF16), w_out[l].astype(BF16))
        x = _moe(x, norm2_g[l][None, :], mod, w_router[l].T, b_router[l][:, None],
                 w_gu, b_gu, w_down, b_down, l)
        ks.append(kv[:, :D_A].reshape(BATCH, SEQ, N_HEADS_A, HEAD_DIM_A))
        vs.append(kv[:, D_A:].reshape(BATCH, SEQ, N_HEADS_A, HEAD_DIM_A))
        cs.append(c_l.reshape(BATCH, 2, N_HEADS_M, HEAD_DIM_M, HEAD_DIM_M))
        ns.append(n_l.reshape(BATCH, 2, N_HEADS_M, HEAD_DIM_M))
        ms.append(m_l[:, :, 0].reshape(BATCH, 2, N_HEADS_M))

    y = _final_norm(x, final_g[None, :])
    return (y[:T_CTX].reshape(BATCH, SEQ, D_MODEL), y[T_CTX:].reshape(DEC_BATCH, DEC_SEQ, D_MODEL),
            jnp.stack(ks, axis=1), jnp.stack(vs, axis=1), jnp.stack(cs, axis=1),
            jnp.stack(ns, axis=1), jnp.stack(ms, axis=1))
```

```python
import functools

import numpy as np
import jax
import jax.numpy as jnp
from jax import lax
from jax.experimental import pallas as pl
from jax.experimental.pallas import tpu as pltpu

F32 = jnp.float32
BF16 = jnp.bfloat16
HIGHEST = lax.Precision.HIGHEST

D_MODEL = 1024
BATCH = 16
SEQ = 256
DEPTH = 2
DEC_BATCH = 8
DEC_SEQ = 1024
PAST_LEN = 512
GRID_W = 64
N_HEADS_A = 8
HEAD_DIM_A = 64
D_A = N_HEADS_A * HEAD_DIM_A
WIN_ROWS = 8
WIN_COLS = 16
D_CONV = 512
CONV_WIDTH = 31
N_HEADS_M = 4
HEAD_DIM_M = 128
D_M = N_HEADS_M * HEAD_DIM_M
N_GATE_M = 4 * N_HEADS_M
CHUNK = 64
N_EXPERTS = 32
TOP_K = 4
D_EXPERT = 1024
SWIGLU_ALPHA = 1.702
SWIGLU_LIMIT = 7.0
RMS_EPS = 1e-6
LN_EPS = 1e-5
GATE_OFF = 3 * D_A + 2 * D_CONV + 4 * D_M
N_IN = GATE_OFF + N_GATE_M + 3 * D_MODEL

T_CTX = BATCH * SEQ
T_LAT = DEC_BATCH * DEC_SEQ
T_ALL = T_CTX + T_LAT
N_SEG = 1 + DEC_BATCH
SEG_PAD = 16
GRID_ROWS = DEC_SEQ // GRID_W
NEG = -1e30

N_ZB = 6 * 512
N_ZF = 5120
QB_QA, QB_KA, QB_VA, QB_QM, QB_KM, QB_VM = 0, 1, 2, 3, 4, 5
ZF_GA, ZF_GC, ZF_GM = 0, 1, 2
ZF_CU, ZF_CG, ZF_OM = 6, 7, 8
ZF_GATES = 36

TM_TOK = 512
TM_PROJ = 1024
TM_MOE = 512
N_TILES = T_ALL // TM_MOE
CHUNK_ROWS = 16
MXU_ROWS = 256
Q_TILE = -(-(TM_MOE * TOP_K + N_EXPERTS * (CHUNK_ROWS - 1)) // MXU_ROWS) * MXU_ROWS
CH_PER_TILE = Q_TILE // CHUNK_ROWS
E_GROUP = 256
CPG = E_GROUP // CHUNK_ROWS
VMEM_LIMIT = 56 * 1024 * 1024


def _cparams(sem=None):
    return pltpu.CompilerParams(dimension_semantics=sem, vmem_limit_bytes=VMEM_LIMIT)


def _seg_of_tile(i, tile):
    n_ctx = T_CTX // tile
    per_lat = DEC_SEQ // tile
    return jnp.where(i < n_ctx, 0, 1 + (i - n_ctx) // per_lat)


def _dot_nt(a, b):
    return lax.dot_general(a, b, (((1,), (1,)), ((), ())), preferred_element_type=F32)


def _sigmoid(x):
    return 1.0 / (1.0 + jnp.exp(-x))


def _mod_kernel(c_ref, w_ref, b_ref, o_ref):
    c = c_ref[...]
    s = c * _sigmoid(c)
    o_ref[...] = jnp.dot(s, w_ref[...], precision=HIGHEST, preferred_element_type=F32) + b_ref[...]


def _modulation(cond, w_mod, b_mod):
    tn = 1536
    return pl.pallas_call(
        _mod_kernel,
        grid=(DEPTH, 6 * D_MODEL // tn),
        in_specs=[pl.BlockSpec((SEG_PAD, D_MODEL), lambda l, j: (0, 0)),
                  pl.BlockSpec((None, D_MODEL, tn), lambda l, j: (l, 0, j)),
                  pl.BlockSpec((None, 1, tn), lambda l, j: (l, 0, j))],
        out_specs=pl.BlockSpec((None, SEG_PAD, tn), lambda l, j: (l, 0, j)),
        out_shape=jax.ShapeDtypeStruct((DEPTH, SEG_PAD, 6 * D_MODEL), F32),
        compiler_params=_cparams(("arbitrary", "arbitrary")),
        name="modulation",
    )(cond, w_mod, b_mod.reshape(DEPTH, 1, 6 * D_MODEL))


def _normmod(x, g, mod, shift_idx, scale_idx):
    y = x * lax.rsqrt(jnp.mean(x * x, axis=-1, keepdims=True) + RMS_EPS) * g
    return y * (1.0 + mod[scale_idx:scale_idx + 1, :]) + mod[shift_idx:shift_idx + 1, :]


def _in_proj_kernel(x_ref, g_ref, mod_ref, w_ref, b_ref, o_ref, h_scr):
    @pl.when(pl.program_id(1) == 0)
    def _():
        h_scr[...] = _normmod(x_ref[...], g_ref[...], mod_ref[...], 0, 1).astype(BF16)

    acc = jnp.dot(h_scr[...], w_ref[...], preferred_element_type=F32) + b_ref[...]
    o_ref[...] = acc.astype(o_ref.dtype)


def _in_proj(x, norm_g, mod, w, b, out_dtype, n_rows):
    tn = 1024
    n = w.shape[1]
    return pl.pallas_call(
        _in_proj_kernel,
        grid=(n_rows // TM_PROJ, n // tn),
        in_specs=[pl.BlockSpec((TM_PROJ, D_MODEL), lambda i, j: (i, 0)),
                  pl.BlockSpec((1, D_MODEL), lambda i, j: (0, 0)),
                  pl.BlockSpec((None, 6, D_MODEL), lambda i, j: (_seg_of_tile(i, TM_PROJ), 0, 0)),
                  pl.BlockSpec((D_MODEL, tn), lambda i, j: (0, j)),
                  pl.BlockSpec((1, tn), lambda i, j: (0, j))],
        out_specs=pl.BlockSpec((TM_PROJ, tn), lambda i, j: (i, j)),
        out_shape=jax.ShapeDtypeStruct((n_rows, n), out_dtype),
        scratch_shapes=[pltpu.VMEM((TM_PROJ, D_MODEL), BF16)],
        compiler_params=_cparams(("arbitrary", "arbitrary")),
        name="in_proj",
    )(x, norm_g, mod, w, b)


HEAD_PAIR = 2 * HEAD_DIM_A
ATT_SCALE = HEAD_DIM_A ** -0.5


def _pair_queries(q2):
    lo = lax.broadcasted_iota(jnp.int32, (1, HEAD_PAIR), 1) < HEAD_DIM_A
    q2 = q2 * ATT_SCALE
    zero = jnp.zeros_like(q2)
    return lo, (jnp.where(lo, q2, zero), jnp.where(lo, zero, q2))


def _ctx_attn_kernel(q_ref, k_ref, v_ref, o_ref):
    for hp in range(N_HEADS_A // 2):
        sl = slice(hp * HEAD_PAIR, (hp + 1) * HEAD_PAIR)
        lo, qs = _pair_queries(q_ref[:, sl])
        k2 = k_ref[:, sl]
        v2 = v_ref[:, sl]
        outs = []
        for qh in qs:
            s = _dot_nt(qh, k2)
            p = jnp.exp(s - jnp.max(s, axis=-1, keepdims=True))
            l = jnp.sum(p, axis=-1, keepdims=True)
            outs.append(jnp.dot(p.astype(BF16), v2, preferred_element_type=F32) / l)
        o_ref[:, sl] = jnp.where(lo, outs[0], outs[1])


def _ctx_attention(zb):
    def spec(cb):
        return pl.BlockSpec((SEQ, D_A), lambda b: (b, cb))

    return pl.pallas_call(
        _ctx_attn_kernel,
        grid=(BATCH,),
        in_specs=[spec(QB_QA), spec(QB_KA), spec(QB_VA)],
        out_specs=pl.BlockSpec((SEQ, D_A), lambda b: (b, 0)),
        out_shape=jax.ShapeDtypeStruct((T_CTX, D_A), F32),
        compiler_params=_cparams(("arbitrary",)),
        name="ctx_attention",
    )(zb, zb, zb)


def _natt_kernel(q_ref, k_ref, v_ref, kc_ref, vc_ref, bias_ref, o_ref):
    r = pl.program_id(1)
    rs = jnp.clip(r - WIN_ROWS // 2, 0, GRID_ROWS - WIN_ROWS)
    start = pl.multiple_of(rs * GRID_W, GRID_W)
    band = WIN_ROWS * GRID_W
    for hp in range(N_HEADS_A // 2):
        sl = slice(hp * HEAD_PAIR, (hp + 1) * HEAD_PAIR)
        lo, qs = _pair_queries(q_ref[:, sl])
        kb = k_ref[pl.ds(start, band), sl]
        vb = v_ref[pl.ds(start, band), sl]
        kc = kc_ref[:, sl]
        vc = vc_ref[:, sl]
        outs = []
        for half, qh in enumerate(qs):
            s_loc = _dot_nt(qh, kb) + bias_ref[2 * hp + half]
            s_ctx = _dot_nt(qh, kc)
            m = jnp.maximum(jnp.max(s_loc, axis=-1, keepdims=True), jnp.max(s_ctx, axis=-1, keepdims=True))
            p_loc = jnp.exp(s_loc - m)
            p_ctx = jnp.exp(s_ctx - m)
            l = jnp.sum(p_loc, axis=-1, keepdims=True) + jnp.sum(p_ctx, axis=-1, keepdims=True)
            o = (jnp.dot(p_loc.astype(BF16), vb, preferred_element_type=F32)
                 + jnp.dot(p_ctx.astype(BF16), vc, preferred_element_type=F32))
            outs.append(o / l)
        o_ref[:, sl] = jnp.where(lo, outs[0], outs[1])


def _band_row_offset(r):
    rs = jnp.clip(r - WIN_ROWS // 2, 0, GRID_ROWS - WIN_ROWS)
    return rs - r + WIN_ROWS - 1


def _natt_bias(rpb_l):
    qc = np.arange(GRID_W)
    kc = np.arange(GRID_W)
    cs = np.clip(qc - WIN_COLS // 2, 0, GRID_W - WIN_COLS)
    ok = (kc[None, :] >= cs[:, None]) & (kc[None, :] < cs[:, None] + WIN_COLS)
    dc = np.clip(kc[None, :] - qc[:, None] + WIN_COLS - 1, 0, 2 * WIN_COLS - 2)
    pick = (dc[None] == np.arange(2 * WIN_COLS - 1)[:, None, None]).astype(np.float32)
    toe = jnp.einsum('hdc,cqk->hdqk', rpb_l, jnp.asarray(pick), precision=HIGHEST)
    toe = jnp.where(jnp.asarray(ok)[None, None], toe, NEG)
    bands = [toe[:, d0:d0 + WIN_ROWS].transpose(0, 2, 1, 3).reshape(N_HEADS_A, GRID_W, WIN_ROWS * GRID_W)
             for d0 in range(WIN_ROWS)]
    return jnp.stack(bands, axis=0)


def _natt(zb, cache_k, cache_v, bias, layer):
    lat0 = T_CTX // DEC_SEQ
    row0 = T_CTX // GRID_W
    return pl.pallas_call(
        _natt_kernel,
        grid=(DEC_BATCH, GRID_ROWS),
        in_specs=[pl.BlockSpec((GRID_W, D_A), lambda b, r: (row0 + b * GRID_ROWS + r, QB_QA)),
                  pl.BlockSpec((DEC_SEQ, D_A), lambda b, r: (lat0 + b, QB_KA)),
                  pl.BlockSpec((DEC_SEQ, D_A), lambda b, r: (lat0 + b, QB_VA)),
                  pl.BlockSpec((None, None, PAST_LEN, D_A), lambda b, r: (b, layer, 0, 0)),
                  pl.BlockSpec((None, None, PAST_LEN, D_A), lambda b, r: (b, layer, 0, 0)),
                  pl.BlockSpec((None, N_HEADS_A, GRID_W, WIN_ROWS * GRID_W),
                               lambda b, r: (_band_row_offset(r), 0, 0, 0))],
        out_specs=pl.BlockSpec((GRID_W, D_A), lambda b, r: (b * GRID_ROWS + r, 0)),
        out_shape=jax.ShapeDtypeStruct((T_LAT, D_A), F32),
        compiler_params=_cparams(("arbitrary", "arbitrary")),
        name="nbr_attention",
    )(zb, zb, zb, cache_k, cache_v, bias)


CONV_HALO = 16
CONV_ROWS = 64


def _conv_kernel(u_ref, g_ref, w_ref, b_ref, lg_ref, lb_ref, o_ref, pad_scr, *, seq):
    zeros = jnp.zeros((CONV_HALO, D_CONV), F32)
    pad_scr[0:CONV_HALO, :] = zeros
    pad_scr[CONV_HALO + seq:2 * CONV_HALO + seq, :] = zeros
    pad_scr[CONV_HALO:CONV_HALO + seq, :] = u_ref[...] * _sigmoid(g_ref[...])
    first = CONV_HALO - CONV_WIDTH // 2
    for c in range(seq // CONV_ROWS):
        base = c * CONV_ROWS
        acc = jnp.broadcast_to(b_ref[...], (CONV_ROWS, D_CONV))
        for j in range(CONV_WIDTH):
            acc = acc + pad_scr[base + first + j:base + first + j + CONV_ROWS, :] * w_ref[j:j + 1, :]
        mu = jnp.mean(acc, axis=-1, keepdims=True)
        xc = acc - mu
        var = jnp.mean(xc * xc, axis=-1, keepdims=True)
        y = xc * lax.rsqrt(var + LN_EPS) * lg_ref[...] + lb_ref[...]
        o_ref[base:base + CONV_ROWS, :] = y * _sigmoid(y)


def _conv(z, w_dw, b_dw, ln_g, ln_b, seq, n_seq, row_block0):
    def vec():
        return pl.BlockSpec((1, D_CONV), lambda b: (0, 0))

    return pl.pallas_call(
        functools.partial(_conv_kernel, seq=seq),
        grid=(n_seq,),
        in_specs=[pl.BlockSpec((seq, D_CONV), lambda b: (row_block0 + b, ZF_CU)),
                  pl.BlockSpec((seq, D_CONV), lambda b: (row_block0 + b, ZF_CG)),
                  pl.BlockSpec((CONV_WIDTH, D_CONV), lambda b: (0, 0)),
                  vec(), vec(), vec()],
        out_specs=pl.BlockSpec((seq, D_CONV), lambda b: (b, 0)),
        out_shape=jax.ShapeDtypeStruct((n_seq * seq, D_CONV), F32),
        scratch_shapes=[pltpu.VMEM((seq + 2 * CONV_HALO, D_CONV), F32)],
        compiler_params=_cparams(("arbitrary",)),
        name="conformer_conv",
    )(z, z, w_dw, b_dw, ln_g, ln_b)


N_STREAM = 2 * N_HEADS_M
MCHUNK = 128


def _mlstm_kernel(q_ref, k_ref, v_ref, om_ref, gt_ref, c0_ref, n0_ref, m0_ref, ng_ref,
                  y_ref, c_out, n_out, m_out, hf_scr, hb_scr, c_scr, n_scr, m_scr, *, seq):
    nc = seq // MCHUNK
    c_scr[...] = c0_ref[...]
    n_scr[...] = n0_ref[...]
    m_scr[...] = m0_ref[...]
    ti = lax.broadcasted_iota(jnp.int32, (MCHUNK, MCHUNK), 0)
    si = lax.broadcasted_iota(jnp.int32, (MCHUNK, MCHUNK), 1)
    masks = (si <= ti, si >= ti)
    tris = tuple(mk.astype(F32) for mk in masks)
    kscale = HEAD_DIM_M ** -0.5

    def chunk_step(c, carry):
        for d in range(2):
            cidx = c if d == 0 else nc - 1 - c
            off = pl.multiple_of(cidx * MCHUNK, MCHUNK)
            last = MCHUNK - 1 if d == 0 else 0
            g = gt_ref[pl.ds(off, MCHUNK), :]
            lf = jnp.minimum(g, 0.0) - jnp.log(1.0 + jnp.exp(-jnp.abs(g)))
            cum = jnp.dot(tris[d], lf, precision=HIGHEST, preferred_element_type=F32)
            g_t = g.T
            cum_t = cum.T
            for h in range(N_HEADS_M):
                s_id = N_HEADS_M * d + h
                icol = 2 * N_HEADS_M * d + h
                fcol = icol + N_HEADS_M
                hs = slice(h * HEAD_DIM_M, (h + 1) * HEAD_DIM_M)
                b_col = cum[:, fcol:fcol + 1]
                i_col = g[:, icol:icol + 1]
                b_row = cum_t[fcol:fcol + 1, :]
                i_row = g_t[icol:icol + 1, :]
                b_last = b_col[last:last + 1, :]
                m_prev = m_scr[s_id:s_id + 1, 0:1]
                log_d = jnp.where(masks[d], b_col - b_row + i_row, NEG)
                inter = b_col + m_prev
                m_t = jnp.maximum(inter, jnp.max(log_d, axis=-1, keepdims=True))
                w_inter = jnp.exp(inter - m_t)
                qb = q_ref[pl.ds(off, MCHUNK), hs]
                kb = k_ref[pl.ds(off, MCHUNK), hs]
                vb = v_ref[pl.ds(off, MCHUNK), hs]
                c_prev = c_scr[s_id]
                n_prev = n_scr[s_id:s_id + 1, :]
                s_mat = _dot_nt(qb, kb) * (kscale * jnp.exp(log_d - m_t))
                num = (w_inter * _dot_nt(qb, c_prev.astype(BF16))
                       + jnp.dot(s_mat.astype(BF16), vb, preferred_element_type=F32))
                den = (w_inter * jnp.sum(qb.astype(F32) * n_prev, axis=-1, keepdims=True)
                       + jnp.sum(s_mat, axis=-1, keepdims=True))
                hh = num / jnp.maximum(jnp.abs(den), jnp.exp(-m_t))
                if d == 0:
                    hf_scr[pl.ds(off, MCHUNK), hs] = hh
                else:
                    hb_scr[pl.ds(off, MCHUNK), hs] = hh
                m_new = m_t[last:last + 1, :]
                w_prev = jnp.exp(b_last + m_prev - m_new)
                w_src = kscale * jnp.exp(b_last - b_col + i_col - m_new)
                upd = jnp.dot((w_src * vb.astype(F32)).T.astype(BF16), kb, preferred_element_type=F32)
                c_scr[s_id] = w_prev * c_prev + upd
                n_scr[s_id:s_id + 1, :] = (w_prev * n_prev
                                           + jnp.sum(w_src * kb.astype(F32), axis=0, keepdims=True))
                m_scr[s_id:s_id + 1, :] = jnp.broadcast_to(m_new, (1, HEAD_DIM_M))
        return carry

    lax.fori_loop(0, nc, chunk_step, 0)

    for h in range(N_HEADS_M):
        hs = slice(h * HEAD_DIM_M, (h + 1) * HEAD_DIM_M)
        hsum = hf_scr[:, hs] + hb_scr[:, hs]
        mu = jnp.mean(hsum, axis=-1, keepdims=True)
        xc = hsum - mu
        var = jnp.mean(xc * xc, axis=-1, keepdims=True)
        hn = xc * lax.rsqrt(var + LN_EPS) * ng_ref[:, hs]
        y_ref[:, hs] = _sigmoid(om_ref[:, hs]) * hn
    c_out[...] = c_scr[...]
    n_out[...] = n_scr[...]
    m_out[...] = m_scr[...]


def _mlstm(zb, zf, c0, n0, m0, norm_g, seq, n_seq, row_block0, state_map):
    lead = len(state_map(0))

    def zspec(cb):
        return pl.BlockSpec((seq, D_M), lambda b: (row_block0 + b, cb))

    def sspec(tail):
        return pl.BlockSpec((None,) * lead + tail, lambda b: state_map(b) + (0,) * len(tail))

    return pl.pallas_call(
        functools.partial(_mlstm_kernel, seq=seq),
        grid=(n_seq,),
        in_specs=[zspec(QB_QM), zspec(QB_KM), zspec(QB_VM), zspec(ZF_OM),
                  pl.BlockSpec((seq, 128), lambda b: (row_block0 + b, ZF_GATES)),
                  sspec((N_STREAM, HEAD_DIM_M, HEAD_DIM_M)),
                  sspec((N_STREAM, HEAD_DIM_M)),
                  sspec((N_STREAM, HEAD_DIM_M)),
                  pl.BlockSpec((1, D_M), lambda b: (0, 0))],
        out_specs=[pl.BlockSpec((seq, D_M), lambda b: (b, 0)),
                   pl.BlockSpec((None, N_STREAM, HEAD_DIM_M, HEAD_DIM_M), lambda b: (b, 0, 0, 0)),
                   pl.BlockSpec((None, N_STREAM, HEAD_DIM_M), lambda b: (b, 0, 0)),
                   pl.BlockSpec((None, N_STREAM, HEAD_DIM_M), lambda b: (b, 0, 0))],
        out_shape=[jax.ShapeDtypeStruct((n_seq * seq, D_M), F32),
                   jax.ShapeDtypeStruct((n_seq, N_STREAM, HEAD_DIM_M, HEAD_DIM_M), F32),
                   jax.ShapeDtypeStruct((n_seq, N_STREAM, HEAD_DIM_M), F32),
                   jax.ShapeDtypeStruct((n_seq, N_STREAM, HEAD_DIM_M), F32)],
        scratch_shapes=[pltpu.VMEM((seq, D_M), F32), pltpu.VMEM((seq, D_M), F32),
                        pltpu.VMEM((N_STREAM, HEAD_DIM_M, HEAD_DIM_M), F32),
                        pltpu.VMEM((N_STREAM, HEAD_DIM_M), F32),
                        pltpu.VMEM((N_STREAM, HEAD_DIM_M), F32)],
        compiler_params=_cparams(("arbitrary",)),
        name="mlstm",
    )(zb, zb, zb, zf, zf, c0, n0, m0, norm_g)


N_CTX_TILES = T_CTX // TM_TOK


def _merge_kernel(x_ref, mod_ref, ya_c, ya_l, yc_c, yc_l, ym_c, ym_l, ga_ref, gc_ref, gm_ref,
                  wa_ref, wc_ref, wm_ref, wo_ref, o_ref):
    is_ctx = pl.program_id(0) < N_CTX_TILES

    def branch(y_ctx, y_lat, g_ref, w_ref):
        y = jnp.where(is_ctx, y_ctx[...], y_lat[...]).astype(BF16)
        return _sigmoid(g_ref[...]) * jnp.dot(y, w_ref[...], preferred_element_type=F32)

    merged = (branch(ya_c, ya_l, ga_ref, wa_ref) + branch(yc_c, yc_l, gc_ref, wc_ref)
              + branch(ym_c, ym_l, gm_ref, wm_ref))
    mix = jnp.dot(merged.astype(BF16), wo_ref[...], preferred_element_type=F32)
    o_ref[...] = x_ref[...] + mod_ref[2:3, :] * mix


def _merge(x, mod, ya, yc, ym, zf, w_pa, w_pc, w_pm, w_out):
    def rows(width, cb=0):
        return pl.BlockSpec((TM_TOK, width), lambda i: (i, cb))

    def ctx_rows(width):
        return pl.BlockSpec((TM_TOK, width), lambda i: (jnp.minimum(i, N_CTX_TILES - 1), 0))

    def lat_rows(width):
        return pl.BlockSpec((TM_TOK, width), lambda i: (jnp.maximum(i - N_CTX_TILES, 0), 0))

    def full(shape):
        return pl.BlockSpec(shape, lambda i: (0, 0))

    return pl.pallas_call(
        _merge_kernel,
        grid=(T_ALL // TM_TOK,),
        in_specs=[rows(D_MODEL),
                  pl.BlockSpec((None, 6, D_MODEL), lambda i: (_seg_of_tile(i, TM_TOK), 0, 0)),
                  ctx_rows(D_A), lat_rows(D_A), ctx_rows(D_CONV), lat_rows(D_CONV),
                  ctx_rows(D_M), lat_rows(D_M),
                  rows(D_MODEL, ZF_GA), rows(D_MODEL, ZF_GC), rows(D_MODEL, ZF_GM),
                  full((D_A, D_MODEL)), full((D_CONV, D_MODEL)), full((D_M, D_MODEL)),
                  full((D_MODEL, D_MODEL))],
        out_specs=rows(D_MODEL),
        out_shape=jax.ShapeDtypeStruct((T_ALL, D_MODEL), F32),
        compiler_params=_cparams(("arbitrary",)),
        name="merge",
    )(x, mod, ya[0], ya[1], yc[0], yc[1], ym[0], ym[1], zf, zf, zf, w_pa, w_pc, w_pm, w_out)


def _route_sort_kernel(x_ref, g_ref, mod_ref, wr_ref, br_ref, xt_ref, pos_ref, gate_ref, nch_ref, seg_ref):
    h = _normmod(x_ref[...], g_ref[...], mod_ref[...], 3, 4)
    hb = h.astype(BF16)
    logits = _dot_nt(wr_ref[...].astype(BF16), hb) + br_ref[...]
    e_iota = lax.broadcasted_iota(jnp.int32, (N_EXPERTS, TM_MOE), 0).astype(F32)
    sels, vals = [], []
    l = logits
    for k in range(TOP_K):
        m = jnp.max(l, axis=0, keepdims=True)
        idx = jnp.min(jnp.where(l == m, e_iota, float(N_EXPERTS)), axis=0, keepdims=True)
        sel = e_iota == idx
        vals.append(m)
        sels.append(sel)
        l = jnp.where(sel, -jnp.inf, l)
    exps = [jnp.exp(v - vals[0]) for v in vals]
    tot = exps[0] + exps[1] + exps[2] + exps[3]
    onehot = jnp.zeros((N_EXPERTS, TM_MOE), F32)
    for k in range(TOP_K):
        gate_ref[k:k + 1, :] = exps[k] / tot
        onehot = onehot + sels[k].astype(F32)
    gate_ref[TOP_K:8, :] = jnp.zeros((8 - TOP_K, TM_MOE), F32)

    cnt = jnp.sum(onehot, axis=1, keepdims=True)
    nch = jnp.floor((cnt + (CHUNK_ROWS - 1)) / CHUNK_ROWS)
    ei = lax.broadcasted_iota(jnp.int32, (N_EXPERTS, N_EXPERTS), 0)
    ej = lax.broadcasted_iota(jnp.int32, (N_EXPERTS, N_EXPERTS), 1)
    seg = jnp.dot((ej < ei).astype(F32), jnp.broadcast_to(nch, (N_EXPERTS, 128)), precision=HIGHEST,
                  preferred_element_type=F32)
    nch_ref[...] = jnp.broadcast_to(nch, (N_EXPERTS, 128)).astype(jnp.int32)
    seg_ref[...] = seg.astype(jnp.int32)

    t_src = lax.broadcasted_iota(jnp.int32, (TM_MOE, TM_MOE), 0)
    t_dst = lax.broadcasted_iota(jnp.int32, (TM_MOE, TM_MOE), 1)
    before = (t_src < t_dst).astype(BF16)
    row_of = (seg[:, 0:1] * CHUNK_ROWS
              + jnp.dot(onehot.astype(BF16), before, preferred_element_type=F32))
    q_iota = lax.broadcasted_iota(jnp.int32, (Q_TILE, TM_MOE), 0)
    perm = jnp.zeros((Q_TILE, TM_MOE), F32)
    for k in range(TOP_K):
        q_k = jnp.sum(jnp.where(sels[k], row_of, 0.0), axis=0, keepdims=True).astype(jnp.int32)
        pos_ref[k:k + 1, :] = q_k
        perm = jnp.where(q_iota == q_k, 1.0, perm)
    pos_ref[TOP_K:8, :] = jnp.zeros((8 - TOP_K, TM_MOE), jnp.int32)
    xt_ref[...] = jnp.dot(perm.astype(BF16), hb, preferred_element_type=F32).astype(BF16)


def _route_sort(x, norm_g, mod, w_rt, b_r):
    tspec = pl.BlockSpec((8, TM_MOE), lambda i: (0, i))
    mspec = pl.BlockSpec((None, N_EXPERTS, 128), lambda i: (i, 0, 0))
    meta = jax.ShapeDtypeStruct((N_TILES, N_EXPERTS, 128), jnp.int32)
    return pl.pallas_call(
        _route_sort_kernel,
        grid=(N_TILES,),
        in_specs=[pl.BlockSpec((TM_MOE, D_MODEL), lambda i: (i, 0)),
                  pl.BlockSpec((1, D_MODEL), lambda i: (0, 0)),
                  pl.BlockSpec((None, 6, D_MODEL), lambda i: (_seg_of_tile(i, TM_MOE), 0, 0)),
                  pl.BlockSpec((N_EXPERTS, D_MODEL), lambda i: (0, 0)),
                  pl.BlockSpec((N_EXPERTS, 1), lambda i: (0, 0))],
        out_specs=[pl.BlockSpec((Q_TILE, D_MODEL), lambda i: (i, 0)), tspec, tspec, mspec, mspec],
        out_shape=[jax.ShapeDtypeStruct((N_TILES * Q_TILE, D_MODEL), BF16),
                   jax.ShapeDtypeStruct((8, T_ALL), jnp.int32), jax.ShapeDtypeStruct((8, T_ALL), F32),
                   meta, meta],
        compiler_params=_cparams(("arbitrary",)),
        name="moe_route_sort",
    )(x, norm_g, mod, w_rt, b_r)


def _expert_kernel(nch_ref, seg_ref, wgu_ref, bgu_ref, wd_ref, bd_ref, xt_ref, yt_ref,
                   wgu_scr, wd_scr, xbuf, ybuf, row_scr, gsem, ssem):
    del xt_ref
    e = pl.program_id(0)

    def chunks_in(i):
        return nch_ref[i * N_EXPERTS + e]

    total = lax.fori_loop(0, N_TILES, lambda i, acc: acc + chunks_in(i), 0)
    n_groups = (total + CPG - 1) // CPG

    def group_chunks(n):
        return jnp.minimum(CPG, total - n * CPG)

    def buf_rows(c):
        return pl.ds(pl.multiple_of(c * CHUNK_ROWS, CHUNK_ROWS), CHUNK_ROWS)

    def hbm_rows(row):
        return pl.ds(pl.multiple_of(row, CHUNK_ROWS), CHUNK_ROWS)

    def copy_in(slot, c, row):
        return pltpu.make_async_copy(yt_ref.at[hbm_rows(row), :], xbuf.at[slot, buf_rows(c), :], gsem.at[slot])

    def copy_out(slot, c, row):
        return pltpu.make_async_copy(ybuf.at[slot, buf_rows(c), :], yt_ref.at[hbm_rows(row), :], ssem.at[slot])

    def issue_group(n, slot, cursor):
        @pl.when(group_chunks(n) < CPG)
        def _():
            xbuf[slot] = jnp.zeros((E_GROUP, D_MODEL), BF16)

        def one(c, cur):
            tile, j = lax.while_loop(lambda s: s[1] >= chunks_in(s[0]), lambda s: (s[0] + 1, 0), cur)
            row = pl.multiple_of((tile * CH_PER_TILE + seg_ref[tile * N_EXPERTS + e] + j) * CHUNK_ROWS,
                                 CHUNK_ROWS)
            row_scr[slot, c] = row
            copy_in(slot, c, row).start()
            return tile, j + 1

        return lax.fori_loop(0, group_chunks(n), one, cursor)

    def wait_out(slot, n_chunks):
        lax.fori_loop(0, n_chunks, lambda c, _: (copy_out(slot, c, 0).wait(), 0)[1], 0)

    @pl.when(total > 0)
    def _():
        wgu_scr[...] = wgu_ref[...].astype(BF16)
        wd_scr[...] = wd_ref[...].astype(BF16)

    def group_step(n, cursor):
        slot = n % 2
        cursor = lax.cond(n + 1 < n_groups, lambda cur: issue_group(n + 1, 1 - slot, cur),
                          lambda cur: cur, cursor)
        lax.fori_loop(0, group_chunks(n), lambda c, _: (copy_in(slot, c, row_scr[slot, c]).wait(), 0)[1], 0)

        @pl.when(n >= 2)
        def _():
            wait_out(slot, CPG)

        hgu = jnp.dot(xbuf[slot], wgu_scr[...], preferred_element_type=F32) + bgu_ref[...]
        h_glu = jnp.minimum(hgu[:, :D_EXPERT], SWIGLU_LIMIT)
        h_lin = jnp.clip(hgu[:, D_EXPERT:], -SWIGLU_LIMIT, SWIGLU_LIMIT)
        act = (h_lin + 1.0) * (h_glu * _sigmoid(SWIGLU_ALPHA * h_glu))
        y = jnp.dot(act.astype(BF16), wd_scr[...], preferred_element_type=F32) + bd_ref[...]
        ybuf[slot] = y.astype(BF16)
        lax.fori_loop(0, group_chunks(n), lambda c, _: (copy_out(slot, c, row_scr[slot, c]).start(), 0)[1], 0)
        return cursor

    first = lax.cond(n_groups > 0, lambda cur: issue_group(0, 0, cur), lambda cur: cur,
                     (jnp.int32(0), jnp.int32(0)))
    lax.fori_loop(0, n_groups, group_step, first)

    @pl.when(n_groups >= 1)
    def _():
        wait_out((n_groups - 1) % 2, group_chunks(n_groups - 1))

    @pl.when(n_groups >= 2)
    def _():
        wait_out(n_groups % 2, CPG)


def _experts(nch_flat, seg_flat, xt, w_gu, b_gu, w_down, b_down, layer):
    return pl.pallas_call(
        _expert_kernel,
        grid_spec=pltpu.PrefetchScalarGridSpec(
            num_scalar_prefetch=2,
            grid=(N_EXPERTS,),
            in_specs=[pl.BlockSpec((None, None, D_MODEL, 2 * D_EXPERT), lambda e, n, s: (layer, e, 0, 0)),
                      pl.BlockSpec((None, None, 1, 2 * D_EXPERT), lambda e, n, s: (layer, e, 0, 0)),
                      pl.BlockSpec((None, None, D_EXPERT, D_MODEL), lambda e, n, s: (layer, e, 0, 0)),
                      pl.BlockSpec((None, None, 1, D_MODEL), lambda e, n, s: (layer, e, 0, 0)),
                      pl.BlockSpec(memory_space=pl.ANY)],
            out_specs=pl.BlockSpec(memory_space=pl.ANY),
            scratch_shapes=[pltpu.VMEM((D_MODEL, 2 * D_EXPERT), BF16),
                            pltpu.VMEM((D_EXPERT, D_MODEL), BF16),
                            pltpu.VMEM((2, E_GROUP, D_MODEL), BF16),
                            pltpu.VMEM((2, E_GROUP, D_MODEL), BF16),
                            pltpu.SMEM((2, CPG), jnp.int32),
                            pltpu.SemaphoreType.DMA((2,)),
                            pltpu.SemaphoreType.DMA((2,))]),
        out_shape=jax.ShapeDtypeStruct((N_TILES * Q_TILE, D_MODEL), BF16),
        input_output_aliases={6: 0},
        compiler_params=_cparams(("arbitrary",)),
        name="moe_experts",
    )(nch_flat, seg_flat, w_gu, b_gu, w_down, b_down, xt)


def _combine_kernel(x_ref, mod_ref, pos_ref, gate_ref, yt_ref, o_ref):
    lane = lax.broadcasted_iota(jnp.int32, (TM_MOE, Q_TILE), 1)
    sel = jnp.zeros((TM_MOE, Q_TILE), F32)
    for k in range(TOP_K):
        sel = jnp.where(lane == pos_ref[:, k:k + 1], gate_ref[:, k:k + 1], sel)
    acc = jnp.dot(sel.astype(BF16), yt_ref[...], preferred_element_type=F32)
    o_ref[...] = x_ref[...] + mod_ref[5:6, :] * acc


def _combine(x, mod, pos_t, gate_t, yt):
    return pl.pallas_call(
        _combine_kernel,
        grid=(N_TILES,),
        in_specs=[pl.BlockSpec((TM_MOE, D_MODEL), lambda i: (i, 0)),
                  pl.BlockSpec((None, 6, D_MODEL), lambda i: (_seg_of_tile(i, TM_MOE), 0, 0)),
                  pl.BlockSpec((TM_MOE, 8), lambda i: (i, 0)),
                  pl.BlockSpec((TM_MOE, 8), lambda i: (i, 0)),
                  pl.BlockSpec((Q_TILE, D_MODEL), lambda i: (i, 0))],
        out_specs=pl.BlockSpec((TM_MOE, D_MODEL), lambda i: (i, 0)),
        out_shape=jax.ShapeDtypeStruct((T_ALL, D_MODEL), F32),
        compiler_params=_cparams(("arbitrary",)),
        name="moe_combine",
    )(x, mod, pos_t, gate_t, yt)


def _moe(x, norm_g, mod, w_rt, b_r, w_gu, b_gu, w_down, b_down, layer):
    xt, pos, gate, nch, seg = _route_sort(x, norm_g, mod, w_rt, b_r)
    yt = _experts(nch[:, :, 0].reshape(-1), seg[:, :, 0].reshape(-1), xt, w_gu,
                  b_gu.reshape(DEPTH, N_EXPERTS, 1, 2 * D_EXPERT), w_down,
                  b_down.reshape(DEPTH, N_EXPERTS, 1, D_MODEL), layer)
    return _combine(x, mod, pos.T, gate.T, yt)


def _final_norm_kernel(x_ref, g_ref, o_ref):
    x = x_ref[...]
    o_ref[...] = x * lax.rsqrt(jnp.mean(x * x, axis=-1, keepdims=True) + RMS_EPS) * g_ref[...]


def _final_norm(x, g):
    return pl.pallas_call(
        _final_norm_kernel,
        grid=(T_ALL // TM_TOK,),
        in_specs=[pl.BlockSpec((TM_TOK, D_MODEL), lambda i: (i, 0)),
                  pl.BlockSpec((1, D_MODEL), lambda i: (0, 0))],
        out_specs=pl.BlockSpec((TM_TOK, D_MODEL), lambda i: (i, 0)),
        out_shape=jax.ShapeDtypeStruct((T_ALL, D_MODEL), F32),
        compiler_params=_cparams(("arbitrary",)),
        name="final_norm",
    )(x, g)


def _split_in_cols(w):
    conv0, mq0, om0 = 3 * D_A, 3 * D_A + 2 * D_CONV, 3 * D_A + 2 * D_CONV + 3 * D_M
    gates_end = GATE_OFF + N_GATE_M
    pad = jnp.zeros(w.shape[:-1] + (N_ZF - (3 * D_MODEL + 2 * D_CONV + D_M + N_GATE_M),), w.dtype)
    zb = jnp.concatenate([w[..., :conv0], w[..., mq0:om0]], axis=-1)
    zf = jnp.concatenate([w[..., gates_end:], w[..., conv0:mq0], w[..., om0:GATE_OFF],
                          w[..., GATE_OFF:gates_end], pad], axis=-1)
    kv = w[..., D_A:3 * D_A]
    return zb, zf, kv


def kernel(x_prompt, x_sample, cache_k, cache_v, state_C, state_n, state_m, c, c_ctx, norm1_g, w_mod, b_mod, w_in, b_in, rpb, w_dw, b_dw, cln_g, cln_b, mnorm_g, w_pa, w_pc, w_pm, w_out, norm2_g, w_router, b_router, w_gu, b_gu, w_down, b_down, final_g):
    cond = jnp.concatenate([c_ctx[None, :], c, jnp.zeros((SEG_PAD - N_SEG, D_MODEL), F32)], axis=0)
    mod_all = _modulation(cond, w_mod, b_mod).reshape(DEPTH, SEG_PAD, 6, D_MODEL)

    x = jnp.concatenate([x_prompt.reshape(T_CTX, D_MODEL), x_sample.reshape(T_LAT, D_MODEL)], axis=0)
    ck = cache_k.reshape(DEC_BATCH, DEPTH, PAST_LEN, D_A).astype(BF16)
    cv = cache_v.reshape(DEC_BATCH, DEPTH, PAST_LEN, D_A).astype(BF16)
    lat_c0 = state_C.reshape(DEC_BATCH, DEPTH, N_STREAM, HEAD_DIM_M, HEAD_DIM_M)
    lat_n0 = state_n.reshape(DEC_BATCH, DEPTH, N_STREAM, HEAD_DIM_M)
    lat_m0 = jnp.broadcast_to(state_m.reshape(DEC_BATCH, DEPTH, N_STREAM, 1),
                              (DEC_BATCH, DEPTH, N_STREAM, HEAD_DIM_M))
    ctx_c0 = jnp.zeros((1, N_STREAM, HEAD_DIM_M, HEAD_DIM_M), F32)
    ctx_n0 = jnp.zeros((1, N_STREAM, HEAD_DIM_M), F32)
    ctx_m0 = jnp.full((1, N_STREAM, HEAD_DIM_M), -jnp.inf, F32)

    ks, vs, cs, ns, ms = [], [], [], [], []
    for l in range(DEPTH):
        mod = mod_all[l]
        g1 = norm1_g[l][None, :]
        w_zb, w_zf, w_kv = _split_in_cols(w_in[l].astype(BF16))
        b_zb, b_zf, b_kv = _split_in_cols(b_in[l][None, :])
        zb = _in_proj(x, g1, mod, w_zb, b_zb, BF16, T_ALL)
        zf = _in_proj(x, g1, mod, w_zf, b_zf, F32, T_ALL)
        kv = _in_proj(x, g1, mod, w_kv, b_kv, F32, T_CTX)
        ya = (_ctx_attention(zb), _natt(zb, ck, cv, _natt_bias(rpb[l]), l))
        conv_w = (w_dw[l], b_dw[l][None, :], cln_g[l][None, :], cln_b[l][None, :])
        yc = (_conv(zf, *conv_w, SEQ, BATCH, 0), _conv(zf, *conv_w, DEC_SEQ, DEC_BATCH, T_CTX // DEC_SEQ))
        ng = mnorm_g[l][None, :]
        ym_ctx, c_l, n_l, m_l = _mlstm(zb, zf, ctx_c0, ctx_n0, ctx_m0, ng, SEQ, BATCH, 0, lambda b: (0,))
        ym_lat, _, _, _ = _mlstm(zb, zf, lat_c0, lat_n0, lat_m0, ng, DEC_SEQ, DEC_BATCH, T_CTX // DEC_SEQ,
                                 lambda b: (b, l))
        x = _merge(x, mod, ya, yc, (ym_ctx, ym_lat), zf, w_pa[l].astype(BF16), w_pc[l].astype(BF16),
                   w_pm[l].astype(BF16), w_out[l].astype(BF16))
        x = _moe(x, norm2_g[l][None, :], mod, w_router[l].T, b_router[l][:, None],
                 w_gu, b_gu, w_down, b_down, l)
        ks.append(kv[:, :D_A].reshape(BATCH, SEQ, N_HEADS_A, HEAD_DIM_A))
        vs.append(kv[:, D_A:].reshape(BATCH, SEQ, N_HEADS_A, HEAD_DIM_A))
        cs.append(c_l.reshape(BATCH, 2, N_HEADS_M, HEAD_DIM_M, HEAD_DIM_M))
        ns.append(n_l.reshape(BATCH, 2, N_HEADS_M, HEAD_DIM_M))
        ms.append(m_l[:, :, 0].reshape(BATCH, 2, N_HEADS_M))

    y = _final_norm(x, final_g[None, :])
    return (y[:T_CTX].reshape(BATCH, SEQ, D_MODEL), y[T_CTX:].reshape(DEC_BATCH, DEC_SEQ, D_MODEL),
            jnp.stack(ks, axis=1), jnp.stack(vs, axis=1), jnp.stack(cs, axis=1),
            jnp.stack(ns, axis=1), jnp.stack(ms, axis=1))
```

```python
import functools

import numpy as np
import jax
import jax.numpy as jnp
from jax import lax
from jax.experimental import pallas as pl
from jax.experimental.pallas import tpu as pltpu

F32 = jnp.float32
BF16 = jnp.bfloat16
HIGHEST = lax.Precision.HIGHEST

D_MODEL = 1024
BATCH = 16
SEQ = 256
DEPTH = 2
DEC_BATCH = 8
DEC_SEQ = 1024
PAST_LEN = 512
GRID_W = 64
N_HEADS_A = 8
HEAD_DIM_A = 64
D_A = N_HEADS_A * HEAD_DIM_A
WIN_ROWS = 8
WIN_COLS = 16
D_CONV = 512
CONV_WIDTH = 31
N_HEADS_M = 4
HEAD_DIM_M = 128
D_M = N_HEADS_M * HEAD_DIM_M
N_GATE_M = 4 * N_HEADS_M
CHUNK = 64
N_EXPERTS = 32
TOP_K = 4
D_EXPERT = 1024
SWIGLU_ALPHA = 1.702
SWIGLU_LIMIT = 7.0
RMS_EPS = 1e-6
LN_EPS = 1e-5
GATE_OFF = 3 * D_A + 2 * D_CONV + 4 * D_M
N_IN = GATE_OFF + N_GATE_M + 3 * D_MODEL

T_CTX = BATCH * SEQ
T_LAT = DEC_BATCH * DEC_SEQ
T_ALL = T_CTX + T_LAT
N_SEG = 1 + DEC_BATCH
SEG_PAD = 16
GRID_ROWS = DEC_SEQ // GRID_W
NEG = -1e30

N_ZB = 6 * 512
N_ZF = 5120
QB_QA, QB_KA, QB_VA, QB_QM, QB_KM, QB_VM = 0, 1, 2, 3, 4, 5
ZF_GA, ZF_GC, ZF_GM = 0, 1, 2
ZF_CU, ZF_CG, ZF_OM = 6, 7, 8
ZF_GATES = 36

TM_TOK = 512
TM_PROJ = 1024
TM_MOE = 512
N_TILES = T_ALL // TM_MOE
CHUNK_ROWS = 16
MXU_ROWS = 256
Q_TILE = -(-(TM_MOE * TOP_K + N_EXPERTS * (CHUNK_ROWS - 1)) // MXU_ROWS) * MXU_ROWS
CH_PER_TILE = Q_TILE // CHUNK_ROWS
E_GROUP = 256
CPG = E_GROUP // CHUNK_ROWS
MAX_CHUNKS = ((T_ALL * TOP_K + N_TILES * N_EXPERTS * (CHUNK_ROWS - 1)) // CHUNK_ROWS
              + N_EXPERTS * (CPG - 1))
VMEM_LIMIT = 56 * 1024 * 1024


def _cparams(sem=None):
    return pltpu.CompilerParams(dimension_semantics=sem, vmem_limit_bytes=VMEM_LIMIT)


def _seg_of_tile(i, tile):
    n_ctx = T_CTX // tile
    per_lat = DEC_SEQ // tile
    return jnp.where(i < n_ctx, 0, 1 + (i - n_ctx) // per_lat)


def _dot_nt(a, b):
    return lax.dot_general(a, b, (((1,), (1,)), ((), ())), preferred_element_type=F32)


def _sigmoid(x):
    return 1.0 / (1.0 + jnp.exp(-x))


def _mod_kernel(c_ref, w_ref, b_ref, o_ref):
    c = c_ref[...]
    s = c * _sigmoid(c)
    o_ref[...] = jnp.dot(s, w_ref[...], precision=HIGHEST, preferred_element_type=F32) + b_ref[...]


def _modulation(cond, w_mod, b_mod):
    tn = 1536
    return pl.pallas_call(
        _mod_kernel,
        grid=(DEPTH, 6 * D_MODEL // tn),
        in_specs=[pl.BlockSpec((SEG_PAD, D_MODEL), lambda l, j: (0, 0)),
                  pl.BlockSpec((None, D_MODEL, tn), lambda l, j: (l, 0, j)),
                  pl.BlockSpec((None, 1, tn), lambda l, j: (l, 0, j))],
        out_specs=pl.BlockSpec((None, SEG_PAD, tn), lambda l, j: (l, 0, j)),
        out_shape=jax.ShapeDtypeStruct((DEPTH, SEG_PAD, 6 * D_MODEL), F32),
        compiler_params=_cparams(("arbitrary", "arbitrary")),
        name="modulation",
    )(cond, w_mod, b_mod.reshape(DEPTH, 1, 6 * D_MODEL))


def _normmod(x, g, mod, shift_idx, scale_idx):
    y = x * lax.rsqrt(jnp.mean(x * x, axis=-1, keepdims=True) + RMS_EPS) * g
    return y * (1.0 + mod[scale_idx:scale_idx + 1, :]) + mod[shift_idx:shift_idx + 1, :]


def _in_proj_kernel(x_ref, g_ref, mod_ref, w_ref, b_ref, o_ref, h_scr):
    @pl.when(pl.program_id(1) == 0)
    def _():
        h_scr[...] = _normmod(x_ref[...], g_ref[...], mod_ref[...], 0, 1).astype(BF16)

    acc = jnp.dot(h_scr[...], w_ref[...], preferred_element_type=F32) + b_ref[...]
    o_ref[...] = acc.astype(o_ref.dtype)


def _in_proj(x, norm_g, mod, w, b, out_dtype, n_rows):
    tn = 1024
    n = w.shape[1]
    return pl.pallas_call(
        _in_proj_kernel,
        grid=(n_rows // TM_PROJ, n // tn),
        in_specs=[pl.BlockSpec((TM_PROJ, D_MODEL), lambda i, j: (i, 0)),
                  pl.BlockSpec((1, D_MODEL), lambda i, j: (0, 0)),
                  pl.BlockSpec((None, 6, D_MODEL), lambda i, j: (_seg_of_tile(i, TM_PROJ), 0, 0)),
                  pl.BlockSpec((D_MODEL, tn), lambda i, j: (0, j)),
                  pl.BlockSpec((1, tn), lambda i, j: (0, j))],
        out_specs=pl.BlockSpec((TM_PROJ, tn), lambda i, j: (i, j)),
        out_shape=jax.ShapeDtypeStruct((n_rows, n), out_dtype),
        scratch_shapes=[pltpu.VMEM((TM_PROJ, D_MODEL), BF16)],
        compiler_params=_cparams(("arbitrary", "arbitrary")),
        name="in_proj",
    )(x, norm_g, mod, w, b)


HEAD_PAIR = 2 * HEAD_DIM_A
ATT_SCALE = HEAD_DIM_A ** -0.5


def _pair_queries(q2):
    lo = lax.broadcasted_iota(jnp.int32, (1, HEAD_PAIR), 1) < HEAD_DIM_A
    q2 = q2 * ATT_SCALE
    zero = jnp.zeros_like(q2)
    return lo, jnp.concatenate([jnp.where(lo, q2, zero), jnp.where(lo, zero, q2)], axis=0)


def _unpair(lo, o_stacked):
    rows = o_stacked.shape[0] // 2
    return jnp.where(lo, o_stacked[:rows], o_stacked[rows:])


def _ctx_attn_kernel(q_ref, k_ref, v_ref, o_ref):
    for hp in range(N_HEADS_A // 2):
        sl = slice(hp * HEAD_PAIR, (hp + 1) * HEAD_PAIR)
        lo, qs = _pair_queries(q_ref[:, sl])
        s = _dot_nt(qs, k_ref[:, sl])
        p = jnp.exp(s - jnp.max(s, axis=-1, keepdims=True))
        l = jnp.sum(p, axis=-1, keepdims=True)
        o = jnp.dot(p.astype(BF16), v_ref[:, sl], preferred_element_type=F32) / l
        o_ref[:, sl] = _unpair(lo, o)


def _ctx_attention(zb):
    def spec(cb):
        return pl.BlockSpec((SEQ, D_A), lambda b: (b, cb))

    return pl.pallas_call(
        _ctx_attn_kernel,
        grid=(BATCH,),
        in_specs=[spec(QB_QA), spec(QB_KA), spec(QB_VA)],
        out_specs=pl.BlockSpec((SEQ, D_A), lambda b: (b, 0)),
        out_shape=jax.ShapeDtypeStruct((T_CTX, D_A), F32),
        compiler_params=_cparams(("arbitrary",)),
        name="ctx_attention",
    )(zb, zb, zb)


def _natt_kernel(q_ref, k_ref, v_ref, kc_ref, vc_ref, bias_ref, o_ref):
    r = pl.program_id(1)
    rs = jnp.clip(r - WIN_ROWS // 2, 0, GRID_ROWS - WIN_ROWS)
    start = pl.multiple_of(rs * GRID_W, GRID_W)
    band = WIN_ROWS * GRID_W
    for hp in range(N_HEADS_A // 2):
        sl = slice(hp * HEAD_PAIR, (hp + 1) * HEAD_PAIR)
        lo, qs = _pair_queries(q_ref[:, sl])
        bias = bias_ref[2 * hp:2 * hp + 2].reshape(2 * GRID_W, band)
        s_loc = _dot_nt(qs, k_ref[pl.ds(start, band), sl]) + bias
        s_ctx = _dot_nt(qs, kc_ref[:, sl])
        m = jnp.maximum(jnp.max(s_loc, axis=-1, keepdims=True), jnp.max(s_ctx, axis=-1, keepdims=True))
        p_loc = jnp.exp(s_loc - m)
        p_ctx = jnp.exp(s_ctx - m)
        l = jnp.sum(p_loc, axis=-1, keepdims=True) + jnp.sum(p_ctx, axis=-1, keepdims=True)
        o = (jnp.dot(p_loc.astype(BF16), v_ref[pl.ds(start, band), sl], preferred_element_type=F32)
             + jnp.dot(p_ctx.astype(BF16), vc_ref[:, sl], preferred_element_type=F32))
        o_ref[:, sl] = _unpair(lo, o / l)


def _band_row_offset(r):
    rs = jnp.clip(r - WIN_ROWS // 2, 0, GRID_ROWS - WIN_ROWS)
    return rs - r + WIN_ROWS - 1


def _natt_bias(rpb_l):
    qc = np.arange(GRID_W)
    kc = np.arange(GRID_W)
    cs = np.clip(qc - WIN_COLS // 2, 0, GRID_W - WIN_COLS)
    ok = (kc[None, :] >= cs[:, None]) & (kc[None, :] < cs[:, None] + WIN_COLS)
    dc = np.clip(kc[None, :] - qc[:, None] + WIN_COLS - 1, 0, 2 * WIN_COLS - 2)
    pick = (dc[None] == np.arange(2 * WIN_COLS - 1)[:, None, None]).astype(np.float32)
    toe = jnp.einsum('hdc,cqk->hdqk', rpb_l, jnp.asarray(pick), precision=HIGHEST)
    toe = jnp.where(jnp.asarray(ok)[None, None], toe, NEG)
    bands = [toe[:, d0:d0 + WIN_ROWS].transpose(0, 2, 1, 3).reshape(N_HEADS_A, GRID_W, WIN_ROWS * GRID_W)
             for d0 in range(WIN_ROWS)]
    return jnp.stack(bands, axis=0)


def _natt(zb, cache_k, cache_v, bias, layer):
    lat0 = T_CTX // DEC_SEQ
    row0 = T_CTX // GRID_W
    return pl.pallas_call(
        _natt_kernel,
        grid=(DEC_BATCH, GRID_ROWS),
        in_specs=[pl.BlockSpec((GRID_W, D_A), lambda b, r: (row0 + b * GRID_ROWS + r, QB_QA)),
                  pl.BlockSpec((DEC_SEQ, D_A), lambda b, r: (lat0 + b, QB_KA)),
                  pl.BlockSpec((DEC_SEQ, D_A), lambda b, r: (lat0 + b, QB_VA)),
                  pl.BlockSpec((None, None, PAST_LEN, D_A), lambda b, r: (b, layer, 0, 0)),
                  pl.BlockSpec((None, None, PAST_LEN, D_A), lambda b, r: (b, layer, 0, 0)),
                  pl.BlockSpec((None, N_HEADS_A, GRID_W, WIN_ROWS * GRID_W),
                               lambda b, r: (_band_row_offset(r), 0, 0, 0))],
        out_specs=pl.BlockSpec((GRID_W, D_A), lambda b, r: (b * GRID_ROWS + r, 0)),
        out_shape=jax.ShapeDtypeStruct((T_LAT, D_A), F32),
        compiler_params=_cparams(("arbitrary", "arbitrary")),
        name="nbr_attention",
    )(zb, zb, zb, cache_k, cache_v, bias)


CONV_HALO = 16
CONV_ROWS = 64


def _conv_kernel(u_ref, g_ref, w_ref, b_ref, lg_ref, lb_ref, o_ref, pad_scr, *, seq):
    zeros = jnp.zeros((CONV_HALO, D_CONV), F32)
    pad_scr[0:CONV_HALO, :] = zeros
    pad_scr[CONV_HALO + seq:2 * CONV_HALO + seq, :] = zeros
    pad_scr[CONV_HALO:CONV_HALO + seq, :] = u_ref[...] * _sigmoid(g_ref[...])
    first = CONV_HALO - CONV_WIDTH // 2
    for c in range(seq // CONV_ROWS):
        base = c * CONV_ROWS
        acc = jnp.broadcast_to(b_ref[...], (CONV_ROWS, D_CONV))
        for j in range(CONV_WIDTH):
            acc = acc + pad_scr[base + first + j:base + first + j + CONV_ROWS, :] * w_ref[j:j + 1, :]
        mu = jnp.mean(acc, axis=-1, keepdims=True)
        xc = acc - mu
        var = jnp.mean(xc * xc, axis=-1, keepdims=True)
        y = xc * lax.rsqrt(var + LN_EPS) * lg_ref[...] + lb_ref[...]
        o_ref[base:base + CONV_ROWS, :] = y * _sigmoid(y)


def _conv(z, w_dw, b_dw, ln_g, ln_b, seq, n_seq, row_block0):
    def vec():
        return pl.BlockSpec((1, D_CONV), lambda b: (0, 0))

    return pl.pallas_call(
        functools.partial(_conv_kernel, seq=seq),
        grid=(n_seq,),
        in_specs=[pl.BlockSpec((seq, D_CONV), lambda b: (row_block0 + b, ZF_CU)),
                  pl.BlockSpec((seq, D_CONV), lambda b: (row_block0 + b, ZF_CG)),
                  pl.BlockSpec((CONV_WIDTH, D_CONV), lambda b: (0, 0)),
                  vec(), vec(), vec()],
        out_specs=pl.BlockSpec((seq, D_CONV), lambda b: (b, 0)),
        out_shape=jax.ShapeDtypeStruct((n_seq * seq, D_CONV), F32),
        scratch_shapes=[pltpu.VMEM((seq + 2 * CONV_HALO, D_CONV), F32)],
        compiler_params=_cparams(("arbitrary",)),
        name="conformer_conv",
    )(z, z, w_dw, b_dw, ln_g, ln_b)


N_STREAM = 2 * N_HEADS_M
MCHUNK = 128


def _mlstm_kernel(q_ref, k_ref, v_ref, om_ref, gt_ref, c0_ref, n0_ref, m0_ref, ng_ref,
                  y_ref, c_out, n_out, m_out, hf_scr, hb_scr, c_scr, n_scr, m_scr, *, seq):
    nc = seq // MCHUNK
    c_scr[...] = c0_ref[...]
    n_scr[...] = n0_ref[...]
    m_scr[...] = m0_ref[...]
    ti = lax.broadcasted_iota(jnp.int32, (MCHUNK, MCHUNK), 0)
    si = lax.broadcasted_iota(jnp.int32, (MCHUNK, MCHUNK), 1)
    masks = (si <= ti, si >= ti)
    tris = tuple(mk.astype(F32) for mk in masks)
    kscale = HEAD_DIM_M ** -0.5

    def chunk_step(c, carry):
        for d in range(2):
            cidx = c if d == 0 else nc - 1 - c
            off = pl.multiple_of(cidx * MCHUNK, MCHUNK)
            last = MCHUNK - 1 if d == 0 else 0
            g = gt_ref[pl.ds(off, MCHUNK), :]
            lf = jnp.minimum(g, 0.0) - jnp.log(1.0 + jnp.exp(-jnp.abs(g)))
            cum = jnp.dot(tris[d], lf, precision=HIGHEST, preferred_element_type=F32)
            g_t = g.T
            cum_t = cum.T
            for h in range(N_HEADS_M):
                s_id = N_HEADS_M * d + h
                icol = 2 * N_HEADS_M * d + h
                fcol = icol + N_HEADS_M
                hs = slice(h * HEAD_DIM_M, (h + 1) * HEAD_DIM_M)
                b_col = cum[:, fcol:fcol + 1]
                i_col = g[:, icol:icol + 1]
                b_row = cum_t[fcol:fcol + 1, :]
                i_row = g_t[icol:icol + 1, :]
                b_last = b_col[last:last + 1, :]
                m_prev = m_scr[s_id:s_id + 1, 0:1]
                log_d = jnp.where(masks[d], b_col - b_row + i_row, NEG)
                inter = b_col + m_prev
                m_t = jnp.maximum(inter, jnp.max(log_d, axis=-1, keepdims=True))
                w_inter = jnp.exp(inter - m_t)
                qb = q_ref[pl.ds(off, MCHUNK), hs]
                kb = k_ref[pl.ds(off, MCHUNK), hs]
                vb = v_ref[pl.ds(off, MCHUNK), hs]
                c_prev = c_scr[s_id]
                n_prev = n_scr[s_id:s_id + 1, :]
                s_mat = _dot_nt(qb, kb) * (kscale * jnp.exp(log_d - m_t))
                num = (w_inter * _dot_nt(qb, c_prev.astype(BF16))
                       + jnp.dot(s_mat.astype(BF16), vb, preferred_element_type=F32))
                den = (w_inter * jnp.sum(qb.astype(F32) * n_prev, axis=-1, keepdims=True)
                       + jnp.sum(s_mat, axis=-1, keepdims=True))
                hh = num / jnp.maximum(jnp.abs(den), jnp.exp(-m_t))
                if d == 0:
                    hf_scr[pl.ds(off, MCHUNK), hs] = hh
                else:
                    hb_scr[pl.ds(off, MCHUNK), hs] = hh
                m_new = m_t[last:last + 1, :]
                w_prev = jnp.exp(b_last + m_prev - m_new)
                w_src = kscale * jnp.exp(b_last - b_col + i_col - m_new)
                upd = jnp.dot((w_src * vb.astype(F32)).T.astype(BF16), kb, preferred_element_type=F32)
                c_scr[s_id] = w_prev * c_prev + upd
                n_scr[s_id:s_id + 1, :] = (w_prev * n_prev
                                           + jnp.sum(w_src * kb.astype(F32), axis=0, keepdims=True))
                m_scr[s_id:s_id + 1, :] = jnp.broadcast_to(m_new, (1, HEAD_DIM_M))
        return carry

    lax.fori_loop(0, nc, chunk_step, 0)

    for h in range(N_HEADS_M):
        hs = slice(h * HEAD_DIM_M, (h + 1) * HEAD_DIM_M)
        hsum = hf_scr[:, hs] + hb_scr[:, hs]
        mu = jnp.mean(hsum, axis=-1, keepdims=True)
        xc = hsum - mu
        var = jnp.mean(xc * xc, axis=-1, keepdims=True)
        hn = xc * lax.rsqrt(var + LN_EPS) * ng_ref[:, hs]
        y_ref[:, hs] = _sigmoid(om_ref[:, hs]) * hn
    c_out[...] = c_scr[...]
    n_out[...] = n_scr[...]
    m_out[...] = m_scr[...]


def _mlstm(zb, zf, c0, n0, m0, norm_g, seq, n_seq, row_block0, state_map):
    lead = len(state_map(0))

    def zspec(cb):
        return pl.BlockSpec((seq, D_M), lambda b: (row_block0 + b, cb))

    def sspec(tail):
        return pl.BlockSpec((None,) * lead + tail, lambda b: state_map(b) + (0,) * len(tail))

    return pl.pallas_call(
        functools.partial(_mlstm_kernel, seq=seq),
        grid=(n_seq,),
        in_specs=[zspec(QB_QM), zspec(QB_KM), zspec(QB_VM), zspec(ZF_OM),
                  pl.BlockSpec((seq, 128), lambda b: (row_block0 + b, ZF_GATES)),
                  sspec((N_STREAM, HEAD_DIM_M, HEAD_DIM_M)),
                  sspec((N_STREAM, HEAD_DIM_M)),
                  sspec((N_STREAM, HEAD_DIM_M)),
                  pl.BlockSpec((1, D_M), lambda b: (0, 0))],
        out_specs=[pl.BlockSpec((seq, D_M), lambda b: (b, 0)),
                   pl.BlockSpec((None, N_STREAM, HEAD_DIM_M, HEAD_DIM_M), lambda b: (b, 0, 0, 0)),
                   pl.BlockSpec((None, N_STREAM, HEAD_DIM_M), lambda b: (b, 0, 0)),
                   pl.BlockSpec((None, N_STREAM, HEAD_DIM_M), lambda b: (b, 0, 0))],
        out_shape=[jax.ShapeDtypeStruct((n_seq * seq, D_M), F32),
                   jax.ShapeDtypeStruct((n_seq, N_STREAM, HEAD_DIM_M, HEAD_DIM_M), F32),
                   jax.ShapeDtypeStruct((n_seq, N_STREAM, HEAD_DIM_M), F32),
                   jax.ShapeDtypeStruct((n_seq, N_STREAM, HEAD_DIM_M), F32)],
        scratch_shapes=[pltpu.VMEM((seq, D_M), F32), pltpu.VMEM((seq, D_M), F32),
                        pltpu.VMEM((N_STREAM, HEAD_DIM_M, HEAD_DIM_M), F32),
                        pltpu.VMEM((N_STREAM, HEAD_DIM_M), F32),
                        pltpu.VMEM((N_STREAM, HEAD_DIM_M), F32)],
        compiler_params=_cparams(("arbitrary",)),
        name="mlstm",
    )(zb, zb, zb, zf, zf, c0, n0, m0, norm_g)


N_CTX_TILES = T_CTX // TM_TOK


def _merge_kernel(x_ref, mod_ref, ya_c, ya_l, yc_c, yc_l, ym_c, ym_l, ga_ref, gc_ref, gm_ref,
                  wa_ref, wc_ref, wm_ref, wo_ref, o_ref):
    is_ctx = pl.program_id(0) < N_CTX_TILES

    def branch(y_ctx, y_lat, g_ref, w_ref):
        y = jnp.where(is_ctx, y_ctx[...], y_lat[...]).astype(BF16)
        return _sigmoid(g_ref[...]) * jnp.dot(y, w_ref[...], preferred_element_type=F32)

    merged = (branch(ya_c, ya_l, ga_ref, wa_ref) + branch(yc_c, yc_l, gc_ref, wc_ref)
              + branch(ym_c, ym_l, gm_ref, wm_ref))
    mix = jnp.dot(merged.astype(BF16), wo_ref[...], preferred_element_type=F32)
    o_ref[...] = x_ref[...] + mod_ref[2:3, :] * mix


def _merge(x, mod, ya, yc, ym, zf, w_pa, w_pc, w_pm, w_out):
    def rows(width, cb=0):
        return pl.BlockSpec((TM_TOK, width), lambda i: (i, cb))

    def ctx_rows(width):
        return pl.BlockSpec((TM_TOK, width), lambda i: (jnp.minimum(i, N_CTX_TILES - 1), 0))

    def lat_rows(width):
        return pl.BlockSpec((TM_TOK, width), lambda i: (jnp.maximum(i - N_CTX_TILES, 0), 0))

    def full(shape):
        return pl.BlockSpec(shape, lambda i: (0, 0))

    return pl.pallas_call(
        _merge_kernel,
        grid=(T_ALL // TM_TOK,),
        in_specs=[rows(D_MODEL),
                  pl.BlockSpec((None, 6, D_MODEL), lambda i: (_seg_of_tile(i, TM_TOK), 0, 0)),
                  ctx_rows(D_A), lat_rows(D_A), ctx_rows(D_CONV), lat_rows(D_CONV),
                  ctx_rows(D_M), lat_rows(D_M),
                  rows(D_MODEL, ZF_GA), rows(D_MODEL, ZF_GC), rows(D_MODEL, ZF_GM),
                  full((D_A, D_MODEL)), full((D_CONV, D_MODEL)), full((D_M, D_MODEL)),
                  full((D_MODEL, D_MODEL))],
        out_specs=rows(D_MODEL),
        out_shape=jax.ShapeDtypeStruct((T_ALL, D_MODEL), F32),
        compiler_params=_cparams(("arbitrary",)),
        name="merge",
    )(x, mod, ya[0], ya[1], yc[0], yc[1], ym[0], ym[1], zf, zf, zf, w_pa, w_pc, w_pm, w_out)


def _route_sort_kernel(x_ref, g_ref, mod_ref, wr_ref, br_ref, xt_ref, pos_ref, gate_ref, nch_ref, seg_ref):
    h = _normmod(x_ref[...], g_ref[...], mod_ref[...], 3, 4)
    hb = h.astype(BF16)
    logits = _dot_nt(wr_ref[...].astype(BF16), hb) + br_ref[...]
    e_iota = lax.broadcasted_iota(jnp.int32, (N_EXPERTS, TM_MOE), 0).astype(F32)
    sels, vals = [], []
    l = logits
    for k in range(TOP_K):
        m = jnp.max(l, axis=0, keepdims=True)
        idx = jnp.min(jnp.where(l == m, e_iota, float(N_EXPERTS)), axis=0, keepdims=True)
        sel = e_iota == idx
        vals.append(m)
        sels.append(sel)
        l = jnp.where(sel, -jnp.inf, l)
    exps = [jnp.exp(v - vals[0]) for v in vals]
    tot = exps[0] + exps[1] + exps[2] + exps[3]
    onehot = jnp.zeros((N_EXPERTS, TM_MOE), F32)
    for k in range(TOP_K):
        gate_ref[k:k + 1, :] = exps[k] / tot
        onehot = onehot + sels[k].astype(F32)
    gate_ref[TOP_K:8, :] = jnp.zeros((8 - TOP_K, TM_MOE), F32)

    cnt = jnp.sum(onehot, axis=1, keepdims=True)
    nch = jnp.floor((cnt + (CHUNK_ROWS - 1)) / CHUNK_ROWS)
    ei = lax.broadcasted_iota(jnp.int32, (N_EXPERTS, N_EXPERTS), 0)
    ej = lax.broadcasted_iota(jnp.int32, (N_EXPERTS, N_EXPERTS), 1)
    seg = jnp.dot((ej < ei).astype(F32), jnp.broadcast_to(nch, (N_EXPERTS, 128)), precision=HIGHEST,
                  preferred_element_type=F32)
    nch_ref[...] = jnp.broadcast_to(nch, (N_EXPERTS, 128)).astype(jnp.int32)
    seg_ref[...] = seg.astype(jnp.int32)

    t_src = lax.broadcasted_iota(jnp.int32, (TM_MOE, TM_MOE), 0)
    t_dst = lax.broadcasted_iota(jnp.int32, (TM_MOE, TM_MOE), 1)
    before = (t_src < t_dst).astype(BF16)
    row_of = (seg[:, 0:1] * CHUNK_ROWS
              + jnp.dot(onehot.astype(BF16), before, preferred_element_type=F32))
    q_iota = lax.broadcasted_iota(jnp.int32, (Q_TILE, TM_MOE), 0)
    perm = jnp.zeros((Q_TILE, TM_MOE), F32)
    for k in range(TOP_K):
        q_k = jnp.sum(jnp.where(sels[k], row_of, 0.0), axis=0, keepdims=True).astype(jnp.int32)
        pos_ref[k:k + 1, :] = q_k
        perm = jnp.where(q_iota == q_k, 1.0, perm)
    pos_ref[TOP_K:8, :] = jnp.zeros((8 - TOP_K, TM_MOE), jnp.int32)
    xt_ref[...] = jnp.dot(perm.astype(BF16), hb, preferred_element_type=F32).astype(BF16)


def _route_sort(x, norm_g, mod, w_rt, b_r):
    tspec = pl.BlockSpec((8, TM_MOE), lambda i: (0, i))
    mspec = pl.BlockSpec((None, N_EXPERTS, 128), lambda i: (i, 0, 0))
    meta = jax.ShapeDtypeStruct((N_TILES, N_EXPERTS, 128), jnp.int32)
    return pl.pallas_call(
        _route_sort_kernel,
        grid=(N_TILES,),
        in_specs=[pl.BlockSpec((TM_MOE, D_MODEL), lambda i: (i, 0)),
                  pl.BlockSpec((1, D_MODEL), lambda i: (0, 0)),
                  pl.BlockSpec((None, 6, D_MODEL), lambda i: (_seg_of_tile(i, TM_MOE), 0, 0)),
                  pl.BlockSpec((N_EXPERTS, D_MODEL), lambda i: (0, 0)),
                  pl.BlockSpec((N_EXPERTS, 1), lambda i: (0, 0))],
        out_specs=[pl.BlockSpec((Q_TILE, D_MODEL), lambda i: (i, 0)), tspec, tspec, mspec, mspec],
        out_shape=[jax.ShapeDtypeStruct((N_TILES * Q_TILE, D_MODEL), BF16),
                   jax.ShapeDtypeStruct((8, T_ALL), jnp.int32), jax.ShapeDtypeStruct((8, T_ALL), F32),
                   meta, meta],
        compiler_params=_cparams(("arbitrary",)),
        name="moe_route_sort",
    )(x, norm_g, mod, w_rt, b_r)


def _expert_kernel(nch_ref, seg_ref, wgu_ref, bgu_ref, wd_ref, bd_ref, xt_ref, yt_ref,
                   wgu_scr, wd_scr, xbuf, ybuf, row_scr, gstart_scr, gsem, ssem):
    del xt_ref
    e = pl.program_id(0)

    @pl.when(e == 0)
    def _():
        xbuf[...] = jnp.zeros_like(xbuf)

        def per_expert(ee, cnt):
            gstart_scr[ee] = cnt // CPG

            def per_tile(t, cnt):
                first = (t * CH_PER_TILE + seg_ref[t * N_EXPERTS + ee]) * CHUNK_ROWS

                def per_chunk(j, cnt):
                    row_scr[cnt] = first + j * CHUNK_ROWS
                    return cnt + 1

                return lax.fori_loop(0, nch_ref[t * N_EXPERTS + ee], per_chunk, cnt)

            cnt = lax.fori_loop(0, N_TILES, per_tile, cnt)
            padded = (cnt + CPG - 1) // CPG * CPG

            def pad(i, carry):
                row_scr[i] = -1
                return carry

            lax.fori_loop(cnt, padded, pad, 0)
            return padded

        total = lax.fori_loop(0, N_EXPERTS, per_expert, 0)
        gstart_scr[N_EXPERTS] = total // CPG

    g_first = gstart_scr[e]
    g_end = gstart_scr[e + 1]
    g_total = gstart_scr[N_EXPERTS]

    def copy_in(g, c, row):
        slot = g % 2
        return pltpu.make_async_copy(yt_ref.at[pl.ds(pl.multiple_of(row, CHUNK_ROWS), CHUNK_ROWS), :],
                                     xbuf.at[slot, c * CHUNK_ROWS:(c + 1) * CHUNK_ROWS, :], gsem.at[slot])

    def copy_out(g, c, row):
        slot = g % 2
        return pltpu.make_async_copy(ybuf.at[slot, c * CHUNK_ROWS:(c + 1) * CHUNK_ROWS, :],
                                     yt_ref.at[pl.ds(pl.multiple_of(row, CHUNK_ROWS), CHUNK_ROWS), :],
                                     ssem.at[slot])

    def for_chunks(g, fn):
        for c in range(CPG):
            row = row_scr[g * CPG + c]

            @pl.when(row >= 0)
            def _(c=c, row=row):
                fn(g, c, row)

    def start_in(g):
        for_chunks(g, lambda g, c, row: copy_in(g, c, row).start())

    def wait_all(g, whole_group_copy, chunk_copy):
        full = row_scr[g * CPG + CPG - 1] >= 0

        @pl.when(full)
        def _():
            whole_group_copy(g % 2).wait()

        @pl.when(jnp.logical_not(full))
        def _():
            for_chunks(g, lambda g, c, row: chunk_copy(g, c, row).wait())

    def wait_in(g):
        wait_all(g, lambda slot: pltpu.make_async_copy(yt_ref.at[pl.ds(0, E_GROUP), :], xbuf.at[slot],
                                                       gsem.at[slot]), copy_in)

    def start_out(g):
        for_chunks(g, lambda g, c, row: copy_out(g, c, row).start())

    def wait_out(g):
        wait_all(g, lambda slot: pltpu.make_async_copy(ybuf.at[slot], yt_ref.at[pl.ds(0, E_GROUP), :],
                                                       ssem.at[slot]), copy_out)

    @pl.when(jnp.logical_and(e == 0, g_total > 0))
    def _():
        start_in(0)

    @pl.when(g_end > g_first)
    def _():
        wgu_scr[...] = wgu_ref[...].astype(BF16)
        wd_scr[...] = wd_ref[...].astype(BF16)

    def group_step(g, carry):
        slot = g % 2

        @pl.when(g + 1 < g_total)
        def _():
            start_in(g + 1)

        wait_in(g)

        @pl.when(g >= 2)
        def _():
            wait_out(g - 2)

        hgu = jnp.dot(xbuf[slot], wgu_scr[...], preferred_element_type=F32) + bgu_ref[...]
        h_glu = jnp.minimum(hgu[:, :D_EXPERT], SWIGLU_LIMIT)
        h_lin = jnp.clip(hgu[:, D_EXPERT:], -SWIGLU_LIMIT, SWIGLU_LIMIT)
        act = (h_lin + 1.0) * (h_glu * _sigmoid(SWIGLU_ALPHA * h_glu))
        y = jnp.dot(act.astype(BF16), wd_scr[...], preferred_element_type=F32) + bd_ref[...]
        ybuf[slot] = y.astype(BF16)
        start_out(g)
        return carry

    lax.fori_loop(g_first, g_end, group_step, 0)

    @pl.when(e == N_EXPERTS - 1)
    def _():
        @pl.when(g_total >= 2)
        def _():
            wait_out(g_total - 2)

        @pl.when(g_total >= 1)
        def _():
            wait_out(g_total - 1)


def _experts(nch_flat, seg_flat, xt, w_gu, b_gu, w_down, b_down, layer):
    return pl.pallas_call(
        _expert_kernel,
        grid_spec=pltpu.PrefetchScalarGridSpec(
            num_scalar_prefetch=2,
            grid=(N_EXPERTS,),
            in_specs=[pl.BlockSpec((None, None, D_MODEL, 2 * D_EXPERT), lambda e, n, s: (layer, e, 0, 0)),
                      pl.BlockSpec((None, None, 1, 2 * D_EXPERT), lambda e, n, s: (layer, e, 0, 0)),
                      pl.BlockSpec((None, None, D_EXPERT, D_MODEL), lambda e, n, s: (layer, e, 0, 0)),
                      pl.BlockSpec((None, None, 1, D_MODEL), lambda e, n, s: (layer, e, 0, 0)),
                      pl.BlockSpec(memory_space=pl.ANY)],
            out_specs=pl.BlockSpec(memory_space=pl.ANY),
            scratch_shapes=[pltpu.VMEM((D_MODEL, 2 * D_EXPERT), BF16),
                            pltpu.VMEM((D_EXPERT, D_MODEL), BF16),
                            pltpu.VMEM((2, E_GROUP, D_MODEL), BF16),
                            pltpu.VMEM((2, E_GROUP, D_MODEL), BF16),
                            pltpu.SMEM((MAX_CHUNKS,), jnp.int32),
                            pltpu.SMEM((N_EXPERTS + 1,), jnp.int32),
                            pltpu.SemaphoreType.DMA((2,)),
                            pltpu.SemaphoreType.DMA((2,))]),
        out_shape=jax.ShapeDtypeStruct((N_TILES * Q_TILE, D_MODEL), BF16),
        input_output_aliases={6: 0},
        compiler_params=_cparams(("arbitrary",)),
        name="moe_experts",
    )(nch_flat, seg_flat, w_gu, b_gu, w_down, b_down, xt)


def _combine_kernel(x_ref, mod_ref, pos_ref, gate_ref, yt_ref, o_ref):
    lane = lax.broadcasted_iota(jnp.int32, (TM_MOE, Q_TILE), 1)
    sel = jnp.zeros((TM_MOE, Q_TILE), F32)
    for k in range(TOP_K):
        sel = jnp.where(lane == pos_ref[:, k:k + 1], gate_ref[:, k:k + 1], sel)
    acc = jnp.dot(sel.astype(BF16), yt_ref[...], preferred_element_type=F32)
    o_ref[...] = x_ref[...] + mod_ref[5:6, :] * acc


def _combine(x, mod, pos_t, gate_t, yt):
    return pl.pallas_call(
        _combine_kernel,
        grid=(N_TILES,),
        in_specs=[pl.BlockSpec((TM_MOE, D_MODEL), lambda i: (i, 0)),
                  pl.BlockSpec((None, 6, D_MODEL), lambda i: (_seg_of_tile(i, TM_MOE), 0, 0)),
                  pl.BlockSpec((TM_MOE, 8), lambda i: (i, 0)),
                  pl.BlockSpec((TM_MOE, 8), lambda i: (i, 0)),
                  pl.BlockSpec((Q_TILE, D_MODEL), lambda i: (i, 0))],
        out_specs=pl.BlockSpec((TM_MOE, D_MODEL), lambda i: (i, 0)),
        out_shape=jax.ShapeDtypeStruct((T_ALL, D_MODEL), F32),
        compiler_params=_cparams(("arbitrary",)),
        name="moe_combine",
    )(x, mod, pos_t, gate_t, yt)


def _moe(x, norm_g, mod, w_rt, b_r, w_gu, b_gu, w_down, b_down, layer):
    xt, pos, gate, nch, seg = _route_sort(x, norm_g, mod, w_rt, b_r)
    yt = _experts(nch[:, :, 0].reshape(-1), seg[:, :, 0].reshape(-1), xt, w_gu,
                  b_gu.reshape(DEPTH, N_EXPERTS, 1, 2 * D_EXPERT), w_down,
                  b_down.reshape(DEPTH, N_EXPERTS, 1, D_MODEL), layer)
    return _combine(x, mod, pos.T, gate.T, yt)


def _final_norm_kernel(x_ref, g_ref, o_ref):
    x = x_ref[...]
    o_ref[...] = x * lax.rsqrt(jnp.mean(x * x, axis=-1, keepdims=True) + RMS_EPS) * g_ref[...]


def _final_norm(x, g):
    return pl.pallas_call(
        _final_norm_kernel,
        grid=(T_ALL // TM_TOK,),
        in_specs=[pl.BlockSpec((TM_TOK, D_MODEL), lambda i: (i, 0)),
                  pl.BlockSpec((1, D_MODEL), lambda i: (0, 0))],
        out_specs=pl.BlockSpec((TM_TOK, D_MODEL), lambda i: (i, 0)),
        out_shape=jax.ShapeDtypeStruct((T_ALL, D_MODEL), F32),
        compiler_params=_cparams(("arbitrary",)),
        name="final_norm",
    )(x, g)


def _split_in_cols(w):
    conv0, mq0, om0 = 3 * D_A, 3 * D_A + 2 * D_CONV, 3 * D_A + 2 * D_CONV + 3 * D_M
    gates_end = GATE_OFF + N_GATE_M
    pad = jnp.zeros(w.shape[:-1] + (N_ZF - (3 * D_MODEL + 2 * D_CONV + D_M + N_GATE_M),), w.dtype)
    zb = jnp.concatenate([w[..., :conv0], w[..., mq0:om0]], axis=-1)
    zf = jnp.concatenate([w[..., gates_end:], w[..., conv0:mq0], w[..., om0:GATE_OFF],
                          w[..., GATE_OFF:gates_end], pad], axis=-1)
    kv = w[..., D_A:3 * D_A]
    return zb, zf, kv


def kernel(x_prompt, x_sample, cache_k, cache_v, state_C, state_n, state_m, c, c_ctx, norm1_g, w_mod, b_mod, w_in, b_in, rpb, w_dw, b_dw, cln_g, cln_b, mnorm_g, w_pa, w_pc, w_pm, w_out, norm2_g, w_router, b_router, w_gu, b_gu, w_down, b_down, final_g):
    cond = jnp.concatenate([c_ctx[None, :], c, jnp.zeros((SEG_PAD - N_SEG, D_MODEL), F32)], axis=0)
    mod_all = _modulation(cond, w_mod, b_mod).reshape(DEPTH, SEG_PAD, 6, D_MODEL)

    x = jnp.concatenate([x_prompt.reshape(T_CTX, D_MODEL), x_sample.reshape(T_LAT, D_MODEL)], axis=0)
    ck = cache_k.reshape(DEC_BATCH, DEPTH, PAST_LEN, D_A).astype(BF16)
    cv = cache_v.reshape(DEC_BATCH, DEPTH, PAST_LEN, D_A).astype(BF16)
    lat_c0 = state_C.reshape(DEC_BATCH, DEPTH, N_STREAM, HEAD_DIM_M, HEAD_DIM_M)
    lat_n0 = state_n.reshape(DEC_BATCH, DEPTH, N_STREAM, HEAD_DIM_M)
    lat_m0 = jnp.broadcast_to(state_m.reshape(DEC_BATCH, DEPTH, N_STREAM, 1),
                              (DEC_BATCH, DEPTH, N_STREAM, HEAD_DIM_M))
    ctx_c0 = jnp.zeros((1, N_STREAM, HEAD_DIM_M, HEAD_DIM_M), F32)
    ctx_n0 = jnp.zeros((1, N_STREAM, HEAD_DIM_M), F32)
    ctx_m0 = jnp.full((1, N_STREAM, HEAD_DIM_M), -jnp.inf, F32)

    ks, vs, cs, ns, ms = [], [], [], [], []
    for l in range(DEPTH):
        mod = mod_all[l]
        g1 = norm1_g[l][None, :]
        w_zb, w_zf, w_kv = _split_in_cols(w_in[l].astype(BF16))
        b_zb, b_zf, b_kv = _split_in_cols(b_in[l][None, :])
        zb = _in_proj(x, g1, mod, w_zb, b_zb, BF16, T_ALL)
        zf = _in_proj(x, g1, mod, w_zf, b_zf, F32, T_ALL)
        kv = _in_proj(x, g1, mod, w_kv, b_kv, F32, T_CTX)
        ya = (_ctx_attention(zb), _natt(zb, ck, cv, _natt_bias(rpb[l]), l))
        conv_w = (w_dw[l], b_dw[l][None, :], cln_g[l][None, :], cln_b[l][None, :])
        yc = (_conv(zf, *conv_w, SEQ, BATCH, 0), _conv(zf, *conv_w, DEC_SEQ, DEC_BATCH, T_CTX // DEC_SEQ))
        ng = mnorm_g[l][None, :]
        ym_ctx, c_l, n_l, m_l = _mlstm(zb, zf, ctx_c0, ctx_n0, ctx_m0, ng, SEQ, BATCH, 0, lambda b: (0,))
        ym_lat, _, _, _ = _mlstm(zb, zf, lat_c0, lat_n0, lat_m0, ng, DEC_SEQ, DEC_BATCH, T_CTX // DEC_SEQ,
                                 lambda b: (b, l))
        x = _merge(x, mod, ya, yc, (ym_ctx, ym_lat), zf, w_pa[l].astype(BF16), w_pc[l].astype(BF16),
                   w_pm[l].astype(BF16), w_out[l].astype(BF16))
        x = _moe(x, norm2_g[l][None, :], mod, w_router[l].T, b_router[l][:, None],
                 w_gu, b_gu, w_down, b_down, l)
        ks.append(kv[:, :D_A].reshape(BATCH, SEQ, N_HEADS_A, HEAD_DIM_A))
        vs.append(kv[:, D_A:].reshape(BATCH, SEQ, N_HEADS_A, HEAD_DIM_A))
        cs.append(c_l.reshape(BATCH, 2, N_HEADS_M, HEAD_DIM_M, HEAD_DIM_M))
        ns.append(n_l.reshape(BATCH, 2, N_HEADS_M, HEAD_DIM_M))
        ms.append(m_l[:, :, 0].reshape(BATCH, 2, N_HEADS_M))

    y = _final_norm(x, final_g[None, :])
    return (y[:T_CTX].reshape(BATCH, SEQ, D_MODEL), y[T_CTX:].reshape(DEC_BATCH, DEC_SEQ, D_MODEL),
            jnp.stack(ks, axis=1), jnp.stack(vs, axis=1), jnp.stack(cs, axis=1),
            jnp.stack(ns, axis=1), jnp.stack(ms, axis=1))
```

```python
import functools

import numpy as np
import jax
import jax.numpy as jnp
from jax import lax
from jax.experimental import pallas as pl
from jax.experimental.pallas import tpu as pltpu

F32 = jnp.float32
BF16 = jnp.bfloat16
HIGHEST = lax.Precision.HIGHEST

D_MODEL = 1024
BATCH = 16
SEQ = 256
DEPTH = 2
DEC_BATCH = 8
DEC_SEQ = 1024
PAST_LEN = 512
GRID_W = 64
N_HEADS_A = 8
HEAD_DIM_A = 64
D_A = N_HEADS_A * HEAD_DIM_A
WIN_ROWS = 8
WIN_COLS = 16
D_CONV = 512
CONV_WIDTH = 31
N_HEADS_M = 4
HEAD_DIM_M = 128
D_M = N_HEADS_M * HEAD_DIM_M
N_GATE_M = 4 * N_HEADS_M
CHUNK = 64
N_EXPERTS = 32
TOP_K = 4
D_EXPERT = 1024
SWIGLU_ALPHA = 1.702
SWIGLU_LIMIT = 7.0
RMS_EPS = 1e-6
LN_EPS = 1e-5
GATE_OFF = 3 * D_A + 2 * D_CONV + 4 * D_M
N_IN = GATE_OFF + N_GATE_M + 3 * D_MODEL

T_CTX = BATCH * SEQ
T_LAT = DEC_BATCH * DEC_SEQ
T_ALL = T_CTX + T_LAT
N_SEG = 1 + DEC_BATCH
SEG_PAD = 16
GRID_ROWS = DEC_SEQ // GRID_W
NEG = -1e30

N_ZB = 6 * 512
N_ZG = 3 * D_MODEL
N_ZF = 1664
QB_QA, QB_KA, QB_VA, QB_QM, QB_KM, QB_VM = 0, 1, 2, 3, 4, 5
ZG_GA, ZG_GC, ZG_GM = 0, 1, 2
ZF_CU, ZF_CG, ZF_OM = 0, 1, 2
ZF_GATES = 12

TM_TOK = 512
TM_PROJ = 256
TM_MOE = 512
N_TILES = T_ALL // TM_MOE
CHUNK_ROWS = 16
MXU_ROWS = 256
Q_TILE = -(-(TM_MOE * TOP_K + N_EXPERTS * (CHUNK_ROWS - 1)) // MXU_ROWS) * MXU_ROWS
CH_PER_TILE = Q_TILE // CHUNK_ROWS
E_GROUP = 256
CPG = E_GROUP // CHUNK_ROWS
MAX_CHUNKS = ((T_ALL * TOP_K + N_TILES * N_EXPERTS * (CHUNK_ROWS - 1)) // CHUNK_ROWS
              + N_EXPERTS * (CPG - 1))
VMEM_LIMIT = 60 * 1024 * 1024


def _cparams(sem=None):
    return pltpu.CompilerParams(dimension_semantics=sem, vmem_limit_bytes=VMEM_LIMIT)


def _seg_of_tile(i, tile):
    n_ctx = T_CTX // tile
    per_lat = DEC_SEQ // tile
    return jnp.where(i < n_ctx, 0, 1 + (i - n_ctx) // per_lat)


def _dot_nt(a, b):
    return lax.dot_general(a, b, (((1,), (1,)), ((), ())), preferred_element_type=F32)


def _sigmoid(x):
    return 1.0 / (1.0 + jnp.exp(-x))


def _mod_kernel(c_ref, w_ref, b_ref, o_ref):
    c = c_ref[...]
    s = c * _sigmoid(c)
    o_ref[...] = jnp.dot(s, w_ref[...], precision=HIGHEST, preferred_element_type=F32) + b_ref[...]


def _modulation(cond, w_mod, b_mod):
    tn = 1536
    return pl.pallas_call(
        _mod_kernel,
        grid=(DEPTH, 6 * D_MODEL // tn),
        in_specs=[pl.BlockSpec((SEG_PAD, D_MODEL), lambda l, j: (0, 0)),
                  pl.BlockSpec((None, D_MODEL, tn), lambda l, j: (l, 0, j)),
                  pl.BlockSpec((None, 1, tn), lambda l, j: (l, 0, j))],
        out_specs=pl.BlockSpec((None, SEG_PAD, tn), lambda l, j: (l, 0, j)),
        out_shape=jax.ShapeDtypeStruct((DEPTH, SEG_PAD, 6 * D_MODEL), F32),
        compiler_params=_cparams(("arbitrary", "arbitrary")),
        name="modulation",
    )(cond, w_mod, b_mod.reshape(DEPTH, 1, 6 * D_MODEL))


def _normmod(x, g, mod, shift_idx, scale_idx):
    y = x * lax.rsqrt(jnp.mean(x * x, axis=-1, keepdims=True) + RMS_EPS) * g
    return y * (1.0 + mod[scale_idx:scale_idx + 1, :]) + mod[shift_idx:shift_idx + 1, :]


PROJ_CTX_TILES = T_CTX // TM_PROJ


def _in_proj_kernel(x_ref, g_ref, mod_ref, wb_ref, bb_ref, wg_ref, bg_ref, wf_ref, bf_ref,
                    zb_ref, zg_ref, zf_ref, kv_ref):
    h = _normmod(x_ref[...], g_ref[...], mod_ref[...], 0, 1).astype(BF16)
    acc = jnp.dot(h, wb_ref[...], preferred_element_type=F32) + bb_ref[...]
    zb_ref[...] = acc.astype(BF16)

    @pl.when(pl.program_id(0) < PROJ_CTX_TILES)
    def _():
        kv_ref[...] = acc[:, D_A:3 * D_A]

    gates = jnp.dot(h, wg_ref[...], preferred_element_type=F32) + bg_ref[...]
    zg_ref[...] = _sigmoid(gates).astype(BF16)
    zf_ref[...] = jnp.dot(h, wf_ref[...], preferred_element_type=F32) + bf_ref[...]


def _in_proj(x, norm_g, mod, w, b):
    def full(a):
        return pl.BlockSpec(a.shape, lambda i: (0, 0))

    def rows(n):
        return pl.BlockSpec((TM_PROJ, n), lambda i: (i, 0))

    return pl.pallas_call(
        _in_proj_kernel,
        grid=(T_ALL // TM_PROJ,),
        in_specs=[rows(D_MODEL),
                  pl.BlockSpec((1, D_MODEL), lambda i: (0, 0)),
                  pl.BlockSpec((None, 6, D_MODEL), lambda i: (_seg_of_tile(i, TM_PROJ), 0, 0)),
                  full(w[0]), full(b[0]), full(w[1]), full(b[1]), full(w[2]), full(b[2])],
        out_specs=[rows(N_ZB), rows(N_ZG), rows(N_ZF),
                   pl.BlockSpec((TM_PROJ, 2 * D_A), lambda i: (jnp.minimum(i, PROJ_CTX_TILES - 1), 0))],
        out_shape=[jax.ShapeDtypeStruct((T_ALL, N_ZB), BF16), jax.ShapeDtypeStruct((T_ALL, N_ZG), BF16),
                   jax.ShapeDtypeStruct((T_ALL, N_ZF), F32), jax.ShapeDtypeStruct((T_CTX, 2 * D_A), F32)],
        compiler_params=_cparams(("arbitrary",)),
        name="in_proj",
    )(x, norm_g, mod, w[0], b[0], w[1], b[1], w[2], b[2])


HEAD_PAIR = 2 * HEAD_DIM_A
ATT_SCALE = HEAD_DIM_A ** -0.5


def _pair_queries(q2):
    lo = lax.broadcasted_iota(jnp.int32, (1, HEAD_PAIR), 1) < HEAD_DIM_A
    q2 = q2 * ATT_SCALE
    zero = jnp.zeros_like(q2)
    return lo, jnp.concatenate([jnp.where(lo, q2, zero), jnp.where(lo, zero, q2)], axis=0)


def _unpair(lo, o_stacked):
    rows = o_stacked.shape[0] // 2
    return jnp.where(lo, o_stacked[:rows], o_stacked[rows:])


def _ctx_attn_kernel(q_ref, k_ref, v_ref, o_ref):
    for hp in range(N_HEADS_A // 2):
        sl = slice(hp * HEAD_PAIR, (hp + 1) * HEAD_PAIR)
        lo, qs = _pair_queries(q_ref[:, sl])
        s = _dot_nt(qs, k_ref[:, sl])
        p = jnp.exp(s - jnp.max(s, axis=-1, keepdims=True))
        l = jnp.sum(p, axis=-1, keepdims=True)
        o = jnp.dot(p.astype(BF16), v_ref[:, sl], preferred_element_type=F32) / l
        o_ref[:, sl] = _unpair(lo, o)


def _ctx_attention(zb):
    def spec(cb):
        return pl.BlockSpec((SEQ, D_A), lambda b: (b, cb))

    return pl.pallas_call(
        _ctx_attn_kernel,
        grid=(BATCH,),
        in_specs=[spec(QB_QA), spec(QB_KA), spec(QB_VA)],
        out_specs=pl.BlockSpec((SEQ, D_A), lambda b: (b, 0)),
        out_shape=jax.ShapeDtypeStruct((T_CTX, D_A), F32),
        compiler_params=_cparams(("arbitrary",)),
        name="ctx_attention",
    )(zb, zb, zb)


def _natt_kernel(q_ref, k_ref, v_ref, kc_ref, vc_ref, bias_ref, o_ref):
    r = pl.program_id(1)
    rs = jnp.clip(r - WIN_ROWS // 2, 0, GRID_ROWS - WIN_ROWS)
    start = pl.multiple_of(rs * GRID_W, GRID_W)
    band = WIN_ROWS * GRID_W
    for hp in range(N_HEADS_A // 2):
        sl = slice(hp * HEAD_PAIR, (hp + 1) * HEAD_PAIR)
        lo, qs = _pair_queries(q_ref[:, sl])
        bias = bias_ref[2 * hp:2 * hp + 2].reshape(2 * GRID_W, band)
        s_loc = _dot_nt(qs, k_ref[pl.ds(start, band), sl]) + bias
        s_ctx = _dot_nt(qs, kc_ref[:, sl])
        m = jnp.maximum(jnp.max(s_loc, axis=-1, keepdims=True), jnp.max(s_ctx, axis=-1, keepdims=True))
        p_loc = jnp.exp(s_loc - m)
        p_ctx = jnp.exp(s_ctx - m)
        l = jnp.sum(p_loc, axis=-1, keepdims=True) + jnp.sum(p_ctx, axis=-1, keepdims=True)
        o = (jnp.dot(p_loc.astype(BF16), v_ref[pl.ds(start, band), sl], preferred_element_type=F32)
             + jnp.dot(p_ctx.astype(BF16), vc_ref[:, sl], preferred_element_type=F32))
        o_ref[:, sl] = _unpair(lo, o / l)


def _band_row_offset(r):
    rs = jnp.clip(r - WIN_ROWS // 2, 0, GRID_ROWS - WIN_ROWS)
    return rs - r + WIN_ROWS - 1


def _natt_bias(rpb_l):
    qc = np.arange(GRID_W)
    kc = np.arange(GRID_W)
    cs = np.clip(qc - WIN_COLS // 2, 0, GRID_W - WIN_COLS)
    ok = (kc[None, :] >= cs[:, None]) & (kc[None, :] < cs[:, None] + WIN_COLS)
    dc = np.clip(kc[None, :] - qc[:, None] + WIN_COLS - 1, 0, 2 * WIN_COLS - 2)
    pick = (dc[None] == np.arange(2 * WIN_COLS - 1)[:, None, None]).astype(np.float32)
    toe = jnp.einsum('hdc,cqk->hdqk', rpb_l, jnp.asarray(pick), precision=HIGHEST)
    toe = jnp.where(jnp.asarray(ok)[None, None], toe, NEG)
    bands = [toe[:, d0:d0 + WIN_ROWS].transpose(0, 2, 1, 3).reshape(N_HEADS_A, GRID_W, WIN_ROWS * GRID_W)
             for d0 in range(WIN_ROWS)]
    return jnp.stack(bands, axis=0)


def _natt(zb, cache_k, cache_v, bias, layer):
    lat0 = T_CTX // DEC_SEQ
    row0 = T_CTX // GRID_W
    return pl.pallas_call(
        _natt_kernel,
        grid=(DEC_BATCH, GRID_ROWS),
        in_specs=[pl.BlockSpec((GRID_W, D_A), lambda b, r: (row0 + b * GRID_ROWS + r, QB_QA)),
                  pl.BlockSpec((DEC_SEQ, D_A), lambda b, r: (lat0 + b, QB_KA)),
                  pl.BlockSpec((DEC_SEQ, D_A), lambda b, r: (lat0 + b, QB_VA)),
                  pl.BlockSpec((None, None, PAST_LEN, D_A), lambda b, r: (b, layer, 0, 0)),
                  pl.BlockSpec((None, None, PAST_LEN, D_A), lambda b, r: (b, layer, 0, 0)),
                  pl.BlockSpec((None, N_HEADS_A, GRID_W, WIN_ROWS * GRID_W),
                               lambda b, r: (_band_row_offset(r), 0, 0, 0))],
        out_specs=pl.BlockSpec((GRID_W, D_A), lambda b, r: (b * GRID_ROWS + r, 0)),
        out_shape=jax.ShapeDtypeStruct((T_LAT, D_A), F32),
        compiler_params=_cparams(("arbitrary", "arbitrary")),
        name="nbr_attention",
    )(zb, zb, zb, cache_k, cache_v, bias)


CONV_HALO = 16
CONV_ROWS = 64


def _conv_kernel(u_ref, g_ref, w_ref, b_ref, lg_ref, lb_ref, o_ref, pad_scr, *, seq):
    zeros = jnp.zeros((CONV_HALO, D_CONV), F32)
    pad_scr[0:CONV_HALO, :] = zeros
    pad_scr[CONV_HALO + seq:2 * CONV_HALO + seq, :] = zeros
    pad_scr[CONV_HALO:CONV_HALO + seq, :] = u_ref[...] * _sigmoid(g_ref[...])
    first = CONV_HALO - CONV_WIDTH // 2
    for c in range(seq // CONV_ROWS):
        base = c * CONV_ROWS
        acc = jnp.broadcast_to(b_ref[...], (CONV_ROWS, D_CONV))
        for j in range(CONV_WIDTH):
            acc = acc + pad_scr[base + first + j:base + first + j + CONV_ROWS, :] * w_ref[j:j + 1, :]
        mu = jnp.mean(acc, axis=-1, keepdims=True)
        xc = acc - mu
        var = jnp.mean(xc * xc, axis=-1, keepdims=True)
        y = xc * lax.rsqrt(var + LN_EPS) * lg_ref[...] + lb_ref[...]
        o_ref[base:base + CONV_ROWS, :] = y * _sigmoid(y)


def _conv(z, w_dw, b_dw, ln_g, ln_b, seq, n_seq, row_block0):
    def vec():
        return pl.BlockSpec((1, D_CONV), lambda b: (0, 0))

    return pl.pallas_call(
        functools.partial(_conv_kernel, seq=seq),
        grid=(n_seq,),
        in_specs=[pl.BlockSpec((seq, D_CONV), lambda b: (row_block0 + b, ZF_CU)),
                  pl.BlockSpec((seq, D_CONV), lambda b: (row_block0 + b, ZF_CG)),
                  pl.BlockSpec((CONV_WIDTH, D_CONV), lambda b: (0, 0)),
                  vec(), vec(), vec()],
        out_specs=pl.BlockSpec((seq, D_CONV), lambda b: (b, 0)),
        out_shape=jax.ShapeDtypeStruct((n_seq * seq, D_CONV), F32),
        scratch_shapes=[pltpu.VMEM((seq + 2 * CONV_HALO, D_CONV), F32)],
        compiler_params=_cparams(("arbitrary",)),
        name="conformer_conv",
    )(z, z, w_dw, b_dw, ln_g, ln_b)


N_STREAM = 2 * N_HEADS_M
MCHUNK = 128


def _mlstm_kernel(q_ref, k_ref, v_ref, om_ref, gt_ref, c0_ref, n0_ref, m0_ref, ng_ref,
                  y_ref, c_out, n_out, m_out, hf_scr, hb_scr, c_scr, n_scr, m_scr, *, seq):
    nc = seq // MCHUNK
    c_scr[...] = c0_ref[...]
    n_scr[...] = n0_ref[...]
    m_scr[...] = m0_ref[...]
    ti = lax.broadcasted_iota(jnp.int32, (MCHUNK, MCHUNK), 0)
    si = lax.broadcasted_iota(jnp.int32, (MCHUNK, MCHUNK), 1)
    masks = (si <= ti, si >= ti)
    tris = tuple(mk.astype(F32) for mk in masks)
    kscale = HEAD_DIM_M ** -0.5

    def chunk_step(c, carry):
        for d in range(2):
            cidx = c if d == 0 else nc - 1 - c
            off = pl.multiple_of(cidx * MCHUNK, MCHUNK)
            last = MCHUNK - 1 if d == 0 else 0
            g = gt_ref[pl.ds(off, MCHUNK), :]
            lf = jnp.minimum(g, 0.0) - jnp.log(1.0 + jnp.exp(-jnp.abs(g)))
            cum = jnp.dot(tris[d], lf, precision=HIGHEST, preferred_element_type=F32)
            g_t = g.T
            cum_t = cum.T
            for h in range(N_HEADS_M):
                s_id = N_HEADS_M * d + h
                icol = 2 * N_HEADS_M * d + h
                fcol = icol + N_HEADS_M
                hs = slice(h * HEAD_DIM_M, (h + 1) * HEAD_DIM_M)
                b_col = cum[:, fcol:fcol + 1]
                i_col = g[:, icol:icol + 1]
                b_row = cum_t[fcol:fcol + 1, :]
                i_row = g_t[icol:icol + 1, :]
                b_last = b_col[last:last + 1, :]
                m_prev = m_scr[s_id:s_id + 1, 0:1]
                log_d = jnp.where(masks[d], b_col - b_row + i_row, NEG)
                inter = b_col + m_prev
                m_t = jnp.maximum(inter, jnp.max(log_d, axis=-1, keepdims=True))
                w_inter = jnp.exp(inter - m_t)
                qb = q_ref[pl.ds(off, MCHUNK), hs]
                kb = k_ref[pl.ds(off, MCHUNK), hs]
                vb = v_ref[pl.ds(off, MCHUNK), hs]
                c_prev = c_scr[s_id]
                n_prev = n_scr[s_id:s_id + 1, :]
                s_mat = _dot_nt(qb, kb) * (kscale * jnp.exp(log_d - m_t))
                num = (w_inter * _dot_nt(qb, c_prev.astype(BF16))
                       + jnp.dot(s_mat.astype(BF16), vb, preferred_element_type=F32))
                den = (w_inter * jnp.sum(qb.astype(F32) * n_prev, axis=-1, keepdims=True)
                       + jnp.sum(s_mat, axis=-1, keepdims=True))
                hh = num / jnp.maximum(jnp.abs(den), jnp.exp(-m_t))
                if d == 0:
                    hf_scr[pl.ds(off, MCHUNK), hs] = hh
                else:
                    hb_scr[pl.ds(off, MCHUNK), hs] = hh
                m_new = m_t[last:last + 1, :]
                w_prev = jnp.exp(b_last + m_prev - m_new)
                w_src = kscale * jnp.exp(b_last - b_col + i_col - m_new)
                upd = jnp.dot((w_src * vb.astype(F32)).T.astype(BF16), kb, preferred_element_type=F32)
                c_scr[s_id] = w_prev * c_prev + upd
                n_scr[s_id:s_id + 1, :] = (w_prev * n_prev
                                           + jnp.sum(w_src * kb.astype(F32), axis=0, keepdims=True))
                m_scr[s_id:s_id + 1, :] = jnp.broadcast_to(m_new, (1, HEAD_DIM_M))
        return carry

    lax.fori_loop(0, nc, chunk_step, 0)

    for h in range(N_HEADS_M):
        hs = slice(h * HEAD_DIM_M, (h + 1) * HEAD_DIM_M)
        hsum = hf_scr[:, hs] + hb_scr[:, hs]
        mu = jnp.mean(hsum, axis=-1, keepdims=True)
        xc = hsum - mu
        var = jnp.mean(xc * xc, axis=-1, keepdims=True)
        hn = xc * lax.rsqrt(var + LN_EPS) * ng_ref[:, hs]
        y_ref[:, hs] = _sigmoid(om_ref[:, hs]) * hn
    c_out[...] = c_scr[...]
    n_out[...] = n_scr[...]
    m_out[...] = m_scr[...]


def _mlstm(zb, zf, c0, n0, m0, norm_g, seq, n_seq, row_block0, state_map):
    lead = len(state_map(0))

    def zspec(cb):
        return pl.BlockSpec((seq, D_M), lambda b: (row_block0 + b, cb))

    def sspec(tail):
        return pl.BlockSpec((None,) * lead + tail, lambda b: state_map(b) + (0,) * len(tail))

    return pl.pallas_call(
        functools.partial(_mlstm_kernel, seq=seq),
        grid=(n_seq,),
        in_specs=[zspec(QB_QM), zspec(QB_KM), zspec(QB_VM), zspec(ZF_OM),
                  pl.BlockSpec((seq, 128), lambda b: (row_block0 + b, ZF_GATES)),
                  sspec((N_STREAM, HEAD_DIM_M, HEAD_DIM_M)),
                  sspec((N_STREAM, HEAD_DIM_M)),
                  sspec((N_STREAM, HEAD_DIM_M)),
                  pl.BlockSpec((1, D_M), lambda b: (0, 0))],
        out_specs=[pl.BlockSpec((seq, D_M), lambda b: (b, 0)),
                   pl.BlockSpec((None, N_STREAM, HEAD_DIM_M, HEAD_DIM_M), lambda b: (b, 0, 0, 0)),
                   pl.BlockSpec((None, N_STREAM, HEAD_DIM_M), lambda b: (b, 0, 0)),
                   pl.BlockSpec((None, N_STREAM, HEAD_DIM_M), lambda b: (b, 0, 0))],
        out_shape=[jax.ShapeDtypeStruct((n_seq * seq, D_M), F32),
                   jax.ShapeDtypeStruct((n_seq, N_STREAM, HEAD_DIM_M, HEAD_DIM_M), F32),
                   jax.ShapeDtypeStruct((n_seq, N_STREAM, HEAD_DIM_M), F32),
                   jax.ShapeDtypeStruct((n_seq, N_STREAM, HEAD_DIM_M), F32)],
        scratch_shapes=[pltpu.VMEM((seq, D_M), F32), pltpu.VMEM((seq, D_M), F32),
                        pltpu.VMEM((N_STREAM, HEAD_DIM_M, HEAD_DIM_M), F32),
                        pltpu.VMEM((N_STREAM, HEAD_DIM_M), F32),
                        pltpu.VMEM((N_STREAM, HEAD_DIM_M), F32)],
        compiler_params=_cparams(("arbitrary",)),
        name="mlstm",
    )(zb, zb, zb, zf, zf, c0, n0, m0, norm_g)


N_CTX_TILES = T_CTX // TM_TOK


def _merge_kernel(x_ref, mod_ref, ya_c, ya_l, yc_c, yc_l, ym_c, ym_l, ga_ref, gc_ref, gm_ref,
                  wa_ref, wc_ref, wm_ref, wo_ref, o_ref):
    is_ctx = pl.program_id(0) < N_CTX_TILES

    def branch(y_ctx, y_lat, g_ref, w_ref):
        y = jnp.where(is_ctx, y_ctx[...], y_lat[...]).astype(BF16)
        return g_ref[...].astype(F32) * jnp.dot(y, w_ref[...], preferred_element_type=F32)

    merged = (branch(ya_c, ya_l, ga_ref, wa_ref) + branch(yc_c, yc_l, gc_ref, wc_ref)
              + branch(ym_c, ym_l, gm_ref, wm_ref))
    mix = jnp.dot(merged.astype(BF16), wo_ref[...], preferred_element_type=F32)
    o_ref[...] = x_ref[...] + mod_ref[2:3, :] * mix


def _merge(x, mod, ya, yc, ym, zg, w_pa, w_pc, w_pm, w_out):
    def rows(width, cb=0):
        return pl.BlockSpec((TM_TOK, width), lambda i: (i, cb))

    def ctx_rows(width):
        return pl.BlockSpec((TM_TOK, width), lambda i: (jnp.minimum(i, N_CTX_TILES - 1), 0))

    def lat_rows(width):
        return pl.BlockSpec((TM_TOK, width), lambda i: (jnp.maximum(i - N_CTX_TILES, 0), 0))

    def full(shape):
        return pl.BlockSpec(shape, lambda i: (0, 0))

    return pl.pallas_call(
        _merge_kernel,
        grid=(T_ALL // TM_TOK,),
        in_specs=[rows(D_MODEL),
                  pl.BlockSpec((None, 6, D_MODEL), lambda i: (_seg_of_tile(i, TM_TOK), 0, 0)),
                  ctx_rows(D_A), lat_rows(D_A), ctx_rows(D_CONV), lat_rows(D_CONV),
                  ctx_rows(D_M), lat_rows(D_M),
                  rows(D_MODEL, ZG_GA), rows(D_MODEL, ZG_GC), rows(D_MODEL, ZG_GM),
                  full((D_A, D_MODEL)), full((D_CONV, D_MODEL)), full((D_M, D_MODEL)),
                  full((D_MODEL, D_MODEL))],
        out_specs=rows(D_MODEL),
        out_shape=jax.ShapeDtypeStruct((T_ALL, D_MODEL), F32),
        compiler_params=_cparams(("arbitrary",)),
        name="merge",
    )(x, mod, ya[0], ya[1], yc[0], yc[1], ym[0], ym[1], zg, zg, zg, w_pa, w_pc, w_pm, w_out)


def _route_sort_kernel(x_ref, g_ref, mod_ref, wr_ref, br_ref, xt_ref, pos_ref, gate_ref, nch_ref, seg_ref):
    h = _normmod(x_ref[...], g_ref[...], mod_ref[...], 3, 4)
    hb = h.astype(BF16)
    logits = _dot_nt(wr_ref[...].astype(BF16), hb) + br_ref[...]
    e_iota = lax.broadcasted_iota(jnp.int32, (N_EXPERTS, TM_MOE), 0).astype(F32)
    sels, vals = [], []
    l = logits
    for k in range(TOP_K):
        m = jnp.max(l, axis=0, keepdims=True)
        idx = jnp.min(jnp.where(l == m, e_iota, float(N_EXPERTS)), axis=0, keepdims=True)
        sel = e_iota == idx
        vals.append(m)
        sels.append(sel)
        l = jnp.where(sel, -jnp.inf, l)
    exps = [jnp.exp(v - vals[0]) for v in vals]
    tot = exps[0] + exps[1] + exps[2] + exps[3]
    onehot = jnp.zeros((N_EXPERTS, TM_MOE), F32)
    for k in range(TOP_K):
        gate_ref[k:k + 1, :] = exps[k] / tot
        onehot = onehot + sels[k].astype(F32)
    gate_ref[TOP_K:8, :] = jnp.zeros((8 - TOP_K, TM_MOE), F32)

    cnt = jnp.sum(onehot, axis=1, keepdims=True)
    nch = jnp.floor((cnt + (CHUNK_ROWS - 1)) / CHUNK_ROWS)
    ei = lax.broadcasted_iota(jnp.int32, (N_EXPERTS, N_EXPERTS), 0)
    ej = lax.broadcasted_iota(jnp.int32, (N_EXPERTS, N_EXPERTS), 1)
    seg = jnp.dot((ej < ei).astype(F32), jnp.broadcast_to(nch, (N_EXPERTS, 128)), precision=HIGHEST,
                  preferred_element_type=F32)
    nch_ref[...] = jnp.broadcast_to(nch, (N_EXPERTS, 128)).astype(jnp.int32)
    seg_ref[...] = seg.astype(jnp.int32)

    t_src = lax.broadcasted_iota(jnp.int32, (TM_MOE, TM_MOE), 0)
    t_dst = lax.broadcasted_iota(jnp.int32, (TM_MOE, TM_MOE), 1)
    before = (t_src < t_dst).astype(BF16)
    row_of = (seg[:, 0:1] * CHUNK_ROWS
              + jnp.dot(onehot.astype(BF16), before, preferred_element_type=F32))
    q_iota = lax.broadcasted_iota(jnp.int32, (Q_TILE, TM_MOE), 0)
    perm = jnp.zeros((Q_TILE, TM_MOE), F32)
    for k in range(TOP_K):
        q_k = jnp.sum(jnp.where(sels[k], row_of, 0.0), axis=0, keepdims=True).astype(jnp.int32)
        pos_ref[k:k + 1, :] = q_k
        perm = jnp.where(q_iota == q_k, 1.0, perm)
    pos_ref[TOP_K:8, :] = jnp.zeros((8 - TOP_K, TM_MOE), jnp.int32)
    xt_ref[...] = jnp.dot(perm.astype(BF16), hb, preferred_element_type=F32).astype(BF16)


def _route_sort(x, norm_g, mod, w_rt, b_r):
    tspec = pl.BlockSpec((8, TM_MOE), lambda i: (0, i))
    mspec = pl.BlockSpec((None, N_EXPERTS, 128), lambda i: (i, 0, 0))
    meta = jax.ShapeDtypeStruct((N_TILES, N_EXPERTS, 128), jnp.int32)
    return pl.pallas_call(
        _route_sort_kernel,
        grid=(N_TILES,),
        in_specs=[pl.BlockSpec((TM_MOE, D_MODEL), lambda i: (i, 0)),
                  pl.BlockSpec((1, D_MODEL), lambda i: (0, 0)),
                  pl.BlockSpec((None, 6, D_MODEL), lambda i: (_seg_of_tile(i, TM_MOE), 0, 0)),
                  pl.BlockSpec((N_EXPERTS, D_MODEL), lambda i: (0, 0)),
                  pl.BlockSpec((N_EXPERTS, 1), lambda i: (0, 0))],
        out_specs=[pl.BlockSpec((Q_TILE, D_MODEL), lambda i: (i, 0)), tspec, tspec, mspec, mspec],
        out_shape=[jax.ShapeDtypeStruct((N_TILES * Q_TILE, D_MODEL), BF16),
                   jax.ShapeDtypeStruct((8, T_ALL), jnp.int32), jax.ShapeDtypeStruct((8, T_ALL), F32),
                   meta, meta],
        compiler_params=_cparams(("arbitrary",)),
        name="moe_route_sort",
    )(x, norm_g, mod, w_rt, b_r)


def _expert_kernel(nch_ref, seg_ref, wgu_ref, bgu_ref, wd_ref, bd_ref, xt_ref, yt_ref,
                   wgu_scr, wd_scr, xbuf, ybuf, row_scr, gstart_scr, gsem, ssem):
    del xt_ref
    e = pl.program_id(0)

    @pl.when(e == 0)
    def _():
        xbuf[...] = jnp.zeros_like(xbuf)

        def per_expert(ee, cnt):
            gstart_scr[ee] = cnt // CPG

            def per_tile(t, cnt):
                first = (t * CH_PER_TILE + seg_ref[t * N_EXPERTS + ee]) * CHUNK_ROWS

                def per_chunk(j, cnt):
                    row_scr[cnt] = first + j * CHUNK_ROWS
                    return cnt + 1

                return lax.fori_loop(0, nch_ref[t * N_EXPERTS + ee], per_chunk, cnt)

            cnt = lax.fori_loop(0, N_TILES, per_tile, cnt)
            padded = (cnt + CPG - 1) // CPG * CPG

            def pad(i, carry):
                row_scr[i] = -1
                return carry

            lax.fori_loop(cnt, padded, pad, 0)
            return padded

        total = lax.fori_loop(0, N_EXPERTS, per_expert, 0)
        gstart_scr[N_EXPERTS] = total // CPG

    g_first = gstart_scr[e]
    g_end = gstart_scr[e + 1]
    g_total = gstart_scr[N_EXPERTS]

    def copy_in(g, c, row):
        slot = g % 2
        return pltpu.make_async_copy(yt_ref.at[pl.ds(pl.multiple_of(row, CHUNK_ROWS), CHUNK_ROWS), :],
                                     xbuf.at[slot, c * CHUNK_ROWS:(c + 1) * CHUNK_ROWS, :], gsem.at[slot])

    def copy_out(g, c, row):
        slot = g % 2
        return pltpu.make_async_copy(ybuf.at[slot, c * CHUNK_ROWS:(c + 1) * CHUNK_ROWS, :],
                                     yt_ref.at[pl.ds(pl.multiple_of(row, CHUNK_ROWS), CHUNK_ROWS), :],
                                     ssem.at[slot])

    def for_chunks(g, fn):
        for c in range(CPG):
            row = row_scr[g * CPG + c]

            @pl.when(row >= 0)
            def _(c=c, row=row):
                fn(g, c, row)

    def start_in(g):
        for_chunks(g, lambda g, c, row: copy_in(g, c, row).start(priority=1))

    def wait_all(g, whole_group_copy, chunk_copy):
        full = row_scr[g * CPG + CPG - 1] >= 0

        @pl.when(full)
        def _():
            whole_group_copy(g % 2).wait()

        @pl.when(jnp.logical_not(full))
        def _():
            for_chunks(g, lambda g, c, row: chunk_copy(g, c, row).wait())

    def wait_in(g):
        wait_all(g, lambda slot: pltpu.make_async_copy(yt_ref.at[pl.ds(0, E_GROUP), :], xbuf.at[slot],
                                                       gsem.at[slot]), copy_in)

    def start_out(g):
        for_chunks(g, lambda g, c, row: copy_out(g, c, row).start(priority=1))

    def wait_out(g):
        wait_all(g, lambda slot: pltpu.make_async_copy(ybuf.at[slot], yt_ref.at[pl.ds(0, E_GROUP), :],
                                                       ssem.at[slot]), copy_out)

    @pl.when(jnp.logical_and(e == 0, g_total > 0))
    def _():
        start_in(0)

    @pl.when(g_end > g_first)
    def _():
        wgu_scr[...] = wgu_ref[...].astype(BF16)
        wd_scr[...] = wd_ref[...].astype(BF16)

    def group_step(g, carry):
        slot = g % 2

        @pl.when(g + 1 < g_total)
        def _():
            start_in(g + 1)

        wait_in(g)

        @pl.when(g >= 2)
        def _():
            wait_out(g - 2)

        hgu = jnp.dot(xbuf[slot], wgu_scr[...], preferred_element_type=F32) + bgu_ref[...]
        h_glu = jnp.minimum(hgu[:, :D_EXPERT], SWIGLU_LIMIT)
        h_lin = jnp.clip(hgu[:, D_EXPERT:], -SWIGLU_LIMIT, SWIGLU_LIMIT)
        act = (h_lin + 1.0) * (h_glu * _sigmoid(SWIGLU_ALPHA * h_glu))
        y = jnp.dot(act.astype(BF16), wd_scr[...], preferred_element_type=F32) + bd_ref[...]
        ybuf[slot] = y.astype(BF16)
        start_out(g)
        return carry

    lax.fori_loop(g_first, g_end, group_step, 0)

    @pl.when(e == N_EXPERTS - 1)
    def _():
        @pl.when(g_total >= 2)
        def _():
            wait_out(g_total - 2)

        @pl.when(g_total >= 1)
        def _():
            wait_out(g_total - 1)


def _experts(nch_flat, seg_flat, xt, w_gu, b_gu, w_down, b_down, layer):
    return pl.pallas_call(
        _expert_kernel,
        grid_spec=pltpu.PrefetchScalarGridSpec(
            num_scalar_prefetch=2,
            grid=(N_EXPERTS,),
            in_specs=[pl.BlockSpec((None, None, D_MODEL, 2 * D_EXPERT), lambda e, n, s: (layer, e, 0, 0)),
                      pl.BlockSpec((None, None, 1, 2 * D_EXPERT), lambda e, n, s: (layer, e, 0, 0)),
                      pl.BlockSpec((None, None, D_EXPERT, D_MODEL), lambda e, n, s: (layer, e, 0, 0)),
                      pl.BlockSpec((None, None, 1, D_MODEL), lambda e, n, s: (layer, e, 0, 0)),
                      pl.BlockSpec(memory_space=pl.ANY)],
            out_specs=pl.BlockSpec(memory_space=pl.ANY),
            scratch_shapes=[pltpu.VMEM((D_MODEL, 2 * D_EXPERT), BF16),
                            pltpu.VMEM((D_EXPERT, D_MODEL), BF16),
                            pltpu.VMEM((2, E_GROUP, D_MODEL), BF16),
                            pltpu.VMEM((2, E_GROUP, D_MODEL), BF16),
                            pltpu.SMEM((MAX_CHUNKS,), jnp.int32),
                            pltpu.SMEM((N_EXPERTS + 1,), jnp.int32),
                            pltpu.SemaphoreType.DMA((2,)),
                            pltpu.SemaphoreType.DMA((2,))]),
        out_shape=jax.ShapeDtypeStruct((N_TILES * Q_TILE, D_MODEL), BF16),
        input_output_aliases={6: 0},
        compiler_params=_cparams(("arbitrary",)),
        name="moe_experts",
    )(nch_flat, seg_flat, w_gu, b_gu, w_down, b_down, xt)


MOE_CTX_TILES = T_CTX // TM_MOE


def _combine_rows(x_ref, mod_ref, pos_ref, gate_ref, yt_ref):
    lane = lax.broadcasted_iota(jnp.int32, (TM_MOE, Q_TILE), 1)
    sel = jnp.zeros((TM_MOE, Q_TILE), F32)
    for k in range(TOP_K):
        sel = jnp.where(lane == pos_ref[:, k:k + 1], gate_ref[:, k:k + 1], sel)
    acc = jnp.dot(sel.astype(BF16), yt_ref[...], preferred_element_type=F32)
    return x_ref[...] + mod_ref[5:6, :] * acc


def _combine_kernel(x_ref, mod_ref, pos_ref, gate_ref, yt_ref, o_ref):
    o_ref[...] = _combine_rows(x_ref, mod_ref, pos_ref, gate_ref, yt_ref)


def _combine_final_kernel(x_ref, mod_ref, pos_ref, gate_ref, yt_ref, fg_ref, ctx_ref, lat_ref):
    y = _combine_rows(x_ref, mod_ref, pos_ref, gate_ref, yt_ref)
    y = y * lax.rsqrt(jnp.mean(y * y, axis=-1, keepdims=True) + RMS_EPS) * fg_ref[...]
    is_ctx = pl.program_id(0) < MOE_CTX_TILES

    @pl.when(is_ctx)
    def _():
        ctx_ref[...] = y

    @pl.when(jnp.logical_not(is_ctx))
    def _():
        lat_ref[...] = y


def _combine(x, mod, pos_t, gate_t, yt, final_g=None):
    tile = pl.BlockSpec((TM_MOE, D_MODEL), lambda i: (i, 0))
    in_specs = [tile,
                pl.BlockSpec((None, 6, D_MODEL), lambda i: (_seg_of_tile(i, TM_MOE), 0, 0)),
                pl.BlockSpec((TM_MOE, 8), lambda i: (i, 0)),
                pl.BlockSpec((TM_MOE, 8), lambda i: (i, 0)),
                pl.BlockSpec((Q_TILE, D_MODEL), lambda i: (i, 0))]
    if final_g is None:
        return pl.pallas_call(
            _combine_kernel, grid=(N_TILES,), in_specs=in_specs, out_specs=tile,
            out_shape=jax.ShapeDtypeStruct((T_ALL, D_MODEL), F32),
            compiler_params=_cparams(("arbitrary",)), name="moe_combine",
        )(x, mod, pos_t, gate_t, yt)
    return pl.pallas_call(
        _combine_final_kernel, grid=(N_TILES,),
        in_specs=in_specs + [pl.BlockSpec((1, D_MODEL), lambda i: (0, 0))],
        out_specs=[pl.BlockSpec((TM_MOE, D_MODEL), lambda i: (jnp.minimum(i, MOE_CTX_TILES - 1), 0)),
                   pl.BlockSpec((TM_MOE, D_MODEL), lambda i: (jnp.maximum(i - MOE_CTX_TILES, 0), 0))],
        out_shape=[jax.ShapeDtypeStruct((T_CTX, D_MODEL), F32), jax.ShapeDtypeStruct((T_LAT, D_MODEL), F32)],
        compiler_params=_cparams(("arbitrary",)), name="moe_combine_final",
    )(x, mod, pos_t, gate_t, yt, final_g)


def _moe(x, norm_g, mod, w_rt, b_r, w_gu, b_gu, w_down, b_down, layer, final_g=None):
    xt, pos, gate, nch, seg = _route_sort(x, norm_g, mod, w_rt, b_r)
    yt = _experts(nch[:, :, 0].reshape(-1), seg[:, :, 0].reshape(-1), xt, w_gu,
                  b_gu.reshape(DEPTH, N_EXPERTS, 1, 2 * D_EXPERT), w_down,
                  b_down.reshape(DEPTH, N_EXPERTS, 1, D_MODEL), layer)
    return _combine(x, mod, pos.T, gate.T, yt, final_g)


def _split_in_cols(w):
    conv0, mq0, om0 = 3 * D_A, 3 * D_A + 2 * D_CONV, 3 * D_A + 2 * D_CONV + 3 * D_M
    gates_end = GATE_OFF + N_GATE_M
    pad = jnp.zeros(w.shape[:-1] + (N_ZF - (2 * D_CONV + D_M + N_GATE_M),), w.dtype)
    zb = jnp.concatenate([w[..., :conv0], w[..., mq0:om0]], axis=-1)
    zg = w[..., gates_end:]
    zf = jnp.concatenate([w[..., conv0:mq0], w[..., om0:GATE_OFF], w[..., GATE_OFF:gates_end], pad], axis=-1)
    return zb, zg, zf


def kernel(x_prompt, x_sample, cache_k, cache_v, state_C, state_n, state_m, c, c_ctx, norm1_g, w_mod, b_mod, w_in, b_in, rpb, w_dw, b_dw, cln_g, cln_b, mnorm_g, w_pa, w_pc, w_pm, w_out, norm2_g, w_router, b_router, w_gu, b_gu, w_down, b_down, final_g):
    cond = jnp.concatenate([c_ctx[None, :], c, jnp.zeros((SEG_PAD - N_SEG, D_MODEL), F32)], axis=0)
    mod_all = _modulation(cond, w_mod, b_mod).reshape(DEPTH, SEG_PAD, 6, D_MODEL)

    x = jnp.concatenate([x_prompt.reshape(T_CTX, D_MODEL), x_sample.reshape(T_LAT, D_MODEL)], axis=0)
    ck = cache_k.reshape(DEC_BATCH, DEPTH, PAST_LEN, D_A).astype(BF16)
    cv = cache_v.reshape(DEC_BATCH, DEPTH, PAST_LEN, D_A).astype(BF16)
    lat_c0 = state_C.reshape(DEC_BATCH, DEPTH, N_STREAM, HEAD_DIM_M, HEAD_DIM_M)
    lat_n0 = state_n.reshape(DEC_BATCH, DEPTH, N_STREAM, HEAD_DIM_M)
    lat_m0 = jnp.broadcast_to(state_m.reshape(DEC_BATCH, DEPTH, N_STREAM, 1),
                              (DEC_BATCH, DEPTH, N_STREAM, HEAD_DIM_M))
    ctx_c0 = jnp.zeros((1, N_STREAM, HEAD_DIM_M, HEAD_DIM_M), F32)
    ctx_n0 = jnp.zeros((1, N_STREAM, HEAD_DIM_M), F32)
    ctx_m0 = jnp.full((1, N_STREAM, HEAD_DIM_M), -jnp.inf, F32)

    ks, vs, cs, ns, ms = [], [], [], [], []
    for l in range(DEPTH):
        mod = mod_all[l]
        g1 = norm1_g[l][None, :]
        zb, zg, zf, kv = _in_proj(x, g1, mod, _split_in_cols(w_in[l].astype(BF16)),
                                  _split_in_cols(b_in[l][None, :]))
        ya = (_ctx_attention(zb), _natt(zb, ck, cv, _natt_bias(rpb[l]), l))
        conv_w = (w_dw[l], b_dw[l][None, :], cln_g[l][None, :], cln_b[l][None, :])
        yc = (_conv(zf, *conv_w, SEQ, BATCH, 0), _conv(zf, *conv_w, DEC_SEQ, DEC_BATCH, T_CTX // DEC_SEQ))
        ng = mnorm_g[l][None, :]
        ym_ctx, c_l, n_l, m_l = _mlstm(zb, zf, ctx_c0, ctx_n0, ctx_m0, ng, SEQ, BATCH, 0, lambda b: (0,))
        ym_lat, _, _, _ = _mlstm(zb, zf, lat_c0, lat_n0, lat_m0, ng, DEC_SEQ, DEC_BATCH, T_CTX // DEC_SEQ,
                                 lambda b: (b, l))
        x = _merge(x, mod, ya, yc, (ym_ctx, ym_lat), zg, w_pa[l].astype(BF16), w_pc[l].astype(BF16),
                   w_pm[l].astype(BF16), w_out[l].astype(BF16))
        x = _moe(x, norm2_g[l][None, :], mod, w_router[l].T, b_router[l][:, None],
                 w_gu, b_gu, w_down, b_down, l, final_g[None, :] if l == DEPTH - 1 else None)
        ks.append(kv[:, :D_A].reshape(BATCH, SEQ, N_HEADS_A, HEAD_DIM_A))
        vs.append(kv[:, D_A:].reshape(BATCH, SEQ, N_HEADS_A, HEAD_DIM_A))
        cs.append(c_l.reshape(BATCH, 2, N_HEADS_M, HEAD_DIM_M, HEAD_DIM_M))
        ns.append(n_l.reshape(BATCH, 2, N_HEADS_M, HEAD_DIM_M))
        ms.append(m_l[:, :, 0].reshape(BATCH, 2, N_HEADS_M))

    y_ctx, y_lat = x
    return (y_ctx.reshape(BATCH, SEQ, D_MODEL), y_lat.reshape(DEC_BATCH, DEC_SEQ, D_MODEL),
            jnp.stack(ks, axis=1), jnp.stack(vs, axis=1), jnp.stack(cs, axis=1),
            jnp.stack(ns, axis=1), jnp.stack(ms, axis=1))
```

```python
import functools

import numpy as np
import jax
import jax.numpy as jnp
from jax import lax
from jax.experimental import pallas as pl
from jax.experimental.pallas import tpu as pltpu

F32 = jnp.float32
BF16 = jnp.bfloat16
HIGHEST = lax.Precision.HIGHEST

D_MODEL = 1024
BATCH = 16
SEQ = 256
DEPTH = 2
DEC_BATCH = 8
DEC_SEQ = 1024
PAST_LEN = 512
GRID_W = 64
N_HEADS_A = 8
HEAD_DIM_A = 64
D_A = N_HEADS_A * HEAD_DIM_A
WIN_ROWS = 8
WIN_COLS = 16
D_CONV = 512
CONV_WIDTH = 31
N_HEADS_M = 4
HEAD_DIM_M = 128
D_M = N_HEADS_M * HEAD_DIM_M
N_GATE_M = 4 * N_HEADS_M
CHUNK = 64
N_EXPERTS = 32
TOP_K = 4
D_EXPERT = 1024
SWIGLU_ALPHA = 1.702
SWIGLU_LIMIT = 7.0
RMS_EPS = 1e-6
LN_EPS = 1e-5
GATE_OFF = 3 * D_A + 2 * D_CONV + 4 * D_M
N_IN = GATE_OFF + N_GATE_M + 3 * D_MODEL

T_CTX = BATCH * SEQ
T_LAT = DEC_BATCH * DEC_SEQ
T_ALL = T_CTX + T_LAT
N_SEG = 1 + DEC_BATCH
SEG_PAD = 16
GRID_ROWS = DEC_SEQ // GRID_W
NEG = -1e30

N_ZB = 6 * 512
N_ZG = 3 * D_MODEL
N_ZF = 1664
QB_QA, QB_KA, QB_VA, QB_QM, QB_KM, QB_VM = 0, 1, 2, 3, 4, 5
ZG_GA, ZG_GC, ZG_GM = 0, 1, 2
ZF_CU, ZF_CG, ZF_OM = 0, 1, 2
ZF_GATES = 12

TM_TOK = 512
TM_PROJ = 256
TM_MOE = 512
N_TILES = T_ALL // TM_MOE
CHUNK_ROWS = 16
MXU_ROWS = 256
Q_TILE = -(-(TM_MOE * TOP_K + N_EXPERTS * (CHUNK_ROWS - 1)) // MXU_ROWS) * MXU_ROWS
CH_PER_TILE = Q_TILE // CHUNK_ROWS
E_GROUP = 256
CPG = E_GROUP // CHUNK_ROWS
MAX_CHUNKS = ((T_ALL * TOP_K + N_TILES * N_EXPERTS * (CHUNK_ROWS - 1)) // CHUNK_ROWS
              + N_EXPERTS * (CPG - 1)) + CPG
assert Q_TILE - (TM_MOE * TOP_K + N_EXPERTS * (CHUNK_ROWS - 1)) >= 2 * CHUNK_ROWS and N_TILES > CPG
READ_SPARE = N_TILES * Q_TILE - CHUNK_ROWS
VMEM_LIMIT = 60 * 1024 * 1024


def _cparams(sem=None):
    return pltpu.CompilerParams(dimension_semantics=sem, vmem_limit_bytes=VMEM_LIMIT)


def _seg_of_tile(i, tile):
    n_ctx = T_CTX // tile
    per_lat = DEC_SEQ // tile
    return jnp.where(i < n_ctx, 0, 1 + (i - n_ctx) // per_lat)


def _dot_nt(a, b):
    return lax.dot_general(a, b, (((1,), (1,)), ((), ())), preferred_element_type=F32)


def _sigmoid(x):
    return 1.0 / (1.0 + jnp.exp(-x))


def _mod_kernel(c_ref, w_ref, b_ref, o_ref):
    c = c_ref[...]
    s = c * _sigmoid(c)
    o_ref[...] = jnp.dot(s, w_ref[...], precision=HIGHEST, preferred_element_type=F32) + b_ref[...]


def _modulation(cond, w_mod, b_mod):
    tn = 1536
    return pl.pallas_call(
        _mod_kernel,
        grid=(DEPTH, 6 * D_MODEL // tn),
        in_specs=[pl.BlockSpec((SEG_PAD, D_MODEL), lambda l, j: (0, 0)),
                  pl.BlockSpec((None, D_MODEL, tn), lambda l, j: (l, 0, j)),
                  pl.BlockSpec((None, 1, tn), lambda l, j: (l, 0, j))],
        out_specs=pl.BlockSpec((None, SEG_PAD, tn), lambda l, j: (l, 0, j)),
        out_shape=jax.ShapeDtypeStruct((DEPTH, SEG_PAD, 6 * D_MODEL), F32),
        compiler_params=_cparams(("arbitrary", "arbitrary")),
        name="modulation",
    )(cond, w_mod, b_mod.reshape(DEPTH, 1, 6 * D_MODEL))


def _normmod(x, g, mod, shift_idx, scale_idx):
    y = x * lax.rsqrt(jnp.mean(x * x, axis=-1, keepdims=True) + RMS_EPS) * g
    return y * (1.0 + mod[scale_idx:scale_idx + 1, :]) + mod[shift_idx:shift_idx + 1, :]


PROJ_CTX_TILES = T_CTX // TM_PROJ


def _in_proj_kernel(x_ref, g_ref, mod_ref, wb_ref, bb_ref, wg_ref, bg_ref, wf_ref, bf_ref,
                    zb_ref, zg_ref, zf_ref, kv_ref):
    h = _normmod(x_ref[...], g_ref[...], mod_ref[...], 0, 1).astype(BF16)
    acc = jnp.dot(h, wb_ref[...], preferred_element_type=F32) + bb_ref[...]
    zb_ref[...] = acc.astype(BF16)

    @pl.when(pl.program_id(0) < PROJ_CTX_TILES)
    def _():
        kv_ref[...] = acc[:, D_A:3 * D_A]

    gates = jnp.dot(h, wg_ref[...], preferred_element_type=F32) + bg_ref[...]
    zg_ref[...] = _sigmoid(gates).astype(BF16)
    zf_ref[...] = jnp.dot(h, wf_ref[...], preferred_element_type=F32) + bf_ref[...]


def _in_proj(x, norm_g, mod, w, b):
    def full(a):
        return pl.BlockSpec(a.shape, lambda i: (0, 0))

    def rows(n):
        return pl.BlockSpec((TM_PROJ, n), lambda i: (i, 0))

    return pl.pallas_call(
        _in_proj_kernel,
        grid=(T_ALL // TM_PROJ,),
        in_specs=[rows(D_MODEL),
                  pl.BlockSpec((1, D_MODEL), lambda i: (0, 0)),
                  pl.BlockSpec((None, 6, D_MODEL), lambda i: (_seg_of_tile(i, TM_PROJ), 0, 0)),
                  full(w[0]), full(b[0]), full(w[1]), full(b[1]), full(w[2]), full(b[2])],
        out_specs=[rows(N_ZB), rows(N_ZG), rows(N_ZF),
                   pl.BlockSpec((TM_PROJ, 2 * D_A), lambda i: (jnp.minimum(i, PROJ_CTX_TILES - 1), 0))],
        out_shape=[jax.ShapeDtypeStruct((T_ALL, N_ZB), BF16), jax.ShapeDtypeStruct((T_ALL, N_ZG), BF16),
                   jax.ShapeDtypeStruct((T_ALL, N_ZF), F32), jax.ShapeDtypeStruct((T_CTX, 2 * D_A), F32)],
        compiler_params=_cparams(("arbitrary",)),
        name="in_proj",
    )(x, norm_g, mod, w[0], b[0], w[1], b[1], w[2], b[2])


HEAD_PAIR = 2 * HEAD_DIM_A
ATT_SCALE = HEAD_DIM_A ** -0.5


def _pair_queries(q2):
    lo = lax.broadcasted_iota(jnp.int32, (1, HEAD_PAIR), 1) < HEAD_DIM_A
    q2 = q2 * ATT_SCALE
    zero = jnp.zeros_like(q2)
    return lo, jnp.concatenate([jnp.where(lo, q2, zero), jnp.where(lo, zero, q2)], axis=0)


def _unpair(lo, o_stacked):
    rows = o_stacked.shape[0] // 2
    return jnp.where(lo, o_stacked[:rows], o_stacked[rows:])


def _ctx_attn_kernel(q_ref, k_ref, v_ref, o_ref):
    for hp in range(N_HEADS_A // 2):
        sl = slice(hp * HEAD_PAIR, (hp + 1) * HEAD_PAIR)
        lo, qs = _pair_queries(q_ref[:, sl])
        s = _dot_nt(qs, k_ref[:, sl])
        p = jnp.exp(s - jnp.max(s, axis=-1, keepdims=True))
        l = jnp.sum(p, axis=-1, keepdims=True)
        o = jnp.dot(p.astype(BF16), v_ref[:, sl], preferred_element_type=F32) / l
        o_ref[:, sl] = _unpair(lo, o)


def _ctx_attention(zb):
    def spec(cb):
        return pl.BlockSpec((SEQ, D_A), lambda b: (b, cb))

    return pl.pallas_call(
        _ctx_attn_kernel,
        grid=(BATCH,),
        in_specs=[spec(QB_QA), spec(QB_KA), spec(QB_VA)],
        out_specs=pl.BlockSpec((SEQ, D_A), lambda b: (b, 0)),
        out_shape=jax.ShapeDtypeStruct((T_CTX, D_A), F32),
        compiler_params=_cparams(("arbitrary",)),
        name="ctx_attention",
    )(zb, zb, zb)


def _natt_kernel(q_ref, k_ref, v_ref, kc_ref, vc_ref, bias_ref, o_ref):
    r = pl.program_id(1)
    rs = jnp.clip(r - WIN_ROWS // 2, 0, GRID_ROWS - WIN_ROWS)
    start = pl.multiple_of(rs * GRID_W, GRID_W)
    band = WIN_ROWS * GRID_W
    for hp in range(N_HEADS_A // 2):
        sl = slice(hp * HEAD_PAIR, (hp + 1) * HEAD_PAIR)
        lo, qs = _pair_queries(q_ref[:, sl])
        bias = bias_ref[2 * hp:2 * hp + 2].reshape(2 * GRID_W, band)
        s_loc = _dot_nt(qs, k_ref[pl.ds(start, band), sl]) + bias
        s_ctx = _dot_nt(qs, kc_ref[:, sl])
        m = jnp.maximum(jnp.max(s_loc, axis=-1, keepdims=True), jnp.max(s_ctx, axis=-1, keepdims=True))
        p_loc = jnp.exp(s_loc - m)
        p_ctx = jnp.exp(s_ctx - m)
        l = jnp.sum(p_loc, axis=-1, keepdims=True) + jnp.sum(p_ctx, axis=-1, keepdims=True)
        o = (jnp.dot(p_loc.astype(BF16), v_ref[pl.ds(start, band), sl], preferred_element_type=F32)
             + jnp.dot(p_ctx.astype(BF16), vc_ref[:, sl], preferred_element_type=F32))
        o_ref[:, sl] = _unpair(lo, o / l)


def _band_row_offset(r):
    rs = jnp.clip(r - WIN_ROWS // 2, 0, GRID_ROWS - WIN_ROWS)
    return rs - r + WIN_ROWS - 1


def _natt_bias(rpb_l):
    qc = np.arange(GRID_W)
    kc = np.arange(GRID_W)
    cs = np.clip(qc - WIN_COLS // 2, 0, GRID_W - WIN_COLS)
    ok = (kc[None, :] >= cs[:, None]) & (kc[None, :] < cs[:, None] + WIN_COLS)
    dc = np.clip(kc[None, :] - qc[:, None] + WIN_COLS - 1, 0, 2 * WIN_COLS - 2)
    pick = (dc[None] == np.arange(2 * WIN_COLS - 1)[:, None, None]).astype(np.float32)
    toe = jnp.einsum('hdc,cqk->hdqk', rpb_l, jnp.asarray(pick), precision=HIGHEST)
    toe = jnp.where(jnp.asarray(ok)[None, None], toe, NEG)
    bands = [toe[:, d0:d0 + WIN_ROWS].transpose(0, 2, 1, 3).reshape(N_HEADS_A, GRID_W, WIN_ROWS * GRID_W)
             for d0 in range(WIN_ROWS)]
    return jnp.stack(bands, axis=0)


def _natt(zb, cache_k, cache_v, bias, layer):
    lat0 = T_CTX // DEC_SEQ
    row0 = T_CTX // GRID_W
    return pl.pallas_call(
        _natt_kernel,
        grid=(DEC_BATCH, GRID_ROWS),
        in_specs=[pl.BlockSpec((GRID_W, D_A), lambda b, r: (row0 + b * GRID_ROWS + r, QB_QA)),
                  pl.BlockSpec((DEC_SEQ, D_A), lambda b, r: (lat0 + b, QB_KA)),
                  pl.BlockSpec((DEC_SEQ, D_A), lambda b, r: (lat0 + b, QB_VA)),
                  pl.BlockSpec((None, None, PAST_LEN, D_A), lambda b, r: (b, layer, 0, 0)),
                  pl.BlockSpec((None, None, PAST_LEN, D_A), lambda b, r: (b, layer, 0, 0)),
                  pl.BlockSpec((None, N_HEADS_A, GRID_W, WIN_ROWS * GRID_W),
                               lambda b, r: (_band_row_offset(r), 0, 0, 0))],
        out_specs=pl.BlockSpec((GRID_W, D_A), lambda b, r: (b * GRID_ROWS + r, 0)),
        out_shape=jax.ShapeDtypeStruct((T_LAT, D_A), F32),
        compiler_params=_cparams(("arbitrary", "arbitrary")),
        name="nbr_attention",
    )(zb, zb, zb, cache_k, cache_v, bias)


CONV_HALO = 16
CONV_ROWS = 64


SUBLANES = 8


def _conv_kernel(u_ref, g_ref, w_ref, b_ref, lg_ref, lb_ref, o_ref, pad_scr, sh_scr, *, seq):
    zeros = jnp.zeros((CONV_HALO, D_CONV), F32)
    pad_scr[0:CONV_HALO, :] = zeros
    pad_scr[CONV_HALO + seq:2 * CONV_HALO + seq, :] = zeros
    pad_scr[CONV_HALO:CONV_HALO + seq, :] = u_ref[...] * _sigmoid(g_ref[...])
    n_sh = seq + 2 * CONV_HALO - SUBLANES
    for s in range(SUBLANES):
        sh_scr[s] = pad_scr[s:s + n_sh, :]
    first = CONV_HALO - CONV_WIDTH // 2
    for c in range(seq // CONV_ROWS):
        base = c * CONV_ROWS
        acc = jnp.broadcast_to(b_ref[...], (CONV_ROWS, D_CONV))
        for j in range(CONV_WIDTH):
            q, s = divmod(first + j, SUBLANES)
            row0 = base + q * SUBLANES
            acc = acc + sh_scr[s, row0:row0 + CONV_ROWS, :] * w_ref[j:j + 1, :]
        mu = jnp.mean(acc, axis=-1, keepdims=True)
        xc = acc - mu
        var = jnp.mean(xc * xc, axis=-1, keepdims=True)
        y = xc * lax.rsqrt(var + LN_EPS) * lg_ref[...] + lb_ref[...]
        o_ref[base:base + CONV_ROWS, :] = y * _sigmoid(y)


def _conv(z, w_dw, b_dw, ln_g, ln_b, seq, n_seq, row_block0):
    def vec():
        return pl.BlockSpec((1, D_CONV), lambda b: (0, 0))

    return pl.pallas_call(
        functools.partial(_conv_kernel, seq=seq),
        grid=(n_seq,),
        in_specs=[pl.BlockSpec((seq, D_CONV), lambda b: (row_block0 + b, ZF_CU)),
                  pl.BlockSpec((seq, D_CONV), lambda b: (row_block0 + b, ZF_CG)),
                  pl.BlockSpec((CONV_WIDTH, D_CONV), lambda b: (0, 0)),
                  vec(), vec(), vec()],
        out_specs=pl.BlockSpec((seq, D_CONV), lambda b: (b, 0)),
        out_shape=jax.ShapeDtypeStruct((n_seq * seq, D_CONV), F32),
        scratch_shapes=[pltpu.VMEM((seq + 2 * CONV_HALO, D_CONV), F32),
                        pltpu.VMEM((SUBLANES, seq + 2 * CONV_HALO - SUBLANES, D_CONV), F32)],
        compiler_params=_cparams(("arbitrary",)),
        name="conformer_conv",
    )(z, z, w_dw, b_dw, ln_g, ln_b)


N_STREAM = 2 * N_HEADS_M
MCHUNK = 128


def _mlstm_kernel(q_ref, k_ref, v_ref, om_ref, gt_ref, c0_ref, n0_ref, m0_ref, ng_ref,
                  y_ref, c_out, n_out, m_out, hf_scr, hb_scr, c_scr, n_scr, m_scr, *, seq):
    nc = seq // MCHUNK
    c_scr[...] = c0_ref[...]
    n_scr[...] = n0_ref[...]
    m_scr[...] = m0_ref[...]
    ti = lax.broadcasted_iota(jnp.int32, (MCHUNK, MCHUNK), 0)
    si = lax.broadcasted_iota(jnp.int32, (MCHUNK, MCHUNK), 1)
    masks = (si <= ti, si >= ti)
    tris = tuple(mk.astype(F32) for mk in masks)
    kscale = HEAD_DIM_M ** -0.5

    def chunk_step(c, carry):
        for d in range(2):
            cidx = c if d == 0 else nc - 1 - c
            off = pl.multiple_of(cidx * MCHUNK, MCHUNK)
            last = MCHUNK - 1 if d == 0 else 0
            g = gt_ref[pl.ds(off, MCHUNK), :]
            lf = jnp.minimum(g, 0.0) - jnp.log(1.0 + jnp.exp(-jnp.abs(g)))
            cum = jnp.dot(tris[d], lf, precision=HIGHEST, preferred_element_type=F32)
            g_t = g.T
            cum_t = cum.T
            for h in range(N_HEADS_M):
                s_id = N_HEADS_M * d + h
                icol = 2 * N_HEADS_M * d + h
                fcol = icol + N_HEADS_M
                hs = slice(h * HEAD_DIM_M, (h + 1) * HEAD_DIM_M)
                b_col = cum[:, fcol:fcol + 1]
                i_col = g[:, icol:icol + 1]
                b_row = cum_t[fcol:fcol + 1, :]
                i_row = g_t[icol:icol + 1, :]
                b_last = b_col[last:last + 1, :]
                m_prev = m_scr[s_id:s_id + 1, 0:1]
                log_d = jnp.where(masks[d], b_col - b_row + i_row, NEG)
                inter = b_col + m_prev
                m_t = jnp.maximum(inter, jnp.max(log_d, axis=-1, keepdims=True))
                w_inter = jnp.exp(inter - m_t)
                qb = q_ref[pl.ds(off, MCHUNK), hs]
                kb = k_ref[pl.ds(off, MCHUNK), hs]
                vb = v_ref[pl.ds(off, MCHUNK), hs]
                c_prev = c_scr[s_id]
                n_prev = n_scr[s_id:s_id + 1, :]
                s_mat = _dot_nt(qb, kb) * (kscale * jnp.exp(log_d - m_t))
                num = (w_inter * _dot_nt(qb, c_prev.astype(BF16))
                       + jnp.dot(s_mat.astype(BF16), vb, preferred_element_type=F32))
                den = (w_inter * jnp.sum(qb.astype(F32) * n_prev, axis=-1, keepdims=True)
                       + jnp.sum(s_mat, axis=-1, keepdims=True))
                hh = num / jnp.maximum(jnp.abs(den), jnp.exp(-m_t))
                if d == 0:
                    hf_scr[pl.ds(off, MCHUNK), hs] = hh
                else:
                    hb_scr[pl.ds(off, MCHUNK), hs] = hh
                m_new = m_t[last:last + 1, :]
                w_prev = jnp.exp(b_last + m_prev - m_new)
                w_src = kscale * jnp.exp(b_last - b_col + i_col - m_new)
                upd = jnp.dot((w_src * vb.astype(F32)).T.astype(BF16), kb, preferred_element_type=F32)
                c_scr[s_id] = w_prev * c_prev + upd
                n_scr[s_id:s_id + 1, :] = (w_prev * n_prev
                                           + jnp.sum(w_src * kb.astype(F32), axis=0, keepdims=True))
                m_scr[s_id:s_id + 1, :] = jnp.broadcast_to(m_new, (1, HEAD_DIM_M))
        return carry

    lax.fori_loop(0, nc, chunk_step, 0)

    for h in range(N_HEADS_M):
        hs = slice(h * HEAD_DIM_M, (h + 1) * HEAD_DIM_M)
        hsum = hf_scr[:, hs] + hb_scr[:, hs]
        mu = jnp.mean(hsum, axis=-1, keepdims=True)
        xc = hsum - mu
        var = jnp.mean(xc * xc, axis=-1, keepdims=True)
        hn = xc * lax.rsqrt(var + LN_EPS) * ng_ref[:, hs]
        y_ref[:, hs] = _sigmoid(om_ref[:, hs]) * hn
    c_out[...] = c_scr[...]
    n_out[...] = n_scr[...]
    m_out[...] = m_scr[...]


def _mlstm(zb, zf, c0, n0, m0, norm_g, seq, n_seq, row_block0, state_map):
    lead = len(state_map(0))

    def zspec(cb):
        return pl.BlockSpec((seq, D_M), lambda b: (row_block0 + b, cb))

    def sspec(tail):
        return pl.BlockSpec((None,) * lead + tail, lambda b: state_map(b) + (0,) * len(tail))

    return pl.pallas_call(
        functools.partial(_mlstm_kernel, seq=seq),
        grid=(n_seq,),
        in_specs=[zspec(QB_QM), zspec(QB_KM), zspec(QB_VM), zspec(ZF_OM),
                  pl.BlockSpec((seq, 128), lambda b: (row_block0 + b, ZF_GATES)),
                  sspec((N_STREAM, HEAD_DIM_M, HEAD_DIM_M)),
                  sspec((N_STREAM, HEAD_DIM_M)),
                  sspec((N_STREAM, HEAD_DIM_M)),
                  pl.BlockSpec((1, D_M), lambda b: (0, 0))],
        out_specs=[pl.BlockSpec((seq, D_M), lambda b: (b, 0)),
                   pl.BlockSpec((None, N_STREAM, HEAD_DIM_M, HEAD_DIM_M), lambda b: (b, 0, 0, 0)),
                   pl.BlockSpec((None, N_STREAM, HEAD_DIM_M), lambda b: (b, 0, 0)),
                   pl.BlockSpec((None, N_STREAM, HEAD_DIM_M), lambda b: (b, 0, 0))],
        out_shape=[jax.ShapeDtypeStruct((n_seq * seq, D_M), F32),
                   jax.ShapeDtypeStruct((n_seq, N_STREAM, HEAD_DIM_M, HEAD_DIM_M), F32),
                   jax.ShapeDtypeStruct((n_seq, N_STREAM, HEAD_DIM_M), F32),
                   jax.ShapeDtypeStruct((n_seq, N_STREAM, HEAD_DIM_M), F32)],
        scratch_shapes=[pltpu.VMEM((seq, D_M), F32), pltpu.VMEM((seq, D_M), F32),
                        pltpu.VMEM((N_STREAM, HEAD_DIM_M, HEAD_DIM_M), F32),
                        pltpu.VMEM((N_STREAM, HEAD_DIM_M), F32),
                        pltpu.VMEM((N_STREAM, HEAD_DIM_M), F32)],
        compiler_params=_cparams(("arbitrary",)),
        name="mlstm",
    )(zb, zb, zb, zf, zf, c0, n0, m0, norm_g)


N_CTX_TILES = T_CTX // TM_TOK


def _merge_kernel(x_ref, mod_ref, ya_c, ya_l, yc_c, yc_l, ym_c, ym_l, ga_ref, gc_ref, gm_ref,
                  wa_ref, wc_ref, wm_ref, wo_ref, o_ref):
    is_ctx = pl.program_id(0) < N_CTX_TILES

    def branch(y_ctx, y_lat, g_ref, w_ref):
        y = jnp.where(is_ctx, y_ctx[...], y_lat[...]).astype(BF16)
        return g_ref[...].astype(F32) * jnp.dot(y, w_ref[...], preferred_element_type=F32)

    merged = (branch(ya_c, ya_l, ga_ref, wa_ref) + branch(yc_c, yc_l, gc_ref, wc_ref)
              + branch(ym_c, ym_l, gm_ref, wm_ref))
    mix = jnp.dot(merged.astype(BF16), wo_ref[...], preferred_element_type=F32)
    o_ref[...] = x_ref[...] + mod_ref[2:3, :] * mix


def _merge(x, mod, ya, yc, ym, zg, w_pa, w_pc, w_pm, w_out):
    def rows(width, cb=0):
        return pl.BlockSpec((TM_TOK, width), lambda i: (i, cb))

    def ctx_rows(width):
        return pl.BlockSpec((TM_TOK, width), lambda i: (jnp.minimum(i, N_CTX_TILES - 1), 0))

    def lat_rows(width):
        return pl.BlockSpec((TM_TOK, width), lambda i: (jnp.maximum(i - N_CTX_TILES, 0), 0))

    def full(shape):
        return pl.BlockSpec(shape, lambda i: (0, 0))

    return pl.pallas_call(
        _merge_kernel,
        grid=(T_ALL // TM_TOK,),
        in_specs=[rows(D_MODEL),
                  pl.BlockSpec((None, 6, D_MODEL), lambda i: (_seg_of_tile(i, TM_TOK), 0, 0)),
                  ctx_rows(D_A), lat_rows(D_A), ctx_rows(D_CONV), lat_rows(D_CONV),
                  ctx_rows(D_M), lat_rows(D_M),
                  rows(D_MODEL, ZG_GA), rows(D_MODEL, ZG_GC), rows(D_MODEL, ZG_GM),
                  full((D_A, D_MODEL)), full((D_CONV, D_MODEL)), full((D_M, D_MODEL)),
                  full((D_MODEL, D_MODEL))],
        out_specs=rows(D_MODEL),
        out_shape=jax.ShapeDtypeStruct((T_ALL, D_MODEL), F32),
        compiler_params=_cparams(("arbitrary",)),
        name="merge",
    )(x, mod, ya[0], ya[1], yc[0], yc[1], ym[0], ym[1], zg, zg, zg, w_pa, w_pc, w_pm, w_out)


def _route_sort_kernel(x_ref, g_ref, mod_ref, wr_ref, br_ref, xt_ref, pos_ref, gate_ref, nch_ref, seg_ref):
    h = _normmod(x_ref[...], g_ref[...], mod_ref[...], 3, 4)
    hb = h.astype(BF16)
    logits = _dot_nt(wr_ref[...].astype(BF16), hb) + br_ref[...]
    e_iota = lax.broadcasted_iota(jnp.int32, (N_EXPERTS, TM_MOE), 0).astype(F32)
    sels, vals = [], []
    l = logits
    for k in range(TOP_K):
        m = jnp.max(l, axis=0, keepdims=True)
        idx = jnp.min(jnp.where(l == m, e_iota, float(N_EXPERTS)), axis=0, keepdims=True)
        sel = e_iota == idx
        vals.append(m)
        sels.append(sel)
        l = jnp.where(sel, -jnp.inf, l)
    exps = [jnp.exp(v - vals[0]) for v in vals]
    tot = exps[0] + exps[1] + exps[2] + exps[3]
    onehot = jnp.zeros((N_EXPERTS, TM_MOE), F32)
    for k in range(TOP_K):
        gate_ref[k:k + 1, :] = exps[k] / tot
        onehot = onehot + sels[k].astype(F32)
    gate_ref[TOP_K:8, :] = jnp.zeros((8 - TOP_K, TM_MOE), F32)

    cnt = jnp.sum(onehot, axis=1, keepdims=True)
    nch = jnp.floor((cnt + (CHUNK_ROWS - 1)) / CHUNK_ROWS)
    ei = lax.broadcasted_iota(jnp.int32, (N_EXPERTS, N_EXPERTS), 0)
    ej = lax.broadcasted_iota(jnp.int32, (N_EXPERTS, N_EXPERTS), 1)
    seg = jnp.dot((ej < ei).astype(F32), jnp.broadcast_to(nch, (N_EXPERTS, 128)), precision=HIGHEST,
                  preferred_element_type=F32)
    nch_ref[...] = jnp.broadcast_to(nch, (N_EXPERTS, 128)).astype(jnp.int32)
    seg_ref[...] = seg.astype(jnp.int32)

    t_src = lax.broadcasted_iota(jnp.int32, (TM_MOE, TM_MOE), 0)
    t_dst = lax.broadcasted_iota(jnp.int32, (TM_MOE, TM_MOE), 1)
    before = (t_src < t_dst).astype(BF16)
    row_of = (seg[:, 0:1] * CHUNK_ROWS
              + jnp.dot(onehot.astype(BF16), before, preferred_element_type=F32))
    q_iota = lax.broadcasted_iota(jnp.int32, (Q_TILE, TM_MOE), 0)
    perm = jnp.zeros((Q_TILE, TM_MOE), F32)
    for k in range(TOP_K):
        q_k = jnp.sum(jnp.where(sels[k], row_of, 0.0), axis=0, keepdims=True).astype(jnp.int32)
        pos_ref[k:k + 1, :] = q_k
        perm = jnp.where(q_iota == q_k, 1.0, perm)
    pos_ref[TOP_K:8, :] = jnp.zeros((8 - TOP_K, TM_MOE), jnp.int32)
    xt_ref[...] = jnp.dot(perm.astype(BF16), hb, preferred_element_type=F32).astype(BF16)


def _route_sort(x, norm_g, mod, w_rt, b_r):
    tspec = pl.BlockSpec((8, TM_MOE), lambda i: (0, i))
    mspec = pl.BlockSpec((None, N_EXPERTS, 128), lambda i: (i, 0, 0))
    meta = jax.ShapeDtypeStruct((N_TILES, N_EXPERTS, 128), jnp.int32)
    return pl.pallas_call(
        _route_sort_kernel,
        grid=(N_TILES,),
        in_specs=[pl.BlockSpec((TM_MOE, D_MODEL), lambda i: (i, 0)),
                  pl.BlockSpec((1, D_MODEL), lambda i: (0, 0)),
                  pl.BlockSpec((None, 6, D_MODEL), lambda i: (_seg_of_tile(i, TM_MOE), 0, 0)),
                  pl.BlockSpec((N_EXPERTS, D_MODEL), lambda i: (0, 0)),
                  pl.BlockSpec((N_EXPERTS, 1), lambda i: (0, 0))],
        out_specs=[pl.BlockSpec((Q_TILE, D_MODEL), lambda i: (i, 0)), tspec, tspec, mspec, mspec],
        out_shape=[jax.ShapeDtypeStruct((N_TILES * Q_TILE, D_MODEL), BF16),
                   jax.ShapeDtypeStruct((8, T_ALL), jnp.int32), jax.ShapeDtypeStruct((8, T_ALL), F32),
                   meta, meta],
        compiler_params=_cparams(("arbitrary",)),
        name="moe_route_sort",
    )(x, norm_g, mod, w_rt, b_r)


def _expert_kernel(nch_ref, seg_ref, wgu_ref, bgu_ref, wd_ref, bd_ref, xt_ref, yt_ref,
                   wgu_scr, wd_scr, xbuf, ybuf, row_scr, gstart_scr, gsem, ssem):
    del xt_ref
    e = pl.program_id(0)

    def src_row(row):
        return pl.multiple_of(jnp.where(row >= 0, row, READ_SPARE), CHUNK_ROWS)

    def dst_row(row, slot, c):
        spare = c * Q_TILE + jnp.where(slot == 0, Q_TILE - CHUNK_ROWS, Q_TILE - 2 * CHUNK_ROWS)
        return pl.multiple_of(jnp.where(row >= 0, row, spare), CHUNK_ROWS)

    def chunk_rows(c):
        return slice(c * CHUNK_ROWS, (c + 1) * CHUNK_ROWS)

    def start_in(g):
        slot = g % 2
        for c in range(CPG):
            row = row_scr[g * CPG + c]
            pltpu.make_async_copy(yt_ref.at[pl.ds(src_row(row), CHUNK_ROWS), :], xbuf.at[slot, chunk_rows(c), :],
                                  gsem.at[slot]).start()

    def start_out(g, slot):
        for c in range(CPG):
            row = row_scr[g * CPG + c]
            pltpu.make_async_copy(ybuf.at[slot, chunk_rows(c), :],
                                  yt_ref.at[pl.ds(dst_row(row, slot, c), CHUNK_ROWS), :], ssem.at[slot]).start()

    def wait_in(slot):
        pltpu.make_async_copy(yt_ref.at[pl.ds(0, E_GROUP), :], xbuf.at[slot], gsem.at[slot]).wait()

    def wait_out(slot):
        pltpu.make_async_copy(ybuf.at[slot], yt_ref.at[pl.ds(0, E_GROUP), :], ssem.at[slot]).wait()

    @pl.when(e == 0)
    def _():
        def per_expert(ee, cnt):
            gstart_scr[ee] = cnt // CPG

            def per_tile(t, cnt):
                first = (t * CH_PER_TILE + seg_ref[t * N_EXPERTS + ee]) * CHUNK_ROWS

                def per_chunk(j, cnt):
                    row_scr[cnt] = first + j * CHUNK_ROWS
                    return cnt + 1

                return lax.fori_loop(0, nch_ref[t * N_EXPERTS + ee], per_chunk, cnt)

            cnt = lax.fori_loop(0, N_TILES, per_tile, cnt)
            padded = (cnt + CPG - 1) // CPG * CPG

            def pad(i, carry):
                row_scr[i] = -1
                return carry

            lax.fori_loop(cnt, padded, pad, 0)
            return padded

        total = lax.fori_loop(0, N_EXPERTS, per_expert, 0)
        gstart_scr[N_EXPERTS] = total // CPG

        def pad(i, carry):
            row_scr[i] = -1
            return carry

        lax.fori_loop(total, total + CPG, pad, 0)
        ybuf[...] = jnp.zeros_like(ybuf)
        start_out(total // CPG, 0)
        start_out(total // CPG, 1)
        start_in(0)

    g_first = gstart_scr[e]
    g_end = gstart_scr[e + 1]

    @pl.when(g_end > g_first)
    def _():
        wgu_scr[...] = wgu_ref[...].astype(BF16)
        wd_scr[...] = wd_ref[...].astype(BF16)

    def group_step(g, carry):
        slot = g % 2
        start_in(g + 1)
        wait_in(slot)
        wait_out(slot)
        hgu = jnp.dot(xbuf[slot], wgu_scr[...], preferred_element_type=F32) + bgu_ref[...]
        h_glu = jnp.minimum(hgu[:, :D_EXPERT], SWIGLU_LIMIT)
        h_lin = jnp.clip(hgu[:, D_EXPERT:], -SWIGLU_LIMIT, SWIGLU_LIMIT)
        act = (h_lin + 1.0) * (h_glu * _sigmoid(SWIGLU_ALPHA * h_glu))
        y = jnp.dot(act.astype(BF16), wd_scr[...], preferred_element_type=F32) + bd_ref[...]
        ybuf[slot] = y.astype(BF16)
        start_out(g, slot)
        return carry

    lax.fori_loop(g_first, g_end, group_step, 0)

    @pl.when(e == N_EXPERTS - 1)
    def _():
        wait_in(gstart_scr[N_EXPERTS] % 2)
        wait_out(0)
        wait_out(1)


def _experts(nch_flat, seg_flat, xt, w_gu, b_gu, w_down, b_down, layer):
    return pl.pallas_call(
        _expert_kernel,
        grid_spec=pltpu.PrefetchScalarGridSpec(
            num_scalar_prefetch=2,
            grid=(N_EXPERTS,),
            in_specs=[pl.BlockSpec((None, None, D_MODEL, 2 * D_EXPERT), lambda e, n, s: (layer, e, 0, 0)),
                      pl.BlockSpec((None, None, 1, 2 * D_EXPERT), lambda e, n, s: (layer, e, 0, 0)),
                      pl.BlockSpec((None, None, D_EXPERT, D_MODEL), lambda e, n, s: (layer, e, 0, 0)),
                      pl.BlockSpec((None, None, 1, D_MODEL), lambda e, n, s: (layer, e, 0, 0)),
                      pl.BlockSpec(memory_space=pl.ANY)],
            out_specs=pl.BlockSpec(memory_space=pl.ANY),
            scratch_shapes=[pltpu.VMEM((D_MODEL, 2 * D_EXPERT), BF16),
                            pltpu.VMEM((D_EXPERT, D_MODEL), BF16),
                            pltpu.VMEM((2, E_GROUP, D_MODEL), BF16),
                            pltpu.VMEM((2, E_GROUP, D_MODEL), BF16),
                            pltpu.SMEM((MAX_CHUNKS,), jnp.int32),
                            pltpu.SMEM((N_EXPERTS + 1,), jnp.int32),
                            pltpu.SemaphoreType.DMA((2,)),
                            pltpu.SemaphoreType.DMA((2,))]),
        out_shape=jax.ShapeDtypeStruct((N_TILES * Q_TILE, D_MODEL), BF16),
        input_output_aliases={6: 0},
        compiler_params=_cparams(("arbitrary",)),
        name="moe_experts",
    )(nch_flat, seg_flat, w_gu, b_gu, w_down, b_down, xt)


MOE_CTX_TILES = T_CTX // TM_MOE


def _combine_rows(x_ref, mod_ref, pos_ref, gate_ref, yt_ref):
    lane = lax.broadcasted_iota(jnp.int32, (TM_MOE, Q_TILE), 1)
    sel = jnp.zeros((TM_MOE, Q_TILE), F32)
    for k in range(TOP_K):
        sel = jnp.where(lane == pos_ref[:, k:k + 1], gate_ref[:, k:k + 1], sel)
    acc = jnp.dot(sel.astype(BF16), yt_ref[...], preferred_element_type=F32)
    return x_ref[...] + mod_ref[5:6, :] * acc


def _combine_kernel(x_ref, mod_ref, pos_ref, gate_ref, yt_ref, o_ref):
    o_ref[...] = _combine_rows(x_ref, mod_ref, pos_ref, gate_ref, yt_ref)


def _combine_final_kernel(x_ref, mod_ref, pos_ref, gate_ref, yt_ref, fg_ref, ctx_ref, lat_ref):
    y = _combine_rows(x_ref, mod_ref, pos_ref, gate_ref, yt_ref)
    y = y * lax.rsqrt(jnp.mean(y * y, axis=-1, keepdims=True) + RMS_EPS) * fg_ref[...]
    is_ctx = pl.program_id(0) < MOE_CTX_TILES

    @pl.when(is_ctx)
    def _():
        ctx_ref[...] = y

    @pl.when(jnp.logical_not(is_ctx))
    def _():
        lat_ref[...] = y


def _combine(x, mod, pos_t, gate_t, yt, final_g=None):
    tile = pl.BlockSpec((TM_MOE, D_MODEL), lambda i: (i, 0))
    in_specs = [tile,
                pl.BlockSpec((None, 6, D_MODEL), lambda i: (_seg_of_tile(i, TM_MOE), 0, 0)),
                pl.BlockSpec((TM_MOE, 8), lambda i: (i, 0)),
                pl.BlockSpec((TM_MOE, 8), lambda i: (i, 0)),
                pl.BlockSpec((Q_TILE, D_MODEL), lambda i: (i, 0))]
    if final_g is None:
        return pl.pallas_call(
            _combine_kernel, grid=(N_TILES,), in_specs=in_specs, out_specs=tile,
            out_shape=jax.ShapeDtypeStruct((T_ALL, D_MODEL), F32),
            compiler_params=_cparams(("arbitrary",)), name="moe_combine",
        )(x, mod, pos_t, gate_t, yt)
    return pl.pallas_call(
        _combine_final_kernel, grid=(N_TILES,),
        in_specs=in_specs + [pl.BlockSpec((1, D_MODEL), lambda i: (0, 0))],
        out_specs=[pl.BlockSpec((TM_MOE, D_MODEL), lambda i: (jnp.minimum(i, MOE_CTX_TILES - 1), 0)),
                   pl.BlockSpec((TM_MOE, D_MODEL), lambda i: (jnp.maximum(i - MOE_CTX_TILES, 0), 0))],
        out_shape=[jax.ShapeDtypeStruct((T_CTX, D_MODEL), F32), jax.ShapeDtypeStruct((T_LAT, D_MODEL), F32)],
        compiler_params=_cparams(("arbitrary",)), name="moe_combine_final",
    )(x, mod, pos_t, gate_t, yt, final_g)


def _moe(x, norm_g, mod, w_rt, b_r, w_gu, b_gu, w_down, b_down, layer, final_g=None):
    xt, pos, gate, nch, seg = _route_sort(x, norm_g, mod, w_rt, b_r)
    yt = _experts(nch[:, :, 0].reshape(-1), seg[:, :, 0].reshape(-1), xt, w_gu,
                  b_gu.reshape(DEPTH, N_EXPERTS, 1, 2 * D_EXPERT), w_down,
                  b_down.reshape(DEPTH, N_EXPERTS, 1, D_MODEL), layer)
    return _combine(x, mod, pos.T, gate.T, yt, final_g)


def _split_in_cols(w):
    conv0, mq0, om0 = 3 * D_A, 3 * D_A + 2 * D_CONV, 3 * D_A + 2 * D_CONV + 3 * D_M
    gates_end = GATE_OFF + N_GATE_M
    pad = jnp.zeros(w.shape[:-1] + (N_ZF - (2 * D_CONV + D_M + N_GATE_M),), w.dtype)
    zb = jnp.concatenate([w[..., :conv0], w[..., mq0:om0]], axis=-1)
    zg = w[..., gates_end:]
    zf = jnp.concatenate([w[..., conv0:mq0], w[..., om0:GATE_OFF], w[..., GATE_OFF:gates_end], pad], axis=-1)
    return zb, zg, zf


def kernel(x_prompt, x_sample, cache_k, cache_v, state_C, state_n, state_m, c, c_ctx, norm1_g, w_mod, b_mod, w_in, b_in, rpb, w_dw, b_dw, cln_g, cln_b, mnorm_g, w_pa, w_pc, w_pm, w_out, norm2_g, w_router, b_router, w_gu, b_gu, w_down, b_down, final_g):
    cond = jnp.concatenate([c_ctx[None, :], c, jnp.zeros((SEG_PAD - N_SEG, D_MODEL), F32)], axis=0)
    mod_all = _modulation(cond, w_mod, b_mod).reshape(DEPTH, SEG_PAD, 6, D_MODEL)

    x = jnp.concatenate([x_prompt.reshape(T_CTX, D_MODEL), x_sample.reshape(T_LAT, D_MODEL)], axis=0)
    ck = cache_k.reshape(DEC_BATCH, DEPTH, PAST_LEN, D_A).astype(BF16)
    cv = cache_v.reshape(DEC_BATCH, DEPTH, PAST_LEN, D_A).astype(BF16)
    lat_c0 = state_C.reshape(DEC_BATCH, DEPTH, N_STREAM, HEAD_DIM_M, HEAD_DIM_M)
    lat_n0 = state_n.reshape(DEC_BATCH, DEPTH, N_STREAM, HEAD_DIM_M)
    lat_m0 = jnp.broadcast_to(state_m.reshape(DEC_BATCH, DEPTH, N_STREAM, 1),
                              (DEC_BATCH, DEPTH, N_STREAM, HEAD_DIM_M))
    ctx_c0 = jnp.zeros((1, N_STREAM, HEAD_DIM_M, HEAD_DIM_M), F32)
    ctx_n0 = jnp.zeros((1, N_STREAM, HEAD_DIM_M), F32)
    ctx_m0 = jnp.full((1, N_STREAM, HEAD_DIM_M), -jnp.inf, F32)

    ks, vs, cs, ns, ms = [], [], [], [], []
    for l in range(DEPTH):
        mod = mod_all[l]
        g1 = norm1_g[l][None, :]
        zb, zg, zf, kv = _in_proj(x, g1, mod, _split_in_cols(w_in[l].astype(BF16)),
                                  _split_in_cols(b_in[l][None, :]))
        ya = (_ctx_attention(zb), _natt(zb, ck, cv, _natt_bias(rpb[l]), l))
        conv_w = (w_dw[l], b_dw[l][None, :], cln_g[l][None, :], cln_b[l][None, :])
        yc = (_conv(zf, *conv_w, SEQ, BATCH, 0), _conv(zf, *conv_w, DEC_SEQ, DEC_BATCH, T_CTX // DEC_SEQ))
        ng = mnorm_g[l][None, :]
        ym_ctx, c_l, n_l, m_l = _mlstm(zb, zf, ctx_c0, ctx_n0, ctx_m0, ng, SEQ, BATCH, 0, lambda b: (0,))
        ym_lat, _, _, _ = _mlstm(zb, zf, lat_c0, lat_n0, lat_m0, ng, DEC_SEQ, DEC_BATCH, T_CTX // DEC_SEQ,
                                 lambda b: (b, l))
        x = _merge(x, mod, ya, yc, (ym_ctx, ym_lat), zg, w_pa[l].astype(BF16), w_pc[l].astype(BF16),
                   w_pm[l].astype(BF16), w_out[l].astype(BF16))
        x = _moe(x, norm2_g[l][None, :], mod, w_router[l].T, b_router[l][:, None],
                 w_gu, b_gu, w_down, b_down, l, final_g[None, :] if l == DEPTH - 1 else None)
        ks.append(kv[:, :D_A].reshape(BATCH, SEQ, N_HEADS_A, HEAD_DIM_A))
        vs.append(kv[:, D_A:].reshape(BATCH, SEQ, N_HEADS_A, HEAD_DIM_A))
        cs.append(c_l.reshape(BATCH, 2, N_HEADS_M, HEAD_DIM_M, HEAD_DIM_M))
        ns.append(n_l.reshape(BATCH, 2, N_HEADS_M, HEAD_DIM_M))
        ms.append(m_l[:, :, 0].reshape(BATCH, 2, N_HEADS_M))

    y_ctx, y_lat = x
    return (y_ctx.reshape(BATCH, SEQ, D_MODEL), y_lat.reshape(DEC_BATCH, DEC_SEQ, D_MODEL),
            jnp.stack(ks, axis=1), jnp.stack(vs, axis=1), jnp.stack(cs, axis=1),
            jnp.stack(ns, axis=1), jnp.stack(ms, axis=1))
```

```python
import functools

import numpy as np
import jax
import jax.numpy as jnp
from jax import lax
from jax.experimental import pallas as pl
from jax.experimental.pallas import tpu as pltpu

F32 = jnp.float32
BF16 = jnp.bfloat16
HIGHEST = lax.Precision.HIGHEST

D_MODEL = 1024
BATCH = 16
SEQ = 256
DEPTH = 2
DEC_BATCH = 8
DEC_SEQ = 1024
PAST_LEN = 512
GRID_W = 64
N_HEADS_A = 8
HEAD_DIM_A = 64
D_A = N_HEADS_A * HEAD_DIM_A
WIN_ROWS = 8
WIN_COLS = 16
D_CONV = 512
CONV_WIDTH = 31
N_HEADS_M = 4
HEAD_DIM_M = 128
D_M = N_HEADS_M * HEAD_DIM_M
N_GATE_M = 4 * N_HEADS_M
CHUNK = 64
N_EXPERTS = 32
TOP_K = 4
D_EXPERT = 1024
SWIGLU_ALPHA = 1.702
SWIGLU_LIMIT = 7.0
RMS_EPS = 1e-6
LN_EPS = 1e-5
GATE_OFF = 3 * D_A + 2 * D_CONV + 4 * D_M
N_IN = GATE_OFF + N_GATE_M + 3 * D_MODEL

T_CTX = BATCH * SEQ
T_LAT = DEC_BATCH * DEC_SEQ
T_ALL = T_CTX + T_LAT
N_SEG = 1 + DEC_BATCH
SEG_PAD = 16
GRID_ROWS = DEC_SEQ // GRID_W
NEG = -1e30

N_ZB = 6 * 512
N_ZG = 3 * D_MODEL
N_ZF = 1664
QB_QA, QB_KA, QB_VA, QB_QM, QB_KM, QB_VM = 0, 1, 2, 3, 4, 5
ZG_GA, ZG_GC, ZG_GM = 0, 1, 2
ZF_CU, ZF_CG, ZF_OM = 0, 1, 2
ZF_GATES = 12

TM_TOK = 512
TM_PROJ = 256
TM_MOE = 512
N_TILES = T_ALL // TM_MOE
CHUNK_ROWS = 16
MXU_ROWS = 256
Q_TILE = -(-(TM_MOE * TOP_K + N_EXPERTS * (CHUNK_ROWS - 1)) // MXU_ROWS) * MXU_ROWS
CH_PER_TILE = Q_TILE // CHUNK_ROWS
E_GROUP = 256
CPG = E_GROUP // CHUNK_ROWS
MAX_CHUNKS = ((T_ALL * TOP_K + N_TILES * N_EXPERTS * (CHUNK_ROWS - 1)) // CHUNK_ROWS
              + N_EXPERTS * (CPG - 1)) + CPG
assert Q_TILE - (TM_MOE * TOP_K + N_EXPERTS * (CHUNK_ROWS - 1)) >= 2 * CHUNK_ROWS and N_TILES > CPG
READ_SPARE = N_TILES * Q_TILE - CHUNK_ROWS
VMEM_LIMIT = 60 * 1024 * 1024


def _cparams(sem=None):
    return pltpu.CompilerParams(dimension_semantics=sem, vmem_limit_bytes=VMEM_LIMIT)


def _seg_of_tile(i, tile):
    n_ctx = T_CTX // tile
    per_lat = DEC_SEQ // tile
    return jnp.where(i < n_ctx, 0, 1 + (i - n_ctx) // per_lat)


def _dot_nt(a, b):
    return lax.dot_general(a, b, (((1,), (1,)), ((), ())), preferred_element_type=F32)


def _sigmoid(x):
    return 1.0 / (1.0 + jnp.exp(-x))


def _mod_kernel(c_ref, w_ref, b_ref, o_ref):
    c = c_ref[...]
    s = c * _sigmoid(c)
    o_ref[...] = jnp.dot(s, w_ref[...], precision=HIGHEST, preferred_element_type=F32) + b_ref[...]


def _modulation(cond, w_mod, b_mod):
    tn = 1536
    return pl.pallas_call(
        _mod_kernel,
        grid=(DEPTH, 6 * D_MODEL // tn),
        in_specs=[pl.BlockSpec((SEG_PAD, D_MODEL), lambda l, j: (0, 0)),
                  pl.BlockSpec((None, D_MODEL, tn), lambda l, j: (l, 0, j)),
                  pl.BlockSpec((None, 1, tn), lambda l, j: (l, 0, j))],
        out_specs=pl.BlockSpec((None, SEG_PAD, tn), lambda l, j: (l, 0, j)),
        out_shape=jax.ShapeDtypeStruct((DEPTH, SEG_PAD, 6 * D_MODEL), F32),
        compiler_params=_cparams(("arbitrary", "arbitrary")),
        name="modulation",
    )(cond, w_mod, b_mod.reshape(DEPTH, 1, 6 * D_MODEL))


def _normmod(x, g, mod, shift_idx, scale_idx):
    y = x * lax.rsqrt(jnp.mean(x * x, axis=-1, keepdims=True) + RMS_EPS) * g
    return y * (1.0 + mod[scale_idx:scale_idx + 1, :]) + mod[shift_idx:shift_idx + 1, :]


PROJ_CTX_TILES = T_CTX // TM_PROJ


def _in_proj_kernel(x_ref, g_ref, mod_ref, wb_ref, bb_ref, wg_ref, bg_ref, wf_ref, bf_ref,
                    zb_ref, zg_ref, zf_ref, kv_ref):
    h = _normmod(x_ref[...], g_ref[...], mod_ref[...], 0, 1).astype(BF16)
    acc = jnp.dot(h, wb_ref[...], preferred_element_type=F32) + bb_ref[...]
    zb_ref[...] = acc.astype(BF16)

    @pl.when(pl.program_id(0) < PROJ_CTX_TILES)
    def _():
        kv_ref[...] = acc[:, D_A:3 * D_A]

    gates = jnp.dot(h, wg_ref[...], preferred_element_type=F32) + bg_ref[...]
    zg_ref[...] = _sigmoid(gates).astype(BF16)
    zf_ref[...] = jnp.dot(h, wf_ref[...], preferred_element_type=F32) + bf_ref[...]


def _in_proj(x, norm_g, mod, w, b):
    def full(a):
        return pl.BlockSpec(a.shape, lambda i: (0, 0))

    def rows(n):
        return pl.BlockSpec((TM_PROJ, n), lambda i: (i, 0))

    return pl.pallas_call(
        _in_proj_kernel,
        grid=(T_ALL // TM_PROJ,),
        in_specs=[rows(D_MODEL),
                  pl.BlockSpec((1, D_MODEL), lambda i: (0, 0)),
                  pl.BlockSpec((None, 6, D_MODEL), lambda i: (_seg_of_tile(i, TM_PROJ), 0, 0)),
                  full(w[0]), full(b[0]), full(w[1]), full(b[1]), full(w[2]), full(b[2])],
        out_specs=[rows(N_ZB), rows(N_ZG), rows(N_ZF),
                   pl.BlockSpec((TM_PROJ, 2 * D_A), lambda i: (jnp.minimum(i, PROJ_CTX_TILES - 1), 0))],
        out_shape=[jax.ShapeDtypeStruct((T_ALL, N_ZB), BF16), jax.ShapeDtypeStruct((T_ALL, N_ZG), BF16),
                   jax.ShapeDtypeStruct((T_ALL, N_ZF), F32), jax.ShapeDtypeStruct((T_CTX, 2 * D_A), F32)],
        compiler_params=_cparams(("arbitrary",)),
        name="in_proj",
    )(x, norm_g, mod, w[0], b[0], w[1], b[1], w[2], b[2])


HEAD_PAIR = 2 * HEAD_DIM_A
ATT_SCALE = HEAD_DIM_A ** -0.5


def _pair_queries(q2):
    lo = lax.broadcasted_iota(jnp.int32, (1, HEAD_PAIR), 1) < HEAD_DIM_A
    q2 = q2 * ATT_SCALE
    zero = jnp.zeros_like(q2)
    return lo, jnp.concatenate([jnp.where(lo, q2, zero), jnp.where(lo, zero, q2)], axis=0)


def _unpair(lo, o_stacked):
    rows = o_stacked.shape[0] // 2
    return jnp.where(lo, o_stacked[:rows], o_stacked[rows:])


def _ctx_attn_kernel(q_ref, k_ref, v_ref, o_ref):
    for hp in range(N_HEADS_A // 2):
        sl = slice(hp * HEAD_PAIR, (hp + 1) * HEAD_PAIR)
        lo, qs = _pair_queries(q_ref[:, sl])
        s = _dot_nt(qs, k_ref[:, sl])
        p = jnp.exp(s - jnp.max(s, axis=-1, keepdims=True))
        l = jnp.sum(p, axis=-1, keepdims=True)
        o = jnp.dot(p.astype(BF16), v_ref[:, sl], preferred_element_type=F32) / l
        o_ref[:, sl] = _unpair(lo, o)


def _ctx_attention(zb):
    def spec(cb):
        return pl.BlockSpec((SEQ, D_A), lambda b: (b, cb))

    return pl.pallas_call(
        _ctx_attn_kernel,
        grid=(BATCH,),
        in_specs=[spec(QB_QA), spec(QB_KA), spec(QB_VA)],
        out_specs=pl.BlockSpec((SEQ, D_A), lambda b: (b, 0)),
        out_shape=jax.ShapeDtypeStruct((T_CTX, D_A), F32),
        compiler_params=_cparams(("arbitrary",)),
        name="ctx_attention",
    )(zb, zb, zb)


NQ_ROWS = 4
NW_ROWS = 12
NQ_BLOCKS = GRID_ROWS // NQ_ROWS
NQ_TOK = NQ_ROWS * GRID_W
NW_TOK = NW_ROWS * GRID_W


def _window_row(qb, xp):
    return xp.clip(qb * NQ_ROWS - WIN_ROWS // 2, 0, GRID_ROWS - NW_ROWS)


def _natt_kernel(q_ref, k_ref, v_ref, kc_ref, vc_ref, bias_ref, o_ref):
    start = pl.multiple_of(_window_row(pl.program_id(0), jnp) * GRID_W, NQ_TOK)
    band = NW_TOK
    for hp in range(N_HEADS_A // 2):
        sl = slice(hp * HEAD_PAIR, (hp + 1) * HEAD_PAIR)
        lo, qs = _pair_queries(q_ref[:, sl])
        bias = bias_ref[2 * hp:2 * hp + 2].reshape(2 * NQ_TOK, band)
        s_loc = _dot_nt(qs, k_ref[pl.ds(start, band), sl]) + bias
        s_ctx = _dot_nt(qs, kc_ref[:, sl])
        m = jnp.maximum(jnp.max(s_loc, axis=-1, keepdims=True), jnp.max(s_ctx, axis=-1, keepdims=True))
        p_loc = jnp.exp(s_loc - m)
        p_ctx = jnp.exp(s_ctx - m)
        l = jnp.sum(p_loc, axis=-1, keepdims=True) + jnp.sum(p_ctx, axis=-1, keepdims=True)
        o = (jnp.dot(p_loc.astype(BF16), v_ref[pl.ds(start, band), sl], preferred_element_type=F32)
             + jnp.dot(p_ctx.astype(BF16), vc_ref[:, sl], preferred_element_type=F32))
        o_ref[:, sl] = _unpair(lo, o / l)


def _natt_bias(rpb_l):
    qc = np.arange(GRID_W)
    kc = np.arange(GRID_W)
    cs = np.clip(qc - WIN_COLS // 2, 0, GRID_W - WIN_COLS)
    ok = (kc[None, :] >= cs[:, None]) & (kc[None, :] < cs[:, None] + WIN_COLS)
    dc = np.clip(kc[None, :] - qc[:, None] + WIN_COLS - 1, 0, 2 * WIN_COLS - 2)
    pick = (dc[None] == np.arange(2 * WIN_COLS - 1)[:, None, None]).astype(np.float32)
    toe = jnp.einsum('hdc,cqk->hdqk', rpb_l, jnp.asarray(pick), precision=HIGHEST)
    toe = jnp.where(jnp.asarray(ok)[None, None], toe, NEG)
    n_rel = 2 * WIN_ROWS - 1
    toe = jnp.concatenate([toe, jnp.full((N_HEADS_A, 1, GRID_W, GRID_W), NEG, F32)], axis=1)
    r = (np.arange(NQ_BLOCKS)[:, None, None] * NQ_ROWS + np.arange(NQ_ROWS)[None, :, None])
    krow = _window_row(np.arange(NQ_BLOCKS), np)[:, None, None] + np.arange(NW_ROWS)[None, None, :]
    rs = np.clip(r - WIN_ROWS // 2, 0, GRID_ROWS - WIN_ROWS)
    assert ((rs >= krow[:, :, :1]) & (rs + WIN_ROWS <= krow[:, :, -1:] + 1)).all()
    rel = np.where((krow >= rs) & (krow < rs + WIN_ROWS), krow - r + WIN_ROWS - 1, n_rel)
    tiles = jnp.take(toe, jnp.asarray(rel.reshape(-1), jnp.int32), axis=1)
    tiles = tiles.reshape(N_HEADS_A, NQ_BLOCKS, NQ_ROWS, NW_ROWS, GRID_W, GRID_W)
    return tiles.transpose(1, 0, 2, 4, 3, 5).reshape(NQ_BLOCKS, N_HEADS_A, NQ_TOK, NW_TOK)


def _natt(zb, cache_k, cache_v, bias, layer):
    lat0 = T_CTX // DEC_SEQ
    row0 = T_CTX // NQ_TOK
    return pl.pallas_call(
        _natt_kernel,
        grid=(NQ_BLOCKS, DEC_BATCH),
        in_specs=[pl.BlockSpec((NQ_TOK, D_A), lambda qb, b: (row0 + b * NQ_BLOCKS + qb, QB_QA)),
                  pl.BlockSpec((DEC_SEQ, D_A), lambda qb, b: (lat0 + b, QB_KA)),
                  pl.BlockSpec((DEC_SEQ, D_A), lambda qb, b: (lat0 + b, QB_VA)),
                  pl.BlockSpec((None, None, PAST_LEN, D_A), lambda qb, b: (b, layer, 0, 0)),
                  pl.BlockSpec((None, None, PAST_LEN, D_A), lambda qb, b: (b, layer, 0, 0)),
                  pl.BlockSpec((None, N_HEADS_A, NQ_TOK, NW_TOK), lambda qb, b: (qb, 0, 0, 0))],
        out_specs=pl.BlockSpec((NQ_TOK, D_A), lambda qb, b: (b * NQ_BLOCKS + qb, 0)),
        out_shape=jax.ShapeDtypeStruct((T_LAT, D_A), F32),
        compiler_params=_cparams(("arbitrary", "arbitrary")),
        name="nbr_attention",
    )(zb, zb, zb, cache_k, cache_v, bias)


CONV_HALO = 16
CONV_ROWS = 64


SUBLANES = 8


def _conv_kernel(u_ref, g_ref, w_ref, b_ref, lg_ref, lb_ref, o_ref, pad_scr, sh_scr, *, seq):
    zeros = jnp.zeros((CONV_HALO, D_CONV), F32)
    pad_scr[0:CONV_HALO, :] = zeros
    pad_scr[CONV_HALO + seq:2 * CONV_HALO + seq, :] = zeros
    pad_scr[CONV_HALO:CONV_HALO + seq, :] = u_ref[...] * _sigmoid(g_ref[...])
    n_sh = seq + 2 * CONV_HALO - SUBLANES
    for s in range(SUBLANES):
        sh_scr[s] = pad_scr[s:s + n_sh, :]
    first = CONV_HALO - CONV_WIDTH // 2
    for c in range(seq // CONV_ROWS):
        base = c * CONV_ROWS
        acc = jnp.broadcast_to(b_ref[...], (CONV_ROWS, D_CONV))
        for j in range(CONV_WIDTH):
            q, s = divmod(first + j, SUBLANES)
            row0 = base + q * SUBLANES
            acc = acc + sh_scr[s, row0:row0 + CONV_ROWS, :] * w_ref[j:j + 1, :]
        mu = jnp.mean(acc, axis=-1, keepdims=True)
        xc = acc - mu
        var = jnp.mean(xc * xc, axis=-1, keepdims=True)
        y = xc * lax.rsqrt(var + LN_EPS) * lg_ref[...] + lb_ref[...]
        o_ref[base:base + CONV_ROWS, :] = y * _sigmoid(y)


def _conv(z, w_dw, b_dw, ln_g, ln_b, seq, n_seq, row_block0):
    def vec():
        return pl.BlockSpec((1, D_CONV), lambda b: (0, 0))

    return pl.pallas_call(
        functools.partial(_conv_kernel, seq=seq),
        grid=(n_seq,),
        in_specs=[pl.BlockSpec((seq, D_CONV), lambda b: (row_block0 + b, ZF_CU)),
                  pl.BlockSpec((seq, D_CONV), lambda b: (row_block0 + b, ZF_CG)),
                  pl.BlockSpec((CONV_WIDTH, D_CONV), lambda b: (0, 0)),
                  vec(), vec(), vec()],
        out_specs=pl.BlockSpec((seq, D_CONV), lambda b: (b, 0)),
        out_shape=jax.ShapeDtypeStruct((n_seq * seq, D_CONV), F32),
        scratch_shapes=[pltpu.VMEM((seq + 2 * CONV_HALO, D_CONV), F32),
                        pltpu.VMEM((SUBLANES, seq + 2 * CONV_HALO - SUBLANES, D_CONV), F32)],
        compiler_params=_cparams(("arbitrary",)),
        name="conformer_conv",
    )(z, z, w_dw, b_dw, ln_g, ln_b)


N_STREAM = 2 * N_HEADS_M
MCHUNK = 128


def _mlstm_kernel(q_ref, k_ref, v_ref, om_ref, gt_ref, c0_ref, n0_ref, m0_ref, ng_ref,
                  y_ref, c_out, n_out, m_out, hf_scr, hb_scr, c_scr, n_scr, m_scr, *, seq):
    nc = seq // MCHUNK
    c_scr[...] = c0_ref[...]
    n_scr[...] = n0_ref[...]
    m_scr[...] = m0_ref[...]
    ti = lax.broadcasted_iota(jnp.int32, (MCHUNK, MCHUNK), 0)
    si = lax.broadcasted_iota(jnp.int32, (MCHUNK, MCHUNK), 1)
    masks = (si <= ti, si >= ti)
    tris = tuple(mk.astype(F32) for mk in masks)
    kscale = HEAD_DIM_M ** -0.5

    def chunk_step(c, carry):
        for d in range(2):
            cidx = c if d == 0 else nc - 1 - c
            off = pl.multiple_of(cidx * MCHUNK, MCHUNK)
            last = MCHUNK - 1 if d == 0 else 0
            g = gt_ref[pl.ds(off, MCHUNK), :]
            lf = jnp.minimum(g, 0.0) - jnp.log(1.0 + jnp.exp(-jnp.abs(g)))
            cum = jnp.dot(tris[d], lf, precision=HIGHEST, preferred_element_type=F32)
            g_t = g.T
            cum_t = cum.T
            for h in range(N_HEADS_M):
                s_id = N_HEADS_M * d + h
                icol = 2 * N_HEADS_M * d + h
                fcol = icol + N_HEADS_M
                hs = slice(h * HEAD_DIM_M, (h + 1) * HEAD_DIM_M)
                b_col = cum[:, fcol:fcol + 1]
                i_col = g[:, icol:icol + 1]
                b_row = cum_t[fcol:fcol + 1, :]
                i_row = g_t[icol:icol + 1, :]
                b_last = b_col[last:last + 1, :]
                m_prev = m_scr[s_id:s_id + 1, 0:1]
                log_d = jnp.where(masks[d], b_col - b_row + i_row, NEG)
                inter = b_col + m_prev
                m_t = jnp.maximum(inter, jnp.max(log_d, axis=-1, keepdims=True))
                w_inter = jnp.exp(inter - m_t)
                qb = q_ref[pl.ds(off, MCHUNK), hs]
                kb = k_ref[pl.ds(off, MCHUNK), hs]
                vb = v_ref[pl.ds(off, MCHUNK), hs]
                c_prev = c_scr[s_id]
                n_prev = n_scr[s_id:s_id + 1, :]
                s_mat = _dot_nt(qb, kb) * (kscale * jnp.exp(log_d - m_t))
                num = (w_inter * _dot_nt(qb, c_prev.astype(BF16))
                       + jnp.dot(s_mat.astype(BF16), vb, preferred_element_type=F32))
                den = (w_inter * jnp.sum(qb.astype(F32) * n_prev, axis=-1, keepdims=True)
                       + jnp.sum(s_mat, axis=-1, keepdims=True))
                hh = num / jnp.maximum(jnp.abs(den), jnp.exp(-m_t))
                if d == 0:
                    hf_scr[pl.ds(off, MCHUNK), hs] = hh
                else:
                    hb_scr[pl.ds(off, MCHUNK), hs] = hh
                m_new = m_t[last:last + 1, :]
                w_prev = jnp.exp(b_last + m_prev - m_new)
                w_src = kscale * jnp.exp(b_last - b_col + i_col - m_new)
                upd = jnp.dot((w_src * vb.astype(F32)).T.astype(BF16), kb, preferred_element_type=F32)
                c_scr[s_id] = w_prev * c_prev + upd
                n_scr[s_id:s_id + 1, :] = (w_prev * n_prev
                                           + jnp.sum(w_src * kb.astype(F32), axis=0, keepdims=True))
                m_scr[s_id:s_id + 1, :] = jnp.broadcast_to(m_new, (1, HEAD_DIM_M))
        return carry

    lax.fori_loop(0, nc, chunk_step, 0)

    for h in range(N_HEADS_M):
        hs = slice(h * HEAD_DIM_M, (h + 1) * HEAD_DIM_M)
        hsum = hf_scr[:, hs] + hb_scr[:, hs]
        mu = jnp.mean(hsum, axis=-1, keepdims=True)
        xc = hsum - mu
        var = jnp.mean(xc * xc, axis=-1, keepdims=True)
        hn = xc * lax.rsqrt(var + LN_EPS) * ng_ref[:, hs]
        y_ref[:, hs] = _sigmoid(om_ref[:, hs]) * hn
    c_out[...] = c_scr[...]
    n_out[...] = n_scr[...]
    m_out[...] = m_scr[...]


def _mlstm(zb, zf, c0, n0, m0, norm_g, seq, n_seq, row_block0, state_map):
    lead = len(state_map(0))

    def zspec(cb):
        return pl.BlockSpec((seq, D_M), lambda b: (row_block0 + b, cb))

    def sspec(tail):
        return pl.BlockSpec((None,) * lead + tail, lambda b: state_map(b) + (0,) * len(tail))

    return pl.pallas_call(
        functools.partial(_mlstm_kernel, seq=seq),
        grid=(n_seq,),
        in_specs=[zspec(QB_QM), zspec(QB_KM), zspec(QB_VM), zspec(ZF_OM),
                  pl.BlockSpec((seq, 128), lambda b: (row_block0 + b, ZF_GATES)),
                  sspec((N_STREAM, HEAD_DIM_M, HEAD_DIM_M)),
                  sspec((N_STREAM, HEAD_DIM_M)),
                  sspec((N_STREAM, HEAD_DIM_M)),
                  pl.BlockSpec((1, D_M), lambda b: (0, 0))],
        out_specs=[pl.BlockSpec((seq, D_M), lambda b: (b, 0)),
                   pl.BlockSpec((None, N_STREAM, HEAD_DIM_M, HEAD_DIM_M), lambda b: (b, 0, 0, 0)),
                   pl.BlockSpec((None, N_STREAM, HEAD_DIM_M), lambda b: (b, 0, 0)),
                   pl.BlockSpec((None, N_STREAM, HEAD_DIM_M), lambda b: (b, 0, 0))],
        out_shape=[jax.ShapeDtypeStruct((n_seq * seq, D_M), F32),
                   jax.ShapeDtypeStruct((n_seq, N_STREAM, HEAD_DIM_M, HEAD_DIM_M), F32),
                   jax.ShapeDtypeStruct((n_seq, N_STREAM, HEAD_DIM_M), F32),
                   jax.ShapeDtypeStruct((n_seq, N_STREAM, HEAD_DIM_M), F32)],
        scratch_shapes=[pltpu.VMEM((seq, D_M), F32), pltpu.VMEM((seq, D_M), F32),
                        pltpu.VMEM((N_STREAM, HEAD_DIM_M, HEAD_DIM_M), F32),
                        pltpu.VMEM((N_STREAM, HEAD_DIM_M), F32),
                        pltpu.VMEM((N_STREAM, HEAD_DIM_M), F32)],
        compiler_params=_cparams(("arbitrary",)),
        name="mlstm",
    )(zb, zb, zb, zf, zf, c0, n0, m0, norm_g)


N_CTX_TILES = T_CTX // TM_TOK


def _merge_kernel(x_ref, mod_ref, ya_c, ya_l, yc_c, yc_l, ym_c, ym_l, ga_ref, gc_ref, gm_ref,
                  wa_ref, wc_ref, wm_ref, wo_ref, o_ref):
    is_ctx = pl.program_id(0) < N_CTX_TILES

    def branch(y_ctx, y_lat, g_ref, w_ref):
        y = jnp.where(is_ctx, y_ctx[...], y_lat[...]).astype(BF16)
        return g_ref[...].astype(F32) * jnp.dot(y, w_ref[...], preferred_element_type=F32)

    merged = (branch(ya_c, ya_l, ga_ref, wa_ref) + branch(yc_c, yc_l, gc_ref, wc_ref)
              + branch(ym_c, ym_l, gm_ref, wm_ref))
    mix = jnp.dot(merged.astype(BF16), wo_ref[...], preferred_element_type=F32)
    o_ref[...] = x_ref[...] + mod_ref[2:3, :] * mix


def _merge(x, mod, ya, yc, ym, zg, w_pa, w_pc, w_pm, w_out):
    def rows(width, cb=0):
        return pl.BlockSpec((TM_TOK, width), lambda i: (i, cb))

    def ctx_rows(width):
        return pl.BlockSpec((TM_TOK, width), lambda i: (jnp.minimum(i, N_CTX_TILES - 1), 0))

    def lat_rows(width):
        return pl.BlockSpec((TM_TOK, width), lambda i: (jnp.maximum(i - N_CTX_TILES, 0), 0))

    def full(shape):
        return pl.BlockSpec(shape, lambda i: (0, 0))

    return pl.pallas_call(
        _merge_kernel,
        grid=(T_ALL // TM_TOK,),
        in_specs=[rows(D_MODEL),
                  pl.BlockSpec((None, 6, D_MODEL), lambda i: (_seg_of_tile(i, TM_TOK), 0, 0)),
                  ctx_rows(D_A), lat_rows(D_A), ctx_rows(D_CONV), lat_rows(D_CONV),
                  ctx_rows(D_M), lat_rows(D_M),
                  rows(D_MODEL, ZG_GA), rows(D_MODEL, ZG_GC), rows(D_MODEL, ZG_GM),
                  full((D_A, D_MODEL)), full((D_CONV, D_MODEL)), full((D_M, D_MODEL)),
                  full((D_MODEL, D_MODEL))],
        out_specs=rows(D_MODEL),
        out_shape=jax.ShapeDtypeStruct((T_ALL, D_MODEL), F32),
        compiler_params=_cparams(("arbitrary",)),
        name="merge",
    )(x, mod, ya[0], ya[1], yc[0], yc[1], ym[0], ym[1], zg, zg, zg, w_pa, w_pc, w_pm, w_out)


def _route_sort_kernel(x_ref, g_ref, mod_ref, wr_ref, br_ref, xt_ref, pos_ref, gate_ref, nch_ref, seg_ref):
    h = _normmod(x_ref[...], g_ref[...], mod_ref[...], 3, 4)
    hb = h.astype(BF16)
    logits = _dot_nt(wr_ref[...].astype(BF16), hb) + br_ref[...]
    e_iota = lax.broadcasted_iota(jnp.int32, (N_EXPERTS, TM_MOE), 0).astype(F32)
    sels, vals = [], []
    l = logits
    for k in range(TOP_K):
        m = jnp.max(l, axis=0, keepdims=True)
        idx = jnp.min(jnp.where(l == m, e_iota, float(N_EXPERTS)), axis=0, keepdims=True)
        sel = e_iota == idx
        vals.append(m)
        sels.append(sel)
        l = jnp.where(sel, -jnp.inf, l)
    exps = [jnp.exp(v - vals[0]) for v in vals]
    tot = exps[0] + exps[1] + exps[2] + exps[3]
    onehot = jnp.zeros((N_EXPERTS, TM_MOE), F32)
    for k in range(TOP_K):
        gate_ref[k:k + 1, :] = exps[k] / tot
        onehot = onehot + sels[k].astype(F32)
    gate_ref[TOP_K:8, :] = jnp.zeros((8 - TOP_K, TM_MOE), F32)

    cnt = jnp.sum(onehot, axis=1, keepdims=True)
    nch = jnp.floor((cnt + (CHUNK_ROWS - 1)) / CHUNK_ROWS)
    ei = lax.broadcasted_iota(jnp.int32, (N_EXPERTS, N_EXPERTS), 0)
    ej = lax.broadcasted_iota(jnp.int32, (N_EXPERTS, N_EXPERTS), 1)
    seg = jnp.dot((ej < ei).astype(F32), jnp.broadcast_to(nch, (N_EXPERTS, 128)), precision=HIGHEST,
                  preferred_element_type=F32)
    nch_ref[...] = jnp.broadcast_to(nch, (N_EXPERTS, 128)).astype(jnp.int32)
    seg_ref[...] = seg.astype(jnp.int32)

    t_src = lax.broadcasted_iota(jnp.int32, (TM_MOE, TM_MOE), 0)
    t_dst = lax.broadcasted_iota(jnp.int32, (TM_MOE, TM_MOE), 1)
    before = (t_src < t_dst).astype(BF16)
    row_of = (seg[:, 0:1] * CHUNK_ROWS
              + jnp.dot(onehot.astype(BF16), before, preferred_element_type=F32))
    q_iota = lax.broadcasted_iota(jnp.int32, (Q_TILE, TM_MOE), 0)
    perm = jnp.zeros((Q_TILE, TM_MOE), F32)
    for k in range(TOP_K):
        q_k = jnp.sum(jnp.where(sels[k], row_of, 0.0), axis=0, keepdims=True).astype(jnp.int32)
        pos_ref[k:k + 1, :] = q_k
        perm = jnp.where(q_iota == q_k, 1.0, perm)
    pos_ref[TOP_K:8, :] = jnp.zeros((8 - TOP_K, TM_MOE), jnp.int32)
    xt_ref[...] = jnp.dot(perm.astype(BF16), hb, preferred_element_type=F32).astype(BF16)


def _route_sort(x, norm_g, mod, w_rt, b_r):
    tspec = pl.BlockSpec((8, TM_MOE), lambda i: (0, i))
    mspec = pl.BlockSpec((None, N_EXPERTS, 128), lambda i: (i, 0, 0))
    meta = jax.ShapeDtypeStruct((N_TILES, N_EXPERTS, 128), jnp.int32)
    return pl.pallas_call(
        _route_sort_kernel,
        grid=(N_TILES,),
        in_specs=[pl.BlockSpec((TM_MOE, D_MODEL), lambda i: (i, 0)),
                  pl.BlockSpec((1, D_MODEL), lambda i: (0, 0)),
                  pl.BlockSpec((None, 6, D_MODEL), lambda i: (_seg_of_tile(i, TM_MOE), 0, 0)),
                  pl.BlockSpec((N_EXPERTS, D_MODEL), lambda i: (0, 0)),
                  pl.BlockSpec((N_EXPERTS, 1), lambda i: (0, 0))],
        out_specs=[pl.BlockSpec((Q_TILE, D_MODEL), lambda i: (i, 0)), tspec, tspec, mspec, mspec],
        out_shape=[jax.ShapeDtypeStruct((N_TILES * Q_TILE, D_MODEL), BF16),
                   jax.ShapeDtypeStruct((8, T_ALL), jnp.int32), jax.ShapeDtypeStruct((8, T_ALL), F32),
                   meta, meta],
        compiler_params=_cparams(("arbitrary",)),
        name="moe_route_sort",
    )(x, norm_g, mod, w_rt, b_r)


def _expert_kernel(nch_ref, seg_ref, wgu_ref, bgu_ref, wd_ref, bd_ref, xt_ref, yt_ref,
                   wgu_scr, wd_scr, xbuf, ybuf, row_scr, gstart_scr, gsem, ssem):
    del xt_ref
    e = pl.program_id(0)

    def src_row(row):
        return pl.multiple_of(jnp.where(row >= 0, row, READ_SPARE), CHUNK_ROWS)

    def dst_row(row, slot, c):
        spare = c * Q_TILE + jnp.where(slot == 0, Q_TILE - CHUNK_ROWS, Q_TILE - 2 * CHUNK_ROWS)
        return pl.multiple_of(jnp.where(row >= 0, row, spare), CHUNK_ROWS)

    def chunk_rows(c):
        return slice(c * CHUNK_ROWS, (c + 1) * CHUNK_ROWS)

    def start_in(g):
        slot = g % 2
        for c in range(CPG):
            row = row_scr[g * CPG + c]
            pltpu.make_async_copy(yt_ref.at[pl.ds(src_row(row), CHUNK_ROWS), :], xbuf.at[slot, chunk_rows(c), :],
                                  gsem.at[slot]).start()

    def start_out(g, slot):
        for c in range(CPG):
            row = row_scr[g * CPG + c]
            pltpu.make_async_copy(ybuf.at[slot, chunk_rows(c), :],
                                  yt_ref.at[pl.ds(dst_row(row, slot, c), CHUNK_ROWS), :], ssem.at[slot]).start()

    def wait_in(slot):
        pltpu.make_async_copy(yt_ref.at[pl.ds(0, E_GROUP), :], xbuf.at[slot], gsem.at[slot]).wait()

    def wait_out(slot):
        pltpu.make_async_copy(ybuf.at[slot], yt_ref.at[pl.ds(0, E_GROUP), :], ssem.at[slot]).wait()

    @pl.when(e == 0)
    def _():
        def per_expert(ee, cnt):
            gstart_scr[ee] = cnt // CPG

            def per_tile(t, cnt):
                first = (t * CH_PER_TILE + seg_ref[t * N_EXPERTS + ee]) * CHUNK_ROWS

                def per_chunk(j, cnt):
                    row_scr[cnt] = first + j * CHUNK_ROWS
                    return cnt + 1

                return lax.fori_loop(0, nch_ref[t * N_EXPERTS + ee], per_chunk, cnt)

            cnt = lax.fori_loop(0, N_TILES, per_tile, cnt)
            padded = (cnt + CPG - 1) // CPG * CPG

            def pad(i, carry):
                row_scr[i] = -1
                return carry

            lax.fori_loop(cnt, padded, pad, 0)
            return padded

        total = lax.fori_loop(0, N_EXPERTS, per_expert, 0)
        gstart_scr[N_EXPERTS] = total // CPG

        def pad(i, carry):
            row_scr[i] = -1
            return carry

        lax.fori_loop(total, total + CPG, pad, 0)
        ybuf[...] = jnp.zeros_like(ybuf)
        start_out(total // CPG, 0)
        start_out(total // CPG, 1)
        start_in(0)

    g_first = gstart_scr[e]
    g_end = gstart_scr[e + 1]

    @pl.when(g_end > g_first)
    def _():
        wgu_scr[...] = wgu_ref[...].astype(BF16)
        wd_scr[...] = wd_ref[...].astype(BF16)

    def group_step(g, carry):
        slot = g % 2
        start_in(g + 1)
        wait_in(slot)
        wait_out(slot)
        hgu = jnp.dot(xbuf[slot], wgu_scr[...], preferred_element_type=F32) + bgu_ref[...]
        h_glu = jnp.minimum(hgu[:, :D_EXPERT], SWIGLU_LIMIT)
        h_lin = jnp.clip(hgu[:, D_EXPERT:], -SWIGLU_LIMIT, SWIGLU_LIMIT)
        act = (h_lin + 1.0) * (h_glu * _sigmoid(SWIGLU_ALPHA * h_glu))
        y = jnp.dot(act.astype(BF16), wd_scr[...], preferred_element_type=F32) + bd_ref[...]
        ybuf[slot] = y.astype(BF16)
        start_out(g, slot)
        return carry

    lax.fori_loop(g_first, g_end, group_step, 0)

    @pl.when(e == N_EXPERTS - 1)
    def _():
        wait_in(gstart_scr[N_EXPERTS] % 2)
        wait_out(0)
        wait_out(1)


def _experts(nch_flat, seg_flat, xt, w_gu, b_gu, w_down, b_down, layer):
    return pl.pallas_call(
        _expert_kernel,
        grid_spec=pltpu.PrefetchScalarGridSpec(
            num_scalar_prefetch=2,
            grid=(N_EXPERTS,),
            in_specs=[pl.BlockSpec((None, None, D_MODEL, 2 * D_EXPERT), lambda e, n, s: (layer, e, 0, 0)),
                      pl.BlockSpec((None, None, 1, 2 * D_EXPERT), lambda e, n, s: (layer, e, 0, 0)),
                      pl.BlockSpec((None, None, D_EXPERT, D_MODEL), lambda e, n, s: (layer, e, 0, 0)),
                      pl.BlockSpec((None, None, 1, D_MODEL), lambda e, n, s: (layer, e, 0, 0)),
                      pl.BlockSpec(memory_space=pl.ANY)],
            out_specs=pl.BlockSpec(memory_space=pl.ANY),
            scratch_shapes=[pltpu.VMEM((D_MODEL, 2 * D_EXPERT), BF16),
                            pltpu.VMEM((D_EXPERT, D_MODEL), BF16),
                            pltpu.VMEM((2, E_GROUP, D_MODEL), BF16),
                            pltpu.VMEM((2, E_GROUP, D_MODEL), BF16),
                            pltpu.SMEM((MAX_CHUNKS,), jnp.int32),
                            pltpu.SMEM((N_EXPERTS + 1,), jnp.int32),
                            pltpu.SemaphoreType.DMA((2,)),
                            pltpu.SemaphoreType.DMA((2,))]),
        out_shape=jax.ShapeDtypeStruct((N_TILES * Q_TILE, D_MODEL), BF16),
        input_output_aliases={6: 0},
        compiler_params=_cparams(("arbitrary",)),
        name="moe_experts",
    )(nch_flat, seg_flat, w_gu, b_gu, w_down, b_down, xt)


MOE_CTX_TILES = T_CTX // TM_MOE


def _combine_rows(x_ref, mod_ref, pos_ref, gate_ref, yt_ref):
    lane = lax.broadcasted_iota(jnp.int32, (TM_MOE, Q_TILE), 1)
    sel = jnp.zeros((TM_MOE, Q_TILE), F32)
    for k in range(TOP_K):
        sel = jnp.where(lane == pos_ref[:, k:k + 1], gate_ref[:, k:k + 1], sel)
    acc = jnp.dot(sel.astype(BF16), yt_ref[...], preferred_element_type=F32)
    return x_ref[...] + mod_ref[5:6, :] * acc


def _combine_kernel(x_ref, mod_ref, pos_ref, gate_ref, yt_ref, o_ref):
    o_ref[...] = _combine_rows(x_ref, mod_ref, pos_ref, gate_ref, yt_ref)


def _combine_final_kernel(x_ref, mod_ref, pos_ref, gate_ref, yt_ref, fg_ref, ctx_ref, lat_ref):
    y = _combine_rows(x_ref, mod_ref, pos_ref, gate_ref, yt_ref)
    y = y * lax.rsqrt(jnp.mean(y * y, axis=-1, keepdims=True) + RMS_EPS) * fg_ref[...]
    is_ctx = pl.program_id(0) < MOE_CTX_TILES

    @pl.when(is_ctx)
    def _():
        ctx_ref[...] = y

    @pl.when(jnp.logical_not(is_ctx))
    def _():
        lat_ref[...] = y


def _combine(x, mod, pos_t, gate_t, yt, final_g=None):
    tile = pl.BlockSpec((TM_MOE, D_MODEL), lambda i: (i, 0))
    in_specs = [tile,
                pl.BlockSpec((None, 6, D_MODEL), lambda i: (_seg_of_tile(i, TM_MOE), 0, 0)),
                pl.BlockSpec((TM_MOE, 8), lambda i: (i, 0)),
                pl.BlockSpec((TM_MOE, 8), lambda i: (i, 0)),
                pl.BlockSpec((Q_TILE, D_MODEL), lambda i: (i, 0))]
    if final_g is None:
        return pl.pallas_call(
            _combine_kernel, grid=(N_TILES,), in_specs=in_specs, out_specs=tile,
            out_shape=jax.ShapeDtypeStruct((T_ALL, D_MODEL), F32),
            compiler_params=_cparams(("arbitrary",)), name="moe_combine",
        )(x, mod, pos_t, gate_t, yt)
    return pl.pallas_call(
        _combine_final_kernel, grid=(N_TILES,),
        in_specs=in_specs + [pl.BlockSpec((1, D_MODEL), lambda i: (0, 0))],
        out_specs=[pl.BlockSpec((TM_MOE, D_MODEL), lambda i: (jnp.minimum(i, MOE_CTX_TILES - 1), 0)),
                   pl.BlockSpec((TM_MOE, D_MODEL), lambda i: (jnp.maximum(i - MOE_CTX_TILES, 0), 0))],
        out_shape=[jax.ShapeDtypeStruct((T_CTX, D_MODEL), F32), jax.ShapeDtypeStruct((T_LAT, D_MODEL), F32)],
        compiler_params=_cparams(("arbitrary",)), name="moe_combine_final",
    )(x, mod, pos_t, gate_t, yt, final_g)


def _moe(x, norm_g, mod, w_rt, b_r, w_gu, b_gu, w_down, b_down, layer, final_g=None):
    xt, pos, gate, nch, seg = _route_sort(x, norm_g, mod, w_rt, b_r)
    yt = _experts(nch[:, :, 0].reshape(-1), seg[:, :, 0].reshape(-1), xt, w_gu,
                  b_gu.reshape(DEPTH, N_EXPERTS, 1, 2 * D_EXPERT), w_down,
                  b_down.reshape(DEPTH, N_EXPERTS, 1, D_MODEL), layer)
    return _combine(x, mod, pos.T, gate.T, yt, final_g)


def _split_in_cols(w):
    conv0, mq0, om0 = 3 * D_A, 3 * D_A + 2 * D_CONV, 3 * D_A + 2 * D_CONV + 3 * D_M
    gates_end = GATE_OFF + N_GATE_M
    pad = jnp.zeros(w.shape[:-1] + (N_ZF - (2 * D_CONV + D_M + N_GATE_M),), w.dtype)
    zb = jnp.concatenate([w[..., :conv0], w[..., mq0:om0]], axis=-1)
    zg = w[..., gates_end:]
    zf = jnp.concatenate([w[..., conv0:mq0], w[..., om0:GATE_OFF], w[..., GATE_OFF:gates_end], pad], axis=-1)
    return zb, zg, zf


def kernel(x_prompt, x_sample, cache_k, cache_v, state_C, state_n, state_m, c, c_ctx, norm1_g, w_mod, b_mod, w_in, b_in, rpb, w_dw, b_dw, cln_g, cln_b, mnorm_g, w_pa, w_pc, w_pm, w_out, norm2_g, w_router, b_router, w_gu, b_gu, w_down, b_down, final_g):
    cond = jnp.concatenate([c_ctx[None, :], c, jnp.zeros((SEG_PAD - N_SEG, D_MODEL), F32)], axis=0)
    mod_all = _modulation(cond, w_mod, b_mod).reshape(DEPTH, SEG_PAD, 6, D_MODEL)

    x = jnp.concatenate([x_prompt.reshape(T_CTX, D_MODEL), x_sample.reshape(T_LAT, D_MODEL)], axis=0)
    ck = cache_k.reshape(DEC_BATCH, DEPTH, PAST_LEN, D_A).astype(BF16)
    cv = cache_v.reshape(DEC_BATCH, DEPTH, PAST_LEN, D_A).astype(BF16)
    lat_c0 = state_C.reshape(DEC_BATCH, DEPTH, N_STREAM, HEAD_DIM_M, HEAD_DIM_M)
    lat_n0 = state_n.reshape(DEC_BATCH, DEPTH, N_STREAM, HEAD_DIM_M)
    lat_m0 = jnp.broadcast_to(state_m.reshape(DEC_BATCH, DEPTH, N_STREAM, 1),
                              (DEC_BATCH, DEPTH, N_STREAM, HEAD_DIM_M))
    ctx_c0 = jnp.zeros((1, N_STREAM, HEAD_DIM_M, HEAD_DIM_M), F32)
    ctx_n0 = jnp.zeros((1, N_STREAM, HEAD_DIM_M), F32)
    ctx_m0 = jnp.full((1, N_STREAM, HEAD_DIM_M), -jnp.inf, F32)

    ks, vs, cs, ns, ms = [], [], [], [], []
    for l in range(DEPTH):
        mod = mod_all[l]
        g1 = norm1_g[l][None, :]
        zb, zg, zf, kv = _in_proj(x, g1, mod, _split_in_cols(w_in[l].astype(BF16)),
                                  _split_in_cols(b_in[l][None, :]))
        ya = (_ctx_attention(zb), _natt(zb, ck, cv, _natt_bias(rpb[l]), l))
        conv_w = (w_dw[l], b_dw[l][None, :], cln_g[l][None, :], cln_b[l][None, :])
        yc = (_conv(zf, *conv_w, SEQ, BATCH, 0), _conv(zf, *conv_w, DEC_SEQ, DEC_BATCH, T_CTX // DEC_SEQ))
        ng = mnorm_g[l][None, :]
        ym_ctx, c_l, n_l, m_l = _mlstm(zb, zf, ctx_c0, ctx_n0, ctx_m0, ng, SEQ, BATCH, 0, lambda b: (0,))
        ym_lat, _, _, _ = _mlstm(zb, zf, lat_c0, lat_n0, lat_m0, ng, DEC_SEQ, DEC_BATCH, T_CTX // DEC_SEQ,
                                 lambda b: (b, l))
        x = _merge(x, mod, ya, yc, (ym_ctx, ym_lat), zg, w_pa[l].astype(BF16), w_pc[l].astype(BF16),
                   w_pm[l].astype(BF16), w_out[l].astype(BF16))
        x = _moe(x, norm2_g[l][None, :], mod, w_router[l].T, b_router[l][:, None],
                 w_gu, b_gu, w_down, b_down, l, final_g[None, :] if l == DEPTH - 1 else None)
        ks.append(kv[:, :D_A].reshape(BATCH, SEQ, N_HEADS_A, HEAD_DIM_A))
        vs.append(kv[:, D_A:].reshape(BATCH, SEQ, N_HEADS_A, HEAD_DIM_A))
        cs.append(c_l.reshape(BATCH, 2, N_HEADS_M, HEAD_DIM_M, HEAD_DIM_M))
        ns.append(n_l.reshape(BATCH, 2, N_HEADS_M, HEAD_DIM_M))
        ms.append(m_l[:, :, 0].reshape(BATCH, 2, N_HEADS_M))

    y_ctx, y_lat = x
    return (y_ctx.reshape(BATCH, SEQ, D_MODEL), y_lat.reshape(DEC_BATCH, DEC_SEQ, D_MODEL),
            jnp.stack(ks, axis=1), jnp.stack(vs, axis=1), jnp.stack(cs, axis=1),
            jnp.stack(ns, axis=1), jnp.stack(ms, axis=1))
```

```python
import functools

import numpy as np
import jax
import jax.numpy as jnp
from jax import lax
from jax.experimental import pallas as pl
from jax.experimental.pallas import tpu as pltpu

F32 = jnp.float32
BF16 = jnp.bfloat16
HIGHEST = lax.Precision.HIGHEST

D_MODEL = 1024
BATCH = 16
SEQ = 256
DEPTH = 2
DEC_BATCH = 8
DEC_SEQ = 1024
PAST_LEN = 512
GRID_W = 64
N_HEADS_A = 8
HEAD_DIM_A = 64
D_A = N_HEADS_A * HEAD_DIM_A
WIN_ROWS = 8
WIN_COLS = 16
D_CONV = 512
CONV_WIDTH = 31
N_HEADS_M = 4
HEAD_DIM_M = 128
D_M = N_HEADS_M * HEAD_DIM_M
N_GATE_M = 4 * N_HEADS_M
CHUNK = 64
N_EXPERTS = 32
TOP_K = 4
D_EXPERT = 1024
SWIGLU_ALPHA = 1.702
SWIGLU_LIMIT = 7.0
RMS_EPS = 1e-6
LN_EPS = 1e-5
GATE_OFF = 3 * D_A + 2 * D_CONV + 4 * D_M
N_IN = GATE_OFF + N_GATE_M + 3 * D_MODEL

T_CTX = BATCH * SEQ
T_LAT = DEC_BATCH * DEC_SEQ
T_ALL = T_CTX + T_LAT
N_SEG = 1 + DEC_BATCH
SEG_PAD = 16
GRID_ROWS = DEC_SEQ // GRID_W
NEG = -1e30

N_ZB = 6 * 512
N_ZG = 3 * D_MODEL
N_ZF = 1664
QB_QA, QB_KA, QB_VA, QB_QM, QB_KM, QB_VM = 0, 1, 2, 3, 4, 5
ZG_GA, ZG_GC, ZG_GM = 0, 1, 2
ZF_CU, ZF_CG, ZF_OM = 0, 1, 2
ZF_GATES = 12

TM_TOK = 512
TM_PROJ = 256
TM_MOE = 512
N_TILES = T_ALL // TM_MOE
CHUNK_ROWS = 16
MXU_ROWS = 256
Q_TILE = -(-(TM_MOE * TOP_K + N_EXPERTS * (CHUNK_ROWS - 1)) // MXU_ROWS) * MXU_ROWS
CH_PER_TILE = Q_TILE // CHUNK_ROWS
E_GROUP = 256
CPG = E_GROUP // CHUNK_ROWS
MAX_CHUNKS = ((T_ALL * TOP_K + N_TILES * N_EXPERTS * (CHUNK_ROWS - 1)) // CHUNK_ROWS
              + N_EXPERTS * (CPG - 1)) + CPG
assert Q_TILE - (TM_MOE * TOP_K + N_EXPERTS * (CHUNK_ROWS - 1)) >= 2 * CHUNK_ROWS and N_TILES > CPG
READ_SPARE = N_TILES * Q_TILE - CHUNK_ROWS
VMEM_LIMIT = 60 * 1024 * 1024


def _cparams(sem=None):
    return pltpu.CompilerParams(dimension_semantics=sem, vmem_limit_bytes=VMEM_LIMIT)


def _seg_of_tile(i, tile):
    n_ctx = T_CTX // tile
    per_lat = DEC_SEQ // tile
    return jnp.where(i < n_ctx, 0, 1 + (i - n_ctx) // per_lat)


def _dot_nt(a, b):
    return lax.dot_general(a, b, (((1,), (1,)), ((), ())), preferred_element_type=F32)


def _sigmoid(x):
    return 1.0 / (1.0 + jnp.exp(-x))


def _mod_kernel(c_ref, w_ref, b_ref, o_ref):
    c = c_ref[...]
    s = c * _sigmoid(c)
    o_ref[...] = jnp.dot(s, w_ref[...], precision=HIGHEST, preferred_element_type=F32) + b_ref[...]


def _modulation(cond, w_mod, b_mod):
    tn = 1536
    return pl.pallas_call(
        _mod_kernel,
        grid=(DEPTH, 6 * D_MODEL // tn),
        in_specs=[pl.BlockSpec((SEG_PAD, D_MODEL), lambda l, j: (0, 0)),
                  pl.BlockSpec((None, D_MODEL, tn), lambda l, j: (l, 0, j)),
                  pl.BlockSpec((None, 1, tn), lambda l, j: (l, 0, j))],
        out_specs=pl.BlockSpec((None, SEG_PAD, tn), lambda l, j: (l, 0, j)),
        out_shape=jax.ShapeDtypeStruct((DEPTH, SEG_PAD, 6 * D_MODEL), F32),
        compiler_params=_cparams(("arbitrary", "arbitrary")),
        name="modulation",
    )(cond, w_mod, b_mod.reshape(DEPTH, 1, 6 * D_MODEL))


def _normmod(x, g, mod, shift_idx, scale_idx):
    y = x * lax.rsqrt(jnp.mean(x * x, axis=-1, keepdims=True) + RMS_EPS) * g
    return y * (1.0 + mod[scale_idx:scale_idx + 1, :]) + mod[shift_idx:shift_idx + 1, :]


PROJ_CTX_TILES = T_CTX // TM_PROJ


def _in_proj_kernel(x_ref, g_ref, mod_ref, wb_ref, bb_ref, wg_ref, bg_ref, wf_ref, bf_ref,
                    zb_ref, zg_ref, zf_ref, kv_ref):
    h = _normmod(x_ref[...], g_ref[...], mod_ref[...], 0, 1).astype(BF16)
    acc = jnp.dot(h, wb_ref[...], preferred_element_type=F32) + bb_ref[...]
    zb_ref[...] = acc.astype(BF16)

    @pl.when(pl.program_id(0) < PROJ_CTX_TILES)
    def _():
        kv_ref[...] = acc[:, D_A:3 * D_A]

    gates = jnp.dot(h, wg_ref[...], preferred_element_type=F32) + bg_ref[...]
    zg_ref[...] = _sigmoid(gates).astype(BF16)
    zf_ref[...] = jnp.dot(h, wf_ref[...], preferred_element_type=F32) + bf_ref[...]


def _in_proj(x, norm_g, mod, w, b):
    def full(a):
        return pl.BlockSpec(a.shape, lambda i: (0, 0))

    def rows(n):
        return pl.BlockSpec((TM_PROJ, n), lambda i: (i, 0))

    return pl.pallas_call(
        _in_proj_kernel,
        grid=(T_ALL // TM_PROJ,),
        in_specs=[rows(D_MODEL),
                  pl.BlockSpec((1, D_MODEL), lambda i: (0, 0)),
                  pl.BlockSpec((None, 6, D_MODEL), lambda i: (_seg_of_tile(i, TM_PROJ), 0, 0)),
                  full(w[0]), full(b[0]), full(w[1]), full(b[1]), full(w[2]), full(b[2])],
        out_specs=[rows(N_ZB), rows(N_ZG), rows(N_ZF),
                   pl.BlockSpec((TM_PROJ, 2 * D_A), lambda i: (jnp.minimum(i, PROJ_CTX_TILES - 1), 0))],
        out_shape=[jax.ShapeDtypeStruct((T_ALL, N_ZB), BF16), jax.ShapeDtypeStruct((T_ALL, N_ZG), BF16),
                   jax.ShapeDtypeStruct((T_ALL, N_ZF), F32), jax.ShapeDtypeStruct((T_CTX, 2 * D_A), F32)],
        compiler_params=_cparams(("arbitrary",)),
        name="in_proj",
    )(x, norm_g, mod, w[0], b[0], w[1], b[1], w[2], b[2])


HEAD_PAIR = 2 * HEAD_DIM_A
ATT_SCALE = HEAD_DIM_A ** -0.5


def _pair_queries(q2):
    lo = lax.broadcasted_iota(jnp.int32, (1, HEAD_PAIR), 1) < HEAD_DIM_A
    q2 = q2 * ATT_SCALE
    zero = jnp.zeros_like(q2)
    return lo, jnp.concatenate([jnp.where(lo, q2, zero), jnp.where(lo, zero, q2)], axis=0)


def _unpair(lo, o_stacked):
    rows = o_stacked.shape[0] // 2
    return jnp.where(lo, o_stacked[:rows], o_stacked[rows:])


def _ctx_attn_kernel(q_ref, k_ref, v_ref, o_ref):
    for hp in range(N_HEADS_A // 2):
        sl = slice(hp * HEAD_PAIR, (hp + 1) * HEAD_PAIR)
        lo, qs = _pair_queries(q_ref[:, sl])
        s = _dot_nt(qs, k_ref[:, sl])
        p = jnp.exp(s - jnp.max(s, axis=-1, keepdims=True))
        l = jnp.sum(p, axis=-1, keepdims=True)
        o = jnp.dot(p.astype(BF16), v_ref[:, sl], preferred_element_type=F32) / l
        o_ref[:, sl] = _unpair(lo, o)


def _ctx_attention(zb):
    def spec(cb):
        return pl.BlockSpec((SEQ, D_A), lambda b: (b, cb))

    return pl.pallas_call(
        _ctx_attn_kernel,
        grid=(BATCH,),
        in_specs=[spec(QB_QA), spec(QB_KA), spec(QB_VA)],
        out_specs=pl.BlockSpec((SEQ, D_A), lambda b: (b, 0)),
        out_shape=jax.ShapeDtypeStruct((T_CTX, D_A), F32),
        compiler_params=_cparams(("arbitrary",)),
        name="ctx_attention",
    )(zb, zb, zb)


NQ_ROWS = 4
NW_ROWS = 12
NQ_BLOCKS = GRID_ROWS // NQ_ROWS
NQ_TOK = NQ_ROWS * GRID_W
NW_TOK = NW_ROWS * GRID_W


def _window_row(qb, xp):
    return xp.clip(qb * NQ_ROWS - WIN_ROWS // 2, 0, GRID_ROWS - NW_ROWS)


N_REL_ROWS = 2 * WIN_ROWS - 1


def _natt_rel_rows():
    r = np.arange(NQ_BLOCKS)[:, None, None] * NQ_ROWS + np.arange(NQ_ROWS)[None, :, None]
    krow = _window_row(np.arange(NQ_BLOCKS), np)[:, None, None] + np.arange(NW_ROWS)[None, None, :]
    rs = np.clip(r - WIN_ROWS // 2, 0, GRID_ROWS - WIN_ROWS)
    assert ((rs >= krow[:, :, :1]) & (rs + WIN_ROWS <= krow[:, :, -1:] + 1)).all()
    return np.where((krow >= rs) & (krow < rs + WIN_ROWS), krow - r + WIN_ROWS - 1, N_REL_ROWS)


NATT_REL = _natt_rel_rows()


def _natt_kernel(q_ref, k_ref, v_ref, kc_ref, vc_ref, toe_ref, o_ref, bias_scr):
    qb = pl.program_id(0)

    @pl.when(pl.program_id(1) == 0)
    def _():
        for v in range(NQ_BLOCKS):
            @pl.when(qb == v)
            def _(v=v):
                for h in range(N_HEADS_A):
                    for rq in range(NQ_ROWS):
                        for kr in range(NW_ROWS):
                            half = (kr % 2) * GRID_W
                            bias_scr[h, rq * GRID_W:(rq + 1) * GRID_W, kr * GRID_W:(kr + 1) * GRID_W] = (
                                toe_ref[h, int(NATT_REL[v, rq, kr]), :, half:half + GRID_W])

    start = pl.multiple_of(_window_row(qb, jnp) * GRID_W, NQ_TOK)
    band = NW_TOK
    for hp in range(N_HEADS_A // 2):
        sl = slice(hp * HEAD_PAIR, (hp + 1) * HEAD_PAIR)
        lo, qs = _pair_queries(q_ref[:, sl])
        bias = bias_scr[2 * hp:2 * hp + 2].reshape(2 * NQ_TOK, band)
        s_loc = _dot_nt(qs, k_ref[pl.ds(start, band), sl]) + bias
        s_ctx = _dot_nt(qs, kc_ref[:, sl])
        m = jnp.maximum(jnp.max(s_loc, axis=-1, keepdims=True), jnp.max(s_ctx, axis=-1, keepdims=True))
        p_loc = jnp.exp(s_loc - m)
        p_ctx = jnp.exp(s_ctx - m)
        l = jnp.sum(p_loc, axis=-1, keepdims=True) + jnp.sum(p_ctx, axis=-1, keepdims=True)
        o = (jnp.dot(p_loc.astype(BF16), v_ref[pl.ds(start, band), sl], preferred_element_type=F32)
             + jnp.dot(p_ctx.astype(BF16), vc_ref[:, sl], preferred_element_type=F32))
        o_ref[:, sl] = _unpair(lo, o / l)


def _natt_bias(rpb_l):
    qc = np.arange(GRID_W)
    kc = np.arange(GRID_W)
    cs = np.clip(qc - WIN_COLS // 2, 0, GRID_W - WIN_COLS)
    ok = (kc[None, :] >= cs[:, None]) & (kc[None, :] < cs[:, None] + WIN_COLS)
    dc = np.clip(kc[None, :] - qc[:, None] + WIN_COLS - 1, 0, 2 * WIN_COLS - 2)
    pick = (dc[None] == np.arange(2 * WIN_COLS - 1)[:, None, None]).astype(np.float32)
    toe = jnp.einsum('hdc,cqk->hdqk', rpb_l, jnp.asarray(pick), precision=HIGHEST)
    toe = jnp.where(jnp.asarray(ok)[None, None], toe, NEG)
    toe = jnp.concatenate([toe, jnp.full((N_HEADS_A, 1, GRID_W, GRID_W), NEG, F32)], axis=1)
    return jnp.concatenate([toe, toe], axis=-1)


def _natt(zb, cache_k, cache_v, toe, layer):
    lat0 = T_CTX // DEC_SEQ
    row0 = T_CTX // NQ_TOK
    return pl.pallas_call(
        _natt_kernel,
        grid=(NQ_BLOCKS, DEC_BATCH),
        in_specs=[pl.BlockSpec((NQ_TOK, D_A), lambda qb, b: (row0 + b * NQ_BLOCKS + qb, QB_QA)),
                  pl.BlockSpec((DEC_SEQ, D_A), lambda qb, b: (lat0 + b, QB_KA)),
                  pl.BlockSpec((DEC_SEQ, D_A), lambda qb, b: (lat0 + b, QB_VA)),
                  pl.BlockSpec((None, None, PAST_LEN, D_A), lambda qb, b: (b, layer, 0, 0)),
                  pl.BlockSpec((None, None, PAST_LEN, D_A), lambda qb, b: (b, layer, 0, 0)),
                  pl.BlockSpec((N_HEADS_A, N_REL_ROWS + 1, GRID_W, 2 * GRID_W), lambda qb, b: (0, 0, 0, 0))],
        out_specs=pl.BlockSpec((NQ_TOK, D_A), lambda qb, b: (b * NQ_BLOCKS + qb, 0)),
        out_shape=jax.ShapeDtypeStruct((T_LAT, D_A), F32),
        scratch_shapes=[pltpu.VMEM((N_HEADS_A, NQ_TOK, NW_TOK), F32)],
        compiler_params=_cparams(("arbitrary", "arbitrary")),
        name="nbr_attention",
    )(zb, zb, zb, cache_k, cache_v, toe)


CONV_HALO = 16
CONV_ROWS = 64


SUBLANES = 8


def _conv_kernel(u_ref, g_ref, w_ref, b_ref, lg_ref, lb_ref, o_ref, pad_scr, sh_scr, *, seq):
    zeros = jnp.zeros((CONV_HALO, D_CONV), F32)
    pad_scr[0:CONV_HALO, :] = zeros
    pad_scr[CONV_HALO + seq:2 * CONV_HALO + seq, :] = zeros
    pad_scr[CONV_HALO:CONV_HALO + seq, :] = u_ref[...] * _sigmoid(g_ref[...])
    n_sh = seq + 2 * CONV_HALO - SUBLANES
    for s in range(SUBLANES):
        sh_scr[s] = pad_scr[s:s + n_sh, :]
    first = CONV_HALO - CONV_WIDTH // 2
    for c in range(seq // CONV_ROWS):
        base = c * CONV_ROWS
        acc = jnp.broadcast_to(b_ref[...], (CONV_ROWS, D_CONV))
        for j in range(CONV_WIDTH):
            q, s = divmod(first + j, SUBLANES)
            row0 = base + q * SUBLANES
            acc = acc + sh_scr[s, row0:row0 + CONV_ROWS, :] * w_ref[j:j + 1, :]
        mu = jnp.mean(acc, axis=-1, keepdims=True)
        xc = acc - mu
        var = jnp.mean(xc * xc, axis=-1, keepdims=True)
        y = xc * lax.rsqrt(var + LN_EPS) * lg_ref[...] + lb_ref[...]
        o_ref[base:base + CONV_ROWS, :] = y * _sigmoid(y)


def _conv(z, w_dw, b_dw, ln_g, ln_b, seq, n_seq, row_block0):
    def vec():
        return pl.BlockSpec((1, D_CONV), lambda b: (0, 0))

    return pl.pallas_call(
        functools.partial(_conv_kernel, seq=seq),
        grid=(n_seq,),
        in_specs=[pl.BlockSpec((seq, D_CONV), lambda b: (row_block0 + b, ZF_CU)),
                  pl.BlockSpec((seq, D_CONV), lambda b: (row_block0 + b, ZF_CG)),
                  pl.BlockSpec((CONV_WIDTH, D_CONV), lambda b: (0, 0)),
                  vec(), vec(), vec()],
        out_specs=pl.BlockSpec((seq, D_CONV), lambda b: (b, 0)),
        out_shape=jax.ShapeDtypeStruct((n_seq * seq, D_CONV), F32),
        scratch_shapes=[pltpu.VMEM((seq + 2 * CONV_HALO, D_CONV), F32),
                        pltpu.VMEM((SUBLANES, seq + 2 * CONV_HALO - SUBLANES, D_CONV), F32)],
        compiler_params=_cparams(("arbitrary",)),
        name="conformer_conv",
    )(z, z, w_dw, b_dw, ln_g, ln_b)


N_STREAM = 2 * N_HEADS_M
MCHUNK = 128


def _mlstm_kernel(q_ref, k_ref, v_ref, om_ref, gt_ref, c0_ref, n0_ref, m0_ref, ng_ref,
                  y_ref, c_out, n_out, m_out, hf_scr, hb_scr, c_scr, n_scr, m_scr, *, seq):
    nc = seq // MCHUNK
    c_scr[...] = c0_ref[...]
    n_scr[...] = n0_ref[...]
    m_scr[...] = m0_ref[...]
    ti = lax.broadcasted_iota(jnp.int32, (MCHUNK, MCHUNK), 0)
    si = lax.broadcasted_iota(jnp.int32, (MCHUNK, MCHUNK), 1)
    masks = (si <= ti, si >= ti)
    tris = tuple(mk.astype(F32) for mk in masks)
    kscale = HEAD_DIM_M ** -0.5

    def chunk_step(c, carry):
        for d in range(2):
            cidx = c if d == 0 else nc - 1 - c
            off = pl.multiple_of(cidx * MCHUNK, MCHUNK)
            last = MCHUNK - 1 if d == 0 else 0
            g = gt_ref[pl.ds(off, MCHUNK), :]
            lf = jnp.minimum(g, 0.0) - jnp.log(1.0 + jnp.exp(-jnp.abs(g)))
            cum = jnp.dot(tris[d], lf, precision=HIGHEST, preferred_element_type=F32)
            g_t = g.T
            cum_t = cum.T
            for h in range(N_HEADS_M):
                s_id = N_HEADS_M * d + h
                icol = 2 * N_HEADS_M * d + h
                fcol = icol + N_HEADS_M
                hs = slice(h * HEAD_DIM_M, (h + 1) * HEAD_DIM_M)
                b_col = cum[:, fcol:fcol + 1]
                i_col = g[:, icol:icol + 1]
                b_row = cum_t[fcol:fcol + 1, :]
                i_row = g_t[icol:icol + 1, :]
                b_last = b_col[last:last + 1, :]
                m_prev = m_scr[s_id:s_id + 1, 0:1]
                log_d = jnp.where(masks[d], b_col - b_row + i_row, NEG)
                inter = b_col + m_prev
                m_t = jnp.maximum(inter, jnp.max(log_d, axis=-1, keepdims=True))
                w_inter = jnp.exp(inter - m_t)
                qb = q_ref[pl.ds(off, MCHUNK), hs]
                kb = k_ref[pl.ds(off, MCHUNK), hs]
                vb = v_ref[pl.ds(off, MCHUNK), hs]
                c_prev = c_scr[s_id]
                n_prev = n_scr[s_id:s_id + 1, :]
                s_mat = _dot_nt(qb, kb) * (kscale * jnp.exp(log_d - m_t))
                num = (w_inter * _dot_nt(qb, c_prev.astype(BF16))
                       + jnp.dot(s_mat.astype(BF16), vb, preferred_element_type=F32))
                den = (w_inter * jnp.sum(qb.astype(F32) * n_prev, axis=-1, keepdims=True)
                       + jnp.sum(s_mat, axis=-1, keepdims=True))
                hh = num / jnp.maximum(jnp.abs(den), jnp.exp(-m_t))
                if d == 0:
                    hf_scr[pl.ds(off, MCHUNK), hs] = hh
                else:
                    hb_scr[pl.ds(off, MCHUNK), hs] = hh
                m_new = m_t[last:last + 1, :]
                w_prev = jnp.exp(b_last + m_prev - m_new)
                w_src = kscale * jnp.exp(b_last - b_col + i_col - m_new)
                upd = jnp.dot((w_src * vb.astype(F32)).T.astype(BF16), kb, preferred_element_type=F32)
                c_scr[s_id] = w_prev * c_prev + upd
                n_scr[s_id:s_id + 1, :] = (w_prev * n_prev
                                           + jnp.sum(w_src * kb.astype(F32), axis=0, keepdims=True))
                m_scr[s_id:s_id + 1, :] = jnp.broadcast_to(m_new, (1, HEAD_DIM_M))
        return carry

    lax.fori_loop(0, nc, chunk_step, 0)

    for h in range(N_HEADS_M):
        hs = slice(h * HEAD_DIM_M, (h + 1) * HEAD_DIM_M)
        hsum = hf_scr[:, hs] + hb_scr[:, hs]
        mu = jnp.mean(hsum, axis=-1, keepdims=True)
        xc = hsum - mu
        var = jnp.mean(xc * xc, axis=-1, keepdims=True)
        hn = xc * lax.rsqrt(var + LN_EPS) * ng_ref[:, hs]
        y_ref[:, hs] = _sigmoid(om_ref[:, hs]) * hn
    c_out[...] = c_scr[...]
    n_out[...] = n_scr[...]
    m_out[...] = m_scr[...]


def _mlstm(zb, zf, c0, n0, m0, norm_g, seq, n_seq, row_block0, state_map):
    lead = len(state_map(0))

    def zspec(cb):
        return pl.BlockSpec((seq, D_M), lambda b: (row_block0 + b, cb))

    def sspec(tail):
        return pl.BlockSpec((None,) * lead + tail, lambda b: state_map(b) + (0,) * len(tail))

    return pl.pallas_call(
        functools.partial(_mlstm_kernel, seq=seq),
        grid=(n_seq,),
        in_specs=[zspec(QB_QM), zspec(QB_KM), zspec(QB_VM), zspec(ZF_OM),
                  pl.BlockSpec((seq, 128), lambda b: (row_block0 + b, ZF_GATES)),
                  sspec((N_STREAM, HEAD_DIM_M, HEAD_DIM_M)),
                  sspec((N_STREAM, HEAD_DIM_M)),
                  sspec((N_STREAM, HEAD_DIM_M)),
                  pl.BlockSpec((1, D_M), lambda b: (0, 0))],
        out_specs=[pl.BlockSpec((seq, D_M), lambda b: (b, 0)),
                   pl.BlockSpec((None, N_STREAM, HEAD_DIM_M, HEAD_DIM_M), lambda b: (b, 0, 0, 0)),
                   pl.BlockSpec((None, N_STREAM, HEAD_DIM_M), lambda b: (b, 0, 0)),
                   pl.BlockSpec((None, N_STREAM, HEAD_DIM_M), lambda b: (b, 0, 0))],
        out_shape=[jax.ShapeDtypeStruct((n_seq * seq, D_M), F32),
                   jax.ShapeDtypeStruct((n_seq, N_STREAM, HEAD_DIM_M, HEAD_DIM_M), F32),
                   jax.ShapeDtypeStruct((n_seq, N_STREAM, HEAD_DIM_M), F32),
                   jax.ShapeDtypeStruct((n_seq, N_STREAM, HEAD_DIM_M), F32)],
        scratch_shapes=[pltpu.VMEM((seq, D_M), F32), pltpu.VMEM((seq, D_M), F32),
                        pltpu.VMEM((N_STREAM, HEAD_DIM_M, HEAD_DIM_M), F32),
                        pltpu.VMEM((N_STREAM, HEAD_DIM_M), F32),
                        pltpu.VMEM((N_STREAM, HEAD_DIM_M), F32)],
        compiler_params=_cparams(("arbitrary",)),
        name="mlstm",
    )(zb, zb, zb, zf, zf, c0, n0, m0, norm_g)


N_CTX_TILES = T_CTX // TM_TOK


def _merge_kernel(x_ref, mod_ref, ya_c, ya_l, yc_c, yc_l, ym_c, ym_l, ga_ref, gc_ref, gm_ref,
                  wa_ref, wc_ref, wm_ref, wo_ref, o_ref):
    is_ctx = pl.program_id(0) < N_CTX_TILES

    def branch(y_ctx, y_lat, g_ref, w_ref):
        y = jnp.where(is_ctx, y_ctx[...], y_lat[...]).astype(BF16)
        return g_ref[...].astype(F32) * jnp.dot(y, w_ref[...], preferred_element_type=F32)

    merged = (branch(ya_c, ya_l, ga_ref, wa_ref) + branch(yc_c, yc_l, gc_ref, wc_ref)
              + branch(ym_c, ym_l, gm_ref, wm_ref))
    mix = jnp.dot(merged.astype(BF16), wo_ref[...], preferred_element_type=F32)
    o_ref[...] = x_ref[...] + mod_ref[2:3, :] * mix


def _merge(x, mod, ya, yc, ym, zg, w_pa, w_pc, w_pm, w_out):
    def rows(width, cb=0):
        return pl.BlockSpec((TM_TOK, width), lambda i: (i, cb))

    def ctx_rows(width):
        return pl.BlockSpec((TM_TOK, width), lambda i: (jnp.minimum(i, N_CTX_TILES - 1), 0))

    def lat_rows(width):
        return pl.BlockSpec((TM_TOK, width), lambda i: (jnp.maximum(i - N_CTX_TILES, 0), 0))

    def full(shape):
        return pl.BlockSpec(shape, lambda i: (0, 0))

    return pl.pallas_call(
        _merge_kernel,
        grid=(T_ALL // TM_TOK,),
        in_specs=[rows(D_MODEL),
                  pl.BlockSpec((None, 6, D_MODEL), lambda i: (_seg_of_tile(i, TM_TOK), 0, 0)),
                  ctx_rows(D_A), lat_rows(D_A), ctx_rows(D_CONV), lat_rows(D_CONV),
                  ctx_rows(D_M), lat_rows(D_M),
                  rows(D_MODEL, ZG_GA), rows(D_MODEL, ZG_GC), rows(D_MODEL, ZG_GM),
                  full((D_A, D_MODEL)), full((D_CONV, D_MODEL)), full((D_M, D_MODEL)),
                  full((D_MODEL, D_MODEL))],
        out_specs=rows(D_MODEL),
        out_shape=jax.ShapeDtypeStruct((T_ALL, D_MODEL), F32),
        compiler_params=_cparams(("arbitrary",)),
        name="merge",
    )(x, mod, ya[0], ya[1], yc[0], yc[1], ym[0], ym[1], zg, zg, zg, w_pa, w_pc, w_pm, w_out)


def _route_sort_kernel(x_ref, g_ref, mod_ref, wr_ref, br_ref, xt_ref, pos_ref, gate_ref, nch_ref, seg_ref):
    h = _normmod(x_ref[...], g_ref[...], mod_ref[...], 3, 4)
    hb = h.astype(BF16)
    logits = _dot_nt(wr_ref[...].astype(BF16), hb) + br_ref[...]
    e_iota = lax.broadcasted_iota(jnp.int32, (N_EXPERTS, TM_MOE), 0).astype(F32)
    sels, vals = [], []
    l = logits
    for k in range(TOP_K):
        m = jnp.max(l, axis=0, keepdims=True)
        idx = jnp.min(jnp.where(l == m, e_iota, float(N_EXPERTS)), axis=0, keepdims=True)
        sel = e_iota == idx
        vals.append(m)
        sels.append(sel)
        l = jnp.where(sel, -jnp.inf, l)
    exps = [jnp.exp(v - vals[0]) for v in vals]
    tot = exps[0] + exps[1] + exps[2] + exps[3]
    onehot = jnp.zeros((N_EXPERTS, TM_MOE), F32)
    for k in range(TOP_K):
        gate_ref[k:k + 1, :] = exps[k] / tot
        onehot = onehot + sels[k].astype(F32)
    gate_ref[TOP_K:8, :] = jnp.zeros((8 - TOP_K, TM_MOE), F32)

    cnt = jnp.sum(onehot, axis=1, keepdims=True)
    nch = jnp.floor((cnt + (CHUNK_ROWS - 1)) / CHUNK_ROWS)
    ei = lax.broadcasted_iota(jnp.int32, (N_EXPERTS, N_EXPERTS), 0)
    ej = lax.broadcasted_iota(jnp.int32, (N_EXPERTS, N_EXPERTS), 1)
    seg = jnp.dot((ej < ei).astype(F32), jnp.broadcast_to(nch, (N_EXPERTS, 128)), precision=HIGHEST,
                  preferred_element_type=F32)
    nch_ref[...] = jnp.broadcast_to(nch, (N_EXPERTS, 128)).astype(jnp.int32)
    seg_ref[...] = seg.astype(jnp.int32)

    t_src = lax.broadcasted_iota(jnp.int32, (TM_MOE, TM_MOE), 0)
    t_dst = lax.broadcasted_iota(jnp.int32, (TM_MOE, TM_MOE), 1)
    before = (t_src < t_dst).astype(BF16)
    row_of = (seg[:, 0:1] * CHUNK_ROWS
              + jnp.dot(onehot.astype(BF16), before, preferred_element_type=F32))
    q_iota = lax.broadcasted_iota(jnp.int32, (Q_TILE, TM_MOE), 0)
    perm = jnp.zeros((Q_TILE, TM_MOE), F32)
    for k in range(TOP_K):
        q_k = jnp.sum(jnp.where(sels[k], row_of, 0.0), axis=0, keepdims=True).astype(jnp.int32)
        pos_ref[k:k + 1, :] = q_k
        perm = jnp.where(q_iota == q_k, 1.0, perm)
    pos_ref[TOP_K:8, :] = jnp.zeros((8 - TOP_K, TM_MOE), jnp.int32)
    xt_ref[...] = jnp.dot(perm.astype(BF16), hb, preferred_element_type=F32).astype(BF16)


def _route_sort(x, norm_g, mod, w_rt, b_r):
    tspec = pl.BlockSpec((8, TM_MOE), lambda i: (0, i))
    mspec = pl.BlockSpec((None, N_EXPERTS, 128), lambda i: (i, 0, 0))
    meta = jax.ShapeDtypeStruct((N_TILES, N_EXPERTS, 128), jnp.int32)
    return pl.pallas_call(
        _route_sort_kernel,
        grid=(N_TILES,),
        in_specs=[pl.BlockSpec((TM_MOE, D_MODEL), lambda i: (i, 0)),
                  pl.BlockSpec((1, D_MODEL), lambda i: (0, 0)),
                  pl.BlockSpec((None, 6, D_MODEL), lambda i: (_seg_of_tile(i, TM_MOE), 0, 0)),
                  pl.BlockSpec((N_EXPERTS, D_MODEL), lambda i: (0, 0)),
                  pl.BlockSpec((N_EXPERTS, 1), lambda i: (0, 0))],
        out_specs=[pl.BlockSpec((Q_TILE, D_MODEL), lambda i: (i, 0)), tspec, tspec, mspec, mspec],
        out_shape=[jax.ShapeDtypeStruct((N_TILES * Q_TILE, D_MODEL), BF16),
                   jax.ShapeDtypeStruct((8, T_ALL), jnp.int32), jax.ShapeDtypeStruct((8, T_ALL), F32),
                   meta, meta],
        compiler_params=_cparams(("arbitrary",)),
        name="moe_route_sort",
    )(x, norm_g, mod, w_rt, b_r)


def _expert_kernel(nch_ref, seg_ref, wgu_ref, bgu_ref, wd_ref, bd_ref, xt_ref, yt_ref,
                   wgu_scr, wd_scr, xbuf, ybuf, row_scr, gstart_scr, gsem, ssem):
    del xt_ref
    e = pl.program_id(0)

    def src_row(row):
        return pl.multiple_of(jnp.where(row >= 0, row, READ_SPARE), CHUNK_ROWS)

    def dst_row(row, slot, c):
        spare = c * Q_TILE + jnp.where(slot == 0, Q_TILE - CHUNK_ROWS, Q_TILE - 2 * CHUNK_ROWS)
        return pl.multiple_of(jnp.where(row >= 0, row, spare), CHUNK_ROWS)

    def chunk_rows(c):
        return slice(c * CHUNK_ROWS, (c + 1) * CHUNK_ROWS)

    def start_in(g):
        slot = g % 2
        for c in range(CPG):
            row = row_scr[g * CPG + c]
            pltpu.make_async_copy(yt_ref.at[pl.ds(src_row(row), CHUNK_ROWS), :], xbuf.at[slot, chunk_rows(c), :],
                                  gsem.at[slot]).start()

    def start_out(g, slot):
        for c in range(CPG):
            row = row_scr[g * CPG + c]
            pltpu.make_async_copy(ybuf.at[slot, chunk_rows(c), :],
                                  yt_ref.at[pl.ds(dst_row(row, slot, c), CHUNK_ROWS), :], ssem.at[slot]).start()

    def wait_in(slot):
        pltpu.make_async_copy(yt_ref.at[pl.ds(0, E_GROUP), :], xbuf.at[slot], gsem.at[slot]).wait()

    def wait_out(slot):
        pltpu.make_async_copy(ybuf.at[slot], yt_ref.at[pl.ds(0, E_GROUP), :], ssem.at[slot]).wait()

    @pl.when(e == 0)
    def _():
        def per_expert(ee, cnt):
            gstart_scr[ee] = cnt // CPG

            def per_tile(t, cnt):
                first = (t * CH_PER_TILE + seg_ref[t * N_EXPERTS + ee]) * CHUNK_ROWS

                def per_chunk(j, cnt):
                    row_scr[cnt] = first + j * CHUNK_ROWS
                    return cnt + 1

                return lax.fori_loop(0, nch_ref[t * N_EXPERTS + ee], per_chunk, cnt)

            cnt = lax.fori_loop(0, N_TILES, per_tile, cnt)
            padded = (cnt + CPG - 1) // CPG * CPG

            def pad(i, carry):
                row_scr[i] = -1
                return carry

            lax.fori_loop(cnt, padded, pad, 0)
            return padded

        total = lax.fori_loop(0, N_EXPERTS, per_expert, 0)
        gstart_scr[N_EXPERTS] = total // CPG

        def pad(i, carry):
            row_scr[i] = -1
            return carry

        lax.fori_loop(total, total + CPG, pad, 0)
        ybuf[...] = jnp.zeros_like(ybuf)
        start_out(total // CPG, 0)
        start_out(total // CPG, 1)
        start_in(0)

    g_first = gstart_scr[e]
    g_end = gstart_scr[e + 1]

    @pl.when(g_end > g_first)
    def _():
        wgu_scr[...] = wgu_ref[...].astype(BF16)
        wd_scr[...] = wd_ref[...].astype(BF16)

    def group_step(g, carry):
        slot = g % 2
        start_in(g + 1)
        wait_in(slot)
        wait_out(slot)
        hgu = jnp.dot(xbuf[slot], wgu_scr[...], preferred_element_type=F32) + bgu_ref[...]
        h_glu = jnp.minimum(hgu[:, :D_EXPERT], SWIGLU_LIMIT)
        h_lin = jnp.clip(hgu[:, D_EXPERT:], -SWIGLU_LIMIT, SWIGLU_LIMIT)
        act = (h_lin + 1.0) * (h_glu * _sigmoid(SWIGLU_ALPHA * h_glu))
        y = jnp.dot(act.astype(BF16), wd_scr[...], preferred_element_type=F32) + bd_ref[...]
        ybuf[slot] = y.astype(BF16)
        start_out(g, slot)
        return carry

    lax.fori_loop(g_first, g_end, group_step, 0)

    @pl.when(e == N_EXPERTS - 1)
    def _():
        wait_in(gstart_scr[N_EXPERTS] % 2)
        wait_out(0)
        wait_out(1)


def _experts(nch_flat, seg_flat, xt, w_gu, b_gu, w_down, b_down, layer):
    return pl.pallas_call(
        _expert_kernel,
        grid_spec=pltpu.PrefetchScalarGridSpec(
            num_scalar_prefetch=2,
            grid=(N_EXPERTS,),
            in_specs=[pl.BlockSpec((None, None, D_MODEL, 2 * D_EXPERT), lambda e, n, s: (layer, e, 0, 0)),
                      pl.BlockSpec((None, None, 1, 2 * D_EXPERT), lambda e, n, s: (layer, e, 0, 0)),
                      pl.BlockSpec((None, None, D_EXPERT, D_MODEL), lambda e, n, s: (layer, e, 0, 0)),
                      pl.BlockSpec((None, None, 1, D_MODEL), lambda e, n, s: (layer, e, 0, 0)),
                      pl.BlockSpec(memory_space=pl.ANY)],
            out_specs=pl.BlockSpec(memory_space=pl.ANY),
            scratch_shapes=[pltpu.VMEM((D_MODEL, 2 * D_EXPERT), BF16),
                            pltpu.VMEM((D_EXPERT, D_MODEL), BF16),
                            pltpu.VMEM((2, E_GROUP, D_MODEL), BF16),
                            pltpu.VMEM((2, E_GROUP, D_MODEL), BF16),
                            pltpu.SMEM((MAX_CHUNKS,), jnp.int32),
                            pltpu.SMEM((N_EXPERTS + 1,), jnp.int32),
                            pltpu.SemaphoreType.DMA((2,)),
                            pltpu.SemaphoreType.DMA((2,))]),
        out_shape=jax.ShapeDtypeStruct((N_TILES * Q_TILE, D_MODEL), BF16),
        input_output_aliases={6: 0},
        compiler_params=_cparams(("arbitrary",)),
        name="moe_experts",
    )(nch_flat, seg_flat, w_gu, b_gu, w_down, b_down, xt)


MOE_CTX_TILES = T_CTX // TM_MOE


def _combine_rows(x_ref, mod_ref, pos_ref, gate_ref, yt_ref):
    lane = lax.broadcasted_iota(jnp.int32, (TM_MOE, Q_TILE), 1)
    sel = jnp.zeros((TM_MOE, Q_TILE), F32)
    for k in range(TOP_K):
        sel = jnp.where(lane == pos_ref[:, k:k + 1], gate_ref[:, k:k + 1], sel)
    acc = jnp.dot(sel.astype(BF16), yt_ref[...], preferred_element_type=F32)
    return x_ref[...] + mod_ref[5:6, :] * acc


def _combine_kernel(x_ref, mod_ref, pos_ref, gate_ref, yt_ref, o_ref):
    o_ref[...] = _combine_rows(x_ref, mod_ref, pos_ref, gate_ref, yt_ref)


def _combine_final_kernel(x_ref, mod_ref, pos_ref, gate_ref, yt_ref, fg_ref, ctx_ref, lat_ref):
    y = _combine_rows(x_ref, mod_ref, pos_ref, gate_ref, yt_ref)
    y = y * lax.rsqrt(jnp.mean(y * y, axis=-1, keepdims=True) + RMS_EPS) * fg_ref[...]
    is_ctx = pl.program_id(0) < MOE_CTX_TILES

    @pl.when(is_ctx)
    def _():
        ctx_ref[...] = y

    @pl.when(jnp.logical_not(is_ctx))
    def _():
        lat_ref[...] = y


def _combine(x, mod, pos_t, gate_t, yt, final_g=None):
    tile = pl.BlockSpec((TM_MOE, D_MODEL), lambda i: (i, 0))
    in_specs = [tile,
                pl.BlockSpec((None, 6, D_MODEL), lambda i: (_seg_of_tile(i, TM_MOE), 0, 0)),
                pl.BlockSpec((TM_MOE, 8), lambda i: (i, 0)),
                pl.BlockSpec((TM_MOE, 8), lambda i: (i, 0)),
                pl.BlockSpec((Q_TILE, D_MODEL), lambda i: (i, 0))]
    if final_g is None:
        return pl.pallas_call(
            _combine_kernel, grid=(N_TILES,), in_specs=in_specs, out_specs=tile,
            out_shape=jax.ShapeDtypeStruct((T_ALL, D_MODEL), F32),
            compiler_params=_cparams(("arbitrary",)), name="moe_combine",
        )(x, mod, pos_t, gate_t, yt)
    return pl.pallas_call(
        _combine_final_kernel, grid=(N_TILES,),
        in_specs=in_specs + [pl.BlockSpec((1, D_MODEL), lambda i: (0, 0))],
        out_specs=[pl.BlockSpec((TM_MOE, D_MODEL), lambda i: (jnp.minimum(i, MOE_CTX_TILES - 1), 0)),
                   pl.BlockSpec((TM_MOE, D_MODEL), lambda i: (jnp.maximum(i - MOE_CTX_TILES, 0), 0))],
        out_shape=[jax.ShapeDtypeStruct((T_CTX, D_MODEL), F32), jax.ShapeDtypeStruct((T_LAT, D_MODEL), F32)],
        compiler_params=_cparams(("arbitrary",)), name="moe_combine_final",
    )(x, mod, pos_t, gate_t, yt, final_g)


def _moe(x, norm_g, mod, w_rt, b_r, w_gu, b_gu, w_down, b_down, layer, final_g=None):
    xt, pos, gate, nch, seg = _route_sort(x, norm_g, mod, w_rt, b_r)
    yt = _experts(nch[:, :, 0].reshape(-1), seg[:, :, 0].reshape(-1), xt, w_gu,
                  b_gu.reshape(DEPTH, N_EXPERTS, 1, 2 * D_EXPERT), w_down,
                  b_down.reshape(DEPTH, N_EXPERTS, 1, D_MODEL), layer)
    return _combine(x, mod, pos.T, gate.T, yt, final_g)


def _split_in_cols(w):
    conv0, mq0, om0 = 3 * D_A, 3 * D_A + 2 * D_CONV, 3 * D_A + 2 * D_CONV + 3 * D_M
    gates_end = GATE_OFF + N_GATE_M
    pad = jnp.zeros(w.shape[:-1] + (N_ZF - (2 * D_CONV + D_M + N_GATE_M),), w.dtype)
    zb = jnp.concatenate([w[..., :conv0], w[..., mq0:om0]], axis=-1)
    zg = w[..., gates_end:]
    zf = jnp.concatenate([w[..., conv0:mq0], w[..., om0:GATE_OFF], w[..., GATE_OFF:gates_end], pad], axis=-1)
    return zb, zg, zf


def kernel(x_prompt, x_sample, cache_k, cache_v, state_C, state_n, state_m, c, c_ctx, norm1_g, w_mod, b_mod, w_in, b_in, rpb, w_dw, b_dw, cln_g, cln_b, mnorm_g, w_pa, w_pc, w_pm, w_out, norm2_g, w_router, b_router, w_gu, b_gu, w_down, b_down, final_g):
    cond = jnp.concatenate([c_ctx[None, :], c, jnp.zeros((SEG_PAD - N_SEG, D_MODEL), F32)], axis=0)
    mod_all = _modulation(cond, w_mod, b_mod).reshape(DEPTH, SEG_PAD, 6, D_MODEL)

    x = jnp.concatenate([x_prompt.reshape(T_CTX, D_MODEL), x_sample.reshape(T_LAT, D_MODEL)], axis=0)
    ck = cache_k.reshape(DEC_BATCH, DEPTH, PAST_LEN, D_A).astype(BF16)
    cv = cache_v.reshape(DEC_BATCH, DEPTH, PAST_LEN, D_A).astype(BF16)
    lat_c0 = state_C.reshape(DEC_BATCH, DEPTH, N_STREAM, HEAD_DIM_M, HEAD_DIM_M)
    lat_n0 = state_n.reshape(DEC_BATCH, DEPTH, N_STREAM, HEAD_DIM_M)
    lat_m0 = jnp.broadcast_to(state_m.reshape(DEC_BATCH, DEPTH, N_STREAM, 1),
                              (DEC_BATCH, DEPTH, N_STREAM, HEAD_DIM_M))
    ctx_c0 = jnp.zeros((1, N_STREAM, HEAD_DIM_M, HEAD_DIM_M), F32)
    ctx_n0 = jnp.zeros((1, N_STREAM, HEAD_DIM_M), F32)
    ctx_m0 = jnp.full((1, N_STREAM, HEAD_DIM_M), -jnp.inf, F32)

    ks, vs, cs, ns, ms = [], [], [], [], []
    for l in range(DEPTH):
        mod = mod_all[l]
        g1 = norm1_g[l][None, :]
        zb, zg, zf, kv = _in_proj(x, g1, mod, _split_in_cols(w_in[l].astype(BF16)),
                                  _split_in_cols(b_in[l][None, :]))
        ya = (_ctx_attention(zb), _natt(zb, ck, cv, _natt_bias(rpb[l]), l))
        conv_w = (w_dw[l], b_dw[l][None, :], cln_g[l][None, :], cln_b[l][None, :])
        yc = (_conv(zf, *conv_w, SEQ, BATCH, 0), _conv(zf, *conv_w, DEC_SEQ, DEC_BATCH, T_CTX // DEC_SEQ))
        ng = mnorm_g[l][None, :]
        ym_ctx, c_l, n_l, m_l = _mlstm(zb, zf, ctx_c0, ctx_n0, ctx_m0, ng, SEQ, BATCH, 0, lambda b: (0,))
        ym_lat, _, _, _ = _mlstm(zb, zf, lat_c0, lat_n0, lat_m0, ng, DEC_SEQ, DEC_BATCH, T_CTX // DEC_SEQ,
                                 lambda b: (b, l))
        x = _merge(x, mod, ya, yc, (ym_ctx, ym_lat), zg, w_pa[l].astype(BF16), w_pc[l].astype(BF16),
                   w_pm[l].astype(BF16), w_out[l].astype(BF16))
        x = _moe(x, norm2_g[l][None, :], mod, w_router[l].T, b_router[l][:, None],
                 w_gu, b_gu, w_down, b_down, l, final_g[None, :] if l == DEPTH - 1 else None)
        ks.append(kv[:, :D_A].reshape(BATCH, SEQ, N_HEADS_A, HEAD_DIM_A))
        vs.append(kv[:, D_A:].reshape(BATCH, SEQ, N_HEADS_A, HEAD_DIM_A))
        cs.append(c_l.reshape(BATCH, 2, N_HEADS_M, HEAD_DIM_M, HEAD_DIM_M))
        ns.append(n_l.reshape(BATCH, 2, N_HEADS_M, HEAD_DIM_M))
        ms.append(m_l[:, :, 0].reshape(BATCH, 2, N_HEADS_M))

    y_ctx, y_lat = x
    return (y_ctx.reshape(BATCH, SEQ, D_MODEL), y_lat.reshape(DEC_BATCH, DEC_SEQ, D_MODEL),
            jnp.stack(ks, axis=1), jnp.stack(vs, axis=1), jnp.stack(cs, axis=1),
            jnp.stack(ns, axis=1), jnp.stack(ms, axis=1))
```

```python
import functools

import numpy as np
import jax
import jax.numpy as jnp
from jax import lax
from jax.experimental import pallas as pl
from jax.experimental.pallas import tpu as pltpu

F32 = jnp.float32
BF16 = jnp.bfloat16
HIGHEST = lax.Precision.HIGHEST

D_MODEL = 1024
BATCH = 16
SEQ = 256
DEPTH = 2
DEC_BATCH = 8
DEC_SEQ = 1024
PAST_LEN = 512
GRID_W = 64
N_HEADS_A = 8
HEAD_DIM_A = 64
D_A = N_HEADS_A * HEAD_DIM_A
WIN_ROWS = 8
WIN_COLS = 16
D_CONV = 512
CONV_WIDTH = 31
N_HEADS_M = 4
HEAD_DIM_M = 128
D_M = N_HEADS_M * HEAD_DIM_M
N_GATE_M = 4 * N_HEADS_M
CHUNK = 64
N_EXPERTS = 32
TOP_K = 4
D_EXPERT = 1024
SWIGLU_ALPHA = 1.702
SWIGLU_LIMIT = 7.0
RMS_EPS = 1e-6
LN_EPS = 1e-5
GATE_OFF = 3 * D_A + 2 * D_CONV + 4 * D_M
N_IN = GATE_OFF + N_GATE_M + 3 * D_MODEL

T_CTX = BATCH * SEQ
T_LAT = DEC_BATCH * DEC_SEQ
T_ALL = T_CTX + T_LAT
N_SEG = 1 + DEC_BATCH
SEG_PAD = 16
GRID_ROWS = DEC_SEQ // GRID_W
NEG = -1e30

N_ZB = 6 * 512
N_ZG = 3 * D_MODEL
N_ZF = 1664
QB_QA, QB_KA, QB_VA, QB_QM, QB_KM, QB_VM = 0, 1, 2, 3, 4, 5
ZG_GA, ZG_GC, ZG_GM = 0, 1, 2
ZF_CU, ZF_CG, ZF_OM = 0, 1, 2
ZF_GATES = 12

TM_TOK = 512
TM_PROJ = 256
TM_MOE = 512
N_TILES = T_ALL // TM_MOE
CHUNK_ROWS = 16
MXU_ROWS = 256
Q_TILE = -(-(TM_MOE * TOP_K + N_EXPERTS * (CHUNK_ROWS - 1)) // MXU_ROWS) * MXU_ROWS
CH_PER_TILE = Q_TILE // CHUNK_ROWS
E_GROUP = 256
CPG = E_GROUP // CHUNK_ROWS
MAX_CHUNKS = ((T_ALL * TOP_K + N_TILES * N_EXPERTS * (CHUNK_ROWS - 1)) // CHUNK_ROWS
              + N_EXPERTS * (CPG - 1)) + CPG
assert Q_TILE - (TM_MOE * TOP_K + N_EXPERTS * (CHUNK_ROWS - 1)) >= 2 * CHUNK_ROWS and N_TILES > CPG
READ_SPARE = N_TILES * Q_TILE - CHUNK_ROWS
VMEM_LIMIT = 60 * 1024 * 1024


def _cparams(sem=None):
    return pltpu.CompilerParams(dimension_semantics=sem, vmem_limit_bytes=VMEM_LIMIT)


def _seg_of_tile(i, tile):
    n_ctx = T_CTX // tile
    per_lat = DEC_SEQ // tile
    return jnp.where(i < n_ctx, 0, 1 + (i - n_ctx) // per_lat)


def _dot_nt(a, b):
    return lax.dot_general(a, b, (((1,), (1,)), ((), ())), preferred_element_type=F32)


def _sigmoid(x):
    return 1.0 / (1.0 + jnp.exp(-x))


def _mod_kernel(c_ref, w_ref, b_ref, o_ref):
    c = c_ref[...]
    s = c * _sigmoid(c)
    o_ref[...] = jnp.dot(s, w_ref[...], precision=HIGHEST, preferred_element_type=F32) + b_ref[...]


def _modulation(cond, w_mod, b_mod):
    tn = 1536
    return pl.pallas_call(
        _mod_kernel,
        grid=(DEPTH, 6 * D_MODEL // tn),
        in_specs=[pl.BlockSpec((SEG_PAD, D_MODEL), lambda l, j: (0, 0)),
                  pl.BlockSpec((None, D_MODEL, tn), lambda l, j: (l, 0, j)),
                  pl.BlockSpec((None, 1, tn), lambda l, j: (l, 0, j))],
        out_specs=pl.BlockSpec((None, SEG_PAD, tn), lambda l, j: (l, 0, j)),
        out_shape=jax.ShapeDtypeStruct((DEPTH, SEG_PAD, 6 * D_MODEL), F32),
        compiler_params=_cparams(("arbitrary", "arbitrary")),
        name="modulation",
    )(cond, w_mod, b_mod.reshape(DEPTH, 1, 6 * D_MODEL))


def _normmod(x, g, mod, shift_idx, scale_idx):
    y = x * lax.rsqrt(jnp.mean(x * x, axis=-1, keepdims=True) + RMS_EPS) * g
    return y * (1.0 + mod[scale_idx:scale_idx + 1, :]) + mod[shift_idx:shift_idx + 1, :]


PROJ_CTX_TILES = T_CTX // TM_PROJ


def _in_proj_kernel(x_ref, g_ref, mod_ref, wb_ref, bb_ref, wg_ref, bg_ref, wf_ref, bf_ref,
                    zb_ref, zg_ref, zf_ref, kv_ref):
    h = _normmod(x_ref[...], g_ref[...], mod_ref[...], 0, 1).astype(BF16)
    acc = jnp.dot(h, wb_ref[...], preferred_element_type=F32) + bb_ref[...]
    zb_ref[...] = acc.astype(BF16)

    @pl.when(pl.program_id(0) < PROJ_CTX_TILES)
    def _():
        kv_ref[...] = acc[:, D_A:3 * D_A]

    gates = jnp.dot(h, wg_ref[...], preferred_element_type=F32) + bg_ref[...]
    zg_ref[...] = _sigmoid(gates).astype(BF16)
    zf_ref[...] = jnp.dot(h, wf_ref[...], preferred_element_type=F32) + bf_ref[...]


def _in_proj(x, norm_g, mod, w, b):
    def full(a):
        return pl.BlockSpec(a.shape, lambda i: (0, 0))

    def rows(n):
        return pl.BlockSpec((TM_PROJ, n), lambda i: (i, 0))

    return pl.pallas_call(
        _in_proj_kernel,
        grid=(T_ALL // TM_PROJ,),
        in_specs=[rows(D_MODEL),
                  pl.BlockSpec((1, D_MODEL), lambda i: (0, 0)),
                  pl.BlockSpec((None, 6, D_MODEL), lambda i: (_seg_of_tile(i, TM_PROJ), 0, 0)),
                  full(w[0]), full(b[0]), full(w[1]), full(b[1]), full(w[2]), full(b[2])],
        out_specs=[rows(N_ZB), rows(N_ZG), rows(N_ZF),
                   pl.BlockSpec((TM_PROJ, 2 * D_A), lambda i: (jnp.minimum(i, PROJ_CTX_TILES - 1), 0))],
        out_shape=[jax.ShapeDtypeStruct((T_ALL, N_ZB), BF16), jax.ShapeDtypeStruct((T_ALL, N_ZG), BF16),
                   jax.ShapeDtypeStruct((T_ALL, N_ZF), F32), jax.ShapeDtypeStruct((T_CTX, 2 * D_A), F32)],
        compiler_params=_cparams(("arbitrary",)),
        name="in_proj",
    )(x, norm_g, mod, w[0], b[0], w[1], b[1], w[2], b[2])


HEAD_PAIR = 2 * HEAD_DIM_A
ATT_SCALE = HEAD_DIM_A ** -0.5


def _pair_queries(q2):
    lo = lax.broadcasted_iota(jnp.int32, (1, HEAD_PAIR), 1) < HEAD_DIM_A
    q2 = q2 * ATT_SCALE
    zero = jnp.zeros_like(q2)
    return lo, jnp.concatenate([jnp.where(lo, q2, zero), jnp.where(lo, zero, q2)], axis=0)


def _unpair(lo, o_stacked):
    rows = o_stacked.shape[0] // 2
    return jnp.where(lo, o_stacked[:rows], o_stacked[rows:])


def _ctx_attn_kernel(q_ref, k_ref, v_ref, o_ref):
    for hp in range(N_HEADS_A // 2):
        sl = slice(hp * HEAD_PAIR, (hp + 1) * HEAD_PAIR)
        lo, qs = _pair_queries(q_ref[:, sl])
        s = _dot_nt(qs, k_ref[:, sl])
        p = jnp.exp(s - jnp.max(s, axis=-1, keepdims=True))
        l = jnp.sum(p, axis=-1, keepdims=True)
        o = jnp.dot(p.astype(BF16), v_ref[:, sl], preferred_element_type=F32) / l
        o_ref[:, sl] = _unpair(lo, o)


def _ctx_attention(zb):
    def spec(cb):
        return pl.BlockSpec((SEQ, D_A), lambda b: (b, cb))

    return pl.pallas_call(
        _ctx_attn_kernel,
        grid=(BATCH,),
        in_specs=[spec(QB_QA), spec(QB_KA), spec(QB_VA)],
        out_specs=pl.BlockSpec((SEQ, D_A), lambda b: (b, 0)),
        out_shape=jax.ShapeDtypeStruct((T_CTX, D_A), F32),
        compiler_params=_cparams(("arbitrary",)),
        name="ctx_attention",
    )(zb, zb, zb)


NQ_ROWS = 4
NW_ROWS = 12
NQ_BLOCKS = GRID_ROWS // NQ_ROWS
NQ_TOK = NQ_ROWS * GRID_W
NW_TOK = NW_ROWS * GRID_W


def _window_row(qb, xp):
    return xp.clip(qb * NQ_ROWS - WIN_ROWS // 2, 0, GRID_ROWS - NW_ROWS)


N_REL_ROWS = 2 * WIN_ROWS - 1


def _natt_rel_rows():
    r = np.arange(NQ_BLOCKS)[:, None, None] * NQ_ROWS + np.arange(NQ_ROWS)[None, :, None]
    krow = _window_row(np.arange(NQ_BLOCKS), np)[:, None, None] + np.arange(NW_ROWS)[None, None, :]
    rs = np.clip(r - WIN_ROWS // 2, 0, GRID_ROWS - WIN_ROWS)
    assert ((rs >= krow[:, :, :1]) & (rs + WIN_ROWS <= krow[:, :, -1:] + 1)).all()
    return np.where((krow >= rs) & (krow < rs + WIN_ROWS), krow - r + WIN_ROWS - 1, N_REL_ROWS)


NATT_REL = _natt_rel_rows()


def _natt_kernel(q_ref, k_ref, v_ref, kc_ref, vc_ref, toe_ref, o_ref, bias_scr):
    qb = pl.program_id(0)

    @pl.when(pl.program_id(1) == 0)
    def _():
        for v in range(NQ_BLOCKS):
            @pl.when(qb == v)
            def _(v=v):
                for h in range(N_HEADS_A):
                    for rq in range(NQ_ROWS):
                        for kr in range(NW_ROWS):
                            half = (kr % 2) * GRID_W
                            bias_scr[h, rq * GRID_W:(rq + 1) * GRID_W, kr * GRID_W:(kr + 1) * GRID_W] = (
                                toe_ref[h, int(NATT_REL[v, rq, kr]), :, half:half + GRID_W])

    start = pl.multiple_of(_window_row(qb, jnp) * GRID_W, NQ_TOK)
    band = NW_TOK
    for hp in range(N_HEADS_A // 2):
        sl = slice(hp * HEAD_PAIR, (hp + 1) * HEAD_PAIR)
        lo, qs = _pair_queries(q_ref[:, sl])
        bias = bias_scr[2 * hp:2 * hp + 2].reshape(2 * NQ_TOK, band)
        s_loc = _dot_nt(qs, k_ref[pl.ds(start, band), sl]) + bias
        s_ctx = _dot_nt(qs, kc_ref[:, sl])
        m = jnp.maximum(jnp.max(s_loc, axis=-1, keepdims=True), jnp.max(s_ctx, axis=-1, keepdims=True))
        p_loc = jnp.exp(s_loc - m)
        p_ctx = jnp.exp(s_ctx - m)
        l = jnp.sum(p_loc, axis=-1, keepdims=True) + jnp.sum(p_ctx, axis=-1, keepdims=True)
        o = (jnp.dot(p_loc.astype(BF16), v_ref[pl.ds(start, band), sl], preferred_element_type=F32)
             + jnp.dot(p_ctx.astype(BF16), vc_ref[:, sl], preferred_element_type=F32))
        o_ref[:, sl] = _unpair(lo, o / l)


def _natt_bias(rpb_l):
    qc = np.arange(GRID_W)
    kc = np.arange(GRID_W)
    cs = np.clip(qc - WIN_COLS // 2, 0, GRID_W - WIN_COLS)
    ok = (kc[None, :] >= cs[:, None]) & (kc[None, :] < cs[:, None] + WIN_COLS)
    dc = np.clip(kc[None, :] - qc[:, None] + WIN_COLS - 1, 0, 2 * WIN_COLS - 2)
    pick = (dc[None] == np.arange(2 * WIN_COLS - 1)[:, None, None]).astype(np.float32)
    toe = jnp.einsum('hdc,cqk->hdqk', rpb_l, jnp.asarray(pick), precision=HIGHEST)
    toe = jnp.where(jnp.asarray(ok)[None, None], toe, NEG)
    toe = jnp.concatenate([toe, jnp.full((N_HEADS_A, 1, GRID_W, GRID_W), NEG, F32)], axis=1)
    return jnp.concatenate([toe, toe], axis=-1)


def _natt(zb, cache_k, cache_v, toe, layer):
    lat0 = T_CTX // DEC_SEQ
    row0 = T_CTX // NQ_TOK
    return pl.pallas_call(
        _natt_kernel,
        grid=(NQ_BLOCKS, DEC_BATCH),
        in_specs=[pl.BlockSpec((NQ_TOK, D_A), lambda qb, b: (row0 + b * NQ_BLOCKS + qb, QB_QA)),
                  pl.BlockSpec((DEC_SEQ, D_A), lambda qb, b: (lat0 + b, QB_KA)),
                  pl.BlockSpec((DEC_SEQ, D_A), lambda qb, b: (lat0 + b, QB_VA)),
                  pl.BlockSpec((None, None, PAST_LEN, D_A), lambda qb, b: (b, layer, 0, 0)),
                  pl.BlockSpec((None, None, PAST_LEN, D_A), lambda qb, b: (b, layer, 0, 0)),
                  pl.BlockSpec((N_HEADS_A, N_REL_ROWS + 1, GRID_W, 2 * GRID_W), lambda qb, b: (0, 0, 0, 0))],
        out_specs=pl.BlockSpec((NQ_TOK, D_A), lambda qb, b: (b * NQ_BLOCKS + qb, 0)),
        out_shape=jax.ShapeDtypeStruct((T_LAT, D_A), F32),
        scratch_shapes=[pltpu.VMEM((N_HEADS_A, NQ_TOK, NW_TOK), F32)],
        compiler_params=_cparams(("arbitrary", "arbitrary")),
        name="nbr_attention",
    )(zb, zb, zb, cache_k, cache_v, toe)


CONV_HALO = 16
CONV_ROWS = 64


SUBLANES = 8


def _conv_kernel(u_ref, g_ref, w_ref, b_ref, lg_ref, lb_ref, o_ref, pad_scr, sh_scr, *, seq):
    zeros = jnp.zeros((CONV_HALO, D_CONV), F32)
    pad_scr[0:CONV_HALO, :] = zeros
    pad_scr[CONV_HALO + seq:2 * CONV_HALO + seq, :] = zeros
    pad_scr[CONV_HALO:CONV_HALO + seq, :] = u_ref[...] * _sigmoid(g_ref[...])
    n_sh = seq + 2 * CONV_HALO - SUBLANES
    for s in range(SUBLANES):
        sh_scr[s] = pad_scr[s:s + n_sh, :]
    first = CONV_HALO - CONV_WIDTH // 2
    for c in range(seq // CONV_ROWS):
        base = c * CONV_ROWS
        acc = jnp.broadcast_to(b_ref[...], (CONV_ROWS, D_CONV))
        for j in range(CONV_WIDTH):
            q, s = divmod(first + j, SUBLANES)
            row0 = base + q * SUBLANES
            acc = acc + sh_scr[s, row0:row0 + CONV_ROWS, :] * w_ref[j:j + 1, :]
        mu = jnp.mean(acc, axis=-1, keepdims=True)
        xc = acc - mu
        var = jnp.mean(xc * xc, axis=-1, keepdims=True)
        y = xc * lax.rsqrt(var + LN_EPS) * lg_ref[...] + lb_ref[...]
        o_ref[base:base + CONV_ROWS, :] = y * _sigmoid(y)


def _conv(z, w_dw, b_dw, ln_g, ln_b, seq, n_seq, row_block0):
    def vec():
        return pl.BlockSpec((1, D_CONV), lambda b: (0, 0))

    return pl.pallas_call(
        functools.partial(_conv_kernel, seq=seq),
        grid=(n_seq,),
        in_specs=[pl.BlockSpec((seq, D_CONV), lambda b: (row_block0 + b, ZF_CU)),
                  pl.BlockSpec((seq, D_CONV), lambda b: (row_block0 + b, ZF_CG)),
                  pl.BlockSpec((CONV_WIDTH, D_CONV), lambda b: (0, 0)),
                  vec(), vec(), vec()],
        out_specs=pl.BlockSpec((seq, D_CONV), lambda b: (b, 0)),
        out_shape=jax.ShapeDtypeStruct((n_seq * seq, D_CONV), F32),
        scratch_shapes=[pltpu.VMEM((seq + 2 * CONV_HALO, D_CONV), F32),
                        pltpu.VMEM((SUBLANES, seq + 2 * CONV_HALO - SUBLANES, D_CONV), F32)],
        compiler_params=_cparams(("arbitrary",)),
        name="conformer_conv",
    )(z, z, w_dw, b_dw, ln_g, ln_b)


N_STREAM = 2 * N_HEADS_M
MCHUNK = 128


PAIR_M = 2 * HEAD_DIM_M
N_PAIR = N_STREAM // 2


def _per_head_rows(r):
    return jnp.concatenate([jnp.broadcast_to(r[0:1], (HEAD_DIM_M, r.shape[1])),
                            jnp.broadcast_to(r[1:2], (HEAD_DIM_M, r.shape[1]))], axis=0)


def _mlstm_kernel(q_ref, k_ref, v_ref, om_ref, gt_ref, c0_ref, n0_ref, m0_ref, ng_ref,
                  y_ref, c_out, n_out, m_out, hf_scr, hb_scr, c_scr, n_scr, m_scr, cbd_scr, vbd_scr,
                  rows_scr, acol_scr, *, seq):
    nc = seq // MCHUNK
    c_scr[...] = c0_ref[...]
    n_scr[...] = n0_ref[...]
    m_scr[...] = m0_ref[...]
    cbd_scr[...] = jnp.zeros_like(cbd_scr)
    vbd_scr[...] = jnp.zeros_like(vbd_scr)
    for s_id in range(N_STREAM):
        blk = slice((s_id % 2) * HEAD_DIM_M, (s_id % 2 + 1) * HEAD_DIM_M)
        cbd_scr[s_id // 2, blk, blk] = c0_ref[s_id].astype(BF16)
    first_head = lax.broadcasted_iota(jnp.int32, (1, PAIR_M), 1) < HEAD_DIM_M
    rows = lax.broadcasted_iota(jnp.int32, (MCHUNK, MCHUNK), 0)
    cols = lax.broadcasted_iota(jnp.int32, (MCHUNK, MCHUNK), 1)
    tris = ((cols <= rows).astype(F32), (cols >= rows).astype(F32))
    reach = (rows <= cols, rows >= cols)
    lane = lax.broadcasted_iota(jnp.int32, (1, MCHUNK), 1)
    kscale = HEAD_DIM_M ** -0.5

    for cc in range(nc):
        g = gt_ref[cc * MCHUNK:(cc + 1) * MCHUNK, :]
        lf = jnp.minimum(g, 0.0) - jnp.log(1.0 + jnp.exp(-jnp.abs(g)))
        g_t = g.T
        for d in range(2):
            cum = jnp.dot(tris[d], lf, precision=HIGHEST, preferred_element_type=F32)
            cum_t = cum.T
            i0 = 2 * N_HEADS_M * d
            i_rows = g_t[i0:i0 + N_HEADS_M, :]
            b_rows = cum_t[i0 + N_HEADS_M:i0 + 2 * N_HEADS_M, :]
            reach_max = i_rows - b_rows
            for step in (1, 2, 4, 8, 16, 32, 64):
                if d == 0:
                    shifted = jnp.where(lane >= step, pltpu.roll(reach_max, step, axis=1), NEG)
                else:
                    shifted = jnp.where(lane < MCHUNK - step, pltpu.roll(reach_max, MCHUNK - step, axis=1), NEG)
                reach_max = jnp.maximum(reach_max, shifted)
            rows_scr[2 * cc + d, 0] = i_rows
            rows_scr[2 * cc + d, 1] = b_rows
            rows_scr[2 * cc + d, 2] = reach_max
            acol_scr[2 * cc + d] = g - pltpu.roll(cum, MCHUNK - N_HEADS_M, axis=1)

    def chunk_step(c, carry):
        for d in range(2):
            cidx = c if d == 0 else nc - 1 - c
            off = pl.multiple_of(cidx * MCHUNK, MCHUNK)
            last = MCHUNK - 1 if d == 0 else 0
            i0 = 2 * N_HEADS_M * d
            heads = slice(N_HEADS_M * d, N_HEADS_M * (d + 1))
            i_rows = rows_scr[2 * cidx + d, 0]
            b_rows = rows_scr[2 * cidx + d, 1]
            reach_max = rows_scr[2 * cidx + d, 2]
            a_cols = acol_scr[2 * cidx + d]
            m_prevs = m_scr[heads, :]
            inters = b_rows + m_prevs
            m_ts = jnp.maximum(inters, b_rows + reach_max)
            w_inters = jnp.exp(inters - m_ts)
            floors = jnp.exp(-m_ts)
            b_lasts = b_rows[:, last:last + 1]
            m_news = m_ts[:, last:last + 1]
            w_prevs = jnp.exp(b_lasts + m_prevs[:, 0:1] - m_news)
            w_srcs = kscale * jnp.exp(b_lasts - b_rows + i_rows - m_news)
            m_scr[heads, :] = jnp.broadcast_to(m_news, (N_HEADS_M, HEAD_DIM_M))
            for hp in range(N_HEADS_M // 2):
                pid = (N_HEADS_M // 2) * d + hp
                hh = slice(2 * hp, 2 * hp + 2)
                ps = slice(hp * PAIR_M, (hp + 1) * PAIR_M)
                q2 = q_ref[pl.ds(off, MCHUNK), ps]
                k2 = k_ref[pl.ds(off, MCHUNK), ps]
                v2_t = v_ref[pl.ds(off, MCHUNK), ps].astype(F32).T
                k_zero = jnp.zeros_like(k2)
                k_stack = jnp.concatenate([jnp.where(first_head, k2, k_zero),
                                           jnp.where(first_head, k_zero, k2)], axis=0)
                decay = jnp.concatenate(
                    [jnp.exp(jnp.where(reach[d], b_rows[h:h + 1] + a_cols[:, i0 + h:i0 + h + 1], NEG)
                             - m_ts[h:h + 1]) for h in (2 * hp, 2 * hp + 1)], axis=0)
                s_t = _dot_nt(k_stack, q2) * (kscale * decay)
                col_sums = jnp.concatenate([jnp.sum(s_t[:MCHUNK], axis=0, keepdims=True),
                                            jnp.sum(s_t[MCHUNK:], axis=0, keepdims=True)], axis=0)
                n2 = n_scr[N_HEADS_M * d + 2 * hp:N_HEADS_M * d + 2 * hp + 2, :]
                n_zero = jnp.zeros((1, HEAD_DIM_M), F32)
                n_mat = jnp.concatenate([jnp.concatenate([n2[0:1], n_zero], axis=1),
                                         jnp.concatenate([n_zero, n2[1:2]], axis=1),
                                         jnp.zeros((SUBLANES - 2, PAIR_M), F32)], axis=0)
                n_q = _dot_nt(n_mat.astype(BF16), q2)[0:2, :]
                den = w_inters[hh] * n_q + col_sums
                inv = 1.0 / jnp.maximum(jnp.abs(den), floors[hh])
                vbd_scr[pid, :HEAD_DIM_M, :MCHUNK] = v2_t[:HEAD_DIM_M].astype(BF16)
                vbd_scr[pid, HEAD_DIM_M:, MCHUNK:] = v2_t[HEAD_DIM_M:].astype(BF16)
                num_t = (_per_head_rows(w_inters[hh]) * _dot_nt(cbd_scr[pid], q2)
                         + jnp.dot(vbd_scr[pid], s_t.astype(BF16), preferred_element_type=F32))
                h_t = num_t * _per_head_rows(inv)
                if d == 0:
                    hf_scr[cidx, ps, :] = h_t
                else:
                    hb_scr[cidx, ps, :] = h_t
                upd = jnp.dot((v2_t * _per_head_rows(w_srcs[hh])).astype(BF16), k2,
                              preferred_element_type=F32)
                w_mat = jnp.concatenate([w_srcs[hh], jnp.zeros((SUBLANES - 2, MCHUNK), F32)], axis=0)
                n_upd = jnp.dot(w_mat.astype(BF16), k2, preferred_element_type=F32)
                for j in range(2):
                    h = 2 * hp + j
                    s_id = N_HEADS_M * d + h
                    blk = slice(j * HEAD_DIM_M, (j + 1) * HEAD_DIM_M)
                    c_new = w_prevs[h:h + 1] * c_scr[s_id] + upd[blk, blk]
                    c_scr[s_id] = c_new
                    cbd_scr[pid, blk, blk] = c_new.astype(BF16)
                    n_scr[s_id:s_id + 1, :] = w_prevs[h:h + 1] * n2[j:j + 1] + n_upd[j:j + 1, blk]
        return carry

    lax.fori_loop(0, nc, chunk_step, 0)

    for c in range(nc):
        ts = slice(c * MCHUNK, (c + 1) * MCHUNK)
        for h in range(N_HEADS_M):
            hs = slice(h * HEAD_DIM_M, (h + 1) * HEAD_DIM_M)
            hsum = hf_scr[c, hs, :] + hb_scr[c, hs, :]
            mu = jnp.mean(hsum, axis=0, keepdims=True)
            xc = hsum - mu
            var = jnp.mean(xc * xc, axis=0, keepdims=True)
            hn = (xc * lax.rsqrt(var + LN_EPS)).T
            y_ref[ts, hs] = _sigmoid(om_ref[ts, hs]) * (hn * ng_ref[:, hs])
    c_out[...] = c_scr[...]
    n_out[...] = n_scr[...]
    m_out[...] = m_scr[...]


def _mlstm(zb, zf, c0, n0, m0, norm_g, seq, n_seq, row_block0, state_map):
    lead = len(state_map(0))

    def zspec(cb):
        return pl.BlockSpec((seq, D_M), lambda b: (row_block0 + b, cb))

    def sspec(tail):
        return pl.BlockSpec((None,) * lead + tail, lambda b: state_map(b) + (0,) * len(tail))

    return pl.pallas_call(
        functools.partial(_mlstm_kernel, seq=seq),
        grid=(n_seq,),
        in_specs=[zspec(QB_QM), zspec(QB_KM), zspec(QB_VM), zspec(ZF_OM),
                  pl.BlockSpec((seq, 128), lambda b: (row_block0 + b, ZF_GATES)),
                  sspec((N_STREAM, HEAD_DIM_M, HEAD_DIM_M)),
                  sspec((N_STREAM, HEAD_DIM_M)),
                  sspec((N_STREAM, HEAD_DIM_M)),
                  pl.BlockSpec((1, D_M), lambda b: (0, 0))],
        out_specs=[pl.BlockSpec((seq, D_M), lambda b: (b, 0)),
                   pl.BlockSpec((None, N_STREAM, HEAD_DIM_M, HEAD_DIM_M), lambda b: (b, 0, 0, 0)),
                   pl.BlockSpec((None, N_STREAM, HEAD_DIM_M), lambda b: (b, 0, 0)),
                   pl.BlockSpec((None, N_STREAM, HEAD_DIM_M), lambda b: (b, 0, 0))],
        out_shape=[jax.ShapeDtypeStruct((n_seq * seq, D_M), F32),
                   jax.ShapeDtypeStruct((n_seq, N_STREAM, HEAD_DIM_M, HEAD_DIM_M), F32),
                   jax.ShapeDtypeStruct((n_seq, N_STREAM, HEAD_DIM_M), F32),
                   jax.ShapeDtypeStruct((n_seq, N_STREAM, HEAD_DIM_M), F32)],
        scratch_shapes=[pltpu.VMEM((seq // MCHUNK, D_M, MCHUNK), F32),
                        pltpu.VMEM((seq // MCHUNK, D_M, MCHUNK), F32),
                        pltpu.VMEM((N_STREAM, HEAD_DIM_M, HEAD_DIM_M), F32),
                        pltpu.VMEM((N_STREAM, HEAD_DIM_M), F32),
                        pltpu.VMEM((N_STREAM, HEAD_DIM_M), F32),
                        pltpu.VMEM((N_PAIR, PAIR_M, PAIR_M), BF16),
                        pltpu.VMEM((N_PAIR, PAIR_M, 2 * MCHUNK), BF16),
                        pltpu.VMEM((2 * (seq // MCHUNK), 3, N_HEADS_M, MCHUNK), F32),
                        pltpu.VMEM((2 * (seq // MCHUNK), MCHUNK, 128), F32)],
        compiler_params=_cparams(("arbitrary",)),
        name="mlstm",
    )(zb, zb, zb, zf, zf, c0, n0, m0, norm_g)


N_CTX_TILES = T_CTX // TM_TOK


def _merge_kernel(x_ref, mod_ref, ya_c, ya_l, yc_c, yc_l, ym_c, ym_l, ga_ref, gc_ref, gm_ref,
                  wa_ref, wc_ref, wm_ref, wo_ref, o_ref):
    is_ctx = pl.program_id(0) < N_CTX_TILES

    def branch(y_ctx, y_lat, g_ref, w_ref):
        y = jnp.where(is_ctx, y_ctx[...], y_lat[...]).astype(BF16)
        return g_ref[...].astype(F32) * jnp.dot(y, w_ref[...], preferred_element_type=F32)

    merged = (branch(ya_c, ya_l, ga_ref, wa_ref) + branch(yc_c, yc_l, gc_ref, wc_ref)
              + branch(ym_c, ym_l, gm_ref, wm_ref))
    mix = jnp.dot(merged.astype(BF16), wo_ref[...], preferred_element_type=F32)
    o_ref[...] = x_ref[...] + mod_ref[2:3, :] * mix


def _merge(x, mod, ya, yc, ym, zg, w_pa, w_pc, w_pm, w_out):
    def rows(width, cb=0):
        return pl.BlockSpec((TM_TOK, width), lambda i: (i, cb))

    def ctx_rows(width):
        return pl.BlockSpec((TM_TOK, width), lambda i: (jnp.minimum(i, N_CTX_TILES - 1), 0))

    def lat_rows(width):
        return pl.BlockSpec((TM_TOK, width), lambda i: (jnp.maximum(i - N_CTX_TILES, 0), 0))

    def full(shape):
        return pl.BlockSpec(shape, lambda i: (0, 0))

    return pl.pallas_call(
        _merge_kernel,
        grid=(T_ALL // TM_TOK,),
        in_specs=[rows(D_MODEL),
                  pl.BlockSpec((None, 6, D_MODEL), lambda i: (_seg_of_tile(i, TM_TOK), 0, 0)),
                  ctx_rows(D_A), lat_rows(D_A), ctx_rows(D_CONV), lat_rows(D_CONV),
                  ctx_rows(D_M), lat_rows(D_M),
                  rows(D_MODEL, ZG_GA), rows(D_MODEL, ZG_GC), rows(D_MODEL, ZG_GM),
                  full((D_A, D_MODEL)), full((D_CONV, D_MODEL)), full((D_M, D_MODEL)),
                  full((D_MODEL, D_MODEL))],
        out_specs=rows(D_MODEL),
        out_shape=jax.ShapeDtypeStruct((T_ALL, D_MODEL), F32),
        compiler_params=_cparams(("arbitrary",)),
        name="merge",
    )(x, mod, ya[0], ya[1], yc[0], yc[1], ym[0], ym[1], zg, zg, zg, w_pa, w_pc, w_pm, w_out)


def _route_sort_kernel(x_ref, g_ref, mod_ref, wr_ref, br_ref, xt_ref, pos_ref, gate_ref, nch_ref, seg_ref):
    h = _normmod(x_ref[...], g_ref[...], mod_ref[...], 3, 4)
    hb = h.astype(BF16)
    logits = _dot_nt(wr_ref[...].astype(BF16), hb) + br_ref[...]
    e_iota = lax.broadcasted_iota(jnp.int32, (N_EXPERTS, TM_MOE), 0).astype(F32)
    sels, vals = [], []
    l = logits
    for k in range(TOP_K):
        m = jnp.max(l, axis=0, keepdims=True)
        idx = jnp.min(jnp.where(l == m, e_iota, float(N_EXPERTS)), axis=0, keepdims=True)
        sel = e_iota == idx
        vals.append(m)
        sels.append(sel)
        l = jnp.where(sel, -jnp.inf, l)
    exps = [jnp.exp(v - vals[0]) for v in vals]
    tot = exps[0] + exps[1] + exps[2] + exps[3]
    onehot = jnp.zeros((N_EXPERTS, TM_MOE), F32)
    for k in range(TOP_K):
        gate_ref[k:k + 1, :] = exps[k] / tot
        onehot = onehot + sels[k].astype(F32)
    gate_ref[TOP_K:8, :] = jnp.zeros((8 - TOP_K, TM_MOE), F32)

    cnt = jnp.sum(onehot, axis=1, keepdims=True)
    nch = jnp.floor((cnt + (CHUNK_ROWS - 1)) / CHUNK_ROWS)
    ei = lax.broadcasted_iota(jnp.int32, (N_EXPERTS, N_EXPERTS), 0)
    ej = lax.broadcasted_iota(jnp.int32, (N_EXPERTS, N_EXPERTS), 1)
    seg = jnp.dot((ej < ei).astype(F32), jnp.broadcast_to(nch, (N_EXPERTS, 128)), precision=HIGHEST,
                  preferred_element_type=F32)
    nch_ref[...] = jnp.broadcast_to(nch, (N_EXPERTS, 128)).astype(jnp.int32)
    seg_ref[...] = seg.astype(jnp.int32)

    t_src = lax.broadcasted_iota(jnp.int32, (TM_MOE, TM_MOE), 0)
    t_dst = lax.broadcasted_iota(jnp.int32, (TM_MOE, TM_MOE), 1)
    before = (t_src < t_dst).astype(BF16)
    row_of = (seg[:, 0:1] * CHUNK_ROWS
              + jnp.dot(onehot.astype(BF16), before, preferred_element_type=F32))
    q_iota = lax.broadcasted_iota(jnp.int32, (Q_TILE, TM_MOE), 0)
    perm = jnp.zeros((Q_TILE, TM_MOE), F32)
    for k in range(TOP_K):
        q_k = jnp.sum(jnp.where(sels[k], row_of, 0.0), axis=0, keepdims=True).astype(jnp.int32)
        pos_ref[k:k + 1, :] = q_k
        perm = jnp.where(q_iota == q_k, 1.0, perm)
    pos_ref[TOP_K:8, :] = jnp.zeros((8 - TOP_K, TM_MOE), jnp.int32)
    xt_ref[...] = jnp.dot(perm.astype(BF16), hb, preferred_element_type=F32).astype(BF16)


def _route_sort(x, norm_g, mod, w_rt, b_r):
    tspec = pl.BlockSpec((8, TM_MOE), lambda i: (0, i))
    mspec = pl.BlockSpec((None, N_EXPERTS, 128), lambda i: (i, 0, 0))
    meta = jax.ShapeDtypeStruct((N_TILES, N_EXPERTS, 128), jnp.int32)
    return pl.pallas_call(
        _route_sort_kernel,
        grid=(N_TILES,),
        in_specs=[pl.BlockSpec((TM_MOE, D_MODEL), lambda i: (i, 0)),
                  pl.BlockSpec((1, D_MODEL), lambda i: (0, 0)),
                  pl.BlockSpec((None, 6, D_MODEL), lambda i: (_seg_of_tile(i, TM_MOE), 0, 0)),
                  pl.BlockSpec((N_EXPERTS, D_MODEL), lambda i: (0, 0)),
                  pl.BlockSpec((N_EXPERTS, 1), lambda i: (0, 0))],
        out_specs=[pl.BlockSpec((Q_TILE, D_MODEL), lambda i: (i, 0)), tspec, tspec, mspec, mspec],
        out_shape=[jax.ShapeDtypeStruct((N_TILES * Q_TILE, D_MODEL), BF16),
                   jax.ShapeDtypeStruct((8, T_ALL), jnp.int32), jax.ShapeDtypeStruct((8, T_ALL), F32),
                   meta, meta],
        compiler_params=_cparams(("arbitrary",)),
        name="moe_route_sort",
    )(x, norm_g, mod, w_rt, b_r)


def _expert_kernel(nch_ref, seg_ref, wgu_ref, bgu_ref, wd_ref, bd_ref, xt_ref, yt_ref,
                   wgu_scr, wd_scr, xbuf, ybuf, row_scr, gstart_scr, gsem, ssem):
    del xt_ref
    e = pl.program_id(0)

    def src_row(row):
        return pl.multiple_of(jnp.where(row >= 0, row, READ_SPARE), CHUNK_ROWS)

    def dst_row(row, slot, c):
        spare = c * Q_TILE + jnp.where(slot == 0, Q_TILE - CHUNK_ROWS, Q_TILE - 2 * CHUNK_ROWS)
        return pl.multiple_of(jnp.where(row >= 0, row, spare), CHUNK_ROWS)

    def chunk_rows(c):
        return slice(c * CHUNK_ROWS, (c + 1) * CHUNK_ROWS)

    def start_in(g):
        slot = g % 2
        for c in range(CPG):
            row = row_scr[g * CPG + c]
            pltpu.make_async_copy(yt_ref.at[pl.ds(src_row(row), CHUNK_ROWS), :], xbuf.at[slot, chunk_rows(c), :],
                                  gsem.at[slot]).start()

    def start_out(g, slot):
        for c in range(CPG):
            row = row_scr[g * CPG + c]
            pltpu.make_async_copy(ybuf.at[slot, chunk_rows(c), :],
                                  yt_ref.at[pl.ds(dst_row(row, slot, c), CHUNK_ROWS), :], ssem.at[slot]).start()

    def wait_in(slot):
        pltpu.make_async_copy(yt_ref.at[pl.ds(0, E_GROUP), :], xbuf.at[slot], gsem.at[slot]).wait()

    def wait_out(slot):
        pltpu.make_async_copy(ybuf.at[slot], yt_ref.at[pl.ds(0, E_GROUP), :], ssem.at[slot]).wait()

    @pl.when(e == 0)
    def _():
        def per_expert(ee, cnt):
            gstart_scr[ee] = cnt // CPG

            def per_tile(t, cnt):
                first = (t * CH_PER_TILE + seg_ref[t * N_EXPERTS + ee]) * CHUNK_ROWS

                def per_chunk(j, cnt):
                    row_scr[cnt] = first + j * CHUNK_ROWS
                    return cnt + 1

                return lax.fori_loop(0, nch_ref[t * N_EXPERTS + ee], per_chunk, cnt)

            cnt = lax.fori_loop(0, N_TILES, per_tile, cnt)
            padded = (cnt + CPG - 1) // CPG * CPG

            def pad(i, carry):
                row_scr[i] = -1
                return carry

            lax.fori_loop(cnt, padded, pad, 0)
            return padded

        total = lax.fori_loop(0, N_EXPERTS, per_expert, 0)
        gstart_scr[N_EXPERTS] = total // CPG

        def pad(i, carry):
            row_scr[i] = -1
            return carry

        lax.fori_loop(total, total + CPG, pad, 0)
        ybuf[...] = jnp.zeros_like(ybuf)
        start_out(total // CPG, 0)
        start_out(total // CPG, 1)
        start_in(0)

    g_first = gstart_scr[e]
    g_end = gstart_scr[e + 1]

    @pl.when(g_end > g_first)
    def _():
        wgu_scr[...] = wgu_ref[...].astype(BF16)
        wd_scr[...] = wd_ref[...].astype(BF16)

    def group_step(g, carry):
        slot = g % 2
        start_in(g + 1)
        wait_in(slot)
        wait_out(slot)
        hgu = jnp.dot(xbuf[slot], wgu_scr[...], preferred_element_type=F32) + bgu_ref[...]
        h_glu = jnp.minimum(hgu[:, :D_EXPERT], SWIGLU_LIMIT)
        h_lin = jnp.clip(hgu[:, D_EXPERT:], -SWIGLU_LIMIT, SWIGLU_LIMIT)
        act = (h_lin + 1.0) * (h_glu * _sigmoid(SWIGLU_ALPHA * h_glu))
        y = jnp.dot(act.astype(BF16), wd_scr[...], preferred_element_type=F32) + bd_ref[...]
        ybuf[slot] = y.astype(BF16)
        start_out(g, slot)
        return carry

    lax.fori_loop(g_first, g_end, group_step, 0)

    @pl.when(e == N_EXPERTS - 1)
    def _():
        wait_in(gstart_scr[N_EXPERTS] % 2)
        wait_out(0)
        wait_out(1)


def _experts(nch_flat, seg_flat, xt, w_gu, b_gu, w_down, b_down, layer):
    return pl.pallas_call(
        _expert_kernel,
        grid_spec=pltpu.PrefetchScalarGridSpec(
            num_scalar_prefetch=2,
            grid=(N_EXPERTS,),
            in_specs=[pl.BlockSpec((None, None, D_MODEL, 2 * D_EXPERT), lambda e, n, s: (layer, e, 0, 0)),
                      pl.BlockSpec((None, None, 1, 2 * D_EXPERT), lambda e, n, s: (layer, e, 0, 0)),
                      pl.BlockSpec((None, None, D_EXPERT, D_MODEL), lambda e, n, s: (layer, e, 0, 0)),
                      pl.BlockSpec((None, None, 1, D_MODEL), lambda e, n, s: (layer, e, 0, 0)),
                      pl.BlockSpec(memory_space=pl.ANY)],
            out_specs=pl.BlockSpec(memory_space=pl.ANY),
            scratch_shapes=[pltpu.VMEM((D_MODEL, 2 * D_EXPERT), BF16),
                            pltpu.VMEM((D_EXPERT, D_MODEL), BF16),
                            pltpu.VMEM((2, E_GROUP, D_MODEL), BF16),
                            pltpu.VMEM((2, E_GROUP, D_MODEL), BF16),
                            pltpu.SMEM((MAX_CHUNKS,), jnp.int32),
                            pltpu.SMEM((N_EXPERTS + 1,), jnp.int32),
                            pltpu.SemaphoreType.DMA((2,)),
                            pltpu.SemaphoreType.DMA((2,))]),
        out_shape=jax.ShapeDtypeStruct((N_TILES * Q_TILE, D_MODEL), BF16),
        input_output_aliases={6: 0},
        compiler_params=_cparams(("arbitrary",)),
        name="moe_experts",
    )(nch_flat, seg_flat, w_gu, b_gu, w_down, b_down, xt)


MOE_CTX_TILES = T_CTX // TM_MOE


def _combine_rows(x_ref, mod_ref, pos_ref, gate_ref, yt_ref):
    lane = lax.broadcasted_iota(jnp.int32, (TM_MOE, Q_TILE), 1)
    sel = jnp.zeros((TM_MOE, Q_TILE), F32)
    for k in range(TOP_K):
        sel = jnp.where(lane == pos_ref[:, k:k + 1], gate_ref[:, k:k + 1], sel)
    acc = jnp.dot(sel.astype(BF16), yt_ref[...], preferred_element_type=F32)
    return x_ref[...] + mod_ref[5:6, :] * acc


def _combine_kernel(x_ref, mod_ref, pos_ref, gate_ref, yt_ref, o_ref):
    o_ref[...] = _combine_rows(x_ref, mod_ref, pos_ref, gate_ref, yt_ref)


def _combine_final_kernel(x_ref, mod_ref, pos_ref, gate_ref, yt_ref, fg_ref, ctx_ref, lat_ref):
    y = _combine_rows(x_ref, mod_ref, pos_ref, gate_ref, yt_ref)
    y = y * lax.rsqrt(jnp.mean(y * y, axis=-1, keepdims=True) + RMS_EPS) * fg_ref[...]
    is_ctx = pl.program_id(0) < MOE_CTX_TILES

    @pl.when(is_ctx)
    def _():
        ctx_ref[...] = y

    @pl.when(jnp.logical_not(is_ctx))
    def _():
        lat_ref[...] = y


def _combine(x, mod, pos_t, gate_t, yt, final_g=None):
    tile = pl.BlockSpec((TM_MOE, D_MODEL), lambda i: (i, 0))
    in_specs = [tile,
                pl.BlockSpec((None, 6, D_MODEL), lambda i: (_seg_of_tile(i, TM_MOE), 0, 0)),
                pl.BlockSpec((TM_MOE, 8), lambda i: (i, 0)),
                pl.BlockSpec((TM_MOE, 8), lambda i: (i, 0)),
                pl.BlockSpec((Q_TILE, D_MODEL), lambda i: (i, 0))]
    if final_g is None:
        return pl.pallas_call(
            _combine_kernel, grid=(N_TILES,), in_specs=in_specs, out_specs=tile,
            out_shape=jax.ShapeDtypeStruct((T_ALL, D_MODEL), F32),
            compiler_params=_cparams(("arbitrary",)), name="moe_combine",
        )(x, mod, pos_t, gate_t, yt)
    return pl.pallas_call(
        _combine_final_kernel, grid=(N_TILES,),
        in_specs=in_specs + [pl.BlockSpec((1, D_MODEL), lambda i: (0, 0))],
        out_specs=[pl.BlockSpec((TM_MOE, D_MODEL), lambda i: (jnp.minimum(i, MOE_CTX_TILES - 1), 0)),
                   pl.BlockSpec((TM_MOE, D_MODEL), lambda i: (jnp.maximum(i - MOE_CTX_TILES, 0), 0))],
        out_shape=[jax.ShapeDtypeStruct((T_CTX, D_MODEL), F32), jax.ShapeDtypeStruct((T_LAT, D_MODEL), F32)],
        compiler_params=_cparams(("arbitrary",)), name="moe_combine_final",
    )(x, mod, pos_t, gate_t, yt, final_g)


def _moe(x, norm_g, mod, w_rt, b_r, w_gu, b_gu, w_down, b_down, layer, final_g=None):
    xt, pos, gate, nch, seg = _route_sort(x, norm_g, mod, w_rt, b_r)
    yt = _experts(nch[:, :, 0].reshape(-1), seg[:, :, 0].reshape(-1), xt, w_gu,
                  b_gu.reshape(DEPTH, N_EXPERTS, 1, 2 * D_EXPERT), w_down,
                  b_down.reshape(DEPTH, N_EXPERTS, 1, D_MODEL), layer)
    return _combine(x, mod, pos.T, gate.T, yt, final_g)


def _split_in_cols(w):
    conv0, mq0, om0 = 3 * D_A, 3 * D_A + 2 * D_CONV, 3 * D_A + 2 * D_CONV + 3 * D_M
    gates_end = GATE_OFF + N_GATE_M
    pad = jnp.zeros(w.shape[:-1] + (N_ZF - (2 * D_CONV + D_M + N_GATE_M),), w.dtype)
    zb = jnp.concatenate([w[..., :conv0], w[..., mq0:om0]], axis=-1)
    zg = w[..., gates_end:]
    zf = jnp.concatenate([w[..., conv0:mq0], w[..., om0:GATE_OFF], w[..., GATE_OFF:gates_end], pad], axis=-1)
    return zb, zg, zf


def kernel(x_prompt, x_sample, cache_k, cache_v, state_C, state_n, state_m, c, c_ctx, norm1_g, w_mod, b_mod, w_in, b_in, rpb, w_dw, b_dw, cln_g, cln_b, mnorm_g, w_pa, w_pc, w_pm, w_out, norm2_g, w_router, b_router, w_gu, b_gu, w_down, b_down, final_g):
    cond = jnp.concatenate([c_ctx[None, :], c, jnp.zeros((SEG_PAD - N_SEG, D_MODEL), F32)], axis=0)
    mod_all = _modulation(cond, w_mod, b_mod).reshape(DEPTH, SEG_PAD, 6, D_MODEL)

    x = jnp.concatenate([x_prompt.reshape(T_CTX, D_MODEL), x_sample.reshape(T_LAT, D_MODEL)], axis=0)
    ck = cache_k.reshape(DEC_BATCH, DEPTH, PAST_LEN, D_A).astype(BF16)
    cv = cache_v.reshape(DEC_BATCH, DEPTH, PAST_LEN, D_A).astype(BF16)
    lat_c0 = state_C.reshape(DEC_BATCH, DEPTH, N_STREAM, HEAD_DIM_M, HEAD_DIM_M)
    lat_n0 = state_n.reshape(DEC_BATCH, DEPTH, N_STREAM, HEAD_DIM_M)
    lat_m0 = jnp.broadcast_to(state_m.reshape(DEC_BATCH, DEPTH, N_STREAM, 1),
                              (DEC_BATCH, DEPTH, N_STREAM, HEAD_DIM_M))
    ctx_c0 = jnp.zeros((1, N_STREAM, HEAD_DIM_M, HEAD_DIM_M), F32)
    ctx_n0 = jnp.zeros((1, N_STREAM, HEAD_DIM_M), F32)
    ctx_m0 = jnp.full((1, N_STREAM, HEAD_DIM_M), -jnp.inf, F32)

    ks, vs, cs, ns, ms = [], [], [], [], []
    for l in range(DEPTH):
        mod = mod_all[l]
        g1 = norm1_g[l][None, :]
        zb, zg, zf, kv = _in_proj(x, g1, mod, _split_in_cols(w_in[l].astype(BF16)),
                                  _split_in_cols(b_in[l][None, :]))
        ya = (_ctx_attention(zb), _natt(zb, ck, cv, _natt_bias(rpb[l]), l))
        conv_w = (w_dw[l], b_dw[l][None, :], cln_g[l][None, :], cln_b[l][None, :])
        yc = (_conv(zf, *conv_w, SEQ, BATCH, 0), _conv(zf, *conv_w, DEC_SEQ, DEC_BATCH, T_CTX // DEC_SEQ))
        ng = mnorm_g[l][None, :]
        ym_ctx, c_l, n_l, m_l = _mlstm(zb, zf, ctx_c0, ctx_n0, ctx_m0, ng, SEQ, BATCH, 0, lambda b: (0,))
        ym_lat, _, _, _ = _mlstm(zb, zf, lat_c0, lat_n0, lat_m0, ng, DEC_SEQ, DEC_BATCH, T_CTX // DEC_SEQ,
                                 lambda b: (b, l))
        x = _merge(x, mod, ya, yc, (ym_ctx, ym_lat), zg, w_pa[l].astype(BF16), w_pc[l].astype(BF16),
                   w_pm[l].astype(BF16), w_out[l].astype(BF16))
        x = _moe(x, norm2_g[l][None, :], mod, w_router[l].T, b_router[l][:, None],
                 w_gu, b_gu, w_down, b_down, l, final_g[None, :] if l == DEPTH - 1 else None)
        ks.append(kv[:, :D_A].reshape(BATCH, SEQ, N_HEADS_A, HEAD_DIM_A))
        vs.append(kv[:, D_A:].reshape(BATCH, SEQ, N_HEADS_A, HEAD_DIM_A))
        cs.append(c_l.reshape(BATCH, 2, N_HEADS_M, HEAD_DIM_M, HEAD_DIM_M))
        ns.append(n_l.reshape(BATCH, 2, N_HEADS_M, HEAD_DIM_M))
        ms.append(m_l[:, :, 0].reshape(BATCH, 2, N_HEADS_M))

    y_ctx, y_lat = x
    return (y_ctx.reshape(BATCH, SEQ, D_MODEL), y_lat.reshape(DEC_BATCH, DEC_SEQ, D_MODEL),
            jnp.stack(ks, axis=1), jnp.stack(vs, axis=1), jnp.stack(cs, axis=1),
            jnp.stack(ns, axis=1), jnp.stack(ms, axis=1))
```

```python
import functools

import numpy as np
import jax
import jax.numpy as jnp
from jax import lax
from jax.experimental import pallas as pl
from jax.experimental.pallas import tpu as pltpu

F32 = jnp.float32
BF16 = jnp.bfloat16
HIGHEST = lax.Precision.HIGHEST

D_MODEL = 1024
BATCH = 16
SEQ = 256
DEPTH = 2
DEC_BATCH = 8
DEC_SEQ = 1024
PAST_LEN = 512
GRID_W = 64
N_HEADS_A = 8
HEAD_DIM_A = 64
D_A = N_HEADS_A * HEAD_DIM_A
WIN_ROWS = 8
WIN_COLS = 16
D_CONV = 512
CONV_WIDTH = 31
N_HEADS_M = 4
HEAD_DIM_M = 128
D_M = N_HEADS_M * HEAD_DIM_M
N_GATE_M = 4 * N_HEADS_M
CHUNK = 64
N_EXPERTS = 32
TOP_K = 4
D_EXPERT = 1024
SWIGLU_ALPHA = 1.702
SWIGLU_LIMIT = 7.0
RMS_EPS = 1e-6
LN_EPS = 1e-5
GATE_OFF = 3 * D_A + 2 * D_CONV + 4 * D_M
N_IN = GATE_OFF + N_GATE_M + 3 * D_MODEL

T_CTX = BATCH * SEQ
T_LAT = DEC_BATCH * DEC_SEQ
T_ALL = T_CTX + T_LAT
N_SEG = 1 + DEC_BATCH
SEG_PAD = 16
GRID_ROWS = DEC_SEQ // GRID_W
NEG = -1e30

N_ZB = 6 * 512
N_ZG = 3 * D_MODEL
N_ZF = 1664
QB_QA, QB_KA, QB_VA, QB_QM, QB_KM, QB_VM = 0, 1, 2, 3, 4, 5
ZG_GA, ZG_GC, ZG_GM = 0, 1, 2
ZF_CU, ZF_CG, ZF_OM = 0, 1, 2
ZF_GATES = 12

TM_TOK = 512
TM_PROJ = 256
TM_MOE = 512
N_TILES = T_ALL // TM_MOE
CHUNK_ROWS = 16
MXU_ROWS = 256
Q_TILE = -(-(TM_MOE * TOP_K + N_EXPERTS * (CHUNK_ROWS - 1)) // MXU_ROWS) * MXU_ROWS
CH_PER_TILE = Q_TILE // CHUNK_ROWS
E_GROUP = 256
CPG = E_GROUP // CHUNK_ROWS
MAX_CHUNKS = ((T_ALL * TOP_K + N_TILES * N_EXPERTS * (CHUNK_ROWS - 1)) // CHUNK_ROWS
              + N_EXPERTS * (CPG - 1)) + CPG
assert Q_TILE - (TM_MOE * TOP_K + N_EXPERTS * (CHUNK_ROWS - 1)) >= 2 * CHUNK_ROWS and N_TILES > CPG
READ_SPARE = N_TILES * Q_TILE - CHUNK_ROWS
VMEM_LIMIT = 60 * 1024 * 1024


def _cparams(sem=None):
    return pltpu.CompilerParams(dimension_semantics=sem, vmem_limit_bytes=VMEM_LIMIT)


def _seg_of_tile(i, tile):
    n_ctx = T_CTX // tile
    per_lat = DEC_SEQ // tile
    return jnp.where(i < n_ctx, 0, 1 + (i - n_ctx) // per_lat)


def _dot_nt(a, b):
    return lax.dot_general(a, b, (((1,), (1,)), ((), ())), preferred_element_type=F32)


def _sigmoid(x):
    return 1.0 / (1.0 + jnp.exp(-x))


def _mod_kernel(c_ref, w_ref, b_ref, o_ref):
    c = c_ref[...]
    s = c * _sigmoid(c)
    o_ref[...] = jnp.dot(s, w_ref[...], precision=HIGHEST, preferred_element_type=F32) + b_ref[...]


def _modulation(cond, w_mod, b_mod):
    tn = 1536
    return pl.pallas_call(
        _mod_kernel,
        grid=(DEPTH, 6 * D_MODEL // tn),
        in_specs=[pl.BlockSpec((SEG_PAD, D_MODEL), lambda l, j: (0, 0)),
                  pl.BlockSpec((None, D_MODEL, tn), lambda l, j: (l, 0, j)),
                  pl.BlockSpec((None, 1, tn), lambda l, j: (l, 0, j))],
        out_specs=pl.BlockSpec((None, SEG_PAD, tn), lambda l, j: (l, 0, j)),
        out_shape=jax.ShapeDtypeStruct((DEPTH, SEG_PAD, 6 * D_MODEL), F32),
        compiler_params=_cparams(("arbitrary", "arbitrary")),
        name="modulation",
    )(cond, w_mod, b_mod.reshape(DEPTH, 1, 6 * D_MODEL))


def _normmod(x, g, mod, shift_idx, scale_idx):
    y = x * lax.rsqrt(jnp.mean(x * x, axis=-1, keepdims=True) + RMS_EPS) * g
    return y * (1.0 + mod[scale_idx:scale_idx + 1, :]) + mod[shift_idx:shift_idx + 1, :]


PROJ_CTX_TILES = T_CTX // TM_PROJ


def _in_proj_kernel(x_ref, g_ref, mod_ref, wb_ref, bb_ref, wg_ref, bg_ref, wf_ref, bf_ref,
                    zb_ref, zg_ref, zf_ref, kv_ref):
    h = _normmod(x_ref[...], g_ref[...], mod_ref[...], 0, 1).astype(BF16)
    acc = jnp.dot(h, wb_ref[...], preferred_element_type=F32) + bb_ref[...]
    zb_ref[...] = acc.astype(BF16)

    @pl.when(pl.program_id(0) < PROJ_CTX_TILES)
    def _():
        kv_ref[...] = acc[:, D_A:3 * D_A]

    gates = jnp.dot(h, wg_ref[...], preferred_element_type=F32) + bg_ref[...]
    zg_ref[...] = _sigmoid(gates).astype(BF16)
    zf_ref[...] = jnp.dot(h, wf_ref[...], preferred_element_type=F32) + bf_ref[...]


def _in_proj(x, norm_g, mod, w, b):
    def full(a):
        return pl.BlockSpec(a.shape, lambda i: (0, 0))

    def rows(n):
        return pl.BlockSpec((TM_PROJ, n), lambda i: (i, 0))

    return pl.pallas_call(
        _in_proj_kernel,
        grid=(T_ALL // TM_PROJ,),
        in_specs=[rows(D_MODEL),
                  pl.BlockSpec((1, D_MODEL), lambda i: (0, 0)),
                  pl.BlockSpec((None, 6, D_MODEL), lambda i: (_seg_of_tile(i, TM_PROJ), 0, 0)),
                  full(w[0]), full(b[0]), full(w[1]), full(b[1]), full(w[2]), full(b[2])],
        out_specs=[rows(N_ZB), rows(N_ZG), rows(N_ZF),
                   pl.BlockSpec((TM_PROJ, 2 * D_A), lambda i: (jnp.minimum(i, PROJ_CTX_TILES - 1), 0))],
        out_shape=[jax.ShapeDtypeStruct((T_ALL, N_ZB), BF16), jax.ShapeDtypeStruct((T_ALL, N_ZG), BF16),
                   jax.ShapeDtypeStruct((T_ALL, N_ZF), F32), jax.ShapeDtypeStruct((T_CTX, 2 * D_A), F32)],
        compiler_params=_cparams(("arbitrary",)),
        name="in_proj",
    )(x, norm_g, mod, w[0], b[0], w[1], b[1], w[2], b[2])


HEAD_PAIR = 2 * HEAD_DIM_A
ATT_SCALE = HEAD_DIM_A ** -0.5


def _pair_queries(q2):
    lo = lax.broadcasted_iota(jnp.int32, (1, HEAD_PAIR), 1) < HEAD_DIM_A
    q2 = q2 * ATT_SCALE
    zero = jnp.zeros_like(q2)
    return lo, jnp.concatenate([jnp.where(lo, q2, zero), jnp.where(lo, zero, q2)], axis=0)


def _unpair(lo, o_stacked):
    rows = o_stacked.shape[0] // 2
    return jnp.where(lo, o_stacked[:rows], o_stacked[rows:])


def _ctx_attn_kernel(q_ref, k_ref, v_ref, o_ref):
    for hp in range(N_HEADS_A // 2):
        sl = slice(hp * HEAD_PAIR, (hp + 1) * HEAD_PAIR)
        lo, qs = _pair_queries(q_ref[:, sl])
        s = _dot_nt(qs, k_ref[:, sl])
        p = jnp.exp(s - jnp.max(s, axis=-1, keepdims=True))
        l = jnp.sum(p, axis=-1, keepdims=True)
        o = jnp.dot(p.astype(BF16), v_ref[:, sl], preferred_element_type=F32) / l
        o_ref[:, sl] = _unpair(lo, o).astype(o_ref.dtype)


def _ctx_attention(zb):
    def spec(cb):
        return pl.BlockSpec((SEQ, D_A), lambda b: (b, cb))

    return pl.pallas_call(
        _ctx_attn_kernel,
        grid=(BATCH,),
        in_specs=[spec(QB_QA), spec(QB_KA), spec(QB_VA)],
        out_specs=pl.BlockSpec((SEQ, D_A), lambda b: (b, 0)),
        out_shape=jax.ShapeDtypeStruct((T_CTX, D_A), BF16),
        compiler_params=_cparams(("arbitrary",)),
        name="ctx_attention",
    )(zb, zb, zb)


NQ_ROWS = 4
NW_ROWS = 12
NQ_BLOCKS = GRID_ROWS // NQ_ROWS
NQ_TOK = NQ_ROWS * GRID_W
NW_TOK = NW_ROWS * GRID_W


def _window_row(qb, xp):
    return xp.clip(qb * NQ_ROWS - WIN_ROWS // 2, 0, GRID_ROWS - NW_ROWS)


N_REL_ROWS = 2 * WIN_ROWS - 1


def _natt_rel_rows():
    r = np.arange(NQ_BLOCKS)[:, None, None] * NQ_ROWS + np.arange(NQ_ROWS)[None, :, None]
    krow = _window_row(np.arange(NQ_BLOCKS), np)[:, None, None] + np.arange(NW_ROWS)[None, None, :]
    rs = np.clip(r - WIN_ROWS // 2, 0, GRID_ROWS - WIN_ROWS)
    assert ((rs >= krow[:, :, :1]) & (rs + WIN_ROWS <= krow[:, :, -1:] + 1)).all()
    return np.where((krow >= rs) & (krow < rs + WIN_ROWS), krow - r + WIN_ROWS - 1, N_REL_ROWS)


NATT_REL = _natt_rel_rows()


def _natt_kernel(q_ref, k_ref, v_ref, kc_ref, vc_ref, toe_ref, o_ref, bias_scr):
    qb = pl.program_id(0)

    @pl.when(pl.program_id(1) == 0)
    def _():
        for v in range(NQ_BLOCKS):
            @pl.when(qb == v)
            def _(v=v):
                for h in range(N_HEADS_A):
                    for rq in range(NQ_ROWS):
                        for kr in range(NW_ROWS):
                            half = (kr % 2) * GRID_W
                            bias_scr[h, rq * GRID_W:(rq + 1) * GRID_W, kr * GRID_W:(kr + 1) * GRID_W] = (
                                toe_ref[h, int(NATT_REL[v, rq, kr]), :, half:half + GRID_W])

    start = pl.multiple_of(_window_row(qb, jnp) * GRID_W, NQ_TOK)
    band = NW_TOK
    for hp in range(N_HEADS_A // 2):
        sl = slice(hp * HEAD_PAIR, (hp + 1) * HEAD_PAIR)
        lo, qs = _pair_queries(q_ref[:, sl])
        bias = bias_scr[2 * hp:2 * hp + 2].reshape(2 * NQ_TOK, band)
        s_loc = _dot_nt(qs, k_ref[pl.ds(start, band), sl]) + bias
        s_ctx = _dot_nt(qs, kc_ref[:, sl])
        m = jnp.maximum(jnp.max(s_loc, axis=-1, keepdims=True), jnp.max(s_ctx, axis=-1, keepdims=True))
        p_loc = jnp.exp(s_loc - m)
        p_ctx = jnp.exp(s_ctx - m)
        l = jnp.sum(p_loc, axis=-1, keepdims=True) + jnp.sum(p_ctx, axis=-1, keepdims=True)
        o = (jnp.dot(p_loc.astype(BF16), v_ref[pl.ds(start, band), sl], preferred_element_type=F32)
             + jnp.dot(p_ctx.astype(BF16), vc_ref[:, sl], preferred_element_type=F32))
        o_ref[:, sl] = _unpair(lo, o / l).astype(o_ref.dtype)


def _natt_bias(rpb_l):
    qc = np.arange(GRID_W)
    kc = np.arange(GRID_W)
    cs = np.clip(qc - WIN_COLS // 2, 0, GRID_W - WIN_COLS)
    ok = (kc[None, :] >= cs[:, None]) & (kc[None, :] < cs[:, None] + WIN_COLS)
    dc = np.clip(kc[None, :] - qc[:, None] + WIN_COLS - 1, 0, 2 * WIN_COLS - 2)
    pick = (dc[None] == np.arange(2 * WIN_COLS - 1)[:, None, None]).astype(np.float32)
    toe = jnp.einsum('hdc,cqk->hdqk', rpb_l, jnp.asarray(pick), precision=HIGHEST)
    toe = jnp.where(jnp.asarray(ok)[None, None], toe, NEG)
    toe = jnp.concatenate([toe, jnp.full((N_HEADS_A, 1, GRID_W, GRID_W), NEG, F32)], axis=1)
    return jnp.concatenate([toe, toe], axis=-1)


def _natt(zb, cache_k, cache_v, toe, layer):
    lat0 = T_CTX // DEC_SEQ
    row0 = T_CTX // NQ_TOK
    return pl.pallas_call(
        _natt_kernel,
        grid=(NQ_BLOCKS, DEC_BATCH),
        in_specs=[pl.BlockSpec((NQ_TOK, D_A), lambda qb, b: (row0 + b * NQ_BLOCKS + qb, QB_QA)),
                  pl.BlockSpec((DEC_SEQ, D_A), lambda qb, b: (lat0 + b, QB_KA)),
                  pl.BlockSpec((DEC_SEQ, D_A), lambda qb, b: (lat0 + b, QB_VA)),
                  pl.BlockSpec((None, None, PAST_LEN, D_A), lambda qb, b: (b, layer, 0, 0)),
                  pl.BlockSpec((None, None, PAST_LEN, D_A), lambda qb, b: (b, layer, 0, 0)),
                  pl.BlockSpec((N_HEADS_A, N_REL_ROWS + 1, GRID_W, 2 * GRID_W), lambda qb, b: (0, 0, 0, 0))],
        out_specs=pl.BlockSpec((NQ_TOK, D_A), lambda qb, b: (b * NQ_BLOCKS + qb, 0)),
        out_shape=jax.ShapeDtypeStruct((T_LAT, D_A), BF16),
        scratch_shapes=[pltpu.VMEM((N_HEADS_A, NQ_TOK, NW_TOK), F32)],
        compiler_params=_cparams(("arbitrary", "arbitrary")),
        name="nbr_attention",
    )(zb, zb, zb, cache_k, cache_v, toe)


CONV_HALO = 16
CONV_ROWS = 64


SUBLANES = 8


def _conv_kernel(u_ref, g_ref, w_ref, b_ref, lg_ref, lb_ref, o_ref, pad_scr, sh_scr, *, seq):
    zeros = jnp.zeros((CONV_HALO, D_CONV), F32)
    pad_scr[0:CONV_HALO, :] = zeros
    pad_scr[CONV_HALO + seq:2 * CONV_HALO + seq, :] = zeros
    pad_scr[CONV_HALO:CONV_HALO + seq, :] = u_ref[...] * _sigmoid(g_ref[...])
    n_sh = seq + 2 * CONV_HALO - SUBLANES
    for s in range(SUBLANES):
        sh_scr[s] = pad_scr[s:s + n_sh, :]
    first = CONV_HALO - CONV_WIDTH // 2
    for c in range(seq // CONV_ROWS):
        base = c * CONV_ROWS
        acc = jnp.broadcast_to(b_ref[...], (CONV_ROWS, D_CONV))
        for j in range(CONV_WIDTH):
            q, s = divmod(first + j, SUBLANES)
            row0 = base + q * SUBLANES
            acc = acc + sh_scr[s, row0:row0 + CONV_ROWS, :] * w_ref[j:j + 1, :]
        mu = jnp.mean(acc, axis=-1, keepdims=True)
        xc = acc - mu
        var = jnp.mean(xc * xc, axis=-1, keepdims=True)
        y = xc * lax.rsqrt(var + LN_EPS) * lg_ref[...] + lb_ref[...]
        o_ref[base:base + CONV_ROWS, :] = (y * _sigmoid(y)).astype(o_ref.dtype)


def _conv(z, w_dw, b_dw, ln_g, ln_b, seq, n_seq, row_block0):
    def vec():
        return pl.BlockSpec((1, D_CONV), lambda b: (0, 0))

    return pl.pallas_call(
        functools.partial(_conv_kernel, seq=seq),
        grid=(n_seq,),
        in_specs=[pl.BlockSpec((seq, D_CONV), lambda b: (row_block0 + b, ZF_CU)),
                  pl.BlockSpec((seq, D_CONV), lambda b: (row_block0 + b, ZF_CG)),
                  pl.BlockSpec((CONV_WIDTH, D_CONV), lambda b: (0, 0)),
                  vec(), vec(), vec()],
        out_specs=pl.BlockSpec((seq, D_CONV), lambda b: (b, 0)),
        out_shape=jax.ShapeDtypeStruct((n_seq * seq, D_CONV), BF16),
        scratch_shapes=[pltpu.VMEM((seq + 2 * CONV_HALO, D_CONV), F32),
                        pltpu.VMEM((SUBLANES, seq + 2 * CONV_HALO - SUBLANES, D_CONV), F32)],
        compiler_params=_cparams(("arbitrary",)),
        name="conformer_conv",
    )(z, z, w_dw, b_dw, ln_g, ln_b)


N_STREAM = 2 * N_HEADS_M
MCHUNK = 128


PAIR_M = 2 * HEAD_DIM_M
N_PAIR = N_STREAM // 2


def _per_head_rows(r):
    return jnp.concatenate([jnp.broadcast_to(r[0:1], (HEAD_DIM_M, r.shape[1])),
                            jnp.broadcast_to(r[1:2], (HEAD_DIM_M, r.shape[1]))], axis=0)


def _mlstm_kernel(q_ref, k_ref, v_ref, om_ref, gt_ref, c0_ref, n0_ref, m0_ref, ng_ref,
                  y_ref, c_out, n_out, m_out, hf_scr, hb_scr, c_scr, n_scr, m_scr, cbd_scr, vbd_scr,
                  rows_scr, acol_scr, *, seq):
    nc = seq // MCHUNK
    c_scr[...] = c0_ref[...]
    n_scr[...] = n0_ref[...]
    m_scr[...] = m0_ref[...]
    cbd_scr[...] = jnp.zeros_like(cbd_scr)
    vbd_scr[...] = jnp.zeros_like(vbd_scr)
    for s_id in range(N_STREAM):
        blk = slice((s_id % 2) * HEAD_DIM_M, (s_id % 2 + 1) * HEAD_DIM_M)
        cbd_scr[s_id // 2, blk, blk] = c0_ref[s_id].astype(BF16)
    first_head = lax.broadcasted_iota(jnp.int32, (1, PAIR_M), 1) < HEAD_DIM_M
    rows = lax.broadcasted_iota(jnp.int32, (MCHUNK, MCHUNK), 0)
    cols = lax.broadcasted_iota(jnp.int32, (MCHUNK, MCHUNK), 1)
    tris = ((cols <= rows).astype(F32), (cols >= rows).astype(F32))
    reach = (rows <= cols, rows >= cols)
    lane = lax.broadcasted_iota(jnp.int32, (1, MCHUNK), 1)
    kscale = HEAD_DIM_M ** -0.5

    for cc in range(nc):
        g = gt_ref[cc * MCHUNK:(cc + 1) * MCHUNK, :]
        lf = jnp.minimum(g, 0.0) - jnp.log(1.0 + jnp.exp(-jnp.abs(g)))
        g_t = g.T
        for d in range(2):
            cum = jnp.dot(tris[d], lf, precision=HIGHEST, preferred_element_type=F32)
            cum_t = cum.T
            i0 = 2 * N_HEADS_M * d
            i_rows = g_t[i0:i0 + N_HEADS_M, :]
            b_rows = cum_t[i0 + N_HEADS_M:i0 + 2 * N_HEADS_M, :]
            reach_max = i_rows - b_rows
            for step in (1, 2, 4, 8, 16, 32, 64):
                if d == 0:
                    shifted = jnp.where(lane >= step, pltpu.roll(reach_max, step, axis=1), NEG)
                else:
                    shifted = jnp.where(lane < MCHUNK - step, pltpu.roll(reach_max, MCHUNK - step, axis=1), NEG)
                reach_max = jnp.maximum(reach_max, shifted)
            rows_scr[2 * cc + d, 0] = i_rows
            rows_scr[2 * cc + d, 1] = b_rows
            rows_scr[2 * cc + d, 2] = reach_max
            acol_scr[2 * cc + d] = g - pltpu.roll(cum, MCHUNK - N_HEADS_M, axis=1)

    def chunk_step(c, carry):
        for d in range(2):
            cidx = c if d == 0 else nc - 1 - c
            off = pl.multiple_of(cidx * MCHUNK, MCHUNK)
            last = MCHUNK - 1 if d == 0 else 0
            i0 = 2 * N_HEADS_M * d
            heads = slice(N_HEADS_M * d, N_HEADS_M * (d + 1))
            i_rows = rows_scr[2 * cidx + d, 0]
            b_rows = rows_scr[2 * cidx + d, 1]
            reach_max = rows_scr[2 * cidx + d, 2]
            a_cols = acol_scr[2 * cidx + d]
            m_prevs = m_scr[heads, :]
            inters = b_rows + m_prevs
            m_ts = jnp.maximum(inters, b_rows + reach_max)
            w_inters = jnp.exp(inters - m_ts)
            floors = jnp.exp(-m_ts)
            b_lasts = b_rows[:, last:last + 1]
            m_news = m_ts[:, last:last + 1]
            w_prevs = jnp.exp(b_lasts + m_prevs[:, 0:1] - m_news)
            w_srcs = kscale * jnp.exp(b_lasts - b_rows + i_rows - m_news)
            m_scr[heads, :] = jnp.broadcast_to(m_news, (N_HEADS_M, HEAD_DIM_M))
            for hp in range(N_HEADS_M // 2):
                pid = (N_HEADS_M // 2) * d + hp
                hh = slice(2 * hp, 2 * hp + 2)
                ps = slice(hp * PAIR_M, (hp + 1) * PAIR_M)
                q2 = q_ref[pl.ds(off, MCHUNK), ps]
                k2 = k_ref[pl.ds(off, MCHUNK), ps]
                v2_t = v_ref[pl.ds(off, MCHUNK), ps].astype(F32).T
                k_zero = jnp.zeros_like(k2)
                k_stack = jnp.concatenate([jnp.where(first_head, k2, k_zero),
                                           jnp.where(first_head, k_zero, k2)], axis=0)
                decay = jnp.concatenate(
                    [jnp.exp(jnp.where(reach[d], b_rows[h:h + 1] + a_cols[:, i0 + h:i0 + h + 1], NEG)
                             - m_ts[h:h + 1]) for h in (2 * hp, 2 * hp + 1)], axis=0)
                s_t = _dot_nt(k_stack, q2) * (kscale * decay)
                col_sums = jnp.concatenate([jnp.sum(s_t[:MCHUNK], axis=0, keepdims=True),
                                            jnp.sum(s_t[MCHUNK:], axis=0, keepdims=True)], axis=0)
                n2 = n_scr[N_HEADS_M * d + 2 * hp:N_HEADS_M * d + 2 * hp + 2, :]
                n_zero = jnp.zeros((1, HEAD_DIM_M), F32)
                n_mat = jnp.concatenate([jnp.concatenate([n2[0:1], n_zero], axis=1),
                                         jnp.concatenate([n_zero, n2[1:2]], axis=1),
                                         jnp.zeros((SUBLANES - 2, PAIR_M), F32)], axis=0)
                n_q = _dot_nt(n_mat.astype(BF16), q2)[0:2, :]
                den = w_inters[hh] * n_q + col_sums
                inv = 1.0 / jnp.maximum(jnp.abs(den), floors[hh])
                vbd_scr[pid, :HEAD_DIM_M, :MCHUNK] = v2_t[:HEAD_DIM_M].astype(BF16)
                vbd_scr[pid, HEAD_DIM_M:, MCHUNK:] = v2_t[HEAD_DIM_M:].astype(BF16)
                num_t = (_per_head_rows(w_inters[hh]) * _dot_nt(cbd_scr[pid], q2)
                         + jnp.dot(vbd_scr[pid], s_t.astype(BF16), preferred_element_type=F32))
                h_t = num_t * _per_head_rows(inv)
                if d == 0:
                    hf_scr[cidx, ps, :] = h_t
                else:
                    hb_scr[cidx, ps, :] = h_t
                upd = jnp.dot((v2_t * _per_head_rows(w_srcs[hh])).astype(BF16), k2,
                              preferred_element_type=F32)
                w_mat = jnp.concatenate([w_srcs[hh], jnp.zeros((SUBLANES - 2, MCHUNK), F32)], axis=0)
                n_upd = jnp.dot(w_mat.astype(BF16), k2, preferred_element_type=F32)
                for j in range(2):
                    h = 2 * hp + j
                    s_id = N_HEADS_M * d + h
                    blk = slice(j * HEAD_DIM_M, (j + 1) * HEAD_DIM_M)
                    c_new = w_prevs[h:h + 1] * c_scr[s_id] + upd[blk, blk]
                    c_scr[s_id] = c_new
                    cbd_scr[pid, blk, blk] = c_new.astype(BF16)
                    n_scr[s_id:s_id + 1, :] = w_prevs[h:h + 1] * n2[j:j + 1] + n_upd[j:j + 1, blk]
        return carry

    lax.fori_loop(0, nc, chunk_step, 0)

    for c in range(nc):
        ts = slice(c * MCHUNK, (c + 1) * MCHUNK)
        for h in range(N_HEADS_M):
            hs = slice(h * HEAD_DIM_M, (h + 1) * HEAD_DIM_M)
            hsum = hf_scr[c, hs, :] + hb_scr[c, hs, :]
            mu = jnp.mean(hsum, axis=0, keepdims=True)
            xc = hsum - mu
            var = jnp.mean(xc * xc, axis=0, keepdims=True)
            hn = (xc * lax.rsqrt(var + LN_EPS)).T
            y_ref[ts, hs] = (_sigmoid(om_ref[ts, hs]) * (hn * ng_ref[:, hs])).astype(y_ref.dtype)
    c_out[...] = c_scr[...]
    n_out[...] = n_scr[...]
    m_out[...] = m_scr[...]


def _mlstm(zb, zf, c0, n0, m0, norm_g, seq, n_seq, row_block0, state_map):
    lead = len(state_map(0))

    def zspec(cb):
        return pl.BlockSpec((seq, D_M), lambda b: (row_block0 + b, cb))

    def sspec(tail):
        return pl.BlockSpec((None,) * lead + tail, lambda b: state_map(b) + (0,) * len(tail))

    return pl.pallas_call(
        functools.partial(_mlstm_kernel, seq=seq),
        grid=(n_seq,),
        in_specs=[zspec(QB_QM), zspec(QB_KM), zspec(QB_VM), zspec(ZF_OM),
                  pl.BlockSpec((seq, 128), lambda b: (row_block0 + b, ZF_GATES)),
                  sspec((N_STREAM, HEAD_DIM_M, HEAD_DIM_M)),
                  sspec((N_STREAM, HEAD_DIM_M)),
                  sspec((N_STREAM, HEAD_DIM_M)),
                  pl.BlockSpec((1, D_M), lambda b: (0, 0))],
        out_specs=[pl.BlockSpec((seq, D_M), lambda b: (b, 0)),
                   pl.BlockSpec((None, N_STREAM, HEAD_DIM_M, HEAD_DIM_M), lambda b: (b, 0, 0, 0)),
                   pl.BlockSpec((None, N_STREAM, HEAD_DIM_M), lambda b: (b, 0, 0)),
                   pl.BlockSpec((None, N_STREAM, HEAD_DIM_M), lambda b: (b, 0, 0))],
        out_shape=[jax.ShapeDtypeStruct((n_seq * seq, D_M), BF16),
                   jax.ShapeDtypeStruct((n_seq, N_STREAM, HEAD_DIM_M, HEAD_DIM_M), F32),
                   jax.ShapeDtypeStruct((n_seq, N_STREAM, HEAD_DIM_M), F32),
                   jax.ShapeDtypeStruct((n_seq, N_STREAM, HEAD_DIM_M), F32)],
        scratch_shapes=[pltpu.VMEM((seq // MCHUNK, D_M, MCHUNK), F32),
                        pltpu.VMEM((seq // MCHUNK, D_M, MCHUNK), F32),
                        pltpu.VMEM((N_STREAM, HEAD_DIM_M, HEAD_DIM_M), F32),
                        pltpu.VMEM((N_STREAM, HEAD_DIM_M), F32),
                        pltpu.VMEM((N_STREAM, HEAD_DIM_M), F32),
                        pltpu.VMEM((N_PAIR, PAIR_M, PAIR_M), BF16),
                        pltpu.VMEM((N_PAIR, PAIR_M, 2 * MCHUNK), BF16),
                        pltpu.VMEM((2 * (seq // MCHUNK), 3, N_HEADS_M, MCHUNK), F32),
                        pltpu.VMEM((2 * (seq // MCHUNK), MCHUNK, 128), F32)],
        compiler_params=_cparams(("arbitrary",)),
        name="mlstm",
    )(zb, zb, zb, zf, zf, c0, n0, m0, norm_g)


N_CTX_TILES = T_CTX // TM_TOK


def _merge_kernel(x_ref, mod_ref, ya_c, ya_l, yc_c, yc_l, ym_c, ym_l, ga_ref, gc_ref, gm_ref,
                  wa_ref, wc_ref, wm_ref, wo_ref, o_ref):
    is_ctx = pl.program_id(0) < N_CTX_TILES

    def branch(y_ctx, y_lat, g_ref, w_ref):
        y = jnp.where(is_ctx, y_ctx[...], y_lat[...])
        return g_ref[...].astype(F32) * jnp.dot(y, w_ref[...], preferred_element_type=F32)

    merged = (branch(ya_c, ya_l, ga_ref, wa_ref) + branch(yc_c, yc_l, gc_ref, wc_ref)
              + branch(ym_c, ym_l, gm_ref, wm_ref))
    mix = jnp.dot(merged.astype(BF16), wo_ref[...], preferred_element_type=F32)
    o_ref[...] = x_ref[...] + mod_ref[2:3, :] * mix


def _merge(x, mod, ya, yc, ym, zg, w_pa, w_pc, w_pm, w_out):
    def rows(width, cb=0):
        return pl.BlockSpec((TM_TOK, width), lambda i: (i, cb))

    def ctx_rows(width):
        return pl.BlockSpec((TM_TOK, width), lambda i: (jnp.minimum(i, N_CTX_TILES - 1), 0))

    def lat_rows(width):
        return pl.BlockSpec((TM_TOK, width), lambda i: (jnp.maximum(i - N_CTX_TILES, 0), 0))

    def full(shape):
        return pl.BlockSpec(shape, lambda i: (0, 0))

    return pl.pallas_call(
        _merge_kernel,
        grid=(T_ALL // TM_TOK,),
        in_specs=[rows(D_MODEL),
                  pl.BlockSpec((None, 6, D_MODEL), lambda i: (_seg_of_tile(i, TM_TOK), 0, 0)),
                  ctx_rows(D_A), lat_rows(D_A), ctx_rows(D_CONV), lat_rows(D_CONV),
                  ctx_rows(D_M), lat_rows(D_M),
                  rows(D_MODEL, ZG_GA), rows(D_MODEL, ZG_GC), rows(D_MODEL, ZG_GM),
                  full((D_A, D_MODEL)), full((D_CONV, D_MODEL)), full((D_M, D_MODEL)),
                  full((D_MODEL, D_MODEL))],
        out_specs=rows(D_MODEL),
        out_shape=jax.ShapeDtypeStruct((T_ALL, D_MODEL), F32),
        compiler_params=_cparams(("arbitrary",)),
        name="merge",
    )(x, mod, ya[0], ya[1], yc[0], yc[1], ym[0], ym[1], zg, zg, zg, w_pa, w_pc, w_pm, w_out)


def _route_sort_kernel(x_ref, g_ref, mod_ref, wr_ref, br_ref, xt_ref, pos_ref, gate_ref, nch_ref, seg_ref):
    h = _normmod(x_ref[...], g_ref[...], mod_ref[...], 3, 4)
    hb = h.astype(BF16)
    logits = _dot_nt(wr_ref[...].astype(BF16), hb) + br_ref[...]
    e_iota = lax.broadcasted_iota(jnp.int32, (N_EXPERTS, TM_MOE), 0).astype(F32)
    sels, vals = [], []
    l = logits
    for k in range(TOP_K):
        m = jnp.max(l, axis=0, keepdims=True)
        idx = jnp.min(jnp.where(l == m, e_iota, float(N_EXPERTS)), axis=0, keepdims=True)
        sel = e_iota == idx
        vals.append(m)
        sels.append(sel)
        l = jnp.where(sel, -jnp.inf, l)
    exps = [jnp.exp(v - vals[0]) for v in vals]
    tot = exps[0] + exps[1] + exps[2] + exps[3]
    onehot = jnp.zeros((N_EXPERTS, TM_MOE), F32)
    for k in range(TOP_K):
        gate_ref[k:k + 1, :] = exps[k] / tot
        onehot = onehot + sels[k].astype(F32)
    gate_ref[TOP_K:8, :] = jnp.zeros((8 - TOP_K, TM_MOE), F32)

    cnt = jnp.sum(onehot, axis=1, keepdims=True)
    nch = jnp.floor((cnt + (CHUNK_ROWS - 1)) / CHUNK_ROWS)
    ei = lax.broadcasted_iota(jnp.int32, (N_EXPERTS, N_EXPERTS), 0)
    ej = lax.broadcasted_iota(jnp.int32, (N_EXPERTS, N_EXPERTS), 1)
    seg = jnp.dot((ej < ei).astype(F32), jnp.broadcast_to(nch, (N_EXPERTS, 128)), precision=HIGHEST,
                  preferred_element_type=F32)
    nch_ref[...] = jnp.broadcast_to(nch, (N_EXPERTS, 128)).astype(jnp.int32)
    seg_ref[...] = seg.astype(jnp.int32)

    t_src = lax.broadcasted_iota(jnp.int32, (TM_MOE, TM_MOE), 0)
    t_dst = lax.broadcasted_iota(jnp.int32, (TM_MOE, TM_MOE), 1)
    before = (t_src < t_dst).astype(BF16)
    row_of = (seg[:, 0:1] * CHUNK_ROWS
              + jnp.dot(onehot.astype(BF16), before, preferred_element_type=F32))
    q_iota = lax.broadcasted_iota(jnp.int32, (Q_TILE, TM_MOE), 0)
    perm = jnp.zeros((Q_TILE, TM_MOE), F32)
    for k in range(TOP_K):
        q_k = jnp.sum(jnp.where(sels[k], row_of, 0.0), axis=0, keepdims=True).astype(jnp.int32)
        pos_ref[k:k + 1, :] = q_k
        perm = jnp.where(q_iota == q_k, 1.0, perm)
    pos_ref[TOP_K:8, :] = jnp.zeros((8 - TOP_K, TM_MOE), jnp.int32)
    xt_ref[...] = jnp.dot(perm.astype(BF16), hb, preferred_element_type=F32).astype(BF16)


def _route_sort(x, norm_g, mod, w_rt, b_r):
    tspec = pl.BlockSpec((8, TM_MOE), lambda i: (0, i))
    mspec = pl.BlockSpec((None, N_EXPERTS, 128), lambda i: (i, 0, 0))
    meta = jax.ShapeDtypeStruct((N_TILES, N_EXPERTS, 128), jnp.int32)
    return pl.pallas_call(
        _route_sort_kernel,
        grid=(N_TILES,),
        in_specs=[pl.BlockSpec((TM_MOE, D_MODEL), lambda i: (i, 0)),
                  pl.BlockSpec((1, D_MODEL), lambda i: (0, 0)),
                  pl.BlockSpec((None, 6, D_MODEL), lambda i: (_seg_of_tile(i, TM_MOE), 0, 0)),
                  pl.BlockSpec((N_EXPERTS, D_MODEL), lambda i: (0, 0)),
                  pl.BlockSpec((N_EXPERTS, 1), lambda i: (0, 0))],
        out_specs=[pl.BlockSpec((Q_TILE, D_MODEL), lambda i: (i, 0)), tspec, tspec, mspec, mspec],
        out_shape=[jax.ShapeDtypeStruct((N_TILES * Q_TILE, D_MODEL), BF16),
                   jax.ShapeDtypeStruct((8, T_ALL), jnp.int32), jax.ShapeDtypeStruct((8, T_ALL), F32),
                   meta, meta],
        compiler_params=_cparams(("arbitrary",)),
        name="moe_route_sort",
    )(x, norm_g, mod, w_rt, b_r)


def _expert_kernel(nch_ref, seg_ref, wgu_ref, bgu_ref, wd_ref, bd_ref, xt_ref, yt_ref,
                   wgu_scr, wd_scr, xbuf, ybuf, row_scr, gstart_scr, gsem, ssem):
    del xt_ref
    e = pl.program_id(0)

    def src_row(row):
        return pl.multiple_of(jnp.where(row >= 0, row, READ_SPARE), CHUNK_ROWS)

    def dst_row(row, slot, c):
        spare = c * Q_TILE + jnp.where(slot == 0, Q_TILE - CHUNK_ROWS, Q_TILE - 2 * CHUNK_ROWS)
        return pl.multiple_of(jnp.where(row >= 0, row, spare), CHUNK_ROWS)

    def chunk_rows(c):
        return slice(c * CHUNK_ROWS, (c + 1) * CHUNK_ROWS)

    def start_in(g):
        slot = g % 2
        for c in range(CPG):
            row = row_scr[g * CPG + c]
            pltpu.make_async_copy(yt_ref.at[pl.ds(src_row(row), CHUNK_ROWS), :], xbuf.at[slot, chunk_rows(c), :],
                                  gsem.at[slot]).start()

    def start_out(g, slot):
        for c in range(CPG):
            row = row_scr[g * CPG + c]
            pltpu.make_async_copy(ybuf.at[slot, chunk_rows(c), :],
                                  yt_ref.at[pl.ds(dst_row(row, slot, c), CHUNK_ROWS), :], ssem.at[slot]).start()

    def wait_in(slot):
        pltpu.make_async_copy(yt_ref.at[pl.ds(0, E_GROUP), :], xbuf.at[slot], gsem.at[slot]).wait()

    def wait_out(slot):
        pltpu.make_async_copy(ybuf.at[slot], yt_ref.at[pl.ds(0, E_GROUP), :], ssem.at[slot]).wait()

    @pl.when(e == 0)
    def _():
        def per_expert(ee, cnt):
            gstart_scr[ee] = cnt // CPG

            def per_tile(t, cnt):
                first = (t * CH_PER_TILE + seg_ref[t * N_EXPERTS + ee]) * CHUNK_ROWS

                def per_chunk(j, cnt):
                    row_scr[cnt] = first + j * CHUNK_ROWS
                    return cnt + 1

                return lax.fori_loop(0, nch_ref[t * N_EXPERTS + ee], per_chunk, cnt)

            cnt = lax.fori_loop(0, N_TILES, per_tile, cnt)
            padded = (cnt + CPG - 1) // CPG * CPG

            def pad(i, carry):
                row_scr[i] = -1
                return carry

            lax.fori_loop(cnt, padded, pad, 0)
            return padded

        total = lax.fori_loop(0, N_EXPERTS, per_expert, 0)
        gstart_scr[N_EXPERTS] = total // CPG

        def pad(i, carry):
            row_scr[i] = -1
            return carry

        lax.fori_loop(total, total + CPG, pad, 0)
        ybuf[...] = jnp.zeros_like(ybuf)
        start_out(total // CPG, 0)
        start_out(total // CPG, 1)
        start_in(0)

    g_first = gstart_scr[e]
    g_end = gstart_scr[e + 1]

    @pl.when(g_end > g_first)
    def _():
        wgu_scr[...] = wgu_ref[...].astype(BF16)
        wd_scr[...] = wd_ref[...].astype(BF16)

    def group_step(g, carry):
        slot = g % 2
        start_in(g + 1)
        wait_in(slot)
        wait_out(slot)
        hgu = jnp.dot(xbuf[slot], wgu_scr[...], preferred_element_type=F32) + bgu_ref[...]
        h_glu = jnp.minimum(hgu[:, :D_EXPERT], SWIGLU_LIMIT)
        h_lin = jnp.clip(hgu[:, D_EXPERT:], -SWIGLU_LIMIT, SWIGLU_LIMIT)
        act = (h_lin + 1.0) * (h_glu * _sigmoid(SWIGLU_ALPHA * h_glu))
        y = jnp.dot(act.astype(BF16), wd_scr[...], preferred_element_type=F32) + bd_ref[...]
        ybuf[slot] = y.astype(BF16)
        start_out(g, slot)
        return carry

    lax.fori_loop(g_first, g_end, group_step, 0)

    @pl.when(e == N_EXPERTS - 1)
    def _():
        wait_in(gstart_scr[N_EXPERTS] % 2)
        wait_out(0)
        wait_out(1)


def _experts(nch_flat, seg_flat, xt, w_gu, b_gu, w_down, b_down, layer):
    return pl.pallas_call(
        _expert_kernel,
        grid_spec=pltpu.PrefetchScalarGridSpec(
            num_scalar_prefetch=2,
            grid=(N_EXPERTS,),
            in_specs=[pl.BlockSpec((None, None, D_MODEL, 2 * D_EXPERT), lambda e, n, s: (layer, e, 0, 0)),
                      pl.BlockSpec((None, None, 1, 2 * D_EXPERT), lambda e, n, s: (layer, e, 0, 0)),
                      pl.BlockSpec((None, None, D_EXPERT, D_MODEL), lambda e, n, s: (layer, e, 0, 0)),
                      pl.BlockSpec((None, None, 1, D_MODEL), lambda e, n, s: (layer, e, 0, 0)),
                      pl.BlockSpec(memory_space=pl.ANY)],
            out_specs=pl.BlockSpec(memory_space=pl.ANY),
            scratch_shapes=[pltpu.VMEM((D_MODEL, 2 * D_EXPERT), BF16),
                            pltpu.VMEM((D_EXPERT, D_MODEL), BF16),
                            pltpu.VMEM((2, E_GROUP, D_MODEL), BF16),
                            pltpu.VMEM((2, E_GROUP, D_MODEL), BF16),
                            pltpu.SMEM((MAX_CHUNKS,), jnp.int32),
                            pltpu.SMEM((N_EXPERTS + 1,), jnp.int32),
                            pltpu.SemaphoreType.DMA((2,)),
                            pltpu.SemaphoreType.DMA((2,))]),
        out_shape=jax.ShapeDtypeStruct((N_TILES * Q_TILE, D_MODEL), BF16),
        input_output_aliases={6: 0},
        compiler_params=_cparams(("arbitrary",)),
        name="moe_experts",
    )(nch_flat, seg_flat, w_gu, b_gu, w_down, b_down, xt)


MOE_CTX_TILES = T_CTX // TM_MOE


def _combine_rows(x_ref, mod_ref, pos_ref, gate_ref, yt_ref):
    lane = lax.broadcasted_iota(jnp.int32, (TM_MOE, Q_TILE), 1)
    sel = jnp.zeros((TM_MOE, Q_TILE), F32)
    for k in range(TOP_K):
        sel = jnp.where(lane == pos_ref[:, k:k + 1], gate_ref[:, k:k + 1], sel)
    acc = jnp.dot(sel.astype(BF16), yt_ref[...], preferred_element_type=F32)
    return x_ref[...] + mod_ref[5:6, :] * acc


def _combine_kernel(x_ref, mod_ref, pos_ref, gate_ref, yt_ref, o_ref):
    o_ref[...] = _combine_rows(x_ref, mod_ref, pos_ref, gate_ref, yt_ref)


def _combine_final_kernel(x_ref, mod_ref, pos_ref, gate_ref, yt_ref, fg_ref, ctx_ref, lat_ref):
    y = _combine_rows(x_ref, mod_ref, pos_ref, gate_ref, yt_ref)
    y = y * lax.rsqrt(jnp.mean(y * y, axis=-1, keepdims=True) + RMS_EPS) * fg_ref[...]
    is_ctx = pl.program_id(0) < MOE_CTX_TILES

    @pl.when(is_ctx)
    def _():
        ctx_ref[...] = y

    @pl.when(jnp.logical_not(is_ctx))
    def _():
        lat_ref[...] = y


def _combine(x, mod, pos_t, gate_t, yt, final_g=None):
    tile = pl.BlockSpec((TM_MOE, D_MODEL), lambda i: (i, 0))
    in_specs = [tile,
                pl.BlockSpec((None, 6, D_MODEL), lambda i: (_seg_of_tile(i, TM_MOE), 0, 0)),
                pl.BlockSpec((TM_MOE, 8), lambda i: (i, 0)),
                pl.BlockSpec((TM_MOE, 8), lambda i: (i, 0)),
                pl.BlockSpec((Q_TILE, D_MODEL), lambda i: (i, 0))]
    if final_g is None:
        return pl.pallas_call(
            _combine_kernel, grid=(N_TILES,), in_specs=in_specs, out_specs=tile,
            out_shape=jax.ShapeDtypeStruct((T_ALL, D_MODEL), F32),
            compiler_params=_cparams(("arbitrary",)), name="moe_combine",
        )(x, mod, pos_t, gate_t, yt)
    return pl.pallas_call(
        _combine_final_kernel, grid=(N_TILES,),
        in_specs=in_specs + [pl.BlockSpec((1, D_MODEL), lambda i: (0, 0))],
        out_specs=[pl.BlockSpec((TM_MOE, D_MODEL), lambda i: (jnp.minimum(i, MOE_CTX_TILES - 1), 0)),
                   pl.BlockSpec((TM_MOE, D_MODEL), lambda i: (jnp.maximum(i - MOE_CTX_TILES, 0), 0))],
        out_shape=[jax.ShapeDtypeStruct((T_CTX, D_MODEL), F32), jax.ShapeDtypeStruct((T_LAT, D_MODEL), F32)],
        compiler_params=_cparams(("arbitrary",)), name="moe_combine_final",
    )(x, mod, pos_t, gate_t, yt, final_g)


def _moe(x, norm_g, mod, w_rt, b_r, w_gu, b_gu, w_down, b_down, layer, final_g=None):
    xt, pos, gate, nch, seg = _route_sort(x, norm_g, mod, w_rt, b_r)
    yt = _experts(nch[:, :, 0].reshape(-1), seg[:, :, 0].reshape(-1), xt, w_gu,
                  b_gu.reshape(DEPTH, N_EXPERTS, 1, 2 * D_EXPERT), w_down,
                  b_down.reshape(DEPTH, N_EXPERTS, 1, D_MODEL), layer)
    return _combine(x, mod, pos.T, gate.T, yt, final_g)


def _split_in_cols(w):
    conv0, mq0, om0 = 3 * D_A, 3 * D_A + 2 * D_CONV, 3 * D_A + 2 * D_CONV + 3 * D_M
    gates_end = GATE_OFF + N_GATE_M
    pad = jnp.zeros(w.shape[:-1] + (N_ZF - (2 * D_CONV + D_M + N_GATE_M),), w.dtype)
    zb = jnp.concatenate([w[..., :conv0], w[..., mq0:om0]], axis=-1)
    zg = w[..., gates_end:]
    zf = jnp.concatenate([w[..., conv0:mq0], w[..., om0:GATE_OFF], w[..., GATE_OFF:gates_end], pad], axis=-1)
    return zb, zg, zf


def kernel(x_prompt, x_sample, cache_k, cache_v, state_C, state_n, state_m, c, c_ctx, norm1_g, w_mod, b_mod, w_in, b_in, rpb, w_dw, b_dw, cln_g, cln_b, mnorm_g, w_pa, w_pc, w_pm, w_out, norm2_g, w_router, b_router, w_gu, b_gu, w_down, b_down, final_g):
    cond = jnp.concatenate([c_ctx[None, :], c, jnp.zeros((SEG_PAD - N_SEG, D_MODEL), F32)], axis=0)
    mod_all = _modulation(cond, w_mod, b_mod).reshape(DEPTH, SEG_PAD, 6, D_MODEL)

    x = jnp.concatenate([x_prompt.reshape(T_CTX, D_MODEL), x_sample.reshape(T_LAT, D_MODEL)], axis=0)
    ck = cache_k.reshape(DEC_BATCH, DEPTH, PAST_LEN, D_A).astype(BF16)
    cv = cache_v.reshape(DEC_BATCH, DEPTH, PAST_LEN, D_A).astype(BF16)
    lat_c0 = state_C.reshape(DEC_BATCH, DEPTH, N_STREAM, HEAD_DIM_M, HEAD_DIM_M)
    lat_n0 = state_n.reshape(DEC_BATCH, DEPTH, N_STREAM, HEAD_DIM_M)
    lat_m0 = jnp.broadcast_to(state_m.reshape(DEC_BATCH, DEPTH, N_STREAM, 1),
                              (DEC_BATCH, DEPTH, N_STREAM, HEAD_DIM_M))
    ctx_c0 = jnp.zeros((1, N_STREAM, HEAD_DIM_M, HEAD_DIM_M), F32)
    ctx_n0 = jnp.zeros((1, N_STREAM, HEAD_DIM_M), F32)
    ctx_m0 = jnp.full((1, N_STREAM, HEAD_DIM_M), -jnp.inf, F32)

    ks, vs, cs, ns, ms = [], [], [], [], []
    for l in range(DEPTH):
        mod = mod_all[l]
        g1 = norm1_g[l][None, :]
        zb, zg, zf, kv = _in_proj(x, g1, mod, _split_in_cols(w_in[l].astype(BF16)),
                                  _split_in_cols(b_in[l][None, :]))
        ya = (_ctx_attention(zb), _natt(zb, ck, cv, _natt_bias(rpb[l]), l))
        conv_w = (w_dw[l], b_dw[l][None, :], cln_g[l][None, :], cln_b[l][None, :])
        yc = (_conv(zf, *conv_w, SEQ, BATCH, 0), _conv(zf, *conv_w, DEC_SEQ, DEC_BATCH, T_CTX // DEC_SEQ))
        ng = mnorm_g[l][None, :]
        ym_ctx, c_l, n_l, m_l = _mlstm(zb, zf, ctx_c0, ctx_n0, ctx_m0, ng, SEQ, BATCH, 0, lambda b: (0,))
        ym_lat, _, _, _ = _mlstm(zb, zf, lat_c0, lat_n0, lat_m0, ng, DEC_SEQ, DEC_BATCH, T_CTX // DEC_SEQ,
                                 lambda b: (b, l))
        x = _merge(x, mod, ya, yc, (ym_ctx, ym_lat), zg, w_pa[l].astype(BF16), w_pc[l].astype(BF16),
                   w_pm[l].astype(BF16), w_out[l].astype(BF16))
        x = _moe(x, norm2_g[l][None, :], mod, w_router[l].T, b_router[l][:, None],
                 w_gu, b_gu, w_down, b_down, l, final_g[None, :] if l == DEPTH - 1 else None)
        ks.append(kv[:, :D_A].reshape(BATCH, SEQ, N_HEADS_A, HEAD_DIM_A))
        vs.append(kv[:, D_A:].reshape(BATCH, SEQ, N_HEADS_A, HEAD_DIM_A))
        cs.append(c_l.reshape(BATCH, 2, N_HEADS_M, HEAD_DIM_M, HEAD_DIM_M))
        ns.append(n_l.reshape(BATCH, 2, N_HEADS_M, HEAD_DIM_M))
        ms.append(m_l[:, :, 0].reshape(BATCH, 2, N_HEADS_M))

    y_ctx, y_lat = x
    return (y_ctx.reshape(BATCH, SEQ, D_MODEL), y_lat.reshape(DEC_BATCH, DEC_SEQ, D_MODEL),
            jnp.stack(ks, axis=1), jnp.stack(vs, axis=1), jnp.stack(cs, axis=1),
            jnp.stack(ns, axis=1), jnp.stack(ms, axis=1))
```

```python
import functools

import numpy as np
import jax
import jax.numpy as jnp
from jax import lax
from jax.experimental import pallas as pl
from jax.experimental.pallas import tpu as pltpu

F32 = jnp.float32
BF16 = jnp.bfloat16
HIGHEST = lax.Precision.HIGHEST

D_MODEL = 1024
BATCH = 16
SEQ = 256
DEPTH = 2
DEC_BATCH = 8
DEC_SEQ = 1024
PAST_LEN = 512
GRID_W = 64
N_HEADS_A = 8
HEAD_DIM_A = 64
D_A = N_HEADS_A * HEAD_DIM_A
WIN_ROWS = 8
WIN_COLS = 16
D_CONV = 512
CONV_WIDTH = 31
N_HEADS_M = 4
HEAD_DIM_M = 128
D_M = N_HEADS_M * HEAD_DIM_M
N_GATE_M = 4 * N_HEADS_M
N_EXPERTS = 32
TOP_K = 4
D_EXPERT = 1024
SWIGLU_ALPHA = 1.702
SWIGLU_LIMIT = 7.0
RMS_EPS = 1e-6
LN_EPS = 1e-5
GATE_OFF = 3 * D_A + 2 * D_CONV + 4 * D_M

T_CTX = BATCH * SEQ
T_LAT = DEC_BATCH * DEC_SEQ
T_ALL = T_CTX + T_LAT
N_SEG = 1 + DEC_BATCH
SEG_PAD = 16
GRID_ROWS = DEC_SEQ // GRID_W
NEG = -1e30

N_ZB = 6 * 512
N_ZG = 3 * D_MODEL
N_ZF = 1664
QB_QA, QB_KA, QB_VA, QB_QM, QB_KM, QB_VM = 0, 1, 2, 3, 4, 5
ZG_GA, ZG_GC, ZG_GM = 0, 1, 2
ZF_CU, ZF_CG, ZF_OM = 0, 1, 2
ZF_GATES = 12

TM_TOK = 512
TM_PROJ = 512
TM_MOE = 512
N_TILES = T_ALL // TM_MOE
CHUNK_ROWS = 16
MXU_ROWS = 256
Q_TILE = -(-(TM_MOE * TOP_K + N_EXPERTS * (CHUNK_ROWS - 1)) // MXU_ROWS) * MXU_ROWS
CH_PER_TILE = Q_TILE // CHUNK_ROWS
E_GROUP = 256
CPG = E_GROUP // CHUNK_ROWS
MAX_CHUNKS = ((T_ALL * TOP_K + N_TILES * N_EXPERTS * (CHUNK_ROWS - 1)) // CHUNK_ROWS
              + N_EXPERTS * (CPG - 1)) + CPG
assert Q_TILE - (TM_MOE * TOP_K + N_EXPERTS * (CHUNK_ROWS - 1)) >= 2 * CHUNK_ROWS and N_TILES > CPG
READ_SPARE = N_TILES * Q_TILE - CHUNK_ROWS
VMEM_LIMIT = 60 * 1024 * 1024


def _cparams(sem=None):
    return pltpu.CompilerParams(dimension_semantics=sem, vmem_limit_bytes=VMEM_LIMIT)


def _seg_of_tile(i, tile):
    n_ctx = T_CTX // tile
    per_lat = DEC_SEQ // tile
    return jnp.where(i < n_ctx, 0, 1 + (i - n_ctx) // per_lat)


def _dot_nt(a, b):
    return lax.dot_general(a, b, (((1,), (1,)), ((), ())), preferred_element_type=F32)


def _sigmoid(x):
    return 1.0 / (1.0 + jnp.exp(-x))


def _mod_kernel(c_ref, w_ref, b_ref, o_ref):
    c = c_ref[...]
    s = c * _sigmoid(c)
    o_ref[...] = jnp.dot(s, w_ref[...], precision=HIGHEST, preferred_element_type=F32) + b_ref[...]


def _modulation(cond, w_mod, b_mod):
    tn = 1536
    return pl.pallas_call(
        _mod_kernel,
        grid=(DEPTH, 6 * D_MODEL // tn),
        in_specs=[pl.BlockSpec((SEG_PAD, D_MODEL), lambda l, j: (0, 0)),
                  pl.BlockSpec((None, D_MODEL, tn), lambda l, j: (l, 0, j)),
                  pl.BlockSpec((None, 1, tn), lambda l, j: (l, 0, j))],
        out_specs=pl.BlockSpec((None, SEG_PAD, tn), lambda l, j: (l, 0, j)),
        out_shape=jax.ShapeDtypeStruct((DEPTH, SEG_PAD, 6 * D_MODEL), F32),
        compiler_params=_cparams(("arbitrary", "arbitrary")),
        name="modulation",
    )(cond, w_mod, b_mod.reshape(DEPTH, 1, 6 * D_MODEL))


def _normmod(x, g, mod, shift_idx, scale_idx):
    y = x * lax.rsqrt(jnp.mean(x * x, axis=-1, keepdims=True) + RMS_EPS) * g
    return y * (1.0 + mod[scale_idx:scale_idx + 1, :]) + mod[shift_idx:shift_idx + 1, :]


PROJ_CTX_TILES = T_CTX // TM_PROJ


def _in_proj_kernel(x_ref, g_ref, mod_ref, wb_ref, bb_ref, wg_ref, bg_ref, wf_ref, bf_ref,
                    zb_ref, zg_ref, zf_ref, kv_ref):
    h = _normmod(x_ref[...], g_ref[...], mod_ref[...], 0, 1).astype(BF16)
    acc = jnp.dot(h, wb_ref[...], preferred_element_type=F32) + bb_ref[...]
    zb_ref[...] = acc.astype(BF16)

    @pl.when(pl.program_id(0) < PROJ_CTX_TILES)
    def _():
        kv_ref[...] = acc[:, D_A:3 * D_A]

    gates = jnp.dot(h, wg_ref[...], preferred_element_type=F32) + bg_ref[...]
    zg_ref[...] = _sigmoid(gates).astype(BF16)
    zf_ref[...] = jnp.dot(h, wf_ref[...], preferred_element_type=F32) + bf_ref[...]


def _in_proj(x, norm_g, mod, w, b):
    def full(a):
        return pl.BlockSpec(a.shape, lambda i: (0, 0), pipeline_mode=pl.Buffered(1))

    def rows(n):
        return pl.BlockSpec((TM_PROJ, n), lambda i: (i, 0))

    return pl.pallas_call(
        _in_proj_kernel,
        grid=(T_ALL // TM_PROJ,),
        in_specs=[rows(D_MODEL),
                  pl.BlockSpec((1, D_MODEL), lambda i: (0, 0)),
                  pl.BlockSpec((None, 6, D_MODEL), lambda i: (_seg_of_tile(i, TM_PROJ), 0, 0)),
                  full(w[0]), full(b[0]), full(w[1]), full(b[1]), full(w[2]), full(b[2])],
        out_specs=[rows(N_ZB), rows(N_ZG), rows(N_ZF),
                   pl.BlockSpec((TM_PROJ, 2 * D_A), lambda i: (jnp.minimum(i, PROJ_CTX_TILES - 1), 0))],
        out_shape=[jax.ShapeDtypeStruct((T_ALL, N_ZB), BF16), jax.ShapeDtypeStruct((T_ALL, N_ZG), BF16),
                   jax.ShapeDtypeStruct((T_ALL, N_ZF), F32), jax.ShapeDtypeStruct((T_CTX, 2 * D_A), F32)],
        compiler_params=_cparams(("arbitrary",)),
        name="in_proj",
    )(x, norm_g, mod, w[0], b[0], w[1], b[1], w[2], b[2])


HEAD_PAIR = 2 * HEAD_DIM_A
ATT_SCALE = HEAD_DIM_A ** -0.5


def _pair_queries(q2):
    lo = lax.broadcasted_iota(jnp.int32, (1, HEAD_PAIR), 1) < HEAD_DIM_A
    q2 = q2 * ATT_SCALE
    zero = jnp.zeros_like(q2)
    return lo, jnp.concatenate([jnp.where(lo, q2, zero), jnp.where(lo, zero, q2)], axis=0)


def _unpair(lo, o_stacked):
    rows = o_stacked.shape[0] // 2
    return jnp.where(lo, o_stacked[:rows], o_stacked[rows:])


def _ctx_attn_kernel(q_ref, k_ref, v_ref, o_ref):
    for hp in range(N_HEADS_A // 2):
        sl = slice(hp * HEAD_PAIR, (hp + 1) * HEAD_PAIR)
        lo, qs = _pair_queries(q_ref[:, sl])
        s = _dot_nt(qs, k_ref[:, sl])
        p = jnp.exp(s - jnp.max(s, axis=-1, keepdims=True))
        l = jnp.sum(p, axis=-1, keepdims=True)
        o = jnp.dot(p.astype(BF16), v_ref[:, sl], preferred_element_type=F32) / l
        o_ref[:, sl] = _unpair(lo, o).astype(o_ref.dtype)


def _ctx_attention(zb):
    def spec(cb):
        return pl.BlockSpec((SEQ, D_A), lambda b: (b, cb))

    return pl.pallas_call(
        _ctx_attn_kernel,
        grid=(BATCH,),
        in_specs=[spec(QB_QA), spec(QB_KA), spec(QB_VA)],
        out_specs=pl.BlockSpec((SEQ, D_A), lambda b: (b, 0)),
        out_shape=jax.ShapeDtypeStruct((T_CTX, D_A), BF16),
        compiler_params=_cparams(("arbitrary",)),
        name="ctx_attention",
    )(zb, zb, zb)


NQ_ROWS = 4
NW_ROWS = 12
NQ_BLOCKS = GRID_ROWS // NQ_ROWS
NQ_TOK = NQ_ROWS * GRID_W
NW_TOK = NW_ROWS * GRID_W


def _window_row(qb, xp):
    return xp.clip(qb * NQ_ROWS - WIN_ROWS // 2, 0, GRID_ROWS - NW_ROWS)


N_REL_ROWS = 2 * WIN_ROWS - 1


def _natt_rel_rows():
    r = np.arange(NQ_BLOCKS)[:, None, None] * NQ_ROWS + np.arange(NQ_ROWS)[None, :, None]
    krow = _window_row(np.arange(NQ_BLOCKS), np)[:, None, None] + np.arange(NW_ROWS)[None, None, :]
    rs = np.clip(r - WIN_ROWS // 2, 0, GRID_ROWS - WIN_ROWS)
    assert ((rs >= krow[:, :, :1]) & (rs + WIN_ROWS <= krow[:, :, -1:] + 1)).all()
    return np.where((krow >= rs) & (krow < rs + WIN_ROWS), krow - r + WIN_ROWS - 1, N_REL_ROWS)


NATT_REL = _natt_rel_rows()


def _natt_kernel(q_ref, k_ref, v_ref, kc_ref, vc_ref, toe_ref, o_ref, bias_scr):
    qb = pl.program_id(0)

    @pl.when(pl.program_id(1) == 0)
    def _():
        for v in range(NQ_BLOCKS):
            @pl.when(qb == v)
            def _(v=v):
                for h in range(N_HEADS_A):
                    for rq in range(NQ_ROWS):
                        for kr in range(NW_ROWS):
                            half = (kr % 2) * GRID_W
                            bias_scr[h, rq * GRID_W:(rq + 1) * GRID_W, kr * GRID_W:(kr + 1) * GRID_W] = (
                                toe_ref[h, int(NATT_REL[v, rq, kr]), :, half:half + GRID_W])

    start = pl.multiple_of(_window_row(qb, jnp) * GRID_W, NQ_TOK)
    band = NW_TOK
    for hp in range(N_HEADS_A // 2):
        sl = slice(hp * HEAD_PAIR, (hp + 1) * HEAD_PAIR)
        lo, qs = _pair_queries(q_ref[:, sl])
        bias = bias_scr[2 * hp:2 * hp + 2].reshape(2 * NQ_TOK, band)
        s_loc = _dot_nt(qs, k_ref[pl.ds(start, band), sl]) + bias
        s_ctx = _dot_nt(qs, kc_ref[:, sl])
        m = jnp.maximum(jnp.max(s_loc, axis=-1, keepdims=True), jnp.max(s_ctx, axis=-1, keepdims=True))
        p_loc = jnp.exp(s_loc - m)
        p_ctx = jnp.exp(s_ctx - m)
        l = jnp.sum(p_loc, axis=-1, keepdims=True) + jnp.sum(p_ctx, axis=-1, keepdims=True)
        o = (jnp.dot(p_loc.astype(BF16), v_ref[pl.ds(start, band), sl], preferred_element_type=F32)
             + jnp.dot(p_ctx.astype(BF16), vc_ref[:, sl], preferred_element_type=F32))
        o_ref[:, sl] = _unpair(lo, o / l).astype(o_ref.dtype)


def _natt_bias(rpb_l):
    qc = np.arange(GRID_W)
    kc = np.arange(GRID_W)
    cs = np.clip(qc - WIN_COLS // 2, 0, GRID_W - WIN_COLS)
    ok = (kc[None, :] >= cs[:, None]) & (kc[None, :] < cs[:, None] + WIN_COLS)
    dc = np.clip(kc[None, :] - qc[:, None] + WIN_COLS - 1, 0, 2 * WIN_COLS - 2)
    pick = (dc[None] == np.arange(2 * WIN_COLS - 1)[:, None, None]).astype(np.float32)
    toe = jnp.einsum('hdc,cqk->hdqk', rpb_l, jnp.asarray(pick), precision=HIGHEST)
    toe = jnp.where(jnp.asarray(ok)[None, None], toe, NEG)
    toe = jnp.concatenate([toe, jnp.full((N_HEADS_A, 1, GRID_W, GRID_W), NEG, F32)], axis=1)
    return jnp.concatenate([toe, toe], axis=-1)


def _natt(zb, cache_k, cache_v, toe, layer):
    lat0 = T_CTX // DEC_SEQ
    row0 = T_CTX // NQ_TOK
    return pl.pallas_call(
        _natt_kernel,
        grid=(NQ_BLOCKS, DEC_BATCH),
        in_specs=[pl.BlockSpec((NQ_TOK, D_A), lambda qb, b: (row0 + b * NQ_BLOCKS + qb, QB_QA)),
                  pl.BlockSpec((DEC_SEQ, D_A), lambda qb, b: (lat0 + b, QB_KA)),
                  pl.BlockSpec((DEC_SEQ, D_A), lambda qb, b: (lat0 + b, QB_VA)),
                  pl.BlockSpec((None, None, PAST_LEN, D_A), lambda qb, b: (b, layer, 0, 0)),
                  pl.BlockSpec((None, None, PAST_LEN, D_A), lambda qb, b: (b, layer, 0, 0)),
                  pl.BlockSpec((N_HEADS_A, N_REL_ROWS + 1, GRID_W, 2 * GRID_W), lambda qb, b: (0, 0, 0, 0))],
        out_specs=pl.BlockSpec((NQ_TOK, D_A), lambda qb, b: (b * NQ_BLOCKS + qb, 0)),
        out_shape=jax.ShapeDtypeStruct((T_LAT, D_A), BF16),
        scratch_shapes=[pltpu.VMEM((N_HEADS_A, NQ_TOK, NW_TOK), F32)],
        compiler_params=_cparams(("arbitrary", "arbitrary")),
        name="nbr_attention",
    )(zb, zb, zb, cache_k, cache_v, toe)


CONV_HALO = 16
CONV_ROWS = 64


SUBLANES = 8


def _conv_kernel(u_ref, g_ref, w_ref, b_ref, lg_ref, lb_ref, o_ref, pad_scr, sh_scr, *, seq):
    zeros = jnp.zeros((CONV_HALO, D_CONV), F32)
    pad_scr[0:CONV_HALO, :] = zeros
    pad_scr[CONV_HALO + seq:2 * CONV_HALO + seq, :] = zeros
    pad_scr[CONV_HALO:CONV_HALO + seq, :] = u_ref[...] * _sigmoid(g_ref[...])
    n_sh = seq + 2 * CONV_HALO - SUBLANES
    for s in range(SUBLANES):
        sh_scr[s] = pad_scr[s:s + n_sh, :]
    first = CONV_HALO - CONV_WIDTH // 2
    for c in range(seq // CONV_ROWS):
        base = c * CONV_ROWS
        acc = jnp.broadcast_to(b_ref[...], (CONV_ROWS, D_CONV))
        for j in range(CONV_WIDTH):
            q, s = divmod(first + j, SUBLANES)
            row0 = base + q * SUBLANES
            acc = acc + sh_scr[s, row0:row0 + CONV_ROWS, :] * w_ref[j:j + 1, :]
        mu = jnp.mean(acc, axis=-1, keepdims=True)
        xc = acc - mu
        var = jnp.mean(xc * xc, axis=-1, keepdims=True)
        y = xc * lax.rsqrt(var + LN_EPS) * lg_ref[...] + lb_ref[...]
        o_ref[base:base + CONV_ROWS, :] = (y * _sigmoid(y)).astype(o_ref.dtype)


def _conv(z, w_dw, b_dw, ln_g, ln_b, seq, n_seq, row_block0):
    def vec():
        return pl.BlockSpec((1, D_CONV), lambda b: (0, 0))

    return pl.pallas_call(
        functools.partial(_conv_kernel, seq=seq),
        grid=(n_seq,),
        in_specs=[pl.BlockSpec((seq, D_CONV), lambda b: (row_block0 + b, ZF_CU)),
                  pl.BlockSpec((seq, D_CONV), lambda b: (row_block0 + b, ZF_CG)),
                  pl.BlockSpec((CONV_WIDTH, D_CONV), lambda b: (0, 0)),
                  vec(), vec(), vec()],
        out_specs=pl.BlockSpec((seq, D_CONV), lambda b: (b, 0)),
        out_shape=jax.ShapeDtypeStruct((n_seq * seq, D_CONV), BF16),
        scratch_shapes=[pltpu.VMEM((seq + 2 * CONV_HALO, D_CONV), F32),
                        pltpu.VMEM((SUBLANES, seq + 2 * CONV_HALO - SUBLANES, D_CONV), F32)],
        compiler_params=_cparams(("arbitrary",)),
        name="conformer_conv",
    )(z, z, w_dw, b_dw, ln_g, ln_b)


N_STREAM = 2 * N_HEADS_M
MCHUNK = 128


PAIR_M = 2 * HEAD_DIM_M
N_PAIR = N_STREAM // 2


def _per_head_rows(r):
    return jnp.concatenate([jnp.broadcast_to(r[0:1], (HEAD_DIM_M, r.shape[1])),
                            jnp.broadcast_to(r[1:2], (HEAD_DIM_M, r.shape[1]))], axis=0)


def _mlstm_kernel(q_ref, k_ref, v_ref, om_ref, gt_ref, c0_ref, n0_ref, m0_ref, ng_ref,
                  y_ref, c_out, n_out, m_out, hf_scr, hb_scr, c_scr, n_scr, m_scr, cbd_scr, vbd_scr,
                  rows_scr, acol_scr, *, seq):
    nc = seq // MCHUNK
    c_scr[...] = c0_ref[...]
    n_scr[...] = n0_ref[...]
    m_scr[...] = m0_ref[...]
    cbd_scr[...] = jnp.zeros_like(cbd_scr)
    vbd_scr[...] = jnp.zeros_like(vbd_scr)
    for s_id in range(N_STREAM):
        blk = slice((s_id % 2) * HEAD_DIM_M, (s_id % 2 + 1) * HEAD_DIM_M)
        cbd_scr[s_id // 2, blk, blk] = c0_ref[s_id].astype(BF16)
    first_head = lax.broadcasted_iota(jnp.int32, (1, PAIR_M), 1) < HEAD_DIM_M
    rows = lax.broadcasted_iota(jnp.int32, (MCHUNK, MCHUNK), 0)
    cols = lax.broadcasted_iota(jnp.int32, (MCHUNK, MCHUNK), 1)
    tris = ((cols <= rows).astype(F32), (cols >= rows).astype(F32))
    reach = (rows <= cols, rows >= cols)
    lane = lax.broadcasted_iota(jnp.int32, (1, MCHUNK), 1)
    kscale = HEAD_DIM_M ** -0.5

    for cc in range(nc):
        g = gt_ref[cc * MCHUNK:(cc + 1) * MCHUNK, :]
        lf = jnp.minimum(g, 0.0) - jnp.log(1.0 + jnp.exp(-jnp.abs(g)))
        g_t = g.T
        for d in range(2):
            cum = jnp.dot(tris[d], lf, precision=HIGHEST, preferred_element_type=F32)
            cum_t = cum.T
            i0 = 2 * N_HEADS_M * d
            i_rows = g_t[i0:i0 + N_HEADS_M, :]
            b_rows = cum_t[i0 + N_HEADS_M:i0 + 2 * N_HEADS_M, :]
            reach_max = i_rows - b_rows
            for step in (1, 2, 4, 8, 16, 32, 64):
                if d == 0:
                    shifted = jnp.where(lane >= step, pltpu.roll(reach_max, step, axis=1), NEG)
                else:
                    shifted = jnp.where(lane < MCHUNK - step, pltpu.roll(reach_max, MCHUNK - step, axis=1), NEG)
                reach_max = jnp.maximum(reach_max, shifted)
            rows_scr[2 * cc + d, 0] = i_rows
            rows_scr[2 * cc + d, 1] = b_rows
            rows_scr[2 * cc + d, 2] = reach_max
            acol_scr[2 * cc + d] = g - pltpu.roll(cum, MCHUNK - N_HEADS_M, axis=1)

    def chunk_step(c, carry):
        for d in range(2):
            cidx = c if d == 0 else nc - 1 - c
            off = pl.multiple_of(cidx * MCHUNK, MCHUNK)
            last = MCHUNK - 1 if d == 0 else 0
            i0 = 2 * N_HEADS_M * d
            heads = slice(N_HEADS_M * d, N_HEADS_M * (d + 1))
            i_rows = rows_scr[2 * cidx + d, 0]
            b_rows = rows_scr[2 * cidx + d, 1]
            reach_max = rows_scr[2 * cidx + d, 2]
            a_cols = acol_scr[2 * cidx + d]
            m_prevs = m_scr[heads, :]
            inters = b_rows + m_prevs
            m_ts = jnp.maximum(inters, b_rows + reach_max)
            w_inters = jnp.exp(inters - m_ts)
            floors = jnp.exp(-m_ts)
            b_lasts = b_rows[:, last:last + 1]
            m_news = m_ts[:, last:last + 1]
            w_prevs = jnp.exp(b_lasts + m_prevs[:, 0:1] - m_news)
            w_srcs = kscale * jnp.exp(b_lasts - b_rows + i_rows - m_news)
            m_scr[heads, :] = jnp.broadcast_to(m_news, (N_HEADS_M, HEAD_DIM_M))
            for hp in range(N_HEADS_M // 2):
                pid = (N_HEADS_M // 2) * d + hp
                hh = slice(2 * hp, 2 * hp + 2)
                ps = slice(hp * PAIR_M, (hp + 1) * PAIR_M)
                q2 = q_ref[pl.ds(off, MCHUNK), ps]
                k2 = k_ref[pl.ds(off, MCHUNK), ps]
                v2_t = v_ref[pl.ds(off, MCHUNK), ps].astype(F32).T
                k_zero = jnp.zeros_like(k2)
                k_stack = jnp.concatenate([jnp.where(first_head, k2, k_zero),
                                           jnp.where(first_head, k_zero, k2)], axis=0)
                decay = jnp.concatenate(
                    [jnp.exp(jnp.where(reach[d], b_rows[h:h + 1] + a_cols[:, i0 + h:i0 + h + 1], NEG)
                             - m_ts[h:h + 1]) for h in (2 * hp, 2 * hp + 1)], axis=0)
                s_t = _dot_nt(k_stack, q2) * (kscale * decay)
                col_sums = jnp.concatenate([jnp.sum(s_t[:MCHUNK], axis=0, keepdims=True),
                                            jnp.sum(s_t[MCHUNK:], axis=0, keepdims=True)], axis=0)
                n2 = n_scr[N_HEADS_M * d + 2 * hp:N_HEADS_M * d + 2 * hp + 2, :]
                n_zero = jnp.zeros((1, HEAD_DIM_M), F32)
                n_mat = jnp.concatenate([jnp.concatenate([n2[0:1], n_zero], axis=1),
                                         jnp.concatenate([n_zero, n2[1:2]], axis=1),
                                         jnp.zeros((SUBLANES - 2, PAIR_M), F32)], axis=0)
                n_q = _dot_nt(n_mat.astype(BF16), q2)[0:2, :]
                den = w_inters[hh] * n_q + col_sums
                inv = 1.0 / jnp.maximum(jnp.abs(den), floors[hh])
                vbd_scr[pid, :HEAD_DIM_M, :MCHUNK] = v2_t[:HEAD_DIM_M].astype(BF16)
                vbd_scr[pid, HEAD_DIM_M:, MCHUNK:] = v2_t[HEAD_DIM_M:].astype(BF16)
                num_t = (_per_head_rows(w_inters[hh]) * _dot_nt(cbd_scr[pid], q2)
                         + jnp.dot(vbd_scr[pid], s_t.astype(BF16), preferred_element_type=F32))
                h_t = num_t * _per_head_rows(inv)
                if d == 0:
                    hf_scr[cidx, ps, :] = h_t
                else:
                    hb_scr[cidx, ps, :] = h_t
                upd = jnp.dot((v2_t * _per_head_rows(w_srcs[hh])).astype(BF16), k2,
                              preferred_element_type=F32)
                w_mat = jnp.concatenate([w_srcs[hh], jnp.zeros((SUBLANES - 2, MCHUNK), F32)], axis=0)
                n_upd = jnp.dot(w_mat.astype(BF16), k2, preferred_element_type=F32)
                for j in range(2):
                    h = 2 * hp + j
                    s_id = N_HEADS_M * d + h
                    blk = slice(j * HEAD_DIM_M, (j + 1) * HEAD_DIM_M)
                    c_new = w_prevs[h:h + 1] * c_scr[s_id] + upd[blk, blk]
                    c_scr[s_id] = c_new
                    cbd_scr[pid, blk, blk] = c_new.astype(BF16)
                    n_scr[s_id:s_id + 1, :] = w_prevs[h:h + 1] * n2[j:j + 1] + n_upd[j:j + 1, blk]
        return carry

    lax.fori_loop(0, nc, chunk_step, 0)

    for c in range(nc):
        ts = slice(c * MCHUNK, (c + 1) * MCHUNK)
        for h in range(N_HEADS_M):
            hs = slice(h * HEAD_DIM_M, (h + 1) * HEAD_DIM_M)
            hsum = hf_scr[c, hs, :] + hb_scr[c, hs, :]
            mu = jnp.mean(hsum, axis=0, keepdims=True)
            xc = hsum - mu
            var = jnp.mean(xc * xc, axis=0, keepdims=True)
            hn = (xc * lax.rsqrt(var + LN_EPS)).T
            y_ref[ts, hs] = (_sigmoid(om_ref[ts, hs]) * (hn * ng_ref[:, hs])).astype(y_ref.dtype)
    c_out[...] = c_scr[...]
    n_out[...] = n_scr[...]
    m_out[...] = m_scr[...]


def _mlstm(zb, zf, c0, n0, m0, norm_g, seq, n_seq, row_block0, state_map):
    lead = len(state_map(0))

    def zspec(cb):
        return pl.BlockSpec((seq, D_M), lambda b: (row_block0 + b, cb))

    def sspec(tail):
        return pl.BlockSpec((None,) * lead + tail, lambda b: state_map(b) + (0,) * len(tail))

    return pl.pallas_call(
        functools.partial(_mlstm_kernel, seq=seq),
        grid=(n_seq,),
        in_specs=[zspec(QB_QM), zspec(QB_KM), zspec(QB_VM), zspec(ZF_OM),
                  pl.BlockSpec((seq, 128), lambda b: (row_block0 + b, ZF_GATES)),
                  sspec((N_STREAM, HEAD_DIM_M, HEAD_DIM_M)),
                  sspec((N_STREAM, HEAD_DIM_M)),
                  sspec((N_STREAM, HEAD_DIM_M)),
                  pl.BlockSpec((1, D_M), lambda b: (0, 0))],
        out_specs=[pl.BlockSpec((seq, D_M), lambda b: (b, 0)),
                   pl.BlockSpec((None, N_STREAM, HEAD_DIM_M, HEAD_DIM_M), lambda b: (b, 0, 0, 0)),
                   pl.BlockSpec((None, N_STREAM, HEAD_DIM_M), lambda b: (b, 0, 0)),
                   pl.BlockSpec((None, N_STREAM, HEAD_DIM_M), lambda b: (b, 0, 0))],
        out_shape=[jax.ShapeDtypeStruct((n_seq * seq, D_M), BF16),
                   jax.ShapeDtypeStruct((n_seq, N_STREAM, HEAD_DIM_M, HEAD_DIM_M), F32),
                   jax.ShapeDtypeStruct((n_seq, N_STREAM, HEAD_DIM_M), F32),
                   jax.ShapeDtypeStruct((n_seq, N_STREAM, HEAD_DIM_M), F32)],
        scratch_shapes=[pltpu.VMEM((seq // MCHUNK, D_M, MCHUNK), F32),
                        pltpu.VMEM((seq // MCHUNK, D_M, MCHUNK), F32),
                        pltpu.VMEM((N_STREAM, HEAD_DIM_M, HEAD_DIM_M), F32),
                        pltpu.VMEM((N_STREAM, HEAD_DIM_M), F32),
                        pltpu.VMEM((N_STREAM, HEAD_DIM_M), F32),
                        pltpu.VMEM((N_PAIR, PAIR_M, PAIR_M), BF16),
                        pltpu.VMEM((N_PAIR, PAIR_M, 2 * MCHUNK), BF16),
                        pltpu.VMEM((2 * (seq // MCHUNK), 3, N_HEADS_M, MCHUNK), F32),
                        pltpu.VMEM((2 * (seq // MCHUNK), MCHUNK, 128), F32)],
        compiler_params=_cparams(("arbitrary",)),
        name="mlstm",
    )(zb, zb, zb, zf, zf, c0, n0, m0, norm_g)


N_CTX_TILES = T_CTX // TM_TOK


def _merge_kernel(x_ref, mod_ref, ya_c, ya_l, yc_c, yc_l, ym_c, ym_l, ga_ref, gc_ref, gm_ref,
                  wa_ref, wc_ref, wm_ref, wo_ref, o_ref):
    is_ctx = pl.program_id(0) < N_CTX_TILES

    def branch(y_ctx, y_lat, g_ref, w_ref):
        y = jnp.where(is_ctx, y_ctx[...], y_lat[...])
        return g_ref[...].astype(F32) * jnp.dot(y, w_ref[...], preferred_element_type=F32)

    merged = (branch(ya_c, ya_l, ga_ref, wa_ref) + branch(yc_c, yc_l, gc_ref, wc_ref)
              + branch(ym_c, ym_l, gm_ref, wm_ref))
    mix = jnp.dot(merged.astype(BF16), wo_ref[...], preferred_element_type=F32)
    o_ref[...] = x_ref[...] + mod_ref[2:3, :] * mix


def _merge(x, mod, ya, yc, ym, zg, w_pa, w_pc, w_pm, w_out):
    def rows(width, cb=0):
        return pl.BlockSpec((TM_TOK, width), lambda i: (i, cb))

    def ctx_rows(width):
        return pl.BlockSpec((TM_TOK, width), lambda i: (jnp.minimum(i, N_CTX_TILES - 1), 0))

    def lat_rows(width):
        return pl.BlockSpec((TM_TOK, width), lambda i: (jnp.maximum(i - N_CTX_TILES, 0), 0))

    def full(shape):
        return pl.BlockSpec(shape, lambda i: (0, 0))

    return pl.pallas_call(
        _merge_kernel,
        grid=(T_ALL // TM_TOK,),
        in_specs=[rows(D_MODEL),
                  pl.BlockSpec((None, 6, D_MODEL), lambda i: (_seg_of_tile(i, TM_TOK), 0, 0)),
                  ctx_rows(D_A), lat_rows(D_A), ctx_rows(D_CONV), lat_rows(D_CONV),
                  ctx_rows(D_M), lat_rows(D_M),
                  rows(D_MODEL, ZG_GA), rows(D_MODEL, ZG_GC), rows(D_MODEL, ZG_GM),
                  full((D_A, D_MODEL)), full((D_CONV, D_MODEL)), full((D_M, D_MODEL)),
                  full((D_MODEL, D_MODEL))],
        out_specs=rows(D_MODEL),
        out_shape=jax.ShapeDtypeStruct((T_ALL, D_MODEL), F32),
        compiler_params=_cparams(("arbitrary",)),
        name="merge",
    )(x, mod, ya[0], ya[1], yc[0], yc[1], ym[0], ym[1], zg, zg, zg, w_pa, w_pc, w_pm, w_out)


def _route_sort_kernel(x_ref, g_ref, mod_ref, wr_ref, br_ref, xt_ref, pos_ref, gate_ref, nch_ref, seg_ref):
    h = _normmod(x_ref[...], g_ref[...], mod_ref[...], 3, 4)
    hb = h.astype(BF16)
    logits = _dot_nt(wr_ref[...].astype(BF16), hb) + br_ref[...]
    e_iota = lax.broadcasted_iota(jnp.int32, (N_EXPERTS, TM_MOE), 0).astype(F32)
    sels, vals = [], []
    l = logits
    for k in range(TOP_K):
        m = jnp.max(l, axis=0, keepdims=True)
        idx = jnp.min(jnp.where(l == m, e_iota, float(N_EXPERTS)), axis=0, keepdims=True)
        sel = e_iota == idx
        vals.append(m)
        sels.append(sel)
        l = jnp.where(sel, -jnp.inf, l)
    exps = [jnp.exp(v - vals[0]) for v in vals]
    tot = exps[0] + exps[1] + exps[2] + exps[3]
    onehot = jnp.zeros((N_EXPERTS, TM_MOE), F32)
    for k in range(TOP_K):
        gate_ref[k:k + 1, :] = exps[k] / tot
        onehot = onehot + sels[k].astype(F32)
    gate_ref[TOP_K:8, :] = jnp.zeros((8 - TOP_K, TM_MOE), F32)

    cnt = jnp.sum(onehot, axis=1, keepdims=True)
    nch = jnp.floor((cnt + (CHUNK_ROWS - 1)) / CHUNK_ROWS)
    ei = lax.broadcasted_iota(jnp.int32, (N_EXPERTS, N_EXPERTS), 0)
    ej = lax.broadcasted_iota(jnp.int32, (N_EXPERTS, N_EXPERTS), 1)
    seg = jnp.dot((ej < ei).astype(F32), jnp.broadcast_to(nch, (N_EXPERTS, 128)), precision=HIGHEST,
                  preferred_element_type=F32)
    nch_ref[...] = jnp.broadcast_to(nch, (N_EXPERTS, 128)).astype(jnp.int32)
    seg_ref[...] = seg.astype(jnp.int32)

    t_src = lax.broadcasted_iota(jnp.int32, (TM_MOE, TM_MOE), 0)
    t_dst = lax.broadcasted_iota(jnp.int32, (TM_MOE, TM_MOE), 1)
    before = (t_src < t_dst).astype(BF16)
    row_of = (seg[:, 0:1] * CHUNK_ROWS
              + jnp.dot(onehot.astype(BF16), before, preferred_element_type=F32))
    q_iota = lax.broadcasted_iota(jnp.int32, (Q_TILE, TM_MOE), 0)
    perm = jnp.zeros((Q_TILE, TM_MOE), F32)
    for k in range(TOP_K):
        q_k = jnp.sum(jnp.where(sels[k], row_of, 0.0), axis=0, keepdims=True).astype(jnp.int32)
        pos_ref[k:k + 1, :] = q_k
        perm = jnp.where(q_iota == q_k, 1.0, perm)
    pos_ref[TOP_K:8, :] = jnp.zeros((8 - TOP_K, TM_MOE), jnp.int32)
    xt_ref[...] = jnp.dot(perm.astype(BF16), hb, preferred_element_type=F32).astype(BF16)


def _route_sort(x, norm_g, mod, w_rt, b_r):
    tspec = pl.BlockSpec((8, TM_MOE), lambda i: (0, i))
    mspec = pl.BlockSpec((None, N_EXPERTS, 128), lambda i: (i, 0, 0))
    meta = jax.ShapeDtypeStruct((N_TILES, N_EXPERTS, 128), jnp.int32)
    return pl.pallas_call(
        _route_sort_kernel,
        grid=(N_TILES,),
        in_specs=[pl.BlockSpec((TM_MOE, D_MODEL), lambda i: (i, 0)),
                  pl.BlockSpec((1, D_MODEL), lambda i: (0, 0)),
                  pl.BlockSpec((None, 6, D_MODEL), lambda i: (_seg_of_tile(i, TM_MOE), 0, 0)),
                  pl.BlockSpec((N_EXPERTS, D_MODEL), lambda i: (0, 0)),
                  pl.BlockSpec((N_EXPERTS, 1), lambda i: (0, 0))],
        out_specs=[pl.BlockSpec((Q_TILE, D_MODEL), lambda i: (i, 0)), tspec, tspec, mspec, mspec],
        out_shape=[jax.ShapeDtypeStruct((N_TILES * Q_TILE, D_MODEL), BF16),
                   jax.ShapeDtypeStruct((8, T_ALL), jnp.int32), jax.ShapeDtypeStruct((8, T_ALL), F32),
                   meta, meta],
        compiler_params=_cparams(("arbitrary",)),
        name="moe_route_sort",
    )(x, norm_g, mod, w_rt, b_r)


def _expert_kernel(nch_ref, seg_ref, wgu_ref, bgu_ref, wd_ref, bd_ref, xt_ref, yt_ref,
                   wgu_scr, wd_scr, xbuf, ybuf, row_scr, gstart_scr, gsem, ssem):
    del xt_ref
    e = pl.program_id(0)

    def src_row(row):
        return pl.multiple_of(jnp.where(row >= 0, row, READ_SPARE), CHUNK_ROWS)

    def dst_row(row, slot, c):
        spare = c * Q_TILE + jnp.where(slot == 0, Q_TILE - CHUNK_ROWS, Q_TILE - 2 * CHUNK_ROWS)
        return pl.multiple_of(jnp.where(row >= 0, row, spare), CHUNK_ROWS)

    def chunk_rows(c):
        return slice(c * CHUNK_ROWS, (c + 1) * CHUNK_ROWS)

    def start_in(g):
        slot = g % 2
        for c in range(CPG):
            row = row_scr[g * CPG + c]
            pltpu.make_async_copy(yt_ref.at[pl.ds(src_row(row), CHUNK_ROWS), :], xbuf.at[slot, chunk_rows(c), :],
                                  gsem.at[slot]).start()

    def start_out(g, slot):
        for c in range(CPG):
            row = row_scr[g * CPG + c]
            pltpu.make_async_copy(ybuf.at[slot, chunk_rows(c), :],
                                  yt_ref.at[pl.ds(dst_row(row, slot, c), CHUNK_ROWS), :], ssem.at[slot]).start()

    def wait_in(slot):
        pltpu.make_async_copy(yt_ref.at[pl.ds(0, E_GROUP), :], xbuf.at[slot], gsem.at[slot]).wait()

    def wait_out(slot):
        pltpu.make_async_copy(ybuf.at[slot], yt_ref.at[pl.ds(0, E_GROUP), :], ssem.at[slot]).wait()

    @pl.when(e == 0)
    def _():
        def per_expert(ee, cnt):
            gstart_scr[ee] = cnt // CPG

            def per_tile(t, cnt):
                first = (t * CH_PER_TILE + seg_ref[t * N_EXPERTS + ee]) * CHUNK_ROWS

                def per_chunk(j, cnt):
                    row_scr[cnt] = first + j * CHUNK_ROWS
                    return cnt + 1

                return lax.fori_loop(0, nch_ref[t * N_EXPERTS + ee], per_chunk, cnt)

            cnt = lax.fori_loop(0, N_TILES, per_tile, cnt)
            padded = (cnt + CPG - 1) // CPG * CPG

            def pad(i, carry):
                row_scr[i] = -1
                return carry

            lax.fori_loop(cnt, padded, pad, 0)
            return padded

        total = lax.fori_loop(0, N_EXPERTS, per_expert, 0)
        gstart_scr[N_EXPERTS] = total // CPG

        def pad(i, carry):
            row_scr[i] = -1
            return carry

        lax.fori_loop(total, total + CPG, pad, 0)
        ybuf[...] = jnp.zeros_like(ybuf)
        start_out(total // CPG, 0)
        start_out(total // CPG, 1)
        start_in(0)

    g_first = gstart_scr[e]
    g_end = gstart_scr[e + 1]

    @pl.when(g_end > g_first)
    def _():
        wgu_scr[...] = wgu_ref[...].astype(BF16)
        wd_scr[...] = wd_ref[...].astype(BF16)

    def group_step(g, carry):
        slot = g % 2
        start_in(g + 1)
        wait_in(slot)
        wait_out(slot)
        hgu = jnp.dot(xbuf[slot], wgu_scr[...], preferred_element_type=F32) + bgu_ref[...]
        h_glu = jnp.minimum(hgu[:, :D_EXPERT], SWIGLU_LIMIT)
        h_lin = jnp.clip(hgu[:, D_EXPERT:], -SWIGLU_LIMIT, SWIGLU_LIMIT)
        act = (h_lin + 1.0) * (h_glu * _sigmoid(SWIGLU_ALPHA * h_glu))
        y = jnp.dot(act.astype(BF16), wd_scr[...], preferred_element_type=F32) + bd_ref[...]
        ybuf[slot] = y.astype(BF16)
        start_out(g, slot)
        return carry

    lax.fori_loop(g_first, g_end, group_step, 0)

    @pl.when(e == N_EXPERTS - 1)
    def _():
        wait_in(gstart_scr[N_EXPERTS] % 2)
        wait_out(0)
        wait_out(1)


def _experts(nch_flat, seg_flat, xt, w_gu, b_gu, w_down, b_down, layer):
    return pl.pallas_call(
        _expert_kernel,
        grid_spec=pltpu.PrefetchScalarGridSpec(
            num_scalar_prefetch=2,
            grid=(N_EXPERTS,),
            in_specs=[pl.BlockSpec((None, None, D_MODEL, 2 * D_EXPERT), lambda e, n, s: (layer, e, 0, 0)),
                      pl.BlockSpec((None, None, 1, 2 * D_EXPERT), lambda e, n, s: (layer, e, 0, 0)),
                      pl.BlockSpec((None, None, D_EXPERT, D_MODEL), lambda e, n, s: (layer, e, 0, 0)),
                      pl.BlockSpec((None, None, 1, D_MODEL), lambda e, n, s: (layer, e, 0, 0)),
                      pl.BlockSpec(memory_space=pl.ANY)],
            out_specs=pl.BlockSpec(memory_space=pl.ANY),
            scratch_shapes=[pltpu.VMEM((D_MODEL, 2 * D_EXPERT), BF16),
                            pltpu.VMEM((D_EXPERT, D_MODEL), BF16),
                            pltpu.VMEM((2, E_GROUP, D_MODEL), BF16),
                            pltpu.VMEM((2, E_GROUP, D_MODEL), BF16),
                            pltpu.SMEM((MAX_CHUNKS,), jnp.int32),
                            pltpu.SMEM((N_EXPERTS + 1,), jnp.int32),
                            pltpu.SemaphoreType.DMA((2,)),
                            pltpu.SemaphoreType.DMA((2,))]),
        out_shape=jax.ShapeDtypeStruct((N_TILES * Q_TILE, D_MODEL), BF16),
        input_output_aliases={6: 0},
        compiler_params=_cparams(("arbitrary",)),
        name="moe_experts",
    )(nch_flat, seg_flat, w_gu, b_gu, w_down, b_down, xt)


MOE_CTX_TILES = T_CTX // TM_MOE


def _combine_rows(x_ref, mod_ref, pos_ref, gate_ref, yt_ref):
    lane = lax.broadcasted_iota(jnp.int32, (TM_MOE, Q_TILE), 1)
    sel = jnp.zeros((TM_MOE, Q_TILE), F32)
    for k in range(TOP_K):
        sel = jnp.where(lane == pos_ref[:, k:k + 1], gate_ref[:, k:k + 1], sel)
    acc = jnp.dot(sel.astype(BF16), yt_ref[...], preferred_element_type=F32)
    return x_ref[...] + mod_ref[5:6, :] * acc


def _combine_kernel(x_ref, mod_ref, pos_ref, gate_ref, yt_ref, o_ref):
    o_ref[...] = _combine_rows(x_ref, mod_ref, pos_ref, gate_ref, yt_ref)


def _combine_final_kernel(x_ref, mod_ref, pos_ref, gate_ref, yt_ref, fg_ref, ctx_ref, lat_ref):
    y = _combine_rows(x_ref, mod_ref, pos_ref, gate_ref, yt_ref)
    y = y * lax.rsqrt(jnp.mean(y * y, axis=-1, keepdims=True) + RMS_EPS) * fg_ref[...]
    is_ctx = pl.program_id(0) < MOE_CTX_TILES

    @pl.when(is_ctx)
    def _():
        ctx_ref[...] = y

    @pl.when(jnp.logical_not(is_ctx))
    def _():
        lat_ref[...] = y


def _combine(x, mod, pos_t, gate_t, yt, final_g=None):
    tile = pl.BlockSpec((TM_MOE, D_MODEL), lambda i: (i, 0))
    in_specs = [tile,
                pl.BlockSpec((None, 6, D_MODEL), lambda i: (_seg_of_tile(i, TM_MOE), 0, 0)),
                pl.BlockSpec((TM_MOE, 8), lambda i: (i, 0)),
                pl.BlockSpec((TM_MOE, 8), lambda i: (i, 0)),
                pl.BlockSpec((Q_TILE, D_MODEL), lambda i: (i, 0))]
    if final_g is None:
        return pl.pallas_call(
            _combine_kernel, grid=(N_TILES,), in_specs=in_specs, out_specs=tile,
            out_shape=jax.ShapeDtypeStruct((T_ALL, D_MODEL), F32),
            compiler_params=_cparams(("arbitrary",)), name="moe_combine",
        )(x, mod, pos_t, gate_t, yt)
    return pl.pallas_call(
        _combine_final_kernel, grid=(N_TILES,),
        in_specs=in_specs + [pl.BlockSpec((1, D_MODEL), lambda i: (0, 0))],
        out_specs=[pl.BlockSpec((TM_MOE, D_MODEL), lambda i: (jnp.minimum(i, MOE_CTX_TILES - 1), 0)),
                   pl.BlockSpec((TM_MOE, D_MODEL), lambda i: (jnp.maximum(i - MOE_CTX_TILES, 0), 0))],
        out_shape=[jax.ShapeDtypeStruct((T_CTX, D_MODEL), F32), jax.ShapeDtypeStruct((T_LAT, D_MODEL), F32)],
        compiler_params=_cparams(("arbitrary",)), name="moe_combine_final",
    )(x, mod, pos_t, gate_t, yt, final_g)


def _moe(x, norm_g, mod, w_rt, b_r, w_gu, b_gu, w_down, b_down, layer, final_g=None):
    xt, pos, gate, nch, seg = _route_sort(x, norm_g, mod, w_rt, b_r)
    yt = _experts(nch[:, :, 0].reshape(-1), seg[:, :, 0].reshape(-1), xt, w_gu,
                  b_gu.reshape(DEPTH, N_EXPERTS, 1, 2 * D_EXPERT), w_down,
                  b_down.reshape(DEPTH, N_EXPERTS, 1, D_MODEL), layer)
    return _combine(x, mod, pos.T, gate.T, yt, final_g)


def _split_in_cols(w):
    conv0, mq0, om0 = 3 * D_A, 3 * D_A + 2 * D_CONV, 3 * D_A + 2 * D_CONV + 3 * D_M
    gates_end = GATE_OFF + N_GATE_M
    pad = jnp.zeros(w.shape[:-1] + (N_ZF - (2 * D_CONV + D_M + N_GATE_M),), w.dtype)
    zb = jnp.concatenate([w[..., :conv0], w[..., mq0:om0]], axis=-1)
    zg = w[..., gates_end:]
    zf = jnp.concatenate([w[..., conv0:mq0], w[..., om0:GATE_OFF], w[..., GATE_OFF:gates_end], pad], axis=-1)
    return zb, zg, zf


def kernel(x_prompt, x_sample, cache_k, cache_v, state_C, state_n, state_m, c, c_ctx, norm1_g, w_mod, b_mod, w_in, b_in, rpb, w_dw, b_dw, cln_g, cln_b, mnorm_g, w_pa, w_pc, w_pm, w_out, norm2_g, w_router, b_router, w_gu, b_gu, w_down, b_down, final_g):
    cond = jnp.concatenate([c_ctx[None, :], c, jnp.zeros((SEG_PAD - N_SEG, D_MODEL), F32)], axis=0)
    mod_all = _modulation(cond, w_mod, b_mod).reshape(DEPTH, SEG_PAD, 6, D_MODEL)

    x = jnp.concatenate([x_prompt.reshape(T_CTX, D_MODEL), x_sample.reshape(T_LAT, D_MODEL)], axis=0)
    ck = cache_k.reshape(DEC_BATCH, DEPTH, PAST_LEN, D_A).astype(BF16)
    cv = cache_v.reshape(DEC_BATCH, DEPTH, PAST_LEN, D_A).astype(BF16)
    lat_c0 = state_C.reshape(DEC_BATCH, DEPTH, N_STREAM, HEAD_DIM_M, HEAD_DIM_M)
    lat_n0 = state_n.reshape(DEC_BATCH, DEPTH, N_STREAM, HEAD_DIM_M)
    lat_m0 = jnp.broadcast_to(state_m.reshape(DEC_BATCH, DEPTH, N_STREAM, 1),
                              (DEC_BATCH, DEPTH, N_STREAM, HEAD_DIM_M))
    ctx_c0 = jnp.zeros((1, N_STREAM, HEAD_DIM_M, HEAD_DIM_M), F32)
    ctx_n0 = jnp.zeros((1, N_STREAM, HEAD_DIM_M), F32)
    ctx_m0 = jnp.full((1, N_STREAM, HEAD_DIM_M), -jnp.inf, F32)

    ks, vs, cs, ns, ms = [], [], [], [], []
    for l in range(DEPTH):
        mod = mod_all[l]
        g1 = norm1_g[l][None, :]
        zb, zg, zf, kv = _in_proj(x, g1, mod, _split_in_cols(w_in[l].astype(BF16)),
                                  _split_in_cols(b_in[l][None, :]))
        ya = (_ctx_attention(zb), _natt(zb, ck, cv, _natt_bias(rpb[l]), l))
        conv_w = (w_dw[l], b_dw[l][None, :], cln_g[l][None, :], cln_b[l][None, :])
        yc = (_conv(zf, *conv_w, SEQ, BATCH, 0), _conv(zf, *conv_w, DEC_SEQ, DEC_BATCH, T_CTX // DEC_SEQ))
        ng = mnorm_g[l][None, :]
        ym_ctx, c_l, n_l, m_l = _mlstm(zb, zf, ctx_c0, ctx_n0, ctx_m0, ng, SEQ, BATCH, 0, lambda b: (0,))
        ym_lat, _, _, _ = _mlstm(zb, zf, lat_c0, lat_n0, lat_m0, ng, DEC_SEQ, DEC_BATCH, T_CTX // DEC_SEQ,
                                 lambda b: (b, l))
        x = _merge(x, mod, ya, yc, (ym_ctx, ym_lat), zg, w_pa[l].astype(BF16), w_pc[l].astype(BF16),
                   w_pm[l].astype(BF16), w_out[l].astype(BF16))
        x = _moe(x, norm2_g[l][None, :], mod, w_router[l].T, b_router[l][:, None],
                 w_gu, b_gu, w_down, b_down, l, final_g[None, :] if l == DEPTH - 1 else None)
        ks.append(kv[:, :D_A].reshape(BATCH, SEQ, N_HEADS_A, HEAD_DIM_A))
        vs.append(kv[:, D_A:].reshape(BATCH, SEQ, N_HEADS_A, HEAD_DIM_A))
        cs.append(c_l.reshape(BATCH, 2, N_HEADS_M, HEAD_DIM_M, HEAD_DIM_M))
        ns.append(n_l.reshape(BATCH, 2, N_HEADS_M, HEAD_DIM_M))
        ms.append(m_l[:, :, 0].reshape(BATCH, 2, N_HEADS_M))

    y_ctx, y_lat = x
    return (y_ctx.reshape(BATCH, SEQ, D_MODEL), y_lat.reshape(DEC_BATCH, DEC_SEQ, D_MODEL),
            jnp.stack(ks, axis=1), jnp.stack(vs, axis=1), jnp.stack(cs, axis=1),
            jnp.stack(ns, axis=1), jnp.stack(ms, axis=1))
```

```python
import functools

import numpy as np
import jax
import jax.numpy as jnp
from jax import lax
from jax.experimental import pallas as pl
from jax.experimental.pallas import tpu as pltpu

F32 = jnp.float32
BF16 = jnp.bfloat16
HIGHEST = lax.Precision.HIGHEST

D_MODEL = 1024
BATCH = 16
SEQ = 256
DEPTH = 2
DEC_BATCH = 8
DEC_SEQ = 1024
PAST_LEN = 512
GRID_W = 64
N_HEADS_A = 8
HEAD_DIM_A = 64
D_A = N_HEADS_A * HEAD_DIM_A
WIN_ROWS = 8
WIN_COLS = 16
D_CONV = 512
CONV_WIDTH = 31
N_HEADS_M = 4
HEAD_DIM_M = 128
D_M = N_HEADS_M * HEAD_DIM_M
N_GATE_M = 4 * N_HEADS_M
N_EXPERTS = 32
TOP_K = 4
D_EXPERT = 1024
SWIGLU_ALPHA = 1.702
SWIGLU_LIMIT = 7.0
RMS_EPS = 1e-6
LN_EPS = 1e-5
GATE_OFF = 3 * D_A + 2 * D_CONV + 4 * D_M

T_CTX = BATCH * SEQ
T_LAT = DEC_BATCH * DEC_SEQ
T_ALL = T_CTX + T_LAT
N_SEG = 1 + DEC_BATCH
SEG_PAD = 16
GRID_ROWS = DEC_SEQ // GRID_W
NEG = -1e30

N_ZB = 6 * 512
N_ZG = 3 * D_MODEL
N_ZF = 1664
QB_QA, QB_KA, QB_VA, QB_QM, QB_KM, QB_VM = 0, 1, 2, 3, 4, 5
ZG_GA, ZG_GC, ZG_GM = 0, 1, 2
ZF_CU, ZF_CG, ZF_OM = 0, 1, 2
ZF_GATES = 12

TM_TOK = 512
TM_PROJ = 512
TM_MOE = 512
N_TILES = T_ALL // TM_MOE
CHUNK_ROWS = 16
MXU_ROWS = 256
Q_TILE = -(-(TM_MOE * TOP_K + N_EXPERTS * (CHUNK_ROWS - 1)) // MXU_ROWS) * MXU_ROWS
CH_PER_TILE = Q_TILE // CHUNK_ROWS
E_GROUP = 256
CPG = E_GROUP // CHUNK_ROWS
MAX_CHUNKS = ((T_ALL * TOP_K + N_TILES * N_EXPERTS * (CHUNK_ROWS - 1)) // CHUNK_ROWS
              + N_EXPERTS * (CPG - 1)) + CPG
assert Q_TILE - (TM_MOE * TOP_K + N_EXPERTS * (CHUNK_ROWS - 1)) >= 2 * CHUNK_ROWS and N_TILES > CPG
READ_SPARE = N_TILES * Q_TILE - CHUNK_ROWS
VMEM_LIMIT = 60 * 1024 * 1024


def _cparams(sem=None):
    return pltpu.CompilerParams(dimension_semantics=sem, vmem_limit_bytes=VMEM_LIMIT)


def _seg_of_tile(i, tile):
    n_ctx = T_CTX // tile
    per_lat = DEC_SEQ // tile
    return jnp.where(i < n_ctx, 0, 1 + (i - n_ctx) // per_lat)


def _dot_nt(a, b):
    return lax.dot_general(a, b, (((1,), (1,)), ((), ())), preferred_element_type=F32)


def _sigmoid(x):
    return 1.0 / (1.0 + jnp.exp(-x))


def _mod_kernel(c_ref, w_ref, b_ref, o_ref):
    c = c_ref[...]
    s = c * _sigmoid(c)
    o_ref[...] = jnp.dot(s, w_ref[...], precision=HIGHEST, preferred_element_type=F32) + b_ref[...]


def _modulation(cond, w_mod, b_mod):
    tn = 1536
    return pl.pallas_call(
        _mod_kernel,
        grid=(DEPTH, 6 * D_MODEL // tn),
        in_specs=[pl.BlockSpec((SEG_PAD, D_MODEL), lambda l, j: (0, 0)),
                  pl.BlockSpec((None, D_MODEL, tn), lambda l, j: (l, 0, j)),
                  pl.BlockSpec((None, 1, tn), lambda l, j: (l, 0, j))],
        out_specs=pl.BlockSpec((None, SEG_PAD, tn), lambda l, j: (l, 0, j)),
        out_shape=jax.ShapeDtypeStruct((DEPTH, SEG_PAD, 6 * D_MODEL), F32),
        compiler_params=_cparams(("arbitrary", "arbitrary")),
        name="modulation",
    )(cond, w_mod, b_mod.reshape(DEPTH, 1, 6 * D_MODEL))


def _normmod(x, g, mod, shift_idx, scale_idx):
    y = x * lax.rsqrt(jnp.mean(x * x, axis=-1, keepdims=True) + RMS_EPS) * g
    return y * (1.0 + mod[scale_idx:scale_idx + 1, :]) + mod[shift_idx:shift_idx + 1, :]


PROJ_CTX_TILES = T_CTX // TM_PROJ


def _in_proj_kernel(x_ref, g_ref, mod_ref, wb_ref, bb_ref, wg_ref, bg_ref, wf_ref, bf_ref,
                    zb_ref, zg_ref, zf_ref, kv_ref):
    h = _normmod(x_ref[...], g_ref[...], mod_ref[...], 0, 1).astype(BF16)
    acc = jnp.dot(h, wb_ref[...], preferred_element_type=F32) + bb_ref[...]
    zb_ref[...] = acc.astype(BF16)

    @pl.when(pl.program_id(0) < PROJ_CTX_TILES)
    def _():
        kv_ref[...] = acc[:, D_A:3 * D_A]

    gates = jnp.dot(h, wg_ref[...], preferred_element_type=F32) + bg_ref[...]
    zg_ref[...] = _sigmoid(gates).astype(BF16)
    zf_ref[...] = jnp.dot(h, wf_ref[...], preferred_element_type=F32) + bf_ref[...]


def _in_proj(x, norm_g, mod, w, b):
    def full(a):
        return pl.BlockSpec(a.shape, lambda i: (0, 0), pipeline_mode=pl.Buffered(1))

    def rows(n):
        return pl.BlockSpec((TM_PROJ, n), lambda i: (i, 0))

    return pl.pallas_call(
        _in_proj_kernel,
        grid=(T_ALL // TM_PROJ,),
        in_specs=[rows(D_MODEL),
                  pl.BlockSpec((1, D_MODEL), lambda i: (0, 0)),
                  pl.BlockSpec((None, 6, D_MODEL), lambda i: (_seg_of_tile(i, TM_PROJ), 0, 0)),
                  full(w[0]), full(b[0]), full(w[1]), full(b[1]), full(w[2]), full(b[2])],
        out_specs=[rows(N_ZB), rows(N_ZG), rows(N_ZF),
                   pl.BlockSpec((TM_PROJ, 2 * D_A), lambda i: (jnp.minimum(i, PROJ_CTX_TILES - 1), 0))],
        out_shape=[jax.ShapeDtypeStruct((T_ALL, N_ZB), BF16), jax.ShapeDtypeStruct((T_ALL, N_ZG), BF16),
                   jax.ShapeDtypeStruct((T_ALL, N_ZF), F32), jax.ShapeDtypeStruct((T_CTX, 2 * D_A), F32)],
        compiler_params=_cparams(("arbitrary",)),
        name="in_proj",
    )(x, norm_g, mod, w[0], b[0], w[1], b[1], w[2], b[2])


HEAD_PAIR = 2 * HEAD_DIM_A
ATT_SCALE = HEAD_DIM_A ** -0.5


def _pair_queries(q2):
    lo = lax.broadcasted_iota(jnp.int32, (1, HEAD_PAIR), 1) < HEAD_DIM_A
    q2 = q2 * ATT_SCALE
    zero = jnp.zeros_like(q2)
    return lo, jnp.concatenate([jnp.where(lo, q2, zero), jnp.where(lo, zero, q2)], axis=0)


def _unpair(lo, o_stacked):
    rows = o_stacked.shape[0] // 2
    return jnp.where(lo, o_stacked[:rows], o_stacked[rows:])


def _ctx_attn_kernel(q_ref, k_ref, v_ref, o_ref):
    for hp in range(N_HEADS_A // 2):
        sl = slice(hp * HEAD_PAIR, (hp + 1) * HEAD_PAIR)
        lo, qs = _pair_queries(q_ref[:, sl])
        s = _dot_nt(qs, k_ref[:, sl])
        p = jnp.exp(s - jnp.max(s, axis=-1, keepdims=True))
        l = jnp.sum(p, axis=-1, keepdims=True)
        o = jnp.dot(p.astype(BF16), v_ref[:, sl], preferred_element_type=F32) / l
        o_ref[:, sl] = _unpair(lo, o).astype(o_ref.dtype)


def _ctx_attention(zb):
    def spec(cb):
        return pl.BlockSpec((SEQ, D_A), lambda b: (b, cb))

    return pl.pallas_call(
        _ctx_attn_kernel,
        grid=(BATCH,),
        in_specs=[spec(QB_QA), spec(QB_KA), spec(QB_VA)],
        out_specs=pl.BlockSpec((SEQ, D_A), lambda b: (b, 0)),
        out_shape=jax.ShapeDtypeStruct((T_CTX, D_A), BF16),
        compiler_params=_cparams(("arbitrary",)),
        name="ctx_attention",
    )(zb, zb, zb)


NQ_ROWS = 4
NW_ROWS = 12
NQ_BLOCKS = GRID_ROWS // NQ_ROWS
NQ_TOK = NQ_ROWS * GRID_W
NW_TOK = NW_ROWS * GRID_W


def _window_row(qb, xp):
    return xp.clip(qb * NQ_ROWS - WIN_ROWS // 2, 0, GRID_ROWS - NW_ROWS)


N_REL_ROWS = 2 * WIN_ROWS - 1


def _natt_rel_rows():
    r = np.arange(NQ_BLOCKS)[:, None, None] * NQ_ROWS + np.arange(NQ_ROWS)[None, :, None]
    krow = _window_row(np.arange(NQ_BLOCKS), np)[:, None, None] + np.arange(NW_ROWS)[None, None, :]
    rs = np.clip(r - WIN_ROWS // 2, 0, GRID_ROWS - WIN_ROWS)
    assert ((rs >= krow[:, :, :1]) & (rs + WIN_ROWS <= krow[:, :, -1:] + 1)).all()
    return np.where((krow >= rs) & (krow < rs + WIN_ROWS), krow - r + WIN_ROWS - 1, N_REL_ROWS)


NATT_REL = _natt_rel_rows()


def _natt_kernel(q_ref, k_ref, v_ref, kc_ref, vc_ref, toe_ref, o_ref, bias_scr):
    qb = pl.program_id(0)

    @pl.when(pl.program_id(1) == 0)
    def _():
        for v in range(NQ_BLOCKS):
            @pl.when(qb == v)
            def _(v=v):
                for h in range(N_HEADS_A):
                    for rq in range(NQ_ROWS):
                        for kr in range(NW_ROWS):
                            half = (kr % 2) * GRID_W
                            bias_scr[h, rq * GRID_W:(rq + 1) * GRID_W, kr * GRID_W:(kr + 1) * GRID_W] = (
                                toe_ref[h, int(NATT_REL[v, rq, kr]), :, half:half + GRID_W])

    start = pl.multiple_of(_window_row(qb, jnp) * GRID_W, NQ_TOK)
    band = NW_TOK
    for hp in range(N_HEADS_A // 2):
        sl = slice(hp * HEAD_PAIR, (hp + 1) * HEAD_PAIR)
        lo, qs = _pair_queries(q_ref[:, sl])
        bias = bias_scr[2 * hp:2 * hp + 2].reshape(2 * NQ_TOK, band)
        s_loc = _dot_nt(qs, k_ref[pl.ds(start, band), sl]) + bias
        s_ctx = _dot_nt(qs, kc_ref[:, sl])
        m = jnp.maximum(jnp.max(s_loc, axis=-1, keepdims=True), jnp.max(s_ctx, axis=-1, keepdims=True))
        p_loc = jnp.exp(s_loc - m)
        p_ctx = jnp.exp(s_ctx - m)
        l = jnp.sum(p_loc, axis=-1, keepdims=True) + jnp.sum(p_ctx, axis=-1, keepdims=True)
        o = (jnp.dot(p_loc.astype(BF16), v_ref[pl.ds(start, band), sl], preferred_element_type=F32)
             + jnp.dot(p_ctx.astype(BF16), vc_ref[:, sl], preferred_element_type=F32))
        o_ref[:, sl] = _unpair(lo, o / l).astype(o_ref.dtype)


def _natt_bias(rpb_l):
    qc = np.arange(GRID_W)
    kc = np.arange(GRID_W)
    cs = np.clip(qc - WIN_COLS // 2, 0, GRID_W - WIN_COLS)
    ok = (kc[None, :] >= cs[:, None]) & (kc[None, :] < cs[:, None] + WIN_COLS)
    dc = np.clip(kc[None, :] - qc[:, None] + WIN_COLS - 1, 0, 2 * WIN_COLS - 2)
    pick = (dc[None] == np.arange(2 * WIN_COLS - 1)[:, None, None]).astype(np.float32)
    toe = jnp.einsum('hdc,cqk->hdqk', rpb_l, jnp.asarray(pick), precision=HIGHEST)
    toe = jnp.where(jnp.asarray(ok)[None, None], toe, NEG)
    toe = jnp.concatenate([toe, jnp.full((N_HEADS_A, 1, GRID_W, GRID_W), NEG, F32)], axis=1)
    return jnp.concatenate([toe, toe], axis=-1)


def _natt(zb, cache_k, cache_v, toe, layer):
    lat0 = T_CTX // DEC_SEQ
    row0 = T_CTX // NQ_TOK
    return pl.pallas_call(
        _natt_kernel,
        grid=(NQ_BLOCKS, DEC_BATCH),
        in_specs=[pl.BlockSpec((NQ_TOK, D_A), lambda qb, b: (row0 + b * NQ_BLOCKS + qb, QB_QA)),
                  pl.BlockSpec((DEC_SEQ, D_A), lambda qb, b: (lat0 + b, QB_KA)),
                  pl.BlockSpec((DEC_SEQ, D_A), lambda qb, b: (lat0 + b, QB_VA)),
                  pl.BlockSpec((None, None, PAST_LEN, D_A), lambda qb, b: (b, layer, 0, 0)),
                  pl.BlockSpec((None, None, PAST_LEN, D_A), lambda qb, b: (b, layer, 0, 0)),
                  pl.BlockSpec((N_HEADS_A, N_REL_ROWS + 1, GRID_W, 2 * GRID_W), lambda qb, b: (0, 0, 0, 0))],
        out_specs=pl.BlockSpec((NQ_TOK, D_A), lambda qb, b: (b * NQ_BLOCKS + qb, 0)),
        out_shape=jax.ShapeDtypeStruct((T_LAT, D_A), BF16),
        scratch_shapes=[pltpu.VMEM((N_HEADS_A, NQ_TOK, NW_TOK), F32)],
        compiler_params=_cparams(("arbitrary", "arbitrary")),
        name="nbr_attention",
    )(zb, zb, zb, cache_k, cache_v, toe)


CONV_HALO = 16
CONV_ROWS = 64


SUBLANES = 8


def _conv_kernel(u_ref, g_ref, w_ref, b_ref, lg_ref, lb_ref, o_ref, pad_scr, sh_scr, *, seq):
    zeros = jnp.zeros((CONV_HALO, D_CONV), F32)
    pad_scr[0:CONV_HALO, :] = zeros
    pad_scr[CONV_HALO + seq:2 * CONV_HALO + seq, :] = zeros
    pad_scr[CONV_HALO:CONV_HALO + seq, :] = u_ref[...] * _sigmoid(g_ref[...])
    n_sh = seq + 2 * CONV_HALO - SUBLANES
    for s in range(SUBLANES):
        sh_scr[s] = pad_scr[s:s + n_sh, :]
    first = CONV_HALO - CONV_WIDTH // 2
    for c in range(seq // CONV_ROWS):
        base = c * CONV_ROWS
        acc = jnp.broadcast_to(b_ref[...], (CONV_ROWS, D_CONV))
        for j in range(CONV_WIDTH):
            q, s = divmod(first + j, SUBLANES)
            row0 = base + q * SUBLANES
            acc = acc + sh_scr[s, row0:row0 + CONV_ROWS, :] * w_ref[j:j + 1, :]
        mu = jnp.mean(acc, axis=-1, keepdims=True)
        xc = acc - mu
        var = jnp.mean(xc * xc, axis=-1, keepdims=True)
        y = xc * lax.rsqrt(var + LN_EPS) * lg_ref[...] + lb_ref[...]
        o_ref[base:base + CONV_ROWS, :] = (y * _sigmoid(y)).astype(o_ref.dtype)


def _conv(z, w_dw, b_dw, ln_g, ln_b, seq, n_seq, row_block0):
    def vec():
        return pl.BlockSpec((1, D_CONV), lambda b: (0, 0))

    return pl.pallas_call(
        functools.partial(_conv_kernel, seq=seq),
        grid=(n_seq,),
        in_specs=[pl.BlockSpec((seq, D_CONV), lambda b: (row_block0 + b, ZF_CU)),
                  pl.BlockSpec((seq, D_CONV), lambda b: (row_block0 + b, ZF_CG)),
                  pl.BlockSpec((CONV_WIDTH, D_CONV), lambda b: (0, 0)),
                  vec(), vec(), vec()],
        out_specs=pl.BlockSpec((seq, D_CONV), lambda b: (b, 0)),
        out_shape=jax.ShapeDtypeStruct((n_seq * seq, D_CONV), BF16),
        scratch_shapes=[pltpu.VMEM((seq + 2 * CONV_HALO, D_CONV), F32),
                        pltpu.VMEM((SUBLANES, seq + 2 * CONV_HALO - SUBLANES, D_CONV), F32)],
        compiler_params=_cparams(("arbitrary",)),
        name="conformer_conv",
    )(z, z, w_dw, b_dw, ln_g, ln_b)


N_STREAM = 2 * N_HEADS_M
MCHUNK = 256


PAIR_M = 2 * HEAD_DIM_M
N_PAIR = N_STREAM // 2


def _per_head_rows(r):
    return jnp.concatenate([jnp.broadcast_to(r[0:1], (HEAD_DIM_M, r.shape[1])),
                            jnp.broadcast_to(r[1:2], (HEAD_DIM_M, r.shape[1]))], axis=0)


def _mlstm_kernel(q_ref, k_ref, v_ref, om_ref, gt_ref, c0_ref, n0_ref, m0_ref, ng_ref,
                  y_ref, c_out, n_out, m_out, hf_scr, hb_scr, c_scr, n_scr, m_scr, cbd_scr, vbd_scr,
                  rows_scr, acol_scr, *, seq):
    nc = seq // MCHUNK
    c_scr[...] = c0_ref[...]
    n_scr[...] = n0_ref[...]
    m_scr[...] = m0_ref[...]
    cbd_scr[...] = jnp.zeros_like(cbd_scr)
    vbd_scr[...] = jnp.zeros_like(vbd_scr)
    for s_id in range(N_STREAM):
        blk = slice((s_id % 2) * HEAD_DIM_M, (s_id % 2 + 1) * HEAD_DIM_M)
        cbd_scr[s_id // 2, blk, blk] = c0_ref[s_id].astype(BF16)
    first_head = lax.broadcasted_iota(jnp.int32, (1, PAIR_M), 1) < HEAD_DIM_M
    rows = lax.broadcasted_iota(jnp.int32, (MCHUNK, MCHUNK), 0)
    cols = lax.broadcasted_iota(jnp.int32, (MCHUNK, MCHUNK), 1)
    tris = ((cols <= rows).astype(F32), (cols >= rows).astype(F32))
    reach = (rows <= cols, rows >= cols)
    lane = lax.broadcasted_iota(jnp.int32, (1, MCHUNK), 1)
    kscale = HEAD_DIM_M ** -0.5

    for cc in range(nc):
        g = gt_ref[cc * MCHUNK:(cc + 1) * MCHUNK, :]
        lf = jnp.minimum(g, 0.0) - jnp.log(1.0 + jnp.exp(-jnp.abs(g)))
        g_t = g.T
        for d in range(2):
            cum = jnp.dot(tris[d], lf, precision=HIGHEST, preferred_element_type=F32)
            cum_t = cum.T
            i0 = 2 * N_HEADS_M * d
            i_rows = g_t[i0:i0 + N_HEADS_M, :]
            b_rows = cum_t[i0 + N_HEADS_M:i0 + 2 * N_HEADS_M, :]
            reach_max = i_rows - b_rows
            for step in (1 << s for s in range(MCHUNK.bit_length() - 1)):
                if d == 0:
                    shifted = jnp.where(lane >= step, pltpu.roll(reach_max, step, axis=1), NEG)
                else:
                    shifted = jnp.where(lane < MCHUNK - step, pltpu.roll(reach_max, MCHUNK - step, axis=1), NEG)
                reach_max = jnp.maximum(reach_max, shifted)
            rows_scr[2 * cc + d, 0] = i_rows
            rows_scr[2 * cc + d, 1] = b_rows
            rows_scr[2 * cc + d, 2] = reach_max
            acol_scr[2 * cc + d] = g - pltpu.roll(cum, cum.shape[1] - N_HEADS_M, axis=1)

    def chunk_step(c, carry):
        for d in range(2):
            cidx = c if d == 0 else nc - 1 - c
            off = pl.multiple_of(cidx * MCHUNK, MCHUNK)
            last = MCHUNK - 1 if d == 0 else 0
            i0 = 2 * N_HEADS_M * d
            heads = slice(N_HEADS_M * d, N_HEADS_M * (d + 1))
            i_rows = rows_scr[2 * cidx + d, 0]
            b_rows = rows_scr[2 * cidx + d, 1]
            reach_max = rows_scr[2 * cidx + d, 2]
            a_cols = acol_scr[2 * cidx + d]
            m_prevs = m_scr[heads, 0:1]
            inters = b_rows + m_prevs
            m_ts = jnp.maximum(inters, b_rows + reach_max)
            w_inters = jnp.exp(inters - m_ts)
            floors = jnp.exp(-m_ts)
            b_lasts = b_rows[:, last:last + 1]
            m_news = m_ts[:, last:last + 1]
            w_prevs = jnp.exp(b_lasts + m_prevs - m_news)
            w_srcs = kscale * jnp.exp(b_lasts - b_rows + i_rows - m_news)
            m_scr[heads, :] = jnp.broadcast_to(m_news, (N_HEADS_M, HEAD_DIM_M))
            for hp in range(N_HEADS_M // 2):
                pid = (N_HEADS_M // 2) * d + hp
                hh = slice(2 * hp, 2 * hp + 2)
                ps = slice(hp * PAIR_M, (hp + 1) * PAIR_M)
                q2 = q_ref[pl.ds(off, MCHUNK), ps]
                k2 = k_ref[pl.ds(off, MCHUNK), ps]
                v2_t = v_ref[pl.ds(off, MCHUNK), ps].astype(F32).T
                k_zero = jnp.zeros_like(k2)
                k_stack = jnp.concatenate([jnp.where(first_head, k2, k_zero),
                                           jnp.where(first_head, k_zero, k2)], axis=0)
                decay = jnp.concatenate(
                    [jnp.exp(jnp.where(reach[d], b_rows[h:h + 1] + a_cols[:, i0 + h:i0 + h + 1], NEG)
                             - m_ts[h:h + 1]) for h in (2 * hp, 2 * hp + 1)], axis=0)
                s_t = _dot_nt(k_stack, q2) * (kscale * decay)
                col_sums = jnp.concatenate([jnp.sum(s_t[:MCHUNK], axis=0, keepdims=True),
                                            jnp.sum(s_t[MCHUNK:], axis=0, keepdims=True)], axis=0)
                n2 = n_scr[N_HEADS_M * d + 2 * hp:N_HEADS_M * d + 2 * hp + 2, :]
                n_zero = jnp.zeros((1, HEAD_DIM_M), F32)
                n_mat = jnp.concatenate([jnp.concatenate([n2[0:1], n_zero], axis=1),
                                         jnp.concatenate([n_zero, n2[1:2]], axis=1),
                                         jnp.zeros((SUBLANES - 2, PAIR_M), F32)], axis=0)
                n_q = _dot_nt(n_mat.astype(BF16), q2)[0:2, :]
                den = w_inters[hh] * n_q + col_sums
                inv = 1.0 / jnp.maximum(jnp.abs(den), floors[hh])
                vbd_scr[pid, :HEAD_DIM_M, :MCHUNK] = v2_t[:HEAD_DIM_M].astype(BF16)
                vbd_scr[pid, HEAD_DIM_M:, MCHUNK:] = v2_t[HEAD_DIM_M:].astype(BF16)
                num_t = (_per_head_rows(w_inters[hh]) * _dot_nt(cbd_scr[pid], q2)
                         + jnp.dot(vbd_scr[pid], s_t.astype(BF16), preferred_element_type=F32))
                h_t = num_t * _per_head_rows(inv)
                if d == 0:
                    hf_scr[cidx, ps, :] = h_t
                else:
                    hb_scr[cidx, ps, :] = h_t
                upd = jnp.dot((v2_t * _per_head_rows(w_srcs[hh])).astype(BF16), k2,
                              preferred_element_type=F32)
                w_mat = jnp.concatenate([w_srcs[hh], jnp.zeros((SUBLANES - 2, MCHUNK), F32)], axis=0)
                n_upd = jnp.dot(w_mat.astype(BF16), k2, preferred_element_type=F32)
                for j in range(2):
                    h = 2 * hp + j
                    s_id = N_HEADS_M * d + h
                    blk = slice(j * HEAD_DIM_M, (j + 1) * HEAD_DIM_M)
                    c_new = w_prevs[h:h + 1] * c_scr[s_id] + upd[blk, blk]
                    c_scr[s_id] = c_new
                    cbd_scr[pid, blk, blk] = c_new.astype(BF16)
                    n_scr[s_id:s_id + 1, :] = w_prevs[h:h + 1] * n2[j:j + 1] + n_upd[j:j + 1, blk]
        return carry

    lax.fori_loop(0, nc, chunk_step, 0)

    for c in range(nc):
        ts = slice(c * MCHUNK, (c + 1) * MCHUNK)
        for h in range(N_HEADS_M):
            hs = slice(h * HEAD_DIM_M, (h + 1) * HEAD_DIM_M)
            hsum = hf_scr[c, hs, :] + hb_scr[c, hs, :]
            mu = jnp.mean(hsum, axis=0, keepdims=True)
            xc = hsum - mu
            var = jnp.mean(xc * xc, axis=0, keepdims=True)
            hn = (xc * lax.rsqrt(var + LN_EPS)).T
            y_ref[ts, hs] = (_sigmoid(om_ref[ts, hs]) * (hn * ng_ref[:, hs])).astype(y_ref.dtype)
    c_out[...] = c_scr[...]
    n_out[...] = n_scr[...]
    m_out[...] = m_scr[...]


def _mlstm(zb, zf, c0, n0, m0, norm_g, seq, n_seq, row_block0, state_map):
    lead = len(state_map(0))

    def zspec(cb):
        return pl.BlockSpec((seq, D_M), lambda b: (row_block0 + b, cb))

    def sspec(tail):
        return pl.BlockSpec((None,) * lead + tail, lambda b: state_map(b) + (0,) * len(tail))

    return pl.pallas_call(
        functools.partial(_mlstm_kernel, seq=seq),
        grid=(n_seq,),
        in_specs=[zspec(QB_QM), zspec(QB_KM), zspec(QB_VM), zspec(ZF_OM),
                  pl.BlockSpec((seq, 128), lambda b: (row_block0 + b, ZF_GATES)),
                  sspec((N_STREAM, HEAD_DIM_M, HEAD_DIM_M)),
                  sspec((N_STREAM, HEAD_DIM_M)),
                  sspec((N_STREAM, HEAD_DIM_M)),
                  pl.BlockSpec((1, D_M), lambda b: (0, 0))],
        out_specs=[pl.BlockSpec((seq, D_M), lambda b: (b, 0)),
                   pl.BlockSpec((None, N_STREAM, HEAD_DIM_M, HEAD_DIM_M), lambda b: (b, 0, 0, 0)),
                   pl.BlockSpec((None, N_STREAM, HEAD_DIM_M), lambda b: (b, 0, 0)),
                   pl.BlockSpec((None, N_STREAM, HEAD_DIM_M), lambda b: (b, 0, 0))],
        out_shape=[jax.ShapeDtypeStruct((n_seq * seq, D_M), BF16),
                   jax.ShapeDtypeStruct((n_seq, N_STREAM, HEAD_DIM_M, HEAD_DIM_M), F32),
                   jax.ShapeDtypeStruct((n_seq, N_STREAM, HEAD_DIM_M), F32),
                   jax.ShapeDtypeStruct((n_seq, N_STREAM, HEAD_DIM_M), F32)],
        scratch_shapes=[pltpu.VMEM((seq // MCHUNK, D_M, MCHUNK), F32),
                        pltpu.VMEM((seq // MCHUNK, D_M, MCHUNK), F32),
                        pltpu.VMEM((N_STREAM, HEAD_DIM_M, HEAD_DIM_M), F32),
                        pltpu.VMEM((N_STREAM, HEAD_DIM_M), F32),
                        pltpu.VMEM((N_STREAM, HEAD_DIM_M), F32),
                        pltpu.VMEM((N_PAIR, PAIR_M, PAIR_M), BF16),
                        pltpu.VMEM((N_PAIR, PAIR_M, 2 * MCHUNK), BF16),
                        pltpu.VMEM((2 * (seq // MCHUNK), 3, N_HEADS_M, MCHUNK), F32),
                        pltpu.VMEM((2 * (seq // MCHUNK), MCHUNK, 128), F32)],
        compiler_params=_cparams(("arbitrary",)),
        name="mlstm",
    )(zb, zb, zb, zf, zf, c0, n0, m0, norm_g)


N_CTX_TILES = T_CTX // TM_TOK


def _merge_kernel(x_ref, mod_ref, ya_c, ya_l, yc_c, yc_l, ym_c, ym_l, ga_ref, gc_ref, gm_ref,
                  wa_ref, wc_ref, wm_ref, wo_ref, o_ref):
    is_ctx = pl.program_id(0) < N_CTX_TILES

    def branch(y_ctx, y_lat, g_ref, w_ref):
        y = jnp.where(is_ctx, y_ctx[...], y_lat[...])
        return g_ref[...].astype(F32) * jnp.dot(y, w_ref[...], preferred_element_type=F32)

    merged = (branch(ya_c, ya_l, ga_ref, wa_ref) + branch(yc_c, yc_l, gc_ref, wc_ref)
              + branch(ym_c, ym_l, gm_ref, wm_ref))
    mix = jnp.dot(merged.astype(BF16), wo_ref[...], preferred_element_type=F32)
    o_ref[...] = x_ref[...] + mod_ref[2:3, :] * mix


def _merge(x, mod, ya, yc, ym, zg, w_pa, w_pc, w_pm, w_out):
    def rows(width, cb=0):
        return pl.BlockSpec((TM_TOK, width), lambda i: (i, cb))

    def ctx_rows(width):
        return pl.BlockSpec((TM_TOK, width), lambda i: (jnp.minimum(i, N_CTX_TILES - 1), 0))

    def lat_rows(width):
        return pl.BlockSpec((TM_TOK, width), lambda i: (jnp.maximum(i - N_CTX_TILES, 0), 0))

    def full(shape):
        return pl.BlockSpec(shape, lambda i: (0, 0))

    return pl.pallas_call(
        _merge_kernel,
        grid=(T_ALL // TM_TOK,),
        in_specs=[rows(D_MODEL),
                  pl.BlockSpec((None, 6, D_MODEL), lambda i: (_seg_of_tile(i, TM_TOK), 0, 0)),
                  ctx_rows(D_A), lat_rows(D_A), ctx_rows(D_CONV), lat_rows(D_CONV),
                  ctx_rows(D_M), lat_rows(D_M),
                  rows(D_MODEL, ZG_GA), rows(D_MODEL, ZG_GC), rows(D_MODEL, ZG_GM),
                  full((D_A, D_MODEL)), full((D_CONV, D_MODEL)), full((D_M, D_MODEL)),
                  full((D_MODEL, D_MODEL))],
        out_specs=rows(D_MODEL),
        out_shape=jax.ShapeDtypeStruct((T_ALL, D_MODEL), F32),
        compiler_params=_cparams(("arbitrary",)),
        name="merge",
    )(x, mod, ya[0], ya[1], yc[0], yc[1], ym[0], ym[1], zg, zg, zg, w_pa, w_pc, w_pm, w_out)


def _route_sort_kernel(x_ref, g_ref, mod_ref, wr_ref, br_ref, xt_ref, pos_ref, gate_ref, nch_ref, seg_ref):
    h = _normmod(x_ref[...], g_ref[...], mod_ref[...], 3, 4)
    hb = h.astype(BF16)
    logits = _dot_nt(wr_ref[...].astype(BF16), hb) + br_ref[...]
    e_iota = lax.broadcasted_iota(jnp.int32, (N_EXPERTS, TM_MOE), 0).astype(F32)
    sels, vals = [], []
    l = logits
    for k in range(TOP_K):
        m = jnp.max(l, axis=0, keepdims=True)
        idx = jnp.min(jnp.where(l == m, e_iota, float(N_EXPERTS)), axis=0, keepdims=True)
        sel = e_iota == idx
        vals.append(m)
        sels.append(sel)
        l = jnp.where(sel, -jnp.inf, l)
    exps = [jnp.exp(v - vals[0]) for v in vals]
    tot = exps[0] + exps[1] + exps[2] + exps[3]
    onehot = jnp.zeros((N_EXPERTS, TM_MOE), F32)
    for k in range(TOP_K):
        gate_ref[k:k + 1, :] = exps[k] / tot
        onehot = onehot + sels[k].astype(F32)
    gate_ref[TOP_K:8, :] = jnp.zeros((8 - TOP_K, TM_MOE), F32)

    cnt = jnp.sum(onehot, axis=1, keepdims=True)
    nch = jnp.floor((cnt + (CHUNK_ROWS - 1)) / CHUNK_ROWS)
    ei = lax.broadcasted_iota(jnp.int32, (N_EXPERTS, N_EXPERTS), 0)
    ej = lax.broadcasted_iota(jnp.int32, (N_EXPERTS, N_EXPERTS), 1)
    seg = jnp.dot((ej < ei).astype(F32), jnp.broadcast_to(nch, (N_EXPERTS, 128)), precision=HIGHEST,
                  preferred_element_type=F32)
    nch_ref[...] = jnp.broadcast_to(nch, (N_EXPERTS, 128)).astype(jnp.int32)
    seg_ref[...] = seg.astype(jnp.int32)

    t_src = lax.broadcasted_iota(jnp.int32, (TM_MOE, TM_MOE), 0)
    t_dst = lax.broadcasted_iota(jnp.int32, (TM_MOE, TM_MOE), 1)
    before = (t_src < t_dst).astype(BF16)
    row_of = (seg[:, 0:1] * CHUNK_ROWS
              + jnp.dot(onehot.astype(BF16), before, preferred_element_type=F32))
    q_iota = lax.broadcasted_iota(jnp.int32, (Q_TILE, TM_MOE), 0)
    perm = jnp.zeros((Q_TILE, TM_MOE), F32)
    for k in range(TOP_K):
        q_k = jnp.sum(jnp.where(sels[k], row_of, 0.0), axis=0, keepdims=True).astype(jnp.int32)
        pos_ref[k:k + 1, :] = q_k
        perm = jnp.where(q_iota == q_k, 1.0, perm)
    pos_ref[TOP_K:8, :] = jnp.zeros((8 - TOP_K, TM_MOE), jnp.int32)
    xt_ref[...] = jnp.dot(perm.astype(BF16), hb, preferred_element_type=F32).astype(BF16)


def _route_sort(x, norm_g, mod, w_rt, b_r):
    tspec = pl.BlockSpec((8, TM_MOE), lambda i: (0, i))
    mspec = pl.BlockSpec((None, N_EXPERTS, 128), lambda i: (i, 0, 0))
    meta = jax.ShapeDtypeStruct((N_TILES, N_EXPERTS, 128), jnp.int32)
    return pl.pallas_call(
        _route_sort_kernel,
        grid=(N_TILES,),
        in_specs=[pl.BlockSpec((TM_MOE, D_MODEL), lambda i: (i, 0)),
                  pl.BlockSpec((1, D_MODEL), lambda i: (0, 0)),
                  pl.BlockSpec((None, 6, D_MODEL), lambda i: (_seg_of_tile(i, TM_MOE), 0, 0)),
                  pl.BlockSpec((N_EXPERTS, D_MODEL), lambda i: (0, 0)),
                  pl.BlockSpec((N_EXPERTS, 1), lambda i: (0, 0))],
        out_specs=[pl.BlockSpec((Q_TILE, D_MODEL), lambda i: (i, 0)), tspec, tspec, mspec, mspec],
        out_shape=[jax.ShapeDtypeStruct((N_TILES * Q_TILE, D_MODEL), BF16),
                   jax.ShapeDtypeStruct((8, T_ALL), jnp.int32), jax.ShapeDtypeStruct((8, T_ALL), F32),
                   meta, meta],
        compiler_params=_cparams(("arbitrary",)),
        name="moe_route_sort",
    )(x, norm_g, mod, w_rt, b_r)


def _expert_kernel(nch_ref, seg_ref, wgu_ref, bgu_ref, wd_ref, bd_ref, xt_ref, yt_ref,
                   wgu_scr, wd_scr, xbuf, ybuf, row_scr, gstart_scr, gsem, ssem):
    del xt_ref
    e = pl.program_id(0)

    def src_row(row):
        return pl.multiple_of(jnp.where(row >= 0, row, READ_SPARE), CHUNK_ROWS)

    def dst_row(row, slot, c):
        spare = c * Q_TILE + jnp.where(slot == 0, Q_TILE - CHUNK_ROWS, Q_TILE - 2 * CHUNK_ROWS)
        return pl.multiple_of(jnp.where(row >= 0, row, spare), CHUNK_ROWS)

    def chunk_rows(c):
        return slice(c * CHUNK_ROWS, (c + 1) * CHUNK_ROWS)

    def start_in(g):
        slot = g % 2
        for c in range(CPG):
            row = row_scr[g * CPG + c]
            pltpu.make_async_copy(yt_ref.at[pl.ds(src_row(row), CHUNK_ROWS), :], xbuf.at[slot, chunk_rows(c), :],
                                  gsem.at[slot]).start()

    def start_out(g, slot):
        for c in range(CPG):
            row = row_scr[g * CPG + c]
            pltpu.make_async_copy(ybuf.at[slot, chunk_rows(c), :],
                                  yt_ref.at[pl.ds(dst_row(row, slot, c), CHUNK_ROWS), :], ssem.at[slot]).start()

    def wait_in(slot):
        pltpu.make_async_copy(yt_ref.at[pl.ds(0, E_GROUP), :], xbuf.at[slot], gsem.at[slot]).wait()

    def wait_out(slot):
        pltpu.make_async_copy(ybuf.at[slot], yt_ref.at[pl.ds(0, E_GROUP), :], ssem.at[slot]).wait()

    @pl.when(e == 0)
    def _():
        def per_expert(ee, cnt):
            gstart_scr[ee] = cnt // CPG

            def per_tile(t, cnt):
                first = (t * CH_PER_TILE + seg_ref[t * N_EXPERTS + ee]) * CHUNK_ROWS

                def per_chunk(j, cnt):
                    row_scr[cnt] = first + j * CHUNK_ROWS
                    return cnt + 1

                return lax.fori_loop(0, nch_ref[t * N_EXPERTS + ee], per_chunk, cnt)

            cnt = lax.fori_loop(0, N_TILES, per_tile, cnt)
            padded = (cnt + CPG - 1) // CPG * CPG

            def pad(i, carry):
                row_scr[i] = -1
                return carry

            lax.fori_loop(cnt, padded, pad, 0)
            return padded

        total = lax.fori_loop(0, N_EXPERTS, per_expert, 0)
        gstart_scr[N_EXPERTS] = total // CPG

        def pad(i, carry):
            row_scr[i] = -1
            return carry

        lax.fori_loop(total, total + CPG, pad, 0)
        ybuf[...] = jnp.zeros_like(ybuf)
        start_out(total // CPG, 0)
        start_out(total // CPG, 1)
        start_in(0)

    g_first = gstart_scr[e]
    g_end = gstart_scr[e + 1]

    @pl.when(g_end > g_first)
    def _():
        wgu_scr[...] = wgu_ref[...].astype(BF16)
        wd_scr[...] = wd_ref[...].astype(BF16)

    def group_step(g, carry):
        slot = g % 2
        start_in(g + 1)
        wait_in(slot)
        wait_out(slot)
        hgu = jnp.dot(xbuf[slot], wgu_scr[...], preferred_element_type=F32) + bgu_ref[...]
        h_glu = jnp.minimum(hgu[:, :D_EXPERT], SWIGLU_LIMIT)
        h_lin = jnp.clip(hgu[:, D_EXPERT:], -SWIGLU_LIMIT, SWIGLU_LIMIT)
        act = (h_lin + 1.0) * (h_glu * _sigmoid(SWIGLU_ALPHA * h_glu))
        y = jnp.dot(act.astype(BF16), wd_scr[...], preferred_element_type=F32) + bd_ref[...]
        ybuf[slot] = y.astype(BF16)
        start_out(g, slot)
        return carry

    lax.fori_loop(g_first, g_end, group_step, 0)

    @pl.when(e == N_EXPERTS - 1)
    def _():
        wait_in(gstart_scr[N_EXPERTS] % 2)
        wait_out(0)
        wait_out(1)


def _experts(nch_flat, seg_flat, xt, w_gu, b_gu, w_down, b_down, layer):
    return pl.pallas_call(
        _expert_kernel,
        grid_spec=pltpu.PrefetchScalarGridSpec(
            num_scalar_prefetch=2,
            grid=(N_EXPERTS,),
            in_specs=[pl.BlockSpec((None, None, D_MODEL, 2 * D_EXPERT), lambda e, n, s: (layer, e, 0, 0)),
                      pl.BlockSpec((None, None, 1, 2 * D_EXPERT), lambda e, n, s: (layer, e, 0, 0)),
                      pl.BlockSpec((None, None, D_EXPERT, D_MODEL), lambda e, n, s: (layer, e, 0, 0)),
                      pl.BlockSpec((None, None, 1, D_MODEL), lambda e, n, s: (layer, e, 0, 0)),
                      pl.BlockSpec(memory_space=pl.ANY)],
            out_specs=pl.BlockSpec(memory_space=pl.ANY),
            scratch_shapes=[pltpu.VMEM((D_MODEL, 2 * D_EXPERT), BF16),
                            pltpu.VMEM((D_EXPERT, D_MODEL), BF16),
                            pltpu.VMEM((2, E_GROUP, D_MODEL), BF16),
                            pltpu.VMEM((2, E_GROUP, D_MODEL), BF16),
                            pltpu.SMEM((MAX_CHUNKS,), jnp.int32),
                            pltpu.SMEM((N_EXPERTS + 1,), jnp.int32),
                            pltpu.SemaphoreType.DMA((2,)),
                            pltpu.SemaphoreType.DMA((2,))]),
        out_shape=jax.ShapeDtypeStruct((N_TILES * Q_TILE, D_MODEL), BF16),
        input_output_aliases={6: 0},
        compiler_params=_cparams(("arbitrary",)),
        name="moe_experts",
    )(nch_flat, seg_flat, w_gu, b_gu, w_down, b_down, xt)


MOE_CTX_TILES = T_CTX // TM_MOE


def _combine_rows(x_ref, mod_ref, pos_ref, gate_ref, yt_ref):
    lane = lax.broadcasted_iota(jnp.int32, (TM_MOE, Q_TILE), 1)
    sel = jnp.zeros((TM_MOE, Q_TILE), F32)
    for k in range(TOP_K):
        sel = jnp.where(lane == pos_ref[:, k:k + 1], gate_ref[:, k:k + 1], sel)
    acc = jnp.dot(sel.astype(BF16), yt_ref[...], preferred_element_type=F32)
    return x_ref[...] + mod_ref[5:6, :] * acc


def _combine_kernel(x_ref, mod_ref, pos_ref, gate_ref, yt_ref, o_ref):
    o_ref[...] = _combine_rows(x_ref, mod_ref, pos_ref, gate_ref, yt_ref)


def _combine_final_kernel(x_ref, mod_ref, pos_ref, gate_ref, yt_ref, fg_ref, ctx_ref, lat_ref):
    y = _combine_rows(x_ref, mod_ref, pos_ref, gate_ref, yt_ref)
    y = y * lax.rsqrt(jnp.mean(y * y, axis=-1, keepdims=True) + RMS_EPS) * fg_ref[...]
    is_ctx = pl.program_id(0) < MOE_CTX_TILES

    @pl.when(is_ctx)
    def _():
        ctx_ref[...] = y

    @pl.when(jnp.logical_not(is_ctx))
    def _():
        lat_ref[...] = y


def _combine(x, mod, pos_t, gate_t, yt, final_g=None):
    tile = pl.BlockSpec((TM_MOE, D_MODEL), lambda i: (i, 0))
    in_specs = [tile,
                pl.BlockSpec((None, 6, D_MODEL), lambda i: (_seg_of_tile(i, TM_MOE), 0, 0)),
                pl.BlockSpec((TM_MOE, 8), lambda i: (i, 0)),
                pl.BlockSpec((TM_MOE, 8), lambda i: (i, 0)),
                pl.BlockSpec((Q_TILE, D_MODEL), lambda i: (i, 0))]
    if final_g is None:
        return pl.pallas_call(
            _combine_kernel, grid=(N_TILES,), in_specs=in_specs, out_specs=tile,
            out_shape=jax.ShapeDtypeStruct((T_ALL, D_MODEL), F32),
            compiler_params=_cparams(("arbitrary",)), name="moe_combine",
        )(x, mod, pos_t, gate_t, yt)
    return pl.pallas_call(
        _combine_final_kernel, grid=(N_TILES,),
        in_specs=in_specs + [pl.BlockSpec((1, D_MODEL), lambda i: (0, 0))],
        out_specs=[pl.BlockSpec((TM_MOE, D_MODEL), lambda i: (jnp.minimum(i, MOE_CTX_TILES - 1), 0)),
                   pl.BlockSpec((TM_MOE, D_MODEL), lambda i: (jnp.maximum(i - MOE_CTX_TILES, 0), 0))],
        out_shape=[jax.ShapeDtypeStruct((T_CTX, D_MODEL), F32), jax.ShapeDtypeStruct((T_LAT, D_MODEL), F32)],
        compiler_params=_cparams(("arbitrary",)), name="moe_combine_final",
    )(x, mod, pos_t, gate_t, yt, final_g)


def _moe(x, norm_g, mod, w_rt, b_r, w_gu, b_gu, w_down, b_down, layer, final_g=None):
    xt, pos, gate, nch, seg = _route_sort(x, norm_g, mod, w_rt, b_r)
    yt = _experts(nch[:, :, 0].reshape(-1), seg[:, :, 0].reshape(-1), xt, w_gu,
                  b_gu.reshape(DEPTH, N_EXPERTS, 1, 2 * D_EXPERT), w_down,
                  b_down.reshape(DEPTH, N_EXPERTS, 1, D_MODEL), layer)
    return _combine(x, mod, pos.T, gate.T, yt, final_g)


def _split_in_cols(w):
    conv0, mq0, om0 = 3 * D_A, 3 * D_A + 2 * D_CONV, 3 * D_A + 2 * D_CONV + 3 * D_M
    gates_end = GATE_OFF + N_GATE_M
    pad = jnp.zeros(w.shape[:-1] + (N_ZF - (2 * D_CONV + D_M + N_GATE_M),), w.dtype)
    zb = jnp.concatenate([w[..., :conv0], w[..., mq0:om0]], axis=-1)
    zg = w[..., gates_end:]
    zf = jnp.concatenate([w[..., conv0:mq0], w[..., om0:GATE_OFF], w[..., GATE_OFF:gates_end], pad], axis=-1)
    return zb, zg, zf


def kernel(x_prompt, x_sample, cache_k, cache_v, state_C, state_n, state_m, c, c_ctx, norm1_g, w_mod, b_mod, w_in, b_in, rpb, w_dw, b_dw, cln_g, cln_b, mnorm_g, w_pa, w_pc, w_pm, w_out, norm2_g, w_router, b_router, w_gu, b_gu, w_down, b_down, final_g):
    cond = jnp.concatenate([c_ctx[None, :], c, jnp.zeros((SEG_PAD - N_SEG, D_MODEL), F32)], axis=0)
    mod_all = _modulation(cond, w_mod, b_mod).reshape(DEPTH, SEG_PAD, 6, D_MODEL)

    x = jnp.concatenate([x_prompt.reshape(T_CTX, D_MODEL), x_sample.reshape(T_LAT, D_MODEL)], axis=0)
    ck = cache_k.reshape(DEC_BATCH, DEPTH, PAST_LEN, D_A).astype(BF16)
    cv = cache_v.reshape(DEC_BATCH, DEPTH, PAST_LEN, D_A).astype(BF16)
    lat_c0 = state_C.reshape(DEC_BATCH, DEPTH, N_STREAM, HEAD_DIM_M, HEAD_DIM_M)
    lat_n0 = state_n.reshape(DEC_BATCH, DEPTH, N_STREAM, HEAD_DIM_M)
    lat_m0 = jnp.broadcast_to(state_m.reshape(DEC_BATCH, DEPTH, N_STREAM, 1),
                              (DEC_BATCH, DEPTH, N_STREAM, HEAD_DIM_M))
    ctx_c0 = jnp.zeros((1, N_STREAM, HEAD_DIM_M, HEAD_DIM_M), F32)
    ctx_n0 = jnp.zeros((1, N_STREAM, HEAD_DIM_M), F32)
    ctx_m0 = jnp.full((1, N_STREAM, HEAD_DIM_M), -jnp.inf, F32)

    ks, vs, cs, ns, ms = [], [], [], [], []
    for l in range(DEPTH):
        mod = mod_all[l]
        g1 = norm1_g[l][None, :]
        zb, zg, zf, kv = _in_proj(x, g1, mod, _split_in_cols(w_in[l].astype(BF16)),
                                  _split_in_cols(b_in[l][None, :]))
        ya = (_ctx_attention(zb), _natt(zb, ck, cv, _natt_bias(rpb[l]), l))
        conv_w = (w_dw[l], b_dw[l][None, :], cln_g[l][None, :], cln_b[l][None, :])
        yc = (_conv(zf, *conv_w, SEQ, BATCH, 0), _conv(zf, *conv_w, DEC_SEQ, DEC_BATCH, T_CTX // DEC_SEQ))
        ng = mnorm_g[l][None, :]
        ym_ctx, c_l, n_l, m_l = _mlstm(zb, zf, ctx_c0, ctx_n0, ctx_m0, ng, SEQ, BATCH, 0, lambda b: (0,))
        ym_lat, _, _, _ = _mlstm(zb, zf, lat_c0, lat_n0, lat_m0, ng, DEC_SEQ, DEC_BATCH, T_CTX // DEC_SEQ,
                                 lambda b: (b, l))
        x = _merge(x, mod, ya, yc, (ym_ctx, ym_lat), zg, w_pa[l].astype(BF16), w_pc[l].astype(BF16),
                   w_pm[l].astype(BF16), w_out[l].astype(BF16))
        x = _moe(x, norm2_g[l][None, :], mod, w_router[l].T, b_router[l][:, None],
                 w_gu, b_gu, w_down, b_down, l, final_g[None, :] if l == DEPTH - 1 else None)
        ks.append(kv[:, :D_A].reshape(BATCH, SEQ, N_HEADS_A, HEAD_DIM_A))
        vs.append(kv[:, D_A:].reshape(BATCH, SEQ, N_HEADS_A, HEAD_DIM_A))
        cs.append(c_l.reshape(BATCH, 2, N_HEADS_M, HEAD_DIM_M, HEAD_DIM_M))
        ns.append(n_l.reshape(BATCH, 2, N_HEADS_M, HEAD_DIM_M))
        ms.append(m_l[:, :, 0].reshape(BATCH, 2, N_HEADS_M))

    y_ctx, y_lat = x
    return (y_ctx.reshape(BATCH, SEQ, D_MODEL), y_lat.reshape(DEC_BATCH, DEC_SEQ, D_MODEL),
            jnp.stack(ks, axis=1), jnp.stack(vs, axis=1), jnp.stack(cs, axis=1),
            jnp.stack(ns, axis=1), jnp.stack(ms, axis=1))
```

```python
import functools

import numpy as np
import jax
import jax.numpy as jnp
from jax import lax
from jax.experimental import pallas as pl
from jax.experimental.pallas import tpu as pltpu

F32 = jnp.float32
BF16 = jnp.bfloat16
HIGHEST = lax.Precision.HIGHEST

D_MODEL = 1024
BATCH = 16
SEQ = 256
DEPTH = 2
DEC_BATCH = 8
DEC_SEQ = 1024
PAST_LEN = 512
GRID_W = 64
N_HEADS_A = 8
HEAD_DIM_A = 64
D_A = N_HEADS_A * HEAD_DIM_A
WIN_ROWS = 8
WIN_COLS = 16
D_CONV = 512
CONV_WIDTH = 31
N_HEADS_M = 4
HEAD_DIM_M = 128
D_M = N_HEADS_M * HEAD_DIM_M
N_GATE_M = 4 * N_HEADS_M
N_EXPERTS = 32
TOP_K = 4
D_EXPERT = 1024
SWIGLU_ALPHA = 1.702
SWIGLU_LIMIT = 7.0
RMS_EPS = 1e-6
LN_EPS = 1e-5
GATE_OFF = 3 * D_A + 2 * D_CONV + 4 * D_M

T_CTX = BATCH * SEQ
T_LAT = DEC_BATCH * DEC_SEQ
T_ALL = T_CTX + T_LAT
N_SEG = 1 + DEC_BATCH
SEG_PAD = 16
GRID_ROWS = DEC_SEQ // GRID_W
NEG = -1e30

N_ZB = 6 * 512
N_ZG = 3 * D_MODEL
N_ZF = 1664
QB_QA, QB_KA, QB_VA, QB_QM, QB_KM, QB_VM = 0, 1, 2, 3, 4, 5
ZG_GA, ZG_GC, ZG_GM = 0, 1, 2
ZF_CU, ZF_CG, ZF_OM = 0, 1, 2
ZF_GATES = 12

TM_TOK = 512
TM_PROJ = 512
TM_MOE = 512
N_TILES = T_ALL // TM_MOE
CHUNK_ROWS = 16
MXU_ROWS = 256
Q_TILE = -(-(TM_MOE * TOP_K + N_EXPERTS * (CHUNK_ROWS - 1)) // MXU_ROWS) * MXU_ROWS
CH_PER_TILE = Q_TILE // CHUNK_ROWS
E_GROUP = 256
CPG = E_GROUP // CHUNK_ROWS
MAX_CHUNKS = ((T_ALL * TOP_K + N_TILES * N_EXPERTS * (CHUNK_ROWS - 1)) // CHUNK_ROWS
              + N_EXPERTS * (CPG - 1)) + CPG
assert Q_TILE - (TM_MOE * TOP_K + N_EXPERTS * (CHUNK_ROWS - 1)) >= 2 * CHUNK_ROWS and N_TILES > CPG
READ_SPARE = N_TILES * Q_TILE - CHUNK_ROWS
VMEM_LIMIT = 60 * 1024 * 1024


def _cparams(sem=None):
    return pltpu.CompilerParams(dimension_semantics=sem, vmem_limit_bytes=VMEM_LIMIT)


def _seg_of_tile(i, tile):
    n_ctx = T_CTX // tile
    per_lat = DEC_SEQ // tile
    return jnp.where(i < n_ctx, 0, 1 + (i - n_ctx) // per_lat)


def _dot_nt(a, b):
    return lax.dot_general(a, b, (((1,), (1,)), ((), ())), preferred_element_type=F32)


def _sigmoid(x):
    return 1.0 / (1.0 + jnp.exp(-x))


def _mod_kernel(c_ref, w_ref, b_ref, o_ref):
    c = c_ref[...]
    s = c * _sigmoid(c)
    o_ref[...] = jnp.dot(s, w_ref[...], precision=HIGHEST, preferred_element_type=F32) + b_ref[...]


def _modulation(cond, w_mod, b_mod):
    tn = 1536
    return pl.pallas_call(
        _mod_kernel,
        grid=(DEPTH, 6 * D_MODEL // tn),
        in_specs=[pl.BlockSpec((SEG_PAD, D_MODEL), lambda l, j: (0, 0)),
                  pl.BlockSpec((None, D_MODEL, tn), lambda l, j: (l, 0, j)),
                  pl.BlockSpec((None, 1, tn), lambda l, j: (l, 0, j))],
        out_specs=pl.BlockSpec((None, SEG_PAD, tn), lambda l, j: (l, 0, j)),
        out_shape=jax.ShapeDtypeStruct((DEPTH, SEG_PAD, 6 * D_MODEL), F32),
        compiler_params=_cparams(("arbitrary", "arbitrary")),
        name="modulation",
    )(cond, w_mod, b_mod.reshape(DEPTH, 1, 6 * D_MODEL))


def _normmod(x, g, mod, shift_idx, scale_idx):
    y = x * lax.rsqrt(jnp.mean(x * x, axis=-1, keepdims=True) + RMS_EPS) * g
    return y * (1.0 + mod[scale_idx:scale_idx + 1, :]) + mod[shift_idx:shift_idx + 1, :]


PROJ_CTX_TILES = T_CTX // TM_PROJ


def _in_proj_kernel(x_ref, g_ref, mod_ref, wb_ref, bb_ref, wg_ref, bg_ref, wf_ref, bf_ref,
                    zb_ref, zg_ref, zf_ref, kc_ref, vc_ref):
    h = _normmod(x_ref[...], g_ref[...], mod_ref[...], 0, 1).astype(BF16)
    acc = jnp.dot(h, wb_ref[...], preferred_element_type=F32) + bb_ref[...]
    zb_ref[...] = acc.astype(BF16)

    @pl.when(pl.program_id(0) < PROJ_CTX_TILES)
    def _():
        kc_ref[...] = acc[:, D_A:2 * D_A]
        vc_ref[...] = acc[:, 2 * D_A:3 * D_A]

    gates = jnp.dot(h, wg_ref[...], preferred_element_type=F32) + bg_ref[...]
    zg_ref[...] = _sigmoid(gates).astype(BF16)
    zf_ref[...] = jnp.dot(h, wf_ref[...], preferred_element_type=F32) + bf_ref[...]


def _in_proj(x, norm_g, mod, w, b):
    def full(a):
        return pl.BlockSpec(a.shape, lambda i: (0, 0), pipeline_mode=pl.Buffered(1))

    def rows(n):
        return pl.BlockSpec((TM_PROJ, n), lambda i: (i, 0))

    cache_spec = pl.BlockSpec((TM_PROJ, D_A), lambda i: (jnp.minimum(i, PROJ_CTX_TILES - 1), 0))
    cache_shape = jax.ShapeDtypeStruct((T_CTX, D_A), F32)

    return pl.pallas_call(
        _in_proj_kernel,
        grid=(T_ALL // TM_PROJ,),
        in_specs=[rows(D_MODEL),
                  pl.BlockSpec((1, D_MODEL), lambda i: (0, 0)),
                  pl.BlockSpec((None, 6, D_MODEL), lambda i: (_seg_of_tile(i, TM_PROJ), 0, 0)),
                  full(w[0]), full(b[0]), full(w[1]), full(b[1]), full(w[2]), full(b[2])],
        out_specs=[rows(N_ZB), rows(N_ZG), rows(N_ZF), cache_spec, cache_spec],
        out_shape=[jax.ShapeDtypeStruct((T_ALL, N_ZB), BF16), jax.ShapeDtypeStruct((T_ALL, N_ZG), BF16),
                   jax.ShapeDtypeStruct((T_ALL, N_ZF), F32), cache_shape, cache_shape],
        compiler_params=_cparams(("arbitrary",)),
        name="in_proj",
    )(x, norm_g, mod, w[0], b[0], w[1], b[1], w[2], b[2])


HEAD_PAIR = 2 * HEAD_DIM_A
ATT_SCALE = HEAD_DIM_A ** -0.5


def _pair_queries(q2):
    lo = lax.broadcasted_iota(jnp.int32, (1, HEAD_PAIR), 1) < HEAD_DIM_A
    q2 = q2 * ATT_SCALE
    zero = jnp.zeros_like(q2)
    return lo, jnp.concatenate([jnp.where(lo, q2, zero), jnp.where(lo, zero, q2)], axis=0)


def _unpair(lo, o_stacked):
    rows = o_stacked.shape[0] // 2
    return jnp.where(lo, o_stacked[:rows], o_stacked[rows:])


def _ctx_attn_kernel(q_ref, k_ref, v_ref, o_ref):
    for hp in range(N_HEADS_A // 2):
        sl = slice(hp * HEAD_PAIR, (hp + 1) * HEAD_PAIR)
        lo, qs = _pair_queries(q_ref[:, sl])
        s = _dot_nt(qs, k_ref[:, sl])
        p = jnp.exp(s - jnp.max(s, axis=-1, keepdims=True))
        l = jnp.sum(p, axis=-1, keepdims=True)
        o = jnp.dot(p.astype(BF16), v_ref[:, sl], preferred_element_type=F32) / l
        o_ref[:, sl] = _unpair(lo, o).astype(o_ref.dtype)


def _ctx_attention(zb):
    def spec(cb):
        return pl.BlockSpec((SEQ, D_A), lambda b: (b, cb))

    return pl.pallas_call(
        _ctx_attn_kernel,
        grid=(BATCH,),
        in_specs=[spec(QB_QA), spec(QB_KA), spec(QB_VA)],
        out_specs=pl.BlockSpec((SEQ, D_A), lambda b: (b, 0)),
        out_shape=jax.ShapeDtypeStruct((T_CTX, D_A), BF16),
        compiler_params=_cparams(("arbitrary",)),
        name="ctx_attention",
    )(zb, zb, zb)


NQ_ROWS = 4
NW_ROWS = 12
NQ_BLOCKS = GRID_ROWS // NQ_ROWS
NQ_TOK = NQ_ROWS * GRID_W
NW_TOK = NW_ROWS * GRID_W


def _window_row(qb, xp):
    return xp.clip(qb * NQ_ROWS - WIN_ROWS // 2, 0, GRID_ROWS - NW_ROWS)


N_REL_ROWS = 2 * WIN_ROWS - 1


def _natt_rel_rows():
    r = np.arange(NQ_BLOCKS)[:, None, None] * NQ_ROWS + np.arange(NQ_ROWS)[None, :, None]
    krow = _window_row(np.arange(NQ_BLOCKS), np)[:, None, None] + np.arange(NW_ROWS)[None, None, :]
    rs = np.clip(r - WIN_ROWS // 2, 0, GRID_ROWS - WIN_ROWS)
    assert ((rs >= krow[:, :, :1]) & (rs + WIN_ROWS <= krow[:, :, -1:] + 1)).all()
    return np.where((krow >= rs) & (krow < rs + WIN_ROWS), krow - r + WIN_ROWS - 1, N_REL_ROWS)


NATT_REL = _natt_rel_rows()


def _natt_kernel(q_ref, k_ref, v_ref, kc_ref, vc_ref, toe_ref, o_ref, bias_scr):
    qb = pl.program_id(0)

    @pl.when(pl.program_id(1) == 0)
    def _():
        for v in range(NQ_BLOCKS):
            @pl.when(qb == v)
            def _(v=v):
                for h in range(N_HEADS_A):
                    for rq in range(NQ_ROWS):
                        for kr in range(NW_ROWS):
                            half = (kr % 2) * GRID_W
                            bias_scr[h, rq * GRID_W:(rq + 1) * GRID_W, kr * GRID_W:(kr + 1) * GRID_W] = (
                                toe_ref[h, int(NATT_REL[v, rq, kr]), :, half:half + GRID_W])

    start = pl.multiple_of(_window_row(qb, jnp) * GRID_W, NQ_TOK)
    band = NW_TOK
    for hp in range(N_HEADS_A // 2):
        sl = slice(hp * HEAD_PAIR, (hp + 1) * HEAD_PAIR)
        lo, qs = _pair_queries(q_ref[:, sl])
        bias = bias_scr[2 * hp:2 * hp + 2].reshape(2 * NQ_TOK, band)
        s_loc = _dot_nt(qs, k_ref[pl.ds(start, band), sl]) + bias
        s_ctx = _dot_nt(qs, kc_ref[:, sl])
        m = jnp.maximum(jnp.max(s_loc, axis=-1, keepdims=True), jnp.max(s_ctx, axis=-1, keepdims=True))
        p_loc = jnp.exp(s_loc - m)
        p_ctx = jnp.exp(s_ctx - m)
        l = jnp.sum(p_loc, axis=-1, keepdims=True) + jnp.sum(p_ctx, axis=-1, keepdims=True)
        o = (jnp.dot(p_loc.astype(BF16), v_ref[pl.ds(start, band), sl], preferred_element_type=F32)
             + jnp.dot(p_ctx.astype(BF16), vc_ref[:, sl], preferred_element_type=F32))
        o_ref[:, sl] = _unpair(lo, o / l).astype(o_ref.dtype)


def _natt_bias(rpb_l):
    qc = np.arange(GRID_W)
    kc = np.arange(GRID_W)
    cs = np.clip(qc - WIN_COLS // 2, 0, GRID_W - WIN_COLS)
    ok = (kc[None, :] >= cs[:, None]) & (kc[None, :] < cs[:, None] + WIN_COLS)
    dc = np.clip(kc[None, :] - qc[:, None] + WIN_COLS - 1, 0, 2 * WIN_COLS - 2)
    pick = (dc[None] == np.arange(2 * WIN_COLS - 1)[:, None, None]).astype(np.float32)
    toe = jnp.einsum('hdc,cqk->hdqk', rpb_l, jnp.asarray(pick), precision=HIGHEST)
    toe = jnp.where(jnp.asarray(ok)[None, None], toe, NEG)
    toe = jnp.concatenate([toe, jnp.full((N_HEADS_A, 1, GRID_W, GRID_W), NEG, F32)], axis=1)
    return jnp.concatenate([toe, toe], axis=-1)


def _natt(zb, cache_k, cache_v, toe, layer):
    lat0 = T_CTX // DEC_SEQ
    row0 = T_CTX // NQ_TOK
    return pl.pallas_call(
        _natt_kernel,
        grid=(NQ_BLOCKS, DEC_BATCH),
        in_specs=[pl.BlockSpec((NQ_TOK, D_A), lambda qb, b: (row0 + b * NQ_BLOCKS + qb, QB_QA)),
                  pl.BlockSpec((DEC_SEQ, D_A), lambda qb, b: (lat0 + b, QB_KA)),
                  pl.BlockSpec((DEC_SEQ, D_A), lambda qb, b: (lat0 + b, QB_VA)),
                  pl.BlockSpec((None, None, PAST_LEN, D_A), lambda qb, b: (b, layer, 0, 0)),
                  pl.BlockSpec((None, None, PAST_LEN, D_A), lambda qb, b: (b, layer, 0, 0)),
                  pl.BlockSpec((N_HEADS_A, N_REL_ROWS + 1, GRID_W, 2 * GRID_W), lambda qb, b: (0, 0, 0, 0))],
        out_specs=pl.BlockSpec((NQ_TOK, D_A), lambda qb, b: (b * NQ_BLOCKS + qb, 0)),
        out_shape=jax.ShapeDtypeStruct((T_LAT, D_A), BF16),
        scratch_shapes=[pltpu.VMEM((N_HEADS_A, NQ_TOK, NW_TOK), F32)],
        compiler_params=_cparams(("arbitrary", "arbitrary")),
        name="nbr_attention",
    )(zb, zb, zb, cache_k, cache_v, toe)


CONV_HALO = 16
CONV_ROWS = 64


SUBLANES = 8


def _conv_kernel(u_ref, g_ref, w_ref, b_ref, lg_ref, lb_ref, o_ref, pad_scr, sh_scr, *, seq):
    zeros = jnp.zeros((CONV_HALO, D_CONV), F32)
    pad_scr[0:CONV_HALO, :] = zeros
    pad_scr[CONV_HALO + seq:2 * CONV_HALO + seq, :] = zeros
    pad_scr[CONV_HALO:CONV_HALO + seq, :] = u_ref[...] * _sigmoid(g_ref[...])
    n_sh = seq + 2 * CONV_HALO - SUBLANES
    for s in range(SUBLANES):
        sh_scr[s] = pad_scr[s:s + n_sh, :]
    first = CONV_HALO - CONV_WIDTH // 2
    for c in range(seq // CONV_ROWS):
        base = c * CONV_ROWS
        acc = jnp.broadcast_to(b_ref[...], (CONV_ROWS, D_CONV))
        for j in range(CONV_WIDTH):
            q, s = divmod(first + j, SUBLANES)
            row0 = base + q * SUBLANES
            acc = acc + sh_scr[s, row0:row0 + CONV_ROWS, :] * w_ref[j:j + 1, :]
        mu = jnp.mean(acc, axis=-1, keepdims=True)
        xc = acc - mu
        var = jnp.mean(xc * xc, axis=-1, keepdims=True)
        y = xc * lax.rsqrt(var + LN_EPS) * lg_ref[...] + lb_ref[...]
        o_ref[base:base + CONV_ROWS, :] = (y * _sigmoid(y)).astype(o_ref.dtype)


def _conv(z, w_dw, b_dw, ln_g, ln_b, seq, n_seq, row_block0):
    def vec():
        return pl.BlockSpec((1, D_CONV), lambda b: (0, 0))

    return pl.pallas_call(
        functools.partial(_conv_kernel, seq=seq),
        grid=(n_seq,),
        in_specs=[pl.BlockSpec((seq, D_CONV), lambda b: (row_block0 + b, ZF_CU)),
                  pl.BlockSpec((seq, D_CONV), lambda b: (row_block0 + b, ZF_CG)),
                  pl.BlockSpec((CONV_WIDTH, D_CONV), lambda b: (0, 0)),
                  vec(), vec(), vec()],
        out_specs=pl.BlockSpec((seq, D_CONV), lambda b: (b, 0)),
        out_shape=jax.ShapeDtypeStruct((n_seq * seq, D_CONV), BF16),
        scratch_shapes=[pltpu.VMEM((seq + 2 * CONV_HALO, D_CONV), F32),
                        pltpu.VMEM((SUBLANES, seq + 2 * CONV_HALO - SUBLANES, D_CONV), F32)],
        compiler_params=_cparams(("arbitrary",)),
        name="conformer_conv",
    )(z, z, w_dw, b_dw, ln_g, ln_b)


N_STREAM = 2 * N_HEADS_M
MCHUNK = 256


PAIR_M = 2 * HEAD_DIM_M
N_PAIR = N_STREAM // 2


def _per_head_rows(r):
    return jnp.concatenate([jnp.broadcast_to(r[0:1], (HEAD_DIM_M, r.shape[1])),
                            jnp.broadcast_to(r[1:2], (HEAD_DIM_M, r.shape[1]))], axis=0)


def _mlstm_kernel(q_ref, k_ref, v_ref, om_ref, gt_ref, c0_ref, n0_ref, m0_ref, ng_ref,
                  y_ref, c_out, n_out, m_out, hf_scr, hb_scr, c_scr, n_scr, m_scr, cbd_scr, vbd_scr,
                  rows_scr, acol_scr, *, seq):
    nc = seq // MCHUNK
    c_scr[...] = c0_ref[...]
    n_scr[...] = n0_ref[...]
    m_scr[...] = m0_ref[...]
    cbd_scr[...] = jnp.zeros_like(cbd_scr)
    vbd_scr[...] = jnp.zeros_like(vbd_scr)
    for s_id in range(N_STREAM):
        blk = slice((s_id % 2) * HEAD_DIM_M, (s_id % 2 + 1) * HEAD_DIM_M)
        cbd_scr[s_id // 2, blk, blk] = c0_ref[s_id].astype(BF16)
    first_head = lax.broadcasted_iota(jnp.int32, (1, PAIR_M), 1) < HEAD_DIM_M
    rows = lax.broadcasted_iota(jnp.int32, (MCHUNK, MCHUNK), 0)
    cols = lax.broadcasted_iota(jnp.int32, (MCHUNK, MCHUNK), 1)
    tris = ((cols <= rows).astype(F32), (cols >= rows).astype(F32))
    reach = (rows <= cols, rows >= cols)
    lane = lax.broadcasted_iota(jnp.int32, (1, MCHUNK), 1)
    kscale = HEAD_DIM_M ** -0.5

    for cc in range(nc):
        g = gt_ref[cc * MCHUNK:(cc + 1) * MCHUNK, :]
        lf = jnp.minimum(g, 0.0) - jnp.log(1.0 + jnp.exp(-jnp.abs(g)))
        g_t = g.T
        for d in range(2):
            cum = jnp.dot(tris[d], lf, precision=HIGHEST, preferred_element_type=F32)
            cum_t = cum.T
            i0 = 2 * N_HEADS_M * d
            i_rows = g_t[i0:i0 + N_HEADS_M, :]
            b_rows = cum_t[i0 + N_HEADS_M:i0 + 2 * N_HEADS_M, :]
            reach_max = i_rows - b_rows
            for step in (1 << s for s in range(MCHUNK.bit_length() - 1)):
                if d == 0:
                    shifted = jnp.where(lane >= step, pltpu.roll(reach_max, step, axis=1), NEG)
                else:
                    shifted = jnp.where(lane < MCHUNK - step, pltpu.roll(reach_max, MCHUNK - step, axis=1), NEG)
                reach_max = jnp.maximum(reach_max, shifted)
            rows_scr[2 * cc + d, 0] = i_rows
            rows_scr[2 * cc + d, 1] = b_rows
            rows_scr[2 * cc + d, 2] = reach_max
            acol_scr[2 * cc + d] = g - pltpu.roll(cum, cum.shape[1] - N_HEADS_M, axis=1)

    def chunk_step(c, carry):
        for d in range(2):
            cidx = c if d == 0 else nc - 1 - c
            off = pl.multiple_of(cidx * MCHUNK, MCHUNK)
            last = MCHUNK - 1 if d == 0 else 0
            i0 = 2 * N_HEADS_M * d
            heads = slice(N_HEADS_M * d, N_HEADS_M * (d + 1))
            i_rows = rows_scr[2 * cidx + d, 0]
            b_rows = rows_scr[2 * cidx + d, 1]
            reach_max = rows_scr[2 * cidx + d, 2]
            a_cols = acol_scr[2 * cidx + d]
            m_prevs = m_scr[heads, 0:1]
            inters = b_rows + m_prevs
            m_ts = jnp.maximum(inters, b_rows + reach_max)
            w_inters = jnp.exp(inters - m_ts)
            floors = jnp.exp(-m_ts)
            b_lasts = b_rows[:, last:last + 1]
            m_news = m_ts[:, last:last + 1]
            w_prevs = jnp.exp(b_lasts + m_prevs - m_news)
            w_srcs = kscale * jnp.exp(b_lasts - b_rows + i_rows - m_news)
            m_scr[heads, :] = jnp.broadcast_to(m_news, (N_HEADS_M, HEAD_DIM_M))
            for hp in range(N_HEADS_M // 2):
                pid = (N_HEADS_M // 2) * d + hp
                hh = slice(2 * hp, 2 * hp + 2)
                ps = slice(hp * PAIR_M, (hp + 1) * PAIR_M)
                q2 = q_ref[pl.ds(off, MCHUNK), ps]
                k2 = k_ref[pl.ds(off, MCHUNK), ps]
                v2_t = v_ref[pl.ds(off, MCHUNK), ps].astype(F32).T
                k_zero = jnp.zeros_like(k2)
                k_stack = jnp.concatenate([jnp.where(first_head, k2, k_zero),
                                           jnp.where(first_head, k_zero, k2)], axis=0)
                decay = jnp.concatenate(
                    [jnp.exp(jnp.where(reach[d], b_rows[h:h + 1] + a_cols[:, i0 + h:i0 + h + 1], NEG)
                             - m_ts[h:h + 1]) for h in (2 * hp, 2 * hp + 1)], axis=0)
                s_t = _dot_nt(k_stack, q2) * (kscale * decay)
                col_sums = jnp.concatenate([jnp.sum(s_t[:MCHUNK], axis=0, keepdims=True),
                                            jnp.sum(s_t[MCHUNK:], axis=0, keepdims=True)], axis=0)
                n2 = n_scr[N_HEADS_M * d + 2 * hp:N_HEADS_M * d + 2 * hp + 2, :]
                n_zero = jnp.zeros((1, HEAD_DIM_M), F32)
                n_mat = jnp.concatenate([jnp.concatenate([n2[0:1], n_zero], axis=1),
                                         jnp.concatenate([n_zero, n2[1:2]], axis=1),
                                         jnp.zeros((SUBLANES - 2, PAIR_M), F32)], axis=0)
                n_q = _dot_nt(n_mat.astype(BF16), q2)[0:2, :]
                den = w_inters[hh] * n_q + col_sums
                inv = 1.0 / jnp.maximum(jnp.abs(den), floors[hh])
                vbd_scr[pid, :HEAD_DIM_M, :MCHUNK] = v2_t[:HEAD_DIM_M].astype(BF16)
                vbd_scr[pid, HEAD_DIM_M:, MCHUNK:] = v2_t[HEAD_DIM_M:].astype(BF16)
                num_t = (_per_head_rows(w_inters[hh]) * _dot_nt(cbd_scr[pid], q2)
                         + jnp.dot(vbd_scr[pid], s_t.astype(BF16), preferred_element_type=F32))
                h_t = num_t * _per_head_rows(inv)
                if d == 0:
                    hf_scr[cidx, ps, :] = h_t
                else:
                    hb_scr[cidx, ps, :] = h_t
                upd = jnp.dot((v2_t * _per_head_rows(w_srcs[hh])).astype(BF16), k2,
                              preferred_element_type=F32)
                w_mat = jnp.concatenate([w_srcs[hh], jnp.zeros((SUBLANES - 2, MCHUNK), F32)], axis=0)
                n_upd = jnp.dot(w_mat.astype(BF16), k2, preferred_element_type=F32)
                for j in range(2):
                    h = 2 * hp + j
                    s_id = N_HEADS_M * d + h
                    blk = slice(j * HEAD_DIM_M, (j + 1) * HEAD_DIM_M)
                    c_new = w_prevs[h:h + 1] * c_scr[s_id] + upd[blk, blk]
                    c_scr[s_id] = c_new
                    cbd_scr[pid, blk, blk] = c_new.astype(BF16)
                    n_scr[s_id:s_id + 1, :] = w_prevs[h:h + 1] * n2[j:j + 1] + n_upd[j:j + 1, blk]
        return carry

    lax.fori_loop(0, nc, chunk_step, 0)

    for c in range(nc):
        ts = slice(c * MCHUNK, (c + 1) * MCHUNK)
        for h in range(N_HEADS_M):
            hs = slice(h * HEAD_DIM_M, (h + 1) * HEAD_DIM_M)
            hsum = hf_scr[c, hs, :] + hb_scr[c, hs, :]
            mu = jnp.mean(hsum, axis=0, keepdims=True)
            xc = hsum - mu
            var = jnp.mean(xc * xc, axis=0, keepdims=True)
            hn = (xc * lax.rsqrt(var + LN_EPS)).T
            y_ref[ts, hs] = (_sigmoid(om_ref[ts, hs]) * (hn * ng_ref[:, hs])).astype(y_ref.dtype)
    c_out[...] = c_scr[...]
    n_out[...] = n_scr[...]
    m_out[...] = m_scr[...]


def _mlstm(zb, zf, c0, n0, m0, norm_g, seq, n_seq, row_block0, state_map):
    lead = len(state_map(0))

    def zspec(cb):
        return pl.BlockSpec((seq, D_M), lambda b: (row_block0 + b, cb))

    def sspec(tail):
        return pl.BlockSpec((None,) * lead + tail, lambda b: state_map(b) + (0,) * len(tail))

    return pl.pallas_call(
        functools.partial(_mlstm_kernel, seq=seq),
        grid=(n_seq,),
        in_specs=[zspec(QB_QM), zspec(QB_KM), zspec(QB_VM), zspec(ZF_OM),
                  pl.BlockSpec((seq, 128), lambda b: (row_block0 + b, ZF_GATES)),
                  sspec((N_STREAM, HEAD_DIM_M, HEAD_DIM_M)),
                  sspec((N_STREAM, HEAD_DIM_M)),
                  sspec((N_STREAM, HEAD_DIM_M)),
                  pl.BlockSpec((1, D_M), lambda b: (0, 0))],
        out_specs=[pl.BlockSpec((seq, D_M), lambda b: (b, 0)),
                   pl.BlockSpec((None, N_STREAM, HEAD_DIM_M, HEAD_DIM_M), lambda b: (b, 0, 0, 0)),
                   pl.BlockSpec((None, N_STREAM, HEAD_DIM_M), lambda b: (b, 0, 0)),
                   pl.BlockSpec((None, N_STREAM, HEAD_DIM_M), lambda b: (b, 0, 0))],
        out_shape=[jax.ShapeDtypeStruct((n_seq * seq, D_M), BF16),
                   jax.ShapeDtypeStruct((n_seq, N_STREAM, HEAD_DIM_M, HEAD_DIM_M), F32),
                   jax.ShapeDtypeStruct((n_seq, N_STREAM, HEAD_DIM_M), F32),
                   jax.ShapeDtypeStruct((n_seq, N_STREAM, HEAD_DIM_M), F32)],
        scratch_shapes=[pltpu.VMEM((seq // MCHUNK, D_M, MCHUNK), F32),
                        pltpu.VMEM((seq // MCHUNK, D_M, MCHUNK), F32),
                        pltpu.VMEM((N_STREAM, HEAD_DIM_M, HEAD_DIM_M), F32),
                        pltpu.VMEM((N_STREAM, HEAD_DIM_M), F32),
                        pltpu.VMEM((N_STREAM, HEAD_DIM_M), F32),
                        pltpu.VMEM((N_PAIR, PAIR_M, PAIR_M), BF16),
                        pltpu.VMEM((N_PAIR, PAIR_M, 2 * MCHUNK), BF16),
                        pltpu.VMEM((2 * (seq // MCHUNK), 3, N_HEADS_M, MCHUNK), F32),
                        pltpu.VMEM((2 * (seq // MCHUNK), MCHUNK, 128), F32)],
        compiler_params=_cparams(("arbitrary",)),
        name="mlstm",
    )(zb, zb, zb, zf, zf, c0, n0, m0, norm_g)


N_CTX_TILES = T_CTX // TM_TOK


def _merge_kernel(x_ref, mod_ref, ya_c, ya_l, yc_c, yc_l, ym_c, ym_l, ga_ref, gc_ref, gm_ref,
                  wa_ref, wc_ref, wm_ref, wo_ref, o_ref):
    is_ctx = pl.program_id(0) < N_CTX_TILES

    def branch(y_ctx, y_lat, g_ref, w_ref):
        y = jnp.where(is_ctx, y_ctx[...], y_lat[...])
        return g_ref[...].astype(F32) * jnp.dot(y, w_ref[...], preferred_element_type=F32)

    merged = (branch(ya_c, ya_l, ga_ref, wa_ref) + branch(yc_c, yc_l, gc_ref, wc_ref)
              + branch(ym_c, ym_l, gm_ref, wm_ref))
    mix = jnp.dot(merged.astype(BF16), wo_ref[...], preferred_element_type=F32)
    o_ref[...] = x_ref[...] + mod_ref[2:3, :] * mix


def _merge(x, mod, ya, yc, ym, zg, w_pa, w_pc, w_pm, w_out):
    def rows(width, cb=0):
        return pl.BlockSpec((TM_TOK, width), lambda i: (i, cb))

    def ctx_rows(width):
        return pl.BlockSpec((TM_TOK, width), lambda i: (jnp.minimum(i, N_CTX_TILES - 1), 0))

    def lat_rows(width):
        return pl.BlockSpec((TM_TOK, width), lambda i: (jnp.maximum(i - N_CTX_TILES, 0), 0))

    def full(shape):
        return pl.BlockSpec(shape, lambda i: (0, 0))

    return pl.pallas_call(
        _merge_kernel,
        grid=(T_ALL // TM_TOK,),
        in_specs=[rows(D_MODEL),
                  pl.BlockSpec((None, 6, D_MODEL), lambda i: (_seg_of_tile(i, TM_TOK), 0, 0)),
                  ctx_rows(D_A), lat_rows(D_A), ctx_rows(D_CONV), lat_rows(D_CONV),
                  ctx_rows(D_M), lat_rows(D_M),
                  rows(D_MODEL, ZG_GA), rows(D_MODEL, ZG_GC), rows(D_MODEL, ZG_GM),
                  full((D_A, D_MODEL)), full((D_CONV, D_MODEL)), full((D_M, D_MODEL)),
                  full((D_MODEL, D_MODEL))],
        out_specs=rows(D_MODEL),
        out_shape=jax.ShapeDtypeStruct((T_ALL, D_MODEL), F32),
        compiler_params=_cparams(("arbitrary",)),
        name="merge",
    )(x, mod, ya[0], ya[1], yc[0], yc[1], ym[0], ym[1], zg, zg, zg, w_pa, w_pc, w_pm, w_out)


def _route_sort_kernel(x_ref, g_ref, mod_ref, wr_ref, br_ref, xt_ref, pos_ref, gate_ref, nch_ref, seg_ref):
    h = _normmod(x_ref[...], g_ref[...], mod_ref[...], 3, 4)
    hb = h.astype(BF16)
    logits = _dot_nt(wr_ref[...].astype(BF16), hb) + br_ref[...]
    e_iota = lax.broadcasted_iota(jnp.int32, (N_EXPERTS, TM_MOE), 0).astype(F32)
    sels, vals = [], []
    l = logits
    for k in range(TOP_K):
        m = jnp.max(l, axis=0, keepdims=True)
        idx = jnp.min(jnp.where(l == m, e_iota, float(N_EXPERTS)), axis=0, keepdims=True)
        sel = e_iota == idx
        vals.append(m)
        sels.append(sel)
        l = jnp.where(sel, -jnp.inf, l)
    exps = [jnp.exp(v - vals[0]) for v in vals]
    tot = exps[0] + exps[1] + exps[2] + exps[3]
    onehot = jnp.zeros((N_EXPERTS, TM_MOE), F32)
    for k in range(TOP_K):
        gate_ref[k:k + 1, :] = exps[k] / tot
        onehot = onehot + sels[k].astype(F32)
    gate_ref[TOP_K:8, :] = jnp.zeros((8 - TOP_K, TM_MOE), F32)

    cnt = jnp.sum(onehot, axis=1, keepdims=True)
    nch = jnp.floor((cnt + (CHUNK_ROWS - 1)) / CHUNK_ROWS)
    ei = lax.broadcasted_iota(jnp.int32, (N_EXPERTS, N_EXPERTS), 0)
    ej = lax.broadcasted_iota(jnp.int32, (N_EXPERTS, N_EXPERTS), 1)
    seg = jnp.dot((ej < ei).astype(F32), jnp.broadcast_to(nch, (N_EXPERTS, 128)), precision=HIGHEST,
                  preferred_element_type=F32)
    nch_ref[...] = jnp.broadcast_to(nch, (N_EXPERTS, 128)).astype(jnp.int32)
    seg_ref[...] = seg.astype(jnp.int32)

    t_src = lax.broadcasted_iota(jnp.int32, (TM_MOE, TM_MOE), 0)
    t_dst = lax.broadcasted_iota(jnp.int32, (TM_MOE, TM_MOE), 1)
    before = (t_src < t_dst).astype(BF16)
    row_of = (seg[:, 0:1] * CHUNK_ROWS
              + jnp.dot(onehot.astype(BF16), before, preferred_element_type=F32))
    q_iota = lax.broadcasted_iota(jnp.int32, (Q_TILE, TM_MOE), 0)
    perm = jnp.zeros((Q_TILE, TM_MOE), F32)
    for k in range(TOP_K):
        q_k = jnp.sum(jnp.where(sels[k], row_of, 0.0), axis=0, keepdims=True).astype(jnp.int32)
        pos_ref[k:k + 1, :] = q_k
        perm = jnp.where(q_iota == q_k, 1.0, perm)
    pos_ref[TOP_K:8, :] = jnp.zeros((8 - TOP_K, TM_MOE), jnp.int32)
    xt_ref[...] = jnp.dot(perm.astype(BF16), hb, preferred_element_type=F32).astype(BF16)


def _route_sort(x, norm_g, mod, w_rt, b_r):
    tspec = pl.BlockSpec((8, TM_MOE), lambda i: (0, i))
    mspec = pl.BlockSpec((None, N_EXPERTS, 128), lambda i: (i, 0, 0))
    meta = jax.ShapeDtypeStruct((N_TILES, N_EXPERTS, 128), jnp.int32)
    return pl.pallas_call(
        _route_sort_kernel,
        grid=(N_TILES,),
        in_specs=[pl.BlockSpec((TM_MOE, D_MODEL), lambda i: (i, 0)),
                  pl.BlockSpec((1, D_MODEL), lambda i: (0, 0)),
                  pl.BlockSpec((None, 6, D_MODEL), lambda i: (_seg_of_tile(i, TM_MOE), 0, 0)),
                  pl.BlockSpec((N_EXPERTS, D_MODEL), lambda i: (0, 0)),
                  pl.BlockSpec((N_EXPERTS, 1), lambda i: (0, 0))],
        out_specs=[pl.BlockSpec((Q_TILE, D_MODEL), lambda i: (i, 0)), tspec, tspec, mspec, mspec],
        out_shape=[jax.ShapeDtypeStruct((N_TILES * Q_TILE, D_MODEL), BF16),
                   jax.ShapeDtypeStruct((8, T_ALL), jnp.int32), jax.ShapeDtypeStruct((8, T_ALL), F32),
                   meta, meta],
        compiler_params=_cparams(("arbitrary",)),
        name="moe_route_sort",
    )(x, norm_g, mod, w_rt, b_r)


def _expert_kernel(nch_ref, seg_ref, wgu_ref, bgu_ref, wd_ref, bd_ref, xt_ref, yt_ref,
                   wgu_scr, wd_scr, xbuf, ybuf, row_scr, gstart_scr, gsem, ssem):
    del xt_ref
    e = pl.program_id(0)

    def src_row(row):
        return pl.multiple_of(jnp.where(row >= 0, row, READ_SPARE), CHUNK_ROWS)

    def dst_row(row, slot, c):
        spare = c * Q_TILE + jnp.where(slot == 0, Q_TILE - CHUNK_ROWS, Q_TILE - 2 * CHUNK_ROWS)
        return pl.multiple_of(jnp.where(row >= 0, row, spare), CHUNK_ROWS)

    def chunk_rows(c):
        return slice(c * CHUNK_ROWS, (c + 1) * CHUNK_ROWS)

    def start_in(g):
        slot = g % 2
        for c in range(CPG):
            row = row_scr[g * CPG + c]
            pltpu.make_async_copy(yt_ref.at[pl.ds(src_row(row), CHUNK_ROWS), :], xbuf.at[slot, chunk_rows(c), :],
                                  gsem.at[slot]).start()

    def start_out(g, slot):
        for c in range(CPG):
            row = row_scr[g * CPG + c]
            pltpu.make_async_copy(ybuf.at[slot, chunk_rows(c), :],
                                  yt_ref.at[pl.ds(dst_row(row, slot, c), CHUNK_ROWS), :], ssem.at[slot]).start()

    def wait_in(slot):
        pltpu.make_async_copy(yt_ref.at[pl.ds(0, E_GROUP), :], xbuf.at[slot], gsem.at[slot]).wait()

    def wait_out(slot):
        pltpu.make_async_copy(ybuf.at[slot], yt_ref.at[pl.ds(0, E_GROUP), :], ssem.at[slot]).wait()

    @pl.when(e == 0)
    def _():
        def per_expert(ee, cnt):
            gstart_scr[ee] = cnt // CPG

            def per_tile(t, cnt):
                first = (t * CH_PER_TILE + seg_ref[t * N_EXPERTS + ee]) * CHUNK_ROWS

                def per_chunk(j, cnt):
                    row_scr[cnt] = first + j * CHUNK_ROWS
                    return cnt + 1

                return lax.fori_loop(0, nch_ref[t * N_EXPERTS + ee], per_chunk, cnt)

            cnt = lax.fori_loop(0, N_TILES, per_tile, cnt)
            padded = (cnt + CPG - 1) // CPG * CPG

            def pad(i, carry):
                row_scr[i] = -1
                return carry

            lax.fori_loop(cnt, padded, pad, 0)
            return padded

        total = lax.fori_loop(0, N_EXPERTS, per_expert, 0)
        gstart_scr[N_EXPERTS] = total // CPG

        def pad(i, carry):
            row_scr[i] = -1
            return carry

        lax.fori_loop(total, total + CPG, pad, 0)
        ybuf[...] = jnp.zeros_like(ybuf)
        start_out(total // CPG, 0)
        start_out(total // CPG, 1)
        start_in(0)

    g_first = gstart_scr[e]
    g_end = gstart_scr[e + 1]

    @pl.when(g_end > g_first)
    def _():
        wgu_scr[...] = wgu_ref[...].astype(BF16)
        wd_scr[...] = wd_ref[...].astype(BF16)

    def group_step(g, carry):
        slot = g % 2
        start_in(g + 1)
        wait_in(slot)
        wait_out(slot)
        hgu = jnp.dot(xbuf[slot], wgu_scr[...], preferred_element_type=F32) + bgu_ref[...]
        h_glu = jnp.minimum(hgu[:, :D_EXPERT], SWIGLU_LIMIT)
        h_lin = jnp.clip(hgu[:, D_EXPERT:], -SWIGLU_LIMIT, SWIGLU_LIMIT)
        act = (h_lin + 1.0) * (h_glu * _sigmoid(SWIGLU_ALPHA * h_glu))
        y = jnp.dot(act.astype(BF16), wd_scr[...], preferred_element_type=F32) + bd_ref[...]
        ybuf[slot] = y.astype(BF16)
        start_out(g, slot)
        return carry

    lax.fori_loop(g_first, g_end, group_step, 0)

    @pl.when(e == N_EXPERTS - 1)
    def _():
        wait_in(gstart_scr[N_EXPERTS] % 2)
        wait_out(0)
        wait_out(1)


def _experts(nch_flat, seg_flat, xt, w_gu, b_gu, w_down, b_down, layer):
    return pl.pallas_call(
        _expert_kernel,
        grid_spec=pltpu.PrefetchScalarGridSpec(
            num_scalar_prefetch=2,
            grid=(N_EXPERTS,),
            in_specs=[pl.BlockSpec((None, None, D_MODEL, 2 * D_EXPERT), lambda e, n, s: (layer, e, 0, 0)),
                      pl.BlockSpec((None, None, 1, 2 * D_EXPERT), lambda e, n, s: (layer, e, 0, 0)),
                      pl.BlockSpec((None, None, D_EXPERT, D_MODEL), lambda e, n, s: (layer, e, 0, 0)),
                      pl.BlockSpec((None, None, 1, D_MODEL), lambda e, n, s: (layer, e, 0, 0)),
                      pl.BlockSpec(memory_space=pl.ANY)],
            out_specs=pl.BlockSpec(memory_space=pl.ANY),
            scratch_shapes=[pltpu.VMEM((D_MODEL, 2 * D_EXPERT), BF16),
                            pltpu.VMEM((D_EXPERT, D_MODEL), BF16),
                            pltpu.VMEM((2, E_GROUP, D_MODEL), BF16),
                            pltpu.VMEM((2, E_GROUP, D_MODEL), BF16),
                            pltpu.SMEM((MAX_CHUNKS,), jnp.int32),
                            pltpu.SMEM((N_EXPERTS + 1,), jnp.int32),
                            pltpu.SemaphoreType.DMA((2,)),
                            pltpu.SemaphoreType.DMA((2,))]),
        out_shape=jax.ShapeDtypeStruct((N_TILES * Q_TILE, D_MODEL), BF16),
        input_output_aliases={6: 0},
        compiler_params=_cparams(("arbitrary",)),
        name="moe_experts",
    )(nch_flat, seg_flat, w_gu, b_gu, w_down, b_down, xt)


MOE_CTX_TILES = T_CTX // TM_MOE


def _combine_rows(x_ref, mod_ref, pos_ref, gate_ref, yt_ref):
    lane = lax.broadcasted_iota(jnp.int32, (TM_MOE, Q_TILE), 1)
    sel = jnp.zeros((TM_MOE, Q_TILE), F32)
    for k in range(TOP_K):
        sel = jnp.where(lane == pos_ref[:, k:k + 1], gate_ref[:, k:k + 1], sel)
    acc = jnp.dot(sel.astype(BF16), yt_ref[...], preferred_element_type=F32)
    return x_ref[...] + mod_ref[5:6, :] * acc


def _combine_kernel(x_ref, mod_ref, pos_ref, gate_ref, yt_ref, o_ref):
    o_ref[...] = _combine_rows(x_ref, mod_ref, pos_ref, gate_ref, yt_ref)


def _combine_final_kernel(x_ref, mod_ref, pos_ref, gate_ref, yt_ref, fg_ref, ctx_ref, lat_ref):
    y = _combine_rows(x_ref, mod_ref, pos_ref, gate_ref, yt_ref)
    y = y * lax.rsqrt(jnp.mean(y * y, axis=-1, keepdims=True) + RMS_EPS) * fg_ref[...]
    is_ctx = pl.program_id(0) < MOE_CTX_TILES

    @pl.when(is_ctx)
    def _():
        ctx_ref[...] = y

    @pl.when(jnp.logical_not(is_ctx))
    def _():
        lat_ref[...] = y


def _combine(x, mod, pos_t, gate_t, yt, final_g=None):
    tile = pl.BlockSpec((TM_MOE, D_MODEL), lambda i: (i, 0))
    in_specs = [tile,
                pl.BlockSpec((None, 6, D_MODEL), lambda i: (_seg_of_tile(i, TM_MOE), 0, 0)),
                pl.BlockSpec((TM_MOE, 8), lambda i: (i, 0)),
                pl.BlockSpec((TM_MOE, 8), lambda i: (i, 0)),
                pl.BlockSpec((Q_TILE, D_MODEL), lambda i: (i, 0))]
    if final_g is None:
        return pl.pallas_call(
            _combine_kernel, grid=(N_TILES,), in_specs=in_specs, out_specs=tile,
            out_shape=jax.ShapeDtypeStruct((T_ALL, D_MODEL), F32),
            compiler_params=_cparams(("arbitrary",)), name="moe_combine",
        )(x, mod, pos_t, gate_t, yt)
    return pl.pallas_call(
        _combine_final_kernel, grid=(N_TILES,),
        in_specs=in_specs + [pl.BlockSpec((1, D_MODEL), lambda i: (0, 0))],
        out_specs=[pl.BlockSpec((TM_MOE, D_MODEL), lambda i: (jnp.minimum(i, MOE_CTX_TILES - 1), 0)),
                   pl.BlockSpec((TM_MOE, D_MODEL), lambda i: (jnp.maximum(i - MOE_CTX_TILES, 0), 0))],
        out_shape=[jax.ShapeDtypeStruct((T_CTX, D_MODEL), F32), jax.ShapeDtypeStruct((T_LAT, D_MODEL), F32)],
        compiler_params=_cparams(("arbitrary",)), name="moe_combine_final",
    )(x, mod, pos_t, gate_t, yt, final_g)


def _moe(x, norm_g, mod, w_rt, b_r, w_gu, b_gu, w_down, b_down, layer, final_g=None):
    xt, pos, gate, nch, seg = _route_sort(x, norm_g, mod, w_rt, b_r)
    yt = _experts(nch[:, :, 0].reshape(-1), seg[:, :, 0].reshape(-1), xt, w_gu,
                  b_gu.reshape(DEPTH, N_EXPERTS, 1, 2 * D_EXPERT), w_down,
                  b_down.reshape(DEPTH, N_EXPERTS, 1, D_MODEL), layer)
    return _combine(x, mod, pos.T, gate.T, yt, final_g)


def _split_in_cols(w):
    conv0, mq0, om0 = 3 * D_A, 3 * D_A + 2 * D_CONV, 3 * D_A + 2 * D_CONV + 3 * D_M
    gates_end = GATE_OFF + N_GATE_M
    pad = jnp.zeros(w.shape[:-1] + (N_ZF - (2 * D_CONV + D_M + N_GATE_M),), w.dtype)
    zb = jnp.concatenate([w[..., :conv0], w[..., mq0:om0]], axis=-1)
    zg = w[..., gates_end:]
    zf = jnp.concatenate([w[..., conv0:mq0], w[..., om0:GATE_OFF], w[..., GATE_OFF:gates_end], pad], axis=-1)
    return zb, zg, zf


def kernel(x_prompt, x_sample, cache_k, cache_v, state_C, state_n, state_m, c, c_ctx, norm1_g, w_mod, b_mod, w_in, b_in, rpb, w_dw, b_dw, cln_g, cln_b, mnorm_g, w_pa, w_pc, w_pm, w_out, norm2_g, w_router, b_router, w_gu, b_gu, w_down, b_down, final_g):
    cond = jnp.concatenate([c_ctx[None, :], c, jnp.zeros((SEG_PAD - N_SEG, D_MODEL), F32)], axis=0)
    mod_all = _modulation(cond, w_mod, b_mod).reshape(DEPTH, SEG_PAD, 6, D_MODEL)

    x = jnp.concatenate([x_prompt.reshape(T_CTX, D_MODEL), x_sample.reshape(T_LAT, D_MODEL)], axis=0)
    ck = cache_k.reshape(DEC_BATCH, DEPTH, PAST_LEN, D_A).astype(BF16)
    cv = cache_v.reshape(DEC_BATCH, DEPTH, PAST_LEN, D_A).astype(BF16)
    lat_c0 = state_C.reshape(DEC_BATCH, DEPTH, N_STREAM, HEAD_DIM_M, HEAD_DIM_M)
    lat_n0 = state_n.reshape(DEC_BATCH, DEPTH, N_STREAM, HEAD_DIM_M)
    lat_m0 = jnp.broadcast_to(state_m.reshape(DEC_BATCH, DEPTH, N_STREAM, 1),
                              (DEC_BATCH, DEPTH, N_STREAM, HEAD_DIM_M))
    ctx_c0 = jnp.zeros((1, N_STREAM, HEAD_DIM_M, HEAD_DIM_M), F32)
    ctx_n0 = jnp.zeros((1, N_STREAM, HEAD_DIM_M), F32)
    ctx_m0 = jnp.full((1, N_STREAM, HEAD_DIM_M), -jnp.inf, F32)

    ks, vs, cs, ns, ms = [], [], [], [], []
    for l in range(DEPTH):
        mod = mod_all[l]
        g1 = norm1_g[l][None, :]
        zb, zg, zf, k_l, v_l = _in_proj(x, g1, mod, _split_in_cols(w_in[l].astype(BF16)),
                                        _split_in_cols(b_in[l][None, :]))
        ya = (_ctx_attention(zb), _natt(zb, ck, cv, _natt_bias(rpb[l]), l))
        conv_w = (w_dw[l], b_dw[l][None, :], cln_g[l][None, :], cln_b[l][None, :])
        yc = (_conv(zf, *conv_w, SEQ, BATCH, 0), _conv(zf, *conv_w, DEC_SEQ, DEC_BATCH, T_CTX // DEC_SEQ))
        ng = mnorm_g[l][None, :]
        ym_ctx, c_l, n_l, m_l = _mlstm(zb, zf, ctx_c0, ctx_n0, ctx_m0, ng, SEQ, BATCH, 0, lambda b: (0,))
        ym_lat, _, _, _ = _mlstm(zb, zf, lat_c0, lat_n0, lat_m0, ng, DEC_SEQ, DEC_BATCH, T_CTX // DEC_SEQ,
                                 lambda b: (b, l))
        x = _merge(x, mod, ya, yc, (ym_ctx, ym_lat), zg, w_pa[l].astype(BF16), w_pc[l].astype(BF16),
                   w_pm[l].astype(BF16), w_out[l].astype(BF16))
        x = _moe(x, norm2_g[l][None, :], mod, w_router[l].T, b_router[l][:, None],
                 w_gu, b_gu, w_down, b_down, l, final_g[None, :] if l == DEPTH - 1 else None)
        ks.append(k_l.reshape(BATCH, SEQ, N_HEADS_A, HEAD_DIM_A))
        vs.append(v_l.reshape(BATCH, SEQ, N_HEADS_A, HEAD_DIM_A))
        cs.append(c_l.reshape(BATCH, 2, N_HEADS_M, HEAD_DIM_M, HEAD_DIM_M))
        ns.append(n_l.reshape(BATCH, 2, N_HEADS_M, HEAD_DIM_M))
        ms.append(m_l[:, :, 0].reshape(BATCH, 2, N_HEADS_M))

    y_ctx, y_lat = x
    return (y_ctx.reshape(BATCH, SEQ, D_MODEL), y_lat.reshape(DEC_BATCH, DEC_SEQ, D_MODEL),
            jnp.stack(ks, axis=1), jnp.stack(vs, axis=1), jnp.stack(cs, axis=1),
            jnp.stack(ns, axis=1), jnp.stack(ms, axis=1))
```

```python
import functools

import numpy as np
import jax
import jax.numpy as jnp
from jax import lax
from jax.experimental import pallas as pl
from jax.experimental.pallas import tpu as pltpu

F32 = jnp.float32
BF16 = jnp.bfloat16
HIGHEST = lax.Precision.HIGHEST

D_MODEL = 1024
BATCH = 16
SEQ = 256
DEPTH = 2
DEC_BATCH = 8
DEC_SEQ = 1024
PAST_LEN = 512
GRID_W = 64
N_HEADS_A = 8
HEAD_DIM_A = 64
D_A = N_HEADS_A * HEAD_DIM_A
WIN_ROWS = 8
WIN_COLS = 16
D_CONV = 512
CONV_WIDTH = 31
N_HEADS_M = 4
HEAD_DIM_M = 128
D_M = N_HEADS_M * HEAD_DIM_M
N_GATE_M = 4 * N_HEADS_M
N_EXPERTS = 32
TOP_K = 4
D_EXPERT = 1024
SWIGLU_ALPHA = 1.702
SWIGLU_LIMIT = 7.0
RMS_EPS = 1e-6
LN_EPS = 1e-5
GATE_OFF = 3 * D_A + 2 * D_CONV + 4 * D_M

T_CTX = BATCH * SEQ
T_LAT = DEC_BATCH * DEC_SEQ
T_ALL = T_CTX + T_LAT
N_SEG = 1 + DEC_BATCH
SEG_PAD = 16
GRID_ROWS = DEC_SEQ // GRID_W
NEG = -1e30

N_ZB = 6 * 512
N_ZG = 3 * D_MODEL
N_ZF = 1664
QB_QA, QB_KA, QB_VA, QB_QM, QB_KM, QB_VM = 0, 1, 2, 3, 4, 5
ZG_GA, ZG_GC, ZG_GM = 0, 1, 2
ZF_CU, ZF_CG, ZF_OM = 0, 1, 2
ZF_GATES = 12

TM_TOK = 512
TM_PROJ = 512
TM_MOE = 512
N_TILES = T_ALL // TM_MOE
CHUNK_ROWS = 16
MXU_ROWS = 256
Q_TILE = -(-(TM_MOE * TOP_K + N_EXPERTS * (CHUNK_ROWS - 1)) // MXU_ROWS) * MXU_ROWS
CH_PER_TILE = Q_TILE // CHUNK_ROWS
E_GROUP = 256
CPG = E_GROUP // CHUNK_ROWS
MAX_CHUNKS = ((T_ALL * TOP_K + N_TILES * N_EXPERTS * (CHUNK_ROWS - 1)) // CHUNK_ROWS
              + N_EXPERTS * (CPG - 1)) + CPG
assert Q_TILE - (TM_MOE * TOP_K + N_EXPERTS * (CHUNK_ROWS - 1)) >= 2 * CHUNK_ROWS and N_TILES > CPG
READ_SPARE = N_TILES * Q_TILE - CHUNK_ROWS
VMEM_LIMIT = 60 * 1024 * 1024


def _cparams(sem=None):
    return pltpu.CompilerParams(dimension_semantics=sem, vmem_limit_bytes=VMEM_LIMIT)


def _seg_of_tile(i, tile):
    n_ctx = T_CTX // tile
    per_lat = DEC_SEQ // tile
    return jnp.where(i < n_ctx, 0, 1 + (i - n_ctx) // per_lat)


def _dot_nt(a, b):
    return lax.dot_general(a, b, (((1,), (1,)), ((), ())), preferred_element_type=F32)


def _sigmoid(x):
    return 1.0 / (1.0 + jnp.exp(-x))


def _mod_kernel(c_ref, w_ref, b_ref, o_ref):
    c = c_ref[...]
    s = c * _sigmoid(c)
    o_ref[...] = jnp.dot(s, w_ref[...], precision=HIGHEST, preferred_element_type=F32) + b_ref[...]


def _modulation(cond, w_mod, b_mod):
    tn = 1536
    return pl.pallas_call(
        _mod_kernel,
        grid=(DEPTH, 6 * D_MODEL // tn),
        in_specs=[pl.BlockSpec((SEG_PAD, D_MODEL), lambda l, j: (0, 0)),
                  pl.BlockSpec((None, D_MODEL, tn), lambda l, j: (l, 0, j)),
                  pl.BlockSpec((None, 1, tn), lambda l, j: (l, 0, j))],
        out_specs=pl.BlockSpec((None, SEG_PAD, tn), lambda l, j: (l, 0, j)),
        out_shape=jax.ShapeDtypeStruct((DEPTH, SEG_PAD, 6 * D_MODEL), F32),
        compiler_params=_cparams(("arbitrary", "arbitrary")),
        name="modulation",
    )(cond, w_mod, b_mod.reshape(DEPTH, 1, 6 * D_MODEL))


def _normmod(x, g, mod, shift_idx, scale_idx):
    y = x * lax.rsqrt(jnp.mean(x * x, axis=-1, keepdims=True) + RMS_EPS) * g
    return y * (1.0 + mod[scale_idx:scale_idx + 1, :]) + mod[shift_idx:shift_idx + 1, :]


PROJ_CTX_TILES = T_CTX // TM_PROJ


def _in_proj_kernel(x_ref, g_ref, mod_ref, wb_ref, bb_ref, wg_ref, bg_ref, wf_ref, bf_ref,
                    zb_ref, zg_ref, zf_ref, kc_ref, vc_ref):
    h = _normmod(x_ref[...], g_ref[...], mod_ref[...], 0, 1).astype(BF16)
    acc = jnp.dot(h, wb_ref[...], preferred_element_type=F32) + bb_ref[...]
    zb_ref[...] = acc.astype(BF16)

    @pl.when(pl.program_id(0) < PROJ_CTX_TILES)
    def _():
        kc_ref[...] = acc[:, D_A:2 * D_A]
        vc_ref[...] = acc[:, 2 * D_A:3 * D_A]

    gates = jnp.dot(h, wg_ref[...], preferred_element_type=F32) + bg_ref[...]
    zg_ref[...] = _sigmoid(gates).astype(BF16)
    zf_ref[...] = jnp.dot(h, wf_ref[...], preferred_element_type=F32) + bf_ref[...]


def _in_proj(x, norm_g, mod, w, b, layer):
    def full(a):
        return pl.BlockSpec((None,) + a.shape[1:], lambda i: (layer, 0, 0), pipeline_mode=pl.Buffered(1))

    def rows(n):
        return pl.BlockSpec((TM_PROJ, n), lambda i: (i, 0))

    cache_spec = pl.BlockSpec((TM_PROJ, D_A), lambda i: (jnp.minimum(i, PROJ_CTX_TILES - 1), 0))
    cache_shape = jax.ShapeDtypeStruct((T_CTX, D_A), F32)

    return pl.pallas_call(
        _in_proj_kernel,
        grid=(T_ALL // TM_PROJ,),
        in_specs=[rows(D_MODEL),
                  pl.BlockSpec((1, D_MODEL), lambda i: (0, 0)),
                  pl.BlockSpec((None, 6, D_MODEL), lambda i: (_seg_of_tile(i, TM_PROJ), 0, 0)),
                  full(w[0]), full(b[0]), full(w[1]), full(b[1]), full(w[2]), full(b[2])],
        out_specs=[rows(N_ZB), rows(N_ZG), rows(N_ZF), cache_spec, cache_spec],
        out_shape=[jax.ShapeDtypeStruct((T_ALL, N_ZB), BF16), jax.ShapeDtypeStruct((T_ALL, N_ZG), BF16),
                   jax.ShapeDtypeStruct((T_ALL, N_ZF), F32), cache_shape, cache_shape],
        compiler_params=_cparams(("arbitrary",)),
        name="in_proj",
    )(x, norm_g, mod, w[0], b[0], w[1], b[1], w[2], b[2])


HEAD_PAIR = 2 * HEAD_DIM_A
ATT_SCALE = HEAD_DIM_A ** -0.5


def _pair_queries(q2):
    lo = lax.broadcasted_iota(jnp.int32, (1, HEAD_PAIR), 1) < HEAD_DIM_A
    q2 = q2 * ATT_SCALE
    zero = jnp.zeros_like(q2)
    return lo, jnp.concatenate([jnp.where(lo, q2, zero), jnp.where(lo, zero, q2)], axis=0)


def _unpair(lo, o_stacked):
    rows = o_stacked.shape[0] // 2
    return jnp.where(lo, o_stacked[:rows], o_stacked[rows:])


def _ctx_attn_kernel(q_ref, k_ref, v_ref, o_ref):
    for hp in range(N_HEADS_A // 2):
        sl = slice(hp * HEAD_PAIR, (hp + 1) * HEAD_PAIR)
        lo, qs = _pair_queries(q_ref[:, sl])
        s = _dot_nt(qs, k_ref[:, sl])
        p = jnp.exp(s - jnp.max(s, axis=-1, keepdims=True))
        l = jnp.sum(p, axis=-1, keepdims=True)
        o = jnp.dot(p.astype(BF16), v_ref[:, sl], preferred_element_type=F32) / l
        o_ref[:, sl] = _unpair(lo, o).astype(o_ref.dtype)


def _ctx_attention(zb):
    def spec(cb):
        return pl.BlockSpec((SEQ, D_A), lambda b: (b, cb))

    return pl.pallas_call(
        _ctx_attn_kernel,
        grid=(BATCH,),
        in_specs=[spec(QB_QA), spec(QB_KA), spec(QB_VA)],
        out_specs=pl.BlockSpec((SEQ, D_A), lambda b: (b, 0)),
        out_shape=jax.ShapeDtypeStruct((T_CTX, D_A), BF16),
        compiler_params=_cparams(("arbitrary",)),
        name="ctx_attention",
    )(zb, zb, zb)


NQ_ROWS = 4
NW_ROWS = 12
NQ_BLOCKS = GRID_ROWS // NQ_ROWS
NQ_TOK = NQ_ROWS * GRID_W
NW_TOK = NW_ROWS * GRID_W


def _window_row(qb, xp):
    return xp.clip(qb * NQ_ROWS - WIN_ROWS // 2, 0, GRID_ROWS - NW_ROWS)


N_REL_ROWS = 2 * WIN_ROWS - 1


def _natt_rel_rows():
    r = np.arange(NQ_BLOCKS)[:, None, None] * NQ_ROWS + np.arange(NQ_ROWS)[None, :, None]
    krow = _window_row(np.arange(NQ_BLOCKS), np)[:, None, None] + np.arange(NW_ROWS)[None, None, :]
    rs = np.clip(r - WIN_ROWS // 2, 0, GRID_ROWS - WIN_ROWS)
    assert ((rs >= krow[:, :, :1]) & (rs + WIN_ROWS <= krow[:, :, -1:] + 1)).all()
    return np.where((krow >= rs) & (krow < rs + WIN_ROWS), krow - r + WIN_ROWS - 1, N_REL_ROWS)


NATT_REL = _natt_rel_rows()


def _natt_kernel(q_ref, k_ref, v_ref, kc_ref, vc_ref, toe_ref, o_ref, bias_scr):
    qb = pl.program_id(0)

    @pl.when(pl.program_id(1) == 0)
    def _():
        for v in range(NQ_BLOCKS):
            @pl.when(qb == v)
            def _(v=v):
                for h in range(N_HEADS_A):
                    for rq in range(NQ_ROWS):
                        for kr in range(NW_ROWS):
                            half = (kr % 2) * GRID_W
                            bias_scr[h, rq * GRID_W:(rq + 1) * GRID_W, kr * GRID_W:(kr + 1) * GRID_W] = (
                                toe_ref[h, int(NATT_REL[v, rq, kr]), :, half:half + GRID_W])

    start = pl.multiple_of(_window_row(qb, jnp) * GRID_W, NQ_TOK)
    band = NW_TOK
    for hp in range(N_HEADS_A // 2):
        sl = slice(hp * HEAD_PAIR, (hp + 1) * HEAD_PAIR)
        lo, qs = _pair_queries(q_ref[:, sl])
        bias = bias_scr[2 * hp:2 * hp + 2].reshape(2 * NQ_TOK, band)
        s_loc = _dot_nt(qs, k_ref[pl.ds(start, band), sl]) + bias
        s_ctx = _dot_nt(qs, kc_ref[:, sl])
        m = jnp.maximum(jnp.max(s_loc, axis=-1, keepdims=True), jnp.max(s_ctx, axis=-1, keepdims=True))
        p_loc = jnp.exp(s_loc - m)
        p_ctx = jnp.exp(s_ctx - m)
        l = jnp.sum(p_loc, axis=-1, keepdims=True) + jnp.sum(p_ctx, axis=-1, keepdims=True)
        o = (jnp.dot(p_loc.astype(BF16), v_ref[pl.ds(start, band), sl], preferred_element_type=F32)
             + jnp.dot(p_ctx.astype(BF16), vc_ref[:, sl], preferred_element_type=F32))
        o_ref[:, sl] = _unpair(lo, o / l).astype(o_ref.dtype)


def _natt_bias(rpb):
    qc = np.arange(GRID_W)
    kc = np.arange(GRID_W)
    cs = np.clip(qc - WIN_COLS // 2, 0, GRID_W - WIN_COLS)
    ok = (kc[None, :] >= cs[:, None]) & (kc[None, :] < cs[:, None] + WIN_COLS)
    dc = np.clip(kc[None, :] - qc[:, None] + WIN_COLS - 1, 0, 2 * WIN_COLS - 2)
    pick = (dc[None] == np.arange(2 * WIN_COLS - 1)[:, None, None]).astype(np.float32)
    toe = jnp.einsum('lhdc,cqk->lhdqk', rpb, jnp.asarray(pick), precision=HIGHEST)
    toe = jnp.where(jnp.asarray(ok), toe, NEG)
    toe = jnp.concatenate([toe, jnp.full((DEPTH, N_HEADS_A, 1, GRID_W, GRID_W), NEG, F32)], axis=2)
    return jnp.concatenate([toe, toe], axis=-1)


def _natt(zb, cache_k, cache_v, toe, layer):
    lat0 = T_CTX // DEC_SEQ
    row0 = T_CTX // NQ_TOK
    return pl.pallas_call(
        _natt_kernel,
        grid=(NQ_BLOCKS, DEC_BATCH),
        in_specs=[pl.BlockSpec((NQ_TOK, D_A), lambda qb, b: (row0 + b * NQ_BLOCKS + qb, QB_QA)),
                  pl.BlockSpec((DEC_SEQ, D_A), lambda qb, b: (lat0 + b, QB_KA)),
                  pl.BlockSpec((DEC_SEQ, D_A), lambda qb, b: (lat0 + b, QB_VA)),
                  pl.BlockSpec((None, None, PAST_LEN, D_A), lambda qb, b: (b, layer, 0, 0)),
                  pl.BlockSpec((None, None, PAST_LEN, D_A), lambda qb, b: (b, layer, 0, 0)),
                  pl.BlockSpec((None, N_HEADS_A, N_REL_ROWS + 1, GRID_W, 2 * GRID_W),
                               lambda qb, b: (layer, 0, 0, 0, 0))],
        out_specs=pl.BlockSpec((NQ_TOK, D_A), lambda qb, b: (b * NQ_BLOCKS + qb, 0)),
        out_shape=jax.ShapeDtypeStruct((T_LAT, D_A), BF16),
        scratch_shapes=[pltpu.VMEM((N_HEADS_A, NQ_TOK, NW_TOK), F32)],
        compiler_params=_cparams(("arbitrary", "arbitrary")),
        name="nbr_attention",
    )(zb, zb, zb, cache_k, cache_v, toe)


CONV_HALO = 16
CONV_ROWS = 64


SUBLANES = 8


def _conv_kernel(u_ref, g_ref, w_ref, b_ref, lg_ref, lb_ref, o_ref, pad_scr, sh_scr, *, seq):
    zeros = jnp.zeros((CONV_HALO, D_CONV), F32)
    pad_scr[0:CONV_HALO, :] = zeros
    pad_scr[CONV_HALO + seq:2 * CONV_HALO + seq, :] = zeros
    pad_scr[CONV_HALO:CONV_HALO + seq, :] = u_ref[...] * _sigmoid(g_ref[...])
    n_sh = seq + 2 * CONV_HALO - SUBLANES
    for s in range(SUBLANES):
        sh_scr[s] = pad_scr[s:s + n_sh, :]
    first = CONV_HALO - CONV_WIDTH // 2
    for c in range(seq // CONV_ROWS):
        base = c * CONV_ROWS
        acc = jnp.broadcast_to(b_ref[...], (CONV_ROWS, D_CONV))
        for j in range(CONV_WIDTH):
            q, s = divmod(first + j, SUBLANES)
            row0 = base + q * SUBLANES
            acc = acc + sh_scr[s, row0:row0 + CONV_ROWS, :] * w_ref[j:j + 1, :]
        mu = jnp.mean(acc, axis=-1, keepdims=True)
        xc = acc - mu
        var = jnp.mean(xc * xc, axis=-1, keepdims=True)
        y = xc * lax.rsqrt(var + LN_EPS) * lg_ref[...] + lb_ref[...]
        o_ref[base:base + CONV_ROWS, :] = (y * _sigmoid(y)).astype(o_ref.dtype)


def _conv(z, w_dw, b_dw, ln_g, ln_b, seq, n_seq, row_block0):
    def vec():
        return pl.BlockSpec((1, D_CONV), lambda b: (0, 0))

    return pl.pallas_call(
        functools.partial(_conv_kernel, seq=seq),
        grid=(n_seq,),
        in_specs=[pl.BlockSpec((seq, D_CONV), lambda b: (row_block0 + b, ZF_CU)),
                  pl.BlockSpec((seq, D_CONV), lambda b: (row_block0 + b, ZF_CG)),
                  pl.BlockSpec((CONV_WIDTH, D_CONV), lambda b: (0, 0)),
                  vec(), vec(), vec()],
        out_specs=pl.BlockSpec((seq, D_CONV), lambda b: (b, 0)),
        out_shape=jax.ShapeDtypeStruct((n_seq * seq, D_CONV), BF16),
        scratch_shapes=[pltpu.VMEM((seq + 2 * CONV_HALO, D_CONV), F32),
                        pltpu.VMEM((SUBLANES, seq + 2 * CONV_HALO - SUBLANES, D_CONV), F32)],
        compiler_params=_cparams(("arbitrary",)),
        name="conformer_conv",
    )(z, z, w_dw, b_dw, ln_g, ln_b)


N_STREAM = 2 * N_HEADS_M
MCHUNK = 256


PAIR_M = 2 * HEAD_DIM_M
N_PAIR = N_STREAM // 2


def _per_head_rows(r):
    return jnp.concatenate([jnp.broadcast_to(r[0:1], (HEAD_DIM_M, r.shape[1])),
                            jnp.broadcast_to(r[1:2], (HEAD_DIM_M, r.shape[1]))], axis=0)


def _mlstm_kernel(q_ref, k_ref, v_ref, om_ref, gt_ref, c0_ref, n0_ref, m0_ref, ng_ref,
                  y_ref, c_out, n_out, m_out, hf_scr, hb_scr, c_scr, n_scr, m_scr, cbd_scr, vbd_scr,
                  rows_scr, acol_scr, *, seq):
    nc = seq // MCHUNK
    c_scr[...] = c0_ref[...]
    n_scr[...] = n0_ref[...]
    m_scr[...] = m0_ref[...]
    cbd_scr[...] = jnp.zeros_like(cbd_scr)
    vbd_scr[...] = jnp.zeros_like(vbd_scr)
    for s_id in range(N_STREAM):
        blk = slice((s_id % 2) * HEAD_DIM_M, (s_id % 2 + 1) * HEAD_DIM_M)
        cbd_scr[s_id // 2, blk, blk] = c0_ref[s_id].astype(BF16)
    first_head = lax.broadcasted_iota(jnp.int32, (1, PAIR_M), 1) < HEAD_DIM_M
    rows = lax.broadcasted_iota(jnp.int32, (MCHUNK, MCHUNK), 0)
    cols = lax.broadcasted_iota(jnp.int32, (MCHUNK, MCHUNK), 1)
    tris = ((cols <= rows).astype(F32), (cols >= rows).astype(F32))
    reach = (rows <= cols, rows >= cols)
    lane = lax.broadcasted_iota(jnp.int32, (1, MCHUNK), 1)
    kscale = HEAD_DIM_M ** -0.5

    for cc in range(nc):
        g = gt_ref[cc * MCHUNK:(cc + 1) * MCHUNK, :]
        lf = jnp.minimum(g, 0.0) - jnp.log(1.0 + jnp.exp(-jnp.abs(g)))
        g_t = g.T
        for d in range(2):
            cum = jnp.dot(tris[d], lf, precision=HIGHEST, preferred_element_type=F32)
            cum_t = cum.T
            i0 = 2 * N_HEADS_M * d
            i_rows = g_t[i0:i0 + N_HEADS_M, :]
            b_rows = cum_t[i0 + N_HEADS_M:i0 + 2 * N_HEADS_M, :]
            reach_max = i_rows - b_rows
            for step in (1 << s for s in range(MCHUNK.bit_length() - 1)):
                if d == 0:
                    shifted = jnp.where(lane >= step, pltpu.roll(reach_max, step, axis=1), NEG)
                else:
                    shifted = jnp.where(lane < MCHUNK - step, pltpu.roll(reach_max, MCHUNK - step, axis=1), NEG)
                reach_max = jnp.maximum(reach_max, shifted)
            rows_scr[2 * cc + d, 0] = i_rows
            rows_scr[2 * cc + d, 1] = b_rows
            rows_scr[2 * cc + d, 2] = reach_max
            acol_scr[2 * cc + d] = g - pltpu.roll(cum, cum.shape[1] - N_HEADS_M, axis=1)

    def chunk_step(c, carry):
        for d in range(2):
            cidx = c if d == 0 else nc - 1 - c
            off = pl.multiple_of(cidx * MCHUNK, MCHUNK)
            last = MCHUNK - 1 if d == 0 else 0
            i0 = 2 * N_HEADS_M * d
            heads = slice(N_HEADS_M * d, N_HEADS_M * (d + 1))
            i_rows = rows_scr[2 * cidx + d, 0]
            b_rows = rows_scr[2 * cidx + d, 1]
            reach_max = rows_scr[2 * cidx + d, 2]
            a_cols = acol_scr[2 * cidx + d]
            m_prevs = m_scr[heads, 0:1]
            inters = b_rows + m_prevs
            m_ts = jnp.maximum(inters, b_rows + reach_max)
            w_inters = jnp.exp(inters - m_ts)
            floors = jnp.exp(-m_ts)
            b_lasts = b_rows[:, last:last + 1]
            m_news = m_ts[:, last:last + 1]
            w_prevs = jnp.exp(b_lasts + m_prevs - m_news)
            w_srcs = kscale * jnp.exp(b_lasts - b_rows + i_rows - m_news)
            m_scr[heads, :] = jnp.broadcast_to(m_news, (N_HEADS_M, HEAD_DIM_M))
            for hp in range(N_HEADS_M // 2):
                pid = (N_HEADS_M // 2) * d + hp
                hh = slice(2 * hp, 2 * hp + 2)
                ps = slice(hp * PAIR_M, (hp + 1) * PAIR_M)
                q2 = q_ref[pl.ds(off, MCHUNK), ps]
                k2 = k_ref[pl.ds(off, MCHUNK), ps]
                v2_t = v_ref[pl.ds(off, MCHUNK), ps].astype(F32).T
                k_zero = jnp.zeros_like(k2)
                k_stack = jnp.concatenate([jnp.where(first_head, k2, k_zero),
                                           jnp.where(first_head, k_zero, k2)], axis=0)
                decay = jnp.concatenate(
                    [jnp.exp(jnp.where(reach[d], b_rows[h:h + 1] + a_cols[:, i0 + h:i0 + h + 1], NEG)
                             - m_ts[h:h + 1]) for h in (2 * hp, 2 * hp + 1)], axis=0)
                s_t = _dot_nt(k_stack, q2) * (kscale * decay)
                col_sums = jnp.concatenate([jnp.sum(s_t[:MCHUNK], axis=0, keepdims=True),
                                            jnp.sum(s_t[MCHUNK:], axis=0, keepdims=True)], axis=0)
                n2 = n_scr[N_HEADS_M * d + 2 * hp:N_HEADS_M * d + 2 * hp + 2, :]
                n_zero = jnp.zeros((1, HEAD_DIM_M), F32)
                n_mat = jnp.concatenate([jnp.concatenate([n2[0:1], n_zero], axis=1),
                                         jnp.concatenate([n_zero, n2[1:2]], axis=1),
                                         jnp.zeros((SUBLANES - 2, PAIR_M), F32)], axis=0)
                n_q = _dot_nt(n_mat.astype(BF16), q2)[0:2, :]
                den = w_inters[hh] * n_q + col_sums
                inv = 1.0 / jnp.maximum(jnp.abs(den), floors[hh])
                vbd_scr[pid, :HEAD_DIM_M, :MCHUNK] = v2_t[:HEAD_DIM_M].astype(BF16)
                vbd_scr[pid, HEAD_DIM_M:, MCHUNK:] = v2_t[HEAD_DIM_M:].astype(BF16)
                num_t = (_per_head_rows(w_inters[hh]) * _dot_nt(cbd_scr[pid], q2)
                         + jnp.dot(vbd_scr[pid], s_t.astype(BF16), preferred_element_type=F32))
                h_t = num_t * _per_head_rows(inv)
                if d == 0:
                    hf_scr[cidx, ps, :] = h_t
                else:
                    hb_scr[cidx, ps, :] = h_t
                upd = jnp.dot((v2_t * _per_head_rows(w_srcs[hh])).astype(BF16), k2,
                              preferred_element_type=F32)
                w_mat = jnp.concatenate([w_srcs[hh], jnp.zeros((SUBLANES - 2, MCHUNK), F32)], axis=0)
                n_upd = jnp.dot(w_mat.astype(BF16), k2, preferred_element_type=F32)
                for j in range(2):
                    h = 2 * hp + j
                    s_id = N_HEADS_M * d + h
                    blk = slice(j * HEAD_DIM_M, (j + 1) * HEAD_DIM_M)
                    c_new = w_prevs[h:h + 1] * c_scr[s_id] + upd[blk, blk]
                    c_scr[s_id] = c_new
                    cbd_scr[pid, blk, blk] = c_new.astype(BF16)
                    n_scr[s_id:s_id + 1, :] = w_prevs[h:h + 1] * n2[j:j + 1] + n_upd[j:j + 1, blk]
        return carry

    lax.fori_loop(0, nc, chunk_step, 0)

    for c in range(nc):
        ts = slice(c * MCHUNK, (c + 1) * MCHUNK)
        for h in range(N_HEADS_M):
            hs = slice(h * HEAD_DIM_M, (h + 1) * HEAD_DIM_M)
            hsum = hf_scr[c, hs, :] + hb_scr[c, hs, :]
            mu = jnp.mean(hsum, axis=0, keepdims=True)
            xc = hsum - mu
            var = jnp.mean(xc * xc, axis=0, keepdims=True)
            hn = (xc * lax.rsqrt(var + LN_EPS)).T
            y_ref[ts, hs] = (_sigmoid(om_ref[ts, hs]) * (hn * ng_ref[:, hs])).astype(y_ref.dtype)
    c_out[...] = c_scr[...]
    n_out[...] = n_scr[...]
    m_out[...] = m_scr[...]


def _mlstm(zb, zf, c0, n0, m0, norm_g, seq, n_seq, row_block0, state_map):
    lead = len(state_map(0))

    def zspec(cb):
        return pl.BlockSpec((seq, D_M), lambda b: (row_block0 + b, cb))

    def sspec(tail):
        return pl.BlockSpec((None,) * lead + tail, lambda b: state_map(b) + (0,) * len(tail))

    return pl.pallas_call(
        functools.partial(_mlstm_kernel, seq=seq),
        grid=(n_seq,),
        in_specs=[zspec(QB_QM), zspec(QB_KM), zspec(QB_VM), zspec(ZF_OM),
                  pl.BlockSpec((seq, 128), lambda b: (row_block0 + b, ZF_GATES)),
                  sspec((N_STREAM, HEAD_DIM_M, HEAD_DIM_M)),
                  sspec((N_STREAM, HEAD_DIM_M)),
                  sspec((N_STREAM, HEAD_DIM_M)),
                  pl.BlockSpec((1, D_M), lambda b: (0, 0))],
        out_specs=[pl.BlockSpec((seq, D_M), lambda b: (b, 0)),
                   pl.BlockSpec((None, N_STREAM, HEAD_DIM_M, HEAD_DIM_M), lambda b: (b, 0, 0, 0)),
                   pl.BlockSpec((None, N_STREAM, HEAD_DIM_M), lambda b: (b, 0, 0)),
                   pl.BlockSpec((None, N_STREAM, HEAD_DIM_M), lambda b: (b, 0, 0))],
        out_shape=[jax.ShapeDtypeStruct((n_seq * seq, D_M), BF16),
                   jax.ShapeDtypeStruct((n_seq, N_STREAM, HEAD_DIM_M, HEAD_DIM_M), F32),
                   jax.ShapeDtypeStruct((n_seq, N_STREAM, HEAD_DIM_M), F32),
                   jax.ShapeDtypeStruct((n_seq, N_STREAM, HEAD_DIM_M), F32)],
        scratch_shapes=[pltpu.VMEM((seq // MCHUNK, D_M, MCHUNK), F32),
                        pltpu.VMEM((seq // MCHUNK, D_M, MCHUNK), F32),
                        pltpu.VMEM((N_STREAM, HEAD_DIM_M, HEAD_DIM_M), F32),
                        pltpu.VMEM((N_STREAM, HEAD_DIM_M), F32),
                        pltpu.VMEM((N_STREAM, HEAD_DIM_M), F32),
                        pltpu.VMEM((N_PAIR, PAIR_M, PAIR_M), BF16),
                        pltpu.VMEM((N_PAIR, PAIR_M, 2 * MCHUNK), BF16),
                        pltpu.VMEM((2 * (seq // MCHUNK), 3, N_HEADS_M, MCHUNK), F32),
                        pltpu.VMEM((2 * (seq // MCHUNK), MCHUNK, 128), F32)],
        compiler_params=_cparams(("arbitrary",)),
        name="mlstm",
    )(zb, zb, zb, zf, zf, c0, n0, m0, norm_g)


N_CTX_TILES = T_CTX // TM_TOK


def _merge_kernel(x_ref, mod_ref, ya_c, ya_l, yc_c, yc_l, ym_c, ym_l, ga_ref, gc_ref, gm_ref,
                  wa_ref, wc_ref, wm_ref, wo_ref, o_ref):
    is_ctx = pl.program_id(0) < N_CTX_TILES

    def branch(y_ctx, y_lat, g_ref, w_ref):
        y = jnp.where(is_ctx, y_ctx[...], y_lat[...])
        return g_ref[...].astype(F32) * jnp.dot(y, w_ref[...], preferred_element_type=F32)

    merged = (branch(ya_c, ya_l, ga_ref, wa_ref) + branch(yc_c, yc_l, gc_ref, wc_ref)
              + branch(ym_c, ym_l, gm_ref, wm_ref))
    mix = jnp.dot(merged.astype(BF16), wo_ref[...], preferred_element_type=F32)
    o_ref[...] = x_ref[...] + mod_ref[2:3, :] * mix


def _merge(x, mod, ya, yc, ym, zg, w_pa, w_pc, w_pm, w_out, layer):
    def rows(width, cb=0):
        return pl.BlockSpec((TM_TOK, width), lambda i: (i, cb))

    def ctx_rows(width):
        return pl.BlockSpec((TM_TOK, width), lambda i: (jnp.minimum(i, N_CTX_TILES - 1), 0))

    def lat_rows(width):
        return pl.BlockSpec((TM_TOK, width), lambda i: (jnp.maximum(i - N_CTX_TILES, 0), 0))

    def full(shape):
        return pl.BlockSpec((None,) + shape, lambda i: (layer, 0, 0))

    return pl.pallas_call(
        _merge_kernel,
        grid=(T_ALL // TM_TOK,),
        in_specs=[rows(D_MODEL),
                  pl.BlockSpec((None, 6, D_MODEL), lambda i: (_seg_of_tile(i, TM_TOK), 0, 0)),
                  ctx_rows(D_A), lat_rows(D_A), ctx_rows(D_CONV), lat_rows(D_CONV),
                  ctx_rows(D_M), lat_rows(D_M),
                  rows(D_MODEL, ZG_GA), rows(D_MODEL, ZG_GC), rows(D_MODEL, ZG_GM),
                  full((D_A, D_MODEL)), full((D_CONV, D_MODEL)), full((D_M, D_MODEL)),
                  full((D_MODEL, D_MODEL))],
        out_specs=rows(D_MODEL),
        out_shape=jax.ShapeDtypeStruct((T_ALL, D_MODEL), F32),
        compiler_params=_cparams(("arbitrary",)),
        name="merge",
    )(x, mod, ya[0], ya[1], yc[0], yc[1], ym[0], ym[1], zg, zg, zg, w_pa, w_pc, w_pm, w_out)


def _route_sort_kernel(x_ref, g_ref, mod_ref, wr_ref, br_ref, xt_ref, pos_ref, gate_ref, nch_ref, seg_ref):
    h = _normmod(x_ref[...], g_ref[...], mod_ref[...], 3, 4)
    hb = h.astype(BF16)
    logits = _dot_nt(wr_ref[...].astype(BF16), hb) + br_ref[...]
    e_iota = lax.broadcasted_iota(jnp.int32, (N_EXPERTS, TM_MOE), 0).astype(F32)
    sels, vals = [], []
    l = logits
    for k in range(TOP_K):
        m = jnp.max(l, axis=0, keepdims=True)
        idx = jnp.min(jnp.where(l == m, e_iota, float(N_EXPERTS)), axis=0, keepdims=True)
        sel = e_iota == idx
        vals.append(m)
        sels.append(sel)
        l = jnp.where(sel, -jnp.inf, l)
    exps = [jnp.exp(v - vals[0]) for v in vals]
    tot = exps[0] + exps[1] + exps[2] + exps[3]
    onehot = jnp.zeros((N_EXPERTS, TM_MOE), F32)
    for k in range(TOP_K):
        gate_ref[k:k + 1, :] = exps[k] / tot
        onehot = onehot + sels[k].astype(F32)
    gate_ref[TOP_K:8, :] = jnp.zeros((8 - TOP_K, TM_MOE), F32)

    cnt = jnp.sum(onehot, axis=1, keepdims=True)
    nch = jnp.floor((cnt + (CHUNK_ROWS - 1)) / CHUNK_ROWS)
    ei = lax.broadcasted_iota(jnp.int32, (N_EXPERTS, N_EXPERTS), 0)
    ej = lax.broadcasted_iota(jnp.int32, (N_EXPERTS, N_EXPERTS), 1)
    seg = jnp.dot((ej < ei).astype(F32), jnp.broadcast_to(nch, (N_EXPERTS, 128)), precision=HIGHEST,
                  preferred_element_type=F32)
    nch_ref[...] = jnp.broadcast_to(nch, (N_EXPERTS, 128)).astype(jnp.int32)
    seg_ref[...] = seg.astype(jnp.int32)

    t_src = lax.broadcasted_iota(jnp.int32, (TM_MOE, TM_MOE), 0)
    t_dst = lax.broadcasted_iota(jnp.int32, (TM_MOE, TM_MOE), 1)
    before = (t_src < t_dst).astype(BF16)
    row_of = (seg[:, 0:1] * CHUNK_ROWS
              + jnp.dot(onehot.astype(BF16), before, preferred_element_type=F32))
    q_iota = lax.broadcasted_iota(jnp.int32, (Q_TILE, TM_MOE), 0)
    perm = jnp.zeros((Q_TILE, TM_MOE), F32)
    for k in range(TOP_K):
        q_k = jnp.sum(jnp.where(sels[k], row_of, 0.0), axis=0, keepdims=True).astype(jnp.int32)
        pos_ref[k:k + 1, :] = q_k
        perm = jnp.where(q_iota == q_k, 1.0, perm)
    pos_ref[TOP_K:8, :] = jnp.zeros((8 - TOP_K, TM_MOE), jnp.int32)
    xt_ref[...] = jnp.dot(perm.astype(BF16), hb, preferred_element_type=F32).astype(BF16)


def _route_sort(x, norm_g, mod, w_rt, b_r):
    tspec = pl.BlockSpec((8, TM_MOE), lambda i: (0, i))
    mspec = pl.BlockSpec((None, N_EXPERTS, 128), lambda i: (i, 0, 0))
    meta = jax.ShapeDtypeStruct((N_TILES, N_EXPERTS, 128), jnp.int32)
    return pl.pallas_call(
        _route_sort_kernel,
        grid=(N_TILES,),
        in_specs=[pl.BlockSpec((TM_MOE, D_MODEL), lambda i: (i, 0)),
                  pl.BlockSpec((1, D_MODEL), lambda i: (0, 0)),
                  pl.BlockSpec((None, 6, D_MODEL), lambda i: (_seg_of_tile(i, TM_MOE), 0, 0)),
                  pl.BlockSpec((N_EXPERTS, D_MODEL), lambda i: (0, 0)),
                  pl.BlockSpec((N_EXPERTS, 1), lambda i: (0, 0))],
        out_specs=[pl.BlockSpec((Q_TILE, D_MODEL), lambda i: (i, 0)), tspec, tspec, mspec, mspec],
        out_shape=[jax.ShapeDtypeStruct((N_TILES * Q_TILE, D_MODEL), BF16),
                   jax.ShapeDtypeStruct((8, T_ALL), jnp.int32), jax.ShapeDtypeStruct((8, T_ALL), F32),
                   meta, meta],
        compiler_params=_cparams(("arbitrary",)),
        name="moe_route_sort",
    )(x, norm_g, mod, w_rt, b_r)


def _expert_kernel(nch_ref, seg_ref, wgu_ref, bgu_ref, wd_ref, bd_ref, xt_ref, yt_ref,
                   wgu_scr, wd_scr, xbuf, ybuf, row_scr, gstart_scr, gsem, ssem):
    del xt_ref
    e = pl.program_id(0)

    def src_row(row):
        return pl.multiple_of(jnp.where(row >= 0, row, READ_SPARE), CHUNK_ROWS)

    def dst_row(row, slot, c):
        spare = c * Q_TILE + jnp.where(slot == 0, Q_TILE - CHUNK_ROWS, Q_TILE - 2 * CHUNK_ROWS)
        return pl.multiple_of(jnp.where(row >= 0, row, spare), CHUNK_ROWS)

    def chunk_rows(c):
        return slice(c * CHUNK_ROWS, (c + 1) * CHUNK_ROWS)

    def start_in(g):
        slot = g % 2
        for c in range(CPG):
            row = row_scr[g * CPG + c]
            pltpu.make_async_copy(yt_ref.at[pl.ds(src_row(row), CHUNK_ROWS), :], xbuf.at[slot, chunk_rows(c), :],
                                  gsem.at[slot]).start()

    def start_out(g, slot):
        for c in range(CPG):
            row = row_scr[g * CPG + c]
            pltpu.make_async_copy(ybuf.at[slot, chunk_rows(c), :],
                                  yt_ref.at[pl.ds(dst_row(row, slot, c), CHUNK_ROWS), :], ssem.at[slot]).start()

    def wait_in(slot):
        pltpu.make_async_copy(yt_ref.at[pl.ds(0, E_GROUP), :], xbuf.at[slot], gsem.at[slot]).wait()

    def wait_out(slot):
        pltpu.make_async_copy(ybuf.at[slot], yt_ref.at[pl.ds(0, E_GROUP), :], ssem.at[slot]).wait()

    @pl.when(e == 0)
    def _():
        def per_expert(ee, cnt):
            gstart_scr[ee] = cnt // CPG

            def per_tile(t, cnt):
                first = (t * CH_PER_TILE + seg_ref[t * N_EXPERTS + ee]) * CHUNK_ROWS

                def per_chunk(j, cnt):
                    row_scr[cnt] = first + j * CHUNK_ROWS
                    return cnt + 1

                return lax.fori_loop(0, nch_ref[t * N_EXPERTS + ee], per_chunk, cnt)

            cnt = lax.fori_loop(0, N_TILES, per_tile, cnt)
            padded = (cnt + CPG - 1) // CPG * CPG

            def pad(i, carry):
                row_scr[i] = -1
                return carry

            lax.fori_loop(cnt, padded, pad, 0)
            return padded

        total = lax.fori_loop(0, N_EXPERTS, per_expert, 0)
        gstart_scr[N_EXPERTS] = total // CPG

        def pad(i, carry):
            row_scr[i] = -1
            return carry

        lax.fori_loop(total, total + CPG, pad, 0)
        ybuf[...] = jnp.zeros_like(ybuf)
        start_out(total // CPG, 0)
        start_out(total // CPG, 1)
        start_in(0)

    g_first = gstart_scr[e]
    g_end = gstart_scr[e + 1]

    @pl.when(g_end > g_first)
    def _():
        wgu_scr[...] = wgu_ref[...].astype(BF16)
        wd_scr[...] = wd_ref[...].astype(BF16)

    def group_step(g, carry):
        slot = g % 2
        start_in(g + 1)
        wait_in(slot)
        wait_out(slot)
        hgu = jnp.dot(xbuf[slot], wgu_scr[...], preferred_element_type=F32) + bgu_ref[...]
        h_glu = jnp.minimum(hgu[:, :D_EXPERT], SWIGLU_LIMIT)
        h_lin = jnp.clip(hgu[:, D_EXPERT:], -SWIGLU_LIMIT, SWIGLU_LIMIT)
        act = (h_lin + 1.0) * (h_glu * _sigmoid(SWIGLU_ALPHA * h_glu))
        y = jnp.dot(act.astype(BF16), wd_scr[...], preferred_element_type=F32) + bd_ref[...]
        ybuf[slot] = y.astype(BF16)
        start_out(g, slot)
        return carry

    lax.fori_loop(g_first, g_end, group_step, 0)

    @pl.when(e == N_EXPERTS - 1)
    def _():
        wait_in(gstart_scr[N_EXPERTS] % 2)
        wait_out(0)
        wait_out(1)


def _experts(nch_flat, seg_flat, xt, w_gu, b_gu, w_down, b_down, layer):
    return pl.pallas_call(
        _expert_kernel,
        grid_spec=pltpu.PrefetchScalarGridSpec(
            num_scalar_prefetch=2,
            grid=(N_EXPERTS,),
            in_specs=[pl.BlockSpec((None, None, D_MODEL, 2 * D_EXPERT), lambda e, n, s: (layer, e, 0, 0)),
                      pl.BlockSpec((None, None, 1, 2 * D_EXPERT), lambda e, n, s: (layer, e, 0, 0)),
                      pl.BlockSpec((None, None, D_EXPERT, D_MODEL), lambda e, n, s: (layer, e, 0, 0)),
                      pl.BlockSpec((None, None, 1, D_MODEL), lambda e, n, s: (layer, e, 0, 0)),
                      pl.BlockSpec(memory_space=pl.ANY)],
            out_specs=pl.BlockSpec(memory_space=pl.ANY),
            scratch_shapes=[pltpu.VMEM((D_MODEL, 2 * D_EXPERT), BF16),
                            pltpu.VMEM((D_EXPERT, D_MODEL), BF16),
                            pltpu.VMEM((2, E_GROUP, D_MODEL), BF16),
                            pltpu.VMEM((2, E_GROUP, D_MODEL), BF16),
                            pltpu.SMEM((MAX_CHUNKS,), jnp.int32),
                            pltpu.SMEM((N_EXPERTS + 1,), jnp.int32),
                            pltpu.SemaphoreType.DMA((2,)),
                            pltpu.SemaphoreType.DMA((2,))]),
        out_shape=jax.ShapeDtypeStruct((N_TILES * Q_TILE, D_MODEL), BF16),
        input_output_aliases={6: 0},
        compiler_params=_cparams(("arbitrary",)),
        name="moe_experts",
    )(nch_flat, seg_flat, w_gu, b_gu, w_down, b_down, xt)


MOE_CTX_TILES = T_CTX // TM_MOE


def _combine_rows(x_ref, mod_ref, pos_ref, gate_ref, yt_ref):
    lane = lax.broadcasted_iota(jnp.int32, (TM_MOE, Q_TILE), 1)
    sel = jnp.zeros((TM_MOE, Q_TILE), F32)
    for k in range(TOP_K):
        sel = jnp.where(lane == pos_ref[:, k:k + 1], gate_ref[:, k:k + 1], sel)
    acc = jnp.dot(sel.astype(BF16), yt_ref[...], preferred_element_type=F32)
    return x_ref[...] + mod_ref[5:6, :] * acc


def _combine_kernel(x_ref, mod_ref, pos_ref, gate_ref, yt_ref, o_ref):
    o_ref[...] = _combine_rows(x_ref, mod_ref, pos_ref, gate_ref, yt_ref)


def _combine_final_kernel(x_ref, mod_ref, pos_ref, gate_ref, yt_ref, fg_ref, ctx_ref, lat_ref):
    y = _combine_rows(x_ref, mod_ref, pos_ref, gate_ref, yt_ref)
    y = y * lax.rsqrt(jnp.mean(y * y, axis=-1, keepdims=True) + RMS_EPS) * fg_ref[...]
    is_ctx = pl.program_id(0) < MOE_CTX_TILES

    @pl.when(is_ctx)
    def _():
        ctx_ref[...] = y

    @pl.when(jnp.logical_not(is_ctx))
    def _():
        lat_ref[...] = y


def _combine(x, mod, pos_t, gate_t, yt, final_g=None):
    tile = pl.BlockSpec((TM_MOE, D_MODEL), lambda i: (i, 0))
    in_specs = [tile,
                pl.BlockSpec((None, 6, D_MODEL), lambda i: (_seg_of_tile(i, TM_MOE), 0, 0)),
                pl.BlockSpec((TM_MOE, 8), lambda i: (i, 0)),
                pl.BlockSpec((TM_MOE, 8), lambda i: (i, 0)),
                pl.BlockSpec((Q_TILE, D_MODEL), lambda i: (i, 0))]
    if final_g is None:
        return pl.pallas_call(
            _combine_kernel, grid=(N_TILES,), in_specs=in_specs, out_specs=tile,
            out_shape=jax.ShapeDtypeStruct((T_ALL, D_MODEL), F32),
            compiler_params=_cparams(("arbitrary",)), name="moe_combine",
        )(x, mod, pos_t, gate_t, yt)
    return pl.pallas_call(
        _combine_final_kernel, grid=(N_TILES,),
        in_specs=in_specs + [pl.BlockSpec((1, D_MODEL), lambda i: (0, 0))],
        out_specs=[pl.BlockSpec((TM_MOE, D_MODEL), lambda i: (jnp.minimum(i, MOE_CTX_TILES - 1), 0)),
                   pl.BlockSpec((TM_MOE, D_MODEL), lambda i: (jnp.maximum(i - MOE_CTX_TILES, 0), 0))],
        out_shape=[jax.ShapeDtypeStruct((T_CTX, D_MODEL), F32), jax.ShapeDtypeStruct((T_LAT, D_MODEL), F32)],
        compiler_params=_cparams(("arbitrary",)), name="moe_combine_final",
    )(x, mod, pos_t, gate_t, yt, final_g)


def _moe(x, norm_g, mod, w_rt, b_r, w_gu, b_gu, w_down, b_down, layer, final_g=None):
    xt, pos, gate, nch, seg = _route_sort(x, norm_g, mod, w_rt, b_r)
    yt = _experts(nch[:, :, 0].reshape(-1), seg[:, :, 0].reshape(-1), xt, w_gu,
                  b_gu.reshape(DEPTH, N_EXPERTS, 1, 2 * D_EXPERT), w_down,
                  b_down.reshape(DEPTH, N_EXPERTS, 1, D_MODEL), layer)
    return _combine(x, mod, pos.T, gate.T, yt, final_g)


def _split_in_cols(w):
    conv0, mq0, om0 = 3 * D_A, 3 * D_A + 2 * D_CONV, 3 * D_A + 2 * D_CONV + 3 * D_M
    gates_end = GATE_OFF + N_GATE_M
    pad = jnp.zeros(w.shape[:-1] + (N_ZF - (2 * D_CONV + D_M + N_GATE_M),), w.dtype)
    zb = jnp.concatenate([w[..., :conv0], w[..., mq0:om0]], axis=-1)
    zg = w[..., gates_end:]
    zf = jnp.concatenate([w[..., conv0:mq0], w[..., om0:GATE_OFF], w[..., GATE_OFF:gates_end], pad], axis=-1)
    return zb, zg, zf


def kernel(x_prompt, x_sample, cache_k, cache_v, state_C, state_n, state_m, c, c_ctx, norm1_g, w_mod, b_mod, w_in, b_in, rpb, w_dw, b_dw, cln_g, cln_b, mnorm_g, w_pa, w_pc, w_pm, w_out, norm2_g, w_router, b_router, w_gu, b_gu, w_down, b_down, final_g):
    cond = jnp.concatenate([c_ctx[None, :], c, jnp.zeros((SEG_PAD - N_SEG, D_MODEL), F32)], axis=0)
    mod_all = _modulation(cond, w_mod, b_mod).reshape(DEPTH, SEG_PAD, 6, D_MODEL)

    x = jnp.concatenate([x_prompt.reshape(T_CTX, D_MODEL), x_sample.reshape(T_LAT, D_MODEL)], axis=0)
    ck = cache_k.reshape(DEC_BATCH, DEPTH, PAST_LEN, D_A).astype(BF16)
    cv = cache_v.reshape(DEC_BATCH, DEPTH, PAST_LEN, D_A).astype(BF16)
    lat_c0 = state_C.reshape(DEC_BATCH, DEPTH, N_STREAM, HEAD_DIM_M, HEAD_DIM_M)
    lat_n0 = state_n.reshape(DEC_BATCH, DEPTH, N_STREAM, HEAD_DIM_M)
    lat_m0 = jnp.broadcast_to(state_m.reshape(DEC_BATCH, DEPTH, N_STREAM, 1),
                              (DEC_BATCH, DEPTH, N_STREAM, HEAD_DIM_M))
    ctx_c0 = jnp.zeros((1, N_STREAM, HEAD_DIM_M, HEAD_DIM_M), F32)
    ctx_n0 = jnp.zeros((1, N_STREAM, HEAD_DIM_M), F32)
    ctx_m0 = jnp.full((1, N_STREAM, HEAD_DIM_M), -jnp.inf, F32)

    w_in_groups = tuple(g.astype(BF16) for g in _split_in_cols(w_in))
    b_in_groups = _split_in_cols(b_in[:, None, :])
    merge_w = tuple(w.astype(BF16) for w in (w_pa, w_pc, w_pm, w_out))
    toe = _natt_bias(rpb)
    ks, vs, cs, ns, ms = [], [], [], [], []
    for l in range(DEPTH):
        mod = mod_all[l]
        g1 = norm1_g[l][None, :]
        zb, zg, zf, k_l, v_l = _in_proj(x, g1, mod, w_in_groups, b_in_groups, l)
        ya = (_ctx_attention(zb), _natt(zb, ck, cv, toe, l))
        conv_w = (w_dw[l], b_dw[l][None, :], cln_g[l][None, :], cln_b[l][None, :])
        yc = (_conv(zf, *conv_w, SEQ, BATCH, 0), _conv(zf, *conv_w, DEC_SEQ, DEC_BATCH, T_CTX // DEC_SEQ))
        ng = mnorm_g[l][None, :]
        ym_ctx, c_l, n_l, m_l = _mlstm(zb, zf, ctx_c0, ctx_n0, ctx_m0, ng, SEQ, BATCH, 0, lambda b: (0,))
        ym_lat, _, _, _ = _mlstm(zb, zf, lat_c0, lat_n0, lat_m0, ng, DEC_SEQ, DEC_BATCH, T_CTX // DEC_SEQ,
                                 lambda b: (b, l))
        x = _merge(x, mod, ya, yc, (ym_ctx, ym_lat), zg, *merge_w, l)
        x = _moe(x, norm2_g[l][None, :], mod, w_router[l].T, b_router[l][:, None],
                 w_gu, b_gu, w_down, b_down, l, final_g[None, :] if l == DEPTH - 1 else None)
        ks.append(k_l.reshape(BATCH, SEQ, N_HEADS_A, HEAD_DIM_A))
        vs.append(v_l.reshape(BATCH, SEQ, N_HEADS_A, HEAD_DIM_A))
        cs.append(c_l.reshape(BATCH, 2, N_HEADS_M, HEAD_DIM_M, HEAD_DIM_M))
        ns.append(n_l.reshape(BATCH, 2, N_HEADS_M, HEAD_DIM_M))
        ms.append(m_l[:, :, 0].reshape(BATCH, 2, N_HEADS_M))

    y_ctx, y_lat = x
    return (y_ctx.reshape(BATCH, SEQ, D_MODEL), y_lat.reshape(DEC_BATCH, DEC_SEQ, D_MODEL),
            jnp.stack(ks, axis=1), jnp.stack(vs, axis=1), jnp.stack(cs, axis=1),
            jnp.stack(ns, axis=1), jnp.stack(ms, axis=1))
```

```python
import functools

import numpy as np
import jax
import jax.numpy as jnp
from jax import lax
from jax.experimental import pallas as pl
from jax.experimental.pallas import tpu as pltpu

F32 = jnp.float32
BF16 = jnp.bfloat16
HIGHEST = lax.Precision.HIGHEST

D_MODEL = 1024
BATCH = 16
SEQ = 256
DEPTH = 2
DEC_BATCH = 8
DEC_SEQ = 1024
PAST_LEN = 512
GRID_W = 64
N_HEADS_A = 8
HEAD_DIM_A = 64
D_A = N_HEADS_A * HEAD_DIM_A
WIN_ROWS = 8
WIN_COLS = 16
D_CONV = 512
CONV_WIDTH = 31
N_HEADS_M = 4
HEAD_DIM_M = 128
D_M = N_HEADS_M * HEAD_DIM_M
N_GATE_M = 4 * N_HEADS_M
N_EXPERTS = 32
TOP_K = 4
D_EXPERT = 1024
SWIGLU_ALPHA = 1.702
SWIGLU_LIMIT = 7.0
RMS_EPS = 1e-6
LN_EPS = 1e-5
GATE_OFF = 3 * D_A + 2 * D_CONV + 4 * D_M

T_CTX = BATCH * SEQ
T_LAT = DEC_BATCH * DEC_SEQ
T_ALL = T_CTX + T_LAT
N_SEG = 1 + DEC_BATCH
SEG_PAD = 16
GRID_ROWS = DEC_SEQ // GRID_W
NEG = -1e30

N_ZB = 6 * 512
N_ZG = 3 * D_MODEL
N_ZF = 1664
QB_QA, QB_KA, QB_VA, QB_QM, QB_KM, QB_VM = 0, 1, 2, 3, 4, 5
ZG_GA, ZG_GC, ZG_GM = 0, 1, 2
ZF_CU, ZF_CG, ZF_OM = 0, 1, 2
ZF_GATES = 12

TM_TOK = 512
TM_PROJ = 512
TM_MOE = 512
N_TILES = T_ALL // TM_MOE
CHUNK_ROWS = 16
MXU_ROWS = 256
Q_TILE = -(-(TM_MOE * TOP_K + N_EXPERTS * (CHUNK_ROWS - 1)) // MXU_ROWS) * MXU_ROWS
CH_PER_TILE = Q_TILE // CHUNK_ROWS
E_GROUP = 256
CPG = E_GROUP // CHUNK_ROWS
MAX_CHUNKS = ((T_ALL * TOP_K + N_TILES * N_EXPERTS * (CHUNK_ROWS - 1)) // CHUNK_ROWS
              + N_EXPERTS * (CPG - 1)) + CPG
assert Q_TILE - (TM_MOE * TOP_K + N_EXPERTS * (CHUNK_ROWS - 1)) >= 2 * CHUNK_ROWS and N_TILES > CPG
READ_SPARE = N_TILES * Q_TILE - CHUNK_ROWS
VMEM_LIMIT = 60 * 1024 * 1024


def _cparams(sem=None):
    return pltpu.CompilerParams(dimension_semantics=sem, vmem_limit_bytes=VMEM_LIMIT)


def _seg_of_tile(i, tile):
    n_ctx = T_CTX // tile
    per_lat = DEC_SEQ // tile
    return jnp.where(i < n_ctx, 0, 1 + (i - n_ctx) // per_lat)


def _dot_nt(a, b):
    return lax.dot_general(a, b, (((1,), (1,)), ((), ())), preferred_element_type=F32)


def _sigmoid(x):
    return 1.0 / (1.0 + jnp.exp(-x))


def _mod_kernel(c_ref, w_ref, b_ref, o_ref):
    c = c_ref[...]
    s = c * _sigmoid(c)
    o_ref[...] = jnp.dot(s, w_ref[...], precision=HIGHEST, preferred_element_type=F32) + b_ref[...]


def _modulation(cond, w_mod, b_mod):
    tn = 1536
    return pl.pallas_call(
        _mod_kernel,
        grid=(DEPTH, 6 * D_MODEL // tn),
        in_specs=[pl.BlockSpec((SEG_PAD, D_MODEL), lambda l, j: (0, 0)),
                  pl.BlockSpec((None, D_MODEL, tn), lambda l, j: (l, 0, j)),
                  pl.BlockSpec((None, 1, tn), lambda l, j: (l, 0, j))],
        out_specs=pl.BlockSpec((None, SEG_PAD, tn), lambda l, j: (l, 0, j)),
        out_shape=jax.ShapeDtypeStruct((DEPTH, SEG_PAD, 6 * D_MODEL), F32),
        compiler_params=_cparams(("arbitrary", "arbitrary")),
        name="modulation",
    )(cond, w_mod, b_mod.reshape(DEPTH, 1, 6 * D_MODEL))


def _normmod(x, g, mod, shift_idx, scale_idx):
    y = x * lax.rsqrt(jnp.mean(x * x, axis=-1, keepdims=True) + RMS_EPS) * g
    return y * (1.0 + mod[scale_idx:scale_idx + 1, :]) + mod[shift_idx:shift_idx + 1, :]


PROJ_CTX_TILES = T_CTX // TM_PROJ


def _in_proj_kernel(x_ref, g_ref, mod_ref, wb_ref, bb_ref, wg_ref, bg_ref, wf_ref, bf_ref,
                    zb_ref, zg_ref, zf_ref, kc_ref, vc_ref):
    h = _normmod(x_ref[...], g_ref[...], mod_ref[...], 0, 1).astype(BF16)
    acc = jnp.dot(h, wb_ref[...], preferred_element_type=F32) + bb_ref[...]
    zb_ref[...] = acc.astype(BF16)

    @pl.when(pl.program_id(0) < PROJ_CTX_TILES)
    def _():
        kc_ref[...] = acc[:, D_A:2 * D_A]
        vc_ref[...] = acc[:, 2 * D_A:3 * D_A]

    gates = jnp.dot(h, wg_ref[...], preferred_element_type=F32) + bg_ref[...]
    zg_ref[...] = _sigmoid(gates).astype(BF16)
    zf_ref[...] = jnp.dot(h, wf_ref[...], preferred_element_type=F32) + bf_ref[...]


def _in_proj(x, norm_g, mod, w, b, layer):
    def full(a):
        return pl.BlockSpec((None,) + a.shape[1:], lambda i: (layer, 0, 0), pipeline_mode=pl.Buffered(1))

    def rows(n):
        return pl.BlockSpec((TM_PROJ, n), lambda i: (i, 0))

    cache_spec = pl.BlockSpec((TM_PROJ, D_A), lambda i: (jnp.minimum(i, PROJ_CTX_TILES - 1), 0))
    cache_shape = jax.ShapeDtypeStruct((T_CTX, D_A), F32)

    return pl.pallas_call(
        _in_proj_kernel,
        grid=(T_ALL // TM_PROJ,),
        in_specs=[rows(D_MODEL),
                  pl.BlockSpec((1, D_MODEL), lambda i: (0, 0)),
                  pl.BlockSpec((None, 6, D_MODEL), lambda i: (_seg_of_tile(i, TM_PROJ), 0, 0)),
                  full(w[0]), full(b[0]), full(w[1]), full(b[1]), full(w[2]), full(b[2])],
        out_specs=[rows(N_ZB), rows(N_ZG), rows(N_ZF), cache_spec, cache_spec],
        out_shape=[jax.ShapeDtypeStruct((T_ALL, N_ZB), BF16), jax.ShapeDtypeStruct((T_ALL, N_ZG), BF16),
                   jax.ShapeDtypeStruct((T_ALL, N_ZF), F32), cache_shape, cache_shape],
        compiler_params=_cparams(("arbitrary",)),
        name="in_proj",
    )(x, norm_g, mod, w[0], b[0], w[1], b[1], w[2], b[2])


HEAD_PAIR = 2 * HEAD_DIM_A
ATT_SCALE = HEAD_DIM_A ** -0.5


def _pair_queries(q2):
    lo = lax.broadcasted_iota(jnp.int32, (1, HEAD_PAIR), 1) < HEAD_DIM_A
    q2 = q2 * ATT_SCALE
    zero = jnp.zeros_like(q2)
    return lo, jnp.concatenate([jnp.where(lo, q2, zero), jnp.where(lo, zero, q2)], axis=0)


def _unpair(lo, o_stacked):
    rows = o_stacked.shape[0] // 2
    return jnp.where(lo, o_stacked[:rows], o_stacked[rows:])


def _ctx_attn_kernel(q_ref, k_ref, v_ref, o_ref):
    for hp in range(N_HEADS_A // 2):
        sl = slice(hp * HEAD_PAIR, (hp + 1) * HEAD_PAIR)
        lo, qs = _pair_queries(q_ref[:, sl])
        s = _dot_nt(qs, k_ref[:, sl])
        p = jnp.exp(s - jnp.max(s, axis=-1, keepdims=True))
        l = jnp.sum(p, axis=-1, keepdims=True)
        o = jnp.dot(p.astype(BF16), v_ref[:, sl], preferred_element_type=F32) / l
        o_ref[:, sl] = _unpair(lo, o).astype(o_ref.dtype)


def _ctx_attention(zb):
    def spec(cb):
        return pl.BlockSpec((SEQ, D_A), lambda b: (b, cb))

    return pl.pallas_call(
        _ctx_attn_kernel,
        grid=(BATCH,),
        in_specs=[spec(QB_QA), spec(QB_KA), spec(QB_VA)],
        out_specs=pl.BlockSpec((SEQ, D_A), lambda b: (b, 0)),
        out_shape=jax.ShapeDtypeStruct((T_CTX, D_A), BF16),
        compiler_params=_cparams(("arbitrary",)),
        name="ctx_attention",
    )(zb, zb, zb)


NQ_ROWS = 4
NW_ROWS = 12
NQ_BLOCKS = GRID_ROWS // NQ_ROWS
NQ_TOK = NQ_ROWS * GRID_W
NW_TOK = NW_ROWS * GRID_W


def _window_row(qb, xp):
    return xp.clip(qb * NQ_ROWS - WIN_ROWS // 2, 0, GRID_ROWS - NW_ROWS)


N_REL_ROWS = 2 * WIN_ROWS - 1


def _natt_rel_rows():
    r = np.arange(NQ_BLOCKS)[:, None, None] * NQ_ROWS + np.arange(NQ_ROWS)[None, :, None]
    krow = _window_row(np.arange(NQ_BLOCKS), np)[:, None, None] + np.arange(NW_ROWS)[None, None, :]
    rs = np.clip(r - WIN_ROWS // 2, 0, GRID_ROWS - WIN_ROWS)
    assert ((rs >= krow[:, :, :1]) & (rs + WIN_ROWS <= krow[:, :, -1:] + 1)).all()
    return np.where((krow >= rs) & (krow < rs + WIN_ROWS), krow - r + WIN_ROWS - 1, N_REL_ROWS)


NATT_REL = _natt_rel_rows()


def _natt_kernel(q_ref, k_ref, v_ref, kc_ref, vc_ref, toe_ref, o_ref, bias_scr):
    qb = pl.program_id(0)

    @pl.when(pl.program_id(1) == 0)
    def _():
        for v in range(NQ_BLOCKS):
            @pl.when(qb == v)
            def _(v=v):
                for h in range(N_HEADS_A):
                    for rq in range(NQ_ROWS):
                        for kr in range(NW_ROWS):
                            half = (kr % 2) * GRID_W
                            bias_scr[h, rq * GRID_W:(rq + 1) * GRID_W, kr * GRID_W:(kr + 1) * GRID_W] = (
                                toe_ref[h, int(NATT_REL[v, rq, kr]), :, half:half + GRID_W])

    start = pl.multiple_of(_window_row(qb, jnp) * GRID_W, NQ_TOK)
    band = NW_TOK
    for hp in range(N_HEADS_A // 2):
        sl = slice(hp * HEAD_PAIR, (hp + 1) * HEAD_PAIR)
        lo, qs = _pair_queries(q_ref[:, sl])
        bias = bias_scr[2 * hp:2 * hp + 2].reshape(2 * NQ_TOK, band)
        s_loc = _dot_nt(qs, k_ref[pl.ds(start, band), sl]) + bias
        s_ctx = _dot_nt(qs, kc_ref[:, sl])
        m = jnp.maximum(jnp.max(s_loc, axis=-1, keepdims=True), jnp.max(s_ctx, axis=-1, keepdims=True))
        p_loc = jnp.exp(s_loc - m)
        p_ctx = jnp.exp(s_ctx - m)
        l = jnp.sum(p_loc, axis=-1, keepdims=True) + jnp.sum(p_ctx, axis=-1, keepdims=True)
        o = (jnp.dot(p_loc.astype(BF16), v_ref[pl.ds(start, band), sl], preferred_element_type=F32)
             + jnp.dot(p_ctx.astype(BF16), vc_ref[:, sl], preferred_element_type=F32))
        o_ref[:, sl] = _unpair(lo, o / l).astype(o_ref.dtype)


def _natt_bias(rpb):
    qc = np.arange(GRID_W)
    kc = np.arange(GRID_W)
    cs = np.clip(qc - WIN_COLS // 2, 0, GRID_W - WIN_COLS)
    ok = (kc[None, :] >= cs[:, None]) & (kc[None, :] < cs[:, None] + WIN_COLS)
    dc = np.clip(kc[None, :] - qc[:, None] + WIN_COLS - 1, 0, 2 * WIN_COLS - 2)
    pick = (dc[None] == np.arange(2 * WIN_COLS - 1)[:, None, None]).astype(np.float32)
    toe = jnp.einsum('lhdc,cqk->lhdqk', rpb, jnp.asarray(pick), precision=HIGHEST)
    toe = jnp.where(jnp.asarray(ok), toe, NEG)
    toe = jnp.concatenate([toe, jnp.full((DEPTH, N_HEADS_A, 1, GRID_W, GRID_W), NEG, F32)], axis=2)
    return jnp.concatenate([toe, toe], axis=-1)


def _natt(zb, cache_k, cache_v, toe, layer):
    lat0 = T_CTX // DEC_SEQ
    row0 = T_CTX // NQ_TOK
    return pl.pallas_call(
        _natt_kernel,
        grid=(NQ_BLOCKS, DEC_BATCH),
        in_specs=[pl.BlockSpec((NQ_TOK, D_A), lambda qb, b: (row0 + b * NQ_BLOCKS + qb, QB_QA)),
                  pl.BlockSpec((DEC_SEQ, D_A), lambda qb, b: (lat0 + b, QB_KA)),
                  pl.BlockSpec((DEC_SEQ, D_A), lambda qb, b: (lat0 + b, QB_VA)),
                  pl.BlockSpec((None, None, PAST_LEN, D_A), lambda qb, b: (b, layer, 0, 0)),
                  pl.BlockSpec((None, None, PAST_LEN, D_A), lambda qb, b: (b, layer, 0, 0)),
                  pl.BlockSpec((None, N_HEADS_A, N_REL_ROWS + 1, GRID_W, 2 * GRID_W),
                               lambda qb, b: (layer, 0, 0, 0, 0))],
        out_specs=pl.BlockSpec((NQ_TOK, D_A), lambda qb, b: (b * NQ_BLOCKS + qb, 0)),
        out_shape=jax.ShapeDtypeStruct((T_LAT, D_A), BF16),
        scratch_shapes=[pltpu.VMEM((N_HEADS_A, NQ_TOK, NW_TOK), F32)],
        compiler_params=_cparams(("arbitrary", "arbitrary")),
        name="nbr_attention",
    )(zb, zb, zb, cache_k, cache_v, toe)


CONV_HALO = 16
CONV_ROWS = 64


SUBLANES = 8


def _conv_kernel(u_ref, g_ref, w_ref, b_ref, lg_ref, lb_ref, o_ref, pad_scr, sh_scr, *, seq):
    zeros = jnp.zeros((CONV_HALO, D_CONV), F32)
    pad_scr[0:CONV_HALO, :] = zeros
    pad_scr[CONV_HALO + seq:2 * CONV_HALO + seq, :] = zeros
    pad_scr[CONV_HALO:CONV_HALO + seq, :] = u_ref[...] * _sigmoid(g_ref[...])
    n_sh = seq + 2 * CONV_HALO - SUBLANES
    for s in range(SUBLANES):
        sh_scr[s] = pad_scr[s:s + n_sh, :]
    first = CONV_HALO - CONV_WIDTH // 2
    for c in range(seq // CONV_ROWS):
        base = c * CONV_ROWS
        acc = jnp.broadcast_to(b_ref[...], (CONV_ROWS, D_CONV))
        for j in range(CONV_WIDTH):
            q, s = divmod(first + j, SUBLANES)
            row0 = base + q * SUBLANES
            acc = acc + sh_scr[s, row0:row0 + CONV_ROWS, :] * w_ref[j:j + 1, :]
        mu = jnp.mean(acc, axis=-1, keepdims=True)
        xc = acc - mu
        var = jnp.mean(xc * xc, axis=-1, keepdims=True)
        y = xc * lax.rsqrt(var + LN_EPS) * lg_ref[...] + lb_ref[...]
        o_ref[base:base + CONV_ROWS, :] = (y * _sigmoid(y)).astype(o_ref.dtype)


def _conv(z, w_dw, b_dw, ln_g, ln_b, seq, n_seq, row_block0):
    def vec():
        return pl.BlockSpec((1, D_CONV), lambda b: (0, 0))

    return pl.pallas_call(
        functools.partial(_conv_kernel, seq=seq),
        grid=(n_seq,),
        in_specs=[pl.BlockSpec((seq, D_CONV), lambda b: (row_block0 + b, ZF_CU)),
                  pl.BlockSpec((seq, D_CONV), lambda b: (row_block0 + b, ZF_CG)),
                  pl.BlockSpec((CONV_WIDTH, D_CONV), lambda b: (0, 0)),
                  vec(), vec(), vec()],
        out_specs=pl.BlockSpec((seq, D_CONV), lambda b: (b, 0)),
        out_shape=jax.ShapeDtypeStruct((n_seq * seq, D_CONV), BF16),
        scratch_shapes=[pltpu.VMEM((seq + 2 * CONV_HALO, D_CONV), F32),
                        pltpu.VMEM((SUBLANES, seq + 2 * CONV_HALO - SUBLANES, D_CONV), F32)],
        compiler_params=_cparams(("arbitrary",)),
        name="conformer_conv",
    )(z, z, w_dw, b_dw, ln_g, ln_b)


N_STREAM = 2 * N_HEADS_M
MCHUNK = 256


PAIR_M = 2 * HEAD_DIM_M
N_PAIR = N_STREAM // 2


def _per_head_rows(r):
    return jnp.concatenate([jnp.broadcast_to(r[0:1], (HEAD_DIM_M, r.shape[1])),
                            jnp.broadcast_to(r[1:2], (HEAD_DIM_M, r.shape[1]))], axis=0)


def _mlstm_kernel(q_ref, k_ref, v_ref, om_ref, gt_ref, c0_ref, n0_ref, m0_ref, ng_ref,
                  y_ref, c_out, n_out, m_out, hf_scr, hb_scr, c_scr, n_scr, m_scr, cbd_scr, vbd_scr,
                  rows_scr, acol_scr, *, seq):
    nc = seq // MCHUNK
    c_scr[...] = c0_ref[...]
    n_scr[...] = n0_ref[...]
    m_scr[...] = m0_ref[...]
    cbd_scr[...] = jnp.zeros_like(cbd_scr)
    vbd_scr[...] = jnp.zeros_like(vbd_scr)
    for s_id in range(N_STREAM):
        blk = slice((s_id % 2) * HEAD_DIM_M, (s_id % 2 + 1) * HEAD_DIM_M)
        cbd_scr[s_id // 2, blk, blk] = c0_ref[s_id].astype(BF16)
    first_head = lax.broadcasted_iota(jnp.int32, (1, PAIR_M), 1) < HEAD_DIM_M
    rows = lax.broadcasted_iota(jnp.int32, (MCHUNK, MCHUNK), 0)
    cols = lax.broadcasted_iota(jnp.int32, (MCHUNK, MCHUNK), 1)
    tris = ((cols <= rows).astype(F32), (cols >= rows).astype(F32))
    reach = (rows <= cols, rows >= cols)
    lane = lax.broadcasted_iota(jnp.int32, (1, MCHUNK), 1)
    kscale = HEAD_DIM_M ** -0.5

    for cc in range(nc):
        g = gt_ref[cc * MCHUNK:(cc + 1) * MCHUNK, :]
        lf = jnp.minimum(g, 0.0) - jnp.log(1.0 + jnp.exp(-jnp.abs(g)))
        g_t = g.T
        for d in range(2):
            cum = jnp.dot(tris[d], lf, precision=HIGHEST, preferred_element_type=F32)
            cum_t = cum.T
            i0 = 2 * N_HEADS_M * d
            i_rows = g_t[i0:i0 + N_HEADS_M, :]
            b_rows = cum_t[i0 + N_HEADS_M:i0 + 2 * N_HEADS_M, :]
            reach_max = i_rows - b_rows
            for step in (1 << s for s in range(MCHUNK.bit_length() - 1)):
                if d == 0:
                    shifted = jnp.where(lane >= step, pltpu.roll(reach_max, step, axis=1), NEG)
                else:
                    shifted = jnp.where(lane < MCHUNK - step, pltpu.roll(reach_max, MCHUNK - step, axis=1), NEG)
                reach_max = jnp.maximum(reach_max, shifted)
            rows_scr[2 * cc + d, 0] = i_rows
            rows_scr[2 * cc + d, 1] = b_rows
            rows_scr[2 * cc + d, 2] = reach_max
            acol_scr[2 * cc + d] = g - pltpu.roll(cum, cum.shape[1] - N_HEADS_M, axis=1)

    def chunk_step(c, carry):
        for d in range(2):
            cidx = c if d == 0 else nc - 1 - c
            off = pl.multiple_of(cidx * MCHUNK, MCHUNK)
            last = MCHUNK - 1 if d == 0 else 0
            i0 = 2 * N_HEADS_M * d
            heads = slice(N_HEADS_M * d, N_HEADS_M * (d + 1))
            i_rows = rows_scr[2 * cidx + d, 0]
            b_rows = rows_scr[2 * cidx + d, 1]
            reach_max = rows_scr[2 * cidx + d, 2]
            a_cols = acol_scr[2 * cidx + d]
            m_prevs = m_scr[heads, 0:1]
            inters = b_rows + m_prevs
            m_ts = jnp.maximum(inters, b_rows + reach_max)
            w_inters = jnp.exp(inters - m_ts)
            floors = jnp.exp(-m_ts)
            b_lasts = b_rows[:, last:last + 1]
            m_news = m_ts[:, last:last + 1]
            w_prevs = jnp.exp(b_lasts + m_prevs - m_news)
            w_srcs = kscale * jnp.exp(b_lasts - b_rows + i_rows - m_news)
            m_scr[heads, :] = jnp.broadcast_to(m_news, (N_HEADS_M, HEAD_DIM_M))
            for hp in range(N_HEADS_M // 2):
                pid = (N_HEADS_M // 2) * d + hp
                hh = slice(2 * hp, 2 * hp + 2)
                ps = slice(hp * PAIR_M, (hp + 1) * PAIR_M)
                q2 = q_ref[pl.ds(off, MCHUNK), ps]
                k2 = k_ref[pl.ds(off, MCHUNK), ps]
                v2_t = v_ref[pl.ds(off, MCHUNK), ps].astype(F32).T
                k_zero = jnp.zeros_like(k2)
                k_stack = jnp.concatenate([jnp.where(first_head, k2, k_zero),
                                           jnp.where(first_head, k_zero, k2)], axis=0)
                decay = jnp.concatenate(
                    [jnp.exp(jnp.where(reach[d], b_rows[h:h + 1] + a_cols[:, i0 + h:i0 + h + 1], NEG)
                             - m_ts[h:h + 1]) for h in (2 * hp, 2 * hp + 1)], axis=0)
                s_t = _dot_nt(k_stack, q2) * (kscale * decay)
                col_sums = jnp.concatenate([jnp.sum(s_t[:MCHUNK], axis=0, keepdims=True),
                                            jnp.sum(s_t[MCHUNK:], axis=0, keepdims=True)], axis=0)
                n2 = n_scr[N_HEADS_M * d + 2 * hp:N_HEADS_M * d + 2 * hp + 2, :]
                n_zero = jnp.zeros((1, HEAD_DIM_M), F32)
                n_mat = jnp.concatenate([jnp.concatenate([n2[0:1], n_zero], axis=1),
                                         jnp.concatenate([n_zero, n2[1:2]], axis=1),
                                         jnp.zeros((SUBLANES - 2, PAIR_M), F32)], axis=0)
                n_q = _dot_nt(n_mat.astype(BF16), q2)[0:2, :]
                den = w_inters[hh] * n_q + col_sums
                inv = 1.0 / jnp.maximum(jnp.abs(den), floors[hh])
                vbd_scr[pid, :HEAD_DIM_M, :MCHUNK] = v2_t[:HEAD_DIM_M].astype(BF16)
                vbd_scr[pid, HEAD_DIM_M:, MCHUNK:] = v2_t[HEAD_DIM_M:].astype(BF16)
                num_t = (_per_head_rows(w_inters[hh]) * _dot_nt(cbd_scr[pid], q2)
                         + jnp.dot(vbd_scr[pid], s_t.astype(BF16), preferred_element_type=F32))
                h_t = num_t * _per_head_rows(inv)
                if d == 0:
                    hf_scr[cidx, ps, :] = h_t
                else:
                    hb_scr[cidx, ps, :] = h_t
                upd = jnp.dot((v2_t * _per_head_rows(w_srcs[hh])).astype(BF16), k2,
                              preferred_element_type=F32)
                w_mat = jnp.concatenate([w_srcs[hh], jnp.zeros((SUBLANES - 2, MCHUNK), F32)], axis=0)
                n_upd = jnp.dot(w_mat.astype(BF16), k2, preferred_element_type=F32)
                for j in range(2):
                    h = 2 * hp + j
                    s_id = N_HEADS_M * d + h
                    blk = slice(j * HEAD_DIM_M, (j + 1) * HEAD_DIM_M)
                    c_new = w_prevs[h:h + 1] * c_scr[s_id] + upd[blk, blk]
                    c_scr[s_id] = c_new
                    cbd_scr[pid, blk, blk] = c_new.astype(BF16)
                    n_scr[s_id:s_id + 1, :] = w_prevs[h:h + 1] * n2[j:j + 1] + n_upd[j:j + 1, blk]
        return carry

    lax.fori_loop(0, nc, chunk_step, 0)

    for c in range(nc):
        ts = slice(c * MCHUNK, (c + 1) * MCHUNK)
        for h in range(N_HEADS_M):
            hs = slice(h * HEAD_DIM_M, (h + 1) * HEAD_DIM_M)
            hsum = hf_scr[c, hs, :] + hb_scr[c, hs, :]
            mu = jnp.mean(hsum, axis=0, keepdims=True)
            xc = hsum - mu
            var = jnp.mean(xc * xc, axis=0, keepdims=True)
            hn = (xc * lax.rsqrt(var + LN_EPS)).T
            y_ref[ts, hs] = (_sigmoid(om_ref[ts, hs]) * (hn * ng_ref[:, hs])).astype(y_ref.dtype)
    c_out[...] = c_scr[...]
    n_out[...] = n_scr[...]
    m_out[...] = m_scr[...]


def _mlstm(zb, zf, c0, n0, m0, norm_g, seq, n_seq, row_block0, state_map):
    lead = len(state_map(0))

    def zspec(cb):
        return pl.BlockSpec((seq, D_M), lambda b: (row_block0 + b, cb))

    def sspec(tail):
        return pl.BlockSpec((None,) * lead + tail, lambda b: state_map(b) + (0,) * len(tail))

    return pl.pallas_call(
        functools.partial(_mlstm_kernel, seq=seq),
        grid=(n_seq,),
        in_specs=[zspec(QB_QM), zspec(QB_KM), zspec(QB_VM), zspec(ZF_OM),
                  pl.BlockSpec((seq, 128), lambda b: (row_block0 + b, ZF_GATES)),
                  sspec((N_STREAM, HEAD_DIM_M, HEAD_DIM_M)),
                  sspec((N_STREAM, HEAD_DIM_M)),
                  sspec((N_STREAM, HEAD_DIM_M)),
                  pl.BlockSpec((1, D_M), lambda b: (0, 0))],
        out_specs=[pl.BlockSpec((seq, D_M), lambda b: (b, 0)),
                   pl.BlockSpec((None, N_STREAM, HEAD_DIM_M, HEAD_DIM_M), lambda b: (b, 0, 0, 0)),
                   pl.BlockSpec((None, N_STREAM, HEAD_DIM_M), lambda b: (b, 0, 0)),
                   pl.BlockSpec((None, N_STREAM, HEAD_DIM_M), lambda b: (b, 0, 0))],
        out_shape=[jax.ShapeDtypeStruct((n_seq * seq, D_M), BF16),
                   jax.ShapeDtypeStruct((n_seq, N_STREAM, HEAD_DIM_M, HEAD_DIM_M), F32),
                   jax.ShapeDtypeStruct((n_seq, N_STREAM, HEAD_DIM_M), F32),
                   jax.ShapeDtypeStruct((n_seq, N_STREAM, HEAD_DIM_M), F32)],
        scratch_shapes=[pltpu.VMEM((seq // MCHUNK, D_M, MCHUNK), F32),
                        pltpu.VMEM((seq // MCHUNK, D_M, MCHUNK), F32),
                        pltpu.VMEM((N_STREAM, HEAD_DIM_M, HEAD_DIM_M), F32),
                        pltpu.VMEM((N_STREAM, HEAD_DIM_M), F32),
                        pltpu.VMEM((N_STREAM, HEAD_DIM_M), F32),
                        pltpu.VMEM((N_PAIR, PAIR_M, PAIR_M), BF16),
                        pltpu.VMEM((N_PAIR, PAIR_M, 2 * MCHUNK), BF16),
                        pltpu.VMEM((2 * (seq // MCHUNK), 3, N_HEADS_M, MCHUNK), F32),
                        pltpu.VMEM((2 * (seq // MCHUNK), MCHUNK, 128), F32)],
        compiler_params=_cparams(("arbitrary",)),
        name="mlstm",
    )(zb, zb, zb, zf, zf, c0, n0, m0, norm_g)


N_CTX_TILES = T_CTX // TM_TOK


def _merge_kernel(x_ref, mod_ref, ya_c, ya_l, yc_c, yc_l, ym_c, ym_l, ga_ref, gc_ref, gm_ref,
                  wa_ref, wc_ref, wm_ref, wo_ref, o_ref):
    is_ctx = pl.program_id(0) < N_CTX_TILES

    def branch(y_ctx, y_lat, g_ref, w_ref):
        y = jnp.where(is_ctx, y_ctx[...], y_lat[...])
        return g_ref[...].astype(F32) * jnp.dot(y, w_ref[...], preferred_element_type=F32)

    merged = (branch(ya_c, ya_l, ga_ref, wa_ref) + branch(yc_c, yc_l, gc_ref, wc_ref)
              + branch(ym_c, ym_l, gm_ref, wm_ref))
    mix = jnp.dot(merged.astype(BF16), wo_ref[...], preferred_element_type=F32)
    o_ref[...] = x_ref[...] + mod_ref[2:3, :] * mix


def _merge(x, mod, ya, yc, ym, zg, w_pa, w_pc, w_pm, w_out, layer):
    def rows(width, cb=0):
        return pl.BlockSpec((TM_TOK, width), lambda i: (i, cb))

    def ctx_rows(width):
        return pl.BlockSpec((TM_TOK, width), lambda i: (jnp.minimum(i, N_CTX_TILES - 1), 0))

    def lat_rows(width):
        return pl.BlockSpec((TM_TOK, width), lambda i: (jnp.maximum(i - N_CTX_TILES, 0), 0))

    def full(shape):
        return pl.BlockSpec((None,) + shape, lambda i: (layer, 0, 0))

    return pl.pallas_call(
        _merge_kernel,
        grid=(T_ALL // TM_TOK,),
        in_specs=[rows(D_MODEL),
                  pl.BlockSpec((None, 6, D_MODEL), lambda i: (_seg_of_tile(i, TM_TOK), 0, 0)),
                  ctx_rows(D_A), lat_rows(D_A), ctx_rows(D_CONV), lat_rows(D_CONV),
                  ctx_rows(D_M), lat_rows(D_M),
                  rows(D_MODEL, ZG_GA), rows(D_MODEL, ZG_GC), rows(D_MODEL, ZG_GM),
                  full((D_A, D_MODEL)), full((D_CONV, D_MODEL)), full((D_M, D_MODEL)),
                  full((D_MODEL, D_MODEL))],
        out_specs=rows(D_MODEL),
        out_shape=jax.ShapeDtypeStruct((T_ALL, D_MODEL), F32),
        compiler_params=_cparams(("arbitrary",)),
        name="merge",
    )(x, mod, ya[0], ya[1], yc[0], yc[1], ym[0], ym[1], zg, zg, zg, w_pa, w_pc, w_pm, w_out)


def _route_sort_kernel(x_ref, g_ref, mod_ref, wr_ref, br_ref, xt_ref, pos_ref, gate_ref, nch_ref, seg_ref):
    h = _normmod(x_ref[...], g_ref[...], mod_ref[...], 3, 4)
    hb = h.astype(BF16)
    logits = _dot_nt(wr_ref[...].astype(BF16), hb) + br_ref[...]
    e_iota = lax.broadcasted_iota(jnp.int32, (N_EXPERTS, TM_MOE), 0).astype(F32)
    sels, vals = [], []
    l = logits
    for k in range(TOP_K):
        m = jnp.max(l, axis=0, keepdims=True)
        idx = jnp.min(jnp.where(l == m, e_iota, float(N_EXPERTS)), axis=0, keepdims=True)
        sel = e_iota == idx
        vals.append(m)
        sels.append(sel)
        l = jnp.where(sel, -jnp.inf, l)
    exps = [jnp.exp(v - vals[0]) for v in vals]
    tot = exps[0] + exps[1] + exps[2] + exps[3]
    onehot = jnp.zeros((N_EXPERTS, TM_MOE), F32)
    for k in range(TOP_K):
        gate_ref[k:k + 1, :] = exps[k] / tot
        onehot = onehot + sels[k].astype(F32)
    gate_ref[TOP_K:8, :] = jnp.zeros((8 - TOP_K, TM_MOE), F32)

    cnt = jnp.sum(onehot, axis=1, keepdims=True)
    nch = jnp.floor((cnt + (CHUNK_ROWS - 1)) / CHUNK_ROWS)
    ei = lax.broadcasted_iota(jnp.int32, (N_EXPERTS, N_EXPERTS), 0)
    ej = lax.broadcasted_iota(jnp.int32, (N_EXPERTS, N_EXPERTS), 1)
    seg = jnp.dot((ej < ei).astype(F32), jnp.broadcast_to(nch, (N_EXPERTS, 128)), precision=HIGHEST,
                  preferred_element_type=F32)
    nch_ref[...] = jnp.broadcast_to(nch, (N_EXPERTS, 128)).astype(jnp.int32)
    seg_ref[...] = seg.astype(jnp.int32)

    t_src = lax.broadcasted_iota(jnp.int32, (TM_MOE, TM_MOE), 0)
    t_dst = lax.broadcasted_iota(jnp.int32, (TM_MOE, TM_MOE), 1)
    before = (t_src < t_dst).astype(BF16)
    row_of = (seg[:, 0:1] * CHUNK_ROWS
              + jnp.dot(onehot.astype(BF16), before, preferred_element_type=F32))
    q_iota = lax.broadcasted_iota(jnp.int32, (Q_TILE, TM_MOE), 0)
    perm = jnp.zeros((Q_TILE, TM_MOE), F32)
    for k in range(TOP_K):
        q_k = jnp.sum(jnp.where(sels[k], row_of, 0.0), axis=0, keepdims=True).astype(jnp.int32)
        pos_ref[k:k + 1, :] = q_k
        perm = jnp.where(q_iota == q_k, 1.0, perm)
    pos_ref[TOP_K:8, :] = jnp.zeros((8 - TOP_K, TM_MOE), jnp.int32)
    xt_ref[...] = jnp.dot(perm.astype(BF16), hb, preferred_element_type=F32).astype(BF16)


def _route_sort(x, norm_g, mod, w_rt, b_r):
    tspec = pl.BlockSpec((8, TM_MOE), lambda i: (0, i))
    mspec = pl.BlockSpec((None, N_EXPERTS, 128), lambda i: (i, 0, 0))
    meta = jax.ShapeDtypeStruct((N_TILES, N_EXPERTS, 128), jnp.int32)
    return pl.pallas_call(
        _route_sort_kernel,
        grid=(N_TILES,),
        in_specs=[pl.BlockSpec((TM_MOE, D_MODEL), lambda i: (i, 0)),
                  pl.BlockSpec((1, D_MODEL), lambda i: (0, 0)),
                  pl.BlockSpec((None, 6, D_MODEL), lambda i: (_seg_of_tile(i, TM_MOE), 0, 0)),
                  pl.BlockSpec((N_EXPERTS, D_MODEL), lambda i: (0, 0)),
                  pl.BlockSpec((N_EXPERTS, 1), lambda i: (0, 0))],
        out_specs=[pl.BlockSpec((Q_TILE, D_MODEL), lambda i: (i, 0)), tspec, tspec, mspec, mspec],
        out_shape=[jax.ShapeDtypeStruct((N_TILES * Q_TILE, D_MODEL), BF16),
                   jax.ShapeDtypeStruct((8, T_ALL), jnp.int32), jax.ShapeDtypeStruct((8, T_ALL), F32),
                   meta, meta],
        compiler_params=_cparams(("arbitrary",)),
        name="moe_route_sort",
    )(x, norm_g, mod, w_rt, b_r)


def _expert_kernel(nch_ref, seg_ref, wgu_ref, bgu_ref, wd_ref, bd_ref, xt_ref, yt_ref,
                   wgu_scr, wd_scr, xbuf, ybuf, row_scr, gstart_scr, gsem, ssem):
    del xt_ref
    e = pl.program_id(0)

    def src_row(row):
        return pl.multiple_of(jnp.where(row >= 0, row, READ_SPARE), CHUNK_ROWS)

    def dst_row(row, slot, c):
        spare = c * Q_TILE + jnp.where(slot == 0, Q_TILE - CHUNK_ROWS, Q_TILE - 2 * CHUNK_ROWS)
        return pl.multiple_of(jnp.where(row >= 0, row, spare), CHUNK_ROWS)

    def chunk_rows(c):
        return slice(c * CHUNK_ROWS, (c + 1) * CHUNK_ROWS)

    def start_in(g):
        slot = g % 2
        for c in range(CPG):
            row = row_scr[g * CPG + c]
            pltpu.make_async_copy(yt_ref.at[pl.ds(src_row(row), CHUNK_ROWS), :], xbuf.at[slot, chunk_rows(c), :],
                                  gsem.at[slot]).start()

    def start_out(g, slot):
        for c in range(CPG):
            row = row_scr[g * CPG + c]
            pltpu.make_async_copy(ybuf.at[slot, chunk_rows(c), :],
                                  yt_ref.at[pl.ds(dst_row(row, slot, c), CHUNK_ROWS), :], ssem.at[slot]).start()

    def wait_in(slot):
        pltpu.make_async_copy(yt_ref.at[pl.ds(0, E_GROUP), :], xbuf.at[slot], gsem.at[slot]).wait()

    def wait_out(slot):
        pltpu.make_async_copy(ybuf.at[slot], yt_ref.at[pl.ds(0, E_GROUP), :], ssem.at[slot]).wait()

    @pl.when(e == 0)
    def _():
        def per_expert(ee, cnt):
            gstart_scr[ee] = cnt // CPG

            def per_tile(t, cnt):
                first = (t * CH_PER_TILE + seg_ref[t * N_EXPERTS + ee]) * CHUNK_ROWS

                def per_chunk(j, cnt):
                    row_scr[cnt] = first + j * CHUNK_ROWS
                    return cnt + 1

                return lax.fori_loop(0, nch_ref[t * N_EXPERTS + ee], per_chunk, cnt)

            cnt = lax.fori_loop(0, N_TILES, per_tile, cnt)
            padded = (cnt + CPG - 1) // CPG * CPG

            def pad(i, carry):
                row_scr[i] = -1
                return carry

            lax.fori_loop(cnt, padded, pad, 0)
            return padded

        total = lax.fori_loop(0, N_EXPERTS, per_expert, 0)
        gstart_scr[N_EXPERTS] = total // CPG

        def pad(i, carry):
            row_scr[i] = -1
            return carry

        lax.fori_loop(total, total + CPG, pad, 0)
        ybuf[...] = jnp.zeros_like(ybuf)
        start_out(total // CPG, 0)
        start_out(total // CPG, 1)
        start_in(0)

    g_first = gstart_scr[e]
    g_end = gstart_scr[e + 1]

    @pl.when(g_end > g_first)
    def _():
        wgu_scr[...] = wgu_ref[...].astype(BF16)
        wd_scr[...] = wd_ref[...].astype(BF16)

    def group_step(g, carry):
        slot = g % 2
        start_in(g + 1)
        wait_in(slot)
        wait_out(slot)
        hgu = jnp.dot(xbuf[slot], wgu_scr[...], preferred_element_type=F32) + bgu_ref[...]
        h_glu = jnp.minimum(hgu[:, :D_EXPERT], SWIGLU_LIMIT)
        h_lin = jnp.clip(hgu[:, D_EXPERT:], -SWIGLU_LIMIT, SWIGLU_LIMIT)
        act = (h_lin + 1.0) * (h_glu * _sigmoid(SWIGLU_ALPHA * h_glu))
        y = jnp.dot(act.astype(BF16), wd_scr[...], preferred_element_type=F32) + bd_ref[...]
        ybuf[slot] = y.astype(BF16)
        start_out(g, slot)
        return carry

    lax.fori_loop(g_first, g_end, group_step, 0)

    @pl.when(e == N_EXPERTS - 1)
    def _():
        wait_in(gstart_scr[N_EXPERTS] % 2)
        wait_out(0)
        wait_out(1)


def _experts(nch_flat, seg_flat, xt, w_gu, b_gu, w_down, b_down, layer):
    return pl.pallas_call(
        _expert_kernel,
        grid_spec=pltpu.PrefetchScalarGridSpec(
            num_scalar_prefetch=2,
            grid=(N_EXPERTS,),
            in_specs=[pl.BlockSpec((None, None, D_MODEL, 2 * D_EXPERT), lambda e, n, s: (layer, e, 0, 0)),
                      pl.BlockSpec((None, None, 1, 2 * D_EXPERT), lambda e, n, s: (layer, e, 0, 0)),
                      pl.BlockSpec((None, None, D_EXPERT, D_MODEL), lambda e, n, s: (layer, e, 0, 0)),
                      pl.BlockSpec((None, None, 1, D_MODEL), lambda e, n, s: (layer, e, 0, 0)),
                      pl.BlockSpec(memory_space=pl.ANY)],
            out_specs=pl.BlockSpec(memory_space=pl.ANY),
            scratch_shapes=[pltpu.VMEM((D_MODEL, 2 * D_EXPERT), BF16),
                            pltpu.VMEM((D_EXPERT, D_MODEL), BF16),
                            pltpu.VMEM((2, E_GROUP, D_MODEL), BF16),
                            pltpu.VMEM((2, E_GROUP, D_MODEL), BF16),
                            pltpu.SMEM((MAX_CHUNKS,), jnp.int32),
                            pltpu.SMEM((N_EXPERTS + 1,), jnp.int32),
                            pltpu.SemaphoreType.DMA((2,)),
                            pltpu.SemaphoreType.DMA((2,))]),
        out_shape=jax.ShapeDtypeStruct((N_TILES * Q_TILE, D_MODEL), BF16),
        input_output_aliases={6: 0},
        compiler_params=_cparams(("arbitrary",)),
        name="moe_experts",
    )(nch_flat, seg_flat, w_gu, b_gu, w_down, b_down, xt)


MOE_CTX_TILES = T_CTX // TM_MOE


def _combine_rows(x_ref, mod_ref, pos_ref, gate_ref, yt_ref):
    lane = lax.broadcasted_iota(jnp.int32, (TM_MOE, Q_TILE), 1)
    sel = jnp.zeros((TM_MOE, Q_TILE), F32)
    for k in range(TOP_K):
        sel = jnp.where(lane == pos_ref[:, k:k + 1], gate_ref[:, k:k + 1], sel)
    acc = jnp.dot(sel.astype(BF16), yt_ref[...], preferred_element_type=F32)
    return x_ref[...] + mod_ref[5:6, :] * acc


def _combine_kernel(x_ref, mod_ref, pos_ref, gate_ref, yt_ref, o_ref):
    o_ref[...] = _combine_rows(x_ref, mod_ref, pos_ref, gate_ref, yt_ref)


def _combine_final_kernel(x_ref, mod_ref, pos_ref, gate_ref, yt_ref, fg_ref, ctx_ref, lat_ref):
    y = _combine_rows(x_ref, mod_ref, pos_ref, gate_ref, yt_ref)
    y = y * lax.rsqrt(jnp.mean(y * y, axis=-1, keepdims=True) + RMS_EPS) * fg_ref[...]
    is_ctx = pl.program_id(0) < MOE_CTX_TILES

    @pl.when(is_ctx)
    def _():
        ctx_ref[...] = y

    @pl.when(jnp.logical_not(is_ctx))
    def _():
        lat_ref[...] = y


def _combine(x, mod, pos_t, gate_t, yt, final_g=None):
    tile = pl.BlockSpec((TM_MOE, D_MODEL), lambda i: (i, 0))
    in_specs = [tile,
                pl.BlockSpec((None, 6, D_MODEL), lambda i: (_seg_of_tile(i, TM_MOE), 0, 0)),
                pl.BlockSpec((TM_MOE, 8), lambda i: (i, 0)),
                pl.BlockSpec((TM_MOE, 8), lambda i: (i, 0)),
                pl.BlockSpec((Q_TILE, D_MODEL), lambda i: (i, 0))]
    if final_g is None:
        return pl.pallas_call(
            _combine_kernel, grid=(N_TILES,), in_specs=in_specs, out_specs=tile,
            out_shape=jax.ShapeDtypeStruct((T_ALL, D_MODEL), F32),
            compiler_params=_cparams(("arbitrary",)), name="moe_combine",
        )(x, mod, pos_t, gate_t, yt)
    return pl.pallas_call(
        _combine_final_kernel, grid=(N_TILES,),
        in_specs=in_specs + [pl.BlockSpec((1, D_MODEL), lambda i: (0, 0))],
        out_specs=[pl.BlockSpec((TM_MOE, D_MODEL), lambda i: (jnp.minimum(i, MOE_CTX_TILES - 1), 0)),
                   pl.BlockSpec((TM_MOE, D_MODEL), lambda i: (jnp.maximum(i - MOE_CTX_TILES, 0), 0))],
        out_shape=[jax.ShapeDtypeStruct((T_CTX, D_MODEL), F32), jax.ShapeDtypeStruct((T_LAT, D_MODEL), F32)],
        compiler_params=_cparams(("arbitrary",)), name="moe_combine_final",
    )(x, mod, pos_t, gate_t, yt, final_g)


def _moe(x, norm_g, mod, w_rt, b_r, w_gu, b_gu, w_down, b_down, layer, final_g=None):
    xt, pos, gate, nch, seg = _route_sort(x, norm_g, mod, w_rt, b_r)
    yt = _experts(nch[:, :, 0].reshape(-1), seg[:, :, 0].reshape(-1), xt, w_gu,
                  b_gu.reshape(DEPTH, N_EXPERTS, 1, 2 * D_EXPERT), w_down,
                  b_down.reshape(DEPTH, N_EXPERTS, 1, D_MODEL), layer)
    return _combine(x, mod, pos.T, gate.T, yt, final_g)


PREP_ROWS = 128


def _prep_in_kernel(w_ref, zb_ref, zg_ref, zf_ref):
    conv0, mq0, om0 = 3 * D_A, 3 * D_A + 2 * D_CONV, 3 * D_A + 2 * D_CONV + 3 * D_M
    gates_end = GATE_OFF + N_GATE_M
    zb_ref[:, :conv0] = w_ref[:, :conv0].astype(BF16)
    zb_ref[:, conv0:] = w_ref[:, mq0:om0].astype(BF16)
    zg_ref[...] = w_ref[:, gates_end:].astype(BF16)
    zf_ref[:, :mq0 - conv0] = w_ref[:, conv0:mq0].astype(BF16)
    zf_ref[:, mq0 - conv0:mq0 - conv0 + D_M] = w_ref[:, om0:GATE_OFF].astype(BF16)
    tail = w_ref[:, GATE_OFF:GATE_OFF + 128]
    keep = lax.broadcasted_iota(jnp.int32, tail.shape, 1) < N_GATE_M
    zf_ref[:, mq0 - conv0 + D_M:] = jnp.where(keep, tail, 0.0).astype(BF16)


def _prep_in_weights(w_in):
    def out(n):
        return (pl.BlockSpec((None, PREP_ROWS, n), lambda l, i: (l, i, 0)),
                jax.ShapeDtypeStruct((DEPTH, D_MODEL, n), BF16))

    specs, shapes = zip(out(N_ZB), out(N_ZG), out(N_ZF))
    return pl.pallas_call(
        _prep_in_kernel,
        grid=(DEPTH, D_MODEL // PREP_ROWS),
        in_specs=[pl.BlockSpec((None, PREP_ROWS, w_in.shape[-1]), lambda l, i: (l, i, 0))],
        out_specs=list(specs),
        out_shape=list(shapes),
        compiler_params=_cparams(("arbitrary", "arbitrary")),
        name="prep_in_weights",
    )(w_in)


def _split_in_cols(w):
    conv0, mq0, om0 = 3 * D_A, 3 * D_A + 2 * D_CONV, 3 * D_A + 2 * D_CONV + 3 * D_M
    gates_end = GATE_OFF + N_GATE_M
    pad = jnp.zeros(w.shape[:-1] + (N_ZF - (2 * D_CONV + D_M + N_GATE_M),), w.dtype)
    zb = jnp.concatenate([w[..., :conv0], w[..., mq0:om0]], axis=-1)
    zg = w[..., gates_end:]
    zf = jnp.concatenate([w[..., conv0:mq0], w[..., om0:GATE_OFF], w[..., GATE_OFF:gates_end], pad], axis=-1)
    return zb, zg, zf


def kernel(x_prompt, x_sample, cache_k, cache_v, state_C, state_n, state_m, c, c_ctx, norm1_g, w_mod, b_mod, w_in, b_in, rpb, w_dw, b_dw, cln_g, cln_b, mnorm_g, w_pa, w_pc, w_pm, w_out, norm2_g, w_router, b_router, w_gu, b_gu, w_down, b_down, final_g):
    cond = jnp.concatenate([c_ctx[None, :], c, jnp.zeros((SEG_PAD - N_SEG, D_MODEL), F32)], axis=0)
    mod_all = _modulation(cond, w_mod, b_mod).reshape(DEPTH, SEG_PAD, 6, D_MODEL)

    x = jnp.concatenate([x_prompt.reshape(T_CTX, D_MODEL), x_sample.reshape(T_LAT, D_MODEL)], axis=0)
    ck = cache_k.reshape(DEC_BATCH, DEPTH, PAST_LEN, D_A).astype(BF16)
    cv = cache_v.reshape(DEC_BATCH, DEPTH, PAST_LEN, D_A).astype(BF16)
    lat_c0 = state_C.reshape(DEC_BATCH, DEPTH, N_STREAM, HEAD_DIM_M, HEAD_DIM_M)
    lat_n0 = state_n.reshape(DEC_BATCH, DEPTH, N_STREAM, HEAD_DIM_M)
    lat_m0 = jnp.broadcast_to(state_m.reshape(DEC_BATCH, DEPTH, N_STREAM, 1),
                              (DEC_BATCH, DEPTH, N_STREAM, HEAD_DIM_M))
    ctx_c0 = jnp.zeros((1, N_STREAM, HEAD_DIM_M, HEAD_DIM_M), F32)
    ctx_n0 = jnp.zeros((1, N_STREAM, HEAD_DIM_M), F32)
    ctx_m0 = jnp.full((1, N_STREAM, HEAD_DIM_M), -jnp.inf, F32)

    w_in_groups = _prep_in_weights(w_in)
    b_in_groups = _split_in_cols(b_in[:, None, :])
    merge_w = tuple(w.astype(BF16) for w in (w_pa, w_pc, w_pm, w_out))
    toe = _natt_bias(rpb)
    ks, vs, cs, ns, ms = [], [], [], [], []
    for l in range(DEPTH):
        mod = mod_all[l]
        g1 = norm1_g[l][None, :]
        zb, zg, zf, k_l, v_l = _in_proj(x, g1, mod, w_in_groups, b_in_groups, l)
        ya = (_ctx_attention(zb), _natt(zb, ck, cv, toe, l))
        conv_w = (w_dw[l], b_dw[l][None, :], cln_g[l][None, :], cln_b[l][None, :])
        yc = (_conv(zf, *conv_w, SEQ, BATCH, 0), _conv(zf, *conv_w, DEC_SEQ, DEC_BATCH, T_CTX // DEC_SEQ))
        ng = mnorm_g[l][None, :]
        ym_ctx, c_l, n_l, m_l = _mlstm(zb, zf, ctx_c0, ctx_n0, ctx_m0, ng, SEQ, BATCH, 0, lambda b: (0,))
        ym_lat, _, _, _ = _mlstm(zb, zf, lat_c0, lat_n0, lat_m0, ng, DEC_SEQ, DEC_BATCH, T_CTX // DEC_SEQ,
                                 lambda b: (b, l))
        x = _merge(x, mod, ya, yc, (ym_ctx, ym_lat), zg, *merge_w, l)
        x = _moe(x, norm2_g[l][None, :], mod, w_router[l].T, b_router[l][:, None],
                 w_gu, b_gu, w_down, b_down, l, final_g[None, :] if l == DEPTH - 1 else None)
        ks.append(k_l.reshape(BATCH, SEQ, N_HEADS_A, HEAD_DIM_A))
        vs.append(v_l.reshape(BATCH, SEQ, N_HEADS_A, HEAD_DIM_A))
        cs.append(c_l.reshape(BATCH, 2, N_HEADS_M, HEAD_DIM_M, HEAD_DIM_M))
        ns.append(n_l.reshape(BATCH, 2, N_HEADS_M, HEAD_DIM_M))
        ms.append(m_l[:, :, 0].reshape(BATCH, 2, N_HEADS_M))

    y_ctx, y_lat = x
    return (y_ctx.reshape(BATCH, SEQ, D_MODEL), y_lat.reshape(DEC_BATCH, DEC_SEQ, D_MODEL),
            jnp.stack(ks, axis=1), jnp.stack(vs, axis=1), jnp.stack(cs, axis=1),
            jnp.stack(ns, axis=1), jnp.stack(ms, axis=1))
```

```python
import functools

import numpy as np
import jax
import jax.numpy as jnp
from jax import lax
from jax.experimental import pallas as pl
from jax.experimental.pallas import tpu as pltpu

F32 = jnp.float32
BF16 = jnp.bfloat16
HIGHEST = lax.Precision.HIGHEST

D_MODEL = 1024
BATCH = 16
SEQ = 256
DEPTH = 2
DEC_BATCH = 8
DEC_SEQ = 1024
PAST_LEN = 512
GRID_W = 64
N_HEADS_A = 8
HEAD_DIM_A = 64
D_A = N_HEADS_A * HEAD_DIM_A
WIN_ROWS = 8
WIN_COLS = 16
D_CONV = 512
CONV_WIDTH = 31
N_HEADS_M = 4
HEAD_DIM_M = 128
D_M = N_HEADS_M * HEAD_DIM_M
N_GATE_M = 4 * N_HEADS_M
N_EXPERTS = 32
TOP_K = 4
D_EXPERT = 1024
SWIGLU_ALPHA = 1.702
SWIGLU_LIMIT = 7.0
RMS_EPS = 1e-6
LN_EPS = 1e-5
GATE_OFF = 3 * D_A + 2 * D_CONV + 4 * D_M

T_CTX = BATCH * SEQ
T_LAT = DEC_BATCH * DEC_SEQ
T_ALL = T_CTX + T_LAT
N_SEG = 1 + DEC_BATCH
SEG_PAD = 16
GRID_ROWS = DEC_SEQ // GRID_W
NEG = -1e30

N_ZB = 6 * 512
N_ZG = 3 * D_MODEL
N_ZF = 1664
QB_QA, QB_KA, QB_VA, QB_QM, QB_KM, QB_VM = 0, 1, 2, 3, 4, 5
ZG_GA, ZG_GC, ZG_GM = 0, 1, 2
ZF_CU, ZF_CG, ZF_OM = 0, 1, 2
ZF_GATES = 12

TM_TOK = 512
TM_PROJ = 512
TM_MOE = 512
N_TILES = T_ALL // TM_MOE
CHUNK_ROWS = 16
MXU_ROWS = 256
Q_TILE = -(-(TM_MOE * TOP_K + N_EXPERTS * (CHUNK_ROWS - 1)) // MXU_ROWS) * MXU_ROWS
CH_PER_TILE = Q_TILE // CHUNK_ROWS
E_GROUP = 256
CPG = E_GROUP // CHUNK_ROWS
MAX_CHUNKS = ((T_ALL * TOP_K + N_TILES * N_EXPERTS * (CHUNK_ROWS - 1)) // CHUNK_ROWS
              + N_EXPERTS * (CPG - 1)) + CPG
assert Q_TILE - (TM_MOE * TOP_K + N_EXPERTS * (CHUNK_ROWS - 1)) >= 2 * CHUNK_ROWS and N_TILES > CPG
READ_SPARE = N_TILES * Q_TILE - CHUNK_ROWS
VMEM_LIMIT = 60 * 1024 * 1024


def _cparams(sem=None):
    return pltpu.CompilerParams(dimension_semantics=sem, vmem_limit_bytes=VMEM_LIMIT)


def _seg_of_tile(i, tile):
    n_ctx = T_CTX // tile
    per_lat = DEC_SEQ // tile
    return jnp.where(i < n_ctx, 0, 1 + (i - n_ctx) // per_lat)


def _mod_spec(tile, layer):
    return pl.BlockSpec((None, None, 6, D_MODEL), lambda i: (layer, _seg_of_tile(i, tile), 0, 0))


def _layer_spec(shape, layer):
    return pl.BlockSpec((None,) + shape, lambda *_: (layer,) + (0,) * len(shape))


def _dot_nt(a, b):
    return lax.dot_general(a, b, (((1,), (1,)), ((), ())), preferred_element_type=F32)


def _sigmoid(x):
    return 1.0 / (1.0 + jnp.exp(-x))


def _mod_kernel(c_ref, w_ref, b_ref, o_ref):
    c = c_ref[...]
    s = c * _sigmoid(c)
    o_ref[...] = jnp.dot(s, w_ref[...], precision=HIGHEST, preferred_element_type=F32) + b_ref[...]


def _modulation(cond, w_mod, b_mod):
    tn = 1536
    return pl.pallas_call(
        _mod_kernel,
        grid=(DEPTH, 6 * D_MODEL // tn),
        in_specs=[pl.BlockSpec((SEG_PAD, D_MODEL), lambda l, j: (0, 0)),
                  pl.BlockSpec((None, D_MODEL, tn), lambda l, j: (l, 0, j)),
                  pl.BlockSpec((None, 1, tn), lambda l, j: (l, 0, j))],
        out_specs=pl.BlockSpec((None, SEG_PAD, tn), lambda l, j: (l, 0, j)),
        out_shape=jax.ShapeDtypeStruct((DEPTH, SEG_PAD, 6 * D_MODEL), F32),
        compiler_params=_cparams(("arbitrary", "arbitrary")),
        name="modulation",
    )(cond, w_mod, b_mod.reshape(DEPTH, 1, 6 * D_MODEL))


def _normmod(x, g, mod, shift_idx, scale_idx):
    y = x * lax.rsqrt(jnp.mean(x * x, axis=-1, keepdims=True) + RMS_EPS) * g
    return y * (1.0 + mod[scale_idx:scale_idx + 1, :]) + mod[shift_idx:shift_idx + 1, :]


PROJ_CTX_TILES = T_CTX // TM_PROJ


def _in_proj_kernel(x_ref, g_ref, mod_ref, wb_ref, bb_ref, wg_ref, bg_ref, wf_ref, bf_ref,
                    zb_ref, zg_ref, zf_ref, kc_ref, vc_ref):
    h = _normmod(x_ref[...], g_ref[...], mod_ref[...], 0, 1).astype(BF16)
    acc = jnp.dot(h, wb_ref[...], preferred_element_type=F32) + bb_ref[...]
    zb_ref[...] = acc.astype(BF16)

    @pl.when(pl.program_id(0) < PROJ_CTX_TILES)
    def _():
        kc_ref[...] = acc[:, D_A:2 * D_A]
        vc_ref[...] = acc[:, 2 * D_A:3 * D_A]

    gates = jnp.dot(h, wg_ref[...], preferred_element_type=F32) + bg_ref[...]
    zg_ref[...] = _sigmoid(gates).astype(BF16)
    zf_ref[...] = jnp.dot(h, wf_ref[...], preferred_element_type=F32) + bf_ref[...]


def _in_proj(x, norm_g, mod, w, b, layer):
    def full(a):
        return pl.BlockSpec((None,) + a.shape[1:], lambda i: (layer, 0, 0), pipeline_mode=pl.Buffered(1))

    def rows(n):
        return pl.BlockSpec((TM_PROJ, n), lambda i: (i, 0))

    cache_spec = pl.BlockSpec((TM_PROJ, D_A), lambda i: (jnp.minimum(i, PROJ_CTX_TILES - 1), 0))
    cache_shape = jax.ShapeDtypeStruct((T_CTX, D_A), F32)

    return pl.pallas_call(
        _in_proj_kernel,
        grid=(T_ALL // TM_PROJ,),
        in_specs=[rows(D_MODEL), _layer_spec((1, D_MODEL), layer), _mod_spec(TM_PROJ, layer),
                  full(w[0]), full(b[0]), full(w[1]), full(b[1]), full(w[2]), full(b[2])],
        out_specs=[rows(N_ZB), rows(N_ZG), rows(N_ZF), cache_spec, cache_spec],
        out_shape=[jax.ShapeDtypeStruct((T_ALL, N_ZB), BF16), jax.ShapeDtypeStruct((T_ALL, N_ZG), BF16),
                   jax.ShapeDtypeStruct((T_ALL, N_ZF), F32), cache_shape, cache_shape],
        compiler_params=_cparams(("arbitrary",)),
        name="in_proj",
    )(x, norm_g, mod, w[0], b[0], w[1], b[1], w[2], b[2])


HEAD_PAIR = 2 * HEAD_DIM_A
ATT_SCALE = HEAD_DIM_A ** -0.5


def _pair_queries(q2):
    lo = lax.broadcasted_iota(jnp.int32, (1, HEAD_PAIR), 1) < HEAD_DIM_A
    q2 = q2 * ATT_SCALE
    zero = jnp.zeros_like(q2)
    return lo, jnp.concatenate([jnp.where(lo, q2, zero), jnp.where(lo, zero, q2)], axis=0)


def _unpair(lo, o_stacked):
    rows = o_stacked.shape[0] // 2
    return jnp.where(lo, o_stacked[:rows], o_stacked[rows:])


def _ctx_attn_kernel(q_ref, k_ref, v_ref, o_ref):
    for hp in range(N_HEADS_A // 2):
        sl = slice(hp * HEAD_PAIR, (hp + 1) * HEAD_PAIR)
        lo, qs = _pair_queries(q_ref[:, sl])
        s = _dot_nt(qs, k_ref[:, sl])
        p = jnp.exp(s - jnp.max(s, axis=-1, keepdims=True))
        l = jnp.sum(p, axis=-1, keepdims=True)
        o = jnp.dot(p.astype(BF16), v_ref[:, sl], preferred_element_type=F32) / l
        o_ref[:, sl] = _unpair(lo, o).astype(o_ref.dtype)


def _ctx_attention(zb):
    def spec(cb):
        return pl.BlockSpec((SEQ, D_A), lambda b: (b, cb))

    return pl.pallas_call(
        _ctx_attn_kernel,
        grid=(BATCH,),
        in_specs=[spec(QB_QA), spec(QB_KA), spec(QB_VA)],
        out_specs=pl.BlockSpec((SEQ, D_A), lambda b: (b, 0)),
        out_shape=jax.ShapeDtypeStruct((T_CTX, D_A), BF16),
        compiler_params=_cparams(("arbitrary",)),
        name="ctx_attention",
    )(zb, zb, zb)


NQ_ROWS = 4
NW_ROWS = 12
NQ_BLOCKS = GRID_ROWS // NQ_ROWS
NQ_TOK = NQ_ROWS * GRID_W
NW_TOK = NW_ROWS * GRID_W


def _window_row(qb, xp):
    return xp.clip(qb * NQ_ROWS - WIN_ROWS // 2, 0, GRID_ROWS - NW_ROWS)


N_REL_ROWS = 2 * WIN_ROWS - 1


def _natt_rel_rows():
    r = np.arange(NQ_BLOCKS)[:, None, None] * NQ_ROWS + np.arange(NQ_ROWS)[None, :, None]
    krow = _window_row(np.arange(NQ_BLOCKS), np)[:, None, None] + np.arange(NW_ROWS)[None, None, :]
    rs = np.clip(r - WIN_ROWS // 2, 0, GRID_ROWS - WIN_ROWS)
    assert ((rs >= krow[:, :, :1]) & (rs + WIN_ROWS <= krow[:, :, -1:] + 1)).all()
    return np.where((krow >= rs) & (krow < rs + WIN_ROWS), krow - r + WIN_ROWS - 1, N_REL_ROWS)


NATT_REL = _natt_rel_rows()


def _natt_kernel(q_ref, k_ref, v_ref, kc_ref, vc_ref, toe_ref, o_ref, bias_scr):
    qb = pl.program_id(0)

    @pl.when(pl.program_id(1) == 0)
    def _():
        for v in range(NQ_BLOCKS):
            @pl.when(qb == v)
            def _(v=v):
                for h in range(N_HEADS_A):
                    for rq in range(NQ_ROWS):
                        for kr in range(NW_ROWS):
                            half = (kr % 2) * GRID_W
                            bias_scr[h, rq * GRID_W:(rq + 1) * GRID_W, kr * GRID_W:(kr + 1) * GRID_W] = (
                                toe_ref[h, int(NATT_REL[v, rq, kr]), :, half:half + GRID_W])

    start = pl.multiple_of(_window_row(qb, jnp) * GRID_W, NQ_TOK)
    band = NW_TOK
    for hp in range(N_HEADS_A // 2):
        sl = slice(hp * HEAD_PAIR, (hp + 1) * HEAD_PAIR)
        lo, qs = _pair_queries(q_ref[:, sl])
        bias = bias_scr[2 * hp:2 * hp + 2].reshape(2 * NQ_TOK, band)
        s_loc = _dot_nt(qs, k_ref[pl.ds(start, band), sl]) + bias
        s_ctx = _dot_nt(qs, kc_ref[:, sl])
        m = jnp.maximum(jnp.max(s_loc, axis=-1, keepdims=True), jnp.max(s_ctx, axis=-1, keepdims=True))
        p_loc = jnp.exp(s_loc - m)
        p_ctx = jnp.exp(s_ctx - m)
        l = jnp.sum(p_loc, axis=-1, keepdims=True) + jnp.sum(p_ctx, axis=-1, keepdims=True)
        o = (jnp.dot(p_loc.astype(BF16), v_ref[pl.ds(start, band), sl], preferred_element_type=F32)
             + jnp.dot(p_ctx.astype(BF16), vc_ref[:, sl], preferred_element_type=F32))
        o_ref[:, sl] = _unpair(lo, o / l).astype(o_ref.dtype)


def _natt_bias(rpb):
    qc = np.arange(GRID_W)
    kc = np.arange(GRID_W)
    cs = np.clip(qc - WIN_COLS // 2, 0, GRID_W - WIN_COLS)
    ok = (kc[None, :] >= cs[:, None]) & (kc[None, :] < cs[:, None] + WIN_COLS)
    dc = np.clip(kc[None, :] - qc[:, None] + WIN_COLS - 1, 0, 2 * WIN_COLS - 2)
    pick = (dc[None] == np.arange(2 * WIN_COLS - 1)[:, None, None]).astype(np.float32)
    toe = jnp.einsum('lhdc,cqk->lhdqk', rpb, jnp.asarray(pick), precision=HIGHEST)
    toe = jnp.where(jnp.asarray(ok), toe, NEG)
    toe = jnp.concatenate([toe, jnp.full((DEPTH, N_HEADS_A, 1, GRID_W, GRID_W), NEG, F32)], axis=2)
    return jnp.concatenate([toe, toe], axis=-1)


def _natt(zb, cache_k, cache_v, toe, layer):
    lat0 = T_CTX // DEC_SEQ
    row0 = T_CTX // NQ_TOK
    return pl.pallas_call(
        _natt_kernel,
        grid=(NQ_BLOCKS, DEC_BATCH),
        in_specs=[pl.BlockSpec((NQ_TOK, D_A), lambda qb, b: (row0 + b * NQ_BLOCKS + qb, QB_QA)),
                  pl.BlockSpec((DEC_SEQ, D_A), lambda qb, b: (lat0 + b, QB_KA)),
                  pl.BlockSpec((DEC_SEQ, D_A), lambda qb, b: (lat0 + b, QB_VA)),
                  pl.BlockSpec((None, None, PAST_LEN, D_A), lambda qb, b: (b, layer, 0, 0)),
                  pl.BlockSpec((None, None, PAST_LEN, D_A), lambda qb, b: (b, layer, 0, 0)),
                  pl.BlockSpec((None, N_HEADS_A, N_REL_ROWS + 1, GRID_W, 2 * GRID_W),
                               lambda qb, b: (layer, 0, 0, 0, 0))],
        out_specs=pl.BlockSpec((NQ_TOK, D_A), lambda qb, b: (b * NQ_BLOCKS + qb, 0)),
        out_shape=jax.ShapeDtypeStruct((T_LAT, D_A), BF16),
        scratch_shapes=[pltpu.VMEM((N_HEADS_A, NQ_TOK, NW_TOK), F32)],
        compiler_params=_cparams(("arbitrary", "arbitrary")),
        name="nbr_attention",
    )(zb, zb, zb, cache_k, cache_v, toe)


CONV_HALO = 16
CONV_ROWS = 64


SUBLANES = 8


def _conv_kernel(u_ref, g_ref, w_ref, b_ref, lg_ref, lb_ref, o_ref, pad_scr, sh_scr, *, seq):
    zeros = jnp.zeros((CONV_HALO, D_CONV), F32)
    pad_scr[0:CONV_HALO, :] = zeros
    pad_scr[CONV_HALO + seq:2 * CONV_HALO + seq, :] = zeros
    pad_scr[CONV_HALO:CONV_HALO + seq, :] = u_ref[...] * _sigmoid(g_ref[...])
    n_sh = seq + 2 * CONV_HALO - SUBLANES
    for s in range(SUBLANES):
        sh_scr[s] = pad_scr[s:s + n_sh, :]
    first = CONV_HALO - CONV_WIDTH // 2
    for c in range(seq // CONV_ROWS):
        base = c * CONV_ROWS
        acc = jnp.broadcast_to(b_ref[...], (CONV_ROWS, D_CONV))
        for j in range(CONV_WIDTH):
            q, s = divmod(first + j, SUBLANES)
            row0 = base + q * SUBLANES
            acc = acc + sh_scr[s, row0:row0 + CONV_ROWS, :] * w_ref[j:j + 1, :]
        mu = jnp.mean(acc, axis=-1, keepdims=True)
        xc = acc - mu
        var = jnp.mean(xc * xc, axis=-1, keepdims=True)
        y = xc * lax.rsqrt(var + LN_EPS) * lg_ref[...] + lb_ref[...]
        o_ref[base:base + CONV_ROWS, :] = (y * _sigmoid(y)).astype(o_ref.dtype)


def _conv(z, w_dw, b_dw, ln_g, ln_b, seq, n_seq, row_block0, layer):
    def vec():
        return _layer_spec((1, D_CONV), layer)

    return pl.pallas_call(
        functools.partial(_conv_kernel, seq=seq),
        grid=(n_seq,),
        in_specs=[pl.BlockSpec((seq, D_CONV), lambda b: (row_block0 + b, ZF_CU)),
                  pl.BlockSpec((seq, D_CONV), lambda b: (row_block0 + b, ZF_CG)),
                  _layer_spec((CONV_WIDTH, D_CONV), layer),
                  vec(), vec(), vec()],
        out_specs=pl.BlockSpec((seq, D_CONV), lambda b: (b, 0)),
        out_shape=jax.ShapeDtypeStruct((n_seq * seq, D_CONV), BF16),
        scratch_shapes=[pltpu.VMEM((seq + 2 * CONV_HALO, D_CONV), F32),
                        pltpu.VMEM((SUBLANES, seq + 2 * CONV_HALO - SUBLANES, D_CONV), F32)],
        compiler_params=_cparams(("arbitrary",)),
        name="conformer_conv",
    )(z, z, w_dw, b_dw, ln_g, ln_b)


N_STREAM = 2 * N_HEADS_M
MCHUNK = 256


PAIR_M = 2 * HEAD_DIM_M
N_PAIR = N_STREAM // 2


def _per_head_rows(r):
    return jnp.concatenate([jnp.broadcast_to(r[0:1], (HEAD_DIM_M, r.shape[1])),
                            jnp.broadcast_to(r[1:2], (HEAD_DIM_M, r.shape[1]))], axis=0)


def _mlstm_kernel(q_ref, k_ref, v_ref, om_ref, gt_ref, c0_ref, n0_ref, m0_ref, ng_ref,
                  y_ref, c_out, n_out, m_out, hf_scr, hb_scr, c_scr, n_scr, m_scr, cbd_scr, vbd_scr,
                  rows_scr, acol_scr, *, seq):
    nc = seq // MCHUNK
    c_scr[...] = c0_ref[...]
    n_scr[...] = n0_ref[...]
    m_scr[...] = m0_ref[...]
    cbd_scr[...] = jnp.zeros_like(cbd_scr)
    vbd_scr[...] = jnp.zeros_like(vbd_scr)
    for s_id in range(N_STREAM):
        blk = slice((s_id % 2) * HEAD_DIM_M, (s_id % 2 + 1) * HEAD_DIM_M)
        cbd_scr[s_id // 2, blk, blk] = c0_ref[s_id].astype(BF16)
    first_head = lax.broadcasted_iota(jnp.int32, (1, PAIR_M), 1) < HEAD_DIM_M
    rows = lax.broadcasted_iota(jnp.int32, (MCHUNK, MCHUNK), 0)
    cols = lax.broadcasted_iota(jnp.int32, (MCHUNK, MCHUNK), 1)
    tris = ((cols <= rows).astype(F32), (cols >= rows).astype(F32))
    reach = (rows <= cols, rows >= cols)
    lane = lax.broadcasted_iota(jnp.int32, (1, MCHUNK), 1)
    kscale = HEAD_DIM_M ** -0.5

    for cc in range(nc):
        g = gt_ref[cc * MCHUNK:(cc + 1) * MCHUNK, :]
        lf = jnp.minimum(g, 0.0) - jnp.log(1.0 + jnp.exp(-jnp.abs(g)))
        g_t = g.T
        for d in range(2):
            cum = jnp.dot(tris[d], lf, precision=HIGHEST, preferred_element_type=F32)
            cum_t = cum.T
            i0 = 2 * N_HEADS_M * d
            i_rows = g_t[i0:i0 + N_HEADS_M, :]
            b_rows = cum_t[i0 + N_HEADS_M:i0 + 2 * N_HEADS_M, :]
            reach_max = i_rows - b_rows
            for step in (1 << s for s in range(MCHUNK.bit_length() - 1)):
                if d == 0:
                    shifted = jnp.where(lane >= step, pltpu.roll(reach_max, step, axis=1), NEG)
                else:
                    shifted = jnp.where(lane < MCHUNK - step, pltpu.roll(reach_max, MCHUNK - step, axis=1), NEG)
                reach_max = jnp.maximum(reach_max, shifted)
            rows_scr[2 * cc + d, 0] = i_rows
            rows_scr[2 * cc + d, 1] = b_rows
            rows_scr[2 * cc + d, 2] = reach_max
            acol_scr[2 * cc + d] = g - pltpu.roll(cum, cum.shape[1] - N_HEADS_M, axis=1)

    def chunk_step(c, carry):
        for d in range(2):
            cidx = c if d == 0 else nc - 1 - c
            off = pl.multiple_of(cidx * MCHUNK, MCHUNK)
            last = MCHUNK - 1 if d == 0 else 0
            i0 = 2 * N_HEADS_M * d
            heads = slice(N_HEADS_M * d, N_HEADS_M * (d + 1))
            i_rows = rows_scr[2 * cidx + d, 0]
            b_rows = rows_scr[2 * cidx + d, 1]
            reach_max = rows_scr[2 * cidx + d, 2]
            a_cols = acol_scr[2 * cidx + d]
            m_prevs = m_scr[heads, 0:1]
            inters = b_rows + m_prevs
            m_ts = jnp.maximum(inters, b_rows + reach_max)
            w_inters = jnp.exp(inters - m_ts)
            floors = jnp.exp(-m_ts)
            b_lasts = b_rows[:, last:last + 1]
            m_news = m_ts[:, last:last + 1]
            w_prevs = jnp.exp(b_lasts + m_prevs - m_news)
            w_srcs = kscale * jnp.exp(b_lasts - b_rows + i_rows - m_news)
            m_scr[heads, :] = jnp.broadcast_to(m_news, (N_HEADS_M, HEAD_DIM_M))
            for hp in range(N_HEADS_M // 2):
                pid = (N_HEADS_M // 2) * d + hp
                hh = slice(2 * hp, 2 * hp + 2)
                ps = slice(hp * PAIR_M, (hp + 1) * PAIR_M)
                q2 = q_ref[pl.ds(off, MCHUNK), ps]
                k2 = k_ref[pl.ds(off, MCHUNK), ps]
                v2_t = v_ref[pl.ds(off, MCHUNK), ps].astype(F32).T
                k_zero = jnp.zeros_like(k2)
                k_stack = jnp.concatenate([jnp.where(first_head, k2, k_zero),
                                           jnp.where(first_head, k_zero, k2)], axis=0)
                decay = jnp.concatenate(
                    [jnp.exp(jnp.where(reach[d], b_rows[h:h + 1] + a_cols[:, i0 + h:i0 + h + 1], NEG)
                             - m_ts[h:h + 1]) for h in (2 * hp, 2 * hp + 1)], axis=0)
                s_t = _dot_nt(k_stack, q2) * (kscale * decay)
                col_sums = jnp.concatenate([jnp.sum(s_t[:MCHUNK], axis=0, keepdims=True),
                                            jnp.sum(s_t[MCHUNK:], axis=0, keepdims=True)], axis=0)
                n2 = n_scr[N_HEADS_M * d + 2 * hp:N_HEADS_M * d + 2 * hp + 2, :]
                n_zero = jnp.zeros((1, HEAD_DIM_M), F32)
                n_mat = jnp.concatenate([jnp.concatenate([n2[0:1], n_zero], axis=1),
                                         jnp.concatenate([n_zero, n2[1:2]], axis=1),
                                         jnp.zeros((SUBLANES - 2, PAIR_M), F32)], axis=0)
                n_q = _dot_nt(n_mat.astype(BF16), q2)[0:2, :]
                den = w_inters[hh] * n_q + col_sums
                inv = 1.0 / jnp.maximum(jnp.abs(den), floors[hh])
                vbd_scr[pid, :HEAD_DIM_M, :MCHUNK] = v2_t[:HEAD_DIM_M].astype(BF16)
                vbd_scr[pid, HEAD_DIM_M:, MCHUNK:] = v2_t[HEAD_DIM_M:].astype(BF16)
                num_t = (_per_head_rows(w_inters[hh]) * _dot_nt(cbd_scr[pid], q2)
                         + jnp.dot(vbd_scr[pid], s_t.astype(BF16), preferred_element_type=F32))
                h_t = num_t * _per_head_rows(inv)
                if d == 0:
                    hf_scr[cidx, ps, :] = h_t
                else:
                    hb_scr[cidx, ps, :] = h_t
                upd = jnp.dot((v2_t * _per_head_rows(w_srcs[hh])).astype(BF16), k2,
                              preferred_element_type=F32)
                w_mat = jnp.concatenate([w_srcs[hh], jnp.zeros((SUBLANES - 2, MCHUNK), F32)], axis=0)
                n_upd = jnp.dot(w_mat.astype(BF16), k2, preferred_element_type=F32)
                for j in range(2):
                    h = 2 * hp + j
                    s_id = N_HEADS_M * d + h
                    blk = slice(j * HEAD_DIM_M, (j + 1) * HEAD_DIM_M)
                    c_new = w_prevs[h:h + 1] * c_scr[s_id] + upd[blk, blk]
                    c_scr[s_id] = c_new
                    cbd_scr[pid, blk, blk] = c_new.astype(BF16)
                    n_scr[s_id:s_id + 1, :] = w_prevs[h:h + 1] * n2[j:j + 1] + n_upd[j:j + 1, blk]
        return carry

    lax.fori_loop(0, nc, chunk_step, 0)

    for c in range(nc):
        ts = slice(c * MCHUNK, (c + 1) * MCHUNK)
        for h in range(N_HEADS_M):
            hs = slice(h * HEAD_DIM_M, (h + 1) * HEAD_DIM_M)
            hsum = hf_scr[c, hs, :] + hb_scr[c, hs, :]
            mu = jnp.mean(hsum, axis=0, keepdims=True)
            xc = hsum - mu
            var = jnp.mean(xc * xc, axis=0, keepdims=True)
            hn = (xc * lax.rsqrt(var + LN_EPS)).T
            y_ref[ts, hs] = (_sigmoid(om_ref[ts, hs]) * (hn * ng_ref[:, hs])).astype(y_ref.dtype)
    c_out[...] = c_scr[...]
    n_out[...] = n_scr[...]
    m_out[...] = m_scr[...]


def _mlstm(zb, zf, c0, n0, m0, norm_g, seq, n_seq, row_block0, state_map, layer):
    lead = len(state_map(0))

    def zspec(cb):
        return pl.BlockSpec((seq, D_M), lambda b: (row_block0 + b, cb))

    def sspec(tail):
        return pl.BlockSpec((None,) * lead + tail, lambda b: state_map(b) + (0,) * len(tail))

    return pl.pallas_call(
        functools.partial(_mlstm_kernel, seq=seq),
        grid=(n_seq,),
        in_specs=[zspec(QB_QM), zspec(QB_KM), zspec(QB_VM), zspec(ZF_OM),
                  pl.BlockSpec((seq, 128), lambda b: (row_block0 + b, ZF_GATES)),
                  sspec((N_STREAM, HEAD_DIM_M, HEAD_DIM_M)),
                  sspec((N_STREAM, HEAD_DIM_M)),
                  sspec((N_STREAM, HEAD_DIM_M)),
                  _layer_spec((1, D_M), layer)],
        out_specs=[pl.BlockSpec((seq, D_M), lambda b: (b, 0)),
                   pl.BlockSpec((None, N_STREAM, HEAD_DIM_M, HEAD_DIM_M), lambda b: (b, 0, 0, 0)),
                   pl.BlockSpec((None, N_STREAM, HEAD_DIM_M), lambda b: (b, 0, 0)),
                   pl.BlockSpec((None, N_STREAM, HEAD_DIM_M), lambda b: (b, 0, 0))],
        out_shape=[jax.ShapeDtypeStruct((n_seq * seq, D_M), BF16),
                   jax.ShapeDtypeStruct((n_seq, N_STREAM, HEAD_DIM_M, HEAD_DIM_M), F32),
                   jax.ShapeDtypeStruct((n_seq, N_STREAM, HEAD_DIM_M), F32),
                   jax.ShapeDtypeStruct((n_seq, N_STREAM, HEAD_DIM_M), F32)],
        scratch_shapes=[pltpu.VMEM((seq // MCHUNK, D_M, MCHUNK), F32),
                        pltpu.VMEM((seq // MCHUNK, D_M, MCHUNK), F32),
                        pltpu.VMEM((N_STREAM, HEAD_DIM_M, HEAD_DIM_M), F32),
                        pltpu.VMEM((N_STREAM, HEAD_DIM_M), F32),
                        pltpu.VMEM((N_STREAM, HEAD_DIM_M), F32),
                        pltpu.VMEM((N_PAIR, PAIR_M, PAIR_M), BF16),
                        pltpu.VMEM((N_PAIR, PAIR_M, 2 * MCHUNK), BF16),
                        pltpu.VMEM((2 * (seq // MCHUNK), 3, N_HEADS_M, MCHUNK), F32),
                        pltpu.VMEM((2 * (seq // MCHUNK), MCHUNK, 128), F32)],
        compiler_params=_cparams(("arbitrary",)),
        name="mlstm",
    )(zb, zb, zb, zf, zf, c0, n0, m0, norm_g)


N_CTX_TILES = T_CTX // TM_TOK


def _merge_kernel(x_ref, mod_ref, ya_c, ya_l, yc_c, yc_l, ym_c, ym_l, ga_ref, gc_ref, gm_ref,
                  wa_ref, wc_ref, wm_ref, wo_ref, o_ref):
    is_ctx = pl.program_id(0) < N_CTX_TILES

    def branch(y_ctx, y_lat, g_ref, w_ref):
        y = jnp.where(is_ctx, y_ctx[...], y_lat[...])
        return g_ref[...].astype(F32) * jnp.dot(y, w_ref[...], preferred_element_type=F32)

    merged = (branch(ya_c, ya_l, ga_ref, wa_ref) + branch(yc_c, yc_l, gc_ref, wc_ref)
              + branch(ym_c, ym_l, gm_ref, wm_ref))
    mix = jnp.dot(merged.astype(BF16), wo_ref[...], preferred_element_type=F32)
    o_ref[...] = x_ref[...] + mod_ref[2:3, :] * mix


def _merge(x, mod, ya, yc, ym, zg, w_pa, w_pc, w_pm, w_out, layer):
    def rows(width, cb=0):
        return pl.BlockSpec((TM_TOK, width), lambda i: (i, cb))

    def ctx_rows(width):
        return pl.BlockSpec((TM_TOK, width), lambda i: (jnp.minimum(i, N_CTX_TILES - 1), 0))

    def lat_rows(width):
        return pl.BlockSpec((TM_TOK, width), lambda i: (jnp.maximum(i - N_CTX_TILES, 0), 0))

    def full(shape):
        return pl.BlockSpec((None,) + shape, lambda i: (layer, 0, 0))

    return pl.pallas_call(
        _merge_kernel,
        grid=(T_ALL // TM_TOK,),
        in_specs=[rows(D_MODEL), _mod_spec(TM_TOK, layer),
                  ctx_rows(D_A), lat_rows(D_A), ctx_rows(D_CONV), lat_rows(D_CONV),
                  ctx_rows(D_M), lat_rows(D_M),
                  rows(D_MODEL, ZG_GA), rows(D_MODEL, ZG_GC), rows(D_MODEL, ZG_GM),
                  full((D_A, D_MODEL)), full((D_CONV, D_MODEL)), full((D_M, D_MODEL)),
                  full((D_MODEL, D_MODEL))],
        out_specs=rows(D_MODEL),
        out_shape=jax.ShapeDtypeStruct((T_ALL, D_MODEL), F32),
        compiler_params=_cparams(("arbitrary",)),
        name="merge",
    )(x, mod, ya[0], ya[1], yc[0], yc[1], ym[0], ym[1], zg, zg, zg, w_pa, w_pc, w_pm, w_out)


def _route_sort_kernel(x_ref, g_ref, mod_ref, wr_ref, br_ref, xt_ref, pos_ref, gate_ref, nch_ref, seg_ref):
    h = _normmod(x_ref[...], g_ref[...], mod_ref[...], 3, 4)
    hb = h.astype(BF16)
    logits = _dot_nt(wr_ref[...].astype(BF16), hb) + br_ref[...]
    e_iota = lax.broadcasted_iota(jnp.int32, (N_EXPERTS, TM_MOE), 0).astype(F32)
    sels, vals = [], []
    l = logits
    for k in range(TOP_K):
        m = jnp.max(l, axis=0, keepdims=True)
        idx = jnp.min(jnp.where(l == m, e_iota, float(N_EXPERTS)), axis=0, keepdims=True)
        sel = e_iota == idx
        vals.append(m)
        sels.append(sel)
        l = jnp.where(sel, -jnp.inf, l)
    exps = [jnp.exp(v - vals[0]) for v in vals]
    tot = exps[0] + exps[1] + exps[2] + exps[3]
    onehot = jnp.zeros((N_EXPERTS, TM_MOE), F32)
    for k in range(TOP_K):
        gate_ref[k:k + 1, :] = exps[k] / tot
        onehot = onehot + sels[k].astype(F32)
    gate_ref[TOP_K:8, :] = jnp.zeros((8 - TOP_K, TM_MOE), F32)

    cnt = jnp.sum(onehot, axis=1, keepdims=True)
    nch = jnp.floor((cnt + (CHUNK_ROWS - 1)) / CHUNK_ROWS)
    ei = lax.broadcasted_iota(jnp.int32, (N_EXPERTS, N_EXPERTS), 0)
    ej = lax.broadcasted_iota(jnp.int32, (N_EXPERTS, N_EXPERTS), 1)
    seg = jnp.dot((ej < ei).astype(F32), jnp.broadcast_to(nch, (N_EXPERTS, 128)), precision=HIGHEST,
                  preferred_element_type=F32)
    nch_ref[...] = jnp.broadcast_to(nch, (N_EXPERTS, 128)).astype(jnp.int32)
    seg_ref[...] = seg.astype(jnp.int32)

    t_src = lax.broadcasted_iota(jnp.int32, (TM_MOE, TM_MOE), 0)
    t_dst = lax.broadcasted_iota(jnp.int32, (TM_MOE, TM_MOE), 1)
    before = (t_src < t_dst).astype(BF16)
    row_of = (seg[:, 0:1] * CHUNK_ROWS
              + jnp.dot(onehot.astype(BF16), before, preferred_element_type=F32))
    q_iota = lax.broadcasted_iota(jnp.int32, (Q_TILE, TM_MOE), 0)
    perm = jnp.zeros((Q_TILE, TM_MOE), F32)
    for k in range(TOP_K):
        q_k = jnp.sum(jnp.where(sels[k], row_of, 0.0), axis=0, keepdims=True).astype(jnp.int32)
        pos_ref[k:k + 1, :] = q_k
        perm = jnp.where(q_iota == q_k, 1.0, perm)
    pos_ref[TOP_K:8, :] = jnp.zeros((8 - TOP_K, TM_MOE), jnp.int32)
    xt_ref[...] = jnp.dot(perm.astype(BF16), hb, preferred_element_type=F32).astype(BF16)


def _route_sort(x, norm_g, mod, w_rt, b_r, layer):
    tspec = pl.BlockSpec((8, TM_MOE), lambda i: (0, i))
    mspec = pl.BlockSpec((None, N_EXPERTS, 128), lambda i: (i, 0, 0))
    meta = jax.ShapeDtypeStruct((N_TILES, N_EXPERTS, 128), jnp.int32)
    return pl.pallas_call(
        _route_sort_kernel,
        grid=(N_TILES,),
        in_specs=[pl.BlockSpec((TM_MOE, D_MODEL), lambda i: (i, 0)),
                  _layer_spec((1, D_MODEL), layer), _mod_spec(TM_MOE, layer),
                  _layer_spec((N_EXPERTS, D_MODEL), layer), _layer_spec((N_EXPERTS, 1), layer)],
        out_specs=[pl.BlockSpec((Q_TILE, D_MODEL), lambda i: (i, 0)), tspec, tspec, mspec, mspec],
        out_shape=[jax.ShapeDtypeStruct((N_TILES * Q_TILE, D_MODEL), BF16),
                   jax.ShapeDtypeStruct((8, T_ALL), jnp.int32), jax.ShapeDtypeStruct((8, T_ALL), F32),
                   meta, meta],
        compiler_params=_cparams(("arbitrary",)),
        name="moe_route_sort",
    )(x, norm_g, mod, w_rt, b_r)


def _expert_kernel(nch_ref, seg_ref, wgu_ref, bgu_ref, wd_ref, bd_ref, xt_ref, yt_ref,
                   wgu_scr, wd_scr, xbuf, ybuf, row_scr, gstart_scr, gsem, ssem):
    del xt_ref
    e = pl.program_id(0)

    def src_row(row):
        return pl.multiple_of(jnp.where(row >= 0, row, READ_SPARE), CHUNK_ROWS)

    def dst_row(row, slot, c):
        spare = c * Q_TILE + jnp.where(slot == 0, Q_TILE - CHUNK_ROWS, Q_TILE - 2 * CHUNK_ROWS)
        return pl.multiple_of(jnp.where(row >= 0, row, spare), CHUNK_ROWS)

    def chunk_rows(c):
        return slice(c * CHUNK_ROWS, (c + 1) * CHUNK_ROWS)

    def start_in(g):
        slot = g % 2
        for c in range(CPG):
            row = row_scr[g * CPG + c]
            pltpu.make_async_copy(yt_ref.at[pl.ds(src_row(row), CHUNK_ROWS), :], xbuf.at[slot, chunk_rows(c), :],
                                  gsem.at[slot]).start()

    def start_out(g, slot):
        for c in range(CPG):
            row = row_scr[g * CPG + c]
            pltpu.make_async_copy(ybuf.at[slot, chunk_rows(c), :],
                                  yt_ref.at[pl.ds(dst_row(row, slot, c), CHUNK_ROWS), :], ssem.at[slot]).start()

    def wait_in(slot):
        pltpu.make_async_copy(yt_ref.at[pl.ds(0, E_GROUP), :], xbuf.at[slot], gsem.at[slot]).wait()

    def wait_out(slot):
        pltpu.make_async_copy(ybuf.at[slot], yt_ref.at[pl.ds(0, E_GROUP), :], ssem.at[slot]).wait()

    @pl.when(e == 0)
    def _():
        def per_expert(ee, cnt):
            gstart_scr[ee] = cnt // CPG

            def per_tile(t, cnt):
                first = (t * CH_PER_TILE + seg_ref[t * N_EXPERTS + ee]) * CHUNK_ROWS

                def per_chunk(j, cnt):
                    row_scr[cnt] = first + j * CHUNK_ROWS
                    return cnt + 1

                return lax.fori_loop(0, nch_ref[t * N_EXPERTS + ee], per_chunk, cnt)

            cnt = lax.fori_loop(0, N_TILES, per_tile, cnt)
            padded = (cnt + CPG - 1) // CPG * CPG

            def pad(i, carry):
                row_scr[i] = -1
                return carry

            lax.fori_loop(cnt, padded, pad, 0)
            return padded

        total = lax.fori_loop(0, N_EXPERTS, per_expert, 0)
        gstart_scr[N_EXPERTS] = total // CPG

        def pad(i, carry):
            row_scr[i] = -1
            return carry

        lax.fori_loop(total, total + CPG, pad, 0)
        ybuf[...] = jnp.zeros_like(ybuf)
        start_out(total // CPG, 0)
        start_out(total // CPG, 1)
        start_in(0)

    g_first = gstart_scr[e]
    g_end = gstart_scr[e + 1]

    @pl.when(g_end > g_first)
    def _():
        wgu_scr[...] = wgu_ref[...].astype(BF16)
        wd_scr[...] = wd_ref[...].astype(BF16)

    def group_step(g, carry):
        slot = g % 2
        start_in(g + 1)
        wait_in(slot)
        wait_out(slot)
        hgu = jnp.dot(xbuf[slot], wgu_scr[...], preferred_element_type=F32) + bgu_ref[...]
        h_glu = jnp.minimum(hgu[:, :D_EXPERT], SWIGLU_LIMIT)
        h_lin = jnp.clip(hgu[:, D_EXPERT:], -SWIGLU_LIMIT, SWIGLU_LIMIT)
        act = (h_lin + 1.0) * (h_glu * _sigmoid(SWIGLU_ALPHA * h_glu))
        y = jnp.dot(act.astype(BF16), wd_scr[...], preferred_element_type=F32) + bd_ref[...]
        ybuf[slot] = y.astype(BF16)
        start_out(g, slot)
        return carry

    lax.fori_loop(g_first, g_end, group_step, 0)

    @pl.when(e == N_EXPERTS - 1)
    def _():
        wait_in(gstart_scr[N_EXPERTS] % 2)
        wait_out(0)
        wait_out(1)


def _experts(nch_flat, seg_flat, xt, w_gu, b_gu, w_down, b_down, layer):
    return pl.pallas_call(
        _expert_kernel,
        grid_spec=pltpu.PrefetchScalarGridSpec(
            num_scalar_prefetch=2,
            grid=(N_EXPERTS,),
            in_specs=[pl.BlockSpec((None, None, D_MODEL, 2 * D_EXPERT), lambda e, n, s: (layer, e, 0, 0)),
                      pl.BlockSpec((None, None, 1, 2 * D_EXPERT), lambda e, n, s: (layer, e, 0, 0)),
                      pl.BlockSpec((None, None, D_EXPERT, D_MODEL), lambda e, n, s: (layer, e, 0, 0)),
                      pl.BlockSpec((None, None, 1, D_MODEL), lambda e, n, s: (layer, e, 0, 0)),
                      pl.BlockSpec(memory_space=pl.ANY)],
            out_specs=pl.BlockSpec(memory_space=pl.ANY),
            scratch_shapes=[pltpu.VMEM((D_MODEL, 2 * D_EXPERT), BF16),
                            pltpu.VMEM((D_EXPERT, D_MODEL), BF16),
                            pltpu.VMEM((2, E_GROUP, D_MODEL), BF16),
                            pltpu.VMEM((2, E_GROUP, D_MODEL), BF16),
                            pltpu.SMEM((MAX_CHUNKS,), jnp.int32),
                            pltpu.SMEM((N_EXPERTS + 1,), jnp.int32),
                            pltpu.SemaphoreType.DMA((2,)),
                            pltpu.SemaphoreType.DMA((2,))]),
        out_shape=jax.ShapeDtypeStruct((N_TILES * Q_TILE, D_MODEL), BF16),
        input_output_aliases={6: 0},
        compiler_params=_cparams(("arbitrary",)),
        name="moe_experts",
    )(nch_flat, seg_flat, w_gu, b_gu, w_down, b_down, xt)


MOE_CTX_TILES = T_CTX // TM_MOE


def _combine_rows(x_ref, mod_ref, pos_ref, gate_ref, yt_ref):
    lane = lax.broadcasted_iota(jnp.int32, (TM_MOE, Q_TILE), 1)
    sel = jnp.zeros((TM_MOE, Q_TILE), F32)
    for k in range(TOP_K):
        sel = jnp.where(lane == pos_ref[:, k:k + 1], gate_ref[:, k:k + 1], sel)
    acc = jnp.dot(sel.astype(BF16), yt_ref[...], preferred_element_type=F32)
    return x_ref[...] + mod_ref[5:6, :] * acc


def _combine_kernel(x_ref, mod_ref, pos_ref, gate_ref, yt_ref, o_ref):
    o_ref[...] = _combine_rows(x_ref, mod_ref, pos_ref, gate_ref, yt_ref)


def _combine_final_kernel(x_ref, mod_ref, pos_ref, gate_ref, yt_ref, fg_ref, ctx_ref, lat_ref):
    y = _combine_rows(x_ref, mod_ref, pos_ref, gate_ref, yt_ref)
    y = y * lax.rsqrt(jnp.mean(y * y, axis=-1, keepdims=True) + RMS_EPS) * fg_ref[...]
    is_ctx = pl.program_id(0) < MOE_CTX_TILES

    @pl.when(is_ctx)
    def _():
        ctx_ref[...] = y

    @pl.when(jnp.logical_not(is_ctx))
    def _():
        lat_ref[...] = y


def _combine(x, mod, pos_t, gate_t, yt, layer, final_g=None):
    tile = pl.BlockSpec((TM_MOE, D_MODEL), lambda i: (i, 0))
    in_specs = [tile, _mod_spec(TM_MOE, layer),
                pl.BlockSpec((TM_MOE, 8), lambda i: (i, 0)),
                pl.BlockSpec((TM_MOE, 8), lambda i: (i, 0)),
                pl.BlockSpec((Q_TILE, D_MODEL), lambda i: (i, 0))]
    if final_g is None:
        return pl.pallas_call(
            _combine_kernel, grid=(N_TILES,), in_specs=in_specs, out_specs=tile,
            out_shape=jax.ShapeDtypeStruct((T_ALL, D_MODEL), F32),
            compiler_params=_cparams(("arbitrary",)), name="moe_combine",
        )(x, mod, pos_t, gate_t, yt)
    return pl.pallas_call(
        _combine_final_kernel, grid=(N_TILES,),
        in_specs=in_specs + [pl.BlockSpec((1, D_MODEL), lambda i: (0, 0))],
        out_specs=[pl.BlockSpec((TM_MOE, D_MODEL), lambda i: (jnp.minimum(i, MOE_CTX_TILES - 1), 0)),
                   pl.BlockSpec((TM_MOE, D_MODEL), lambda i: (jnp.maximum(i - MOE_CTX_TILES, 0), 0))],
        out_shape=[jax.ShapeDtypeStruct((T_CTX, D_MODEL), F32), jax.ShapeDtypeStruct((T_LAT, D_MODEL), F32)],
        compiler_params=_cparams(("arbitrary",)), name="moe_combine_final",
    )(x, mod, pos_t, gate_t, yt, final_g)


def _moe(x, norm_g, mod, w_rt, b_r, w_gu, b_gu, w_down, b_down, layer, final_g=None):
    xt, pos, gate, nch, seg = _route_sort(x, norm_g, mod, w_rt, b_r, layer)
    yt = _experts(nch[:, :, 0].reshape(-1), seg[:, :, 0].reshape(-1), xt, w_gu,
                  b_gu.reshape(DEPTH, N_EXPERTS, 1, 2 * D_EXPERT), w_down,
                  b_down.reshape(DEPTH, N_EXPERTS, 1, D_MODEL), layer)
    return _combine(x, mod, pos.T, gate.T, yt, layer, final_g)


PREP_ROWS = 128


def _prep_in_kernel(w_ref, zb_ref, zg_ref, zf_ref):
    conv0, mq0, om0 = 3 * D_A, 3 * D_A + 2 * D_CONV, 3 * D_A + 2 * D_CONV + 3 * D_M
    gates_end = GATE_OFF + N_GATE_M
    zb_ref[:, :conv0] = w_ref[:, :conv0].astype(BF16)
    zb_ref[:, conv0:] = w_ref[:, mq0:om0].astype(BF16)
    zg_ref[...] = w_ref[:, gates_end:].astype(BF16)
    zf_ref[:, :mq0 - conv0] = w_ref[:, conv0:mq0].astype(BF16)
    zf_ref[:, mq0 - conv0:mq0 - conv0 + D_M] = w_ref[:, om0:GATE_OFF].astype(BF16)
    tail = w_ref[:, GATE_OFF:GATE_OFF + 128]
    keep = lax.broadcasted_iota(jnp.int32, tail.shape, 1) < N_GATE_M
    zf_ref[:, mq0 - conv0 + D_M:] = jnp.where(keep, tail, 0.0).astype(BF16)


def _prep_in_weights(w_in):
    def out(n):
        return (pl.BlockSpec((None, PREP_ROWS, n), lambda l, i: (l, i, 0)),
                jax.ShapeDtypeStruct((DEPTH, D_MODEL, n), BF16))

    specs, shapes = zip(out(N_ZB), out(N_ZG), out(N_ZF))
    return pl.pallas_call(
        _prep_in_kernel,
        grid=(DEPTH, D_MODEL // PREP_ROWS),
        in_specs=[pl.BlockSpec((None, PREP_ROWS, w_in.shape[-1]), lambda l, i: (l, i, 0))],
        out_specs=list(specs),
        out_shape=list(shapes),
        compiler_params=_cparams(("arbitrary", "arbitrary")),
        name="prep_in_weights",
    )(w_in)


def _split_in_cols(w):
    conv0, mq0, om0 = 3 * D_A, 3 * D_A + 2 * D_CONV, 3 * D_A + 2 * D_CONV + 3 * D_M
    gates_end = GATE_OFF + N_GATE_M
    pad = jnp.zeros(w.shape[:-1] + (N_ZF - (2 * D_CONV + D_M + N_GATE_M),), w.dtype)
    zb = jnp.concatenate([w[..., :conv0], w[..., mq0:om0]], axis=-1)
    zg = w[..., gates_end:]
    zf = jnp.concatenate([w[..., conv0:mq0], w[..., om0:GATE_OFF], w[..., GATE_OFF:gates_end], pad], axis=-1)
    return zb, zg, zf


def kernel(x_prompt, x_sample, cache_k, cache_v, state_C, state_n, state_m, c, c_ctx, norm1_g, w_mod, b_mod, w_in, b_in, rpb, w_dw, b_dw, cln_g, cln_b, mnorm_g, w_pa, w_pc, w_pm, w_out, norm2_g, w_router, b_router, w_gu, b_gu, w_down, b_down, final_g):
    cond = jnp.concatenate([c_ctx[None, :], c, jnp.zeros((SEG_PAD - N_SEG, D_MODEL), F32)], axis=0)
    mod_all = _modulation(cond, w_mod, b_mod).reshape(DEPTH, SEG_PAD, 6, D_MODEL)

    x = jnp.concatenate([x_prompt.reshape(T_CTX, D_MODEL), x_sample.reshape(T_LAT, D_MODEL)], axis=0)
    ck = cache_k.reshape(DEC_BATCH, DEPTH, PAST_LEN, D_A).astype(BF16)
    cv = cache_v.reshape(DEC_BATCH, DEPTH, PAST_LEN, D_A).astype(BF16)
    lat_c0 = state_C.reshape(DEC_BATCH, DEPTH, N_STREAM, HEAD_DIM_M, HEAD_DIM_M)
    lat_n0 = state_n.reshape(DEC_BATCH, DEPTH, N_STREAM, HEAD_DIM_M)
    lat_m0 = jnp.broadcast_to(state_m.reshape(DEC_BATCH, DEPTH, N_STREAM, 1),
                              (DEC_BATCH, DEPTH, N_STREAM, HEAD_DIM_M))
    ctx_c0 = jnp.zeros((1, N_STREAM, HEAD_DIM_M, HEAD_DIM_M), F32)
    ctx_n0 = jnp.zeros((1, N_STREAM, HEAD_DIM_M), F32)
    ctx_m0 = jnp.full((1, N_STREAM, HEAD_DIM_M), -jnp.inf, F32)

    w_in_groups = _prep_in_weights(w_in)
    b_in_groups = _split_in_cols(b_in[:, None, :])
    merge_w = tuple(w.astype(BF16) for w in (w_pa, w_pc, w_pm, w_out))
    toe = _natt_bias(rpb)
    mod = mod_all
    g1, g2, ng = norm1_g[:, None, :], norm2_g[:, None, :], mnorm_g[:, None, :]
    conv_w = (w_dw, b_dw[:, None, :], cln_g[:, None, :], cln_b[:, None, :])
    w_rt, b_r = jnp.swapaxes(w_router, 1, 2), b_router[:, :, None]
    ks, vs, cs, ns, ms = [], [], [], [], []
    for l in range(DEPTH):
        zb, zg, zf, k_l, v_l = _in_proj(x, g1, mod, w_in_groups, b_in_groups, l)
        ya = (_ctx_attention(zb), _natt(zb, ck, cv, toe, l))
        yc = (_conv(zf, *conv_w, SEQ, BATCH, 0, l),
              _conv(zf, *conv_w, DEC_SEQ, DEC_BATCH, T_CTX // DEC_SEQ, l))
        ym_ctx, c_l, n_l, m_l = _mlstm(zb, zf, ctx_c0, ctx_n0, ctx_m0, ng, SEQ, BATCH, 0, lambda b: (0,), l)
        ym_lat, _, _, _ = _mlstm(zb, zf, lat_c0, lat_n0, lat_m0, ng, DEC_SEQ, DEC_BATCH, T_CTX // DEC_SEQ,
                                 lambda b: (b, l), l)
        x = _merge(x, mod, ya, yc, (ym_ctx, ym_lat), zg, *merge_w, l)
        x = _moe(x, g2, mod, w_rt, b_r, w_gu, b_gu, w_down, b_down, l,
                 final_g[None, :] if l == DEPTH - 1 else None)
        ks.append(k_l.reshape(BATCH, SEQ, N_HEADS_A, HEAD_DIM_A))
        vs.append(v_l.reshape(BATCH, SEQ, N_HEADS_A, HEAD_DIM_A))
        cs.append(c_l.reshape(BATCH, 2, N_HEADS_M, HEAD_DIM_M, HEAD_DIM_M))
        ns.append(n_l.reshape(BATCH, 2, N_HEADS_M, HEAD_DIM_M))
        ms.append(m_l[:, :, 0].reshape(BATCH, 2, N_HEADS_M))

    y_ctx, y_lat = x
    return (y_ctx.reshape(BATCH, SEQ, D_MODEL), y_lat.reshape(DEC_BATCH, DEC_SEQ, D_MODEL),
            jnp.stack(ks, axis=1), jnp.stack(vs, axis=1), jnp.stack(cs, axis=1),
            jnp.stack(ns, axis=1), jnp.stack(ms, axis=1))
```

```python
import functools

import numpy as np
import jax
import jax.numpy as jnp
from jax import lax
from jax.experimental import pallas as pl
from jax.experimental.pallas import tpu as pltpu

F32 = jnp.float32
BF16 = jnp.bfloat16
HIGHEST = lax.Precision.HIGHEST

D_MODEL = 1024
BATCH = 16
SEQ = 256
DEPTH = 2
DEC_BATCH = 8
DEC_SEQ = 1024
PAST_LEN = 512
GRID_W = 64
N_HEADS_A = 8
HEAD_DIM_A = 64
D_A = N_HEADS_A * HEAD_DIM_A
WIN_ROWS = 8
WIN_COLS = 16
D_CONV = 512
CONV_WIDTH = 31
N_HEADS_M = 4
HEAD_DIM_M = 128
D_M = N_HEADS_M * HEAD_DIM_M
N_GATE_M = 4 * N_HEADS_M
N_EXPERTS = 32
TOP_K = 4
D_EXPERT = 1024
SWIGLU_ALPHA = 1.702
SWIGLU_LIMIT = 7.0
RMS_EPS = 1e-6
LN_EPS = 1e-5
GATE_OFF = 3 * D_A + 2 * D_CONV + 4 * D_M

T_CTX = BATCH * SEQ
T_LAT = DEC_BATCH * DEC_SEQ
T_ALL = T_CTX + T_LAT
N_SEG = 1 + DEC_BATCH
SEG_PAD = 16
GRID_ROWS = DEC_SEQ // GRID_W
NEG = -1e30

N_ZB = 6 * 512
N_ZG = 3 * D_MODEL
N_ZF = 1664
QB_QA, QB_KA, QB_VA, QB_QM, QB_KM, QB_VM = 0, 1, 2, 3, 4, 5
ZG_GA, ZG_GC, ZG_GM = 0, 1, 2
ZF_CU, ZF_CG, ZF_OM = 0, 1, 2
ZF_GATES = 12

TM_TOK = 512
TM_PROJ = 512
TM_MOE = 512
N_TILES = T_ALL // TM_MOE
CHUNK_ROWS = 16
MXU_ROWS = 256
Q_TILE = -(-(TM_MOE * TOP_K + N_EXPERTS * (CHUNK_ROWS - 1)) // MXU_ROWS) * MXU_ROWS
CH_PER_TILE = Q_TILE // CHUNK_ROWS
E_GROUP = 256
CPG = E_GROUP // CHUNK_ROWS
MAX_CHUNKS = ((T_ALL * TOP_K + N_TILES * N_EXPERTS * (CHUNK_ROWS - 1)) // CHUNK_ROWS
              + N_EXPERTS * (CPG - 1)) + CPG
assert Q_TILE - (TM_MOE * TOP_K + N_EXPERTS * (CHUNK_ROWS - 1)) >= 2 * CHUNK_ROWS and N_TILES > CPG
READ_SPARE = N_TILES * Q_TILE - CHUNK_ROWS
VMEM_LIMIT = 60 * 1024 * 1024


def _cparams(sem=None):
    return pltpu.CompilerParams(dimension_semantics=sem, vmem_limit_bytes=VMEM_LIMIT)


def _seg_of_tile(i, tile):
    n_ctx = T_CTX // tile
    per_lat = DEC_SEQ // tile
    return jnp.where(i < n_ctx, 0, 1 + (i - n_ctx) // per_lat)


def _mod_spec(tile, layer):
    return pl.BlockSpec((None, None, 6, D_MODEL), lambda i: (layer, _seg_of_tile(i, tile), 0, 0))


def _layer_spec(shape, layer):
    return pl.BlockSpec((None,) + shape, lambda *_: (layer,) + (0,) * len(shape))


def _x_pair(x, tile):
    n_ctx = T_CTX // tile
    arrays, lat0 = (x, 0) if isinstance(x, tuple) else ((x, x), n_ctx)
    specs = (pl.BlockSpec((tile, D_MODEL), lambda i: (jnp.minimum(i, n_ctx - 1), 0)),
             pl.BlockSpec((tile, D_MODEL), lambda i: (lat0 + jnp.maximum(i - n_ctx, 0), 0)))
    return arrays, specs


def _dot_nt(a, b):
    return lax.dot_general(a, b, (((1,), (1,)), ((), ())), preferred_element_type=F32)


def _sigmoid(x):
    return 1.0 / (1.0 + jnp.exp(-x))


def _mod_kernel(c_ref, w_ref, b_ref, o_ref):
    c = c_ref[...]
    s = c * _sigmoid(c)
    o_ref[...] = jnp.dot(s, w_ref[...], precision=HIGHEST, preferred_element_type=F32) + b_ref[...]


def _modulation(cond, w_mod, b_mod):
    tn = 1536
    return pl.pallas_call(
        _mod_kernel,
        grid=(DEPTH, 6 * D_MODEL // tn),
        in_specs=[pl.BlockSpec((SEG_PAD, D_MODEL), lambda l, j: (0, 0)),
                  pl.BlockSpec((None, D_MODEL, tn), lambda l, j: (l, 0, j)),
                  pl.BlockSpec((None, 1, tn), lambda l, j: (l, 0, j))],
        out_specs=pl.BlockSpec((None, SEG_PAD, tn), lambda l, j: (l, 0, j)),
        out_shape=jax.ShapeDtypeStruct((DEPTH, SEG_PAD, 6 * D_MODEL), F32),
        compiler_params=_cparams(("arbitrary", "arbitrary")),
        name="modulation",
    )(cond, w_mod, b_mod.reshape(DEPTH, 1, 6 * D_MODEL))


def _normmod(x, g, mod, shift_idx, scale_idx):
    y = x * lax.rsqrt(jnp.mean(x * x, axis=-1, keepdims=True) + RMS_EPS) * g
    return y * (1.0 + mod[scale_idx:scale_idx + 1, :]) + mod[shift_idx:shift_idx + 1, :]


PROJ_CTX_TILES = T_CTX // TM_PROJ


def _in_proj_kernel(xc_ref, xl_ref, g_ref, mod_ref, wb_ref, bb_ref, wg_ref, bg_ref, wf_ref, bf_ref,
                    zb_ref, zg_ref, zf_ref, kc_ref, vc_ref):
    x = jnp.where(pl.program_id(0) < PROJ_CTX_TILES, xc_ref[...], xl_ref[...])
    h = _normmod(x, g_ref[...], mod_ref[...], 0, 1).astype(BF16)
    acc = jnp.dot(h, wb_ref[...], preferred_element_type=F32) + bb_ref[...]
    zb_ref[...] = acc.astype(BF16)

    @pl.when(pl.program_id(0) < PROJ_CTX_TILES)
    def _():
        kc_ref[...] = acc[:, D_A:2 * D_A]
        vc_ref[...] = acc[:, 2 * D_A:3 * D_A]

    gates = jnp.dot(h, wg_ref[...], preferred_element_type=F32) + bg_ref[...]
    zg_ref[...] = _sigmoid(gates).astype(BF16)
    zf_ref[...] = jnp.dot(h, wf_ref[...], preferred_element_type=F32) + bf_ref[...]


def _in_proj(x, norm_g, mod, w, b, layer):
    def full(a):
        return pl.BlockSpec((None,) + a.shape[1:], lambda i: (layer, 0, 0), pipeline_mode=pl.Buffered(1))

    def rows(n):
        return pl.BlockSpec((TM_PROJ, n), lambda i: (i, 0))

    x_arrays, x_specs = _x_pair(x, TM_PROJ)
    cache_spec = pl.BlockSpec((TM_PROJ, D_A), lambda i: (jnp.minimum(i, PROJ_CTX_TILES - 1), 0))
    cache_shape = jax.ShapeDtypeStruct((T_CTX, D_A), F32)

    return pl.pallas_call(
        _in_proj_kernel,
        grid=(T_ALL // TM_PROJ,),
        in_specs=[*x_specs, _layer_spec((1, D_MODEL), layer), _mod_spec(TM_PROJ, layer),
                  full(w[0]), full(b[0]), full(w[1]), full(b[1]), full(w[2]), full(b[2])],
        out_specs=[rows(N_ZB), rows(N_ZG), rows(N_ZF), cache_spec, cache_spec],
        out_shape=[jax.ShapeDtypeStruct((T_ALL, N_ZB), BF16), jax.ShapeDtypeStruct((T_ALL, N_ZG), BF16),
                   jax.ShapeDtypeStruct((T_ALL, N_ZF), F32), cache_shape, cache_shape],
        compiler_params=_cparams(("arbitrary",)),
        name="in_proj",
    )(*x_arrays, norm_g, mod, w[0], b[0], w[1], b[1], w[2], b[2])


HEAD_PAIR = 2 * HEAD_DIM_A
ATT_SCALE = HEAD_DIM_A ** -0.5


def _pair_queries(q2):
    lo = lax.broadcasted_iota(jnp.int32, (1, HEAD_PAIR), 1) < HEAD_DIM_A
    q2 = q2 * ATT_SCALE
    zero = jnp.zeros_like(q2)
    return lo, jnp.concatenate([jnp.where(lo, q2, zero), jnp.where(lo, zero, q2)], axis=0)


def _unpair(lo, o_stacked):
    rows = o_stacked.shape[0] // 2
    return jnp.where(lo, o_stacked[:rows], o_stacked[rows:])


def _ctx_attn_kernel(q_ref, k_ref, v_ref, o_ref):
    for hp in range(N_HEADS_A // 2):
        sl = slice(hp * HEAD_PAIR, (hp + 1) * HEAD_PAIR)
        lo, qs = _pair_queries(q_ref[:, sl])
        s = _dot_nt(qs, k_ref[:, sl])
        p = jnp.exp(s - jnp.max(s, axis=-1, keepdims=True))
        l = jnp.sum(p, axis=-1, keepdims=True)
        o = jnp.dot(p.astype(BF16), v_ref[:, sl], preferred_element_type=F32) / l
        o_ref[:, sl] = _unpair(lo, o).astype(o_ref.dtype)


def _ctx_attention(zb):
    def spec(cb):
        return pl.BlockSpec((SEQ, D_A), lambda b: (b, cb))

    return pl.pallas_call(
        _ctx_attn_kernel,
        grid=(BATCH,),
        in_specs=[spec(QB_QA), spec(QB_KA), spec(QB_VA)],
        out_specs=pl.BlockSpec((SEQ, D_A), lambda b: (b, 0)),
        out_shape=jax.ShapeDtypeStruct((T_CTX, D_A), BF16),
        compiler_params=_cparams(("arbitrary",)),
        name="ctx_attention",
    )(zb, zb, zb)


NQ_ROWS = 4
NW_ROWS = 12
NQ_BLOCKS = GRID_ROWS // NQ_ROWS
NQ_TOK = NQ_ROWS * GRID_W
NW_TOK = NW_ROWS * GRID_W


def _window_row(qb, xp):
    return xp.clip(qb * NQ_ROWS - WIN_ROWS // 2, 0, GRID_ROWS - NW_ROWS)


N_REL_ROWS = 2 * WIN_ROWS - 1


def _natt_rel_rows():
    r = np.arange(NQ_BLOCKS)[:, None, None] * NQ_ROWS + np.arange(NQ_ROWS)[None, :, None]
    krow = _window_row(np.arange(NQ_BLOCKS), np)[:, None, None] + np.arange(NW_ROWS)[None, None, :]
    rs = np.clip(r - WIN_ROWS // 2, 0, GRID_ROWS - WIN_ROWS)
    assert ((rs >= krow[:, :, :1]) & (rs + WIN_ROWS <= krow[:, :, -1:] + 1)).all()
    return np.where((krow >= rs) & (krow < rs + WIN_ROWS), krow - r + WIN_ROWS - 1, N_REL_ROWS)


NATT_REL = _natt_rel_rows()


def _natt_kernel(q_ref, k_ref, v_ref, kc_ref, vc_ref, toe_ref, o_ref, bias_scr):
    qb = pl.program_id(0)

    @pl.when(pl.program_id(1) == 0)
    def _():
        for v in range(NQ_BLOCKS):
            @pl.when(qb == v)
            def _(v=v):
                for h in range(N_HEADS_A):
                    for rq in range(NQ_ROWS):
                        for kr in range(NW_ROWS):
                            half = (kr % 2) * GRID_W
                            bias_scr[h, rq * GRID_W:(rq + 1) * GRID_W, kr * GRID_W:(kr + 1) * GRID_W] = (
                                toe_ref[h, int(NATT_REL[v, rq, kr]), :, half:half + GRID_W])

    start = pl.multiple_of(_window_row(qb, jnp) * GRID_W, NQ_TOK)
    band = NW_TOK
    for hp in range(N_HEADS_A // 2):
        sl = slice(hp * HEAD_PAIR, (hp + 1) * HEAD_PAIR)
        lo, qs = _pair_queries(q_ref[:, sl])
        bias = bias_scr[2 * hp:2 * hp + 2].reshape(2 * NQ_TOK, band)
        s_loc = _dot_nt(qs, k_ref[pl.ds(start, band), sl]) + bias
        s_ctx = _dot_nt(qs, kc_ref[:, sl])
        m = jnp.maximum(jnp.max(s_loc, axis=-1, keepdims=True), jnp.max(s_ctx, axis=-1, keepdims=True))
        p_loc = jnp.exp(s_loc - m)
        p_ctx = jnp.exp(s_ctx - m)
        l = jnp.sum(p_loc, axis=-1, keepdims=True) + jnp.sum(p_ctx, axis=-1, keepdims=True)
        o = (jnp.dot(p_loc.astype(BF16), v_ref[pl.ds(start, band), sl], preferred_element_type=F32)
             + jnp.dot(p_ctx.astype(BF16), vc_ref[:, sl], preferred_element_type=F32))
        o_ref[:, sl] = _unpair(lo, o / l).astype(o_ref.dtype)


def _natt_bias(rpb):
    qc = np.arange(GRID_W)
    kc = np.arange(GRID_W)
    cs = np.clip(qc - WIN_COLS // 2, 0, GRID_W - WIN_COLS)
    ok = (kc[None, :] >= cs[:, None]) & (kc[None, :] < cs[:, None] + WIN_COLS)
    dc = np.clip(kc[None, :] - qc[:, None] + WIN_COLS - 1, 0, 2 * WIN_COLS - 2)
    pick = (dc[None] == np.arange(2 * WIN_COLS - 1)[:, None, None]).astype(np.float32)
    toe = jnp.einsum('lhdc,cqk->lhdqk', rpb, jnp.asarray(pick), precision=HIGHEST)
    toe = jnp.where(jnp.asarray(ok), toe, NEG)
    toe = jnp.concatenate([toe, jnp.full((DEPTH, N_HEADS_A, 1, GRID_W, GRID_W), NEG, F32)], axis=2)
    return jnp.concatenate([toe, toe], axis=-1)


def _natt(zb, cache_k, cache_v, toe, layer):
    lat0 = T_CTX // DEC_SEQ
    row0 = T_CTX // NQ_TOK
    return pl.pallas_call(
        _natt_kernel,
        grid=(NQ_BLOCKS, DEC_BATCH),
        in_specs=[pl.BlockSpec((NQ_TOK, D_A), lambda qb, b: (row0 + b * NQ_BLOCKS + qb, QB_QA)),
                  pl.BlockSpec((DEC_SEQ, D_A), lambda qb, b: (lat0 + b, QB_KA)),
                  pl.BlockSpec((DEC_SEQ, D_A), lambda qb, b: (lat0 + b, QB_VA)),
                  pl.BlockSpec((None, None, PAST_LEN, D_A), lambda qb, b: (b, layer, 0, 0)),
                  pl.BlockSpec((None, None, PAST_LEN, D_A), lambda qb, b: (b, layer, 0, 0)),
                  pl.BlockSpec((None, N_HEADS_A, N_REL_ROWS + 1, GRID_W, 2 * GRID_W),
                               lambda qb, b: (layer, 0, 0, 0, 0))],
        out_specs=pl.BlockSpec((NQ_TOK, D_A), lambda qb, b: (b * NQ_BLOCKS + qb, 0)),
        out_shape=jax.ShapeDtypeStruct((T_LAT, D_A), BF16),
        scratch_shapes=[pltpu.VMEM((N_HEADS_A, NQ_TOK, NW_TOK), F32)],
        compiler_params=_cparams(("arbitrary", "arbitrary")),
        name="nbr_attention",
    )(zb, zb, zb, cache_k, cache_v, toe)


CONV_HALO = 16
CONV_ROWS = 64


SUBLANES = 8


def _conv_kernel(u_ref, g_ref, w_ref, b_ref, lg_ref, lb_ref, o_ref, pad_scr, sh_scr, *, seq):
    zeros = jnp.zeros((CONV_HALO, D_CONV), F32)
    pad_scr[0:CONV_HALO, :] = zeros
    pad_scr[CONV_HALO + seq:2 * CONV_HALO + seq, :] = zeros
    pad_scr[CONV_HALO:CONV_HALO + seq, :] = u_ref[...] * _sigmoid(g_ref[...])
    n_sh = seq + 2 * CONV_HALO - SUBLANES
    for s in range(SUBLANES):
        sh_scr[s] = pad_scr[s:s + n_sh, :]
    first = CONV_HALO - CONV_WIDTH // 2
    for c in range(seq // CONV_ROWS):
        base = c * CONV_ROWS
        acc = jnp.broadcast_to(b_ref[...], (CONV_ROWS, D_CONV))
        for j in range(CONV_WIDTH):
            q, s = divmod(first + j, SUBLANES)
            row0 = base + q * SUBLANES
            acc = acc + sh_scr[s, row0:row0 + CONV_ROWS, :] * w_ref[j:j + 1, :]
        mu = jnp.mean(acc, axis=-1, keepdims=True)
        xc = acc - mu
        var = jnp.mean(xc * xc, axis=-1, keepdims=True)
        y = xc * lax.rsqrt(var + LN_EPS) * lg_ref[...] + lb_ref[...]
        o_ref[base:base + CONV_ROWS, :] = (y * _sigmoid(y)).astype(o_ref.dtype)


def _conv(z, w_dw, b_dw, ln_g, ln_b, seq, n_seq, row_block0, layer):
    def vec():
        return _layer_spec((1, D_CONV), layer)

    return pl.pallas_call(
        functools.partial(_conv_kernel, seq=seq),
        grid=(n_seq,),
        in_specs=[pl.BlockSpec((seq, D_CONV), lambda b: (row_block0 + b, ZF_CU)),
                  pl.BlockSpec((seq, D_CONV), lambda b: (row_block0 + b, ZF_CG)),
                  _layer_spec((CONV_WIDTH, D_CONV), layer),
                  vec(), vec(), vec()],
        out_specs=pl.BlockSpec((seq, D_CONV), lambda b: (b, 0)),
        out_shape=jax.ShapeDtypeStruct((n_seq * seq, D_CONV), BF16),
        scratch_shapes=[pltpu.VMEM((seq + 2 * CONV_HALO, D_CONV), F32),
                        pltpu.VMEM((SUBLANES, seq + 2 * CONV_HALO - SUBLANES, D_CONV), F32)],
        compiler_params=_cparams(("arbitrary",)),
        name="conformer_conv",
    )(z, z, w_dw, b_dw, ln_g, ln_b)


N_STREAM = 2 * N_HEADS_M
MCHUNK = 256


PAIR_M = 2 * HEAD_DIM_M
N_PAIR = N_STREAM // 2


def _per_head_rows(r):
    return jnp.concatenate([jnp.broadcast_to(r[0:1], (HEAD_DIM_M, r.shape[1])),
                            jnp.broadcast_to(r[1:2], (HEAD_DIM_M, r.shape[1]))], axis=0)


def _mlstm_kernel(q_ref, k_ref, v_ref, om_ref, gt_ref, c0_ref, n0_ref, m0_ref, ng_ref,
                  y_ref, c_out, n_out, m_out, hf_scr, hb_scr, c_scr, n_scr, m_scr, cbd_scr, vbd_scr,
                  rows_scr, acol_scr, *, seq):
    nc = seq // MCHUNK
    c_scr[...] = c0_ref[...]
    n_scr[...] = n0_ref[...]
    m_scr[...] = m0_ref[...]
    cbd_scr[...] = jnp.zeros_like(cbd_scr)
    vbd_scr[...] = jnp.zeros_like(vbd_scr)
    for s_id in range(N_STREAM):
        blk = slice((s_id % 2) * HEAD_DIM_M, (s_id % 2 + 1) * HEAD_DIM_M)
        cbd_scr[s_id // 2, blk, blk] = c0_ref[s_id].astype(BF16)
    first_head = lax.broadcasted_iota(jnp.int32, (1, PAIR_M), 1) < HEAD_DIM_M
    rows = lax.broadcasted_iota(jnp.int32, (MCHUNK, MCHUNK), 0)
    cols = lax.broadcasted_iota(jnp.int32, (MCHUNK, MCHUNK), 1)
    tris = ((cols <= rows).astype(F32), (cols >= rows).astype(F32))
    reach = (rows <= cols, rows >= cols)
    lane = lax.broadcasted_iota(jnp.int32, (1, MCHUNK), 1)
    kscale = HEAD_DIM_M ** -0.5

    for cc in range(nc):
        g = gt_ref[cc * MCHUNK:(cc + 1) * MCHUNK, :]
        lf = jnp.minimum(g, 0.0) - jnp.log(1.0 + jnp.exp(-jnp.abs(g)))
        g_t = g.T
        for d in range(2):
            cum = jnp.dot(tris[d], lf, precision=HIGHEST, preferred_element_type=F32)
            cum_t = cum.T
            i0 = 2 * N_HEADS_M * d
            i_rows = g_t[i0:i0 + N_HEADS_M, :]
            b_rows = cum_t[i0 + N_HEADS_M:i0 + 2 * N_HEADS_M, :]
            reach_max = i_rows - b_rows
            for step in (1 << s for s in range(MCHUNK.bit_length() - 1)):
                if d == 0:
                    shifted = jnp.where(lane >= step, pltpu.roll(reach_max, step, axis=1), NEG)
                else:
                    shifted = jnp.where(lane < MCHUNK - step, pltpu.roll(reach_max, MCHUNK - step, axis=1), NEG)
                reach_max = jnp.maximum(reach_max, shifted)
            rows_scr[2 * cc + d, 0] = i_rows
            rows_scr[2 * cc + d, 1] = b_rows
            rows_scr[2 * cc + d, 2] = reach_max
            acol_scr[2 * cc + d] = g - pltpu.roll(cum, cum.shape[1] - N_HEADS_M, axis=1)

    def chunk_step(c, carry):
        for d in range(2):
            cidx = c if d == 0 else nc - 1 - c
            off = pl.multiple_of(cidx * MCHUNK, MCHUNK)
            last = MCHUNK - 1 if d == 0 else 0
            i0 = 2 * N_HEADS_M * d
            heads = slice(N_HEADS_M * d, N_HEADS_M * (d + 1))
            i_rows = rows_scr[2 * cidx + d, 0]
            b_rows = rows_scr[2 * cidx + d, 1]
            reach_max = rows_scr[2 * cidx + d, 2]
            a_cols = acol_scr[2 * cidx + d]
            m_prevs = m_scr[heads, 0:1]
            inters = b_rows + m_prevs
            m_ts = jnp.maximum(inters, b_rows + reach_max)
            w_inters = jnp.exp(inters - m_ts)
            floors = jnp.exp(-m_ts)
            b_lasts = b_rows[:, last:last + 1]
            m_news = m_ts[:, last:last + 1]
            w_prevs = jnp.exp(b_lasts + m_prevs - m_news)
            w_srcs = kscale * jnp.exp(b_lasts - b_rows + i_rows - m_news)
            m_scr[heads, :] = jnp.broadcast_to(m_news, (N_HEADS_M, HEAD_DIM_M))
            for hp in range(N_HEADS_M // 2):
                pid = (N_HEADS_M // 2) * d + hp
                hh = slice(2 * hp, 2 * hp + 2)
                ps = slice(hp * PAIR_M, (hp + 1) * PAIR_M)
                q2 = q_ref[pl.ds(off, MCHUNK), ps]
                k2 = k_ref[pl.ds(off, MCHUNK), ps]
                v2_t = v_ref[pl.ds(off, MCHUNK), ps].astype(F32).T
                k_zero = jnp.zeros_like(k2)
                k_stack = jnp.concatenate([jnp.where(first_head, k2, k_zero),
                                           jnp.where(first_head, k_zero, k2)], axis=0)
                decay = jnp.concatenate(
                    [jnp.exp(jnp.where(reach[d], b_rows[h:h + 1] + a_cols[:, i0 + h:i0 + h + 1], NEG)
                             - m_ts[h:h + 1]) for h in (2 * hp, 2 * hp + 1)], axis=0)
                s_t = _dot_nt(k_stack, q2) * (kscale * decay)
                col_sums = jnp.concatenate([jnp.sum(s_t[:MCHUNK], axis=0, keepdims=True),
                                            jnp.sum(s_t[MCHUNK:], axis=0, keepdims=True)], axis=0)
                n2 = n_scr[N_HEADS_M * d + 2 * hp:N_HEADS_M * d + 2 * hp + 2, :]
                n_zero = jnp.zeros((1, HEAD_DIM_M), F32)
                n_mat = jnp.concatenate([jnp.concatenate([n2[0:1], n_zero], axis=1),
                                         jnp.concatenate([n_zero, n2[1:2]], axis=1),
                                         jnp.zeros((SUBLANES - 2, PAIR_M), F32)], axis=0)
                n_q = _dot_nt(n_mat.astype(BF16), q2)[0:2, :]
                den = w_inters[hh] * n_q + col_sums
                inv = 1.0 / jnp.maximum(jnp.abs(den), floors[hh])
                vbd_scr[pid, :HEAD_DIM_M, :MCHUNK] = v2_t[:HEAD_DIM_M].astype(BF16)
                vbd_scr[pid, HEAD_DIM_M:, MCHUNK:] = v2_t[HEAD_DIM_M:].astype(BF16)
                num_t = (_per_head_rows(w_inters[hh]) * _dot_nt(cbd_scr[pid], q2)
                         + jnp.dot(vbd_scr[pid], s_t.astype(BF16), preferred_element_type=F32))
                h_t = num_t * _per_head_rows(inv)
                if d == 0:
                    hf_scr[cidx, ps, :] = h_t
                else:
                    hb_scr[cidx, ps, :] = h_t
                upd = jnp.dot((v2_t * _per_head_rows(w_srcs[hh])).astype(BF16), k2,
                              preferred_element_type=F32)
                w_mat = jnp.concatenate([w_srcs[hh], jnp.zeros((SUBLANES - 2, MCHUNK), F32)], axis=0)
                n_upd = jnp.dot(w_mat.astype(BF16), k2, preferred_element_type=F32)
                for j in range(2):
                    h = 2 * hp + j
                    s_id = N_HEADS_M * d + h
                    blk = slice(j * HEAD_DIM_M, (j + 1) * HEAD_DIM_M)
                    c_new = w_prevs[h:h + 1] * c_scr[s_id] + upd[blk, blk]
                    c_scr[s_id] = c_new
                    cbd_scr[pid, blk, blk] = c_new.astype(BF16)
                    n_scr[s_id:s_id + 1, :] = w_prevs[h:h + 1] * n2[j:j + 1] + n_upd[j:j + 1, blk]
        return carry

    lax.fori_loop(0, nc, chunk_step, 0)

    for c in range(nc):
        ts = slice(c * MCHUNK, (c + 1) * MCHUNK)
        for h in range(N_HEADS_M):
            hs = slice(h * HEAD_DIM_M, (h + 1) * HEAD_DIM_M)
            hsum = hf_scr[c, hs, :] + hb_scr[c, hs, :]
            mu = jnp.mean(hsum, axis=0, keepdims=True)
            xc = hsum - mu
            var = jnp.mean(xc * xc, axis=0, keepdims=True)
            hn = (xc * lax.rsqrt(var + LN_EPS)).T
            y_ref[ts, hs] = (_sigmoid(om_ref[ts, hs]) * (hn * ng_ref[:, hs])).astype(y_ref.dtype)
    c_out[...] = c_scr[...]
    n_out[...] = n_scr[...]
    m_out[...] = m_scr[...]


def _mlstm(zb, zf, c0, n0, m0, norm_g, seq, n_seq, row_block0, state_map, layer):
    lead = len(state_map(0))

    def zspec(cb):
        return pl.BlockSpec((seq, D_M), lambda b: (row_block0 + b, cb))

    def sspec(tail):
        return pl.BlockSpec((None,) * lead + tail, lambda b: state_map(b) + (0,) * len(tail))

    return pl.pallas_call(
        functools.partial(_mlstm_kernel, seq=seq),
        grid=(n_seq,),
        in_specs=[zspec(QB_QM), zspec(QB_KM), zspec(QB_VM), zspec(ZF_OM),
                  pl.BlockSpec((seq, 128), lambda b: (row_block0 + b, ZF_GATES)),
                  sspec((N_STREAM, HEAD_DIM_M, HEAD_DIM_M)),
                  sspec((N_STREAM, HEAD_DIM_M)),
                  sspec((N_STREAM, HEAD_DIM_M)),
                  _layer_spec((1, D_M), layer)],
        out_specs=[pl.BlockSpec((seq, D_M), lambda b: (b, 0)),
                   pl.BlockSpec((None, N_STREAM, HEAD_DIM_M, HEAD_DIM_M), lambda b: (b, 0, 0, 0)),
                   pl.BlockSpec((None, N_STREAM, HEAD_DIM_M), lambda b: (b, 0, 0)),
                   pl.BlockSpec((None, N_STREAM, HEAD_DIM_M), lambda b: (b, 0, 0))],
        out_shape=[jax.ShapeDtypeStruct((n_seq * seq, D_M), BF16),
                   jax.ShapeDtypeStruct((n_seq, N_STREAM, HEAD_DIM_M, HEAD_DIM_M), F32),
                   jax.ShapeDtypeStruct((n_seq, N_STREAM, HEAD_DIM_M), F32),
                   jax.ShapeDtypeStruct((n_seq, N_STREAM, HEAD_DIM_M), F32)],
        scratch_shapes=[pltpu.VMEM((seq // MCHUNK, D_M, MCHUNK), F32),
                        pltpu.VMEM((seq // MCHUNK, D_M, MCHUNK), F32),
                        pltpu.VMEM((N_STREAM, HEAD_DIM_M, HEAD_DIM_M), F32),
                        pltpu.VMEM((N_STREAM, HEAD_DIM_M), F32),
                        pltpu.VMEM((N_STREAM, HEAD_DIM_M), F32),
                        pltpu.VMEM((N_PAIR, PAIR_M, PAIR_M), BF16),
                        pltpu.VMEM((N_PAIR, PAIR_M, 2 * MCHUNK), BF16),
                        pltpu.VMEM((2 * (seq // MCHUNK), 3, N_HEADS_M, MCHUNK), F32),
                        pltpu.VMEM((2 * (seq // MCHUNK), MCHUNK, 128), F32)],
        compiler_params=_cparams(("arbitrary",)),
        name="mlstm",
    )(zb, zb, zb, zf, zf, c0, n0, m0, norm_g)


N_CTX_TILES = T_CTX // TM_TOK


def _merge_kernel(xc_ref, xl_ref, mod_ref, ya_c, ya_l, yc_c, yc_l, ym_c, ym_l, ga_ref, gc_ref, gm_ref,
                  wa_ref, wc_ref, wm_ref, wo_ref, o_ref):
    is_ctx = pl.program_id(0) < N_CTX_TILES

    def branch(y_ctx, y_lat, g_ref, w_ref):
        y = jnp.where(is_ctx, y_ctx[...], y_lat[...])
        return g_ref[...].astype(F32) * jnp.dot(y, w_ref[...], preferred_element_type=F32)

    merged = (branch(ya_c, ya_l, ga_ref, wa_ref) + branch(yc_c, yc_l, gc_ref, wc_ref)
              + branch(ym_c, ym_l, gm_ref, wm_ref))
    mix = jnp.dot(merged.astype(BF16), wo_ref[...], preferred_element_type=F32)
    o_ref[...] = jnp.where(is_ctx, xc_ref[...], xl_ref[...]) + mod_ref[2:3, :] * mix


def _merge(x, mod, ya, yc, ym, zg, w_pa, w_pc, w_pm, w_out, layer):
    x_arrays, x_specs = _x_pair(x, TM_TOK)

    def rows(width, cb=0):
        return pl.BlockSpec((TM_TOK, width), lambda i: (i, cb))

    def ctx_rows(width):
        return pl.BlockSpec((TM_TOK, width), lambda i: (jnp.minimum(i, N_CTX_TILES - 1), 0))

    def lat_rows(width):
        return pl.BlockSpec((TM_TOK, width), lambda i: (jnp.maximum(i - N_CTX_TILES, 0), 0))

    def full(shape):
        return pl.BlockSpec((None,) + shape, lambda i: (layer, 0, 0))

    return pl.pallas_call(
        _merge_kernel,
        grid=(T_ALL // TM_TOK,),
        in_specs=[*x_specs, _mod_spec(TM_TOK, layer),
                  ctx_rows(D_A), lat_rows(D_A), ctx_rows(D_CONV), lat_rows(D_CONV),
                  ctx_rows(D_M), lat_rows(D_M),
                  rows(D_MODEL, ZG_GA), rows(D_MODEL, ZG_GC), rows(D_MODEL, ZG_GM),
                  full((D_A, D_MODEL)), full((D_CONV, D_MODEL)), full((D_M, D_MODEL)),
                  full((D_MODEL, D_MODEL))],
        out_specs=rows(D_MODEL),
        out_shape=jax.ShapeDtypeStruct((T_ALL, D_MODEL), F32),
        compiler_params=_cparams(("arbitrary",)),
        name="merge",
    )(*x_arrays, mod, ya[0], ya[1], yc[0], yc[1], ym[0], ym[1], zg, zg, zg, w_pa, w_pc, w_pm, w_out)


def _route_sort_kernel(x_ref, g_ref, mod_ref, wr_ref, br_ref, xt_ref, pos_ref, gate_ref, nch_ref, seg_ref):
    h = _normmod(x_ref[...], g_ref[...], mod_ref[...], 3, 4)
    hb = h.astype(BF16)
    logits = _dot_nt(wr_ref[...].astype(BF16), hb) + br_ref[...]
    e_iota = lax.broadcasted_iota(jnp.int32, (N_EXPERTS, TM_MOE), 0).astype(F32)
    sels, vals = [], []
    l = logits
    for k in range(TOP_K):
        m = jnp.max(l, axis=0, keepdims=True)
        idx = jnp.min(jnp.where(l == m, e_iota, float(N_EXPERTS)), axis=0, keepdims=True)
        sel = e_iota == idx
        vals.append(m)
        sels.append(sel)
        l = jnp.where(sel, -jnp.inf, l)
    exps = [jnp.exp(v - vals[0]) for v in vals]
    tot = exps[0] + exps[1] + exps[2] + exps[3]
    onehot = jnp.zeros((N_EXPERTS, TM_MOE), F32)
    for k in range(TOP_K):
        gate_ref[k:k + 1, :] = exps[k] / tot
        onehot = onehot + sels[k].astype(F32)
    gate_ref[TOP_K:8, :] = jnp.zeros((8 - TOP_K, TM_MOE), F32)

    cnt = jnp.sum(onehot, axis=1, keepdims=True)
    nch = jnp.floor((cnt + (CHUNK_ROWS - 1)) / CHUNK_ROWS)
    ei = lax.broadcasted_iota(jnp.int32, (N_EXPERTS, N_EXPERTS), 0)
    ej = lax.broadcasted_iota(jnp.int32, (N_EXPERTS, N_EXPERTS), 1)
    seg = jnp.dot((ej < ei).astype(F32), jnp.broadcast_to(nch, (N_EXPERTS, 128)), precision=HIGHEST,
                  preferred_element_type=F32)
    nch_ref[...] = jnp.broadcast_to(nch, (N_EXPERTS, 128)).astype(jnp.int32)
    seg_ref[...] = seg.astype(jnp.int32)

    t_src = lax.broadcasted_iota(jnp.int32, (TM_MOE, TM_MOE), 0)
    t_dst = lax.broadcasted_iota(jnp.int32, (TM_MOE, TM_MOE), 1)
    before = (t_src < t_dst).astype(BF16)
    row_of = (seg[:, 0:1] * CHUNK_ROWS
              + jnp.dot(onehot.astype(BF16), before, preferred_element_type=F32))
    q_iota = lax.broadcasted_iota(jnp.int32, (Q_TILE, TM_MOE), 0)
    perm = jnp.zeros((Q_TILE, TM_MOE), F32)
    for k in range(TOP_K):
        q_k = jnp.sum(jnp.where(sels[k], row_of, 0.0), axis=0, keepdims=True).astype(jnp.int32)
        pos_ref[k:k + 1, :] = q_k
        perm = jnp.where(q_iota == q_k, 1.0, perm)
    pos_ref[TOP_K:8, :] = jnp.zeros((8 - TOP_K, TM_MOE), jnp.int32)
    xt_ref[...] = jnp.dot(perm.astype(BF16), hb, preferred_element_type=F32).astype(BF16)


def _route_sort(x, norm_g, mod, w_rt, b_r, layer):
    tspec = pl.BlockSpec((8, TM_MOE), lambda i: (0, i))
    mspec = pl.BlockSpec((None, N_EXPERTS, 128), lambda i: (i, 0, 0))
    meta = jax.ShapeDtypeStruct((N_TILES, N_EXPERTS, 128), jnp.int32)
    return pl.pallas_call(
        _route_sort_kernel,
        grid=(N_TILES,),
        in_specs=[pl.BlockSpec((TM_MOE, D_MODEL), lambda i: (i, 0)),
                  _layer_spec((1, D_MODEL), layer), _mod_spec(TM_MOE, layer),
                  _layer_spec((N_EXPERTS, D_MODEL), layer), _layer_spec((N_EXPERTS, 1), layer)],
        out_specs=[pl.BlockSpec((Q_TILE, D_MODEL), lambda i: (i, 0)), tspec, tspec, mspec, mspec],
        out_shape=[jax.ShapeDtypeStruct((N_TILES * Q_TILE, D_MODEL), BF16),
                   jax.ShapeDtypeStruct((8, T_ALL), jnp.int32), jax.ShapeDtypeStruct((8, T_ALL), F32),
                   meta, meta],
        compiler_params=_cparams(("arbitrary",)),
        name="moe_route_sort",
    )(x, norm_g, mod, w_rt, b_r)


def _expert_kernel(nch_ref, seg_ref, wgu_ref, bgu_ref, wd_ref, bd_ref, xt_ref, yt_ref,
                   wgu_scr, wd_scr, xbuf, ybuf, row_scr, gstart_scr, gsem, ssem):
    del xt_ref
    e = pl.program_id(0)

    def src_row(row):
        return pl.multiple_of(jnp.where(row >= 0, row, READ_SPARE), CHUNK_ROWS)

    def dst_row(row, slot, c):
        spare = c * Q_TILE + jnp.where(slot == 0, Q_TILE - CHUNK_ROWS, Q_TILE - 2 * CHUNK_ROWS)
        return pl.multiple_of(jnp.where(row >= 0, row, spare), CHUNK_ROWS)

    def chunk_rows(c):
        return slice(c * CHUNK_ROWS, (c + 1) * CHUNK_ROWS)

    def start_in(g):
        slot = g % 2
        for c in range(CPG):
            row = row_scr[g * CPG + c]
            pltpu.make_async_copy(yt_ref.at[pl.ds(src_row(row), CHUNK_ROWS), :], xbuf.at[slot, chunk_rows(c), :],
                                  gsem.at[slot]).start()

    def start_out(g, slot):
        for c in range(CPG):
            row = row_scr[g * CPG + c]
            pltpu.make_async_copy(ybuf.at[slot, chunk_rows(c), :],
                                  yt_ref.at[pl.ds(dst_row(row, slot, c), CHUNK_ROWS), :], ssem.at[slot]).start()

    def wait_in(slot):
        pltpu.make_async_copy(yt_ref.at[pl.ds(0, E_GROUP), :], xbuf.at[slot], gsem.at[slot]).wait()

    def wait_out(slot):
        pltpu.make_async_copy(ybuf.at[slot], yt_ref.at[pl.ds(0, E_GROUP), :], ssem.at[slot]).wait()

    @pl.when(e == 0)
    def _():
        def per_expert(ee, cnt):
            gstart_scr[ee] = cnt // CPG

            def per_tile(t, cnt):
                first = (t * CH_PER_TILE + seg_ref[t * N_EXPERTS + ee]) * CHUNK_ROWS

                def per_chunk(j, cnt):
                    row_scr[cnt] = first + j * CHUNK_ROWS
                    return cnt + 1

                return lax.fori_loop(0, nch_ref[t * N_EXPERTS + ee], per_chunk, cnt)

            cnt = lax.fori_loop(0, N_TILES, per_tile, cnt)
            padded = (cnt + CPG - 1) // CPG * CPG

            def pad(i, carry):
                row_scr[i] = -1
                return carry

            lax.fori_loop(cnt, padded, pad, 0)
            return padded

        total = lax.fori_loop(0, N_EXPERTS, per_expert, 0)
        gstart_scr[N_EXPERTS] = total // CPG

        def pad(i, carry):
            row_scr[i] = -1
            return carry

        lax.fori_loop(total, total + CPG, pad, 0)
        ybuf[...] = jnp.zeros_like(ybuf)
        start_out(total // CPG, 0)
        start_out(total // CPG, 1)
        start_in(0)

    g_first = gstart_scr[e]
    g_end = gstart_scr[e + 1]

    @pl.when(g_end > g_first)
    def _():
        wgu_scr[...] = wgu_ref[...].astype(BF16)
        wd_scr[...] = wd_ref[...].astype(BF16)

    def group_step(g, carry):
        slot = g % 2
        start_in(g + 1)
        wait_in(slot)
        wait_out(slot)
        hgu = jnp.dot(xbuf[slot], wgu_scr[...], preferred_element_type=F32) + bgu_ref[...]
        h_glu = jnp.minimum(hgu[:, :D_EXPERT], SWIGLU_LIMIT)
        h_lin = jnp.clip(hgu[:, D_EXPERT:], -SWIGLU_LIMIT, SWIGLU_LIMIT)
        act = (h_lin + 1.0) * (h_glu * _sigmoid(SWIGLU_ALPHA * h_glu))
        y = jnp.dot(act.astype(BF16), wd_scr[...], preferred_element_type=F32) + bd_ref[...]
        ybuf[slot] = y.astype(BF16)
        start_out(g, slot)
        return carry

    lax.fori_loop(g_first, g_end, group_step, 0)

    @pl.when(e == N_EXPERTS - 1)
    def _():
        wait_in(gstart_scr[N_EXPERTS] % 2)
        wait_out(0)
        wait_out(1)


def _experts(nch_flat, seg_flat, xt, w_gu, b_gu, w_down, b_down, layer):
    return pl.pallas_call(
        _expert_kernel,
        grid_spec=pltpu.PrefetchScalarGridSpec(
            num_scalar_prefetch=2,
            grid=(N_EXPERTS,),
            in_specs=[pl.BlockSpec((None, None, D_MODEL, 2 * D_EXPERT), lambda e, n, s: (layer, e, 0, 0)),
                      pl.BlockSpec((None, None, 1, 2 * D_EXPERT), lambda e, n, s: (layer, e, 0, 0)),
                      pl.BlockSpec((None, None, D_EXPERT, D_MODEL), lambda e, n, s: (layer, e, 0, 0)),
                      pl.BlockSpec((None, None, 1, D_MODEL), lambda e, n, s: (layer, e, 0, 0)),
                      pl.BlockSpec(memory_space=pl.ANY)],
            out_specs=pl.BlockSpec(memory_space=pl.ANY),
            scratch_shapes=[pltpu.VMEM((D_MODEL, 2 * D_EXPERT), BF16),
                            pltpu.VMEM((D_EXPERT, D_MODEL), BF16),
                            pltpu.VMEM((2, E_GROUP, D_MODEL), BF16),
                            pltpu.VMEM((2, E_GROUP, D_MODEL), BF16),
                            pltpu.SMEM((MAX_CHUNKS,), jnp.int32),
                            pltpu.SMEM((N_EXPERTS + 1,), jnp.int32),
                            pltpu.SemaphoreType.DMA((2,)),
                            pltpu.SemaphoreType.DMA((2,))]),
        out_shape=jax.ShapeDtypeStruct((N_TILES * Q_TILE, D_MODEL), BF16),
        input_output_aliases={6: 0},
        compiler_params=_cparams(("arbitrary",)),
        name="moe_experts",
    )(nch_flat, seg_flat, w_gu, b_gu, w_down, b_down, xt)


MOE_CTX_TILES = T_CTX // TM_MOE


def _combine_rows(x_ref, mod_ref, pos_ref, gate_ref, yt_ref):
    lane = lax.broadcasted_iota(jnp.int32, (TM_MOE, Q_TILE), 1)
    sel = jnp.zeros((TM_MOE, Q_TILE), F32)
    for k in range(TOP_K):
        sel = jnp.where(lane == pos_ref[:, k:k + 1], gate_ref[:, k:k + 1], sel)
    acc = jnp.dot(sel.astype(BF16), yt_ref[...], preferred_element_type=F32)
    return x_ref[...] + mod_ref[5:6, :] * acc


def _combine_kernel(x_ref, mod_ref, pos_ref, gate_ref, yt_ref, o_ref):
    o_ref[...] = _combine_rows(x_ref, mod_ref, pos_ref, gate_ref, yt_ref)


def _combine_final_kernel(x_ref, mod_ref, pos_ref, gate_ref, yt_ref, fg_ref, ctx_ref, lat_ref):
    y = _combine_rows(x_ref, mod_ref, pos_ref, gate_ref, yt_ref)
    y = y * lax.rsqrt(jnp.mean(y * y, axis=-1, keepdims=True) + RMS_EPS) * fg_ref[...]
    is_ctx = pl.program_id(0) < MOE_CTX_TILES

    @pl.when(is_ctx)
    def _():
        ctx_ref[...] = y

    @pl.when(jnp.logical_not(is_ctx))
    def _():
        lat_ref[...] = y


def _combine(x, mod, pos_t, gate_t, yt, layer, final_g=None):
    tile = pl.BlockSpec((TM_MOE, D_MODEL), lambda i: (i, 0))
    in_specs = [tile, _mod_spec(TM_MOE, layer),
                pl.BlockSpec((TM_MOE, 8), lambda i: (i, 0)),
                pl.BlockSpec((TM_MOE, 8), lambda i: (i, 0)),
                pl.BlockSpec((Q_TILE, D_MODEL), lambda i: (i, 0))]
    if final_g is None:
        return pl.pallas_call(
            _combine_kernel, grid=(N_TILES,), in_specs=in_specs, out_specs=tile,
            out_shape=jax.ShapeDtypeStruct((T_ALL, D_MODEL), F32),
            compiler_params=_cparams(("arbitrary",)), name="moe_combine",
        )(x, mod, pos_t, gate_t, yt)
    return pl.pallas_call(
        _combine_final_kernel, grid=(N_TILES,),
        in_specs=in_specs + [pl.BlockSpec((1, D_MODEL), lambda i: (0, 0))],
        out_specs=[pl.BlockSpec((TM_MOE, D_MODEL), lambda i: (jnp.minimum(i, MOE_CTX_TILES - 1), 0)),
                   pl.BlockSpec((TM_MOE, D_MODEL), lambda i: (jnp.maximum(i - MOE_CTX_TILES, 0), 0))],
        out_shape=[jax.ShapeDtypeStruct((T_CTX, D_MODEL), F32), jax.ShapeDtypeStruct((T_LAT, D_MODEL), F32)],
        compiler_params=_cparams(("arbitrary",)), name="moe_combine_final",
    )(x, mod, pos_t, gate_t, yt, final_g)


def _moe(x, norm_g, mod, w_rt, b_r, w_gu, b_gu, w_down, b_down, layer, final_g=None):
    xt, pos, gate, nch, seg = _route_sort(x, norm_g, mod, w_rt, b_r, layer)
    yt = _experts(nch[:, :, 0].reshape(-1), seg[:, :, 0].reshape(-1), xt, w_gu,
                  b_gu.reshape(DEPTH, N_EXPERTS, 1, 2 * D_EXPERT), w_down,
                  b_down.reshape(DEPTH, N_EXPERTS, 1, D_MODEL), layer)
    return _combine(x, mod, pos.T, gate.T, yt, layer, final_g)


PREP_ROWS = 128


def _prep_in_kernel(w_ref, zb_ref, zg_ref, zf_ref):
    conv0, mq0, om0 = 3 * D_A, 3 * D_A + 2 * D_CONV, 3 * D_A + 2 * D_CONV + 3 * D_M
    gates_end = GATE_OFF + N_GATE_M
    zb_ref[:, :conv0] = w_ref[:, :conv0].astype(BF16)
    zb_ref[:, conv0:] = w_ref[:, mq0:om0].astype(BF16)
    zg_ref[...] = w_ref[:, gates_end:].astype(BF16)
    zf_ref[:, :mq0 - conv0] = w_ref[:, conv0:mq0].astype(BF16)
    zf_ref[:, mq0 - conv0:mq0 - conv0 + D_M] = w_ref[:, om0:GATE_OFF].astype(BF16)
    tail = w_ref[:, GATE_OFF:GATE_OFF + 128]
    keep = lax.broadcasted_iota(jnp.int32, tail.shape, 1) < N_GATE_M
    zf_ref[:, mq0 - conv0 + D_M:] = jnp.where(keep, tail, 0.0).astype(BF16)


def _prep_in_weights(w_in):
    def out(n):
        return (pl.BlockSpec((None, PREP_ROWS, n), lambda l, i: (l, i, 0)),
                jax.ShapeDtypeStruct((DEPTH, D_MODEL, n), BF16))

    specs, shapes = zip(out(N_ZB), out(N_ZG), out(N_ZF))
    return pl.pallas_call(
        _prep_in_kernel,
        grid=(DEPTH, D_MODEL // PREP_ROWS),
        in_specs=[pl.BlockSpec((None, PREP_ROWS, w_in.shape[-1]), lambda l, i: (l, i, 0))],
        out_specs=list(specs),
        out_shape=list(shapes),
        compiler_params=_cparams(("arbitrary", "arbitrary")),
        name="prep_in_weights",
    )(w_in)


def _split_in_cols(w):
    conv0, mq0, om0 = 3 * D_A, 3 * D_A + 2 * D_CONV, 3 * D_A + 2 * D_CONV + 3 * D_M
    gates_end = GATE_OFF + N_GATE_M
    pad = jnp.zeros(w.shape[:-1] + (N_ZF - (2 * D_CONV + D_M + N_GATE_M),), w.dtype)
    zb = jnp.concatenate([w[..., :conv0], w[..., mq0:om0]], axis=-1)
    zg = w[..., gates_end:]
    zf = jnp.concatenate([w[..., conv0:mq0], w[..., om0:GATE_OFF], w[..., GATE_OFF:gates_end], pad], axis=-1)
    return zb, zg, zf


def kernel(x_prompt, x_sample, cache_k, cache_v, state_C, state_n, state_m, c, c_ctx, norm1_g, w_mod, b_mod, w_in, b_in, rpb, w_dw, b_dw, cln_g, cln_b, mnorm_g, w_pa, w_pc, w_pm, w_out, norm2_g, w_router, b_router, w_gu, b_gu, w_down, b_down, final_g):
    cond = jnp.concatenate([c_ctx[None, :], c, jnp.zeros((SEG_PAD - N_SEG, D_MODEL), F32)], axis=0)
    mod_all = _modulation(cond, w_mod, b_mod).reshape(DEPTH, SEG_PAD, 6, D_MODEL)

    x = (x_prompt.reshape(T_CTX, D_MODEL), x_sample.reshape(T_LAT, D_MODEL))
    ck = cache_k.reshape(DEC_BATCH, DEPTH, PAST_LEN, D_A).astype(BF16)
    cv = cache_v.reshape(DEC_BATCH, DEPTH, PAST_LEN, D_A).astype(BF16)
    lat_c0 = state_C.reshape(DEC_BATCH, DEPTH, N_STREAM, HEAD_DIM_M, HEAD_DIM_M)
    lat_n0 = state_n.reshape(DEC_BATCH, DEPTH, N_STREAM, HEAD_DIM_M)
    lat_m0 = jnp.broadcast_to(state_m.reshape(DEC_BATCH, DEPTH, N_STREAM, 1),
                              (DEC_BATCH, DEPTH, N_STREAM, HEAD_DIM_M))
    ctx_c0 = jnp.zeros((1, N_STREAM, HEAD_DIM_M, HEAD_DIM_M), F32)
    ctx_n0 = jnp.zeros((1, N_STREAM, HEAD_DIM_M), F32)
    ctx_m0 = jnp.full((1, N_STREAM, HEAD_DIM_M), -jnp.inf, F32)

    w_in_groups = _prep_in_weights(w_in)
    b_in_groups = _split_in_cols(b_in[:, None, :])
    merge_w = tuple(w.astype(BF16) for w in (w_pa, w_pc, w_pm, w_out))
    toe = _natt_bias(rpb)
    mod = mod_all
    g1, g2, ng = norm1_g[:, None, :], norm2_g[:, None, :], mnorm_g[:, None, :]
    conv_w = (w_dw, b_dw[:, None, :], cln_g[:, None, :], cln_b[:, None, :])
    w_rt, b_r = jnp.swapaxes(w_router, 1, 2), b_router[:, :, None]
    ks, vs, cs, ns, ms = [], [], [], [], []
    for l in range(DEPTH):
        zb, zg, zf, k_l, v_l = _in_proj(x, g1, mod, w_in_groups, b_in_groups, l)
        ya = (_ctx_attention(zb), _natt(zb, ck, cv, toe, l))
        yc = (_conv(zf, *conv_w, SEQ, BATCH, 0, l),
              _conv(zf, *conv_w, DEC_SEQ, DEC_BATCH, T_CTX // DEC_SEQ, l))
        ym_ctx, c_l, n_l, m_l = _mlstm(zb, zf, ctx_c0, ctx_n0, ctx_m0, ng, SEQ, BATCH, 0, lambda b: (0,), l)
        ym_lat, _, _, _ = _mlstm(zb, zf, lat_c0, lat_n0, lat_m0, ng, DEC_SEQ, DEC_BATCH, T_CTX // DEC_SEQ,
                                 lambda b: (b, l), l)
        x = _merge(x, mod, ya, yc, (ym_ctx, ym_lat), zg, *merge_w, l)
        x = _moe(x, g2, mod, w_rt, b_r, w_gu, b_gu, w_down, b_down, l,
                 final_g[None, :] if l == DEPTH - 1 else None)
        ks.append(k_l.reshape(BATCH, SEQ, N_HEADS_A, HEAD_DIM_A))
        vs.append(v_l.reshape(BATCH, SEQ, N_HEADS_A, HEAD_DIM_A))
        cs.append(c_l.reshape(BATCH, 2, N_HEADS_M, HEAD_DIM_M, HEAD_DIM_M))
        ns.append(n_l.reshape(BATCH, 2, N_HEADS_M, HEAD_DIM_M))
        ms.append(m_l[:, :, 0].reshape(BATCH, 2, N_HEADS_M))

    y_ctx, y_lat = x
    return (y_ctx.reshape(BATCH, SEQ, D_MODEL), y_lat.reshape(DEC_BATCH, DEC_SEQ, D_MODEL),
            jnp.stack(ks, axis=1), jnp.stack(vs, axis=1), jnp.stack(cs, axis=1),
            jnp.stack(ns, axis=1), jnp.stack(ms, axis=1))
```

```python
import functools

import numpy as np
import jax
import jax.numpy as jnp
from jax import lax
from jax.experimental import pallas as pl
from jax.experimental.pallas import tpu as pltpu

F32 = jnp.float32
BF16 = jnp.bfloat16
HIGHEST = lax.Precision.HIGHEST

D_MODEL = 1024
BATCH = 16
SEQ = 256
DEPTH = 2
DEC_BATCH = 8
DEC_SEQ = 1024
PAST_LEN = 512
GRID_W = 64
N_HEADS_A = 8
HEAD_DIM_A = 64
D_A = N_HEADS_A * HEAD_DIM_A
WIN_ROWS = 8
WIN_COLS = 16
D_CONV = 512
CONV_WIDTH = 31
N_HEADS_M = 4
HEAD_DIM_M = 128
D_M = N_HEADS_M * HEAD_DIM_M
N_GATE_M = 4 * N_HEADS_M
N_EXPERTS = 32
TOP_K = 4
D_EXPERT = 1024
SWIGLU_ALPHA = 1.702
SWIGLU_LIMIT = 7.0
RMS_EPS = 1e-6
LN_EPS = 1e-5
GATE_OFF = 3 * D_A + 2 * D_CONV + 4 * D_M

T_CTX = BATCH * SEQ
T_LAT = DEC_BATCH * DEC_SEQ
T_ALL = T_CTX + T_LAT
N_SEG = 1 + DEC_BATCH
SEG_PAD = 16
GRID_ROWS = DEC_SEQ // GRID_W
NEG = -1e30

N_ZB = 6 * 512
N_ZG = 3 * D_MODEL
N_ZF = 1664
QB_QA, QB_KA, QB_VA, QB_QM, QB_KM, QB_VM = 0, 1, 2, 3, 4, 5
ZG_GA, ZG_GC, ZG_GM = 0, 1, 2
ZF_CU, ZF_CG, ZF_OM = 0, 1, 2
ZF_GATES = 12

TM_TOK = 512
TM_PROJ = 512
TM_MOE = 512
N_TILES = T_ALL // TM_MOE
CHUNK_ROWS = 16
MXU_ROWS = 256
Q_TILE = -(-(TM_MOE * TOP_K + N_EXPERTS * (CHUNK_ROWS - 1)) // MXU_ROWS) * MXU_ROWS
CH_PER_TILE = Q_TILE // CHUNK_ROWS
E_GROUP = 256
CPG = E_GROUP // CHUNK_ROWS
MAX_CHUNKS = ((T_ALL * TOP_K + N_TILES * N_EXPERTS * (CHUNK_ROWS - 1)) // CHUNK_ROWS
              + N_EXPERTS * (CPG - 1)) + CPG
assert Q_TILE - (TM_MOE * TOP_K + N_EXPERTS * (CHUNK_ROWS - 1)) >= 2 * CHUNK_ROWS and N_TILES > CPG
READ_SPARE = N_TILES * Q_TILE - CHUNK_ROWS
VMEM_LIMIT = 60 * 1024 * 1024


def _cparams(sem=None):
    return pltpu.CompilerParams(dimension_semantics=sem, vmem_limit_bytes=VMEM_LIMIT)


def _seg_of_tile(i, tile):
    n_ctx = T_CTX // tile
    per_lat = DEC_SEQ // tile
    return jnp.where(i < n_ctx, 0, 1 + (i - n_ctx) // per_lat)


def _mod_spec(tile, layer):
    return pl.BlockSpec((None, None, 6, D_MODEL), lambda i: (layer, _seg_of_tile(i, tile), 0, 0))


def _layer_spec(shape, layer):
    return pl.BlockSpec((None,) + shape, lambda *_: (layer,) + (0,) * len(shape))


def _x_pair(x, tile):
    n_ctx = T_CTX // tile
    arrays, lat0 = (x, 0) if isinstance(x, tuple) else ((x, x), n_ctx)
    specs = (pl.BlockSpec((tile, D_MODEL), lambda i: (jnp.minimum(i, n_ctx - 1), 0)),
             pl.BlockSpec((tile, D_MODEL), lambda i: (lat0 + jnp.maximum(i - n_ctx, 0), 0)))
    return arrays, specs


def _dot_nt(a, b):
    return lax.dot_general(a, b, (((1,), (1,)), ((), ())), preferred_element_type=F32)


def _sigmoid(x):
    return 1.0 / (1.0 + jnp.exp(-x))


def _mod_kernel(c_ref, w_ref, b_ref, o_ref):
    c = c_ref[...]
    s = c * _sigmoid(c)
    o_ref[...] = jnp.dot(s, w_ref[...], precision=HIGHEST, preferred_element_type=F32) + b_ref[...]


def _modulation(cond, w_mod, b_mod):
    tn = 1536
    return pl.pallas_call(
        _mod_kernel,
        grid=(DEPTH, 6 * D_MODEL // tn),
        in_specs=[pl.BlockSpec((SEG_PAD, D_MODEL), lambda l, j: (0, 0)),
                  pl.BlockSpec((None, D_MODEL, tn), lambda l, j: (l, 0, j)),
                  pl.BlockSpec((None, 1, tn), lambda l, j: (l, 0, j))],
        out_specs=pl.BlockSpec((None, SEG_PAD, tn), lambda l, j: (l, 0, j)),
        out_shape=jax.ShapeDtypeStruct((DEPTH, SEG_PAD, 6 * D_MODEL), F32),
        compiler_params=_cparams(("arbitrary", "arbitrary")),
        name="modulation",
    )(cond, w_mod, b_mod.reshape(DEPTH, 1, 6 * D_MODEL))


def _normmod(x, g, mod, shift_idx, scale_idx):
    y = x * lax.rsqrt(jnp.mean(x * x, axis=-1, keepdims=True) + RMS_EPS) * g
    return y * (1.0 + mod[scale_idx:scale_idx + 1, :]) + mod[shift_idx:shift_idx + 1, :]


PROJ_CTX_TILES = T_CTX // TM_PROJ


def _in_proj_kernel(xc_ref, xl_ref, g_ref, mod_ref, wb_ref, bb_ref, wg_ref, bg_ref, wf_ref, bf_ref,
                    zb_ref, zg_ref, zf_ref, kc_ref, vc_ref):
    x = jnp.where(pl.program_id(0) < PROJ_CTX_TILES, xc_ref[...], xl_ref[...])
    h = _normmod(x, g_ref[...], mod_ref[...], 0, 1).astype(BF16)
    acc = jnp.dot(h, wb_ref[...], preferred_element_type=F32) + bb_ref[...]
    zb_ref[...] = acc.astype(BF16)

    @pl.when(pl.program_id(0) < PROJ_CTX_TILES)
    def _():
        kc_ref[...] = acc[:, D_A:2 * D_A]
        vc_ref[...] = acc[:, 2 * D_A:3 * D_A]

    gates = jnp.dot(h, wg_ref[...], preferred_element_type=F32) + bg_ref[...]
    zg_ref[...] = _sigmoid(gates).astype(BF16)
    zf_ref[...] = jnp.dot(h, wf_ref[...], preferred_element_type=F32) + bf_ref[...]


def _in_proj(x, norm_g, mod, w, b, layer):
    def full(a):
        return pl.BlockSpec((None,) + a.shape[1:], lambda i: (layer, 0, 0), pipeline_mode=pl.Buffered(1))

    def rows(n):
        return pl.BlockSpec((TM_PROJ, n), lambda i: (i, 0))

    x_arrays, x_specs = _x_pair(x, TM_PROJ)
    cache_spec = pl.BlockSpec((TM_PROJ, D_A), lambda i: (jnp.minimum(i, PROJ_CTX_TILES - 1), 0))
    cache_shape = jax.ShapeDtypeStruct((T_CTX, D_A), F32)

    return pl.pallas_call(
        _in_proj_kernel,
        grid=(T_ALL // TM_PROJ,),
        in_specs=[*x_specs, _layer_spec((1, D_MODEL), layer), _mod_spec(TM_PROJ, layer),
                  full(w[0]), full(b[0]), full(w[1]), full(b[1]), full(w[2]), full(b[2])],
        out_specs=[rows(N_ZB), rows(N_ZG), rows(N_ZF), cache_spec, cache_spec],
        out_shape=[jax.ShapeDtypeStruct((T_ALL, N_ZB), BF16), jax.ShapeDtypeStruct((T_ALL, N_ZG), BF16),
                   jax.ShapeDtypeStruct((T_ALL, N_ZF), F32), cache_shape, cache_shape],
        compiler_params=_cparams(("arbitrary",)),
        name="in_proj",
    )(*x_arrays, norm_g, mod, w[0], b[0], w[1], b[1], w[2], b[2])


HEAD_PAIR = 2 * HEAD_DIM_A
ATT_SCALE = HEAD_DIM_A ** -0.5


def _pair_queries(q2):
    lo = lax.broadcasted_iota(jnp.int32, (1, HEAD_PAIR), 1) < HEAD_DIM_A
    q2 = q2 * ATT_SCALE
    zero = jnp.zeros_like(q2)
    return lo, jnp.concatenate([jnp.where(lo, q2, zero), jnp.where(lo, zero, q2)], axis=0)


def _unpair(lo, o_stacked):
    rows = o_stacked.shape[0] // 2
    return jnp.where(lo, o_stacked[:rows], o_stacked[rows:])


def _ctx_attn_kernel(q_ref, k_ref, v_ref, o_ref):
    for hp in range(N_HEADS_A // 2):
        sl = slice(hp * HEAD_PAIR, (hp + 1) * HEAD_PAIR)
        lo, qs = _pair_queries(q_ref[:, sl])
        s = _dot_nt(qs, k_ref[:, sl])
        p = jnp.exp(s - jnp.max(s, axis=-1, keepdims=True))
        l = jnp.sum(p, axis=-1, keepdims=True)
        o = jnp.dot(p.astype(BF16), v_ref[:, sl], preferred_element_type=F32) / l
        o_ref[:, sl] = _unpair(lo, o).astype(o_ref.dtype)


def _ctx_attention(zb):
    def spec(cb):
        return pl.BlockSpec((SEQ, D_A), lambda b: (b, cb))

    return pl.pallas_call(
        _ctx_attn_kernel,
        grid=(BATCH,),
        in_specs=[spec(QB_QA), spec(QB_KA), spec(QB_VA)],
        out_specs=pl.BlockSpec((SEQ, D_A), lambda b: (b, 0)),
        out_shape=jax.ShapeDtypeStruct((T_CTX, D_A), BF16),
        compiler_params=_cparams(("arbitrary",)),
        name="ctx_attention",
    )(zb, zb, zb)


NQ_ROWS = 4
NW_ROWS = 12
NQ_BLOCKS = GRID_ROWS // NQ_ROWS
NQ_TOK = NQ_ROWS * GRID_W
NW_TOK = NW_ROWS * GRID_W


def _window_row(qb, xp):
    return xp.clip(qb * NQ_ROWS - WIN_ROWS // 2, 0, GRID_ROWS - NW_ROWS)


N_REL_ROWS = 2 * WIN_ROWS - 1


def _natt_rel_rows():
    r = np.arange(NQ_BLOCKS)[:, None, None] * NQ_ROWS + np.arange(NQ_ROWS)[None, :, None]
    krow = _window_row(np.arange(NQ_BLOCKS), np)[:, None, None] + np.arange(NW_ROWS)[None, None, :]
    rs = np.clip(r - WIN_ROWS // 2, 0, GRID_ROWS - WIN_ROWS)
    assert ((rs >= krow[:, :, :1]) & (rs + WIN_ROWS <= krow[:, :, -1:] + 1)).all()
    return np.where((krow >= rs) & (krow < rs + WIN_ROWS), krow - r + WIN_ROWS - 1, N_REL_ROWS)


NATT_REL = _natt_rel_rows()


def _natt_kernel(q_ref, k_ref, v_ref, kc_ref, vc_ref, toe_ref, o_ref, bias_scr):
    qb = pl.program_id(0)

    @pl.when(pl.program_id(1) == 0)
    def _():
        for v in range(NQ_BLOCKS):
            @pl.when(qb == v)
            def _(v=v):
                for h in range(N_HEADS_A):
                    for rq in range(NQ_ROWS):
                        for kr in range(NW_ROWS):
                            half = (kr % 2) * GRID_W
                            bias_scr[h, rq * GRID_W:(rq + 1) * GRID_W, kr * GRID_W:(kr + 1) * GRID_W] = (
                                toe_ref[h, int(NATT_REL[v, rq, kr]), :, half:half + GRID_W])

    start = pl.multiple_of(_window_row(qb, jnp) * GRID_W, NQ_TOK)
    band = NW_TOK
    for hp in range(N_HEADS_A // 2):
        sl = slice(hp * HEAD_PAIR, (hp + 1) * HEAD_PAIR)
        lo, qs = _pair_queries(q_ref[:, sl])
        bias = bias_scr[2 * hp:2 * hp + 2].reshape(2 * NQ_TOK, band)
        s_loc = _dot_nt(qs, k_ref[pl.ds(start, band), sl]) + bias
        s_ctx = _dot_nt(qs, kc_ref[:, sl])
        m = jnp.maximum(jnp.max(s_loc, axis=-1, keepdims=True), jnp.max(s_ctx, axis=-1, keepdims=True))
        p_loc = jnp.exp(s_loc - m)
        p_ctx = jnp.exp(s_ctx - m)
        l = jnp.sum(p_loc, axis=-1, keepdims=True) + jnp.sum(p_ctx, axis=-1, keepdims=True)
        o = (jnp.dot(p_loc.astype(BF16), v_ref[pl.ds(start, band), sl], preferred_element_type=F32)
             + jnp.dot(p_ctx.astype(BF16), vc_ref[:, sl], preferred_element_type=F32))
        o_ref[:, sl] = _unpair(lo, o / l).astype(o_ref.dtype)


def _natt_bias(rpb):
    qc = np.arange(GRID_W)
    kc = np.arange(GRID_W)
    cs = np.clip(qc - WIN_COLS // 2, 0, GRID_W - WIN_COLS)
    ok = (kc[None, :] >= cs[:, None]) & (kc[None, :] < cs[:, None] + WIN_COLS)
    dc = np.clip(kc[None, :] - qc[:, None] + WIN_COLS - 1, 0, 2 * WIN_COLS - 2)
    pick = (dc[None] == np.arange(2 * WIN_COLS - 1)[:, None, None]).astype(np.float32)
    toe = jnp.einsum('lhdc,cqk->lhdqk', rpb, jnp.asarray(pick), precision=HIGHEST)
    toe = jnp.where(jnp.asarray(ok), toe, NEG)
    toe = jnp.concatenate([toe, jnp.full((DEPTH, N_HEADS_A, 1, GRID_W, GRID_W), NEG, F32)], axis=2)
    return jnp.concatenate([toe, toe], axis=-1)


def _natt(zb, cache_k, cache_v, toe, layer):
    lat0 = T_CTX // DEC_SEQ
    row0 = T_CTX // NQ_TOK
    return pl.pallas_call(
        _natt_kernel,
        grid=(NQ_BLOCKS, DEC_BATCH),
        in_specs=[pl.BlockSpec((NQ_TOK, D_A), lambda qb, b: (row0 + b * NQ_BLOCKS + qb, QB_QA)),
                  pl.BlockSpec((DEC_SEQ, D_A), lambda qb, b: (lat0 + b, QB_KA)),
                  pl.BlockSpec((DEC_SEQ, D_A), lambda qb, b: (lat0 + b, QB_VA)),
                  pl.BlockSpec((None, None, PAST_LEN, D_A), lambda qb, b: (b, layer, 0, 0)),
                  pl.BlockSpec((None, None, PAST_LEN, D_A), lambda qb, b: (b, layer, 0, 0)),
                  pl.BlockSpec((None, N_HEADS_A, N_REL_ROWS + 1, GRID_W, 2 * GRID_W),
                               lambda qb, b: (layer, 0, 0, 0, 0))],
        out_specs=pl.BlockSpec((NQ_TOK, D_A), lambda qb, b: (b * NQ_BLOCKS + qb, 0)),
        out_shape=jax.ShapeDtypeStruct((T_LAT, D_A), BF16),
        scratch_shapes=[pltpu.VMEM((N_HEADS_A, NQ_TOK, NW_TOK), F32)],
        compiler_params=_cparams(("arbitrary", "arbitrary")),
        name="nbr_attention",
    )(zb, zb, zb, cache_k, cache_v, toe)


CONV_HALO = 16
CONV_ROWS = 64


SUBLANES = 8


def _conv_kernel(u_ref, g_ref, w_ref, b_ref, lg_ref, lb_ref, o_ref, pad_scr, sh_scr, *, seq):
    zeros = jnp.zeros((CONV_HALO, D_CONV), F32)
    pad_scr[0:CONV_HALO, :] = zeros
    pad_scr[CONV_HALO + seq:2 * CONV_HALO + seq, :] = zeros
    pad_scr[CONV_HALO:CONV_HALO + seq, :] = u_ref[...] * _sigmoid(g_ref[...])
    n_sh = seq + 2 * CONV_HALO - SUBLANES
    for s in range(SUBLANES):
        sh_scr[s] = pad_scr[s:s + n_sh, :]
    first = CONV_HALO - CONV_WIDTH // 2
    for c in range(seq // CONV_ROWS):
        base = c * CONV_ROWS
        acc = jnp.broadcast_to(b_ref[...], (CONV_ROWS, D_CONV))
        for j in range(CONV_WIDTH):
            q, s = divmod(first + j, SUBLANES)
            row0 = base + q * SUBLANES
            acc = acc + sh_scr[s, row0:row0 + CONV_ROWS, :] * w_ref[j:j + 1, :]
        mu = jnp.mean(acc, axis=-1, keepdims=True)
        xc = acc - mu
        var = jnp.mean(xc * xc, axis=-1, keepdims=True)
        y = xc * lax.rsqrt(var + LN_EPS) * lg_ref[...] + lb_ref[...]
        o_ref[base:base + CONV_ROWS, :] = (y * _sigmoid(y)).astype(o_ref.dtype)


def _conv(z, w_dw, b_dw, ln_g, ln_b, seq, n_seq, row_block0, layer):
    def vec():
        return _layer_spec((1, D_CONV), layer)

    return pl.pallas_call(
        functools.partial(_conv_kernel, seq=seq),
        grid=(n_seq,),
        in_specs=[pl.BlockSpec((seq, D_CONV), lambda b: (row_block0 + b, ZF_CU)),
                  pl.BlockSpec((seq, D_CONV), lambda b: (row_block0 + b, ZF_CG)),
                  _layer_spec((CONV_WIDTH, D_CONV), layer),
                  vec(), vec(), vec()],
        out_specs=pl.BlockSpec((seq, D_CONV), lambda b: (b, 0)),
        out_shape=jax.ShapeDtypeStruct((n_seq * seq, D_CONV), BF16),
        scratch_shapes=[pltpu.VMEM((seq + 2 * CONV_HALO, D_CONV), F32),
                        pltpu.VMEM((SUBLANES, seq + 2 * CONV_HALO - SUBLANES, D_CONV), F32)],
        compiler_params=_cparams(("arbitrary",)),
        name="conformer_conv",
    )(z, z, w_dw, b_dw, ln_g, ln_b)


N_STREAM = 2 * N_HEADS_M
MCHUNK = 256


PAIR_M = 2 * HEAD_DIM_M
N_PAIR = N_STREAM // 2


def _per_head_rows(r):
    return jnp.concatenate([jnp.broadcast_to(r[0:1], (HEAD_DIM_M, r.shape[1])),
                            jnp.broadcast_to(r[1:2], (HEAD_DIM_M, r.shape[1]))], axis=0)


def _mlstm_kernel(q_ref, k_ref, v_ref, om_ref, gt_ref, c0_ref, n0_ref, m0_ref, ng_ref,
                  y_ref, c_out, n_out, m_out, hf_scr, hb_scr, c_scr, n_scr, m_scr, cbd_scr, vbd_scr,
                  rows_scr, acol_scr, *, seq):
    nc = seq // MCHUNK
    c_scr[...] = c0_ref[...]
    n_scr[...] = n0_ref[...]
    m_scr[...] = m0_ref[...]
    cbd_scr[...] = jnp.zeros_like(cbd_scr)
    vbd_scr[...] = jnp.zeros_like(vbd_scr)
    for s_id in range(N_STREAM):
        blk = slice((s_id % 2) * HEAD_DIM_M, (s_id % 2 + 1) * HEAD_DIM_M)
        cbd_scr[s_id // 2, blk, blk] = c0_ref[s_id].astype(BF16)
    first_head = lax.broadcasted_iota(jnp.int32, (1, PAIR_M), 1) < HEAD_DIM_M
    rows = lax.broadcasted_iota(jnp.int32, (MCHUNK, MCHUNK), 0)
    cols = lax.broadcasted_iota(jnp.int32, (MCHUNK, MCHUNK), 1)
    tris = ((cols <= rows).astype(F32), (cols >= rows).astype(F32))
    reach = (rows <= cols, rows >= cols)
    lane = lax.broadcasted_iota(jnp.int32, (1, MCHUNK), 1)
    kscale = HEAD_DIM_M ** -0.5

    for cc in range(nc):
        g = gt_ref[cc * MCHUNK:(cc + 1) * MCHUNK, :]
        lf = jnp.minimum(g, 0.0) - jnp.log(1.0 + jnp.exp(-jnp.abs(g)))
        g_t = g.T
        for d in range(2):
            cum = jnp.dot(tris[d], lf, precision=HIGHEST, preferred_element_type=F32)
            cum_t = cum.T
            i0 = 2 * N_HEADS_M * d
            i_rows = g_t[i0:i0 + N_HEADS_M, :]
            b_rows = cum_t[i0 + N_HEADS_M:i0 + 2 * N_HEADS_M, :]
            reach_max = i_rows - b_rows
            for step in (1 << s for s in range(MCHUNK.bit_length() - 1)):
                if d == 0:
                    shifted = jnp.where(lane >= step, pltpu.roll(reach_max, step, axis=1), NEG)
                else:
                    shifted = jnp.where(lane < MCHUNK - step, pltpu.roll(reach_max, MCHUNK - step, axis=1), NEG)
                reach_max = jnp.maximum(reach_max, shifted)
            rows_scr[2 * cc + d, 0] = i_rows
            rows_scr[2 * cc + d, 1] = b_rows
            rows_scr[2 * cc + d, 2] = reach_max
            acol_scr[2 * cc + d] = g - pltpu.roll(cum, cum.shape[1] - N_HEADS_M, axis=1)

    def chunk_step(c, carry):
        for d in range(2):
            cidx = c if d == 0 else nc - 1 - c
            off = pl.multiple_of(cidx * MCHUNK, MCHUNK)
            last = MCHUNK - 1 if d == 0 else 0
            i0 = 2 * N_HEADS_M * d
            heads = slice(N_HEADS_M * d, N_HEADS_M * (d + 1))
            i_rows = rows_scr[2 * cidx + d, 0]
            b_rows = rows_scr[2 * cidx + d, 1]
            reach_max = rows_scr[2 * cidx + d, 2]
            a_cols = acol_scr[2 * cidx + d]
            m_prevs = m_scr[heads, 0:1]
            inters = b_rows + m_prevs
            m_ts = jnp.maximum(inters, b_rows + reach_max)
            w_inters = jnp.exp(inters - m_ts)
            floors = jnp.exp(-m_ts)
            b_lasts = b_rows[:, last:last + 1]
            m_news = m_ts[:, last:last + 1]
            w_prevs = jnp.exp(b_lasts + m_prevs - m_news)
            w_srcs = kscale * jnp.exp(b_lasts - b_rows + i_rows - m_news)
            m_scr[heads, :] = jnp.broadcast_to(m_news, (N_HEADS_M, HEAD_DIM_M))
            for hp in range(N_HEADS_M // 2):
                pid = (N_HEADS_M // 2) * d + hp
                hh = slice(2 * hp, 2 * hp + 2)
                ps = slice(hp * PAIR_M, (hp + 1) * PAIR_M)
                q2 = q_ref[pl.ds(off, MCHUNK), ps]
                k2 = k_ref[pl.ds(off, MCHUNK), ps]
                v2_t = v_ref[pl.ds(off, MCHUNK), ps].astype(F32).T
                k_zero = jnp.zeros_like(k2)
                k_stack = jnp.concatenate([jnp.where(first_head, k2, k_zero),
                                           jnp.where(first_head, k_zero, k2)], axis=0)
                decay = jnp.concatenate(
                    [jnp.exp(jnp.where(reach[d], b_rows[h:h + 1] + a_cols[:, i0 + h:i0 + h + 1], NEG)
                             - m_ts[h:h + 1]) for h in (2 * hp, 2 * hp + 1)], axis=0)
                s_t = _dot_nt(k_stack, q2) * (kscale * decay)
                col_sums = jnp.concatenate([jnp.sum(s_t[:MCHUNK], axis=0, keepdims=True),
                                            jnp.sum(s_t[MCHUNK:], axis=0, keepdims=True)], axis=0)
                n2 = n_scr[N_HEADS_M * d + 2 * hp:N_HEADS_M * d + 2 * hp + 2, :]
                n_zero = jnp.zeros((1, HEAD_DIM_M), F32)
                n_mat = jnp.concatenate([jnp.concatenate([n2[0:1], n_zero], axis=1),
                                         jnp.concatenate([n_zero, n2[1:2]], axis=1),
                                         jnp.zeros((SUBLANES - 2, PAIR_M), F32)], axis=0)
                n_q = _dot_nt(n_mat.astype(BF16), q2)[0:2, :]
                den = w_inters[hh] * n_q + col_sums
                inv = 1.0 / jnp.maximum(jnp.abs(den), floors[hh])
                vbd_scr[pid, :HEAD_DIM_M, :MCHUNK] = v2_t[:HEAD_DIM_M].astype(BF16)
                vbd_scr[pid, HEAD_DIM_M:, MCHUNK:] = v2_t[HEAD_DIM_M:].astype(BF16)
                num_t = (_per_head_rows(w_inters[hh]) * _dot_nt(cbd_scr[pid], q2)
                         + jnp.dot(vbd_scr[pid], s_t.astype(BF16), preferred_element_type=F32))
                h_t = num_t * _per_head_rows(inv)
                if d == 0:
                    hf_scr[cidx, ps, :] = h_t
                else:
                    hb_scr[cidx, ps, :] = h_t
                upd = jnp.dot((v2_t * _per_head_rows(w_srcs[hh])).astype(BF16), k2,
                              preferred_element_type=F32)
                w_mat = jnp.concatenate([w_srcs[hh], jnp.zeros((SUBLANES - 2, MCHUNK), F32)], axis=0)
                n_upd = jnp.dot(w_mat.astype(BF16), k2, preferred_element_type=F32)
                for j in range(2):
                    h = 2 * hp + j
                    s_id = N_HEADS_M * d + h
                    blk = slice(j * HEAD_DIM_M, (j + 1) * HEAD_DIM_M)
                    c_new = w_prevs[h:h + 1] * c_scr[s_id] + upd[blk, blk]
                    c_scr[s_id] = c_new
                    cbd_scr[pid, blk, blk] = c_new.astype(BF16)
                    n_scr[s_id:s_id + 1, :] = w_prevs[h:h + 1] * n2[j:j + 1] + n_upd[j:j + 1, blk]
        return carry

    lax.fori_loop(0, nc, chunk_step, 0)

    for c in range(nc):
        ts = slice(c * MCHUNK, (c + 1) * MCHUNK)
        for h in range(N_HEADS_M):
            hs = slice(h * HEAD_DIM_M, (h + 1) * HEAD_DIM_M)
            hsum = hf_scr[c, hs, :] + hb_scr[c, hs, :]
            mu = jnp.mean(hsum, axis=0, keepdims=True)
            xc = hsum - mu
            var = jnp.mean(xc * xc, axis=0, keepdims=True)
            hn = (xc * lax.rsqrt(var + LN_EPS)).T
            y_ref[ts, hs] = (_sigmoid(om_ref[ts, hs]) * (hn * ng_ref[:, hs])).astype(y_ref.dtype)
    c_out[...] = c_scr[...]
    n_out[...] = n_scr[...]
    m_out[...] = m_scr[...]


def _mlstm(zb, zf, c0, n0, m0, norm_g, seq, n_seq, row_block0, state_map, layer):
    lead = len(state_map(0))

    def zspec(cb):
        return pl.BlockSpec((seq, D_M), lambda b: (row_block0 + b, cb))

    def sspec(tail):
        return pl.BlockSpec((None,) * lead + tail, lambda b: state_map(b) + (0,) * len(tail))

    return pl.pallas_call(
        functools.partial(_mlstm_kernel, seq=seq),
        grid=(n_seq,),
        in_specs=[zspec(QB_QM), zspec(QB_KM), zspec(QB_VM), zspec(ZF_OM),
                  pl.BlockSpec((seq, 128), lambda b: (row_block0 + b, ZF_GATES)),
                  sspec((N_STREAM, HEAD_DIM_M, HEAD_DIM_M)),
                  sspec((N_STREAM, HEAD_DIM_M)),
                  sspec((N_STREAM, HEAD_DIM_M)),
                  _layer_spec((1, D_M), layer)],
        out_specs=[pl.BlockSpec((seq, D_M), lambda b: (b, 0)),
                   pl.BlockSpec((None, N_STREAM, HEAD_DIM_M, HEAD_DIM_M), lambda b: (b, 0, 0, 0)),
                   pl.BlockSpec((None, N_STREAM, HEAD_DIM_M), lambda b: (b, 0, 0)),
                   pl.BlockSpec((None, N_STREAM, HEAD_DIM_M), lambda b: (b, 0, 0))],
        out_shape=[jax.ShapeDtypeStruct((n_seq * seq, D_M), BF16),
                   jax.ShapeDtypeStruct((n_seq, N_STREAM, HEAD_DIM_M, HEAD_DIM_M), F32),
                   jax.ShapeDtypeStruct((n_seq, N_STREAM, HEAD_DIM_M), F32),
                   jax.ShapeDtypeStruct((n_seq, N_STREAM, HEAD_DIM_M), F32)],
        scratch_shapes=[pltpu.VMEM((seq // MCHUNK, D_M, MCHUNK), F32),
                        pltpu.VMEM((seq // MCHUNK, D_M, MCHUNK), F32),
                        pltpu.VMEM((N_STREAM, HEAD_DIM_M, HEAD_DIM_M), F32),
                        pltpu.VMEM((N_STREAM, HEAD_DIM_M), F32),
                        pltpu.VMEM((N_STREAM, HEAD_DIM_M), F32),
                        pltpu.VMEM((N_PAIR, PAIR_M, PAIR_M), BF16),
                        pltpu.VMEM((N_PAIR, PAIR_M, 2 * MCHUNK), BF16),
                        pltpu.VMEM((2 * (seq // MCHUNK), 3, N_HEADS_M, MCHUNK), F32),
                        pltpu.VMEM((2 * (seq // MCHUNK), MCHUNK, 128), F32)],
        compiler_params=_cparams(("arbitrary",)),
        name="mlstm",
    )(zb, zb, zb, zf, zf, c0, n0, m0, norm_g)


N_CTX_TILES = T_CTX // TM_TOK


def _merge_kernel(xc_ref, xl_ref, mod_ref, ya_c, ya_l, yc_c, yc_l, ym_c, ym_l, ga_ref, gc_ref, gm_ref,
                  wa_ref, wc_ref, wm_ref, wo_ref, o_ref):
    is_ctx = pl.program_id(0) < N_CTX_TILES

    def branch(y_ctx, y_lat, g_ref, w_ref):
        y = jnp.where(is_ctx, y_ctx[...], y_lat[...])
        return g_ref[...].astype(F32) * jnp.dot(y, w_ref[...], preferred_element_type=F32)

    merged = (branch(ya_c, ya_l, ga_ref, wa_ref) + branch(yc_c, yc_l, gc_ref, wc_ref)
              + branch(ym_c, ym_l, gm_ref, wm_ref))
    mix = jnp.dot(merged.astype(BF16), wo_ref[...], preferred_element_type=F32)
    o_ref[...] = jnp.where(is_ctx, xc_ref[...], xl_ref[...]) + mod_ref[2:3, :] * mix


def _merge(x, mod, ya, yc, ym, zg, w_pa, w_pc, w_pm, w_out, layer):
    x_arrays, x_specs = _x_pair(x, TM_TOK)

    def rows(width, cb=0):
        return pl.BlockSpec((TM_TOK, width), lambda i: (i, cb))

    def ctx_rows(width):
        return pl.BlockSpec((TM_TOK, width), lambda i: (jnp.minimum(i, N_CTX_TILES - 1), 0))

    def lat_rows(width):
        return pl.BlockSpec((TM_TOK, width), lambda i: (jnp.maximum(i - N_CTX_TILES, 0), 0))

    def full(shape):
        return pl.BlockSpec((None,) + shape, lambda i: (layer, 0, 0))

    return pl.pallas_call(
        _merge_kernel,
        grid=(T_ALL // TM_TOK,),
        in_specs=[*x_specs, _mod_spec(TM_TOK, layer),
                  ctx_rows(D_A), lat_rows(D_A), ctx_rows(D_CONV), lat_rows(D_CONV),
                  ctx_rows(D_M), lat_rows(D_M),
                  rows(D_MODEL, ZG_GA), rows(D_MODEL, ZG_GC), rows(D_MODEL, ZG_GM),
                  full((D_A, D_MODEL)), full((D_CONV, D_MODEL)), full((D_M, D_MODEL)),
                  full((D_MODEL, D_MODEL))],
        out_specs=rows(D_MODEL),
        out_shape=jax.ShapeDtypeStruct((T_ALL, D_MODEL), F32),
        compiler_params=_cparams(("arbitrary",)),
        name="merge",
    )(*x_arrays, mod, ya[0], ya[1], yc[0], yc[1], ym[0], ym[1], zg, zg, zg, w_pa, w_pc, w_pm, w_out)


def _route_sort_kernel(x_ref, g_ref, mod_ref, wr_ref, br_ref, xt_ref, pos_ref, gate_ref, nch_ref, seg_ref):
    h = _normmod(x_ref[...], g_ref[...], mod_ref[...], 3, 4)
    hb = h.astype(BF16)
    logits = _dot_nt(wr_ref[...].astype(BF16), hb) + br_ref[...]
    e_iota = lax.broadcasted_iota(jnp.int32, (N_EXPERTS, TM_MOE), 0).astype(F32)
    sels, vals = [], []
    l = logits
    for k in range(TOP_K):
        m = jnp.max(l, axis=0, keepdims=True)
        idx = jnp.min(jnp.where(l == m, e_iota, float(N_EXPERTS)), axis=0, keepdims=True)
        sel = e_iota == idx
        vals.append(m)
        sels.append(sel)
        l = jnp.where(sel, -jnp.inf, l)
    exps = [jnp.exp(v - vals[0]) for v in vals]
    tot = exps[0] + exps[1] + exps[2] + exps[3]
    onehot = jnp.zeros((N_EXPERTS, TM_MOE), F32)
    for k in range(TOP_K):
        gate_ref[k:k + 1, :] = exps[k] / tot
        onehot = onehot + sels[k].astype(F32)
    gate_ref[TOP_K:8, :] = jnp.zeros((8 - TOP_K, TM_MOE), F32)

    cnt = jnp.sum(onehot, axis=1, keepdims=True)
    nch = jnp.floor((cnt + (CHUNK_ROWS - 1)) / CHUNK_ROWS)
    ei = lax.broadcasted_iota(jnp.int32, (N_EXPERTS, N_EXPERTS), 0)
    ej = lax.broadcasted_iota(jnp.int32, (N_EXPERTS, N_EXPERTS), 1)
    seg = jnp.dot((ej < ei).astype(F32), jnp.broadcast_to(nch, (N_EXPERTS, 128)), precision=HIGHEST,
                  preferred_element_type=F32)
    nch_ref[...] = jnp.broadcast_to(nch, (N_EXPERTS, 128)).astype(jnp.int32)
    seg_ref[...] = seg.astype(jnp.int32)

    t_src = lax.broadcasted_iota(jnp.int32, (TM_MOE, TM_MOE), 0)
    t_dst = lax.broadcasted_iota(jnp.int32, (TM_MOE, TM_MOE), 1)
    before = (t_src < t_dst).astype(BF16)
    row_of = (seg[:, 0:1] * CHUNK_ROWS
              + jnp.dot(onehot.astype(BF16), before, preferred_element_type=F32))
    q_iota = lax.broadcasted_iota(jnp.int32, (Q_TILE, TM_MOE), 0)
    perm = jnp.zeros((Q_TILE, TM_MOE), F32)
    for k in range(TOP_K):
        q_k = jnp.sum(jnp.where(sels[k], row_of, 0.0), axis=0, keepdims=True).astype(jnp.int32)
        pos_ref[k:k + 1, :] = q_k
        perm = jnp.where(q_iota == q_k, 1.0, perm)
    pos_ref[TOP_K:8, :] = jnp.zeros((8 - TOP_K, TM_MOE), jnp.int32)
    xt_ref[...] = jnp.dot(perm.astype(BF16), hb, preferred_element_type=F32).astype(BF16)


def _route_sort(x, norm_g, mod, w_rt, b_r, layer):
    tspec = pl.BlockSpec((8, TM_MOE), lambda i: (0, i))
    mspec = pl.BlockSpec((None, N_EXPERTS, 128), lambda i: (i, 0, 0))
    meta = jax.ShapeDtypeStruct((N_TILES, N_EXPERTS, 128), jnp.int32)
    return pl.pallas_call(
        _route_sort_kernel,
        grid=(N_TILES,),
        in_specs=[pl.BlockSpec((TM_MOE, D_MODEL), lambda i: (i, 0)),
                  _layer_spec((1, D_MODEL), layer), _mod_spec(TM_MOE, layer),
                  _layer_spec((N_EXPERTS, D_MODEL), layer), _layer_spec((N_EXPERTS, 1), layer)],
        out_specs=[pl.BlockSpec((Q_TILE, D_MODEL), lambda i: (i, 0)), tspec, tspec, mspec, mspec],
        out_shape=[jax.ShapeDtypeStruct((N_TILES * Q_TILE, D_MODEL), BF16),
                   jax.ShapeDtypeStruct((8, T_ALL), jnp.int32), jax.ShapeDtypeStruct((8, T_ALL), F32),
                   meta, meta],
        compiler_params=_cparams(("arbitrary",)),
        name="moe_route_sort",
    )(x, norm_g, mod, w_rt, b_r)


def _expert_kernel(nch_ref, seg_ref, wgu_ref, bgu_ref, wd_ref, bd_ref, xt_ref, yt_ref,
                   wgu_scr, wd_scr, xbuf, ybuf, row_scr, gstart_scr, gsem, ssem):
    del xt_ref
    e = pl.program_id(0)

    def src_row(row):
        return pl.multiple_of(jnp.where(row >= 0, row, READ_SPARE), CHUNK_ROWS)

    def dst_row(row, slot, c):
        spare = c * Q_TILE + jnp.where(slot == 0, Q_TILE - CHUNK_ROWS, Q_TILE - 2 * CHUNK_ROWS)
        return pl.multiple_of(jnp.where(row >= 0, row, spare), CHUNK_ROWS)

    def chunk_rows(c):
        return slice(c * CHUNK_ROWS, (c + 1) * CHUNK_ROWS)

    def start_in(g):
        slot = g % 2
        for c in range(CPG):
            row = row_scr[g * CPG + c]
            pltpu.make_async_copy(yt_ref.at[pl.ds(src_row(row), CHUNK_ROWS), :], xbuf.at[slot, chunk_rows(c), :],
                                  gsem.at[slot]).start(priority=c % 2)

    def start_out(g, slot):
        for c in range(CPG):
            row = row_scr[g * CPG + c]
            pltpu.make_async_copy(ybuf.at[slot, chunk_rows(c), :],
                                  yt_ref.at[pl.ds(dst_row(row, slot, c), CHUNK_ROWS), :],
                                  ssem.at[slot]).start(priority=c % 2)

    def wait_in(slot):
        pltpu.make_async_copy(yt_ref.at[pl.ds(0, E_GROUP), :], xbuf.at[slot], gsem.at[slot]).wait()

    def wait_out(slot):
        pltpu.make_async_copy(ybuf.at[slot], yt_ref.at[pl.ds(0, E_GROUP), :], ssem.at[slot]).wait()

    @pl.when(e == 0)
    def _():
        def per_expert(ee, cnt):
            gstart_scr[ee] = cnt // CPG

            def per_tile(t, cnt):
                first = (t * CH_PER_TILE + seg_ref[t * N_EXPERTS + ee]) * CHUNK_ROWS

                def per_chunk(j, cnt):
                    row_scr[cnt] = first + j * CHUNK_ROWS
                    return cnt + 1

                return lax.fori_loop(0, nch_ref[t * N_EXPERTS + ee], per_chunk, cnt)

            cnt = lax.fori_loop(0, N_TILES, per_tile, cnt)
            padded = (cnt + CPG - 1) // CPG * CPG

            def pad(i, carry):
                row_scr[i] = -1
                return carry

            lax.fori_loop(cnt, padded, pad, 0)
            return padded

        total = lax.fori_loop(0, N_EXPERTS, per_expert, 0)
        gstart_scr[N_EXPERTS] = total // CPG

        def pad(i, carry):
            row_scr[i] = -1
            return carry

        lax.fori_loop(total, total + CPG, pad, 0)
        ybuf[...] = jnp.zeros_like(ybuf)
        start_out(total // CPG, 0)
        start_out(total // CPG, 1)
        start_in(0)

    g_first = gstart_scr[e]
    g_end = gstart_scr[e + 1]

    @pl.when(g_end > g_first)
    def _():
        wgu_scr[...] = wgu_ref[...].astype(BF16)
        wd_scr[...] = wd_ref[...].astype(BF16)

    def group_step(g, carry):
        slot = g % 2
        start_in(g + 1)
        wait_in(slot)
        wait_out(slot)
        hgu = jnp.dot(xbuf[slot], wgu_scr[...], preferred_element_type=F32) + bgu_ref[...]
        h_glu = jnp.minimum(hgu[:, :D_EXPERT], SWIGLU_LIMIT)
        h_lin = jnp.clip(hgu[:, D_EXPERT:], -SWIGLU_LIMIT, SWIGLU_LIMIT)
        act = (h_lin + 1.0) * (h_glu * _sigmoid(SWIGLU_ALPHA * h_glu))
        y = jnp.dot(act.astype(BF16), wd_scr[...], preferred_element_type=F32) + bd_ref[...]
        ybuf[slot] = y.astype(BF16)
        start_out(g, slot)
        return carry

    lax.fori_loop(g_first, g_end, group_step, 0)

    @pl.when(e == N_EXPERTS - 1)
    def _():
        wait_in(gstart_scr[N_EXPERTS] % 2)
        wait_out(0)
        wait_out(1)


def _experts(nch_flat, seg_flat, xt, w_gu, b_gu, w_down, b_down, layer):
    return pl.pallas_call(
        _expert_kernel,
        grid_spec=pltpu.PrefetchScalarGridSpec(
            num_scalar_prefetch=2,
            grid=(N_EXPERTS,),
            in_specs=[pl.BlockSpec((None, None, D_MODEL, 2 * D_EXPERT), lambda e, n, s: (layer, e, 0, 0)),
                      pl.BlockSpec((None, None, 1, 2 * D_EXPERT), lambda e, n, s: (layer, e, 0, 0)),
                      pl.BlockSpec((None, None, D_EXPERT, D_MODEL), lambda e, n, s: (layer, e, 0, 0)),
                      pl.BlockSpec((None, None, 1, D_MODEL), lambda e, n, s: (layer, e, 0, 0)),
                      pl.BlockSpec(memory_space=pl.ANY)],
            out_specs=pl.BlockSpec(memory_space=pl.ANY),
            scratch_shapes=[pltpu.VMEM((D_MODEL, 2 * D_EXPERT), BF16),
                            pltpu.VMEM((D_EXPERT, D_MODEL), BF16),
                            pltpu.VMEM((2, E_GROUP, D_MODEL), BF16),
                            pltpu.VMEM((2, E_GROUP, D_MODEL), BF16),
                            pltpu.SMEM((MAX_CHUNKS,), jnp.int32),
                            pltpu.SMEM((N_EXPERTS + 1,), jnp.int32),
                            pltpu.SemaphoreType.DMA((2,)),
                            pltpu.SemaphoreType.DMA((2,))]),
        out_shape=jax.ShapeDtypeStruct((N_TILES * Q_TILE, D_MODEL), BF16),
        input_output_aliases={6: 0},
        compiler_params=_cparams(("arbitrary",)),
        name="moe_experts",
    )(nch_flat, seg_flat, w_gu, b_gu, w_down, b_down, xt)


MOE_CTX_TILES = T_CTX // TM_MOE


def _combine_rows(x_ref, mod_ref, pos_ref, gate_ref, yt_ref):
    lane = lax.broadcasted_iota(jnp.int32, (TM_MOE, Q_TILE), 1)
    sel = jnp.zeros((TM_MOE, Q_TILE), F32)
    for k in range(TOP_K):
        sel = jnp.where(lane == pos_ref[:, k:k + 1], gate_ref[:, k:k + 1], sel)
    acc = jnp.dot(sel.astype(BF16), yt_ref[...], preferred_element_type=F32)
    return x_ref[...] + mod_ref[5:6, :] * acc


def _combine_kernel(x_ref, mod_ref, pos_ref, gate_ref, yt_ref, o_ref):
    o_ref[...] = _combine_rows(x_ref, mod_ref, pos_ref, gate_ref, yt_ref)


def _combine_final_kernel(x_ref, mod_ref, pos_ref, gate_ref, yt_ref, fg_ref, ctx_ref, lat_ref):
    y = _combine_rows(x_ref, mod_ref, pos_ref, gate_ref, yt_ref)
    y = y * lax.rsqrt(jnp.mean(y * y, axis=-1, keepdims=True) + RMS_EPS) * fg_ref[...]
    is_ctx = pl.program_id(0) < MOE_CTX_TILES

    @pl.when(is_ctx)
    def _():
        ctx_ref[...] = y

    @pl.when(jnp.logical_not(is_ctx))
    def _():
        lat_ref[...] = y


def _combine(x, mod, pos_t, gate_t, yt, layer, final_g=None):
    tile = pl.BlockSpec((TM_MOE, D_MODEL), lambda i: (i, 0))
    in_specs = [tile, _mod_spec(TM_MOE, layer),
                pl.BlockSpec((TM_MOE, 8), lambda i: (i, 0)),
                pl.BlockSpec((TM_MOE, 8), lambda i: (i, 0)),
                pl.BlockSpec((Q_TILE, D_MODEL), lambda i: (i, 0))]
    if final_g is None:
        return pl.pallas_call(
            _combine_kernel, grid=(N_TILES,), in_specs=in_specs, out_specs=tile,
            out_shape=jax.ShapeDtypeStruct((T_ALL, D_MODEL), F32),
            compiler_params=_cparams(("arbitrary",)), name="moe_combine",
        )(x, mod, pos_t, gate_t, yt)
    return pl.pallas_call(
        _combine_final_kernel, grid=(N_TILES,),
        in_specs=in_specs + [pl.BlockSpec((1, D_MODEL), lambda i: (0, 0))],
        out_specs=[pl.BlockSpec((TM_MOE, D_MODEL), lambda i: (jnp.minimum(i, MOE_CTX_TILES - 1), 0)),
                   pl.BlockSpec((TM_MOE, D_MODEL), lambda i: (jnp.maximum(i - MOE_CTX_TILES, 0), 0))],
        out_shape=[jax.ShapeDtypeStruct((T_CTX, D_MODEL), F32), jax.ShapeDtypeStruct((T_LAT, D_MODEL), F32)],
        compiler_params=_cparams(("arbitrary",)), name="moe_combine_final",
    )(x, mod, pos_t, gate_t, yt, final_g)


def _moe(x, norm_g, mod, w_rt, b_r, w_gu, b_gu, w_down, b_down, layer, final_g=None):
    xt, pos, gate, nch, seg = _route_sort(x, norm_g, mod, w_rt, b_r, layer)
    yt = _experts(nch[:, :, 0].reshape(-1), seg[:, :, 0].reshape(-1), xt, w_gu,
                  b_gu.reshape(DEPTH, N_EXPERTS, 1, 2 * D_EXPERT), w_down,
                  b_down.reshape(DEPTH, N_EXPERTS, 1, D_MODEL), layer)
    return _combine(x, mod, pos.T, gate.T, yt, layer, final_g)


PREP_ROWS = 128


def _prep_in_kernel(w_ref, zb_ref, zg_ref, zf_ref):
    conv0, mq0, om0 = 3 * D_A, 3 * D_A + 2 * D_CONV, 3 * D_A + 2 * D_CONV + 3 * D_M
    gates_end = GATE_OFF + N_GATE_M
    zb_ref[:, :conv0] = w_ref[:, :conv0].astype(BF16)
    zb_ref[:, conv0:] = w_ref[:, mq0:om0].astype(BF16)
    zg_ref[...] = w_ref[:, gates_end:].astype(BF16)
    zf_ref[:, :mq0 - conv0] = w_ref[:, conv0:mq0].astype(BF16)
    zf_ref[:, mq0 - conv0:mq0 - conv0 + D_M] = w_ref[:, om0:GATE_OFF].astype(BF16)
    tail = w_ref[:, GATE_OFF:GATE_OFF + 128]
    keep = lax.broadcasted_iota(jnp.int32, tail.shape, 1) < N_GATE_M
    zf_ref[:, mq0 - conv0 + D_M:] = jnp.where(keep, tail, 0.0).astype(BF16)


def _prep_in_weights(w_in):
    def out(n):
        return (pl.BlockSpec((None, PREP_ROWS, n), lambda l, i: (l, i, 0)),
                jax.ShapeDtypeStruct((DEPTH, D_MODEL, n), BF16))

    specs, shapes = zip(out(N_ZB), out(N_ZG), out(N_ZF))
    return pl.pallas_call(
        _prep_in_kernel,
        grid=(DEPTH, D_MODEL // PREP_ROWS),
        in_specs=[pl.BlockSpec((None, PREP_ROWS, w_in.shape[-1]), lambda l, i: (l, i, 0))],
        out_specs=list(specs),
        out_shape=list(shapes),
        compiler_params=_cparams(("arbitrary", "arbitrary")),
        name="prep_in_weights",
    )(w_in)


def _split_in_cols(w):
    conv0, mq0, om0 = 3 * D_A, 3 * D_A + 2 * D_CONV, 3 * D_A + 2 * D_CONV + 3 * D_M
    gates_end = GATE_OFF + N_GATE_M
    pad = jnp.zeros(w.shape[:-1] + (N_ZF - (2 * D_CONV + D_M + N_GATE_M),), w.dtype)
    zb = jnp.concatenate([w[..., :conv0], w[..., mq0:om0]], axis=-1)
    zg = w[..., gates_end:]
    zf = jnp.concatenate([w[..., conv0:mq0], w[..., om0:GATE_OFF], w[..., GATE_OFF:gates_end], pad], axis=-1)
    return zb, zg, zf


def kernel(x_prompt, x_sample, cache_k, cache_v, state_C, state_n, state_m, c, c_ctx, norm1_g, w_mod, b_mod, w_in, b_in, rpb, w_dw, b_dw, cln_g, cln_b, mnorm_g, w_pa, w_pc, w_pm, w_out, norm2_g, w_router, b_router, w_gu, b_gu, w_down, b_down, final_g):
    cond = jnp.concatenate([c_ctx[None, :], c, jnp.zeros((SEG_PAD - N_SEG, D_MODEL), F32)], axis=0)
    mod_all = _modulation(cond, w_mod, b_mod).reshape(DEPTH, SEG_PAD, 6, D_MODEL)

    x = (x_prompt.reshape(T_CTX, D_MODEL), x_sample.reshape(T_LAT, D_MODEL))
    ck = cache_k.reshape(DEC_BATCH, DEPTH, PAST_LEN, D_A).astype(BF16)
    cv = cache_v.reshape(DEC_BATCH, DEPTH, PAST_LEN, D_A).astype(BF16)
    lat_c0 = state_C.reshape(DEC_BATCH, DEPTH, N_STREAM, HEAD_DIM_M, HEAD_DIM_M)
    lat_n0 = state_n.reshape(DEC_BATCH, DEPTH, N_STREAM, HEAD_DIM_M)
    lat_m0 = jnp.broadcast_to(state_m.reshape(DEC_BATCH, DEPTH, N_STREAM, 1),
                              (DEC_BATCH, DEPTH, N_STREAM, HEAD_DIM_M))
    ctx_c0 = jnp.zeros((1, N_STREAM, HEAD_DIM_M, HEAD_DIM_M), F32)
    ctx_n0 = jnp.zeros((1, N_STREAM, HEAD_DIM_M), F32)
    ctx_m0 = jnp.full((1, N_STREAM, HEAD_DIM_M), -jnp.inf, F32)

    w_in_groups = _prep_in_weights(w_in)
    b_in_groups = _split_in_cols(b_in[:, None, :])
    merge_w = tuple(w.astype(BF16) for w in (w_pa, w_pc, w_pm, w_out))
    toe = _natt_bias(rpb)
    mod = mod_all
    g1, g2, ng = norm1_g[:, None, :], norm2_g[:, None, :], mnorm_g[:, None, :]
    conv_w = (w_dw, b_dw[:, None, :], cln_g[:, None, :], cln_b[:, None, :])
    w_rt, b_r = jnp.swapaxes(w_router, 1, 2), b_router[:, :, None]
    ks, vs, cs, ns, ms = [], [], [], [], []
    for l in range(DEPTH):
        zb, zg, zf, k_l, v_l = _in_proj(x, g1, mod, w_in_groups, b_in_groups, l)
        ya = (_ctx_attention(zb), _natt(zb, ck, cv, toe, l))
        yc = (_conv(zf, *conv_w, SEQ, BATCH, 0, l),
              _conv(zf, *conv_w, DEC_SEQ, DEC_BATCH, T_CTX // DEC_SEQ, l))
        ym_ctx, c_l, n_l, m_l = _mlstm(zb, zf, ctx_c0, ctx_n0, ctx_m0, ng, SEQ, BATCH, 0, lambda b: (0,), l)
        ym_lat, _, _, _ = _mlstm(zb, zf, lat_c0, lat_n0, lat_m0, ng, DEC_SEQ, DEC_BATCH, T_CTX // DEC_SEQ,
                                 lambda b: (b, l), l)
        x = _merge(x, mod, ya, yc, (ym_ctx, ym_lat), zg, *merge_w, l)
        x = _moe(x, g2, mod, w_rt, b_r, w_gu, b_gu, w_down, b_down, l,
                 final_g[None, :] if l == DEPTH - 1 else None)
        ks.append(k_l.reshape(BATCH, SEQ, N_HEADS_A, HEAD_DIM_A))
        vs.append(v_l.reshape(BATCH, SEQ, N_HEADS_A, HEAD_DIM_A))
        cs.append(c_l.reshape(BATCH, 2, N_HEADS_M, HEAD_DIM_M, HEAD_DIM_M))
        ns.append(n_l.reshape(BATCH, 2, N_HEADS_M, HEAD_DIM_M))
        ms.append(m_l[:, :, 0].reshape(BATCH, 2, N_HEADS_M))

    y_ctx, y_lat = x
    return (y_ctx.reshape(BATCH, SEQ, D_MODEL), y_lat.reshape(DEC_BATCH, DEC_SEQ, D_MODEL),
            jnp.stack(ks, axis=1), jnp.stack(vs, axis=1), jnp.stack(cs, axis=1),
            jnp.stack(ns, axis=1), jnp.stack(ms, axis=1))
```

```python
import functools

import numpy as np
import jax
import jax.numpy as jnp
from jax import lax
from jax.experimental import pallas as pl
from jax.experimental.pallas import tpu as pltpu

F32 = jnp.float32
BF16 = jnp.bfloat16
HIGHEST = lax.Precision.HIGHEST

D_MODEL = 1024
BATCH = 16
SEQ = 256
DEPTH = 2
DEC_BATCH = 8
DEC_SEQ = 1024
PAST_LEN = 512
GRID_W = 64
N_HEADS_A = 8
HEAD_DIM_A = 64
D_A = N_HEADS_A * HEAD_DIM_A
WIN_ROWS = 8
WIN_COLS = 16
D_CONV = 512
CONV_WIDTH = 31
N_HEADS_M = 4
HEAD_DIM_M = 128
D_M = N_HEADS_M * HEAD_DIM_M
N_GATE_M = 4 * N_HEADS_M
N_EXPERTS = 32
TOP_K = 4
D_EXPERT = 1024
SWIGLU_ALPHA = 1.702
SWIGLU_LIMIT = 7.0
RMS_EPS = 1e-6
LN_EPS = 1e-5
GATE_OFF = 3 * D_A + 2 * D_CONV + 4 * D_M

T_CTX = BATCH * SEQ
T_LAT = DEC_BATCH * DEC_SEQ
T_ALL = T_CTX + T_LAT
N_SEG = 1 + DEC_BATCH
SEG_PAD = 16
GRID_ROWS = DEC_SEQ // GRID_W
NEG = -1e30

N_ZB = 6 * 512
N_ZG = 3 * D_MODEL
N_ZF = 1664
QB_QA, QB_KA, QB_VA, QB_QM, QB_KM, QB_VM = 0, 1, 2, 3, 4, 5
ZG_GA, ZG_GC, ZG_GM = 0, 1, 2
ZF_CU, ZF_CG, ZF_OM = 0, 1, 2
ZF_GATES = 12

TM_TOK = 512
TM_PROJ = 512
TM_MOE = 512
N_TILES = T_ALL // TM_MOE
CHUNK_ROWS = 16
MXU_ROWS = 256
Q_TILE = -(-(TM_MOE * TOP_K + N_EXPERTS * (CHUNK_ROWS - 1)) // MXU_ROWS) * MXU_ROWS
CH_PER_TILE = Q_TILE // CHUNK_ROWS
E_GROUP = 256
CPG = E_GROUP // CHUNK_ROWS
MAX_CHUNKS = ((T_ALL * TOP_K + N_TILES * N_EXPERTS * (CHUNK_ROWS - 1)) // CHUNK_ROWS
              + N_EXPERTS * (CPG - 1)) + CPG
assert Q_TILE - (TM_MOE * TOP_K + N_EXPERTS * (CHUNK_ROWS - 1)) >= 2 * CHUNK_ROWS and N_TILES > CPG
READ_SPARE = N_TILES * Q_TILE - CHUNK_ROWS
VMEM_LIMIT = 60 * 1024 * 1024


def _cparams(sem=None):
    return pltpu.CompilerParams(dimension_semantics=sem, vmem_limit_bytes=VMEM_LIMIT)


def _seg_of_tile(i, tile):
    n_ctx = T_CTX // tile
    per_lat = DEC_SEQ // tile
    return jnp.where(i < n_ctx, 0, 1 + (i - n_ctx) // per_lat)


def _mod_spec(tile, layer):
    return pl.BlockSpec((None, None, 6, D_MODEL), lambda i: (layer, _seg_of_tile(i, tile), 0, 0))


def _layer_spec(shape, layer):
    return pl.BlockSpec((None,) + shape, lambda *_: (layer,) + (0,) * len(shape))


def _x_pair(x, tile):
    n_ctx = T_CTX // tile
    arrays, lat0 = (x, 0) if isinstance(x, tuple) else ((x, x), n_ctx)
    specs = (pl.BlockSpec((tile, D_MODEL), lambda i: (jnp.minimum(i, n_ctx - 1), 0)),
             pl.BlockSpec((tile, D_MODEL), lambda i: (lat0 + jnp.maximum(i - n_ctx, 0), 0)))
    return arrays, specs


def _dot_nt(a, b):
    return lax.dot_general(a, b, (((1,), (1,)), ((), ())), preferred_element_type=F32)


def _sigmoid(x):
    return 1.0 / (1.0 + jnp.exp(-x))


def _mod_kernel(c_ref, w_ref, b_ref, o_ref):
    c = c_ref[...]
    s = c * _sigmoid(c)
    o_ref[...] = jnp.dot(s.astype(BF16), w_ref[...].astype(BF16), preferred_element_type=F32) + b_ref[...]


def _modulation(cond, w_mod, b_mod):
    tn = 1536
    return pl.pallas_call(
        _mod_kernel,
        grid=(DEPTH, 6 * D_MODEL // tn),
        in_specs=[pl.BlockSpec((SEG_PAD, D_MODEL), lambda l, j: (0, 0)),
                  pl.BlockSpec((None, D_MODEL, tn), lambda l, j: (l, 0, j)),
                  pl.BlockSpec((None, 1, tn), lambda l, j: (l, 0, j))],
        out_specs=pl.BlockSpec((None, SEG_PAD, tn), lambda l, j: (l, 0, j)),
        out_shape=jax.ShapeDtypeStruct((DEPTH, SEG_PAD, 6 * D_MODEL), F32),
        compiler_params=_cparams(("arbitrary", "arbitrary")),
        name="modulation",
    )(cond, w_mod, b_mod.reshape(DEPTH, 1, 6 * D_MODEL))


def _normmod(x, g, mod, shift_idx, scale_idx):
    y = x * lax.rsqrt(jnp.mean(x * x, axis=-1, keepdims=True) + RMS_EPS) * g
    return y * (1.0 + mod[scale_idx:scale_idx + 1, :]) + mod[shift_idx:shift_idx + 1, :]


PROJ_CTX_TILES = T_CTX // TM_PROJ


def _in_proj_kernel(xc_ref, xl_ref, g_ref, mod_ref, wb_ref, bb_ref, wg_ref, bg_ref, wf_ref, bf_ref,
                    zb_ref, zg_ref, zf_ref, kc_ref, vc_ref):
    x = jnp.where(pl.program_id(0) < PROJ_CTX_TILES, xc_ref[...], xl_ref[...])
    h = _normmod(x, g_ref[...], mod_ref[...], 0, 1).astype(BF16)
    acc = jnp.dot(h, wb_ref[...], preferred_element_type=F32) + bb_ref[...]
    zb_ref[...] = acc.astype(BF16)

    @pl.when(pl.program_id(0) < PROJ_CTX_TILES)
    def _():
        kc_ref[...] = acc[:, D_A:2 * D_A]
        vc_ref[...] = acc[:, 2 * D_A:3 * D_A]

    gates = jnp.dot(h, wg_ref[...], preferred_element_type=F32) + bg_ref[...]
    zg_ref[...] = _sigmoid(gates).astype(BF16)
    zf_ref[...] = jnp.dot(h, wf_ref[...], preferred_element_type=F32) + bf_ref[...]


def _in_proj(x, norm_g, mod, w, b, layer):
    def full(a):
        return pl.BlockSpec((None,) + a.shape[1:], lambda i: (layer, 0, 0), pipeline_mode=pl.Buffered(1))

    def rows(n):
        return pl.BlockSpec((TM_PROJ, n), lambda i: (i, 0))

    x_arrays, x_specs = _x_pair(x, TM_PROJ)
    cache_spec = pl.BlockSpec((TM_PROJ, D_A), lambda i: (jnp.minimum(i, PROJ_CTX_TILES - 1), 0))
    cache_shape = jax.ShapeDtypeStruct((T_CTX, D_A), F32)

    return pl.pallas_call(
        _in_proj_kernel,
        grid=(T_ALL // TM_PROJ,),
        in_specs=[*x_specs, _layer_spec((1, D_MODEL), layer), _mod_spec(TM_PROJ, layer),
                  full(w[0]), full(b[0]), full(w[1]), full(b[1]), full(w[2]), full(b[2])],
        out_specs=[rows(N_ZB), rows(N_ZG), rows(N_ZF), cache_spec, cache_spec],
        out_shape=[jax.ShapeDtypeStruct((T_ALL, N_ZB), BF16), jax.ShapeDtypeStruct((T_ALL, N_ZG), BF16),
                   jax.ShapeDtypeStruct((T_ALL, N_ZF), F32), cache_shape, cache_shape],
        compiler_params=_cparams(("arbitrary",)),
        name="in_proj",
    )(*x_arrays, norm_g, mod, w[0], b[0], w[1], b[1], w[2], b[2])


HEAD_PAIR = 2 * HEAD_DIM_A
ATT_SCALE = HEAD_DIM_A ** -0.5


def _pair_queries(q2):
    lo = lax.broadcasted_iota(jnp.int32, (1, HEAD_PAIR), 1) < HEAD_DIM_A
    q2 = q2 * ATT_SCALE
    zero = jnp.zeros_like(q2)
    return lo, jnp.concatenate([jnp.where(lo, q2, zero), jnp.where(lo, zero, q2)], axis=0)


def _unpair(lo, o_stacked):
    rows = o_stacked.shape[0] // 2
    return jnp.where(lo, o_stacked[:rows], o_stacked[rows:])


def _ctx_attn_kernel(q_ref, k_ref, v_ref, o_ref):
    for hp in range(N_HEADS_A // 2):
        sl = slice(hp * HEAD_PAIR, (hp + 1) * HEAD_PAIR)
        lo, qs = _pair_queries(q_ref[:, sl])
        s = _dot_nt(qs, k_ref[:, sl])
        p = jnp.exp(s - jnp.max(s, axis=-1, keepdims=True))
        l = jnp.sum(p, axis=-1, keepdims=True)
        o = jnp.dot(p.astype(BF16), v_ref[:, sl], preferred_element_type=F32) / l
        o_ref[:, sl] = _unpair(lo, o).astype(o_ref.dtype)


def _ctx_attention(zb):
    def spec(cb):
        return pl.BlockSpec((SEQ, D_A), lambda b: (b, cb))

    return pl.pallas_call(
        _ctx_attn_kernel,
        grid=(BATCH,),
        in_specs=[spec(QB_QA), spec(QB_KA), spec(QB_VA)],
        out_specs=pl.BlockSpec((SEQ, D_A), lambda b: (b, 0)),
        out_shape=jax.ShapeDtypeStruct((T_CTX, D_A), BF16),
        compiler_params=_cparams(("arbitrary",)),
        name="ctx_attention",
    )(zb, zb, zb)


NQ_ROWS = 4
NW_ROWS = 12
NQ_BLOCKS = GRID_ROWS // NQ_ROWS
NQ_TOK = NQ_ROWS * GRID_W
NW_TOK = NW_ROWS * GRID_W


def _window_row(qb, xp):
    return xp.clip(qb * NQ_ROWS - WIN_ROWS // 2, 0, GRID_ROWS - NW_ROWS)


N_REL_ROWS = 2 * WIN_ROWS - 1


def _natt_rel_rows():
    r = np.arange(NQ_BLOCKS)[:, None, None] * NQ_ROWS + np.arange(NQ_ROWS)[None, :, None]
    krow = _window_row(np.arange(NQ_BLOCKS), np)[:, None, None] + np.arange(NW_ROWS)[None, None, :]
    rs = np.clip(r - WIN_ROWS // 2, 0, GRID_ROWS - WIN_ROWS)
    assert ((rs >= krow[:, :, :1]) & (rs + WIN_ROWS <= krow[:, :, -1:] + 1)).all()
    return np.where((krow >= rs) & (krow < rs + WIN_ROWS), krow - r + WIN_ROWS - 1, N_REL_ROWS)


NATT_REL = _natt_rel_rows()


def _natt_kernel(q_ref, k_ref, v_ref, kc_ref, vc_ref, toe_ref, o_ref, bias_scr):
    qb = pl.program_id(0)

    @pl.when(pl.program_id(1) == 0)
    def _():
        for v in range(NQ_BLOCKS):
            @pl.when(qb == v)
            def _(v=v):
                for h in range(N_HEADS_A):
                    for rq in range(NQ_ROWS):
                        for kr in range(NW_ROWS):
                            half = (kr % 2) * GRID_W
                            bias_scr[h, rq * GRID_W:(rq + 1) * GRID_W, kr * GRID_W:(kr + 1) * GRID_W] = (
                                toe_ref[h, int(NATT_REL[v, rq, kr]), :, half:half + GRID_W])

    start = pl.multiple_of(_window_row(qb, jnp) * GRID_W, NQ_TOK)
    band = NW_TOK
    for hp in range(N_HEADS_A // 2):
        sl = slice(hp * HEAD_PAIR, (hp + 1) * HEAD_PAIR)
        lo, qs = _pair_queries(q_ref[:, sl])
        bias = bias_scr[2 * hp:2 * hp + 2].reshape(2 * NQ_TOK, band)
        s_loc = _dot_nt(qs, k_ref[pl.ds(start, band), sl]) + bias
        s_ctx = _dot_nt(qs, kc_ref[:, sl])
        m = jnp.maximum(jnp.max(s_loc, axis=-1, keepdims=True), jnp.max(s_ctx, axis=-1, keepdims=True))
        p_loc = jnp.exp(s_loc - m)
        p_ctx = jnp.exp(s_ctx - m)
        l = jnp.sum(p_loc, axis=-1, keepdims=True) + jnp.sum(p_ctx, axis=-1, keepdims=True)
        o = (jnp.dot(p_loc.astype(BF16), v_ref[pl.ds(start, band), sl], preferred_element_type=F32)
             + jnp.dot(p_ctx.astype(BF16), vc_ref[:, sl], preferred_element_type=F32))
        o_ref[:, sl] = _unpair(lo, o / l).astype(o_ref.dtype)


def _natt_bias(rpb):
    qc = np.arange(GRID_W)
    kc = np.arange(GRID_W)
    cs = np.clip(qc - WIN_COLS // 2, 0, GRID_W - WIN_COLS)
    ok = (kc[None, :] >= cs[:, None]) & (kc[None, :] < cs[:, None] + WIN_COLS)
    dc = np.clip(kc[None, :] - qc[:, None] + WIN_COLS - 1, 0, 2 * WIN_COLS - 2)
    pick = (dc[None] == np.arange(2 * WIN_COLS - 1)[:, None, None]).astype(np.float32)
    toe = jnp.einsum('lhdc,cqk->lhdqk', rpb, jnp.asarray(pick), precision=HIGHEST)
    toe = jnp.where(jnp.asarray(ok), toe, NEG)
    toe = jnp.concatenate([toe, jnp.full((DEPTH, N_HEADS_A, 1, GRID_W, GRID_W), NEG, F32)], axis=2)
    return jnp.concatenate([toe, toe], axis=-1)


def _natt(zb, cache_k, cache_v, toe, layer):
    lat0 = T_CTX // DEC_SEQ
    row0 = T_CTX // NQ_TOK
    return pl.pallas_call(
        _natt_kernel,
        grid=(NQ_BLOCKS, DEC_BATCH),
        in_specs=[pl.BlockSpec((NQ_TOK, D_A), lambda qb, b: (row0 + b * NQ_BLOCKS + qb, QB_QA)),
                  pl.BlockSpec((DEC_SEQ, D_A), lambda qb, b: (lat0 + b, QB_KA)),
                  pl.BlockSpec((DEC_SEQ, D_A), lambda qb, b: (lat0 + b, QB_VA)),
                  pl.BlockSpec((None, None, PAST_LEN, D_A), lambda qb, b: (b, layer, 0, 0)),
                  pl.BlockSpec((None, None, PAST_LEN, D_A), lambda qb, b: (b, layer, 0, 0)),
                  pl.BlockSpec((None, N_HEADS_A, N_REL_ROWS + 1, GRID_W, 2 * GRID_W),
                               lambda qb, b: (layer, 0, 0, 0, 0))],
        out_specs=pl.BlockSpec((NQ_TOK, D_A), lambda qb, b: (b * NQ_BLOCKS + qb, 0)),
        out_shape=jax.ShapeDtypeStruct((T_LAT, D_A), BF16),
        scratch_shapes=[pltpu.VMEM((N_HEADS_A, NQ_TOK, NW_TOK), F32)],
        compiler_params=_cparams(("arbitrary", "arbitrary")),
        name="nbr_attention",
    )(zb, zb, zb, cache_k, cache_v, toe)


CONV_HALO = 16
CONV_ROWS = 64


SUBLANES = 8


def _conv_kernel(u_ref, g_ref, w_ref, b_ref, lg_ref, lb_ref, o_ref, pad_scr, sh_scr, *, seq):
    zeros = jnp.zeros((CONV_HALO, D_CONV), F32)
    pad_scr[0:CONV_HALO, :] = zeros
    pad_scr[CONV_HALO + seq:2 * CONV_HALO + seq, :] = zeros
    pad_scr[CONV_HALO:CONV_HALO + seq, :] = u_ref[...] * _sigmoid(g_ref[...])
    n_sh = seq + 2 * CONV_HALO - SUBLANES
    for s in range(SUBLANES):
        sh_scr[s] = pad_scr[s:s + n_sh, :]
    first = CONV_HALO - CONV_WIDTH // 2
    for c in range(seq // CONV_ROWS):
        base = c * CONV_ROWS
        acc = jnp.broadcast_to(b_ref[...], (CONV_ROWS, D_CONV))
        for j in range(CONV_WIDTH):
            q, s = divmod(first + j, SUBLANES)
            row0 = base + q * SUBLANES
            acc = acc + sh_scr[s, row0:row0 + CONV_ROWS, :] * w_ref[j:j + 1, :]
        mu = jnp.mean(acc, axis=-1, keepdims=True)
        xc = acc - mu
        var = jnp.mean(xc * xc, axis=-1, keepdims=True)
        y = xc * lax.rsqrt(var + LN_EPS) * lg_ref[...] + lb_ref[...]
        o_ref[base:base + CONV_ROWS, :] = (y * _sigmoid(y)).astype(o_ref.dtype)


def _conv(z, w_dw, b_dw, ln_g, ln_b, seq, n_seq, row_block0, layer):
    def vec():
        return _layer_spec((1, D_CONV), layer)

    return pl.pallas_call(
        functools.partial(_conv_kernel, seq=seq),
        grid=(n_seq,),
        in_specs=[pl.BlockSpec((seq, D_CONV), lambda b: (row_block0 + b, ZF_CU)),
                  pl.BlockSpec((seq, D_CONV), lambda b: (row_block0 + b, ZF_CG)),
                  _layer_spec((CONV_WIDTH, D_CONV), layer),
                  vec(), vec(), vec()],
        out_specs=pl.BlockSpec((seq, D_CONV), lambda b: (b, 0)),
        out_shape=jax.ShapeDtypeStruct((n_seq * seq, D_CONV), BF16),
        scratch_shapes=[pltpu.VMEM((seq + 2 * CONV_HALO, D_CONV), F32),
                        pltpu.VMEM((SUBLANES, seq + 2 * CONV_HALO - SUBLANES, D_CONV), F32)],
        compiler_params=_cparams(("arbitrary",)),
        name="conformer_conv",
    )(z, z, w_dw, b_dw, ln_g, ln_b)


N_STREAM = 2 * N_HEADS_M
MCHUNK = 256


PAIR_M = 2 * HEAD_DIM_M
N_PAIR = N_STREAM // 2


def _per_head_rows(r):
    return jnp.concatenate([jnp.broadcast_to(r[0:1], (HEAD_DIM_M, r.shape[1])),
                            jnp.broadcast_to(r[1:2], (HEAD_DIM_M, r.shape[1]))], axis=0)


def _mlstm_kernel(q_ref, k_ref, v_ref, om_ref, gt_ref, c0_ref, n0_ref, m0_ref, ng_ref,
                  y_ref, c_out, n_out, m_out, hf_scr, hb_scr, c_scr, n_scr, m_scr, cbd_scr, vbd_scr,
                  rows_scr, acol_scr, *, seq):
    nc = seq // MCHUNK
    c_scr[...] = c0_ref[...]
    n_scr[...] = n0_ref[...]
    m_scr[...] = m0_ref[...]
    cbd_scr[...] = jnp.zeros_like(cbd_scr)
    vbd_scr[...] = jnp.zeros_like(vbd_scr)
    for s_id in range(N_STREAM):
        blk = slice((s_id % 2) * HEAD_DIM_M, (s_id % 2 + 1) * HEAD_DIM_M)
        cbd_scr[s_id // 2, blk, blk] = c0_ref[s_id].astype(BF16)
    first_head = lax.broadcasted_iota(jnp.int32, (1, PAIR_M), 1) < HEAD_DIM_M
    rows = lax.broadcasted_iota(jnp.int32, (MCHUNK, MCHUNK), 0)
    cols = lax.broadcasted_iota(jnp.int32, (MCHUNK, MCHUNK), 1)
    tris = ((cols <= rows).astype(F32), (cols >= rows).astype(F32))
    reach = (rows <= cols, rows >= cols)
    lane = lax.broadcasted_iota(jnp.int32, (1, MCHUNK), 1)
    kscale = HEAD_DIM_M ** -0.5

    for cc in range(nc):
        g = gt_ref[cc * MCHUNK:(cc + 1) * MCHUNK, :]
        lf = jnp.minimum(g, 0.0) - jnp.log(1.0 + jnp.exp(-jnp.abs(g)))
        g_t = g.T
        for d in range(2):
            cum = jnp.dot(tris[d], lf, precision=HIGHEST, preferred_element_type=F32)
            cum_t = cum.T
            i0 = 2 * N_HEADS_M * d
            i_rows = g_t[i0:i0 + N_HEADS_M, :]
            b_rows = cum_t[i0 + N_HEADS_M:i0 + 2 * N_HEADS_M, :]
            reach_max = i_rows - b_rows
            for step in (1 << s for s in range(MCHUNK.bit_length() - 1)):
                if d == 0:
                    shifted = jnp.where(lane >= step, pltpu.roll(reach_max, step, axis=1), NEG)
                else:
                    shifted = jnp.where(lane < MCHUNK - step, pltpu.roll(reach_max, MCHUNK - step, axis=1), NEG)
                reach_max = jnp.maximum(reach_max, shifted)
            rows_scr[2 * cc + d, 0] = i_rows
            rows_scr[2 * cc + d, 1] = b_rows
            rows_scr[2 * cc + d, 2] = reach_max
            acol_scr[2 * cc + d] = g - pltpu.roll(cum, cum.shape[1] - N_HEADS_M, axis=1)

    def chunk_step(c, carry):
        for d in range(2):
            cidx = c if d == 0 else nc - 1 - c
            off = pl.multiple_of(cidx * MCHUNK, MCHUNK)
            last = MCHUNK - 1 if d == 0 else 0
            i0 = 2 * N_HEADS_M * d
            heads = slice(N_HEADS_M * d, N_HEADS_M * (d + 1))
            i_rows = rows_scr[2 * cidx + d, 0]
            b_rows = rows_scr[2 * cidx + d, 1]
            reach_max = rows_scr[2 * cidx + d, 2]
            a_cols = acol_scr[2 * cidx + d]
            m_prevs = m_scr[heads, 0:1]
            inters = b_rows + m_prevs
            m_ts = jnp.maximum(inters, b_rows + reach_max)
            w_inters = jnp.exp(inters - m_ts)
            floors = jnp.exp(-m_ts)
            b_lasts = b_rows[:, last:last + 1]
            m_news = m_ts[:, last:last + 1]
            w_prevs = jnp.exp(b_lasts + m_prevs - m_news)
            w_srcs = kscale * jnp.exp(b_lasts - b_rows + i_rows - m_news)
            m_scr[heads, :] = jnp.broadcast_to(m_news, (N_HEADS_M, HEAD_DIM_M))
            for hp in range(N_HEADS_M // 2):
                pid = (N_HEADS_M // 2) * d + hp
                hh = slice(2 * hp, 2 * hp + 2)
                ps = slice(hp * PAIR_M, (hp + 1) * PAIR_M)
                q2 = q_ref[pl.ds(off, MCHUNK), ps]
                k2 = k_ref[pl.ds(off, MCHUNK), ps]
                v2_t = v_ref[pl.ds(off, MCHUNK), ps].astype(F32).T
                k_zero = jnp.zeros_like(k2)
                k_stack = jnp.concatenate([jnp.where(first_head, k2, k_zero),
                                           jnp.where(first_head, k_zero, k2)], axis=0)
                decay = jnp.concatenate(
                    [jnp.exp(jnp.where(reach[d], b_rows[h:h + 1] + a_cols[:, i0 + h:i0 + h + 1], NEG)
                             - m_ts[h:h + 1]) for h in (2 * hp, 2 * hp + 1)], axis=0)
                s_t = _dot_nt(k_stack, q2) * (kscale * decay)
                col_sums = jnp.concatenate([jnp.sum(s_t[:MCHUNK], axis=0, keepdims=True),
                                            jnp.sum(s_t[MCHUNK:], axis=0, keepdims=True)], axis=0)
                n2 = n_scr[N_HEADS_M * d + 2 * hp:N_HEADS_M * d + 2 * hp + 2, :]
                n_zero = jnp.zeros((1, HEAD_DIM_M), F32)
                n_mat = jnp.concatenate([jnp.concatenate([n2[0:1], n_zero], axis=1),
                                         jnp.concatenate([n_zero, n2[1:2]], axis=1),
                                         jnp.zeros((SUBLANES - 2, PAIR_M), F32)], axis=0)
                n_q = _dot_nt(n_mat.astype(BF16), q2)[0:2, :]
                den = w_inters[hh] * n_q + col_sums
                inv = 1.0 / jnp.maximum(jnp.abs(den), floors[hh])
                vbd_scr[pid, :HEAD_DIM_M, :MCHUNK] = v2_t[:HEAD_DIM_M].astype(BF16)
                vbd_scr[pid, HEAD_DIM_M:, MCHUNK:] = v2_t[HEAD_DIM_M:].astype(BF16)
                num_t = (_per_head_rows(w_inters[hh]) * _dot_nt(cbd_scr[pid], q2)
                         + jnp.dot(vbd_scr[pid], s_t.astype(BF16), preferred_element_type=F32))
                h_t = num_t * _per_head_rows(inv)
                if d == 0:
                    hf_scr[cidx, ps, :] = h_t
                else:
                    hb_scr[cidx, ps, :] = h_t
                upd = jnp.dot((v2_t * _per_head_rows(w_srcs[hh])).astype(BF16), k2,
                              preferred_element_type=F32)
                w_mat = jnp.concatenate([w_srcs[hh], jnp.zeros((SUBLANES - 2, MCHUNK), F32)], axis=0)
                n_upd = jnp.dot(w_mat.astype(BF16), k2, preferred_element_type=F32)
                for j in range(2):
                    h = 2 * hp + j
                    s_id = N_HEADS_M * d + h
                    blk = slice(j * HEAD_DIM_M, (j + 1) * HEAD_DIM_M)
                    c_new = w_prevs[h:h + 1] * c_scr[s_id] + upd[blk, blk]
                    c_scr[s_id] = c_new
                    cbd_scr[pid, blk, blk] = c_new.astype(BF16)
                    n_scr[s_id:s_id + 1, :] = w_prevs[h:h + 1] * n2[j:j + 1] + n_upd[j:j + 1, blk]
        return carry

    lax.fori_loop(0, nc, chunk_step, 0)

    for c in range(nc):
        ts = slice(c * MCHUNK, (c + 1) * MCHUNK)
        for h in range(N_HEADS_M):
            hs = slice(h * HEAD_DIM_M, (h + 1) * HEAD_DIM_M)
            hsum = hf_scr[c, hs, :] + hb_scr[c, hs, :]
            mu = jnp.mean(hsum, axis=0, keepdims=True)
            xc = hsum - mu
            var = jnp.mean(xc * xc, axis=0, keepdims=True)
            hn = (xc * lax.rsqrt(var + LN_EPS)).T
            y_ref[ts, hs] = (_sigmoid(om_ref[ts, hs]) * (hn * ng_ref[:, hs])).astype(y_ref.dtype)
    c_out[...] = c_scr[...]
    n_out[...] = n_scr[...]
    m_out[...] = m_scr[...]


def _mlstm(zb, zf, c0, n0, m0, norm_g, seq, n_seq, row_block0, state_map, layer):
    lead = len(state_map(0))

    def zspec(cb):
        return pl.BlockSpec((seq, D_M), lambda b: (row_block0 + b, cb))

    def sspec(tail):
        return pl.BlockSpec((None,) * lead + tail, lambda b: state_map(b) + (0,) * len(tail))

    return pl.pallas_call(
        functools.partial(_mlstm_kernel, seq=seq),
        grid=(n_seq,),
        in_specs=[zspec(QB_QM), zspec(QB_KM), zspec(QB_VM), zspec(ZF_OM),
                  pl.BlockSpec((seq, 128), lambda b: (row_block0 + b, ZF_GATES)),
                  sspec((N_STREAM, HEAD_DIM_M, HEAD_DIM_M)),
                  sspec((N_STREAM, HEAD_DIM_M)),
                  sspec((N_STREAM, HEAD_DIM_M)),
                  _layer_spec((1, D_M), layer)],
        out_specs=[pl.BlockSpec((seq, D_M), lambda b: (b, 0)),
                   pl.BlockSpec((None, N_STREAM, HEAD_DIM_M, HEAD_DIM_M), lambda b: (b, 0, 0, 0)),
                   pl.BlockSpec((None, N_STREAM, HEAD_DIM_M), lambda b: (b, 0, 0)),
                   pl.BlockSpec((None, N_STREAM, HEAD_DIM_M), lambda b: (b, 0, 0))],
        out_shape=[jax.ShapeDtypeStruct((n_seq * seq, D_M), BF16),
                   jax.ShapeDtypeStruct((n_seq, N_STREAM, HEAD_DIM_M, HEAD_DIM_M), F32),
                   jax.ShapeDtypeStruct((n_seq, N_STREAM, HEAD_DIM_M), F32),
                   jax.ShapeDtypeStruct((n_seq, N_STREAM, HEAD_DIM_M), F32)],
        scratch_shapes=[pltpu.VMEM((seq // MCHUNK, D_M, MCHUNK), F32),
                        pltpu.VMEM((seq // MCHUNK, D_M, MCHUNK), F32),
                        pltpu.VMEM((N_STREAM, HEAD_DIM_M, HEAD_DIM_M), F32),
                        pltpu.VMEM((N_STREAM, HEAD_DIM_M), F32),
                        pltpu.VMEM((N_STREAM, HEAD_DIM_M), F32),
                        pltpu.VMEM((N_PAIR, PAIR_M, PAIR_M), BF16),
                        pltpu.VMEM((N_PAIR, PAIR_M, 2 * MCHUNK), BF16),
                        pltpu.VMEM((2 * (seq // MCHUNK), 3, N_HEADS_M, MCHUNK), F32),
                        pltpu.VMEM((2 * (seq // MCHUNK), MCHUNK, 128), F32)],
        compiler_params=_cparams(("arbitrary",)),
        name="mlstm",
    )(zb, zb, zb, zf, zf, c0, n0, m0, norm_g)


N_CTX_TILES = T_CTX // TM_TOK


def _merge_kernel(xc_ref, xl_ref, mod_ref, ya_c, ya_l, yc_c, yc_l, ym_c, ym_l, ga_ref, gc_ref, gm_ref,
                  wa_ref, wc_ref, wm_ref, wo_ref, o_ref):
    is_ctx = pl.program_id(0) < N_CTX_TILES

    def branch(y_ctx, y_lat, g_ref, w_ref):
        y = jnp.where(is_ctx, y_ctx[...], y_lat[...])
        return g_ref[...].astype(F32) * jnp.dot(y, w_ref[...], preferred_element_type=F32)

    merged = (branch(ya_c, ya_l, ga_ref, wa_ref) + branch(yc_c, yc_l, gc_ref, wc_ref)
              + branch(ym_c, ym_l, gm_ref, wm_ref))
    mix = jnp.dot(merged.astype(BF16), wo_ref[...], preferred_element_type=F32)
    o_ref[...] = jnp.where(is_ctx, xc_ref[...], xl_ref[...]) + mod_ref[2:3, :] * mix


def _merge(x, mod, ya, yc, ym, zg, w_pa, w_pc, w_pm, w_out, layer):
    x_arrays, x_specs = _x_pair(x, TM_TOK)

    def rows(width, cb=0):
        return pl.BlockSpec((TM_TOK, width), lambda i: (i, cb))

    def ctx_rows(width):
        return pl.BlockSpec((TM_TOK, width), lambda i: (jnp.minimum(i, N_CTX_TILES - 1), 0))

    def lat_rows(width):
        return pl.BlockSpec((TM_TOK, width), lambda i: (jnp.maximum(i - N_CTX_TILES, 0), 0))

    def full(shape):
        return pl.BlockSpec((None,) + shape, lambda i: (layer, 0, 0))

    return pl.pallas_call(
        _merge_kernel,
        grid=(T_ALL // TM_TOK,),
        in_specs=[*x_specs, _mod_spec(TM_TOK, layer),
                  ctx_rows(D_A), lat_rows(D_A), ctx_rows(D_CONV), lat_rows(D_CONV),
                  ctx_rows(D_M), lat_rows(D_M),
                  rows(D_MODEL, ZG_GA), rows(D_MODEL, ZG_GC), rows(D_MODEL, ZG_GM),
                  full((D_A, D_MODEL)), full((D_CONV, D_MODEL)), full((D_M, D_MODEL)),
                  full((D_MODEL, D_MODEL))],
        out_specs=rows(D_MODEL),
        out_shape=jax.ShapeDtypeStruct((T_ALL, D_MODEL), F32),
        compiler_params=_cparams(("arbitrary",)),
        name="merge",
    )(*x_arrays, mod, ya[0], ya[1], yc[0], yc[1], ym[0], ym[1], zg, zg, zg, w_pa, w_pc, w_pm, w_out)


def _route_sort_kernel(x_ref, g_ref, mod_ref, wr_ref, br_ref, xt_ref, pos_ref, gate_ref, nch_ref, seg_ref):
    h = _normmod(x_ref[...], g_ref[...], mod_ref[...], 3, 4)
    hb = h.astype(BF16)
    logits = _dot_nt(wr_ref[...].astype(BF16), hb) + br_ref[...]
    e_iota = lax.broadcasted_iota(jnp.int32, (N_EXPERTS, TM_MOE), 0).astype(F32)
    sels, vals = [], []
    l = logits
    for k in range(TOP_K):
        m = jnp.max(l, axis=0, keepdims=True)
        idx = jnp.min(jnp.where(l == m, e_iota, float(N_EXPERTS)), axis=0, keepdims=True)
        sel = e_iota == idx
        vals.append(m)
        sels.append(sel)
        l = jnp.where(sel, -jnp.inf, l)
    exps = [jnp.exp(v - vals[0]) for v in vals]
    tot = exps[0] + exps[1] + exps[2] + exps[3]
    onehot = jnp.zeros((N_EXPERTS, TM_MOE), F32)
    for k in range(TOP_K):
        gate_ref[k:k + 1, :] = exps[k] / tot
        onehot = onehot + sels[k].astype(F32)
    gate_ref[TOP_K:8, :] = jnp.zeros((8 - TOP_K, TM_MOE), F32)

    cnt = jnp.sum(onehot, axis=1, keepdims=True)
    nch = jnp.floor((cnt + (CHUNK_ROWS - 1)) / CHUNK_ROWS)
    ei = lax.broadcasted_iota(jnp.int32, (N_EXPERTS, N_EXPERTS), 0)
    ej = lax.broadcasted_iota(jnp.int32, (N_EXPERTS, N_EXPERTS), 1)
    seg = jnp.dot((ej < ei).astype(F32), jnp.broadcast_to(nch, (N_EXPERTS, 128)), precision=HIGHEST,
                  preferred_element_type=F32)
    nch_ref[...] = jnp.broadcast_to(nch, (N_EXPERTS, 128)).astype(jnp.int32)
    seg_ref[...] = seg.astype(jnp.int32)

    t_src = lax.broadcasted_iota(jnp.int32, (TM_MOE, TM_MOE), 0)
    t_dst = lax.broadcasted_iota(jnp.int32, (TM_MOE, TM_MOE), 1)
    before = (t_src < t_dst).astype(BF16)
    row_of = (seg[:, 0:1] * CHUNK_ROWS
              + jnp.dot(onehot.astype(BF16), before, preferred_element_type=F32))
    q_iota = lax.broadcasted_iota(jnp.int32, (Q_TILE, TM_MOE), 0)
    perm = jnp.zeros((Q_TILE, TM_MOE), F32)
    for k in range(TOP_K):
        q_k = jnp.sum(jnp.where(sels[k], row_of, 0.0), axis=0, keepdims=True).astype(jnp.int32)
        pos_ref[k:k + 1, :] = q_k
        perm = jnp.where(q_iota == q_k, 1.0, perm)
    pos_ref[TOP_K:8, :] = jnp.zeros((8 - TOP_K, TM_MOE), jnp.int32)
    xt_ref[...] = jnp.dot(perm.astype(BF16), hb, preferred_element_type=F32).astype(BF16)


def _route_sort(x, norm_g, mod, w_rt, b_r, layer):
    tspec = pl.BlockSpec((8, TM_MOE), lambda i: (0, i))
    mspec = pl.BlockSpec((None, N_EXPERTS, 128), lambda i: (i, 0, 0))
    meta = jax.ShapeDtypeStruct((N_TILES, N_EXPERTS, 128), jnp.int32)
    return pl.pallas_call(
        _route_sort_kernel,
        grid=(N_TILES,),
        in_specs=[pl.BlockSpec((TM_MOE, D_MODEL), lambda i: (i, 0)),
                  _layer_spec((1, D_MODEL), layer), _mod_spec(TM_MOE, layer),
                  _layer_spec((N_EXPERTS, D_MODEL), layer), _layer_spec((N_EXPERTS, 1), layer)],
        out_specs=[pl.BlockSpec((Q_TILE, D_MODEL), lambda i: (i, 0)), tspec, tspec, mspec, mspec],
        out_shape=[jax.ShapeDtypeStruct((N_TILES * Q_TILE, D_MODEL), BF16),
                   jax.ShapeDtypeStruct((8, T_ALL), jnp.int32), jax.ShapeDtypeStruct((8, T_ALL), F32),
                   meta, meta],
        compiler_params=_cparams(("arbitrary",)),
        name="moe_route_sort",
    )(x, norm_g, mod, w_rt, b_r)


def _expert_kernel(nch_ref, seg_ref, wgu_ref, bgu_ref, wd_ref, bd_ref, xt_ref, yt_ref,
                   wgu_scr, wd_scr, xbuf, ybuf, row_scr, gstart_scr, gsem, ssem):
    del xt_ref
    e = pl.program_id(0)

    def src_row(row):
        return pl.multiple_of(jnp.where(row >= 0, row, READ_SPARE), CHUNK_ROWS)

    def dst_row(row, slot, c):
        spare = c * Q_TILE + jnp.where(slot == 0, Q_TILE - CHUNK_ROWS, Q_TILE - 2 * CHUNK_ROWS)
        return pl.multiple_of(jnp.where(row >= 0, row, spare), CHUNK_ROWS)

    def chunk_rows(c):
        return slice(c * CHUNK_ROWS, (c + 1) * CHUNK_ROWS)

    def start_in(g):
        slot = g % 2
        for c in range(CPG):
            row = row_scr[g * CPG + c]
            pltpu.make_async_copy(yt_ref.at[pl.ds(src_row(row), CHUNK_ROWS), :], xbuf.at[slot, chunk_rows(c), :],
                                  gsem.at[slot]).start()

    def start_out(g, slot):
        for c in range(CPG):
            row = row_scr[g * CPG + c]
            pltpu.make_async_copy(ybuf.at[slot, chunk_rows(c), :],
                                  yt_ref.at[pl.ds(dst_row(row, slot, c), CHUNK_ROWS), :], ssem.at[slot]).start()

    def wait_in(slot):
        pltpu.make_async_copy(yt_ref.at[pl.ds(0, E_GROUP), :], xbuf.at[slot], gsem.at[slot]).wait()

    def wait_out(slot):
        pltpu.make_async_copy(ybuf.at[slot], yt_ref.at[pl.ds(0, E_GROUP), :], ssem.at[slot]).wait()

    @pl.when(e == 0)
    def _():
        def per_expert(ee, cnt):
            gstart_scr[ee] = cnt // CPG

            def per_tile(t, cnt):
                first = (t * CH_PER_TILE + seg_ref[t * N_EXPERTS + ee]) * CHUNK_ROWS

                def per_chunk(j, cnt):
                    row_scr[cnt] = first + j * CHUNK_ROWS
                    return cnt + 1

                return lax.fori_loop(0, nch_ref[t * N_EXPERTS + ee], per_chunk, cnt)

            cnt = lax.fori_loop(0, N_TILES, per_tile, cnt)
            padded = (cnt + CPG - 1) // CPG * CPG

            def pad(i, carry):
                row_scr[i] = -1
                return carry

            lax.fori_loop(cnt, padded, pad, 0)
            return padded

        total = lax.fori_loop(0, N_EXPERTS, per_expert, 0)
        gstart_scr[N_EXPERTS] = total // CPG

        def pad(i, carry):
            row_scr[i] = -1
            return carry

        lax.fori_loop(total, total + CPG, pad, 0)
        ybuf[...] = jnp.zeros_like(ybuf)
        start_out(total // CPG, 0)
        start_out(total // CPG, 1)
        start_in(0)

    g_first = gstart_scr[e]
    g_end = gstart_scr[e + 1]

    @pl.when(g_end > g_first)
    def _():
        wgu_scr[...] = wgu_ref[...].astype(BF16)
        wd_scr[...] = wd_ref[...].astype(BF16)

    def group_step(g, carry):
        slot = g % 2
        start_in(g + 1)
        wait_in(slot)
        wait_out(slot)
        hgu = jnp.dot(xbuf[slot], wgu_scr[...], preferred_element_type=F32) + bgu_ref[...]
        h_glu = jnp.minimum(hgu[:, :D_EXPERT], SWIGLU_LIMIT)
        h_lin = jnp.clip(hgu[:, D_EXPERT:], -SWIGLU_LIMIT, SWIGLU_LIMIT)
        act = (h_lin + 1.0) * (h_glu * _sigmoid(SWIGLU_ALPHA * h_glu))
        y = jnp.dot(act.astype(BF16), wd_scr[...], preferred_element_type=F32) + bd_ref[...]
        ybuf[slot] = y.astype(BF16)
        start_out(g, slot)
        return carry

    lax.fori_loop(g_first, g_end, group_step, 0)

    @pl.when(e == N_EXPERTS - 1)
    def _():
        wait_in(gstart_scr[N_EXPERTS] % 2)
        wait_out(0)
        wait_out(1)


def _experts(nch_flat, seg_flat, xt, w_gu, b_gu, w_down, b_down, layer):
    return pl.pallas_call(
        _expert_kernel,
        grid_spec=pltpu.PrefetchScalarGridSpec(
            num_scalar_prefetch=2,
            grid=(N_EXPERTS,),
            in_specs=[pl.BlockSpec((None, None, D_MODEL, 2 * D_EXPERT), lambda e, n, s: (layer, e, 0, 0)),
                      pl.BlockSpec((None, None, 1, 2 * D_EXPERT), lambda e, n, s: (layer, e, 0, 0)),
                      pl.BlockSpec((None, None, D_EXPERT, D_MODEL), lambda e, n, s: (layer, e, 0, 0)),
                      pl.BlockSpec((None, None, 1, D_MODEL), lambda e, n, s: (layer, e, 0, 0)),
                      pl.BlockSpec(memory_space=pl.ANY)],
            out_specs=pl.BlockSpec(memory_space=pl.ANY),
            scratch_shapes=[pltpu.VMEM((D_MODEL, 2 * D_EXPERT), BF16),
                            pltpu.VMEM((D_EXPERT, D_MODEL), BF16),
                            pltpu.VMEM((2, E_GROUP, D_MODEL), BF16),
                            pltpu.VMEM((2, E_GROUP, D_MODEL), BF16),
                            pltpu.SMEM((MAX_CHUNKS,), jnp.int32),
                            pltpu.SMEM((N_EXPERTS + 1,), jnp.int32),
                            pltpu.SemaphoreType.DMA((2,)),
                            pltpu.SemaphoreType.DMA((2,))]),
        out_shape=jax.ShapeDtypeStruct((N_TILES * Q_TILE, D_MODEL), BF16),
        input_output_aliases={6: 0},
        compiler_params=_cparams(("arbitrary",)),
        name="moe_experts",
    )(nch_flat, seg_flat, w_gu, b_gu, w_down, b_down, xt)


MOE_CTX_TILES = T_CTX // TM_MOE


def _combine_rows(x_ref, mod_ref, pos_ref, gate_ref, yt_ref):
    lane = lax.broadcasted_iota(jnp.int32, (TM_MOE, Q_TILE), 1)
    sel = jnp.zeros((TM_MOE, Q_TILE), F32)
    for k in range(TOP_K):
        sel = jnp.where(lane == pos_ref[:, k:k + 1], gate_ref[:, k:k + 1], sel)
    acc = jnp.dot(sel.astype(BF16), yt_ref[...], preferred_element_type=F32)
    return x_ref[...] + mod_ref[5:6, :] * acc


def _combine_kernel(x_ref, mod_ref, pos_ref, gate_ref, yt_ref, o_ref):
    o_ref[...] = _combine_rows(x_ref, mod_ref, pos_ref, gate_ref, yt_ref)


def _combine_final_kernel(x_ref, mod_ref, pos_ref, gate_ref, yt_ref, fg_ref, ctx_ref, lat_ref):
    y = _combine_rows(x_ref, mod_ref, pos_ref, gate_ref, yt_ref)
    y = y * lax.rsqrt(jnp.mean(y * y, axis=-1, keepdims=True) + RMS_EPS) * fg_ref[...]
    is_ctx = pl.program_id(0) < MOE_CTX_TILES

    @pl.when(is_ctx)
    def _():
        ctx_ref[...] = y

    @pl.when(jnp.logical_not(is_ctx))
    def _():
        lat_ref[...] = y


def _combine(x, mod, pos_t, gate_t, yt, layer, final_g=None):
    tile = pl.BlockSpec((TM_MOE, D_MODEL), lambda i: (i, 0))
    in_specs = [tile, _mod_spec(TM_MOE, layer),
                pl.BlockSpec((TM_MOE, 8), lambda i: (i, 0)),
                pl.BlockSpec((TM_MOE, 8), lambda i: (i, 0)),
                pl.BlockSpec((Q_TILE, D_MODEL), lambda i: (i, 0))]
    if final_g is None:
        return pl.pallas_call(
            _combine_kernel, grid=(N_TILES,), in_specs=in_specs, out_specs=tile,
            out_shape=jax.ShapeDtypeStruct((T_ALL, D_MODEL), F32),
            compiler_params=_cparams(("arbitrary",)), name="moe_combine",
        )(x, mod, pos_t, gate_t, yt)
    return pl.pallas_call(
        _combine_final_kernel, grid=(N_TILES,),
        in_specs=in_specs + [pl.BlockSpec((1, D_MODEL), lambda i: (0, 0))],
        out_specs=[pl.BlockSpec((TM_MOE, D_MODEL), lambda i: (jnp.minimum(i, MOE_CTX_TILES - 1), 0)),
                   pl.BlockSpec((TM_MOE, D_MODEL), lambda i: (jnp.maximum(i - MOE_CTX_TILES, 0), 0))],
        out_shape=[jax.ShapeDtypeStruct((T_CTX, D_MODEL), F32), jax.ShapeDtypeStruct((T_LAT, D_MODEL), F32)],
        compiler_params=_cparams(("arbitrary",)), name="moe_combine_final",
    )(x, mod, pos_t, gate_t, yt, final_g)


def _moe(x, norm_g, mod, w_rt, b_r, w_gu, b_gu, w_down, b_down, layer, final_g=None):
    xt, pos, gate, nch, seg = _route_sort(x, norm_g, mod, w_rt, b_r, layer)
    yt = _experts(nch[:, :, 0].reshape(-1), seg[:, :, 0].reshape(-1), xt, w_gu,
                  b_gu.reshape(DEPTH, N_EXPERTS, 1, 2 * D_EXPERT), w_down,
                  b_down.reshape(DEPTH, N_EXPERTS, 1, D_MODEL), layer)
    return _combine(x, mod, pos.T, gate.T, yt, layer, final_g)


PREP_ROWS = 128


def _prep_in_kernel(w_ref, zb_ref, zg_ref, zf_ref):
    conv0, mq0, om0 = 3 * D_A, 3 * D_A + 2 * D_CONV, 3 * D_A + 2 * D_CONV + 3 * D_M
    gates_end = GATE_OFF + N_GATE_M
    zb_ref[:, :conv0] = w_ref[:, :conv0].astype(BF16)
    zb_ref[:, conv0:] = w_ref[:, mq0:om0].astype(BF16)
    zg_ref[...] = w_ref[:, gates_end:].astype(BF16)
    zf_ref[:, :mq0 - conv0] = w_ref[:, conv0:mq0].astype(BF16)
    zf_ref[:, mq0 - conv0:mq0 - conv0 + D_M] = w_ref[:, om0:GATE_OFF].astype(BF16)
    tail = w_ref[:, GATE_OFF:GATE_OFF + 128]
    keep = lax.broadcasted_iota(jnp.int32, tail.shape, 1) < N_GATE_M
    zf_ref[:, mq0 - conv0 + D_M:] = jnp.where(keep, tail, 0.0).astype(BF16)


def _prep_in_weights(w_in):
    def out(n):
        return (pl.BlockSpec((None, PREP_ROWS, n), lambda l, i: (l, i, 0)),
                jax.ShapeDtypeStruct((DEPTH, D_MODEL, n), BF16))

    specs, shapes = zip(out(N_ZB), out(N_ZG), out(N_ZF))
    return pl.pallas_call(
        _prep_in_kernel,
        grid=(DEPTH, D_MODEL // PREP_ROWS),
        in_specs=[pl.BlockSpec((None, PREP_ROWS, w_in.shape[-1]), lambda l, i: (l, i, 0))],
        out_specs=list(specs),
        out_shape=list(shapes),
        compiler_params=_cparams(("arbitrary", "arbitrary")),
        name="prep_in_weights",
    )(w_in)


def _split_in_cols(w):
    conv0, mq0, om0 = 3 * D_A, 3 * D_A + 2 * D_CONV, 3 * D_A + 2 * D_CONV + 3 * D_M
    gates_end = GATE_OFF + N_GATE_M
    pad = jnp.zeros(w.shape[:-1] + (N_ZF - (2 * D_CONV + D_M + N_GATE_M),), w.dtype)
    zb = jnp.concatenate([w[..., :conv0], w[..., mq0:om0]], axis=-1)
    zg = w[..., gates_end:]
    zf = jnp.concatenate([w[..., conv0:mq0], w[..., om0:GATE_OFF], w[..., GATE_OFF:gates_end], pad], axis=-1)
    return zb, zg, zf


def kernel(x_prompt, x_sample, cache_k, cache_v, state_C, state_n, state_m, c, c_ctx, norm1_g, w_mod, b_mod, w_in, b_in, rpb, w_dw, b_dw, cln_g, cln_b, mnorm_g, w_pa, w_pc, w_pm, w_out, norm2_g, w_router, b_router, w_gu, b_gu, w_down, b_down, final_g):
    cond = jnp.concatenate([c_ctx[None, :], c, jnp.zeros((SEG_PAD - N_SEG, D_MODEL), F32)], axis=0)
    mod_all = _modulation(cond, w_mod, b_mod).reshape(DEPTH, SEG_PAD, 6, D_MODEL)

    x = (x_prompt.reshape(T_CTX, D_MODEL), x_sample.reshape(T_LAT, D_MODEL))
    ck = cache_k.reshape(DEC_BATCH, DEPTH, PAST_LEN, D_A).astype(BF16)
    cv = cache_v.reshape(DEC_BATCH, DEPTH, PAST_LEN, D_A).astype(BF16)
    lat_c0 = state_C.reshape(DEC_BATCH, DEPTH, N_STREAM, HEAD_DIM_M, HEAD_DIM_M)
    lat_n0 = state_n.reshape(DEC_BATCH, DEPTH, N_STREAM, HEAD_DIM_M)
    lat_m0 = jnp.broadcast_to(state_m.reshape(DEC_BATCH, DEPTH, N_STREAM, 1),
                              (DEC_BATCH, DEPTH, N_STREAM, HEAD_DIM_M))
    ctx_c0 = jnp.zeros((1, N_STREAM, HEAD_DIM_M, HEAD_DIM_M), F32)
    ctx_n0 = jnp.zeros((1, N_STREAM, HEAD_DIM_M), F32)
    ctx_m0 = jnp.full((1, N_STREAM, HEAD_DIM_M), -jnp.inf, F32)

    w_in_groups = _prep_in_weights(w_in)
    b_in_groups = _split_in_cols(b_in[:, None, :])
    merge_w = tuple(w.astype(BF16) for w in (w_pa, w_pc, w_pm, w_out))
    toe = _natt_bias(rpb)
    mod = mod_all
    g1, g2, ng = norm1_g[:, None, :], norm2_g[:, None, :], mnorm_g[:, None, :]
    conv_w = (w_dw, b_dw[:, None, :], cln_g[:, None, :], cln_b[:, None, :])
    w_rt, b_r = jnp.swapaxes(w_router, 1, 2), b_router[:, :, None]
    ks, vs, cs, ns, ms = [], [], [], [], []
    for l in range(DEPTH):
        zb, zg, zf, k_l, v_l = _in_proj(x, g1, mod, w_in_groups, b_in_groups, l)
        ya = (_ctx_attention(zb), _natt(zb, ck, cv, toe, l))
        yc = (_conv(zf, *conv_w, SEQ, BATCH, 0, l),
              _conv(zf, *conv_w, DEC_SEQ, DEC_BATCH, T_CTX // DEC_SEQ, l))
        ym_ctx, c_l, n_l, m_l = _mlstm(zb, zf, ctx_c0, ctx_n0, ctx_m0, ng, SEQ, BATCH, 0, lambda b: (0,), l)
        ym_lat, _, _, _ = _mlstm(zb, zf, lat_c0, lat_n0, lat_m0, ng, DEC_SEQ, DEC_BATCH, T_CTX // DEC_SEQ,
                                 lambda b: (b, l), l)
        x = _merge(x, mod, ya, yc, (ym_ctx, ym_lat), zg, *merge_w, l)
        x = _moe(x, g2, mod, w_rt, b_r, w_gu, b_gu, w_down, b_down, l,
                 final_g[None, :] if l == DEPTH - 1 else None)
        ks.append(k_l.reshape(BATCH, SEQ, N_HEADS_A, HEAD_DIM_A))
        vs.append(v_l.reshape(BATCH, SEQ, N_HEADS_A, HEAD_DIM_A))
        cs.append(c_l.reshape(BATCH, 2, N_HEADS_M, HEAD_DIM_M, HEAD_DIM_M))
        ns.append(n_l.reshape(BATCH, 2, N_HEADS_M, HEAD_DIM_M))
        ms.append(m_l[:, :, 0].reshape(BATCH, 2, N_HEADS_M))

    y_ctx, y_lat = x
    return (y_ctx.reshape(BATCH, SEQ, D_MODEL), y_lat.reshape(DEC_BATCH, DEC_SEQ, D_MODEL),
            jnp.stack(ks, axis=1), jnp.stack(vs, axis=1), jnp.stack(cs, axis=1),
            jnp.stack(ns, axis=1), jnp.stack(ms, axis=1))
```
